```python
import jax, jax.numpy as jnp
from jax import lax
import numpy as np

D_MODEL = 2048
BATCH = 8
SEQ = 2048
DEPTH = 2

N_A_LAYERS = DEPTH // 2
N_B_LAYERS = DEPTH - N_A_LAYERS

RWKV_HEAD_DIM = 64
RWKV_HEADS = D_MODEL // RWKV_HEAD_DIM
DECAY_LORA = max(32, int(round(1.8 * D_MODEL ** 0.5 / 32)) * 32)
AAA_LORA = max(32, int(round(1.8 * D_MODEL ** 0.5 / 32)) * 32)
GATE_LORA = max(32, int(round(0.6 * D_MODEL ** 0.8 / 32)) * 32)
N_MIX = 6
GN_EPS = 64e-5
L2_EPS = 1e-12

HEAD_DIM = 64
N_Q_HEADS = D_MODEL // HEAD_DIM
N_KV_HEADS = max(1, N_Q_HEADS // 8)
Q_PER_KV = N_Q_HEADS // N_KV_HEADS
WINDOW = 128
BLOCK = 128
ROPE_THETA = 10000.0

D_FF = 4 * D_MODEL
RMS_EPS = 1e-6

kernel_name = "yoco_rwkv7_swa_sink_hybrid"


def rms_norm(x, gain):
    xf = x.astype(jnp.float32)
    y = xf * lax.rsqrt(jnp.mean(xf * xf, axis=-1, keepdims=True) + RMS_EPS)
    return (y * gain.astype(jnp.float32)).astype(x.dtype)


def rope(x, pos):
    half = x.shape[-1] // 2
    inv_freq = jnp.power(ROPE_THETA, -jnp.arange(half, dtype=jnp.float32) / half)
    ang = pos.astype(jnp.float32)[:, None] * inv_freq[None, :]
    cos = jnp.cos(ang)[None, :, None, :]
    sin = jnp.sin(ang)[None, :, None, :]
    xf = x.astype(jnp.float32)
    x1, x2 = xf[..., :half], xf[..., half:]
    return jnp.concatenate([x1 * cos - x2 * sin, x2 * cos + x1 * sin], axis=-1).astype(x.dtype)


def squared_relu_mlp(xn, w_up, w_down):
    return jnp.square(jax.nn.relu(xn @ w_up)) @ w_down


def wkv7_scan(r, decay, k, v, a, b):
    bsz, _, h, n = r.shape

    def step(S, inp):
        r_t, w_t, k_t, v_t, a_t, b_t = inp
        sa = jnp.einsum('bhvk,bhk->bhv', S, a_t)
        S = S * w_t[:, :, None, :] + sa[..., None] * b_t[:, :, None, :] + v_t[..., None] * k_t[:, :, None, :]
        y_t = jnp.einsum('bhvk,bhk->bhv', S, r_t)
        return S, y_t

    xs = (jnp.moveaxis(r, 1, 0), jnp.moveaxis(decay, 1, 0), jnp.moveaxis(k, 1, 0),
          jnp.moveaxis(v, 1, 0), jnp.moveaxis(a, 1, 0), jnp.moveaxis(b, 1, 0))
    S0 = jnp.zeros((bsz, h, n, n), jnp.float32)
    _, y = lax.scan(step, S0, xs)
    return jnp.moveaxis(y, 0, 1)


def rwkv7_time_mix(xn, mix, w_rkv, w0, w1, w2, a0, a1, a2, g1, g2, k_k, k_a, r_k, ln_w, ln_b, w_out):
    bsz, t, c = xn.shape
    h, n = RWKV_HEADS, RWKV_HEAD_DIM
    f32 = jnp.float32
    xx = jnp.pad(xn[:, :-1], ((0, 0), (1, 0), (0, 0))) - xn
    lerp = lambda i: xn + xx * mix[i]
    r = lerp(0) @ w_rkv[0]
    k = lerp(1) @ w_rkv[1]
    v = lerp(2) @ w_rkv[2]
    w_log = -jax.nn.softplus(-(w0 + jnp.tanh(lerp(3) @ w1) @ w2).astype(f32)) - 0.5
    decay = jnp.exp(-jnp.exp(w_log))
    a = jax.nn.sigmoid((a0 + (lerp(4) @ a1) @ a2).astype(f32))
    g = jax.nn.sigmoid(lerp(5) @ g1) @ g2
    kk = (k * k_k).astype(f32).reshape(bsz, t, h, n)
    kk = kk / jnp.maximum(jnp.sqrt(jnp.sum(kk * kk, axis=-1, keepdims=True)), L2_EPS)
    k = k.astype(f32) * (1.0 + (a - 1.0) * k_a.astype(f32))
    rh = r.astype(f32).reshape(bsz, t, h, n)
    kh = k.reshape(bsz, t, h, n)
    vh = v.astype(f32).reshape(bsz, t, h, n)
    ah = a.reshape(bsz, t, h, n)
    y = wkv7_scan(rh, decay.reshape(bsz, t, h, n), kh, vh, -kk, kk * ah)
    mu = jnp.mean(y, axis=-1, keepdims=True)
    var = jnp.mean(jnp.square(y - mu), axis=-1, keepdims=True)
    y = (y - mu) * lax.rsqrt(var + GN_EPS)
    y = y * ln_w.astype(f32).reshape(h, n) + ln_b.astype(f32).reshape(h, n)
    y = y + jnp.sum(rh * kh * r_k.astype(f32), axis=-1, keepdims=True) * vh
    y = y.reshape(bsz, t, c).astype(xn.dtype)
    return (y * g) @ w_out


def banded_sink_attention(q, k, v, sinks):
    bsz, t, _, d = q.shape
    nb = t // BLOCK
    qb = q.reshape(bsz, nb, BLOCK, N_KV_HEADS, Q_PER_KV, d)
    kb = k.reshape(bsz, nb, BLOCK, N_KV_HEADS, d)
    vb = v.reshape(bsz, nb, BLOCK, N_KV_HEADS, d)
    padw = ((0, 0), (1, 0), (0, 0), (0, 0), (0, 0))
    k_band = jnp.concatenate([jnp.pad(kb[:, :-1], padw), kb], axis=2)
    v_band = jnp.concatenate([jnp.pad(vb[:, :-1], padw), vb], axis=2)
    s = jnp.einsum('bnqkgd,bnskd->bnkgqs', qb, k_band).astype(jnp.float32) * (d ** -0.5)
    qpos = jnp.arange(BLOCK)[:, None] + BLOCK
    kpos = jnp.arange(2 * BLOCK)[None, :]
    diff = qpos - kpos
    in_window = (diff >= 0) & (diff < WINDOW)
    not_pad = (jnp.arange(nb)[:, None, None] > 0) | (kpos[None] >= BLOCK)
    valid = in_window[None] & not_pad
    s = jnp.where(valid[None, :, None, None], s, -jnp.inf)
    sink = jnp.broadcast_to(
        sinks.astype(jnp.float32).reshape(N_KV_HEADS, Q_PER_KV)[None, None, :, :, None, None],
        s.shape[:-1] + (1,))
    p = jax.nn.softmax(jnp.concatenate([s, sink], axis=-1), axis=-1)[..., :-1]
    o = jnp.einsum('bnkgqs,bnskd->bnqkgd', p.astype(v.dtype), v_band)
    return o.reshape(bsz, t, N_Q_HEADS, d)


def swa_cross_layer(xn, k_shared, v_shared, pos, w_q, q_norm, sinks, w_o):
    bsz, t, _ = xn.shape
    q = (xn @ w_q).reshape(bsz, t, N_Q_HEADS, HEAD_DIM)
    q = rope(rms_norm(q, q_norm), pos)
    o = banded_sink_attention(q, k_shared, v_shared, sinks)
    return o.reshape(bsz, t, N_Q_HEADS * HEAD_DIM) @ w_o


def _fwd_setup_inputs(seed: int = 0) -> dict:
    key = jax.random.key(seed)
    ks = list(jax.random.split(key, 40))
    f32 = jnp.float32
    nxt = lambda: ks.pop()
    nrm = lambda shape, scale: jax.random.normal(nxt(), shape, f32) * scale
    D, nA, nB = D_MODEL, N_A_LAYERS, N_B_LAYERS
    H, N = RWKV_HEADS, RWKV_HEAD_DIM
    return {
        "x": nrm((BATCH, SEQ, D), 1.0),
        "a_norm": 1.0 + nrm((nA, D), 0.02),
        "a_mix": jax.random.uniform(nxt(), (nA, N_MIX, D), f32),
        "a_w_rkv": nrm((nA, 3, D, D), D ** -0.5),
        "a_w0": jax.random.uniform(nxt(), (nA, D), f32, -6.0, 0.0),
        "a_w1": nrm((nA, D, DECAY_LORA), D ** -0.5),
        "a_w2": nrm((nA, DECAY_LORA, D), 0.1 * DECAY_LORA ** -0.5),
        "a_a0": nrm((nA, D), 0.5),
        "a_a1": nrm((nA, D, AAA_LORA), D ** -0.5),
        "a_a2": nrm((nA, AAA_LORA, D), 0.1 * AAA_LORA ** -0.5),
        "a_g1": nrm((nA, D, GATE_LORA), D ** -0.5),
        "a_g2": nrm((nA, GATE_LORA, D), GATE_LORA ** -0.5),
        "a_k_k": 0.85 + nrm((nA, D), 0.05),
        "a_k_a": 1.0 + nrm((nA, D), 0.05),
        "a_r_k": nrm((nA, H, N), 0.1),
        "a_ln_x_w": 1.0 + nrm((nA, D), 0.02),
        "a_ln_x_b": nrm((nA, D), 0.02),
        "a_w_out": nrm((nA, D, D), D ** -0.5),
        "mlp_norm": 1.0 + nrm((DEPTH, D), 0.02),
        "mlp_w_up": nrm((DEPTH, D, D_FF), D ** -0.5),
        "mlp_w_down": nrm((DEPTH, D_FF, D), D_FF ** -0.5),
        "kv_norm": 1.0 + nrm((D,), 0.02),
        "w_kv": nrm((D, 2 * N_KV_HEADS * HEAD_DIM), D ** -0.5),
        "k_norm": 1.0 + nrm((HEAD_DIM,), 0.02),
        "b_norm": 1.0 + nrm((nB, D), 0.02),
        "b_w_q": nrm((nB, D, N_Q_HEADS * HEAD_DIM), D ** -0.5),
        "b_q_norm": 1.0 + nrm((nB, HEAD_DIM), 0.02),
        "b_sinks": nrm((nB, N_Q_HEADS), 0.5),
        "b_w_o": nrm((nB, N_Q_HEADS * HEAD_DIM, D), (N_Q_HEADS * HEAD_DIM) ** -0.5),
    }


def _fwd_reference(x, a_norm, a_mix, a_w_rkv, a_w0, a_w1, a_w2, a_a0, a_a1, a_a2, a_g1, a_g2,
              a_k_k, a_k_a, a_r_k, a_ln_x_w, a_ln_x_b, a_w_out,
              mlp_norm, mlp_w_up, mlp_w_down,
              kv_norm, w_kv, k_norm,
              b_norm, b_w_q, b_q_norm, b_sinks, b_w_o):
    bsz, t, _ = x.shape
    pos = jnp.arange(t, dtype=jnp.int32)
    h = x
    k_shared = None
    v_shared = None
    for layer in range(DEPTH):
        if layer < N_A_LAYERS:
            i = layer
            h = h + rwkv7_time_mix(rms_norm(h, a_norm[i]), a_mix[i], a_w_rkv[i], a_w0[i], a_w1[i], a_w2[i],
                                   a_a0[i], a_a1[i], a_a2[i], a_g1[i], a_g2[i], a_k_k[i], a_k_a[i],
                                   a_r_k[i], a_ln_x_w[i], a_ln_x_b[i], a_w_out[i])
        else:
            if layer == N_A_LAYERS:
                kv = rms_norm(h, kv_norm) @ w_kv
                k_s, v_s = jnp.split(kv, 2, axis=-1)
                k_shared = rope(rms_norm(k_s.reshape(bsz, t, N_KV_HEADS, HEAD_DIM), k_norm), pos)
                v_shared = v_s.reshape(bsz, t, N_KV_HEADS, HEAD_DIM)
            j = layer - N_A_LAYERS
            h = h + swa_cross_layer(rms_norm(h, b_norm[j]), k_shared, v_shared, pos,
                                    b_w_q[j], b_q_norm[j], b_sinks[j], b_w_o[j])
        h = h + squared_relu_mlp(rms_norm(h, mlp_norm[layer]), mlp_w_up[layer], mlp_w_down[layer])
    return h


import jax as _jax
import jax.numpy as _jnp

TWIN_FORMAT = 'train_step'
FWD_PARAMS = ['x', 'a_norm', 'a_mix', 'a_w_rkv', 'a_w0', 'a_w1', 'a_w2', 'a_a0', 'a_a1', 'a_a2', 'a_g1', 'a_g2', 'a_k_k', 'a_k_a', 'a_r_k', 'a_ln_x_w', 'a_ln_x_b', 'a_w_out', 'mlp_norm', 'mlp_w_up', 'mlp_w_down', 'kv_norm', 'w_kv', 'k_norm', 'b_norm', 'b_w_q', 'b_q_norm', 'b_sinks', 'b_w_o']
TWIN_WEIGHTS = ['a_norm', 'a_mix', 'a_w_rkv', 'a_w0', 'a_w1', 'a_w2', 'a_a0', 'a_a1', 'a_a2', 'a_g1', 'a_g2', 'a_k_k', 'a_k_a', 'a_r_k', 'a_ln_x_w', 'a_ln_x_b', 'a_w_out', 'mlp_norm', 'mlp_w_up', 'mlp_w_down', 'kv_norm', 'w_kv', 'k_norm', 'b_norm', 'b_w_q', 'b_q_norm', 'b_sinks', 'b_w_o']
TWIN_DIFF_INPUT = 'x'
TWIN_INPUTS = ['x', 'a_norm', 'a_mix', 'a_w_rkv', 'a_w0', 'a_w1', 'a_w2', 'a_a0', 'a_a1', 'a_a2', 'a_g1', 'a_g2', 'a_k_k', 'a_k_a', 'a_r_k', 'a_ln_x_w', 'a_ln_x_b', 'a_w_out', 'mlp_norm', 'mlp_w_up', 'mlp_w_down', 'kv_norm', 'w_kv', 'k_norm', 'b_norm', 'b_w_q', 'b_q_norm', 'b_sinks', 'b_w_o', 'loss_target', 'm_a_norm', 'm_a_mix', 'm_a_w_rkv', 'm_a_w0', 'm_a_w1', 'm_a_w2', 'm_a_a0', 'm_a_a1', 'm_a_a2', 'm_a_g1', 'm_a_g2', 'm_a_k_k', 'm_a_k_a', 'm_a_r_k', 'm_a_ln_x_w', 'm_a_ln_x_b', 'm_a_w_out', 'm_mlp_norm', 'm_mlp_w_up', 'm_mlp_w_down', 'm_kv_norm', 'm_w_kv', 'm_k_norm', 'm_b_norm', 'm_b_w_q', 'm_b_q_norm', 'm_b_sinks', 'm_b_w_o', 'v_a_norm', 'v_a_mix', 'v_a_w_rkv', 'v_a_w0', 'v_a_w1', 'v_a_w2', 'v_a_a0', 'v_a_a1', 'v_a_a2', 'v_a_g1', 'v_a_g2', 'v_a_k_k', 'v_a_k_a', 'v_a_r_k', 'v_a_ln_x_w', 'v_a_ln_x_b', 'v_a_w_out', 'v_mlp_norm', 'v_mlp_w_up', 'v_mlp_w_down', 'v_kv_norm', 'v_w_kv', 'v_k_norm', 'v_b_norm', 'v_b_w_q', 'v_b_q_norm', 'v_b_sinks', 'v_b_w_o']
TWIN_OUTPUTS = ['loss', 'grad_x', 'grad_a_norm', 'grad_a_mix', 'grad_a_w_rkv', 'grad_a_w0', 'grad_a_w1', 'grad_a_w2', 'grad_a_a0', 'grad_a_a1', 'grad_a_a2', 'grad_a_g1', 'grad_a_g2', 'grad_a_k_k', 'grad_a_k_a', 'grad_a_r_k', 'grad_a_ln_x_w', 'grad_a_ln_x_b', 'grad_a_w_out', 'grad_mlp_norm', 'grad_mlp_w_up', 'grad_mlp_w_down', 'grad_kv_norm', 'grad_w_kv', 'grad_k_norm', 'grad_b_norm', 'grad_b_w_q', 'grad_b_q_norm', 'grad_b_sinks', 'grad_b_w_o', 'delta_a_norm', 'delta_a_mix', 'delta_a_w_rkv', 'delta_a_w0', 'delta_a_w1', 'delta_a_w2', 'delta_a_a0', 'delta_a_a1', 'delta_a_a2', 'delta_a_g1', 'delta_a_g2', 'delta_a_k_k', 'delta_a_k_a', 'delta_a_r_k', 'delta_a_ln_x_w', 'delta_a_ln_x_b', 'delta_a_w_out', 'delta_mlp_norm', 'delta_mlp_w_up', 'delta_mlp_w_down', 'delta_kv_norm', 'delta_w_kv', 'delta_k_norm', 'delta_b_norm', 'delta_b_w_q', 'delta_b_q_norm', 'delta_b_sinks', 'delta_b_w_o', 'new_m_a_norm', 'new_m_a_mix', 'new_m_a_w_rkv', 'new_m_a_w0', 'new_m_a_w1', 'new_m_a_w2', 'new_m_a_a0', 'new_m_a_a1', 'new_m_a_a2', 'new_m_a_g1', 'new_m_a_g2', 'new_m_a_k_k', 'new_m_a_k_a', 'new_m_a_r_k', 'new_m_a_ln_x_w', 'new_m_a_ln_x_b', 'new_m_a_w_out', 'new_m_mlp_norm', 'new_m_mlp_w_up', 'new_m_mlp_w_down', 'new_m_kv_norm', 'new_m_w_kv', 'new_m_k_norm', 'new_m_b_norm', 'new_m_b_w_q', 'new_m_b_q_norm', 'new_m_b_sinks', 'new_m_b_w_o', 'new_v_a_norm', 'new_v_a_mix', 'new_v_a_w_rkv', 'new_v_a_w0', 'new_v_a_w1', 'new_v_a_w2', 'new_v_a_a0', 'new_v_a_a1', 'new_v_a_a2', 'new_v_a_g1', 'new_v_a_g2', 'new_v_a_k_k', 'new_v_a_k_a', 'new_v_a_r_k', 'new_v_a_ln_x_w', 'new_v_a_ln_x_b', 'new_v_a_w_out', 'new_v_mlp_norm', 'new_v_mlp_w_up', 'new_v_mlp_w_down', 'new_v_kv_norm', 'new_v_w_kv', 'new_v_k_norm', 'new_v_b_norm', 'new_v_b_w_q', 'new_v_b_q_norm', 'new_v_b_sinks', 'new_v_b_w_o']
TWIN_LEAF_KINDS = {'loss': 'loss', 'grad_x': 'grad_x', 'grad_a_norm': 'grad_w', 'grad_a_mix': 'grad_w', 'grad_a_w_rkv': 'grad_w', 'grad_a_w0': 'grad_w', 'grad_a_w1': 'grad_w', 'grad_a_w2': 'grad_w', 'grad_a_a0': 'grad_w', 'grad_a_a1': 'grad_w', 'grad_a_a2': 'grad_w', 'grad_a_g1': 'grad_w', 'grad_a_g2': 'grad_w', 'grad_a_k_k': 'grad_w', 'grad_a_k_a': 'grad_w', 'grad_a_r_k': 'grad_w', 'grad_a_ln_x_w': 'grad_w', 'grad_a_ln_x_b': 'grad_w', 'grad_a_w_out': 'grad_w', 'grad_mlp_norm': 'grad_w', 'grad_mlp_w_up': 'grad_w', 'grad_mlp_w_down': 'grad_w', 'grad_kv_norm': 'grad_w', 'grad_w_kv': 'grad_w', 'grad_k_norm': 'grad_w', 'grad_b_norm': 'grad_w', 'grad_b_w_q': 'grad_w', 'grad_b_q_norm': 'grad_w', 'grad_b_sinks': 'grad_w', 'grad_b_w_o': 'grad_w', 'delta_a_norm': 'delta_w', 'delta_a_mix': 'delta_w', 'delta_a_w_rkv': 'delta_w', 'delta_a_w0': 'delta_w', 'delta_a_w1': 'delta_w', 'delta_a_w2': 'delta_w', 'delta_a_a0': 'delta_w', 'delta_a_a1': 'delta_w', 'delta_a_a2': 'delta_w', 'delta_a_g1': 'delta_w', 'delta_a_g2': 'delta_w', 'delta_a_k_k': 'delta_w', 'delta_a_k_a': 'delta_w', 'delta_a_r_k': 'delta_w', 'delta_a_ln_x_w': 'delta_w', 'delta_a_ln_x_b': 'delta_w', 'delta_a_w_out': 'delta_w', 'delta_mlp_norm': 'delta_w', 'delta_mlp_w_up': 'delta_w', 'delta_mlp_w_down': 'delta_w', 'delta_kv_norm': 'delta_w', 'delta_w_kv': 'delta_w', 'delta_k_norm': 'delta_w', 'delta_b_norm': 'delta_w', 'delta_b_w_q': 'delta_w', 'delta_b_q_norm': 'delta_w', 'delta_b_sinks': 'delta_w', 'delta_b_w_o': 'delta_w', 'new_m_a_norm': 'new_m', 'new_m_a_mix': 'new_m', 'new_m_a_w_rkv': 'new_m', 'new_m_a_w0': 'new_m', 'new_m_a_w1': 'new_m', 'new_m_a_w2': 'new_m', 'new_m_a_a0': 'new_m', 'new_m_a_a1': 'new_m', 'new_m_a_a2': 'new_m', 'new_m_a_g1': 'new_m', 'new_m_a_g2': 'new_m', 'new_m_a_k_k': 'new_m', 'new_m_a_k_a': 'new_m', 'new_m_a_r_k': 'new_m', 'new_m_a_ln_x_w': 'new_m', 'new_m_a_ln_x_b': 'new_m', 'new_m_a_w_out': 'new_m', 'new_m_mlp_norm': 'new_m', 'new_m_mlp_w_up': 'new_m', 'new_m_mlp_w_down': 'new_m', 'new_m_kv_norm': 'new_m', 'new_m_w_kv': 'new_m', 'new_m_k_norm': 'new_m', 'new_m_b_norm': 'new_m', 'new_m_b_w_q': 'new_m', 'new_m_b_q_norm': 'new_m', 'new_m_b_sinks': 'new_m', 'new_m_b_w_o': 'new_m', 'new_v_a_norm': 'new_v', 'new_v_a_mix': 'new_v', 'new_v_a_w_rkv': 'new_v', 'new_v_a_w0': 'new_v', 'new_v_a_w1': 'new_v', 'new_v_a_w2': 'new_v', 'new_v_a_a0': 'new_v', 'new_v_a_a1': 'new_v', 'new_v_a_a2': 'new_v', 'new_v_a_g1': 'new_v', 'new_v_a_g2': 'new_v', 'new_v_a_k_k': 'new_v', 'new_v_a_k_a': 'new_v', 'new_v_a_r_k': 'new_v', 'new_v_a_ln_x_w': 'new_v', 'new_v_a_ln_x_b': 'new_v', 'new_v_a_w_out': 'new_v', 'new_v_mlp_norm': 'new_v', 'new_v_mlp_w_up': 'new_v', 'new_v_mlp_w_down': 'new_v', 'new_v_kv_norm': 'new_v', 'new_v_w_kv': 'new_v', 'new_v_k_norm': 'new_v', 'new_v_b_norm': 'new_v', 'new_v_b_w_q': 'new_v', 'new_v_b_q_norm': 'new_v', 'new_v_b_sinks': 'new_v', 'new_v_b_w_o': 'new_v'}


def _forward(args):
    return _fwd_reference(*[args[k] for k in FWD_PARAMS])


def _output_shape():
    out = _jax.eval_shape(lambda: _forward(_fwd_setup_inputs(0)))
    return out.shape, out.dtype

N_MICROBATCH = 1
ADAM_LR = 0.001
ADAM_B1 = 0.9
ADAM_B2 = 0.999
ADAM_EPS = 1e-08
ADAM_WD = 0.01
ADAM_STEP = 10
PER_EXAMPLE_BATCH_AXIS = {'x': 0, 'loss_target': 0}
SHARED_INPUTS = []
_WEIGHT_DTYPES = {'a_norm': _jnp.float32, 'a_mix': _jnp.float32, 'a_w_rkv': _jnp.float32, 'a_w0': _jnp.float32, 'a_w1': _jnp.float32, 'a_w2': _jnp.float32, 'a_a0': _jnp.float32, 'a_a1': _jnp.float32, 'a_a2': _jnp.float32, 'a_g1': _jnp.float32, 'a_g2': _jnp.float32, 'a_k_k': _jnp.float32, 'a_k_a': _jnp.float32, 'a_r_k': _jnp.float32, 'a_ln_x_w': _jnp.float32, 'a_ln_x_b': _jnp.float32, 'a_w_out': _jnp.float32, 'mlp_norm': _jnp.float32, 'mlp_w_up': _jnp.float32, 'mlp_w_down': _jnp.float32, 'kv_norm': _jnp.float32, 'w_kv': _jnp.float32, 'k_norm': _jnp.float32, 'b_norm': _jnp.float32, 'b_w_q': _jnp.float32, 'b_q_norm': _jnp.float32, 'b_sinks': _jnp.float32, 'b_w_o': _jnp.float32}
MOMENT_SCALE = {'a_norm': 5.433923e-01, 'a_mix': 2.833546e-01, 'a_w_rkv': 2.427177e-01, 'a_w0': 8.280787e-02, 'a_w1': 5.252700e-03, 'a_w2': 1.083641e-02, 'a_a0': 1.211320e-01, 'a_a1': 4.232546e-02, 'a_a2': 9.184096e-02, 'a_g1': 2.169582e-01, 'a_g2': 1.971813e+00, 'a_k_k': 2.865495e-01, 'a_k_a': 3.172362e-01, 'a_r_k': 1.422256e+00, 'a_ln_x_w': 3.267598e+00, 'a_ln_x_b': 6.928980e+00, 'a_w_out': 2.764850e-01, 'mlp_norm': 2.418139e+01, 'mlp_w_up': 1.044651e+00, 'mlp_w_down': 4.615114e+00, 'kv_norm': 3.402647e+00, 'w_kv': 6.198332e+00, 'k_norm': 3.012776e+00, 'b_norm': 3.981445e-02, 'b_w_q': 4.078769e-02, 'b_q_norm': 3.007768e+00, 'b_sinks': 4.152898e-01, 'b_w_o': 2.474307e+00}


def _to_microbatches(a, axis):
    t = _jnp.moveaxis(a, axis, 0)
    t = t.reshape((N_MICROBATCH, t.shape[0] // N_MICROBATCH) + t.shape[1:])
    return _jnp.moveaxis(t, 1, axis + 1)


def setup_inputs(seed: int = 0) -> dict:
    inp = _fwd_setup_inputs(seed)
    key = _jax.random.fold_in(_jax.random.key(seed), 7919)
    shape, _ = _output_shape()
    out = dict(inp)
    out["loss_target"] = _jax.random.normal(_jax.random.fold_in(key, 0), shape, _jnp.float32)
    for i, name in enumerate(TWIN_WEIGHTS):
        w = inp[name].astype(_jnp.float32)
        if MOMENT_SCALE is None:
            s = _jnp.sqrt(_jnp.mean(_jnp.square(w)) + 1e-30)
        else:
            s = MOMENT_SCALE[name]
        km, kv = _jax.random.split(_jax.random.fold_in(key, i + 1))
        out[name] = w
        out["m_" + name] = s * _jax.random.normal(km, w.shape, _jnp.float32)
        out["v_" + name] = (s * s) * _jax.random.uniform(kv, w.shape, _jnp.float32, 0.5, 1.5)
    if N_MICROBATCH > 1:
        for name, axis in PER_EXAMPLE_BATCH_AXIS.items():
            out[name] = _to_microbatches(out[name], axis)
    return {'x': out['x'], 'a_norm': out['a_norm'], 'a_mix': out['a_mix'], 'a_w_rkv': out['a_w_rkv'], 'a_w0': out['a_w0'], 'a_w1': out['a_w1'], 'a_w2': out['a_w2'], 'a_a0': out['a_a0'], 'a_a1': out['a_a1'], 'a_a2': out['a_a2'], 'a_g1': out['a_g1'], 'a_g2': out['a_g2'], 'a_k_k': out['a_k_k'], 'a_k_a': out['a_k_a'], 'a_r_k': out['a_r_k'], 'a_ln_x_w': out['a_ln_x_w'], 'a_ln_x_b': out['a_ln_x_b'], 'a_w_out': out['a_w_out'], 'mlp_norm': out['mlp_norm'], 'mlp_w_up': out['mlp_w_up'], 'mlp_w_down': out['mlp_w_down'], 'kv_norm': out['kv_norm'], 'w_kv': out['w_kv'], 'k_norm': out['k_norm'], 'b_norm': out['b_norm'], 'b_w_q': out['b_w_q'], 'b_q_norm': out['b_q_norm'], 'b_sinks': out['b_sinks'], 'b_w_o': out['b_w_o'], 'loss_target': out['loss_target'], 'm_a_norm': out['m_a_norm'], 'm_a_mix': out['m_a_mix'], 'm_a_w_rkv': out['m_a_w_rkv'], 'm_a_w0': out['m_a_w0'], 'm_a_w1': out['m_a_w1'], 'm_a_w2': out['m_a_w2'], 'm_a_a0': out['m_a_a0'], 'm_a_a1': out['m_a_a1'], 'm_a_a2': out['m_a_a2'], 'm_a_g1': out['m_a_g1'], 'm_a_g2': out['m_a_g2'], 'm_a_k_k': out['m_a_k_k'], 'm_a_k_a': out['m_a_k_a'], 'm_a_r_k': out['m_a_r_k'], 'm_a_ln_x_w': out['m_a_ln_x_w'], 'm_a_ln_x_b': out['m_a_ln_x_b'], 'm_a_w_out': out['m_a_w_out'], 'm_mlp_norm': out['m_mlp_norm'], 'm_mlp_w_up': out['m_mlp_w_up'], 'm_mlp_w_down': out['m_mlp_w_down'], 'm_kv_norm': out['m_kv_norm'], 'm_w_kv': out['m_w_kv'], 'm_k_norm': out['m_k_norm'], 'm_b_norm': out['m_b_norm'], 'm_b_w_q': out['m_b_w_q'], 'm_b_q_norm': out['m_b_q_norm'], 'm_b_sinks': out['m_b_sinks'], 'm_b_w_o': out['m_b_w_o'], 'v_a_norm': out['v_a_norm'], 'v_a_mix': out['v_a_mix'], 'v_a_w_rkv': out['v_a_w_rkv'], 'v_a_w0': out['v_a_w0'], 'v_a_w1': out['v_a_w1'], 'v_a_w2': out['v_a_w2'], 'v_a_a0': out['v_a_a0'], 'v_a_a1': out['v_a_a1'], 'v_a_a2': out['v_a_a2'], 'v_a_g1': out['v_a_g1'], 'v_a_g2': out['v_a_g2'], 'v_a_k_k': out['v_a_k_k'], 'v_a_k_a': out['v_a_k_a'], 'v_a_r_k': out['v_a_r_k'], 'v_a_ln_x_w': out['v_a_ln_x_w'], 'v_a_ln_x_b': out['v_a_ln_x_b'], 'v_a_w_out': out['v_a_w_out'], 'v_mlp_norm': out['v_mlp_norm'], 'v_mlp_w_up': out['v_mlp_w_up'], 'v_mlp_w_down': out['v_mlp_w_down'], 'v_kv_norm': out['v_kv_norm'], 'v_w_kv': out['v_w_kv'], 'v_k_norm': out['v_k_norm'], 'v_b_norm': out['v_b_norm'], 'v_b_w_q': out['v_b_w_q'], 'v_b_q_norm': out['v_b_q_norm'], 'v_b_sinks': out['v_b_sinks'], 'v_b_w_o': out['v_b_w_o']}


def _loss(weights, diff, rest, loss_target):
    with _jax.named_scope("forward"):
        args = {**rest, TWIN_DIFF_INPUT: diff, **{k: w.astype(_WEIGHT_DTYPES[k]) for k, w in weights.items()}}
        y = _forward(args)
    with _jax.named_scope("loss_head"):
        err = _jnp.square(y.astype(_jnp.float32) - loss_target)
        return 0.5 * _jnp.sum(_jnp.mean(err, axis=-1)) if err.ndim else 0.5 * err


def _adamw(w, g, m, v):
    m = ADAM_B1 * m + (1.0 - ADAM_B1) * g
    v = ADAM_B2 * v + (1.0 - ADAM_B2) * _jnp.square(g)
    m_hat = m / (1.0 - ADAM_B1 ** ADAM_STEP)
    v_hat = v / (1.0 - ADAM_B2 ** ADAM_STEP)
    delta = -ADAM_LR * (m_hat / (_jnp.sqrt(v_hat) + ADAM_EPS) + ADAM_WD * w)
    return delta, m, v


def reference(x, a_norm, a_mix, a_w_rkv, a_w0, a_w1, a_w2, a_a0, a_a1, a_a2, a_g1, a_g2, a_k_k, a_k_a, a_r_k, a_ln_x_w, a_ln_x_b, a_w_out, mlp_norm, mlp_w_up, mlp_w_down, kv_norm, w_kv, k_norm, b_norm, b_w_q, b_q_norm, b_sinks, b_w_o, loss_target, m_a_norm, m_a_mix, m_a_w_rkv, m_a_w0, m_a_w1, m_a_w2, m_a_a0, m_a_a1, m_a_a2, m_a_g1, m_a_g2, m_a_k_k, m_a_k_a, m_a_r_k, m_a_ln_x_w, m_a_ln_x_b, m_a_w_out, m_mlp_norm, m_mlp_w_up, m_mlp_w_down, m_kv_norm, m_w_kv, m_k_norm, m_b_norm, m_b_w_q, m_b_q_norm, m_b_sinks, m_b_w_o, v_a_norm, v_a_mix, v_a_w_rkv, v_a_w0, v_a_w1, v_a_w2, v_a_a0, v_a_a1, v_a_a2, v_a_g1, v_a_g2, v_a_k_k, v_a_k_a, v_a_r_k, v_a_ln_x_w, v_a_ln_x_b, v_a_w_out, v_mlp_norm, v_mlp_w_up, v_mlp_w_down, v_kv_norm, v_w_kv, v_k_norm, v_b_norm, v_b_w_q, v_b_q_norm, v_b_sinks, v_b_w_o):
    given = dict(x=x, a_norm=a_norm, a_mix=a_mix, a_w_rkv=a_w_rkv, a_w0=a_w0, a_w1=a_w1, a_w2=a_w2, a_a0=a_a0, a_a1=a_a1, a_a2=a_a2, a_g1=a_g1, a_g2=a_g2, a_k_k=a_k_k, a_k_a=a_k_a, a_r_k=a_r_k, a_ln_x_w=a_ln_x_w, a_ln_x_b=a_ln_x_b, a_w_out=a_w_out, mlp_norm=mlp_norm, mlp_w_up=mlp_w_up, mlp_w_down=mlp_w_down, kv_norm=kv_norm, w_kv=w_kv, k_norm=k_norm, b_norm=b_norm, b_w_q=b_w_q, b_q_norm=b_q_norm, b_sinks=b_sinks, b_w_o=b_w_o, loss_target=loss_target, m_a_norm=m_a_norm, m_a_mix=m_a_mix, m_a_w_rkv=m_a_w_rkv, m_a_w0=m_a_w0, m_a_w1=m_a_w1, m_a_w2=m_a_w2, m_a_a0=m_a_a0, m_a_a1=m_a_a1, m_a_a2=m_a_a2, m_a_g1=m_a_g1, m_a_g2=m_a_g2, m_a_k_k=m_a_k_k, m_a_k_a=m_a_k_a, m_a_r_k=m_a_r_k, m_a_ln_x_w=m_a_ln_x_w, m_a_ln_x_b=m_a_ln_x_b, m_a_w_out=m_a_w_out, m_mlp_norm=m_mlp_norm, m_mlp_w_up=m_mlp_w_up, m_mlp_w_down=m_mlp_w_down, m_kv_norm=m_kv_norm, m_w_kv=m_w_kv, m_k_norm=m_k_norm, m_b_norm=m_b_norm, m_b_w_q=m_b_w_q, m_b_q_norm=m_b_q_norm, m_b_sinks=m_b_sinks, m_b_w_o=m_b_w_o, v_a_norm=v_a_norm, v_a_mix=v_a_mix, v_a_w_rkv=v_a_w_rkv, v_a_w0=v_a_w0, v_a_w1=v_a_w1, v_a_w2=v_a_w2, v_a_a0=v_a_a0, v_a_a1=v_a_a1, v_a_a2=v_a_a2, v_a_g1=v_a_g1, v_a_g2=v_a_g2, v_a_k_k=v_a_k_k, v_a_k_a=v_a_k_a, v_a_r_k=v_a_r_k, v_a_ln_x_w=v_a_ln_x_w, v_a_ln_x_b=v_a_ln_x_b, v_a_w_out=v_a_w_out, v_mlp_norm=v_mlp_norm, v_mlp_w_up=v_mlp_w_up, v_mlp_w_down=v_mlp_w_down, v_kv_norm=v_kv_norm, v_w_kv=v_w_kv, v_k_norm=v_k_norm, v_b_norm=v_b_norm, v_b_w_q=v_b_w_q, v_b_q_norm=v_b_q_norm, v_b_sinks=v_b_sinks, v_b_w_o=v_b_w_o)
    weights = {n: given[n] for n in TWIN_WEIGHTS}
    shared = {n: given[n] for n in SHARED_INPUTS}
    per_example = {n: given[n] for n in ['x']}
    grad_fn = _jax.value_and_grad(_loss, argnums=(0, 1))

    def one_microbatch(ex, loss_target):
        ex = dict(ex)
        diff = ex.pop(TWIN_DIFF_INPUT)
        return grad_fn(weights, diff, {**shared, **ex}, loss_target)

    if N_MICROBATCH == 1:
        loss, (grad_w, grad_x) = one_microbatch(per_example, given["loss_target"])
    else:
        def body(carry, xs):
            loss_sum, grad_sum = carry
            l_k, (gw_k, gx_k) = one_microbatch(xs[0], xs[1])
            with _jax.named_scope("update"):
                return (loss_sum + l_k, _jax.tree.map(_jnp.add, grad_sum, gw_k)), gx_k

        init = (_jnp.zeros((), _jnp.float32), _jax.tree.map(_jnp.zeros_like, weights))
        (loss, grad_w), grad_x = _jax.lax.scan(body, init, (per_example, given["loss_target"]))
    with _jax.named_scope("update"):
        delta_w, new_m, new_v = {}, {}, {}
        for n in TWIN_WEIGHTS:
            delta_w[n], new_m[n], new_v[n] = _adamw(weights[n], grad_w[n], given["m_" + n], given["v_" + n])
    return (loss, grad_x, *[grad_w[n] for n in TWIN_WEIGHTS], *[delta_w[n] for n in TWIN_WEIGHTS],
            *[new_m[n] for n in TWIN_WEIGHTS], *[new_v[n] for n in TWIN_WEIGHTS])
```

```python
import functools
import math

import jax
import jax.numpy as jnp
from jax import lax
from jax.experimental import pallas as pl
from jax.experimental.pallas import tpu as pltpu

F32 = jnp.float32
BF16 = jnp.bfloat16

D_MODEL = 2048
N_HEADS = 32
HEAD_DIM = 64
N_KV_HEADS = 4
Q_PER_KV = 8
ATT_BLOCK = 128
WKV_CHUNK = 64
LORA_PAD = 128
D_FF = 8192
N_DEV = 8
RMS_EPS = 1e-6
GN_EPS = 64e-5
L2_EPS = 1e-12
ROPE_THETA = 10000.0
ADAM_LR, ADAM_B1, ADAM_B2, ADAM_EPS, ADAM_WD, ADAM_STEP = 0.001, 0.9, 0.999, 1e-08, 0.01, 10
MASK_VALUE = -1e30
VMEM_LIMIT_BYTES = 56 * 1024 * 1024
MESH_AXES = ("x", "y", "c")
HI = lax.Precision.HIGHEST

_NN = (((1,), (0,)), ((), ()))
_NT = (((1,), (1,)), ((), ()))
_TN = (((0,), (0,)), ((), ()))


def _params(sem):
    return pltpu.CompilerParams(dimension_semantics=sem, vmem_limit_bytes=VMEM_LIMIT_BYTES)


def _hdot(a, b, dims=_NN):
    return lax.dot_general(a, b, dims, precision=HI, preferred_element_type=F32)


def _b16dot(a, b, dims):
    return lax.dot_general(a.astype(BF16), b.astype(BF16), dims, preferred_element_type=F32)


@jax.custom_vjp
def _bdot(a, b):
    return _b16dot(a, b, _NN)


def _bdot_fwd(a, b):
    return _b16dot(a, b, _NN), (a, b)


def _bdot_bwd(res, g):
    a, b = res
    return _b16dot(g, b, _NT), _b16dot(a, g, _TN)


_bdot.defvjp(_bdot_fwd, _bdot_bwd)


@jax.custom_vjp
def _bdot_nt(a, b):
    return _b16dot(a, b, _NT)


def _bdot_nt_fwd(a, b):
    return _b16dot(a, b, _NT), (a, b)


def _bdot_nt_bwd(res, g):
    a, b = res
    return _b16dot(g, b, _NN), _b16dot(g, a, _TN)


_bdot_nt.defvjp(_bdot_nt_fwd, _bdot_nt_bwd)


def _rms(x, gain):
    return x * lax.rsqrt(jnp.mean(x * x, axis=-1, keepdims=True) + RMS_EPS) * gain


def _vjp_of(f, n_in, diff):
    def g(*args):
        ins, cts = args[:n_in], args[n_in:]

        def fd(*d):
            full = list(ins)
            for pos, i in enumerate(diff):
                full[i] = d[pos]
            return f(*full)

        _, pull = jax.vjp(fd, *[ins[i] for i in diff])
        return pull(tuple(cts))
    return g


def _mm(a, b, *, name, ta=False, tb=False, a_pro=None, epi=None, epi_args=(), out_dtype=F32,
        tm=1024, tn=1024, tk=512, dims=None, b_spec=None, o_spec=None, o_shape=None):
    if dims is None:
        m, k = (a.shape[1], a.shape[0]) if ta else a.shape
        n = b.shape[0] if tb else b.shape[1]
    else:
        m, n, k = dims
    tm, tn, tk = min(tm, m), min(tn, n), min(tk, k)
    assert m % tm == 0 and n % tn == 0 and k % tk == 0, (name, m, n, k, tm, tn, tk)
    nk = k // tk
    ne = len(epi_args)
    cdims = (((0 if ta else 1,), (1 if tb else 0,)), ((), ()))

    def body(a_ref, b_ref, *rest):
        e_refs, o_ref, acc = rest[:ne], rest[ne], rest[ne + 1]
        kk = pl.program_id(2)

        @pl.when(kk == 0)
        def _():
            acc[...] = jnp.zeros_like(acc)

        av = a_ref[...]
        if a_pro is not None:
            av = a_pro(av.astype(F32))
        acc[...] += lax.dot_general(av.astype(BF16), b_ref[...].astype(BF16), cdims, preferred_element_type=F32)

        @pl.when(kk == nk - 1)
        def _():
            r = acc[...]
            if epi is not None:
                r = epi(r, *[e[...] for e in e_refs])
            o_ref[...] = r.astype(o_ref.dtype)

    a_spec = pl.BlockSpec((tk, tm), lambda i, j, q: (q, i)) if ta else pl.BlockSpec((tm, tk), lambda i, j, q: (i, q))
    if b_spec is None:
        b_spec = pl.BlockSpec((tn, tk), lambda i, j, q: (j, q)) if tb else pl.BlockSpec((tk, tn), lambda i, j, q: (q, j))
    if o_spec is None:
        o_spec = pl.BlockSpec((tm, tn), lambda i, j, q: (i, j))
        o_shape = (m, n)
    e_specs = [pl.BlockSpec((tm, tn), lambda i, j, q: (i, j)) for _ in epi_args]
    return pl.pallas_call(
        body, grid=(m // tm, n // tn, nk), in_specs=[a_spec, b_spec] + e_specs, out_specs=o_spec,
        out_shape=jax.ShapeDtypeStruct(o_shape, out_dtype), scratch_shapes=[pltpu.VMEM((tm, tn), F32)],
        compiler_params=_params(("parallel", "parallel", "arbitrary")), name=name,
    )(a, b, *epi_args)


def _rowwise(fn, rows, params, out_rows, out_params=(), *, tm=256, name):
    t = rows[0].shape[0]
    tm = min(tm, t)
    assert t % tm == 0
    nr, npar, nor, nop = len(rows), len(params), len(out_rows), len(out_params)

    def body(*refs):
        r, p = refs[:nr], refs[nr:nr + npar]
        o, op = refs[nr + npar:nr + npar + nor], refs[nr + npar + nor:]
        outs = fn(*[x[...] for x in r], *[x[...] for x in p])
        for ref, val in zip(o, outs[:nor]):
            ref[...] = val.astype(ref.dtype)
        if nop:
            @pl.when(pl.program_id(0) == 0)
            def _():
                for ref in op:
                    ref[...] = jnp.zeros_like(ref)

            for ref, val in zip(op, outs[nor:]):
                ref[...] += val.astype(F32)

    in_specs = [pl.BlockSpec((tm, x.shape[1]), lambda i: (i, 0)) for x in rows]
    in_specs += [pl.BlockSpec(p.shape, lambda i: (0, 0)) for p in params]
    out_specs = [pl.BlockSpec((tm, s.shape[1]), lambda i: (i, 0)) for s in out_rows]
    out_specs += [pl.BlockSpec(s.shape, lambda i: (0, 0)) for s in out_params]
    return pl.pallas_call(
        body, grid=(t // tm,), in_specs=in_specs, out_specs=out_specs, out_shape=list(out_rows) + list(out_params),
        compiler_params=_params(("arbitrary",)), name=name,
    )(*rows, *params)


def _headwise(fn, rows, params, out_rows, out_params=(), *, tt=512, name):
    nh, t, n = rows[0].shape
    tt = min(tt, t)
    assert t % tt == 0
    nr, npar, nor, nop = len(rows), len(params), len(out_rows), len(out_params)

    def body(*refs):
        r, p = refs[:nr], refs[nr:nr + npar]
        o, op = refs[nr + npar:nr + npar + nor], refs[nr + npar + nor:]
        outs = fn(*[x[...] for x in r], *[x[...] for x in p])
        for ref, val in zip(o, outs[:nor]):
            ref[...] = val.astype(ref.dtype)
        if nop:
            @pl.when(pl.program_id(1) == 0)
            def _():
                for ref in op:
                    ref[...] = jnp.zeros_like(ref)

            for ref, val in zip(op, outs[nor:]):
                ref[...] += val.astype(F32)

    row_spec = pl.BlockSpec((None, tt, n), lambda h, i: (h, i, 0))
    par_spec = pl.BlockSpec((None, 1, n), lambda h, i: (h, 0, 0))
    return pl.pallas_call(
        body, grid=(nh, t // tt), in_specs=[row_spec] * nr + [par_spec] * npar,
        out_specs=[row_spec] * nor + [par_spec] * nop, out_shape=list(out_rows) + list(out_params),
        compiler_params=_params(("parallel", "arbitrary")), name=name,
    )(*rows, *params)


def _sds(shape, dtype=F32):
    return jax.ShapeDtypeStruct(tuple(shape), dtype)


def _wkv_chunk(s0, r, lw, k, v, a, b):
    c = r.shape[0]
    ti = lax.broadcasted_iota(jnp.int32, (c, c), 0)
    si = lax.broadcasted_iota(jnp.int32, (c, c), 1)
    incl, strict = si <= ti, si < ti
    cum = _hdot(incl.astype(F32), lw)
    rcum = _hdot((si > ti).astype(F32), lw)
    tot = jnp.sum(lw, axis=0, keepdims=True)
    w_inv = jnp.exp(-cum)
    at, rt, bt, kt = a * jnp.exp(cum - lw), r * jnp.exp(cum), b * w_inv, k * w_inv
    l_ab = jnp.where(strict, _hdot(at, bt, _NT), 0.0)
    l_ak = jnp.where(strict, _hdot(at, kt, _NT), 0.0)
    t_rb = jnp.where(incl, _hdot(rt, bt, _NT), 0.0)
    t_rk = jnp.where(incl, _hdot(rt, kt, _NT), 0.0)
    u = _hdot(at, s0, _NT) + _hdot(l_ak, v)
    p = l_ab
    n_it = int(math.log2(c))
    for it in range(n_it):
        u = u + _hdot(p, u)
        if it + 1 < n_it:
            p = _hdot(p, p)
    y = _hdot(rt, s0, _NT) + _hdot(t_rb, u) + _hdot(t_rk, v)
    e = jnp.exp(rcum)
    s1 = s0 * jnp.exp(tot) + _hdot(u, b * e, _TN) + _hdot(v, k * e, _TN)
    return y, s1


def _wkv_fwd(r, lw, k, v, a, b):
    nh, t, n = r.shape
    nc = t // WKV_CHUNK

    def body(r_ref, lw_ref, k_ref, v_ref, a_ref, b_ref, y_ref, s_ref, state):
        @pl.when(pl.program_id(1) == 0)
        def _():
            state[...] = jnp.zeros_like(state)

        s0 = state[...]
        s_ref[...] = s0
        y, s1 = _wkv_chunk(s0, r_ref[...], lw_ref[...], k_ref[...], v_ref[...], a_ref[...], b_ref[...])
        y_ref[...] = y
        state[...] = s1

    blk = pl.BlockSpec((None, WKV_CHUNK, n), lambda h, c: (h, c, 0))
    sblk = pl.BlockSpec((None, None, n, n), lambda h, c: (h, c, 0, 0))
    return pl.pallas_call(
        body, grid=(nh, nc), in_specs=[blk] * 6, out_specs=[blk, sblk],
        out_shape=[_sds((nh, t, n)), _sds((nh, nc, n, n))], scratch_shapes=[pltpu.VMEM((n, n), F32)],
        compiler_params=_params(("parallel", "arbitrary")), name="wkv_fwd",
    )(r, lw, k, v, a, b)


def _wkv_bwd(r, lw, k, v, a, b, states, dy):
    nh, t, n = r.shape
    nc = t // WKV_CHUNK

    def body(r_ref, lw_ref, k_ref, v_ref, a_ref, b_ref, s_ref, dy_ref,
             dr_ref, dlw_ref, dk_ref, dv_ref, da_ref, db_ref, dstate):
        @pl.when(pl.program_id(1) == 0)
        def _():
            dstate[...] = jnp.zeros_like(dstate)

        _, pull = jax.vjp(_wkv_chunk, s_ref[...], r_ref[...], lw_ref[...], k_ref[...], v_ref[...], a_ref[...], b_ref[...])
        ds0, dr, dlw, dk, dv, da, db = pull((dy_ref[...], dstate[...]))
        dr_ref[...], dlw_ref[...], dk_ref[...] = dr, dlw, dk
        dv_ref[...], da_ref[...], db_ref[...] = dv, da, db
        dstate[...] = ds0

    blk = pl.BlockSpec((None, WKV_CHUNK, n), lambda h, c: (h, nc - 1 - c, 0))
    sblk = pl.BlockSpec((None, None, n, n), lambda h, c: (h, nc - 1 - c, 0, 0))
    return pl.pallas_call(
        body, grid=(nh, nc), in_specs=[blk] * 6 + [sblk, blk], out_specs=[blk] * 6,
        out_shape=[_sds((nh, t, n))] * 6, scratch_shapes=[pltpu.VMEM((n, n), F32)],
        compiler_params=_params(("parallel", "arbitrary")), name="wkv_bwd",
    )(r, lw, k, v, a, b, states, dy)


def _wkv_prep(k, wl, al, w0, a0, k_k, k_a):
    z = -(w0 + wl)
    softplus = jnp.maximum(z, 0.0) + jnp.log1p(jnp.exp(-jnp.abs(z)))
    lw = -jnp.exp(-softplus - 0.5)
    asig = jax.nn.sigmoid(a0 + al)
    kk = k * k_k
    kk = kk / jnp.maximum(jnp.sqrt(jnp.sum(kk * kk, axis=-1, keepdims=True)), L2_EPS)
    kmod = k * (1.0 + (asig - 1.0) * k_a)
    return lw, kmod, -kk, kk * asig


def _wkv_post(y, r, kmod, v, ln_w, ln_b, r_k):
    mu = jnp.mean(y, axis=-1, keepdims=True)
    var = jnp.mean(jnp.square(y - mu), axis=-1, keepdims=True)
    yn = (y - mu) * lax.rsqrt(var + GN_EPS)
    yn = yn * ln_w + ln_b
    return (yn + jnp.sum(r * kmod * r_k, axis=-1, keepdims=True) * v,)


def _attn_block(nonzero_block, qs, kc, kp, vc, vp, cos_c, sin_c, cos_p, sin_p, q_gain, k_gain, sinks):
    ri = lax.broadcasted_iota(jnp.int32, (HEAD_DIM, HEAD_DIM), 0)
    ci = lax.broadcasted_iota(jnp.int32, (HEAD_DIM, HEAD_DIM), 1)
    half = HEAD_DIM // 2
    rot = jnp.where(ri == ci + half, -1.0, 0.0) + jnp.where(ri + half == ci, 1.0, 0.0)

    def rope(x, cos, sin):
        return x * cos + _hdot(x, rot) * sin

    kcr = rope(_rms(kc, k_gain), cos_c, sin_c)
    kpr = rope(_rms(kp, k_gain), cos_p, sin_p)
    qi = lax.broadcasted_iota(jnp.int32, (ATT_BLOCK, ATT_BLOCK), 0)
    ki = lax.broadcasted_iota(jnp.int32, (ATT_BLOCK, ATT_BLOCK), 1)
    mask_c = ki <= qi
    mask_p = jnp.logical_and(ki > qi, nonzero_block)
    lane0 = (lax.broadcasted_iota(jnp.int32, (1, 128), 1) == 0).astype(F32)
    outs = []
    for g in range(Q_PER_KV):
        q = rope(_rms(qs[g], q_gain), cos_c, sin_c)
        sc = jnp.where(mask_c, _bdot_nt(q, kcr) * (HEAD_DIM ** -0.5), MASK_VALUE)
        sp = jnp.where(mask_p, _bdot_nt(q, kpr) * (HEAD_DIM ** -0.5), MASK_VALUE)
        sk = jnp.sum(sinks[g] * lane0, axis=1, keepdims=True)
        mx = jnp.maximum(jnp.maximum(jnp.max(sc, axis=1, keepdims=True), jnp.max(sp, axis=1, keepdims=True)), sk)
        mx = lax.stop_gradient(mx)
        ec, ep = jnp.exp(sc - mx), jnp.exp(sp - mx)
        den = jnp.sum(ec, axis=1, keepdims=True) + jnp.sum(ep, axis=1, keepdims=True) + jnp.exp(sk - mx)
        outs.append(_bdot(ec / den, vc) + _bdot(ep / den, vp))
    return tuple(outs)


def _attn_specs(t):
    nb = t // ATT_BLOCK
    prev = lambda n: jnp.maximum(n - 1, 0)
    q_spec = pl.BlockSpec((Q_PER_KV, ATT_BLOCK, HEAD_DIM), lambda g, n: (g, n, 0))
    kv_c = pl.BlockSpec((None, ATT_BLOCK, HEAD_DIM), lambda g, n: (g, n, 0))
    kv_p = pl.BlockSpec((None, ATT_BLOCK, HEAD_DIM), lambda g, n: (g, prev(n), 0))
    tab_c = pl.BlockSpec((ATT_BLOCK, HEAD_DIM), lambda g, n: (n, 0))
    tab_p = pl.BlockSpec((ATT_BLOCK, HEAD_DIM), lambda g, n: (prev(n), 0))
    gain = pl.BlockSpec((1, HEAD_DIM), lambda g, n: (0, 0))
    sink = pl.BlockSpec((Q_PER_KV, 1, 128), lambda g, n: (g, 0, 0))
    return nb, q_spec, kv_c, kv_p, tab_c, tab_p, gain, sink


def _attn_fwd(q, k, v, cos, sin, q_gain, k_gain, sinks):
    t = q.shape[1]
    nb, q_spec, kv_c, kv_p, tab_c, tab_p, gain, sink = _attn_specs(t)

    def body(q_ref, kc, kp, vc, vp, cc, sc, cp, sp, qg, kg, sk, o_ref):
        outs = _attn_block(pl.program_id(1) > 0, [q_ref[g] for g in range(Q_PER_KV)], kc[...], kp[...], vc[...], vp[...],
                           cc[...], sc[...], cp[...], sp[...], qg[...], kg[...], [sk[g] for g in range(Q_PER_KV)])
        for g in range(Q_PER_KV):
            o_ref[g] = outs[g]

    return pl.pallas_call(
        body, grid=(N_KV_HEADS, nb), in_specs=[q_spec, kv_c, kv_p, kv_c, kv_p, tab_c, tab_c, tab_p, tab_p, gain, gain, sink],
        out_specs=q_spec, out_shape=_sds(q.shape), compiler_params=_params(("parallel", "arbitrary")), name="attn_fwd",
    )(q, k, k, v, v, cos, sin, cos, sin, q_gain, k_gain, sinks)


def _attn_bwd(q, k, v, cos, sin, q_gain, k_gain, sinks, do):
    t = q.shape[1]
    nb, q_spec, kv_c, kv_p, tab_c, tab_p, gain, sink = _attn_specs(t)
    nq = Q_PER_KV

    def body(q_ref, kc, kp, vc, vp, cc, sc, cp, sp, qg, kg, sk, do_ref,
             dq_ref, dkc_ref, dkp_ref, dvc_ref, dvp_ref, dqg_ref, dkg_ref, dsk_ref):
        nonzero = pl.program_id(1) > 0
        tabs = (cc[...], sc[...], cp[...], sp[...])

        def f(qs, kcv, kpv, vcv, vpv, qgv, kgv, sks):
            return _attn_block(nonzero, qs, kcv, kpv, vcv, vpv, *tabs, qgv, kgv, sks)

        _, pull = jax.vjp(f, [q_ref[g] for g in range(nq)], kc[...], kp[...], vc[...], vp[...], qg[...], kg[...],
                          [sk[g] for g in range(nq)])
        dqs, dkc, dkp, dvc, dvp, dqg, dkg, dsks = pull(tuple(do_ref[g] for g in range(nq)))
        for g in range(nq):
            dq_ref[g] = dqs[g]
        dkc_ref[...], dkp_ref[...], dvc_ref[...], dvp_ref[...] = dkc, dkp, dvc, dvp
        first = jnp.logical_and(pl.program_id(0) == 0, pl.program_id(1) == 0)

        @pl.when(first)
        def _():
            dqg_ref[...] = jnp.zeros_like(dqg_ref)
            dkg_ref[...] = jnp.zeros_like(dkg_ref)

        @pl.when(pl.program_id(1) == 0)
        def _():
            dsk_ref[...] = jnp.zeros_like(dsk_ref)

        dqg_ref[...] += dqg
        dkg_ref[...] += dkg
        for g in range(nq):
            dsk_ref[g] += dsks[g]

    kv_shape = _sds(k.shape)
    return pl.pallas_call(
        body, grid=(N_KV_HEADS, nb),
        in_specs=[q_spec, kv_c, kv_p, kv_c, kv_p, tab_c, tab_c, tab_p, tab_p, gain, gain, sink, q_spec],
        out_specs=[q_spec, kv_c, kv_c, kv_c, kv_c, gain, gain, sink],
        out_shape=[_sds(q.shape), kv_shape, kv_shape, kv_shape, kv_shape, _sds((1, HEAD_DIM)), _sds((1, HEAD_DIM)), _sds(sinks.shape)],
        compiler_params=_params(("arbitrary", "arbitrary")), name="attn_bwd",
    )(q, k, k, v, v, cos, sin, cos, sin, q_gain, k_gain, sinks, do)


def _time_shift_lerps(x, xs, gain, *mix):
    xn, xsn = _rms(x, gain), _rms(xs, gain)
    xx = xsn - xn
    return tuple(xn + xx * m for m in mix)


def _residual_norm(h, delta, gain):
    hn = h + delta
    return hn, _rms(hn, gain)


def _residual_norm2(h, delta, gain_a, gain_b):
    hn = h + delta
    return hn, _rms(hn, gain_a), _rms(hn, gain_b)


def _relu2(u):
    return jnp.square(jnp.maximum(u, 0.0))


def _sigmoid(z):
    return jax.nn.sigmoid(z)


def _to_heads(x2d, nh):
    t = x2d.shape[0]
    return x2d.reshape(t, nh, HEAD_DIM).transpose(1, 0, 2)


def _from_heads(xh):
    nh, t, n = xh.shape
    return xh.transpose(1, 0, 2).reshape(t, nh * n)


def _shift_down(x):
    return jnp.pad(x[:-1], ((1, 0), (0, 0)))


def _shift_up(x):
    return jnp.pad(x[1:], ((0, 1), (0, 0)))


def _rope_tables(t):
    half = HEAD_DIM // 2
    inv_freq = jnp.power(ROPE_THETA, -jnp.arange(half, dtype=F32) / half)
    ang = jnp.arange(t, dtype=jnp.int32).astype(F32)[:, None] * inv_freq[None, :]
    cos, sin = jnp.cos(ang), jnp.sin(ang)
    return jnp.concatenate([cos, cos], axis=1), jnp.concatenate([sin, sin], axis=1)


def _mlp_fwd(hn, w_up, w_down, layer, up_dev_major):
    t = hn.shape[0]
    if up_dev_major:
        cw = w_up.shape[2]
        u = _mm(hn, w_up, name=f"mlp{layer}_up", dims=(t, D_FF, D_MODEL), tn=cw,
                b_spec=pl.BlockSpec((None, 512, cw), lambda i, j, q: (j, q, 0)))
    else:
        u = _mm(hn, w_up, name=f"mlp{layer}_up")
    out = _mm(u, w_down, a_pro=_relu2, name=f"mlp{layer}_down")
    return u, out


def _mlp_bwd(hn, u, dh, w_up, w_down, layer, up_dev_major):
    t = hn.shape[0]
    du = _mm(dh, w_down, tb=True, epi=lambda r, uu: r * (2.0 * jnp.maximum(uu, 0.0)), epi_args=(u,), out_dtype=BF16,
             name=f"mlp{layer}_du")
    d_down = _mm(u, dh, ta=True, a_pro=_relu2, name=f"mlp{layer}_ddown")
    if up_dev_major:
        cw = w_up.shape[2]
        d_up = _mm(hn, du, ta=True, name=f"mlp{layer}_dup", dims=(D_MODEL, D_FF, t), tn=cw,
                   o_spec=pl.BlockSpec((None, 1024, cw), lambda i, j, q: (j, i, 0)), o_shape=(N_DEV, D_MODEL, cw))
        dhn = _mm(du, w_up, tb=True, name=f"mlp{layer}_dhn", dims=(t, D_MODEL, D_FF), tk=cw,
                  b_spec=pl.BlockSpec((None, 1024, cw), lambda i, j, q: (q, j, 0)))
    else:
        d_up = _mm(hn, du, ta=True, name=f"mlp{layer}_dup")
        dhn = _mm(du, w_up, tb=True, name=f"mlp{layer}_dhn")
    return dhn, d_up, d_down


def _local_step(x, target, w, up_dev_major=True):
    t = x.shape[0]
    g = {}
    row = lambda: _sds((t, D_MODEL))
    rowb = lambda: _sds((t, D_MODEL), BF16)
    vec = lambda: _sds((1, D_MODEL))

    xs = _shift_down(x)
    mix = [w["a_mix"][i:i + 1] for i in range(6)]
    xr, xk, xv, xw, xa, xg = _rowwise(_time_shift_lerps, [x, xs], [w["a_norm"]] + mix, [rowb()] * 6, tm=256, name="tmix_lerp")
    r = _mm(xr, w["a_w_r"], name="tmix_r")
    k = _mm(xk, w["a_w_k"], name="tmix_k")
    v = _mm(xv, w["a_w_v"], name="tmix_v")
    lw1 = _mm(xw, w["a_w1"], name="tmix_w1")
    wl = _mm(lw1, w["a_w2"], a_pro=jnp.tanh, name="tmix_w2")
    la1 = _mm(xa, w["a_a1"], name="tmix_a1")
    al = _mm(la1, w["a_a2"], name="tmix_a2")
    lg1 = _mm(xg, w["a_g1"], name="tmix_g1")
    gate = _mm(lg1, w["a_g2"], a_pro=_sigmoid, name="tmix_g2")

    hv = lambda name: w[name].reshape(N_HEADS, 1, HEAD_DIM)
    rh, kh, vh, wlh, alh = [_to_heads(z, N_HEADS) for z in (r, k, v, wl, al)]
    head = lambda: _sds((N_HEADS, t, HEAD_DIM))
    prep_params = [hv("a_w0"), hv("a_a0"), hv("a_k_k"), hv("a_k_a")]
    lwh, kmh, ah, bh = _headwise(_wkv_prep, [kh, wlh, alh], prep_params, [head()] * 4, name="wkv_prep")
    yh, states = _wkv_fwd(rh, lwh, kmh, vh, ah, bh)
    post_params = [hv("a_ln_x_w"), hv("a_ln_x_b"), hv("a_r_k")]
    (y2h,) = _headwise(_wkv_post, [yh, rh, kmh, vh], post_params, [head()], name="wkv_post")
    y2 = _from_heads(y2h)
    (yg,) = _rowwise(lambda a, b: (a * b,), [y2, gate], [], [rowb()], name="tmix_gate")
    att = _mm(yg, w["a_w_out"], name="tmix_out")

    h1, hn0 = _rowwise(_residual_norm, [x, att], [w["mlp_norm0"]], [row(), rowb()], name="res_norm0")
    u0, m0 = _mlp_fwd(hn0, w["mlp_w_up0"], w["mlp_w_down0"], 0, up_dev_major)

    h2, kvn, qn = _rowwise(_residual_norm2, [h1, m0], [w["kv_norm"], w["b_norm"]], [row(), rowb(), rowb()], name="res_norm_kvq")
    kv = _mm(kvn, w["w_kv"], name="kv_proj")
    q = _mm(qn, w["b_w_q"], name="q_proj")
    half_kv = N_KV_HEADS * HEAD_DIM
    ksh, vsh = _to_heads(kv[:, :half_kv], N_KV_HEADS), _to_heads(kv[:, half_kv:], N_KV_HEADS)
    qh = _to_heads(q, N_HEADS)
    cos, sin = _rope_tables(t)
    sinks = jnp.broadcast_to(w["b_sinks"].reshape(N_HEADS, 1, 1), (N_HEADS, 1, 128))
    oh = _attn_fwd(qh, ksh, vsh, cos, sin, w["b_q_norm"], w["k_norm"], sinks)
    o = _from_heads(oh).astype(BF16)
    att2 = _mm(o, w["b_w_o"], name="attn_out")

    h3, hn1 = _rowwise(_residual_norm, [h2, att2], [w["mlp_norm1"]], [row(), rowb()], name="res_norm1")
    u1, m1 = _mlp_fwd(hn1, w["mlp_w_up1"], w["mlp_w_down1"], 1, up_dev_major)

    def loss_fn(h, m, tg):
        diff = (h + m) - tg
        part = 0.5 * jnp.sum(jnp.mean(jnp.square(diff), axis=-1, keepdims=True), axis=0, keepdims=True)
        return diff * (1.0 / D_MODEL), jnp.broadcast_to(part, (1, 128))

    dh4, loss = _rowwise(loss_fn, [h3, m1, target], [], [row()], [_sds((1, 128))], name="loss")

    def res_norm_bwd(h, dnext, dhn, gain):
        dh, dgain = _vjp_of(lambda hh, gg: (_rms(hh, gg),), 2, (0, 1))(h, gain, dhn)
        return dnext + dh, dgain

    dhn1, g["mlp_w_up1"], g["mlp_w_down1"] = _mlp_bwd(hn1, u1, dh4, w["mlp_w_up1"], w["mlp_w_down1"], 1, up_dev_major)
    dh3, g["mlp_norm1"] = _rowwise(res_norm_bwd, [h3, dh4, dhn1], [w["mlp_norm1"]], [row()], [vec()], name="res_norm1_bwd")

    g["b_w_o"] = _mm(o, dh3, ta=True, name="attn_out_dw")
    do = _mm(dh3, w["b_w_o"], tb=True, name="attn_out_dx")
    dqh, dkc, dkp, dvc, dvp, g["b_q_norm"], g["k_norm"], dsinks = _attn_bwd(
        qh, ksh, vsh, cos, sin, w["b_q_norm"], w["k_norm"], sinks, _to_heads(do, N_HEADS))
    g["b_sinks"] = dsinks[:, 0, 0].reshape(1, N_HEADS)
    dq = _from_heads(dqh)
    g["b_w_q"] = _mm(qn, dq, ta=True, name="q_proj_dw")
    dqn = _mm(dq, w["b_w_q"], tb=True, name="q_proj_dx")
    shift_blk = lambda z: jnp.pad(z[:, ATT_BLOCK:], ((0, 0), (0, ATT_BLOCK), (0, 0)))
    kv_parts = [_from_heads(z) for z in (dkc, shift_blk(dkp), dvc, shift_blk(dvp))]
    (dkv,) = _rowwise(lambda a, b, c, d: (jnp.concatenate([a + b, c + d], axis=1),), kv_parts, [],
                      [_sds((t, 2 * half_kv), BF16)], name="kv_grad_sum")
    g["w_kv"] = _mm(kvn, dkv, ta=True, name="kv_proj_dw")
    dkvn = _mm(dkv, w["w_kv"], tb=True, name="kv_proj_dx")

    def res_norm2_bwd(h, dnext, dna, dnb, gain_a, gain_b):
        dha, dga = _vjp_of(lambda hh, gg: (_rms(hh, gg),), 2, (0, 1))(h, gain_a, dna)
        dhb, dgb = _vjp_of(lambda hh, gg: (_rms(hh, gg),), 2, (0, 1))(h, gain_b, dnb)
        return dnext + dha + dhb, dga, dgb

    dh2, g["kv_norm"], g["b_norm"] = _rowwise(res_norm2_bwd, [h2, dh3, dkvn, dqn], [w["kv_norm"], w["b_norm"]],
                                              [row()], [vec(), vec()], name="res_norm_kvq_bwd")

    dhn0, g["mlp_w_up0"], g["mlp_w_down0"] = _mlp_bwd(hn0, u0, dh2, w["mlp_w_up0"], w["mlp_w_down0"], 0, up_dev_major)
    dh1, g["mlp_norm0"] = _rowwise(res_norm_bwd, [h1, dh2, dhn0], [w["mlp_norm0"]], [row()], [vec()], name="res_norm0_bwd")

    g["a_w_out"] = _mm(yg, dh1, ta=True, name="tmix_out_dw")
    dyg = _mm(dh1, w["a_w_out"], tb=True, name="tmix_out_dx")
    dy2, dgate = _rowwise(lambda d, a, b: (d * b, d * a), [dyg, y2, gate], [], [row(), rowb()], name="tmix_gate_bwd")
    g["a_g2"] = _mm(lg1, dgate, ta=True, a_pro=_sigmoid, name="tmix_g2_dw")

    def dsigmoid(rr, z):
        s = jax.nn.sigmoid(z)
        return rr * s * (1.0 - s)

    dlg1 = _mm(dgate, w["a_g2"], tb=True, epi=dsigmoid, epi_args=(lg1,), out_dtype=BF16, name="tmix_g2_dx")
    g["a_g1"] = _mm(xg, dlg1, ta=True, name="tmix_g1_dw")
    dxg = _mm(dlg1, w["a_g1"], tb=True, name="tmix_g1_dx")

    post_vjp = _vjp_of(_wkv_post, 7, (0, 1, 2, 3, 4, 5, 6))

    def post_bwd(y_, r_, km_, v_, dy_, lnw_, lnb_, rk_):
        return post_vjp(y_, r_, km_, v_, lnw_, lnb_, rk_, dy_)

    dyh, drh1, dkmh1, dvh1, dlnw, dlnb, drk = _headwise(
        post_bwd, [yh, rh, kmh, vh, _to_heads(dy2, N_HEADS)], post_params, [head()] * 4, [_sds((N_HEADS, 1, HEAD_DIM))] * 3,
        name="wkv_post_bwd")
    drh2, dlwh, dkmh2, dvh2, dah, dbh = _wkv_bwd(rh, lwh, kmh, vh, ah, bh, states, dyh)

    def prep_bwd(kk_, wl_, al_, dlw_, dkm1, dkm2, da_, db_, dr1, dr2, dv1, dv2, w0_, a0_, kk0_, ka_):
        grads = _vjp_of(_wkv_prep, 7, (0, 1, 2, 3, 4, 5, 6))(kk_, wl_, al_, w0_, a0_, kk0_, ka_, dlw_, dkm1 + dkm2, da_, db_)
        dk_, dwl_, dal_, dw0_, da0_, dkk_, dka_ = grads
        return dk_, dwl_, dal_, dr1 + dr2, dv1 + dv2, dw0_, da0_, dkk_, dka_

    dkh, dwlh, dalh, drh, dvh, dw0, da0, dk_k, dk_a = _headwise(
        prep_bwd, [kh, wlh, alh, dlwh, dkmh1, dkmh2, dah, dbh, drh1, drh2, dvh1, dvh2], prep_params,
        [_sds((N_HEADS, t, HEAD_DIM), BF16)] * 5, [_sds((N_HEADS, 1, HEAD_DIM))] * 4, tt=256, name="wkv_prep_bwd")
    flat = lambda z: z.reshape(1, D_MODEL)
    g["a_w0"], g["a_a0"], g["a_k_k"], g["a_k_a"] = flat(dw0), flat(da0), flat(dk_k), flat(dk_a)
    g["a_ln_x_w"], g["a_ln_x_b"], g["a_r_k"] = flat(dlnw), flat(dlnb), flat(drk)
    dr, dk, dv, dwl, dal = [_from_heads(z) for z in (drh, dkh, dvh, dwlh, dalh)]

    g["a_w_r"] = _mm(xr, dr, ta=True, name="tmix_r_dw")
    g["a_w_k"] = _mm(xk, dk, ta=True, name="tmix_k_dw")
    g["a_w_v"] = _mm(xv, dv, ta=True, name="tmix_v_dw")
    dxr = _mm(dr, w["a_w_r"], tb=True, name="tmix_r_dx")
    dxk = _mm(dk, w["a_w_k"], tb=True, name="tmix_k_dx")
    dxv = _mm(dv, w["a_w_v"], tb=True, name="tmix_v_dx")
    g["a_w2"] = _mm(lw1, dwl, ta=True, a_pro=jnp.tanh, name="tmix_w2_dw")

    def dtanh(rr, z):
        th = jnp.tanh(z)
        return rr * (1.0 - th * th)

    dlw1 = _mm(dwl, w["a_w2"], tb=True, epi=dtanh, epi_args=(lw1,), out_dtype=BF16, name="tmix_w2_dx")
    g["a_w1"] = _mm(xw, dlw1, ta=True, name="tmix_w1_dw")
    dxw = _mm(dlw1, w["a_w1"], tb=True, name="tmix_w1_dx")
    g["a_a2"] = _mm(la1, dal, ta=True, name="tmix_a2_dw")
    dla1 = _mm(dal, w["a_a2"], tb=True, out_dtype=BF16, name="tmix_a2_dx")
    g["a_a1"] = _mm(xa, dla1, ta=True, name="tmix_a1_dw")
    dxa = _mm(dla1, w["a_a1"], tb=True, name="tmix_a1_dx")

    lerp_bwd = _vjp_of(_time_shift_lerps, 9, tuple(range(9)))

    def lerp_bwd_rows(x_, xs_, d0, d1, d2, d3, d4, d5, gain, *mx):
        return lerp_bwd(x_, xs_, gain, *mx, d0, d1, d2, d3, d4, d5)

    outs = _rowwise(lerp_bwd_rows, [x, xs, dxr, dxk, dxv, dxw, dxa, dxg], [w["a_norm"]] + mix, [row(), row()], [vec()] * 7,
                    tm=128, name="tmix_lerp_bwd")
    dx_a, dxs, g["a_norm"] = outs[0], outs[1], outs[2]
    g["a_mix"] = jnp.concatenate(outs[3:9], axis=0)
    (grad_x,) = _rowwise(lambda a, b, c: (a + b + c,), [dh1, dx_a, _shift_up(dxs)], [], [row()], name="grad_x_sum")
    return loss, grad_x, g


_ANY = pl.BlockSpec(memory_space=pl.ANY)
_MESH_ID = pl.DeviceIdType.MESH
N_PEERS = N_DEV - 1


def _linear(pos):
    return 4 * pos[0] + 2 * pos[1] + pos[2]


def _all_gather(shards, name):
    n = len(shards)

    def body(*refs):
        ins, outs = refs[:n], refs[n:2 * n]
        send_sems, recv_sems, local_sems = refs[2 * n:]
        x, y, c = lax.axis_index("x"), lax.axis_index("y"), lax.axis_index("c")
        me, sibling = (x, y, c), (x, y, 1 - c)
        chips = [(1 - x, y), (x, 1 - y), (1 - x, 1 - y)]

        def copy(a, k, block, to, src=None):
            dst = outs[a].at[_linear(block)]
            return pltpu.make_async_remote_copy(
                src_ref=dst if src is None else src, dst_ref=dst, send_sem=send_sems.at[a * N_PEERS + k],
                recv_sem=recv_sems.at[a * N_PEERS + k], device_id=to, device_id_type=_MESH_ID)

        mine = [pltpu.make_async_copy(ins[a], outs[a].at[_linear(me)], local_sems.at[a]) for a in range(n)]
        for cp in mine:
            cp.start()
        first = []
        for a in range(n):
            first.append(copy(a, 0, me, sibling, src=ins[a]))
            first += [copy(a, 1 + j, me, (*chip, c), src=ins[a]) for j, chip in enumerate(chips)]
        for cp in first:
            cp.start()
        passed = []
        for j, chip in enumerate(chips):
            for a in range(n):
                copy(a, 1 + j, (*chip, c), me).wait_recv()
                fwd = copy(a, 4 + j, (*chip, c), sibling)
                fwd.start()
                passed.append(fwd)
        for a in range(n):
            copy(a, 0, sibling, me).wait_recv()
            for j, chip in enumerate(chips):
                copy(a, 4 + j, (*chip, 1 - c), me).wait_recv()
        for cp in first + passed:
            cp.wait_send()
        for cp in mine:
            cp.wait()

    return pl.pallas_call(
        body, out_shape=[_sds((N_DEV,) + s.shape, s.dtype) for s in shards], in_specs=[_ANY] * n, out_specs=[_ANY] * n,
        scratch_shapes=[pltpu.SemaphoreType.DMA((n * N_PEERS,)), pltpu.SemaphoreType.DMA((n * N_PEERS,)),
                        pltpu.SemaphoreType.DMA((n,))],
        name=name,
    )(*shards)


def _reduce_scatter(parts, name):
    n = len(parts)

    def body(*refs):
        ins, outs = refs[:n], refs[n:2 * n]
        send_sems, recv_sems, local_sems = refs[2 * n:]
        x, y, c = lax.axis_index("x"), lax.axis_index("y"), lax.axis_index("c")
        me = _linear((x, y, c))
        mine = [pltpu.make_async_copy(ins[a].at[me], outs[a].at[me], local_sems.at[a]) for a in range(n)]
        for cp in mine:
            cp.start()
        copies = []
        for k in range(1, N_DEV):
            peer = (1 - x if k & 4 else x, 1 - y if k & 2 else y, 1 - c if k & 1 else c)
            for a in range(n):
                cp = pltpu.make_async_remote_copy(
                    src_ref=ins[a].at[_linear(peer)], dst_ref=outs[a].at[me], send_sem=send_sems.at[a * N_PEERS + k - 1],
                    recv_sem=recv_sems.at[a * N_PEERS + k - 1], device_id=peer, device_id_type=_MESH_ID)
                cp.start()
                copies.append(cp)
        for cp in copies:
            cp.wait()
        for cp in mine:
            cp.wait()

    return pl.pallas_call(
        body, out_shape=[_sds(p.shape, p.dtype) for p in parts], in_specs=[_ANY] * n, out_specs=[_ANY] * n,
        scratch_shapes=[pltpu.SemaphoreType.DMA((n * N_PEERS,)), pltpu.SemaphoreType.DMA((n * N_PEERS,)),
                        pltpu.SemaphoreType.DMA((n,))],
        name=name,
    )(*parts)


def _adamw(w, m, v, slots, name):
    r, c = w.shape
    ns = slots.shape[0]
    tr = max(8, min(r, (1 << 18) // c))
    assert r % tr == 0, (name, r, tr)

    def body(w_ref, m_ref, v_ref, g_ref, g_out, d_out, m_out, v_out):
        g = g_ref[0].astype(F32)
        for s in range(1, ns):
            g = g + g_ref[s].astype(F32)
        m_new = ADAM_B1 * m_ref[...] + (1.0 - ADAM_B1) * g
        v_new = ADAM_B2 * v_ref[...] + (1.0 - ADAM_B2) * jnp.square(g)
        m_hat = m_new / (1.0 - ADAM_B1 ** ADAM_STEP)
        v_hat = v_new / (1.0 - ADAM_B2 ** ADAM_STEP)
        d_out[...] = -ADAM_LR * (m_hat / (jnp.sqrt(v_hat) + ADAM_EPS) + ADAM_WD * w_ref[...])
        g_out[...], m_out[...], v_out[...] = g, m_new, v_new

    spec = pl.BlockSpec((tr, c), lambda i: (i, 0))
    return pl.pallas_call(
        body, grid=(r // tr,), in_specs=[spec, spec, spec, pl.BlockSpec((ns, tr, c), lambda i: (0, i, 0))],
        out_specs=[spec] * 4, out_shape=[_sds((r, c))] * 4, compiler_params=_params(("parallel",)), name=name,
    )(w, m, v, slots)


_COL_VECTORS = ("a_norm", "a_mix", "a_w0", "a_a0", "a_k_k", "a_k_a", "a_ln_x_w", "a_ln_x_b")
_COL_VEC_ROWS = 16
_COL_ROWS = _COL_VEC_ROWS + 2 * LORA_PAD + 256
_ROW_COLS = 2 * LORA_PAD + 256 + 512
_REPL_ROWS = 8


def _pad_to(a, size, axis):
    widths = [(0, 0)] * a.ndim
    widths[axis] = (0, size - a.shape[axis])
    return jnp.pad(a, widths)


def _pack_cols(p):
    width = p["a_norm"].shape[-1]
    vecs = jnp.concatenate([p[n].reshape(-1, width) for n in _COL_VECTORS], axis=0)
    return jnp.concatenate([_pad_to(vecs, _COL_VEC_ROWS, 0), _pad_to(p["a_w2"].reshape(-1, width), LORA_PAD, 0),
                            _pad_to(p["a_a2"].reshape(-1, width), LORA_PAD, 0), p["a_g2"].reshape(-1, width)], axis=0)


def _unpack_cols(a, lead):
    width = a.shape[-1]
    out, row = {}, 0
    for n in _COL_VECTORS:
        k = 6 if n == "a_mix" else 1
        out[n] = a[row:row + k].reshape(lead + ((6, width) if n == "a_mix" else (width,)))
        row += k
    base = _COL_VEC_ROWS
    out["a_w2"] = a[base:base + 96].reshape(lead + (96, width))
    out["a_a2"] = a[base + LORA_PAD:base + LORA_PAD + 96].reshape(lead + (96, width))
    out["a_g2"] = a[base + 2 * LORA_PAD:].reshape(lead + (256, width))
    return out


def _pack_rows(p):
    rows = p["w_kv"].shape[0]
    return jnp.concatenate([_pad_to(p["a_w1"].reshape(rows, -1), LORA_PAD, 1), _pad_to(p["a_a1"].reshape(rows, -1), LORA_PAD, 1),
                            p["a_g1"].reshape(rows, -1), p["w_kv"]], axis=1)


def _unpack_rows(a, lead):
    rows = a.shape[0]
    return {"a_w1": a[:, :96].reshape(lead + (rows, 96)), "a_a1": a[:, LORA_PAD:LORA_PAD + 96].reshape(lead + (rows, 96)),
            "a_g1": a[:, 2 * LORA_PAD:2 * LORA_PAD + 256].reshape(lead + (rows, 256)), "w_kv": a[:, 2 * LORA_PAD + 256:]}


def _pack_repl(p):
    row = lambda a: _pad_to(a.reshape(1, -1), D_MODEL, 1)
    return jnp.concatenate([p["mlp_norm"].reshape(2, D_MODEL), row(p["kv_norm"]), row(p["b_norm"]), row(p["a_r_k"]),
                            row(p["k_norm"]), row(p["b_q_norm"]), row(p["b_sinks"])], axis=0)


def _unpack_repl(a):
    return {"mlp_norm": a[0:2], "kv_norm": a[2], "b_norm": a[3:4], "a_r_k": a[4].reshape(1, N_HEADS, HEAD_DIM),
            "k_norm": a[5, :HEAD_DIM], "b_q_norm": a[6:7, :HEAD_DIM], "b_sinks": a[7:8, :N_HEADS]}


_WEIGHTS = ("a_norm", "a_mix", "a_w_rkv", "a_w0", "a_w1", "a_w2", "a_a0", "a_a1", "a_a2", "a_g1", "a_g2", "a_k_k", "a_k_a",
            "a_r_k", "a_ln_x_w", "a_ln_x_b", "a_w_out", "mlp_norm", "mlp_w_up", "mlp_w_down", "kv_norm", "w_kv", "k_norm",
            "b_norm", "b_w_q", "b_q_norm", "b_sinks", "b_w_o")


def _big_shards(p):
    return [p["a_w_rkv"][0, 0], p["a_w_rkv"][0, 1], p["a_w_rkv"][0, 2], p["a_w_out"][0], p["mlp_w_up"][0], p["mlp_w_up"][1],
            p["mlp_w_down"][0], p["mlp_w_down"][1], p["b_w_q"][0], p["b_w_o"][0]]


_BIG_NAMES = ("a_w_r", "a_w_k", "a_w_v", "a_w_out", "mlp_w_up0", "mlp_w_up1", "mlp_w_down0", "mlp_w_down1", "b_w_q", "b_w_o")


def kernel(x, a_norm, a_mix, a_w_rkv, a_w0, a_w1, a_w2, a_a0, a_a1, a_a2, a_g1, a_g2, a_k_k, a_k_a, a_r_k, a_ln_x_w,
           a_ln_x_b, a_w_out, mlp_norm, mlp_w_up, mlp_w_down, kv_norm, w_kv, k_norm, b_norm, b_w_q, b_q_norm, b_sinks,
           b_w_o, loss_target, m_a_norm, m_a_mix, m_a_w_rkv, m_a_w0, m_a_w1, m_a_w2, m_a_a0, m_a_a1, m_a_a2, m_a_g1,
           m_a_g2, m_a_k_k, m_a_k_a, m_a_r_k, m_a_ln_x_w, m_a_ln_x_b, m_a_w_out, m_mlp_norm, m_mlp_w_up, m_mlp_w_down,
           m_kv_norm, m_w_kv, m_k_norm, m_b_norm, m_b_w_q, m_b_q_norm, m_b_sinks, m_b_w_o, v_a_norm, v_a_mix, v_a_w_rkv,
           v_a_w0, v_a_w1, v_a_w2, v_a_a0, v_a_a1, v_a_a2, v_a_g1, v_a_g2, v_a_k_k, v_a_k_a, v_a_r_k, v_a_ln_x_w,
           v_a_ln_x_b, v_a_w_out, v_mlp_norm, v_mlp_w_up, v_mlp_w_down, v_kv_norm, v_w_kv, v_k_norm, v_b_norm, v_b_w_q,
           v_b_q_norm, v_b_sinks, v_b_w_o):
    given = locals()
    wts = {n: given[n] for n in _WEIGHTS}
    mom = {n: given["m_" + n] for n in _WEIGHTS}
    var = {n: given["v_" + n] for n in _WEIGHTS}

    cols_w, rows_w, repl_w = _pack_cols(wts), _pack_rows(wts), _pack_repl(wts)
    big_w = _big_shards(wts)
    gathered = _all_gather([cols_w, rows_w] + [b.astype(BF16) for b in big_w], name="gather_weights")
    full_cols = gathered[0].transpose(1, 0, 2).reshape(_COL_ROWS, D_MODEL)
    full_rows = gathered[1].reshape(D_MODEL, _ROW_COLS)
    w = {}
    w.update({k: v.reshape(v.shape[1:]) for k, v in _unpack_cols(full_cols, (1,)).items()})
    for k in ("a_norm", "a_w0", "a_a0", "a_k_k", "a_k_a", "a_ln_x_w", "a_ln_x_b"):
        w[k] = w[k].reshape(1, D_MODEL)
    for k in ("a_w2", "a_a2"):
        w[k] = _pad_to(w[k], LORA_PAD, 0)
    rows_full = _unpack_rows(full_rows, ())
    w["a_w1"], w["a_a1"] = _pad_to(rows_full["a_w1"], LORA_PAD, 1), _pad_to(rows_full["a_a1"], LORA_PAD, 1)
    w["a_g1"], w["w_kv"] = rows_full["a_g1"], rows_full["w_kv"]
    for k, arr in zip(_BIG_NAMES, gathered[2:]):
        w[k] = arr if k.startswith("mlp_w_up") else arr.reshape(N_DEV * arr.shape[1], arr.shape[2])
    w["mlp_norm0"], w["mlp_norm1"] = mlp_norm[0:1], mlp_norm[1:2]
    w["kv_norm"], w["k_norm"] = kv_norm.reshape(1, D_MODEL), k_norm.reshape(1, HEAD_DIM)
    w["b_norm"], w["b_q_norm"], w["b_sinks"], w["a_r_k"] = b_norm, b_q_norm, b_sinks, a_r_k.reshape(1, D_MODEL)

    loss_local, grad_x, g = _local_step(x[0], loss_target[0], w)
    loss = lax.psum(loss_local[0, 0], MESH_AXES)

    g_lead = {k: g[k][None] for k in ("a_norm", "a_mix", "a_w0", "a_a0", "a_k_k", "a_k_a", "a_ln_x_w", "a_ln_x_b", "a_g2")}
    g_lead["a_w2"], g_lead["a_a2"] = g["a_w2"][None, :96], g["a_a2"][None, :96]
    g_cols = _pack_cols(g_lead).reshape(_COL_ROWS, N_DEV, D_MODEL // N_DEV).transpose(1, 0, 2)
    g_rows = _pack_rows({"a_w1": g["a_w1"][:, :96], "a_a1": g["a_a1"][:, :96], "a_g1": g["a_g1"], "w_kv": g["w_kv"]})
    g_rows = g_rows.reshape(N_DEV, D_MODEL // N_DEV, _ROW_COLS)
    g_big = [g[k] if k.startswith("mlp_w_up") else g[k].reshape((N_DEV, g[k].shape[0] // N_DEV, g[k].shape[1])) for k in _BIG_NAMES]
    reduced = _reduce_scatter([g_cols, g_rows] + g_big, name="scatter_grads")
    g_repl = _pack_repl({"mlp_norm": jnp.concatenate([g["mlp_norm0"], g["mlp_norm1"]], axis=0), "kv_norm": g["kv_norm"],
                         "b_norm": g["b_norm"], "a_r_k": g["a_r_k"], "k_norm": g["k_norm"], "b_q_norm": g["b_q_norm"],
                         "b_sinks": g["b_sinks"]})
    (repl_slots,) = _all_gather([g_repl], name="gather_replicated_grads")

    res = {}
    cols4 = _adamw(cols_w, _pack_cols(mom), _pack_cols(var), reduced[0], name="adamw_cols")
    rows4 = _adamw(rows_w, _pack_rows(mom), _pack_rows(var), reduced[1], name="adamw_rows")
    repl4 = _adamw(repl_w, _pack_repl(mom), _pack_repl(var), repl_slots, name="adamw_replicated")
    for unpacked in ([_unpack_cols(a, (1,)) for a in cols4], [_unpack_rows(a, (1,)) for a in rows4], [_unpack_repl(a) for a in repl4]):
        for k in unpacked[0]:
            res[k] = tuple(u[k] for u in unpacked)
    big4 = [_adamw(bw, bm, bv, slots, name="adamw_" + k)
            for k, bw, bm, bv, slots in zip(_BIG_NAMES, big_w, _big_shards(mom), _big_shards(var), reduced[2:])]
    res["a_w_rkv"] = tuple(jnp.stack([big4[0][i], big4[1][i], big4[2][i]])[None] for i in range(4))
    res["a_w_out"] = tuple(a[None] for a in big4[3])
    res["mlp_w_up"] = tuple(jnp.stack([big4[4][i], big4[5][i]]) for i in range(4))
    res["mlp_w_down"] = tuple(jnp.stack([big4[6][i], big4[7][i]]) for i in range(4))
    res["b_w_q"] = tuple(a[None] for a in big4[8])
    res["b_w_o"] = tuple(a[None] for a in big4[9])
    res["w_kv"] = tuple(a.reshape(w_kv.shape) for a in res["w_kv"])

    outs = [loss, grad_x[None]]
    for i in range(4):
        outs += [res[n][i].reshape(given[n].shape) for n in _WEIGHTS]
    return tuple(outs)
```

```python
import functools
import math

import jax
import jax.numpy as jnp
from jax import lax
from jax.experimental import pallas as pl
from jax.experimental.pallas import tpu as pltpu

F32 = jnp.float32
BF16 = jnp.bfloat16

D_MODEL = 2048
N_HEADS = 32
HEAD_DIM = 64
N_KV_HEADS = 4
Q_PER_KV = 8
ATT_BLOCK = 128
WKV_CHUNK = 64
LORA_PAD = 128
D_FF = 8192
N_DEV = 8
RMS_EPS = 1e-6
GN_EPS = 64e-5
L2_EPS = 1e-12
ROPE_THETA = 10000.0
ADAM_LR, ADAM_B1, ADAM_B2, ADAM_EPS, ADAM_WD, ADAM_STEP = 0.001, 0.9, 0.999, 1e-08, 0.01, 10
MASK_VALUE = -1e30
VMEM_LIMIT_BYTES = 56 * 1024 * 1024
MESH_AXES = ("x", "y", "c")
HI = lax.Precision.HIGHEST

_NN = (((1,), (0,)), ((), ()))
_NT = (((1,), (1,)), ((), ()))
_TN = (((0,), (0,)), ((), ()))


def _params(sem):
    return pltpu.CompilerParams(dimension_semantics=sem, vmem_limit_bytes=VMEM_LIMIT_BYTES)


def _split2(a):
    hi = a.astype(BF16)
    return hi, (a - hi.astype(F32)).astype(BF16)


def _dot3(a, b, dims):
    ah, al = _split2(a)
    bh, bl = _split2(b)
    d = lambda p, q: lax.dot_general(p, q, dims, preferred_element_type=F32)
    return d(ah, bh) + (d(al, bh) + d(ah, bl))


@functools.partial(jax.custom_vjp, nondiff_argnums=(2,))
def _hdot(a, b, dims=_NN):
    return _dot3(a, b, dims)


def _hdot_fwd(a, b, dims):
    return _dot3(a, b, dims), (a, b)


def _hdot_bwd(dims, res, g):
    a, b = res
    if dims == _NN:
        return _dot3(g, b, _NT), _dot3(a, g, _TN)
    if dims == _NT:
        return _dot3(g, b, _NN), _dot3(g, a, _TN)
    assert dims == _TN
    return _dot3(b, g, _NT), _dot3(a, g, _NN)


_hdot.defvjp(_hdot_fwd, _hdot_bwd)


def _tri_parts(x):
    hi = x.astype(BF16)
    r1 = x - hi.astype(F32)
    mid = r1.astype(BF16)
    return hi, mid, (r1 - mid.astype(F32)).astype(BF16)


@jax.custom_vjp
def _mask_dot(mask, x):
    mb = mask.astype(BF16)
    p0, p1, p2 = _tri_parts(x)
    d = lambda p: lax.dot_general(mb, p, _NN, preferred_element_type=F32)
    return d(p0) + (d(p1) + d(p2))


def _mask_dot_fwd(mask, x):
    return _mask_dot(mask, x), mask


def _mask_dot_bwd(mask, g):
    mb = mask.astype(BF16)
    p0, p1, p2 = _tri_parts(g)
    d = lambda p: lax.dot_general(mb, p, _TN, preferred_element_type=F32)
    return jnp.zeros_like(mask), d(p0) + (d(p1) + d(p2))


_mask_dot.defvjp(_mask_dot_fwd, _mask_dot_bwd)


def _b16dot(a, b, dims):
    return lax.dot_general(a.astype(BF16), b.astype(BF16), dims, preferred_element_type=F32)


@jax.custom_vjp
def _bdot(a, b):
    return _b16dot(a, b, _NN)


def _bdot_fwd(a, b):
    return _b16dot(a, b, _NN), (a, b)


def _bdot_bwd(res, g):
    a, b = res
    return _b16dot(g, b, _NT), _b16dot(a, g, _TN)


_bdot.defvjp(_bdot_fwd, _bdot_bwd)


@jax.custom_vjp
def _bdot_nt(a, b):
    return _b16dot(a, b, _NT)


def _bdot_nt_fwd(a, b):
    return _b16dot(a, b, _NT), (a, b)


def _bdot_nt_bwd(res, g):
    a, b = res
    return _b16dot(g, b, _NN), _b16dot(g, a, _TN)


_bdot_nt.defvjp(_bdot_nt_fwd, _bdot_nt_bwd)


def _rms(x, gain):
    return x * lax.rsqrt(jnp.mean(x * x, axis=-1, keepdims=True) + RMS_EPS) * gain


def _vjp_of(f, n_in, diff):
    def g(*args):
        ins, cts = args[:n_in], args[n_in:]

        def fd(*d):
            full = list(ins)
            for pos, i in enumerate(diff):
                full[i] = d[pos]
            return f(*full)

        _, pull = jax.vjp(fd, *[ins[i] for i in diff])
        return pull(tuple(cts))
    return g


def _mm(a, b, *, name, ta=False, tb=False, a_pro=None, epi=None, epi_args=(), out_dtype=F32,
        tm=1024, tn=1024, tk=512, dims=None, b_spec=None, o_spec=None, o_shape=None):
    if dims is None:
        m, k = (a.shape[1], a.shape[0]) if ta else a.shape
        n = b.shape[0] if tb else b.shape[1]
    else:
        m, n, k = dims
    tm, tn, tk = min(tm, m), min(tn, n), min(tk, k)
    assert m % tm == 0 and n % tn == 0 and k % tk == 0, (name, m, n, k, tm, tn, tk)
    nk = k // tk
    ne = len(epi_args)
    cdims = (((0 if ta else 1,), (1 if tb else 0,)), ((), ()))

    def body(a_ref, b_ref, *rest):
        e_refs, o_ref, acc = rest[:ne], rest[ne], rest[ne + 1]
        kk = pl.program_id(2)

        @pl.when(kk == 0)
        def _():
            acc[...] = jnp.zeros_like(acc)

        av = a_ref[...]
        if a_pro is not None:
            av = a_pro(av.astype(F32))
        acc[...] += lax.dot_general(av.astype(BF16), b_ref[...].astype(BF16), cdims, preferred_element_type=F32)

        @pl.when(kk == nk - 1)
        def _():
            r = acc[...]
            if epi is not None:
                r = epi(r, *[e[...] for e in e_refs])
            o_ref[...] = r.astype(o_ref.dtype)

    a_spec = pl.BlockSpec((tk, tm), lambda i, j, q: (q, i)) if ta else pl.BlockSpec((tm, tk), lambda i, j, q: (i, q))
    if b_spec is None:
        b_spec = pl.BlockSpec((tn, tk), lambda i, j, q: (j, q)) if tb else pl.BlockSpec((tk, tn), lambda i, j, q: (q, j))
    if o_spec is None:
        o_spec = pl.BlockSpec((tm, tn), lambda i, j, q: (i, j))
        o_shape = (m, n)
    e_specs = [pl.BlockSpec((tm, tn), lambda i, j, q: (i, j)) for _ in epi_args]
    return pl.pallas_call(
        body, grid=(m // tm, n // tn, nk), in_specs=[a_spec, b_spec] + e_specs, out_specs=o_spec,
        out_shape=jax.ShapeDtypeStruct(o_shape, out_dtype), scratch_shapes=[pltpu.VMEM((tm, tn), F32)],
        compiler_params=_params(("parallel", "parallel", "arbitrary")), name=name,
    )(a, b, *epi_args)


def _rowwise(fn, rows, params, out_rows, out_params=(), *, tm=256, name):
    t = rows[0].shape[0]
    tm = min(tm, t)
    assert t % tm == 0
    nr, npar, nor, nop = len(rows), len(params), len(out_rows), len(out_params)

    def body(*refs):
        r, p = refs[:nr], refs[nr:nr + npar]
        o, op = refs[nr + npar:nr + npar + nor], refs[nr + npar + nor:]
        outs = fn(*[x[...] for x in r], *[x[...] for x in p])
        for ref, val in zip(o, outs[:nor]):
            ref[...] = val.astype(ref.dtype)
        if nop:
            @pl.when(pl.program_id(0) == 0)
            def _():
                for ref in op:
                    ref[...] = jnp.zeros_like(ref)

            for ref, val in zip(op, outs[nor:]):
                ref[...] += val.astype(F32)

    in_specs = [pl.BlockSpec((tm, x.shape[1]), lambda i: (i, 0)) for x in rows]
    in_specs += [pl.BlockSpec(p.shape, lambda i: (0, 0)) for p in params]
    out_specs = [pl.BlockSpec((tm, s.shape[1]), lambda i: (i, 0)) for s in out_rows]
    out_specs += [pl.BlockSpec(s.shape, lambda i: (0, 0)) for s in out_params]
    return pl.pallas_call(
        body, grid=(t // tm,), in_specs=in_specs, out_specs=out_specs, out_shape=list(out_rows) + list(out_params),
        compiler_params=_params(("arbitrary",)), name=name,
    )(*rows, *params)


def _headwise(fn, rows, params, out_rows, out_params=(), *, tt=512, name):
    nh, t, n = rows[0].shape
    tt = min(tt, t)
    assert t % tt == 0
    nr, npar, nor, nop = len(rows), len(params), len(out_rows), len(out_params)

    def body(*refs):
        r, p = refs[:nr], refs[nr:nr + npar]
        o, op = refs[nr + npar:nr + npar + nor], refs[nr + npar + nor:]
        outs = fn(*[x[...] for x in r], *[x[...] for x in p])
        for ref, val in zip(o, outs[:nor]):
            ref[...] = val.astype(ref.dtype)
        if nop:
            @pl.when(pl.program_id(1) == 0)
            def _():
                for ref in op:
                    ref[...] = jnp.zeros_like(ref)

            for ref, val in zip(op, outs[nor:]):
                ref[...] += val.astype(F32)

    row_spec = pl.BlockSpec((None, tt, n), lambda h, i: (h, i, 0))
    par_spec = pl.BlockSpec((None, 1, n), lambda h, i: (h, 0, 0))
    return pl.pallas_call(
        body, grid=(nh, t // tt), in_specs=[row_spec] * nr + [par_spec] * npar,
        out_specs=[row_spec] * nor + [par_spec] * nop, out_shape=list(out_rows) + list(out_params),
        compiler_params=_params(("parallel", "arbitrary")), name=name,
    )(*rows, *params)


def _sds(shape, dtype=F32):
    return jax.ShapeDtypeStruct(tuple(shape), dtype)


def _wkv_chunk(s0, r, lw, k, v, a, b):
    c = r.shape[0]
    ti = lax.broadcasted_iota(jnp.int32, (c, c), 0)
    si = lax.broadcasted_iota(jnp.int32, (c, c), 1)
    incl, strict = si <= ti, si < ti
    cum = _mask_dot(incl.astype(F32), lw)
    rcum = _mask_dot((si > ti).astype(F32), lw)
    tot = jnp.sum(lw, axis=0, keepdims=True)
    w_inv = jnp.exp(-cum)
    at, rt, bt, kt = a * jnp.exp(cum - lw), r * jnp.exp(cum), b * w_inv, k * w_inv
    l_ab = jnp.where(strict, _hdot(at, bt, _NT), 0.0)
    l_ak = jnp.where(strict, _hdot(at, kt, _NT), 0.0)
    t_rb = jnp.where(incl, _hdot(rt, bt, _NT), 0.0)
    t_rk = jnp.where(incl, _hdot(rt, kt, _NT), 0.0)
    u = _hdot(at, s0, _NT) + _hdot(l_ak, v, _NN)
    p = l_ab
    n_it = int(math.log2(c))
    for it in range(n_it):
        u = u + _hdot(p, u, _NN)
        if it + 1 < n_it:
            p = _hdot(p, p, _NN)
    y = _hdot(rt, s0, _NT) + _hdot(t_rb, u, _NN) + _hdot(t_rk, v, _NN)
    e = jnp.exp(rcum)
    s1 = s0 * jnp.exp(tot) + _hdot(u, b * e, _TN) + _hdot(v, k * e, _TN)
    return y, s1


WKV_HEADS_PER_STEP = 4


def _wkv_fwd(r, lw, k, v, a, b):
    nh, t, n = r.shape
    nc = t // WKV_CHUNK
    hb = WKV_HEADS_PER_STEP

    def body(r_ref, lw_ref, k_ref, v_ref, a_ref, b_ref, y_ref, s_ref, state):
        @pl.when(pl.program_id(1) == 0)
        def _():
            state[...] = jnp.zeros_like(state)

        for j in range(hb):
            s0 = state[j]
            s_ref[j, 0] = s0
            y, s1 = _wkv_chunk(s0, r_ref[j], lw_ref[j], k_ref[j], v_ref[j], a_ref[j], b_ref[j])
            y_ref[j] = y
            state[j] = s1

    blk = pl.BlockSpec((hb, WKV_CHUNK, n), lambda h, c: (h, c, 0))
    sblk = pl.BlockSpec((hb, 1, n, n), lambda h, c: (h, c, 0, 0))
    return pl.pallas_call(
        body, grid=(nh // hb, nc), in_specs=[blk] * 6, out_specs=[blk, sblk],
        out_shape=[_sds((nh, t, n)), _sds((nh, nc, n, n))], scratch_shapes=[pltpu.VMEM((hb, n, n), F32)],
        compiler_params=_params(("parallel", "arbitrary")), name="wkv_fwd",
    )(r, lw, k, v, a, b)


def _wkv_bwd(r, lw, k, v, a, b, states, dy):
    nh, t, n = r.shape
    nc = t // WKV_CHUNK
    hb = WKV_HEADS_PER_STEP

    def body(r_ref, lw_ref, k_ref, v_ref, a_ref, b_ref, s_ref, dy_ref,
             dr_ref, dlw_ref, dk_ref, dv_ref, da_ref, db_ref, dstate):
        @pl.when(pl.program_id(1) == 0)
        def _():
            dstate[...] = jnp.zeros_like(dstate)

        for j in range(hb):
            _, pull = jax.vjp(_wkv_chunk, s_ref[j, 0], r_ref[j], lw_ref[j], k_ref[j], v_ref[j], a_ref[j], b_ref[j])
            ds0, dr, dlw, dk, dv, da, db = pull((dy_ref[j], dstate[j]))
            dr_ref[j], dlw_ref[j], dk_ref[j] = dr, dlw, dk
            dv_ref[j], da_ref[j], db_ref[j] = dv, da, db
            dstate[j] = ds0

    blk = pl.BlockSpec((hb, WKV_CHUNK, n), lambda h, c: (h, nc - 1 - c, 0))
    sblk = pl.BlockSpec((hb, 1, n, n), lambda h, c: (h, nc - 1 - c, 0, 0))
    return pl.pallas_call(
        body, grid=(nh // hb, nc), in_specs=[blk] * 6 + [sblk, blk], out_specs=[blk] * 6,
        out_shape=[_sds((nh, t, n))] * 6, scratch_shapes=[pltpu.VMEM((hb, n, n), F32)],
        compiler_params=_params(("parallel", "arbitrary")), name="wkv_bwd",
    )(r, lw, k, v, a, b, states, dy)


def _wkv_prep(k, wl, al, w0, a0, k_k, k_a):
    z = -(w0 + wl)
    softplus = jnp.maximum(z, 0.0) + jnp.log1p(jnp.exp(-jnp.abs(z)))
    lw = -jnp.exp(-softplus - 0.5)
    asig = jax.nn.sigmoid(a0 + al)
    kk = k * k_k
    kk = kk / jnp.maximum(jnp.sqrt(jnp.sum(kk * kk, axis=-1, keepdims=True)), L2_EPS)
    kmod = k * (1.0 + (asig - 1.0) * k_a)
    return lw, kmod, -kk, kk * asig


def _wkv_post(y, r, kmod, v, ln_w, ln_b, r_k):
    mu = jnp.mean(y, axis=-1, keepdims=True)
    var = jnp.mean(jnp.square(y - mu), axis=-1, keepdims=True)
    yn = (y - mu) * lax.rsqrt(var + GN_EPS)
    yn = yn * ln_w + ln_b
    return (yn + jnp.sum(r * kmod * r_k, axis=-1, keepdims=True) * v,)


def _attn_block(nonzero_block, qs, kc, kp, vc, vp, cos_c, sin_c, cos_p, sin_p, q_gain, k_gain, sinks):
    ri = lax.broadcasted_iota(jnp.int32, (HEAD_DIM, HEAD_DIM), 0)
    ci = lax.broadcasted_iota(jnp.int32, (HEAD_DIM, HEAD_DIM), 1)
    half = HEAD_DIM // 2
    rot = jnp.where(ri == ci + half, -1.0, 0.0) + jnp.where(ri + half == ci, 1.0, 0.0)

    def rope(x, cos, sin):
        return x * cos + _hdot(x, rot, _NN) * sin

    kcr = rope(_rms(kc, k_gain), cos_c, sin_c)
    kpr = rope(_rms(kp, k_gain), cos_p, sin_p)
    qi = lax.broadcasted_iota(jnp.int32, (ATT_BLOCK, ATT_BLOCK), 0)
    ki = lax.broadcasted_iota(jnp.int32, (ATT_BLOCK, ATT_BLOCK), 1)
    mask_c = ki <= qi
    mask_p = jnp.logical_and(ki > qi, nonzero_block)
    lane0 = (lax.broadcasted_iota(jnp.int32, (1, 128), 1) == 0).astype(F32)
    outs = []
    for g in range(Q_PER_KV):
        q = rope(_rms(qs[g], q_gain), cos_c, sin_c)
        sc = jnp.where(mask_c, _bdot_nt(q, kcr) * (HEAD_DIM ** -0.5), MASK_VALUE)
        sp = jnp.where(mask_p, _bdot_nt(q, kpr) * (HEAD_DIM ** -0.5), MASK_VALUE)
        sk = jnp.sum(sinks[g] * lane0, axis=1, keepdims=True)
        mx = jnp.maximum(jnp.maximum(jnp.max(sc, axis=1, keepdims=True), jnp.max(sp, axis=1, keepdims=True)), sk)
        mx = lax.stop_gradient(mx)
        ec, ep = jnp.exp(sc - mx), jnp.exp(sp - mx)
        den = jnp.sum(ec, axis=1, keepdims=True) + jnp.sum(ep, axis=1, keepdims=True) + jnp.exp(sk - mx)
        outs.append(_bdot(ec / den, vc) + _bdot(ep / den, vp))
    return tuple(outs)


def _attn_specs(t):
    nb = t // ATT_BLOCK
    prev = lambda n: jnp.maximum(n - 1, 0)
    q_spec = pl.BlockSpec((Q_PER_KV, ATT_BLOCK, HEAD_DIM), lambda g, n: (g, n, 0))
    kv_c = pl.BlockSpec((None, ATT_BLOCK, HEAD_DIM), lambda g, n: (g, n, 0))
    kv_p = pl.BlockSpec((None, ATT_BLOCK, HEAD_DIM), lambda g, n: (g, prev(n), 0))
    tab_c = pl.BlockSpec((ATT_BLOCK, HEAD_DIM), lambda g, n: (n, 0))
    tab_p = pl.BlockSpec((ATT_BLOCK, HEAD_DIM), lambda g, n: (prev(n), 0))
    gain = pl.BlockSpec((1, HEAD_DIM), lambda g, n: (0, 0))
    sink = pl.BlockSpec((Q_PER_KV, 1, 128), lambda g, n: (g, 0, 0))
    return nb, q_spec, kv_c, kv_p, tab_c, tab_p, gain, sink


def _attn_fwd(q, k, v, cos, sin, q_gain, k_gain, sinks):
    t = q.shape[1]
    nb, q_spec, kv_c, kv_p, tab_c, tab_p, gain, sink = _attn_specs(t)

    def body(q_ref, kc, kp, vc, vp, cc, sc, cp, sp, qg, kg, sk, o_ref):
        outs = _attn_block(pl.program_id(1) > 0, [q_ref[g] for g in range(Q_PER_KV)], kc[...], kp[...], vc[...], vp[...],
                           cc[...], sc[...], cp[...], sp[...], qg[...], kg[...], [sk[g] for g in range(Q_PER_KV)])
        for g in range(Q_PER_KV):
            o_ref[g] = outs[g]

    return pl.pallas_call(
        body, grid=(N_KV_HEADS, nb), in_specs=[q_spec, kv_c, kv_p, kv_c, kv_p, tab_c, tab_c, tab_p, tab_p, gain, gain, sink],
        out_specs=q_spec, out_shape=_sds(q.shape), compiler_params=_params(("parallel", "arbitrary")), name="attn_fwd",
    )(q, k, k, v, v, cos, sin, cos, sin, q_gain, k_gain, sinks)


def _attn_bwd(q, k, v, cos, sin, q_gain, k_gain, sinks, do):
    t = q.shape[1]
    nb, q_spec, kv_c, kv_p, tab_c, tab_p, gain, sink = _attn_specs(t)
    nq = Q_PER_KV

    def body(q_ref, kc, kp, vc, vp, cc, sc, cp, sp, qg, kg, sk, do_ref,
             dq_ref, dkc_ref, dkp_ref, dvc_ref, dvp_ref, dqg_ref, dkg_ref, dsk_ref):
        nonzero = pl.program_id(1) > 0
        tabs = (cc[...], sc[...], cp[...], sp[...])

        def f(qs, kcv, kpv, vcv, vpv, qgv, kgv, sks):
            return _attn_block(nonzero, qs, kcv, kpv, vcv, vpv, *tabs, qgv, kgv, sks)

        _, pull = jax.vjp(f, [q_ref[g] for g in range(nq)], kc[...], kp[...], vc[...], vp[...], qg[...], kg[...],
                          [sk[g] for g in range(nq)])
        dqs, dkc, dkp, dvc, dvp, dqg, dkg, dsks = pull(tuple(do_ref[g] for g in range(nq)))
        for g in range(nq):
            dq_ref[g] = dqs[g]
        dkc_ref[...], dkp_ref[...], dvc_ref[...], dvp_ref[...] = dkc, dkp, dvc, dvp
        first = jnp.logical_and(pl.program_id(0) == 0, pl.program_id(1) == 0)

        @pl.when(first)
        def _():
            dqg_ref[...] = jnp.zeros_like(dqg_ref)
            dkg_ref[...] = jnp.zeros_like(dkg_ref)

        @pl.when(pl.program_id(1) == 0)
        def _():
            dsk_ref[...] = jnp.zeros_like(dsk_ref)

        dqg_ref[...] += dqg
        dkg_ref[...] += dkg
        for g in range(nq):
            dsk_ref[g] += dsks[g]

    kv_shape = _sds(k.shape)
    return pl.pallas_call(
        body, grid=(N_KV_HEADS, nb),
        in_specs=[q_spec, kv_c, kv_p, kv_c, kv_p, tab_c, tab_c, tab_p, tab_p, gain, gain, sink, q_spec],
        out_specs=[q_spec, kv_c, kv_c, kv_c, kv_c, gain, gain, sink],
        out_shape=[_sds(q.shape), kv_shape, kv_shape, kv_shape, kv_shape, _sds((1, HEAD_DIM)), _sds((1, HEAD_DIM)), _sds(sinks.shape)],
        compiler_params=_params(("arbitrary", "arbitrary")), name="attn_bwd",
    )(q, k, k, v, v, cos, sin, cos, sin, q_gain, k_gain, sinks, do)


def _time_shift_lerps(x, xs, gain, *mix):
    xn, xsn = _rms(x, gain), _rms(xs, gain)
    xx = xsn - xn
    return tuple(xn + xx * m for m in mix)


def _residual_norm(h, delta, gain):
    hn = h + delta
    return hn, _rms(hn, gain)


def _residual_norm2(h, delta, gain_a, gain_b):
    hn = h + delta
    return hn, _rms(hn, gain_a), _rms(hn, gain_b)


def _relu2(u):
    return jnp.square(jnp.maximum(u, 0.0))


def _sigmoid(z):
    return jax.nn.sigmoid(z)


def _to_heads(x2d, nh):
    t = x2d.shape[0]
    return x2d.reshape(t, nh, HEAD_DIM).transpose(1, 0, 2)


def _from_heads(xh):
    nh, t, n = xh.shape
    return xh.transpose(1, 0, 2).reshape(t, nh * n)


def _shift_down(x):
    return jnp.pad(x[:-1], ((1, 0), (0, 0)))


def _shift_up(x):
    return jnp.pad(x[1:], ((0, 1), (0, 0)))


def _rope_tables(t):
    half = HEAD_DIM // 2
    inv_freq = jnp.power(ROPE_THETA, -jnp.arange(half, dtype=F32) / half)
    ang = jnp.arange(t, dtype=jnp.int32).astype(F32)[:, None] * inv_freq[None, :]
    cos, sin = jnp.cos(ang), jnp.sin(ang)
    return jnp.concatenate([cos, cos], axis=1), jnp.concatenate([sin, sin], axis=1)


def _mlp_fwd(hn, w_up, w_down, layer, up_dev_major):
    t = hn.shape[0]
    if up_dev_major:
        cw = w_up.shape[2]
        u = _mm(hn, w_up, name=f"mlp{layer}_up", dims=(t, D_FF, D_MODEL), tn=cw,
                b_spec=pl.BlockSpec((None, 512, cw), lambda i, j, q: (j, q, 0)))
    else:
        u = _mm(hn, w_up, name=f"mlp{layer}_up")
    out = _mm(u, w_down, a_pro=_relu2, name=f"mlp{layer}_down")
    return u, out


def _mlp_bwd(hn, u, dh, w_up, w_down, layer, up_dev_major):
    t = hn.shape[0]
    du = _mm(dh, w_down, tb=True, epi=lambda r, uu: r * (2.0 * jnp.maximum(uu, 0.0)), epi_args=(u,), out_dtype=BF16,
             name=f"mlp{layer}_du")
    d_down = _mm(u, dh, ta=True, a_pro=_relu2, name=f"mlp{layer}_ddown")
    if up_dev_major:
        cw = w_up.shape[2]
        d_up = _mm(hn, du, ta=True, name=f"mlp{layer}_dup", dims=(D_MODEL, D_FF, t), tn=cw,
                   o_spec=pl.BlockSpec((None, 1024, cw), lambda i, j, q: (j, i, 0)), o_shape=(N_DEV, D_MODEL, cw))
        dhn = _mm(du, w_up, tb=True, name=f"mlp{layer}_dhn", dims=(t, D_MODEL, D_FF), tk=cw,
                  b_spec=pl.BlockSpec((None, 1024, cw), lambda i, j, q: (q, j, 0)))
    else:
        d_up = _mm(hn, du, ta=True, name=f"mlp{layer}_dup")
        dhn = _mm(du, w_up, tb=True, name=f"mlp{layer}_dhn")
    return dhn, d_up, d_down


def _local_step(x, target, w, up_dev_major=True):
    t = x.shape[0]
    g = {}
    row = lambda: _sds((t, D_MODEL))
    rowb = lambda: _sds((t, D_MODEL), BF16)
    vec = lambda: _sds((1, D_MODEL))

    xs = _shift_down(x)
    mix = [w["a_mix"][i:i + 1] for i in range(6)]
    xr, xk, xv, xw, xa, xg = _rowwise(_time_shift_lerps, [x, xs], [w["a_norm"]] + mix, [rowb()] * 6, tm=256, name="tmix_lerp")
    r = _mm(xr, w["a_w_r"], name="tmix_r")
    k = _mm(xk, w["a_w_k"], name="tmix_k")
    v = _mm(xv, w["a_w_v"], name="tmix_v")
    lw1 = _mm(xw, w["a_w1"], name="tmix_w1")
    wl = _mm(lw1, w["a_w2"], a_pro=jnp.tanh, name="tmix_w2")
    la1 = _mm(xa, w["a_a1"], name="tmix_a1")
    al = _mm(la1, w["a_a2"], name="tmix_a2")
    lg1 = _mm(xg, w["a_g1"], name="tmix_g1")
    gate = _mm(lg1, w["a_g2"], a_pro=_sigmoid, name="tmix_g2")

    hv = lambda name: w[name].reshape(N_HEADS, 1, HEAD_DIM)
    rh, kh, vh, wlh, alh = [_to_heads(z, N_HEADS) for z in (r, k, v, wl, al)]
    head = lambda: _sds((N_HEADS, t, HEAD_DIM))
    prep_params = [hv("a_w0"), hv("a_a0"), hv("a_k_k"), hv("a_k_a")]
    lwh, kmh, ah, bh = _headwise(_wkv_prep, [kh, wlh, alh], prep_params, [head()] * 4, name="wkv_prep")
    yh, states = _wkv_fwd(rh, lwh, kmh, vh, ah, bh)
    post_params = [hv("a_ln_x_w"), hv("a_ln_x_b"), hv("a_r_k")]
    (y2h,) = _headwise(_wkv_post, [yh, rh, kmh, vh], post_params, [head()], name="wkv_post")
    y2 = _from_heads(y2h)
    (yg,) = _rowwise(lambda a, b: (a * b,), [y2, gate], [], [rowb()], name="tmix_gate")
    att = _mm(yg, w["a_w_out"], name="tmix_out")

    h1, hn0 = _rowwise(_residual_norm, [x, att], [w["mlp_norm0"]], [row(), rowb()], name="res_norm0")
    u0, m0 = _mlp_fwd(hn0, w["mlp_w_up0"], w["mlp_w_down0"], 0, up_dev_major)

    h2, kvn, qn = _rowwise(_residual_norm2, [h1, m0], [w["kv_norm"], w["b_norm"]], [row(), rowb(), rowb()], name="res_norm_kvq")
    kv = _mm(kvn, w["w_kv"], name="kv_proj")
    q = _mm(qn, w["b_w_q"], name="q_proj")
    half_kv = N_KV_HEADS * HEAD_DIM
    ksh, vsh = _to_heads(kv[:, :half_kv], N_KV_HEADS), _to_heads(kv[:, half_kv:], N_KV_HEADS)
    qh = _to_heads(q, N_HEADS)
    cos, sin = _rope_tables(t)
    sinks = jnp.broadcast_to(w["b_sinks"].reshape(N_HEADS, 1, 1), (N_HEADS, 1, 128))
    oh = _attn_fwd(qh, ksh, vsh, cos, sin, w["b_q_norm"], w["k_norm"], sinks)
    o = _from_heads(oh).astype(BF16)
    att2 = _mm(o, w["b_w_o"], name="attn_out")

    h3, hn1 = _rowwise(_residual_norm, [h2, att2], [w["mlp_norm1"]], [row(), rowb()], name="res_norm1")
    u1, m1 = _mlp_fwd(hn1, w["mlp_w_up1"], w["mlp_w_down1"], 1, up_dev_major)

    def loss_fn(h, m, tg):
        diff = (h + m) - tg
        part = 0.5 * jnp.sum(jnp.mean(jnp.square(diff), axis=-1, keepdims=True), axis=0, keepdims=True)
        return diff * (1.0 / D_MODEL), jnp.broadcast_to(part, (1, 128))

    dh4, loss = _rowwise(loss_fn, [h3, m1, target], [], [row()], [_sds((1, 128))], name="loss")

    def res_norm_bwd(h, dnext, dhn, gain):
        dh, dgain = _vjp_of(lambda hh, gg: (_rms(hh, gg),), 2, (0, 1))(h, gain, dhn)
        return dnext + dh, dgain

    dhn1, g["mlp_w_up1"], g["mlp_w_down1"] = _mlp_bwd(hn1, u1, dh4, w["mlp_w_up1"], w["mlp_w_down1"], 1, up_dev_major)
    dh3, g["mlp_norm1"] = _rowwise(res_norm_bwd, [h3, dh4, dhn1], [w["mlp_norm1"]], [row()], [vec()], name="res_norm1_bwd")

    g["b_w_o"] = _mm(o, dh3, ta=True, name="attn_out_dw")
    do = _mm(dh3, w["b_w_o"], tb=True, name="attn_out_dx")
    dqh, dkc, dkp, dvc, dvp, g["b_q_norm"], g["k_norm"], dsinks = _attn_bwd(
        qh, ksh, vsh, cos, sin, w["b_q_norm"], w["k_norm"], sinks, _to_heads(do, N_HEADS))
    g["b_sinks"] = dsinks[:, 0, 0].reshape(1, N_HEADS)
    dq = _from_heads(dqh)
    g["b_w_q"] = _mm(qn, dq, ta=True, name="q_proj_dw")
    dqn = _mm(dq, w["b_w_q"], tb=True, name="q_proj_dx")
    shift_blk = lambda z: jnp.pad(z[:, ATT_BLOCK:], ((0, 0), (0, ATT_BLOCK), (0, 0)))
    kv_parts = [_from_heads(z) for z in (dkc, shift_blk(dkp), dvc, shift_blk(dvp))]
    (dkv,) = _rowwise(lambda a, b, c, d: (jnp.concatenate([a + b, c + d], axis=1),), kv_parts, [],
                      [_sds((t, 2 * half_kv), BF16)], name="kv_grad_sum")
    g["w_kv"] = _mm(kvn, dkv, ta=True, name="kv_proj_dw")
    dkvn = _mm(dkv, w["w_kv"], tb=True, name="kv_proj_dx")

    def res_norm2_bwd(h, dnext, dna, dnb, gain_a, gain_b):
        dha, dga = _vjp_of(lambda hh, gg: (_rms(hh, gg),), 2, (0, 1))(h, gain_a, dna)
        dhb, dgb = _vjp_of(lambda hh, gg: (_rms(hh, gg),), 2, (0, 1))(h, gain_b, dnb)
        return dnext + dha + dhb, dga, dgb

    dh2, g["kv_norm"], g["b_norm"] = _rowwise(res_norm2_bwd, [h2, dh3, dkvn, dqn], [w["kv_norm"], w["b_norm"]],
                                              [row()], [vec(), vec()], name="res_norm_kvq_bwd")

    dhn0, g["mlp_w_up0"], g["mlp_w_down0"] = _mlp_bwd(hn0, u0, dh2, w["mlp_w_up0"], w["mlp_w_down0"], 0, up_dev_major)
    dh1, g["mlp_norm0"] = _rowwise(res_norm_bwd, [h1, dh2, dhn0], [w["mlp_norm0"]], [row()], [vec()], name="res_norm0_bwd")

    g["a_w_out"] = _mm(yg, dh1, ta=True, name="tmix_out_dw")
    dyg = _mm(dh1, w["a_w_out"], tb=True, name="tmix_out_dx")
    dy2, dgate = _rowwise(lambda d, a, b: (d * b, d * a), [dyg, y2, gate], [], [row(), rowb()], name="tmix_gate_bwd")
    g["a_g2"] = _mm(lg1, dgate, ta=True, a_pro=_sigmoid, name="tmix_g2_dw")

    def dsigmoid(rr, z):
        s = jax.nn.sigmoid(z)
        return rr * s * (1.0 - s)

    dlg1 = _mm(dgate, w["a_g2"], tb=True, epi=dsigmoid, epi_args=(lg1,), out_dtype=BF16, name="tmix_g2_dx")
    g["a_g1"] = _mm(xg, dlg1, ta=True, name="tmix_g1_dw")
    dxg = _mm(dlg1, w["a_g1"], tb=True, name="tmix_g1_dx")

    post_vjp = _vjp_of(_wkv_post, 7, (0, 1, 2, 3, 4, 5, 6))

    def post_bwd(y_, r_, km_, v_, dy_, lnw_, lnb_, rk_):
        return post_vjp(y_, r_, km_, v_, lnw_, lnb_, rk_, dy_)

    dyh, drh1, dkmh1, dvh1, dlnw, dlnb, drk = _headwise(
        post_bwd, [yh, rh, kmh, vh, _to_heads(dy2, N_HEADS)], post_params, [head()] * 4, [_sds((N_HEADS, 1, HEAD_DIM))] * 3,
        name="wkv_post_bwd")
    drh2, dlwh, dkmh2, dvh2, dah, dbh = _wkv_bwd(rh, lwh, kmh, vh, ah, bh, states, dyh)

    def prep_bwd(kk_, wl_, al_, dlw_, dkm1, dkm2, da_, db_, dr1, dr2, dv1, dv2, w0_, a0_, kk0_, ka_):
        grads = _vjp_of(_wkv_prep, 7, (0, 1, 2, 3, 4, 5, 6))(kk_, wl_, al_, w0_, a0_, kk0_, ka_, dlw_, dkm1 + dkm2, da_, db_)
        dk_, dwl_, dal_, dw0_, da0_, dkk_, dka_ = grads
        return dk_, dwl_, dal_, dr1 + dr2, dv1 + dv2, dw0_, da0_, dkk_, dka_

    dkh, dwlh, dalh, drh, dvh, dw0, da0, dk_k, dk_a = _headwise(
        prep_bwd, [kh, wlh, alh, dlwh, dkmh1, dkmh2, dah, dbh, drh1, drh2, dvh1, dvh2], prep_params,
        [_sds((N_HEADS, t, HEAD_DIM), BF16)] * 5, [_sds((N_HEADS, 1, HEAD_DIM))] * 4, tt=256, name="wkv_prep_bwd")
    flat = lambda z: z.reshape(1, D_MODEL)
    g["a_w0"], g["a_a0"], g["a_k_k"], g["a_k_a"] = flat(dw0), flat(da0), flat(dk_k), flat(dk_a)
    g["a_ln_x_w"], g["a_ln_x_b"], g["a_r_k"] = flat(dlnw), flat(dlnb), flat(drk)
    dr, dk, dv, dwl, dal = [_from_heads(z) for z in (drh, dkh, dvh, dwlh, dalh)]

    g["a_w_r"] = _mm(xr, dr, ta=True, name="tmix_r_dw")
    g["a_w_k"] = _mm(xk, dk, ta=True, name="tmix_k_dw")
    g["a_w_v"] = _mm(xv, dv, ta=True, name="tmix_v_dw")
    dxr = _mm(dr, w["a_w_r"], tb=True, name="tmix_r_dx")
    dxk = _mm(dk, w["a_w_k"], tb=True, name="tmix_k_dx")
    dxv = _mm(dv, w["a_w_v"], tb=True, name="tmix_v_dx")
    g["a_w2"] = _mm(lw1, dwl, ta=True, a_pro=jnp.tanh, name="tmix_w2_dw")

    def dtanh(rr, z):
        th = jnp.tanh(z)
        return rr * (1.0 - th * th)

    dlw1 = _mm(dwl, w["a_w2"], tb=True, epi=dtanh, epi_args=(lw1,), out_dtype=BF16, name="tmix_w2_dx")
    g["a_w1"] = _mm(xw, dlw1, ta=True, name="tmix_w1_dw")
    dxw = _mm(dlw1, w["a_w1"], tb=True, name="tmix_w1_dx")
    g["a_a2"] = _mm(la1, dal, ta=True, name="tmix_a2_dw")
    dla1 = _mm(dal, w["a_a2"], tb=True, out_dtype=BF16, name="tmix_a2_dx")
    g["a_a1"] = _mm(xa, dla1, ta=True, name="tmix_a1_dw")
    dxa = _mm(dla1, w["a_a1"], tb=True, name="tmix_a1_dx")

    lerp_bwd = _vjp_of(_time_shift_lerps, 9, tuple(range(9)))

    def lerp_bwd_rows(x_, xs_, d0, d1, d2, d3, d4, d5, gain, *mx):
        return lerp_bwd(x_, xs_, gain, *mx, d0, d1, d2, d3, d4, d5)

    outs = _rowwise(lerp_bwd_rows, [x, xs, dxr, dxk, dxv, dxw, dxa, dxg], [w["a_norm"]] + mix, [row(), row()], [vec()] * 7,
                    tm=128, name="tmix_lerp_bwd")
    dx_a, dxs, g["a_norm"] = outs[0], outs[1], outs[2]
    g["a_mix"] = jnp.concatenate(outs[3:9], axis=0)
    (grad_x,) = _rowwise(lambda a, b, c: (a + b + c,), [dh1, dx_a, _shift_up(dxs)], [], [row()], name="grad_x_sum")
    return loss, grad_x, g


_ANY = pl.BlockSpec(memory_space=pl.ANY)
_MESH_ID = pl.DeviceIdType.MESH
N_PEERS = N_DEV - 1


def _linear(pos):
    return 4 * pos[0] + 2 * pos[1] + pos[2]


def _all_gather(shards, name):
    n = len(shards)

    def body(*refs):
        ins, outs = refs[:n], refs[n:2 * n]
        send_sems, recv_sems, local_sems = refs[2 * n:]
        x, y, c = lax.axis_index("x"), lax.axis_index("y"), lax.axis_index("c")
        me, sibling = (x, y, c), (x, y, 1 - c)
        chips = [(1 - x, y), (x, 1 - y), (1 - x, 1 - y)]

        def copy(a, k, block, to, src=None):
            dst = outs[a].at[_linear(block)]
            return pltpu.make_async_remote_copy(
                src_ref=dst if src is None else src, dst_ref=dst, send_sem=send_sems.at[a * N_PEERS + k],
                recv_sem=recv_sems.at[a * N_PEERS + k], device_id=to, device_id_type=_MESH_ID)

        mine = [pltpu.make_async_copy(ins[a], outs[a].at[_linear(me)], local_sems.at[a]) for a in range(n)]
        for cp in mine:
            cp.start()
        first = []
        for a in range(n):
            first.append(copy(a, 0, me, sibling, src=ins[a]))
            first += [copy(a, 1 + j, me, (*chip, c), src=ins[a]) for j, chip in enumerate(chips)]
        for cp in first:
            cp.start()
        passed = []
        for j, chip in enumerate(chips):
            for a in range(n):
                copy(a, 1 + j, (*chip, c), me).wait_recv()
                fwd = copy(a, 4 + j, (*chip, c), sibling)
                fwd.start()
                passed.append(fwd)
        for a in range(n):
            copy(a, 0, sibling, me).wait_recv()
            for j, chip in enumerate(chips):
                copy(a, 4 + j, (*chip, 1 - c), me).wait_recv()
        for cp in first + passed:
            cp.wait_send()
        for cp in mine:
            cp.wait()

    return pl.pallas_call(
        body, out_shape=[_sds((N_DEV,) + s.shape, s.dtype) for s in shards], in_specs=[_ANY] * n, out_specs=[_ANY] * n,
        scratch_shapes=[pltpu.SemaphoreType.DMA((n * N_PEERS,)), pltpu.SemaphoreType.DMA((n * N_PEERS,)),
                        pltpu.SemaphoreType.DMA((n,))],
        name=name,
    )(*shards)


N_CHIPS = 4


def _exchange_with_sibling(parts, name):
    n = len(parts)

    def body(*refs):
        ins, outs = refs[:n], refs[n:2 * n]
        send_sems, recv_sems = refs[2 * n:]
        x, y, c = lax.axis_index("x"), lax.axis_index("y"), lax.axis_index("c")
        copies = []
        for a in range(n):
            for q in range(N_CHIPS):
                cp = pltpu.make_async_remote_copy(
                    src_ref=ins[a].at[2 * q + (1 - c)], dst_ref=outs[a].at[q], send_sem=send_sems.at[a * N_CHIPS + q],
                    recv_sem=recv_sems.at[a * N_CHIPS + q], device_id=(x, y, 1 - c), device_id_type=_MESH_ID)
                cp.start()
                copies.append(cp)
        for cp in copies:
            cp.wait()

    return pl.pallas_call(
        body, out_shape=[_sds((N_CHIPS,) + p.shape[1:], p.dtype) for p in parts], in_specs=[_ANY] * n, out_specs=[_ANY] * n,
        scratch_shapes=[pltpu.SemaphoreType.DMA((n * N_CHIPS,)), pltpu.SemaphoreType.DMA((n * N_CHIPS,))],
        name=name,
    )(*parts)


def _pair_sum(part, recv, core, out_dtype, name):
    _, r, cdim = recv.shape
    tr = max(8, min(r, (1 << 18) // cdim))
    assert r % tr == 0, (name, r, tr)

    def body(core_ref, p_ref, r_ref, o_ref):
        o_ref[...] = (p_ref[...] + r_ref[...]).astype(o_ref.dtype)

    grid_spec = pltpu.PrefetchScalarGridSpec(
        num_scalar_prefetch=1, grid=(N_CHIPS, r // tr),
        in_specs=[pl.BlockSpec((None, None, tr, cdim), lambda q, i, core_ref: (q, core_ref[0], i, 0)),
                  pl.BlockSpec((None, tr, cdim), lambda q, i, core_ref: (q, i, 0))],
        out_specs=pl.BlockSpec((None, tr, cdim), lambda q, i, core_ref: (q, i, 0)))
    return pl.pallas_call(
        body, grid_spec=grid_spec, out_shape=_sds((N_CHIPS, r, cdim), out_dtype),
        compiler_params=_params(("parallel", "parallel")), name=name,
    )(core, part.reshape(N_CHIPS, 2, r, cdim), recv)


def _exchange_between_chips(parts, name):
    n = len(parts)
    n_other = N_CHIPS - 1

    def body(*refs):
        ins, outs = refs[:n], refs[n:2 * n]
        send_sems, recv_sems, local_sems = refs[2 * n:]
        x, y, c = lax.axis_index("x"), lax.axis_index("y"), lax.axis_index("c")
        my_chip = 2 * x + y
        mine = [pltpu.make_async_copy(ins[a].at[my_chip], outs[a].at[my_chip], local_sems.at[a]) for a in range(n)]
        for cp in mine:
            cp.start()
        copies = []
        for j, (fx, fy) in enumerate([(1, 0), (0, 1), (1, 1)]):
            px, py = (1 - x if fx else x), (1 - y if fy else y)
            for a in range(n):
                cp = pltpu.make_async_remote_copy(
                    src_ref=ins[a].at[2 * px + py], dst_ref=outs[a].at[my_chip], send_sem=send_sems.at[a * n_other + j],
                    recv_sem=recv_sems.at[a * n_other + j], device_id=(px, py, c), device_id_type=_MESH_ID)
                cp.start()
                copies.append(cp)
        for cp in copies:
            cp.wait()
        for cp in mine:
            cp.wait()

    return pl.pallas_call(
        body, out_shape=[_sds(p.shape, p.dtype) for p in parts], in_specs=[_ANY] * n, out_specs=[_ANY] * n,
        scratch_shapes=[pltpu.SemaphoreType.DMA((n * n_other,)), pltpu.SemaphoreType.DMA((n * n_other,)),
                        pltpu.SemaphoreType.DMA((n,))],
        name=name,
    )(*parts)


def _reduce_scatter(parts, names):
    core = lax.axis_index("c").astype(jnp.int32).reshape(1)
    from_sibling = _exchange_with_sibling(parts, name="scatter_grads_sibling")
    chip_sums = [_pair_sum(p, r, core, F32 if nm.startswith("pack") else BF16, name="pair_sum_" + nm)
                 for p, r, nm in zip(parts, from_sibling, names)]
    return _exchange_between_chips(chip_sums, name="scatter_grads_chips")


def _adamw(w, m, v, slots, name):
    r, c = w.shape
    ns = slots.shape[0]
    tr = max(8, min(r, (1 << 18) // c))
    assert r % tr == 0, (name, r, tr)

    def body(w_ref, m_ref, v_ref, g_ref, g_out, d_out, m_out, v_out):
        g = g_ref[0].astype(F32)
        for s in range(1, ns):
            g = g + g_ref[s].astype(F32)
        m_new = ADAM_B1 * m_ref[...] + (1.0 - ADAM_B1) * g
        v_new = ADAM_B2 * v_ref[...] + (1.0 - ADAM_B2) * jnp.square(g)
        m_hat = m_new / (1.0 - ADAM_B1 ** ADAM_STEP)
        v_hat = v_new / (1.0 - ADAM_B2 ** ADAM_STEP)
        d_out[...] = -ADAM_LR * (m_hat / (jnp.sqrt(v_hat) + ADAM_EPS) + ADAM_WD * w_ref[...])
        g_out[...], m_out[...], v_out[...] = g, m_new, v_new

    spec = pl.BlockSpec((tr, c), lambda i: (i, 0))
    return pl.pallas_call(
        body, grid=(r // tr,), in_specs=[spec, spec, spec, pl.BlockSpec((ns, tr, c), lambda i: (0, i, 0))],
        out_specs=[spec] * 4, out_shape=[_sds((r, c))] * 4, compiler_params=_params(("parallel",)), name=name,
    )(w, m, v, slots)


_COL_VECTORS = ("a_norm", "a_mix", "a_w0", "a_a0", "a_k_k", "a_k_a", "a_ln_x_w", "a_ln_x_b")
_COL_VEC_ROWS = 16
_COL_ROWS = _COL_VEC_ROWS + 2 * LORA_PAD + 256
_ROW_COLS = 2 * LORA_PAD + 256 + 512
_REPL_ROWS = 8


def _pad_to(a, size, axis):
    widths = [(0, 0)] * a.ndim
    widths[axis] = (0, size - a.shape[axis])
    return jnp.pad(a, widths)


def _pack_cols(p):
    width = p["a_norm"].shape[-1]
    vecs = jnp.concatenate([p[n].reshape(-1, width) for n in _COL_VECTORS], axis=0)
    return jnp.concatenate([_pad_to(vecs, _COL_VEC_ROWS, 0), _pad_to(p["a_w2"].reshape(-1, width), LORA_PAD, 0),
                            _pad_to(p["a_a2"].reshape(-1, width), LORA_PAD, 0), p["a_g2"].reshape(-1, width)], axis=0)


def _unpack_cols(a, lead):
    width = a.shape[-1]
    out, row = {}, 0
    for n in _COL_VECTORS:
        k = 6 if n == "a_mix" else 1
        out[n] = a[row:row + k].reshape(lead + ((6, width) if n == "a_mix" else (width,)))
        row += k
    base = _COL_VEC_ROWS
    out["a_w2"] = a[base:base + 96].reshape(lead + (96, width))
    out["a_a2"] = a[base + LORA_PAD:base + LORA_PAD + 96].reshape(lead + (96, width))
    out["a_g2"] = a[base + 2 * LORA_PAD:].reshape(lead + (256, width))
    return out


def _pack_rows(p):
    rows = p["w_kv"].shape[0]
    return jnp.concatenate([_pad_to(p["a_w1"].reshape(rows, -1), LORA_PAD, 1), _pad_to(p["a_a1"].reshape(rows, -1), LORA_PAD, 1),
                            p["a_g1"].reshape(rows, -1), p["w_kv"]], axis=1)


def _unpack_rows(a, lead):
    rows = a.shape[0]
    return {"a_w1": a[:, :96].reshape(lead + (rows, 96)), "a_a1": a[:, LORA_PAD:LORA_PAD + 96].reshape(lead + (rows, 96)),
            "a_g1": a[:, 2 * LORA_PAD:2 * LORA_PAD + 256].reshape(lead + (rows, 256)), "w_kv": a[:, 2 * LORA_PAD + 256:]}


def _pack_repl(p):
    row = lambda a: _pad_to(a.reshape(1, -1), D_MODEL, 1)
    return jnp.concatenate([p["mlp_norm"].reshape(2, D_MODEL), row(p["kv_norm"]), row(p["b_norm"]), row(p["a_r_k"]),
                            row(p["k_norm"]), row(p["b_q_norm"]), row(p["b_sinks"])], axis=0)


def _unpack_repl(a):
    return {"mlp_norm": a[0:2], "kv_norm": a[2], "b_norm": a[3:4], "a_r_k": a[4].reshape(1, N_HEADS, HEAD_DIM),
            "k_norm": a[5, :HEAD_DIM], "b_q_norm": a[6:7, :HEAD_DIM], "b_sinks": a[7:8, :N_HEADS]}


_WEIGHTS = ("a_norm", "a_mix", "a_w_rkv", "a_w0", "a_w1", "a_w2", "a_a0", "a_a1", "a_a2", "a_g1", "a_g2", "a_k_k", "a_k_a",
            "a_r_k", "a_ln_x_w", "a_ln_x_b", "a_w_out", "mlp_norm", "mlp_w_up", "mlp_w_down", "kv_norm", "w_kv", "k_norm",
            "b_norm", "b_w_q", "b_q_norm", "b_sinks", "b_w_o")


def _big_shards(p):
    return [p["a_w_rkv"][0, 0], p["a_w_rkv"][0, 1], p["a_w_rkv"][0, 2], p["a_w_out"][0], p["mlp_w_up"][0], p["mlp_w_up"][1],
            p["mlp_w_down"][0], p["mlp_w_down"][1], p["b_w_q"][0], p["b_w_o"][0]]


_BIG_NAMES = ("a_w_r", "a_w_k", "a_w_v", "a_w_out", "mlp_w_up0", "mlp_w_up1", "mlp_w_down0", "mlp_w_down1", "b_w_q", "b_w_o")


def kernel(x, a_norm, a_mix, a_w_rkv, a_w0, a_w1, a_w2, a_a0, a_a1, a_a2, a_g1, a_g2, a_k_k, a_k_a, a_r_k, a_ln_x_w,
           a_ln_x_b, a_w_out, mlp_norm, mlp_w_up, mlp_w_down, kv_norm, w_kv, k_norm, b_norm, b_w_q, b_q_norm, b_sinks,
           b_w_o, loss_target, m_a_norm, m_a_mix, m_a_w_rkv, m_a_w0, m_a_w1, m_a_w2, m_a_a0, m_a_a1, m_a_a2, m_a_g1,
           m_a_g2, m_a_k_k, m_a_k_a, m_a_r_k, m_a_ln_x_w, m_a_ln_x_b, m_a_w_out, m_mlp_norm, m_mlp_w_up, m_mlp_w_down,
           m_kv_norm, m_w_kv, m_k_norm, m_b_norm, m_b_w_q, m_b_q_norm, m_b_sinks, m_b_w_o, v_a_norm, v_a_mix, v_a_w_rkv,
           v_a_w0, v_a_w1, v_a_w2, v_a_a0, v_a_a1, v_a_a2, v_a_g1, v_a_g2, v_a_k_k, v_a_k_a, v_a_r_k, v_a_ln_x_w,
           v_a_ln_x_b, v_a_w_out, v_mlp_norm, v_mlp_w_up, v_mlp_w_down, v_kv_norm, v_w_kv, v_k_norm, v_b_norm, v_b_w_q,
           v_b_q_norm, v_b_sinks, v_b_w_o):
    given = locals()
    wts = {n: given[n] for n in _WEIGHTS}
    mom = {n: given["m_" + n] for n in _WEIGHTS}
    var = {n: given["v_" + n] for n in _WEIGHTS}

    cols_w, rows_w, repl_w = _pack_cols(wts), _pack_rows(wts), _pack_repl(wts)
    big_w = _big_shards(wts)
    gathered = _all_gather([cols_w, rows_w] + [b.astype(BF16) for b in big_w], name="gather_weights")
    full_cols = gathered[0].transpose(1, 0, 2).reshape(_COL_ROWS, D_MODEL)
    full_rows = gathered[1].reshape(D_MODEL, _ROW_COLS)
    w = {}
    w.update({k: v.reshape(v.shape[1:]) for k, v in _unpack_cols(full_cols, (1,)).items()})
    for k in ("a_norm", "a_w0", "a_a0", "a_k_k", "a_k_a", "a_ln_x_w", "a_ln_x_b"):
        w[k] = w[k].reshape(1, D_MODEL)
    for k in ("a_w2", "a_a2"):
        w[k] = _pad_to(w[k], LORA_PAD, 0)
    rows_full = _unpack_rows(full_rows, ())
    w["a_w1"], w["a_a1"] = _pad_to(rows_full["a_w1"], LORA_PAD, 1), _pad_to(rows_full["a_a1"], LORA_PAD, 1)
    w["a_g1"], w["w_kv"] = rows_full["a_g1"], rows_full["w_kv"]
    for k, arr in zip(_BIG_NAMES, gathered[2:]):
        w[k] = arr if k.startswith("mlp_w_up") else arr.reshape(N_DEV * arr.shape[1], arr.shape[2])
    w["mlp_norm0"], w["mlp_norm1"] = mlp_norm[0:1], mlp_norm[1:2]
    w["kv_norm"], w["k_norm"] = kv_norm.reshape(1, D_MODEL), k_norm.reshape(1, HEAD_DIM)
    w["b_norm"], w["b_q_norm"], w["b_sinks"], w["a_r_k"] = b_norm, b_q_norm, b_sinks, a_r_k.reshape(1, D_MODEL)

    loss_local, grad_x, g = _local_step(x[0], loss_target[0], w)
    loss = lax.psum(loss_local[0, 0], MESH_AXES)

    g_lead = {k: g[k][None] for k in ("a_norm", "a_mix", "a_w0", "a_a0", "a_k_k", "a_k_a", "a_ln_x_w", "a_ln_x_b", "a_g2")}
    g_lead["a_w2"], g_lead["a_a2"] = g["a_w2"][None, :96], g["a_a2"][None, :96]
    g_cols = _pack_cols(g_lead).reshape(_COL_ROWS, N_DEV, D_MODEL // N_DEV).transpose(1, 0, 2)
    g_rows = _pack_rows({"a_w1": g["a_w1"][:, :96], "a_a1": g["a_a1"][:, :96], "a_g1": g["a_g1"], "w_kv": g["w_kv"]})
    g_rows = g_rows.reshape(N_DEV, D_MODEL // N_DEV, _ROW_COLS)
    g_big = [g[k] if k.startswith("mlp_w_up") else g[k].reshape((N_DEV, g[k].shape[0] // N_DEV, g[k].shape[1])) for k in _BIG_NAMES]
    reduced = _reduce_scatter([g_cols, g_rows] + g_big, ("pack_cols", "pack_rows") + _BIG_NAMES)
    g_repl = _pack_repl({"mlp_norm": jnp.concatenate([g["mlp_norm0"], g["mlp_norm1"]], axis=0), "kv_norm": g["kv_norm"],
                         "b_norm": g["b_norm"], "a_r_k": g["a_r_k"], "k_norm": g["k_norm"], "b_q_norm": g["b_q_norm"],
                         "b_sinks": g["b_sinks"]})
    (repl_slots,) = _all_gather([g_repl], name="gather_replicated_grads")

    res = {}
    cols4 = _adamw(cols_w, _pack_cols(mom), _pack_cols(var), reduced[0], name="adamw_cols")
    rows4 = _adamw(rows_w, _pack_rows(mom), _pack_rows(var), reduced[1], name="adamw_rows")
    repl4 = _adamw(repl_w, _pack_repl(mom), _pack_repl(var), repl_slots, name="adamw_replicated")
    for unpacked in ([_unpack_cols(a, (1,)) for a in cols4], [_unpack_rows(a, (1,)) for a in rows4], [_unpack_repl(a) for a in repl4]):
        for k in unpacked[0]:
            res[k] = tuple(u[k] for u in unpacked)
    big4 = [_adamw(bw, bm, bv, slots, name="adamw_" + k)
            for k, bw, bm, bv, slots in zip(_BIG_NAMES, big_w, _big_shards(mom), _big_shards(var), reduced[2:])]
    res["a_w_rkv"] = tuple(jnp.stack([big4[0][i], big4[1][i], big4[2][i]])[None] for i in range(4))
    res["a_w_out"] = tuple(a[None] for a in big4[3])
    res["mlp_w_up"] = tuple(jnp.stack([big4[4][i], big4[5][i]]) for i in range(4))
    res["mlp_w_down"] = tuple(jnp.stack([big4[6][i], big4[7][i]]) for i in range(4))
    res["b_w_q"] = tuple(a[None] for a in big4[8])
    res["b_w_o"] = tuple(a[None] for a in big4[9])
    res["w_kv"] = tuple(a.reshape(w_kv.shape) for a in res["w_kv"])

    outs = [loss, grad_x[None]]
    for i in range(4):
        outs += [res[n][i].reshape(given[n].shape) for n in _WEIGHTS]
    return tuple(outs)
```

```python
import functools
import math

import jax
import jax.numpy as jnp
from jax import lax
from jax.experimental import pallas as pl
from jax.experimental.pallas import tpu as pltpu

F32 = jnp.float32
BF16 = jnp.bfloat16

D_MODEL = 2048
N_HEADS = 32
HEAD_DIM = 64
N_KV_HEADS = 4
Q_PER_KV = 8
ATT_BLOCK = 128
WKV_CHUNK = 64
LORA_PAD = 128
D_FF = 8192
N_DEV = 8
RMS_EPS = 1e-6
GN_EPS = 64e-5
L2_EPS = 1e-12
ROPE_THETA = 10000.0
ADAM_LR, ADAM_B1, ADAM_B2, ADAM_EPS, ADAM_WD, ADAM_STEP = 0.001, 0.9, 0.999, 1e-08, 0.01, 10
MASK_VALUE = -1e30
VMEM_LIMIT_BYTES = 56 * 1024 * 1024
MESH_AXES = ("x", "y", "c")
HI = lax.Precision.HIGHEST

_NN = (((1,), (0,)), ((), ()))
_NT = (((1,), (1,)), ((), ()))
_TN = (((0,), (0,)), ((), ()))
_BNN = (((2,), (1,)), ((0,), (0,)))
_BNT = (((2,), (2,)), ((0,), (0,)))
_BTN = (((1,), (1,)), ((0,), (0,)))


def _params(sem):
    return pltpu.CompilerParams(dimension_semantics=sem, vmem_limit_bytes=VMEM_LIMIT_BYTES)


def _split2(a):
    hi = a.astype(BF16)
    return hi, (a - hi.astype(F32)).astype(BF16)


def _dot3(a, b, dims):
    ah, al = _split2(a)
    bh, bl = _split2(b)
    d = lambda p, q: lax.dot_general(p, q, dims, preferred_element_type=F32)
    return d(ah, bh) + (d(al, bh) + d(ah, bl))


@functools.partial(jax.custom_vjp, nondiff_argnums=(2,))
def _hdot(a, b, dims=_NN):
    return _dot3(a, b, dims)


def _hdot_fwd(a, b, dims):
    return _dot3(a, b, dims), (a, b)


def _hdot_bwd(dims, res, g):
    a, b = res
    nn, nt, tn = (_NN, _NT, _TN) if dims in (_NN, _NT, _TN) else (_BNN, _BNT, _BTN)
    if dims == nn:
        return _dot3(g, b, nt), _dot3(a, g, tn)
    if dims == nt:
        return _dot3(g, b, nn), _dot3(g, a, tn)
    assert dims == tn
    return _dot3(b, g, nt), _dot3(a, g, nn)


_hdot.defvjp(_hdot_fwd, _hdot_bwd)


def _tri_parts(x):
    hi = x.astype(BF16)
    r1 = x - hi.astype(F32)
    mid = r1.astype(BF16)
    return hi, mid, (r1 - mid.astype(F32)).astype(BF16)


@jax.custom_vjp
def _mask_dot(mask, x):
    mb = mask.astype(BF16)
    p0, p1, p2 = _tri_parts(x)
    d = lambda p: lax.dot_general(mb, p, _BNN, preferred_element_type=F32)
    return d(p0) + (d(p1) + d(p2))


def _mask_dot_fwd(mask, x):
    return _mask_dot(mask, x), mask


def _mask_dot_bwd(mask, g):
    mb = mask.astype(BF16)
    p0, p1, p2 = _tri_parts(g)
    d = lambda p: lax.dot_general(mb, p, _BTN, preferred_element_type=F32)
    return jnp.zeros_like(mask), d(p0) + (d(p1) + d(p2))


_mask_dot.defvjp(_mask_dot_fwd, _mask_dot_bwd)


def _b16dot(a, b, dims):
    return lax.dot_general(a.astype(BF16), b.astype(BF16), dims, preferred_element_type=F32)


@jax.custom_vjp
def _bdot(a, b):
    return _b16dot(a, b, _NN)


def _bdot_fwd(a, b):
    return _b16dot(a, b, _NN), (a, b)


def _bdot_bwd(res, g):
    a, b = res
    return _b16dot(g, b, _NT), _b16dot(a, g, _TN)


_bdot.defvjp(_bdot_fwd, _bdot_bwd)


@jax.custom_vjp
def _bdot_nt(a, b):
    return _b16dot(a, b, _NT)


def _bdot_nt_fwd(a, b):
    return _b16dot(a, b, _NT), (a, b)


def _bdot_nt_bwd(res, g):
    a, b = res
    return _b16dot(g, b, _NN), _b16dot(g, a, _TN)


_bdot_nt.defvjp(_bdot_nt_fwd, _bdot_nt_bwd)


def _rms(x, gain):
    return x * lax.rsqrt(jnp.mean(x * x, axis=-1, keepdims=True) + RMS_EPS) * gain


def _vjp_of(f, n_in, diff):
    def g(*args):
        ins, cts = args[:n_in], args[n_in:]

        def fd(*d):
            full = list(ins)
            for pos, i in enumerate(diff):
                full[i] = d[pos]
            return f(*full)

        _, pull = jax.vjp(fd, *[ins[i] for i in diff])
        return pull(tuple(cts))
    return g


def _mm(a, b, *, name, ta=False, tb=False, a_pro=None, epi=None, epi_args=(), out_dtype=F32,
        tm=1024, tn=1024, tk=512, dims=None, b_spec=None, o_spec=None, o_shape=None):
    if dims is None:
        m, k = (a.shape[1], a.shape[0]) if ta else a.shape
        n = b.shape[0] if tb else b.shape[1]
    else:
        m, n, k = dims
    tm, tn, tk = min(tm, m), min(tn, n), min(tk, k)
    assert m % tm == 0 and n % tn == 0 and k % tk == 0, (name, m, n, k, tm, tn, tk)
    nk = k // tk
    ne = len(epi_args)
    cdims = (((0 if ta else 1,), (1 if tb else 0,)), ((), ()))

    def body(a_ref, b_ref, *rest):
        e_refs, o_ref, acc = rest[:ne], rest[ne], rest[ne + 1]
        kk = pl.program_id(2)

        @pl.when(kk == 0)
        def _():
            acc[...] = jnp.zeros_like(acc)

        av = a_ref[...]
        if a_pro is not None:
            av = a_pro(av.astype(F32))
        acc[...] += lax.dot_general(av.astype(BF16), b_ref[...].astype(BF16), cdims, preferred_element_type=F32)

        @pl.when(kk == nk - 1)
        def _():
            r = acc[...]
            if epi is not None:
                r = epi(r, *[e[...] for e in e_refs])
            o_ref[...] = r.astype(o_ref.dtype)

    a_spec = pl.BlockSpec((tk, tm), lambda i, j, q: (q, i)) if ta else pl.BlockSpec((tm, tk), lambda i, j, q: (i, q))
    if b_spec is None:
        b_spec = pl.BlockSpec((tn, tk), lambda i, j, q: (j, q)) if tb else pl.BlockSpec((tk, tn), lambda i, j, q: (q, j))
    if o_spec is None:
        o_spec = pl.BlockSpec((tm, tn), lambda i, j, q: (i, j))
        o_shape = (m, n)
    e_specs = [pl.BlockSpec((tm, tn), lambda i, j, q: (i, j)) for _ in epi_args]
    return pl.pallas_call(
        body, grid=(m // tm, n // tn, nk), in_specs=[a_spec, b_spec] + e_specs, out_specs=o_spec,
        out_shape=jax.ShapeDtypeStruct(o_shape, out_dtype), scratch_shapes=[pltpu.VMEM((tm, tn), F32)],
        compiler_params=_params(("parallel", "parallel", "arbitrary")), name=name,
    )(a, b, *epi_args)


def _rowwise(fn, rows, params, out_rows, out_params=(), *, tm=256, name):
    t = rows[0].shape[0]
    tm = min(tm, t)
    assert t % tm == 0
    nr, npar, nor, nop = len(rows), len(params), len(out_rows), len(out_params)

    def body(*refs):
        r, p = refs[:nr], refs[nr:nr + npar]
        o, op = refs[nr + npar:nr + npar + nor], refs[nr + npar + nor:]
        outs = fn(*[x[...] for x in r], *[x[...] for x in p])
        for ref, val in zip(o, outs[:nor]):
            ref[...] = val.astype(ref.dtype)
        if nop:
            @pl.when(pl.program_id(0) == 0)
            def _():
                for ref in op:
                    ref[...] = jnp.zeros_like(ref)

            for ref, val in zip(op, outs[nor:]):
                ref[...] += val.astype(F32)

    in_specs = [pl.BlockSpec((tm, x.shape[1]), lambda i: (i, 0)) for x in rows]
    in_specs += [pl.BlockSpec(p.shape, lambda i: (0, 0)) for p in params]
    out_specs = [pl.BlockSpec((tm, s.shape[1]), lambda i: (i, 0)) for s in out_rows]
    out_specs += [pl.BlockSpec(s.shape, lambda i: (0, 0)) for s in out_params]
    return pl.pallas_call(
        body, grid=(t // tm,), in_specs=in_specs, out_specs=out_specs, out_shape=list(out_rows) + list(out_params),
        compiler_params=_params(("arbitrary",)), name=name,
    )(*rows, *params)


def _headwise(fn, rows, params, out_rows, out_params=(), *, tt=512, name):
    nh, t, n = rows[0].shape
    tt = min(tt, t)
    assert t % tt == 0
    nr, npar, nor, nop = len(rows), len(params), len(out_rows), len(out_params)

    def body(*refs):
        r, p = refs[:nr], refs[nr:nr + npar]
        o, op = refs[nr + npar:nr + npar + nor], refs[nr + npar + nor:]
        outs = fn(*[x[...] for x in r], *[x[...] for x in p])
        for ref, val in zip(o, outs[:nor]):
            ref[...] = val.astype(ref.dtype)
        if nop:
            @pl.when(pl.program_id(1) == 0)
            def _():
                for ref in op:
                    ref[...] = jnp.zeros_like(ref)

            for ref, val in zip(op, outs[nor:]):
                ref[...] += val.astype(F32)

    row_spec = pl.BlockSpec((None, tt, n), lambda h, i: (h, i, 0))
    par_spec = pl.BlockSpec((None, 1, n), lambda h, i: (h, 0, 0))
    return pl.pallas_call(
        body, grid=(nh, t // tt), in_specs=[row_spec] * nr + [par_spec] * npar,
        out_specs=[row_spec] * nor + [par_spec] * nop, out_shape=list(out_rows) + list(out_params),
        compiler_params=_params(("parallel", "arbitrary")), name=name,
    )(*rows, *params)


def _sds(shape, dtype=F32):
    return jax.ShapeDtypeStruct(tuple(shape), dtype)


def _wkv_chunk(s0, r, lw, k, v, a, b):
    nb, c, _ = r.shape
    ti = lax.broadcasted_iota(jnp.int32, (nb, c, c), 1)
    si = lax.broadcasted_iota(jnp.int32, (nb, c, c), 2)
    incl, strict = si <= ti, si < ti
    cum = _mask_dot(incl.astype(F32), lw)
    rcum = _mask_dot((si > ti).astype(F32), lw)
    tot = jnp.sum(lw, axis=1, keepdims=True)
    w_inv = jnp.exp(-cum)
    at, rt, bt, kt = a * jnp.exp(cum - lw), r * jnp.exp(cum), b * w_inv, k * w_inv
    l_ab = jnp.where(strict, _hdot(at, bt, _BNT), 0.0)
    l_ak = jnp.where(strict, _hdot(at, kt, _BNT), 0.0)
    t_rb = jnp.where(incl, _hdot(rt, bt, _BNT), 0.0)
    t_rk = jnp.where(incl, _hdot(rt, kt, _BNT), 0.0)
    u = _hdot(at, s0, _BNT) + _hdot(l_ak, v, _BNN)
    p = l_ab
    n_it = int(math.log2(c))
    for it in range(n_it):
        u = u + _hdot(p, u, _BNN)
        if it + 1 < n_it:
            p = _hdot(p, p, _BNN)
    y = _hdot(rt, s0, _BNT) + _hdot(t_rb, u, _BNN) + _hdot(t_rk, v, _BNN)
    e = jnp.exp(rcum)
    s1 = s0 * jnp.exp(tot) + _hdot(u, b * e, _BTN) + _hdot(v, k * e, _BTN)
    return y, s1


WKV_HEADS_PER_STEP = 4


def _first_and_last_step(grid):
    i, j = pl.program_id(0), pl.program_id(1)
    return jnp.logical_and(i == 0, j == 0), jnp.logical_and(i == grid[0] - 1, j == grid[1] - 1)


def _wkv_fwd(r, lw, k, v, a, b, shards=()):
    nh, t, n = r.shape
    nc = t // WKV_CHUNK
    hb = WKV_HEADS_PER_STEP
    grid = (nh // hb, nc)
    ns = len(shards)

    def body(*refs):
        r_ref, lw_ref, k_ref, v_ref, a_ref, b_ref = refs[:6]
        y_ref, s_ref = refs[6 + ns:8 + ns]
        state = refs[8 + 2 * ns]
        if ns:
            start, finish = _gather_plan(refs[6:6 + ns], refs[8 + ns:8 + 2 * ns], *refs[9 + 2 * ns:])
            first, last = _first_and_last_step(grid)
            pl.when(first)(start)

        @pl.when(pl.program_id(1) == 0)
        def _():
            state[...] = jnp.zeros_like(state)

        s0 = state[...]
        s_ref[:, 0] = s0
        y, s1 = _wkv_chunk(s0, r_ref[...], lw_ref[...], k_ref[...], v_ref[...], a_ref[...], b_ref[...])
        y_ref[...] = y
        state[...] = s1
        if ns:
            pl.when(last)(finish)

    blk = pl.BlockSpec((hb, WKV_CHUNK, n), lambda h, c: (h, c, 0))
    sblk = pl.BlockSpec((hb, 1, n, n), lambda h, c: (h, c, 0, 0))
    outs = pl.pallas_call(
        body, grid=grid, in_specs=[blk] * 6 + [_ANY] * ns, out_specs=[blk, sblk] + [_ANY] * ns,
        out_shape=[_sds((nh, t, n)), _sds((nh, nc, n, n))] + [_sds((N_DEV,) + s.shape, s.dtype) for s in shards],
        scratch_shapes=[pltpu.VMEM((hb, n, n), F32)] + (_gather_semaphores(ns) if ns else []),
        compiler_params=_params(("arbitrary", "arbitrary")), name="wkv_fwd",
    )(r, lw, k, v, a, b, *shards)
    return outs[0], outs[1], list(outs[2:])


def _wkv_bwd(r, lw, k, v, a, b, states, dy, chip_sums=()):
    nh, t, n = r.shape
    nc = t // WKV_CHUNK
    hb = WKV_HEADS_PER_STEP
    grid = (nh // hb, nc)
    ns = len(chip_sums)

    def body(*refs):
        r_ref, lw_ref, k_ref, v_ref, a_ref, b_ref, s_ref, dy_ref = refs[:8]
        dr_ref, dlw_ref, dk_ref, dv_ref, da_ref, db_ref = refs[8 + ns:14 + ns]
        dstate = refs[14 + 2 * ns]
        if ns:
            start, finish = _chip_exchange_plan(refs[8:8 + ns], refs[14 + ns:14 + 2 * ns], *refs[15 + 2 * ns:])
            first, last = _first_and_last_step(grid)
            pl.when(first)(start)

        @pl.when(pl.program_id(1) == 0)
        def _():
            dstate[...] = jnp.zeros_like(dstate)

        _, pull = jax.vjp(_wkv_chunk, s_ref[:, 0], r_ref[...], lw_ref[...], k_ref[...], v_ref[...], a_ref[...], b_ref[...])
        ds0, dr, dlw, dk, dv, da, db = pull((dy_ref[...], dstate[...]))
        dr_ref[...], dlw_ref[...], dk_ref[...] = dr, dlw, dk
        dv_ref[...], da_ref[...], db_ref[...] = dv, da, db
        dstate[...] = ds0
        if ns:
            pl.when(last)(finish)

    blk = pl.BlockSpec((hb, WKV_CHUNK, n), lambda h, c: (h, nc - 1 - c, 0))
    sblk = pl.BlockSpec((hb, 1, n, n), lambda h, c: (h, nc - 1 - c, 0, 0))
    outs = pl.pallas_call(
        body, grid=grid, in_specs=[blk] * 6 + [sblk, blk] + [_ANY] * ns, out_specs=[blk] * 6 + [_ANY] * ns,
        out_shape=[_sds((nh, t, n))] * 6 + [_sds(p.shape, p.dtype) for p in chip_sums],
        scratch_shapes=[pltpu.VMEM((hb, n, n), F32)] + (_chip_exchange_semaphores(ns) if ns else []),
        compiler_params=_params(("arbitrary", "arbitrary")), name="wkv_bwd",
    )(r, lw, k, v, a, b, states, dy, *chip_sums)
    return outs[:6], list(outs[6:])


def _wkv_prep(k, wl, al, w0, a0, k_k, k_a):
    z = -(w0 + wl)
    softplus = jnp.maximum(z, 0.0) + jnp.log1p(jnp.exp(-jnp.abs(z)))
    lw = -jnp.exp(-softplus - 0.5)
    asig = jax.nn.sigmoid(a0 + al)
    kk = k * k_k
    kk = kk / jnp.maximum(jnp.sqrt(jnp.sum(kk * kk, axis=-1, keepdims=True)), L2_EPS)
    kmod = k * (1.0 + (asig - 1.0) * k_a)
    return lw, kmod, -kk, kk * asig


def _wkv_post(y, r, kmod, v, ln_w, ln_b, r_k):
    mu = jnp.mean(y, axis=-1, keepdims=True)
    var = jnp.mean(jnp.square(y - mu), axis=-1, keepdims=True)
    yn = (y - mu) * lax.rsqrt(var + GN_EPS)
    yn = yn * ln_w + ln_b
    return (yn + jnp.sum(r * kmod * r_k, axis=-1, keepdims=True) * v,)


def _attn_block(nonzero_block, qs, kc, kp, vc, vp, cos_c, sin_c, cos_p, sin_p, q_gain, k_gain, sinks):
    ri = lax.broadcasted_iota(jnp.int32, (HEAD_DIM, HEAD_DIM), 0)
    ci = lax.broadcasted_iota(jnp.int32, (HEAD_DIM, HEAD_DIM), 1)
    half = HEAD_DIM // 2
    rot = jnp.where(ri == ci + half, -1.0, 0.0) + jnp.where(ri + half == ci, 1.0, 0.0)

    def rope(x, cos, sin):
        return x * cos + _hdot(x, rot, _NN) * sin

    kcr = rope(_rms(kc, k_gain), cos_c, sin_c)
    kpr = rope(_rms(kp, k_gain), cos_p, sin_p)
    qi = lax.broadcasted_iota(jnp.int32, (ATT_BLOCK, ATT_BLOCK), 0)
    ki = lax.broadcasted_iota(jnp.int32, (ATT_BLOCK, ATT_BLOCK), 1)
    mask_c = ki <= qi
    mask_p = jnp.logical_and(ki > qi, nonzero_block)
    lane0 = (lax.broadcasted_iota(jnp.int32, (1, 128), 1) == 0).astype(F32)
    outs = []
    for g in range(Q_PER_KV):
        q = rope(_rms(qs[g], q_gain), cos_c, sin_c)
        sc = jnp.where(mask_c, _bdot_nt(q, kcr) * (HEAD_DIM ** -0.5), MASK_VALUE)
        sp = jnp.where(mask_p, _bdot_nt(q, kpr) * (HEAD_DIM ** -0.5), MASK_VALUE)
        sk = jnp.sum(sinks[g] * lane0, axis=1, keepdims=True)
        mx = jnp.maximum(jnp.maximum(jnp.max(sc, axis=1, keepdims=True), jnp.max(sp, axis=1, keepdims=True)), sk)
        mx = lax.stop_gradient(mx)
        ec, ep = jnp.exp(sc - mx), jnp.exp(sp - mx)
        den = jnp.sum(ec, axis=1, keepdims=True) + jnp.sum(ep, axis=1, keepdims=True) + jnp.exp(sk - mx)
        outs.append(_bdot(ec / den, vc) + _bdot(ep / den, vp))
    return tuple(outs)


def _attn_specs(t):
    nb = t // ATT_BLOCK
    prev = lambda n: jnp.maximum(n - 1, 0)
    q_spec = pl.BlockSpec((Q_PER_KV, ATT_BLOCK, HEAD_DIM), lambda g, n: (g, n, 0))
    kv_c = pl.BlockSpec((None, ATT_BLOCK, HEAD_DIM), lambda g, n: (g, n, 0))
    kv_p = pl.BlockSpec((None, ATT_BLOCK, HEAD_DIM), lambda g, n: (g, prev(n), 0))
    tab_c = pl.BlockSpec((ATT_BLOCK, HEAD_DIM), lambda g, n: (n, 0))
    tab_p = pl.BlockSpec((ATT_BLOCK, HEAD_DIM), lambda g, n: (prev(n), 0))
    gain = pl.BlockSpec((1, HEAD_DIM), lambda g, n: (0, 0))
    sink = pl.BlockSpec((Q_PER_KV, 1, 128), lambda g, n: (g, 0, 0))
    return nb, q_spec, kv_c, kv_p, tab_c, tab_p, gain, sink


def _attn_fwd(q, k, v, cos, sin, q_gain, k_gain, sinks):
    t = q.shape[1]
    nb, q_spec, kv_c, kv_p, tab_c, tab_p, gain, sink = _attn_specs(t)

    def body(q_ref, kc, kp, vc, vp, cc, sc, cp, sp, qg, kg, sk, o_ref):
        outs = _attn_block(pl.program_id(1) > 0, [q_ref[g] for g in range(Q_PER_KV)], kc[...], kp[...], vc[...], vp[...],
                           cc[...], sc[...], cp[...], sp[...], qg[...], kg[...], [sk[g] for g in range(Q_PER_KV)])
        for g in range(Q_PER_KV):
            o_ref[g] = outs[g]

    return pl.pallas_call(
        body, grid=(N_KV_HEADS, nb), in_specs=[q_spec, kv_c, kv_p, kv_c, kv_p, tab_c, tab_c, tab_p, tab_p, gain, gain, sink],
        out_specs=q_spec, out_shape=_sds(q.shape), compiler_params=_params(("parallel", "arbitrary")), name="attn_fwd",
    )(q, k, k, v, v, cos, sin, cos, sin, q_gain, k_gain, sinks)


def _attn_bwd(q, k, v, cos, sin, q_gain, k_gain, sinks, do):
    t = q.shape[1]
    nb, q_spec, kv_c, kv_p, tab_c, tab_p, gain, sink = _attn_specs(t)
    nq = Q_PER_KV

    def body(q_ref, kc, kp, vc, vp, cc, sc, cp, sp, qg, kg, sk, do_ref,
             dq_ref, dkc_ref, dkp_ref, dvc_ref, dvp_ref, dqg_ref, dkg_ref, dsk_ref):
        nonzero = pl.program_id(1) > 0
        tabs = (cc[...], sc[...], cp[...], sp[...])

        def f(qs, kcv, kpv, vcv, vpv, qgv, kgv, sks):
            return _attn_block(nonzero, qs, kcv, kpv, vcv, vpv, *tabs, qgv, kgv, sks)

        _, pull = jax.vjp(f, [q_ref[g] for g in range(nq)], kc[...], kp[...], vc[...], vp[...], qg[...], kg[...],
                          [sk[g] for g in range(nq)])
        dqs, dkc, dkp, dvc, dvp, dqg, dkg, dsks = pull(tuple(do_ref[g] for g in range(nq)))
        for g in range(nq):
            dq_ref[g] = dqs[g]
        dkc_ref[...], dkp_ref[...], dvc_ref[...], dvp_ref[...] = dkc, dkp, dvc, dvp
        first = jnp.logical_and(pl.program_id(0) == 0, pl.program_id(1) == 0)

        @pl.when(first)
        def _():
            dqg_ref[...] = jnp.zeros_like(dqg_ref)
            dkg_ref[...] = jnp.zeros_like(dkg_ref)

        @pl.when(pl.program_id(1) == 0)
        def _():
            dsk_ref[...] = jnp.zeros_like(dsk_ref)

        dqg_ref[...] += dqg
        dkg_ref[...] += dkg
        for g in range(nq):
            dsk_ref[g] += dsks[g]

    kv_shape = _sds(k.shape)
    return pl.pallas_call(
        body, grid=(N_KV_HEADS, nb),
        in_specs=[q_spec, kv_c, kv_p, kv_c, kv_p, tab_c, tab_c, tab_p, tab_p, gain, gain, sink, q_spec],
        out_specs=[q_spec, kv_c, kv_c, kv_c, kv_c, gain, gain, sink],
        out_shape=[_sds(q.shape), kv_shape, kv_shape, kv_shape, kv_shape, _sds((1, HEAD_DIM)), _sds((1, HEAD_DIM)), _sds(sinks.shape)],
        compiler_params=_params(("arbitrary", "arbitrary")), name="attn_bwd",
    )(q, k, k, v, v, cos, sin, cos, sin, q_gain, k_gain, sinks, do)


def _time_shift_lerps(x, xs, gain, *mix):
    xn, xsn = _rms(x, gain), _rms(xs, gain)
    xx = xsn - xn
    return tuple(xn + xx * m for m in mix)


def _residual_norm(h, delta, gain):
    hn = h + delta
    return hn, _rms(hn, gain)


def _residual_norm2(h, delta, gain_a, gain_b):
    hn = h + delta
    return hn, _rms(hn, gain_a), _rms(hn, gain_b)


def _relu2(u):
    return jnp.square(jnp.maximum(u, 0.0))


def _sigmoid(z):
    return jax.nn.sigmoid(z)


def _to_heads(x2d, nh):
    t = x2d.shape[0]
    return x2d.reshape(t, nh, HEAD_DIM).transpose(1, 0, 2)


def _from_heads(xh):
    nh, t, n = xh.shape
    return xh.transpose(1, 0, 2).reshape(t, nh * n)


def _shift_down(x):
    return jnp.pad(x[:-1], ((1, 0), (0, 0)))


def _shift_up(x):
    return jnp.pad(x[1:], ((0, 1), (0, 0)))


def _rope_tables(t):
    half = HEAD_DIM // 2
    inv_freq = jnp.power(ROPE_THETA, -jnp.arange(half, dtype=F32) / half)
    ang = jnp.arange(t, dtype=jnp.int32).astype(F32)[:, None] * inv_freq[None, :]
    cos, sin = jnp.cos(ang), jnp.sin(ang)
    return jnp.concatenate([cos, cos], axis=1), jnp.concatenate([sin, sin], axis=1)


def _mlp_fwd(hn, w_up, w_down, layer, up_dev_major):
    t = hn.shape[0]
    if up_dev_major:
        cw = w_up.shape[2]
        u = _mm(hn, w_up, name=f"mlp{layer}_up", dims=(t, D_FF, D_MODEL), tn=cw,
                b_spec=pl.BlockSpec((None, 512, cw), lambda i, j, q: (j, q, 0)))
    else:
        u = _mm(hn, w_up, name=f"mlp{layer}_up")
    out = _mm(u, w_down, a_pro=_relu2, name=f"mlp{layer}_down")
    return u, out


def _mlp_bwd(hn, u, dh, w_up, w_down, layer, up_dev_major):
    t = hn.shape[0]
    du = _mm(dh, w_down, tb=True, epi=lambda r, uu: r * (2.0 * jnp.maximum(uu, 0.0)), epi_args=(u,), out_dtype=BF16,
             name=f"mlp{layer}_du")
    d_down = _mm(u, dh, ta=True, a_pro=_relu2, name=f"mlp{layer}_ddown")
    if up_dev_major:
        cw = w_up.shape[2]
        d_up = _mm(hn, du, ta=True, name=f"mlp{layer}_dup", dims=(D_MODEL, D_FF, t), tn=cw,
                   o_spec=pl.BlockSpec((None, 1024, cw), lambda i, j, q: (j, i, 0)), o_shape=(N_DEV, D_MODEL, cw))
        dhn = _mm(du, w_up, tb=True, name=f"mlp{layer}_dhn", dims=(t, D_MODEL, D_FF), tk=cw,
                  b_spec=pl.BlockSpec((None, 1024, cw), lambda i, j, q: (q, j, 0)))
    else:
        d_up = _mm(hn, du, ta=True, name=f"mlp{layer}_dup")
        dhn = _mm(du, w_up, tb=True, name=f"mlp{layer}_dhn")
    return dhn, d_up, d_down


_LATE_WEIGHTS = ("mlp_w_up0", "mlp_w_up1", "mlp_w_down0", "mlp_w_down1", "b_w_q", "b_w_o")
_EARLY_GRADS = _LATE_WEIGHTS + ("a_w_out",)


def _local_step(x, target, w, late_shards=None, up_dev_major=True):
    t = x.shape[0]
    g = {}
    w = dict(w)
    row = lambda: _sds((t, D_MODEL))
    rowb = lambda: _sds((t, D_MODEL), BF16)
    vec = lambda: _sds((1, D_MODEL))

    xs = _shift_down(x)
    mix = [w["a_mix"][i:i + 1] for i in range(6)]
    xr, xk, xv, xw, xa, xg = _rowwise(_time_shift_lerps, [x, xs], [w["a_norm"]] + mix, [rowb()] * 6, tm=256, name="tmix_lerp")
    r = _mm(xr, w["a_w_r"], name="tmix_r")
    k = _mm(xk, w["a_w_k"], name="tmix_k")
    v = _mm(xv, w["a_w_v"], name="tmix_v")
    lw1 = _mm(xw, w["a_w1"], name="tmix_w1")
    wl = _mm(lw1, w["a_w2"], a_pro=jnp.tanh, name="tmix_w2")
    la1 = _mm(xa, w["a_a1"], name="tmix_a1")
    al = _mm(la1, w["a_a2"], name="tmix_a2")
    lg1 = _mm(xg, w["a_g1"], name="tmix_g1")
    gate = _mm(lg1, w["a_g2"], a_pro=_sigmoid, name="tmix_g2")

    hv = lambda name: w[name].reshape(N_HEADS, 1, HEAD_DIM)
    rh, kh, vh, wlh, alh = [_to_heads(z, N_HEADS) for z in (r, k, v, wl, al)]
    head = lambda: _sds((N_HEADS, t, HEAD_DIM))
    prep_params = [hv("a_w0"), hv("a_a0"), hv("a_k_k"), hv("a_k_a")]
    lwh, kmh, ah, bh = _headwise(_wkv_prep, [kh, wlh, alh], prep_params, [head()] * 4, name="wkv_prep")
    yh, states, gathered = _wkv_fwd(rh, lwh, kmh, vh, ah, bh, late_shards or ())
    for name, arr in zip(_LATE_WEIGHTS, gathered):
        w[name] = arr if name.startswith("mlp_w_up") else arr.reshape(N_DEV * arr.shape[1], arr.shape[2])
    post_params = [hv("a_ln_x_w"), hv("a_ln_x_b"), hv("a_r_k")]
    (y2h,) = _headwise(_wkv_post, [yh, rh, kmh, vh], post_params, [head()], name="wkv_post")
    y2 = _from_heads(y2h)
    (yg,) = _rowwise(lambda a, b: (a * b,), [y2, gate], [], [rowb()], name="tmix_gate")
    att = _mm(yg, w["a_w_out"], name="tmix_out")

    h1, hn0 = _rowwise(_residual_norm, [x, att], [w["mlp_norm0"]], [row(), rowb()], name="res_norm0")
    u0, m0 = _mlp_fwd(hn0, w["mlp_w_up0"], w["mlp_w_down0"], 0, up_dev_major)

    h2, kvn, qn = _rowwise(_residual_norm2, [h1, m0], [w["kv_norm"], w["b_norm"]], [row(), rowb(), rowb()], name="res_norm_kvq")
    kv = _mm(kvn, w["w_kv"], name="kv_proj")
    q = _mm(qn, w["b_w_q"], name="q_proj")
    half_kv = N_KV_HEADS * HEAD_DIM
    ksh, vsh = _to_heads(kv[:, :half_kv], N_KV_HEADS), _to_heads(kv[:, half_kv:], N_KV_HEADS)
    qh = _to_heads(q, N_HEADS)
    cos, sin = _rope_tables(t)
    sinks = jnp.broadcast_to(w["b_sinks"].reshape(N_HEADS, 1, 1), (N_HEADS, 1, 128))
    oh = _attn_fwd(qh, ksh, vsh, cos, sin, w["b_q_norm"], w["k_norm"], sinks)
    o = _from_heads(oh).astype(BF16)
    att2 = _mm(o, w["b_w_o"], name="attn_out")

    h3, hn1 = _rowwise(_residual_norm, [h2, att2], [w["mlp_norm1"]], [row(), rowb()], name="res_norm1")
    u1, m1 = _mlp_fwd(hn1, w["mlp_w_up1"], w["mlp_w_down1"], 1, up_dev_major)

    def loss_fn(h, m, tg):
        diff = (h + m) - tg
        part = 0.5 * jnp.sum(jnp.mean(jnp.square(diff), axis=-1, keepdims=True), axis=0, keepdims=True)
        return diff * (1.0 / D_MODEL), jnp.broadcast_to(part, (1, 128))

    dh4, loss = _rowwise(loss_fn, [h3, m1, target], [], [row()], [_sds((1, 128))], name="loss")

    def res_norm_bwd(h, dnext, dhn, gain):
        dh, dgain = _vjp_of(lambda hh, gg: (_rms(hh, gg),), 2, (0, 1))(h, gain, dhn)
        return dnext + dh, dgain

    dhn1, g["mlp_w_up1"], g["mlp_w_down1"] = _mlp_bwd(hn1, u1, dh4, w["mlp_w_up1"], w["mlp_w_down1"], 1, up_dev_major)
    dh3, g["mlp_norm1"] = _rowwise(res_norm_bwd, [h3, dh4, dhn1], [w["mlp_norm1"]], [row()], [vec()], name="res_norm1_bwd")

    g["b_w_o"] = _mm(o, dh3, ta=True, name="attn_out_dw")
    do = _mm(dh3, w["b_w_o"], tb=True, name="attn_out_dx")
    dqh, dkc, dkp, dvc, dvp, g["b_q_norm"], g["k_norm"], dsinks = _attn_bwd(
        qh, ksh, vsh, cos, sin, w["b_q_norm"], w["k_norm"], sinks, _to_heads(do, N_HEADS))
    g["b_sinks"] = dsinks[:, 0, 0].reshape(1, N_HEADS)
    dq = _from_heads(dqh)
    g["b_w_q"] = _mm(qn, dq, ta=True, name="q_proj_dw")
    dqn = _mm(dq, w["b_w_q"], tb=True, name="q_proj_dx")
    shift_blk = lambda z: jnp.pad(z[:, ATT_BLOCK:], ((0, 0), (0, ATT_BLOCK), (0, 0)))
    kv_parts = [_from_heads(z) for z in (dkc, shift_blk(dkp), dvc, shift_blk(dvp))]
    (dkv,) = _rowwise(lambda a, b, c, d: (jnp.concatenate([a + b, c + d], axis=1),), kv_parts, [],
                      [_sds((t, 2 * half_kv), BF16)], name="kv_grad_sum")
    g["w_kv"] = _mm(kvn, dkv, ta=True, name="kv_proj_dw")
    dkvn = _mm(dkv, w["w_kv"], tb=True, name="kv_proj_dx")

    def res_norm2_bwd(h, dnext, dna, dnb, gain_a, gain_b):
        dha, dga = _vjp_of(lambda hh, gg: (_rms(hh, gg),), 2, (0, 1))(h, gain_a, dna)
        dhb, dgb = _vjp_of(lambda hh, gg: (_rms(hh, gg),), 2, (0, 1))(h, gain_b, dnb)
        return dnext + dha + dhb, dga, dgb

    dh2, g["kv_norm"], g["b_norm"] = _rowwise(res_norm2_bwd, [h2, dh3, dkvn, dqn], [w["kv_norm"], w["b_norm"]],
                                              [row()], [vec(), vec()], name="res_norm_kvq_bwd")

    dhn0, g["mlp_w_up0"], g["mlp_w_down0"] = _mlp_bwd(hn0, u0, dh2, w["mlp_w_up0"], w["mlp_w_down0"], 0, up_dev_major)
    dh1, g["mlp_norm0"] = _rowwise(res_norm_bwd, [h1, dh2, dhn0], [w["mlp_norm0"]], [row()], [vec()], name="res_norm0_bwd")

    g["a_w_out"] = _mm(yg, dh1, ta=True, name="tmix_out_dw")
    dyg = _mm(dh1, w["a_w_out"], tb=True, name="tmix_out_dx")
    dy2, dgate = _rowwise(lambda d, a, b: (d * b, d * a), [dyg, y2, gate], [], [row(), rowb()], name="tmix_gate_bwd")
    g["a_g2"] = _mm(lg1, dgate, ta=True, a_pro=_sigmoid, name="tmix_g2_dw")

    def dsigmoid(rr, z):
        s = jax.nn.sigmoid(z)
        return rr * s * (1.0 - s)

    dlg1 = _mm(dgate, w["a_g2"], tb=True, epi=dsigmoid, epi_args=(lg1,), out_dtype=BF16, name="tmix_g2_dx")
    g["a_g1"] = _mm(xg, dlg1, ta=True, name="tmix_g1_dw")
    dxg = _mm(dlg1, w["a_g1"], tb=True, name="tmix_g1_dx")

    post_vjp = _vjp_of(_wkv_post, 7, (0, 1, 2, 3, 4, 5, 6))

    def post_bwd(y_, r_, km_, v_, dy_, lnw_, lnb_, rk_):
        return post_vjp(y_, r_, km_, v_, lnw_, lnb_, rk_, dy_)

    dyh, drh1, dkmh1, dvh1, dlnw, dlnb, drk = _headwise(
        post_bwd, [yh, rh, kmh, vh, _to_heads(dy2, N_HEADS)], post_params, [head()] * 4, [_sds((N_HEADS, 1, HEAD_DIM))] * 3,
        name="wkv_post_bwd")
    early_sums = ()
    if late_shards is not None:
        early_sums = _chip_sums([_device_major(name, g[name]) for name in _EARLY_GRADS], _EARLY_GRADS, "early")
    (drh2, dlwh, dkmh2, dvh2, dah, dbh), early_reduced = _wkv_bwd(rh, lwh, kmh, vh, ah, bh, states, dyh, early_sums)

    def prep_bwd(kk_, wl_, al_, dlw_, dkm1, dkm2, da_, db_, dr1, dr2, dv1, dv2, w0_, a0_, kk0_, ka_):
        grads = _vjp_of(_wkv_prep, 7, (0, 1, 2, 3, 4, 5, 6))(kk_, wl_, al_, w0_, a0_, kk0_, ka_, dlw_, dkm1 + dkm2, da_, db_)
        dk_, dwl_, dal_, dw0_, da0_, dkk_, dka_ = grads
        return dk_, dwl_, dal_, dr1 + dr2, dv1 + dv2, dw0_, da0_, dkk_, dka_

    dkh, dwlh, dalh, drh, dvh, dw0, da0, dk_k, dk_a = _headwise(
        prep_bwd, [kh, wlh, alh, dlwh, dkmh1, dkmh2, dah, dbh, drh1, drh2, dvh1, dvh2], prep_params,
        [_sds((N_HEADS, t, HEAD_DIM), BF16)] * 5, [_sds((N_HEADS, 1, HEAD_DIM))] * 4, tt=256, name="wkv_prep_bwd")
    flat = lambda z: z.reshape(1, D_MODEL)
    g["a_w0"], g["a_a0"], g["a_k_k"], g["a_k_a"] = flat(dw0), flat(da0), flat(dk_k), flat(dk_a)
    g["a_ln_x_w"], g["a_ln_x_b"], g["a_r_k"] = flat(dlnw), flat(dlnb), flat(drk)
    dr, dk, dv, dwl, dal = [_from_heads(z) for z in (drh, dkh, dvh, dwlh, dalh)]

    g["a_w_r"] = _mm(xr, dr, ta=True, name="tmix_r_dw")
    g["a_w_k"] = _mm(xk, dk, ta=True, name="tmix_k_dw")
    g["a_w_v"] = _mm(xv, dv, ta=True, name="tmix_v_dw")
    dxr = _mm(dr, w["a_w_r"], tb=True, name="tmix_r_dx")
    dxk = _mm(dk, w["a_w_k"], tb=True, name="tmix_k_dx")
    dxv = _mm(dv, w["a_w_v"], tb=True, name="tmix_v_dx")
    g["a_w2"] = _mm(lw1, dwl, ta=True, a_pro=jnp.tanh, name="tmix_w2_dw")

    def dtanh(rr, z):
        th = jnp.tanh(z)
        return rr * (1.0 - th * th)

    dlw1 = _mm(dwl, w["a_w2"], tb=True, epi=dtanh, epi_args=(lw1,), out_dtype=BF16, name="tmix_w2_dx")
    g["a_w1"] = _mm(xw, dlw1, ta=True, name="tmix_w1_dw")
    dxw = _mm(dlw1, w["a_w1"], tb=True, name="tmix_w1_dx")
    g["a_a2"] = _mm(la1, dal, ta=True, name="tmix_a2_dw")
    dla1 = _mm(dal, w["a_a2"], tb=True, out_dtype=BF16, name="tmix_a2_dx")
    g["a_a1"] = _mm(xa, dla1, ta=True, name="tmix_a1_dw")
    dxa = _mm(dla1, w["a_a1"], tb=True, name="tmix_a1_dx")

    lerp_bwd = _vjp_of(_time_shift_lerps, 9, tuple(range(9)))

    def lerp_bwd_rows(x_, xs_, d0, d1, d2, d3, d4, d5, gain, *mx):
        return lerp_bwd(x_, xs_, gain, *mx, d0, d1, d2, d3, d4, d5)

    outs = _rowwise(lerp_bwd_rows, [x, xs, dxr, dxk, dxv, dxw, dxa, dxg], [w["a_norm"]] + mix, [row(), row()], [vec()] * 7,
                    tm=128, name="tmix_lerp_bwd")
    dx_a, dxs, g["a_norm"] = outs[0], outs[1], outs[2]
    g["a_mix"] = jnp.concatenate(outs[3:9], axis=0)
    (grad_x,) = _rowwise(lambda a, b, c: (a + b + c,), [dh1, dx_a, _shift_up(dxs)], [], [row()], name="grad_x_sum")
    return loss, grad_x, g, (early_reduced if late_shards is not None else None)


def _device_major(name, grad):
    return grad if name.startswith("mlp_w_up") else grad.reshape((N_DEV, grad.shape[0] // N_DEV, grad.shape[1]))


_ANY = pl.BlockSpec(memory_space=pl.ANY)
_MESH_ID = pl.DeviceIdType.MESH
N_PEERS = N_DEV - 1


def _linear(pos):
    return 4 * pos[0] + 2 * pos[1] + pos[2]


def _all_gather(shards, name):
    n = len(shards)

    def body(*refs):
        start, finish = _gather_plan(refs[:n], refs[n:2 * n], *refs[2 * n:])
        start()
        finish()

    return pl.pallas_call(
        body, out_shape=[_sds((N_DEV,) + s.shape, s.dtype) for s in shards], in_specs=[_ANY] * n, out_specs=[_ANY] * n,
        scratch_shapes=_gather_semaphores(n), name=name,
    )(*shards)


def _gather_semaphores(n):
    return [pltpu.SemaphoreType.DMA((n * N_PEERS,)), pltpu.SemaphoreType.DMA((n * N_PEERS,)), pltpu.SemaphoreType.DMA((n,))]


def _gather_plan(ins, outs, send_sems, recv_sems, local_sems):
    n = len(ins)
    x, y, c = lax.axis_index("x"), lax.axis_index("y"), lax.axis_index("c")
    me, sibling = (x, y, c), (x, y, 1 - c)
    chips = [(1 - x, y), (x, 1 - y), (1 - x, 1 - y)]

    def copy(a, k, block, to, src=None):
        dst = outs[a].at[_linear(block)]
        return pltpu.make_async_remote_copy(
            src_ref=dst if src is None else src, dst_ref=dst, send_sem=send_sems.at[a * N_PEERS + k],
            recv_sem=recv_sems.at[a * N_PEERS + k], device_id=to, device_id_type=_MESH_ID)

    def own_copies():
        mine = [pltpu.make_async_copy(ins[a], outs[a].at[_linear(me)], local_sems.at[a]) for a in range(n)]
        first = []
        for a in range(n):
            first.append(copy(a, 0, me, sibling, src=ins[a]))
            first += [copy(a, 1 + j, me, (*chip, c), src=ins[a]) for j, chip in enumerate(chips)]
        return mine, first

    def start():
        mine, first = own_copies()
        for cp in mine + first:
            cp.start()

    def finish():
        mine, first = own_copies()
        passed = []
        for j, chip in enumerate(chips):
            for a in range(n):
                copy(a, 1 + j, (*chip, c), me).wait_recv()
                fwd = copy(a, 4 + j, (*chip, c), sibling)
                fwd.start()
                passed.append(fwd)
        for a in range(n):
            copy(a, 0, sibling, me).wait_recv()
            for j, chip in enumerate(chips):
                copy(a, 4 + j, (*chip, 1 - c), me).wait_recv()
        for cp in first + passed:
            cp.wait_send()
        for cp in mine:
            cp.wait()

    return start, finish


N_CHIPS = 4


def _exchange_with_sibling(parts, name):
    n = len(parts)

    def body(*refs):
        ins, outs = refs[:n], refs[n:2 * n]
        send_sems, recv_sems = refs[2 * n:]
        x, y, c = lax.axis_index("x"), lax.axis_index("y"), lax.axis_index("c")
        copies = []
        for a in range(n):
            for q in range(N_CHIPS):
                cp = pltpu.make_async_remote_copy(
                    src_ref=ins[a].at[2 * q + (1 - c)], dst_ref=outs[a].at[q], send_sem=send_sems.at[a * N_CHIPS + q],
                    recv_sem=recv_sems.at[a * N_CHIPS + q], device_id=(x, y, 1 - c), device_id_type=_MESH_ID)
                cp.start()
                copies.append(cp)
        for cp in copies:
            cp.wait()

    return pl.pallas_call(
        body, out_shape=[_sds((N_CHIPS,) + p.shape[1:], p.dtype) for p in parts], in_specs=[_ANY] * n, out_specs=[_ANY] * n,
        scratch_shapes=[pltpu.SemaphoreType.DMA((n * N_CHIPS,)), pltpu.SemaphoreType.DMA((n * N_CHIPS,))],
        name=name,
    )(*parts)


def _pair_sum(part, recv, core, out_dtype, name):
    _, r, cdim = recv.shape
    tr = max(8, min(r, (1 << 18) // cdim))
    assert r % tr == 0, (name, r, tr)

    def body(core_ref, p_ref, r_ref, o_ref):
        o_ref[...] = (p_ref[...] + r_ref[...]).astype(o_ref.dtype)

    grid_spec = pltpu.PrefetchScalarGridSpec(
        num_scalar_prefetch=1, grid=(N_CHIPS, r // tr),
        in_specs=[pl.BlockSpec((None, None, tr, cdim), lambda q, i, core_ref: (q, core_ref[0], i, 0)),
                  pl.BlockSpec((None, tr, cdim), lambda q, i, core_ref: (q, i, 0))],
        out_specs=pl.BlockSpec((None, tr, cdim), lambda q, i, core_ref: (q, i, 0)))
    return pl.pallas_call(
        body, grid_spec=grid_spec, out_shape=_sds((N_CHIPS, r, cdim), out_dtype),
        compiler_params=_params(("parallel", "parallel")), name=name,
    )(core, part.reshape(N_CHIPS, 2, r, cdim), recv)


def _exchange_between_chips(parts, name):
    n = len(parts)

    def body(*refs):
        start, finish = _chip_exchange_plan(refs[:n], refs[n:2 * n], *refs[2 * n:])
        start()
        finish()

    return pl.pallas_call(
        body, out_shape=[_sds(p.shape, p.dtype) for p in parts], in_specs=[_ANY] * n, out_specs=[_ANY] * n,
        scratch_shapes=_chip_exchange_semaphores(n), name=name,
    )(*parts)


def _chip_exchange_semaphores(n):
    n_other = N_CHIPS - 1
    return [pltpu.SemaphoreType.DMA((n * n_other,)), pltpu.SemaphoreType.DMA((n * n_other,)), pltpu.SemaphoreType.DMA((n,))]


def _chip_exchange_plan(ins, outs, send_sems, recv_sems, local_sems):
    n = len(ins)
    n_other = N_CHIPS - 1
    x, y, c = lax.axis_index("x"), lax.axis_index("y"), lax.axis_index("c")
    my_chip = 2 * x + y

    def all_copies():
        mine = [pltpu.make_async_copy(ins[a].at[my_chip], outs[a].at[my_chip], local_sems.at[a]) for a in range(n)]
        remote = []
        for j, (fx, fy) in enumerate([(1, 0), (0, 1), (1, 1)]):
            px, py = (1 - x if fx else x), (1 - y if fy else y)
            for a in range(n):
                remote.append(pltpu.make_async_remote_copy(
                    src_ref=ins[a].at[2 * px + py], dst_ref=outs[a].at[my_chip], send_sem=send_sems.at[a * n_other + j],
                    recv_sem=recv_sems.at[a * n_other + j], device_id=(px, py, c), device_id_type=_MESH_ID))
        return mine, remote

    def start():
        mine, remote = all_copies()
        for cp in mine + remote:
            cp.start()

    def finish():
        mine, remote = all_copies()
        for cp in remote + mine:
            cp.wait()

    return start, finish


def _chip_sums(parts, names, tag):
    core = lax.axis_index("c").astype(jnp.int32).reshape(1)
    from_sibling = _exchange_with_sibling(parts, name="scatter_grads_sibling_" + tag)
    return [_pair_sum(p, r, core, F32 if nm.startswith("pack") else BF16, name="pair_sum_" + nm)
            for p, r, nm in zip(parts, from_sibling, names)]


def _reduce_scatter(parts, names, tag):
    return _exchange_between_chips(_chip_sums(parts, names, tag), name="scatter_grads_chips_" + tag)


def _adamw(w, m, v, slots, name):
    r, c = w.shape
    ns = slots.shape[0]
    tr = max(8, min(r, (1 << 18) // c))
    assert r % tr == 0, (name, r, tr)

    def body(w_ref, m_ref, v_ref, g_ref, g_out, d_out, m_out, v_out):
        g = g_ref[0].astype(F32)
        for s in range(1, ns):
            g = g + g_ref[s].astype(F32)
        m_new = ADAM_B1 * m_ref[...] + (1.0 - ADAM_B1) * g
        v_new = ADAM_B2 * v_ref[...] + (1.0 - ADAM_B2) * jnp.square(g)
        m_hat = m_new / (1.0 - ADAM_B1 ** ADAM_STEP)
        v_hat = v_new / (1.0 - ADAM_B2 ** ADAM_STEP)
        d_out[...] = -ADAM_LR * (m_hat / (jnp.sqrt(v_hat) + ADAM_EPS) + ADAM_WD * w_ref[...])
        g_out[...], m_out[...], v_out[...] = g, m_new, v_new

    spec = pl.BlockSpec((tr, c), lambda i: (i, 0))
    return pl.pallas_call(
        body, grid=(r // tr,), in_specs=[spec, spec, spec, pl.BlockSpec((ns, tr, c), lambda i: (0, i, 0))],
        out_specs=[spec] * 4, out_shape=[_sds((r, c))] * 4, compiler_params=_params(("parallel",)), name=name,
    )(w, m, v, slots)


_COL_VECTORS = ("a_norm", "a_mix", "a_w0", "a_a0", "a_k_k", "a_k_a", "a_ln_x_w", "a_ln_x_b")
_COL_VEC_ROWS = 16
_COL_ROWS = _COL_VEC_ROWS + 2 * LORA_PAD + 256
_ROW_COLS = 2 * LORA_PAD + 256 + 512
_REPL_ROWS = 8


def _pad_to(a, size, axis):
    widths = [(0, 0)] * a.ndim
    widths[axis] = (0, size - a.shape[axis])
    return jnp.pad(a, widths)


def _pack_cols(p):
    width = p["a_norm"].shape[-1]
    vecs = jnp.concatenate([p[n].reshape(-1, width) for n in _COL_VECTORS], axis=0)
    return jnp.concatenate([_pad_to(vecs, _COL_VEC_ROWS, 0), _pad_to(p["a_w2"].reshape(-1, width), LORA_PAD, 0),
                            _pad_to(p["a_a2"].reshape(-1, width), LORA_PAD, 0), p["a_g2"].reshape(-1, width)], axis=0)


def _unpack_cols(a, lead):
    width = a.shape[-1]
    out, row = {}, 0
    for n in _COL_VECTORS:
        k = 6 if n == "a_mix" else 1
        out[n] = a[row:row + k].reshape(lead + ((6, width) if n == "a_mix" else (width,)))
        row += k
    base = _COL_VEC_ROWS
    out["a_w2"] = a[base:base + 96].reshape(lead + (96, width))
    out["a_a2"] = a[base + LORA_PAD:base + LORA_PAD + 96].reshape(lead + (96, width))
    out["a_g2"] = a[base + 2 * LORA_PAD:].reshape(lead + (256, width))
    return out


def _pack_rows(p):
    rows = p["w_kv"].shape[0]
    return jnp.concatenate([_pad_to(p["a_w1"].reshape(rows, -1), LORA_PAD, 1), _pad_to(p["a_a1"].reshape(rows, -1), LORA_PAD, 1),
                            p["a_g1"].reshape(rows, -1), p["w_kv"]], axis=1)


def _unpack_rows(a, lead):
    rows = a.shape[0]
    return {"a_w1": a[:, :96].reshape(lead + (rows, 96)), "a_a1": a[:, LORA_PAD:LORA_PAD + 96].reshape(lead + (rows, 96)),
            "a_g1": a[:, 2 * LORA_PAD:2 * LORA_PAD + 256].reshape(lead + (rows, 256)), "w_kv": a[:, 2 * LORA_PAD + 256:]}


def _pack_repl(p):
    row = lambda a: _pad_to(a.reshape(1, -1), D_MODEL, 1)
    return jnp.concatenate([p["mlp_norm"].reshape(2, D_MODEL), row(p["kv_norm"]), row(p["b_norm"]), row(p["a_r_k"]),
                            row(p["k_norm"]), row(p["b_q_norm"]), row(p["b_sinks"])], axis=0)


def _unpack_repl(a):
    return {"mlp_norm": a[0:2], "kv_norm": a[2], "b_norm": a[3:4], "a_r_k": a[4].reshape(1, N_HEADS, HEAD_DIM),
            "k_norm": a[5, :HEAD_DIM], "b_q_norm": a[6:7, :HEAD_DIM], "b_sinks": a[7:8, :N_HEADS]}


_WEIGHTS = ("a_norm", "a_mix", "a_w_rkv", "a_w0", "a_w1", "a_w2", "a_a0", "a_a1", "a_a2", "a_g1", "a_g2", "a_k_k", "a_k_a",
            "a_r_k", "a_ln_x_w", "a_ln_x_b", "a_w_out", "mlp_norm", "mlp_w_up", "mlp_w_down", "kv_norm", "w_kv", "k_norm",
            "b_norm", "b_w_q", "b_q_norm", "b_sinks", "b_w_o")


def _big_shards(p):
    return [p["a_w_rkv"][0, 0], p["a_w_rkv"][0, 1], p["a_w_rkv"][0, 2], p["a_w_out"][0], p["mlp_w_up"][0], p["mlp_w_up"][1],
            p["mlp_w_down"][0], p["mlp_w_down"][1], p["b_w_q"][0], p["b_w_o"][0]]


_BIG_NAMES = ("a_w_r", "a_w_k", "a_w_v", "a_w_out", "mlp_w_up0", "mlp_w_up1", "mlp_w_down0", "mlp_w_down1", "b_w_q", "b_w_o")


def kernel(x, a_norm, a_mix, a_w_rkv, a_w0, a_w1, a_w2, a_a0, a_a1, a_a2, a_g1, a_g2, a_k_k, a_k_a, a_r_k, a_ln_x_w,
           a_ln_x_b, a_w_out, mlp_norm, mlp_w_up, mlp_w_down, kv_norm, w_kv, k_norm, b_norm, b_w_q, b_q_norm, b_sinks,
           b_w_o, loss_target, m_a_norm, m_a_mix, m_a_w_rkv, m_a_w0, m_a_w1, m_a_w2, m_a_a0, m_a_a1, m_a_a2, m_a_g1,
           m_a_g2, m_a_k_k, m_a_k_a, m_a_r_k, m_a_ln_x_w, m_a_ln_x_b, m_a_w_out, m_mlp_norm, m_mlp_w_up, m_mlp_w_down,
           m_kv_norm, m_w_kv, m_k_norm, m_b_norm, m_b_w_q, m_b_q_norm, m_b_sinks, m_b_w_o, v_a_norm, v_a_mix, v_a_w_rkv,
           v_a_w0, v_a_w1, v_a_w2, v_a_a0, v_a_a1, v_a_a2, v_a_g1, v_a_g2, v_a_k_k, v_a_k_a, v_a_r_k, v_a_ln_x_w,
           v_a_ln_x_b, v_a_w_out, v_mlp_norm, v_mlp_w_up, v_mlp_w_down, v_kv_norm, v_w_kv, v_k_norm, v_b_norm, v_b_w_q,
           v_b_q_norm, v_b_sinks, v_b_w_o):
    given = locals()
    wts = {n: given[n] for n in _WEIGHTS}
    mom = {n: given["m_" + n] for n in _WEIGHTS}
    var = {n: given["v_" + n] for n in _WEIGHTS}

    cols_w, rows_w, repl_w = _pack_cols(wts), _pack_rows(wts), _pack_repl(wts)
    big_w = _big_shards(wts)
    big_bf16 = dict(zip(_BIG_NAMES, [b.astype(BF16) for b in big_w]))
    first_names = [k for k in _BIG_NAMES if k not in _LATE_WEIGHTS]
    gathered = _all_gather([cols_w, rows_w] + [big_bf16[k] for k in first_names], name="gather_weights")
    full_cols = gathered[0].transpose(1, 0, 2).reshape(_COL_ROWS, D_MODEL)
    full_rows = gathered[1].reshape(D_MODEL, _ROW_COLS)
    w = {}
    w.update({k: v.reshape(v.shape[1:]) for k, v in _unpack_cols(full_cols, (1,)).items()})
    for k in ("a_norm", "a_w0", "a_a0", "a_k_k", "a_k_a", "a_ln_x_w", "a_ln_x_b"):
        w[k] = w[k].reshape(1, D_MODEL)
    for k in ("a_w2", "a_a2"):
        w[k] = _pad_to(w[k], LORA_PAD, 0)
    rows_full = _unpack_rows(full_rows, ())
    w["a_w1"], w["a_a1"] = _pad_to(rows_full["a_w1"], LORA_PAD, 1), _pad_to(rows_full["a_a1"], LORA_PAD, 1)
    w["a_g1"], w["w_kv"] = rows_full["a_g1"], rows_full["w_kv"]
    for k, arr in zip(first_names, gathered[2:]):
        w[k] = arr.reshape(N_DEV * arr.shape[1], arr.shape[2])
    w["mlp_norm0"], w["mlp_norm1"] = mlp_norm[0:1], mlp_norm[1:2]
    w["kv_norm"], w["k_norm"] = kv_norm.reshape(1, D_MODEL), k_norm.reshape(1, HEAD_DIM)
    w["b_norm"], w["b_q_norm"], w["b_sinks"], w["a_r_k"] = b_norm, b_q_norm, b_sinks, a_r_k.reshape(1, D_MODEL)

    loss_local, grad_x, g, early_reduced = _local_step(x[0], loss_target[0], w, [big_bf16[k] for k in _LATE_WEIGHTS])
    loss = lax.psum(loss_local[0, 0], MESH_AXES)

    g_lead = {k: g[k][None] for k in ("a_norm", "a_mix", "a_w0", "a_a0", "a_k_k", "a_k_a", "a_ln_x_w", "a_ln_x_b", "a_g2")}
    g_lead["a_w2"], g_lead["a_a2"] = g["a_w2"][None, :96], g["a_a2"][None, :96]
    g_cols = _pack_cols(g_lead).reshape(_COL_ROWS, N_DEV, D_MODEL // N_DEV).transpose(1, 0, 2)
    g_rows = _pack_rows({"a_w1": g["a_w1"][:, :96], "a_a1": g["a_a1"][:, :96], "a_g1": g["a_g1"], "w_kv": g["w_kv"]})
    g_rows = g_rows.reshape(N_DEV, D_MODEL // N_DEV, _ROW_COLS)
    late_names = tuple(k for k in _BIG_NAMES if k not in _EARLY_GRADS)
    late_reduced = _reduce_scatter([g_cols, g_rows] + [_device_major(k, g[k]) for k in late_names],
                                   ("pack_cols", "pack_rows") + late_names, "late")
    big_reduced = dict(zip(late_names, late_reduced[2:]))
    big_reduced.update(zip(_EARLY_GRADS, early_reduced))
    reduced = list(late_reduced[:2]) + [big_reduced[k] for k in _BIG_NAMES]
    g_repl = _pack_repl({"mlp_norm": jnp.concatenate([g["mlp_norm0"], g["mlp_norm1"]], axis=0), "kv_norm": g["kv_norm"],
                         "b_norm": g["b_norm"], "a_r_k": g["a_r_k"], "k_norm": g["k_norm"], "b_q_norm": g["b_q_norm"],
                         "b_sinks": g["b_sinks"]})
    (repl_slots,) = _all_gather([g_repl], name="gather_replicated_grads")

    res = {}
    cols4 = _adamw(cols_w, _pack_cols(mom), _pack_cols(var), reduced[0], name="adamw_cols")
    rows4 = _adamw(rows_w, _pack_rows(mom), _pack_rows(var), reduced[1], name="adamw_rows")
    repl4 = _adamw(repl_w, _pack_repl(mom), _pack_repl(var), repl_slots, name="adamw_replicated")
    for unpacked in ([_unpack_cols(a, (1,)) for a in cols4], [_unpack_rows(a, (1,)) for a in rows4], [_unpack_repl(a) for a in repl4]):
        for k in unpacked[0]:
            res[k] = tuple(u[k] for u in unpacked)
    big4 = [_adamw(bw, bm, bv, slots, name="adamw_" + k)
            for k, bw, bm, bv, slots in zip(_BIG_NAMES, big_w, _big_shards(mom), _big_shards(var), reduced[2:])]
    res["a_w_rkv"] = tuple(jnp.stack([big4[0][i], big4[1][i], big4[2][i]])[None] for i in range(4))
    res["a_w_out"] = tuple(a[None] for a in big4[3])
    res["mlp_w_up"] = tuple(jnp.stack([big4[4][i], big4[5][i]]) for i in range(4))
    res["mlp_w_down"] = tuple(jnp.stack([big4[6][i], big4[7][i]]) for i in range(4))
    res["b_w_q"] = tuple(a[None] for a in big4[8])
    res["b_w_o"] = tuple(a[None] for a in big4[9])
    res["w_kv"] = tuple(a.reshape(w_kv.shape) for a in res["w_kv"])

    outs = [loss, grad_x[None]]
    for i in range(4):
        outs += [res[n][i].reshape(given[n].shape) for n in _WEIGHTS]
    return tuple(outs)
```

```python
import functools
import math

import jax
import jax.numpy as jnp
from jax import lax
from jax.experimental import pallas as pl
from jax.experimental.pallas import tpu as pltpu

F32 = jnp.float32
BF16 = jnp.bfloat16

D_MODEL = 2048
N_HEADS = 32
HEAD_DIM = 64
N_KV_HEADS = 4
Q_PER_KV = 8
ATT_BLOCK = 128
WKV_CHUNK = 64
LORA_PAD = 128
D_FF = 8192
N_DEV = 8
RMS_EPS = 1e-6
GN_EPS = 64e-5
L2_EPS = 1e-12
ROPE_THETA = 10000.0
ADAM_LR, ADAM_B1, ADAM_B2, ADAM_EPS, ADAM_WD, ADAM_STEP = 0.001, 0.9, 0.999, 1e-08, 0.01, 10
MASK_VALUE = -1e30
VMEM_LIMIT_BYTES = 56 * 1024 * 1024
MESH_AXES = ("x", "y", "c")
HI = lax.Precision.HIGHEST

_NN = (((1,), (0,)), ((), ()))
_NT = (((1,), (1,)), ((), ()))
_TN = (((0,), (0,)), ((), ()))
_BNN = (((2,), (1,)), ((0,), (0,)))
_BNT = (((2,), (2,)), ((0,), (0,)))
_BTN = (((1,), (1,)), ((0,), (0,)))


def _params(sem):
    return pltpu.CompilerParams(dimension_semantics=sem, vmem_limit_bytes=VMEM_LIMIT_BYTES)


def _split2(a):
    hi = a.astype(BF16)
    return hi, (a - hi.astype(F32)).astype(BF16)


def _dot3(a, b, dims):
    ah, al = _split2(a)
    bh, bl = _split2(b)
    d = lambda p, q: lax.dot_general(p, q, dims, preferred_element_type=F32)
    return d(ah, bh) + (d(al, bh) + d(ah, bl))


@functools.partial(jax.custom_vjp, nondiff_argnums=(2,))
def _hdot(a, b, dims=_NN):
    return _dot3(a, b, dims)


def _hdot_fwd(a, b, dims):
    return _dot3(a, b, dims), (a, b)


def _hdot_bwd(dims, res, g):
    a, b = res
    nn, nt, tn = (_NN, _NT, _TN) if dims in (_NN, _NT, _TN) else (_BNN, _BNT, _BTN)
    if dims == nn:
        return _dot3(g, b, nt), _dot3(a, g, tn)
    if dims == nt:
        return _dot3(g, b, nn), _dot3(g, a, tn)
    assert dims == tn
    return _dot3(b, g, nt), _dot3(a, g, nn)


_hdot.defvjp(_hdot_fwd, _hdot_bwd)


def _tri_parts(x):
    hi = x.astype(BF16)
    r1 = x - hi.astype(F32)
    mid = r1.astype(BF16)
    return hi, mid, (r1 - mid.astype(F32)).astype(BF16)


@jax.custom_vjp
def _mask_dot(mask, x):
    mb = mask.astype(BF16)
    p0, p1, p2 = _tri_parts(x)
    d = lambda p: lax.dot_general(mb, p, _BNN, preferred_element_type=F32)
    return d(p0) + (d(p1) + d(p2))


def _mask_dot_fwd(mask, x):
    return _mask_dot(mask, x), mask


def _mask_dot_bwd(mask, g):
    mb = mask.astype(BF16)
    p0, p1, p2 = _tri_parts(g)
    d = lambda p: lax.dot_general(mb, p, _BTN, preferred_element_type=F32)
    return jnp.zeros_like(mask), d(p0) + (d(p1) + d(p2))


_mask_dot.defvjp(_mask_dot_fwd, _mask_dot_bwd)


def _b16dot(a, b, dims):
    return lax.dot_general(a.astype(BF16), b.astype(BF16), dims, preferred_element_type=F32)


@jax.custom_vjp
def _bdot(a, b):
    return _b16dot(a, b, _NN)


def _bdot_fwd(a, b):
    return _b16dot(a, b, _NN), (a, b)


def _bdot_bwd(res, g):
    a, b = res
    return _b16dot(g, b, _NT), _b16dot(a, g, _TN)


_bdot.defvjp(_bdot_fwd, _bdot_bwd)


@jax.custom_vjp
def _bdot_nt(a, b):
    return _b16dot(a, b, _NT)


def _bdot_nt_fwd(a, b):
    return _b16dot(a, b, _NT), (a, b)


def _bdot_nt_bwd(res, g):
    a, b = res
    return _b16dot(g, b, _NN), _b16dot(g, a, _TN)


_bdot_nt.defvjp(_bdot_nt_fwd, _bdot_nt_bwd)


def _rms(x, gain):
    return x * lax.rsqrt(jnp.mean(x * x, axis=-1, keepdims=True) + RMS_EPS) * gain


def _vjp_of(f, n_in, diff):
    def g(*args):
        ins, cts = args[:n_in], args[n_in:]

        def fd(*d):
            full = list(ins)
            for pos, i in enumerate(diff):
                full[i] = d[pos]
            return f(*full)

        _, pull = jax.vjp(fd, *[ins[i] for i in diff])
        return pull(tuple(cts))
    return g


def _mm(a, b, *, name, ta=False, tb=False, a_pro=None, epi=None, epi_args=(), out_dtype=F32,
        tm=1024, tn=1024, tk=2048, dims=None, b_spec=None, o_spec=None, o_shape=None):
    if dims is None:
        m, k = (a.shape[1], a.shape[0]) if ta else a.shape
        n = b.shape[0] if tb else b.shape[1]
    else:
        m, n, k = dims
    tm, tn, tk = min(tm, m), min(tn, n), min(tk, k)
    assert m % tm == 0 and n % tn == 0 and k % tk == 0, (name, m, n, k, tm, tn, tk)
    nk = k // tk
    ne = len(epi_args)
    cdims = (((0 if ta else 1,), (1 if tb else 0,)), ((), ()))

    def body(a_ref, b_ref, *rest):
        e_refs, o_ref, acc = rest[:ne], rest[ne], rest[ne + 1]
        kk = pl.program_id(2)

        @pl.when(kk == 0)
        def _():
            acc[...] = jnp.zeros_like(acc)

        av = a_ref[...]
        if a_pro is not None:
            av = a_pro(av.astype(F32))
        acc[...] += lax.dot_general(av.astype(BF16), b_ref[...].astype(BF16), cdims, preferred_element_type=F32)

        @pl.when(kk == nk - 1)
        def _():
            r = acc[...]
            if epi is not None:
                r = epi(r, *[e[...] for e in e_refs])
            o_ref[...] = r.astype(o_ref.dtype)

    a_spec = pl.BlockSpec((tk, tm), lambda i, j, q: (q, i)) if ta else pl.BlockSpec((tm, tk), lambda i, j, q: (i, q))
    if b_spec is None:
        b_spec = pl.BlockSpec((tn, tk), lambda i, j, q: (j, q)) if tb else pl.BlockSpec((tk, tn), lambda i, j, q: (q, j))
    if o_spec is None:
        o_spec = pl.BlockSpec((tm, tn), lambda i, j, q: (i, j))
        o_shape = (m, n)
    e_specs = [pl.BlockSpec((tm, tn), lambda i, j, q: (i, j)) for _ in epi_args]
    return pl.pallas_call(
        body, grid=(m // tm, n // tn, nk), in_specs=[a_spec, b_spec] + e_specs, out_specs=o_spec,
        out_shape=jax.ShapeDtypeStruct(o_shape, out_dtype), scratch_shapes=[pltpu.VMEM((tm, tn), F32)],
        compiler_params=_params(("parallel", "parallel", "arbitrary")), name=name,
    )(a, b, *epi_args)


def _rowwise(fn, rows, params, out_rows, out_params=(), *, tm=256, name):
    t = rows[0].shape[0]
    tm = min(tm, t)
    assert t % tm == 0
    nr, npar, nor, nop = len(rows), len(params), len(out_rows), len(out_params)

    def body(*refs):
        r, p = refs[:nr], refs[nr:nr + npar]
        o, op = refs[nr + npar:nr + npar + nor], refs[nr + npar + nor:]
        outs = fn(*[x[...] for x in r], *[x[...] for x in p])
        for ref, val in zip(o, outs[:nor]):
            ref[...] = val.astype(ref.dtype)
        if nop:
            @pl.when(pl.program_id(0) == 0)
            def _():
                for ref in op:
                    ref[...] = jnp.zeros_like(ref)

            for ref, val in zip(op, outs[nor:]):
                ref[...] += val.astype(F32)

    in_specs = [pl.BlockSpec((tm, x.shape[1]), lambda i: (i, 0)) for x in rows]
    in_specs += [pl.BlockSpec(p.shape, lambda i: (0, 0)) for p in params]
    out_specs = [pl.BlockSpec((tm, s.shape[1]), lambda i: (i, 0)) for s in out_rows]
    out_specs += [pl.BlockSpec(s.shape, lambda i: (0, 0)) for s in out_params]
    return pl.pallas_call(
        body, grid=(t // tm,), in_specs=in_specs, out_specs=out_specs, out_shape=list(out_rows) + list(out_params),
        compiler_params=_params(("arbitrary",)), name=name,
    )(*rows, *params)


def _sds(shape, dtype=F32):
    return jax.ShapeDtypeStruct(tuple(shape), dtype)


def _wkv_chunk(s0, r, lw, k, v, a, b):
    nb, c, _ = r.shape
    ti = lax.broadcasted_iota(jnp.int32, (nb, c, c), 1)
    si = lax.broadcasted_iota(jnp.int32, (nb, c, c), 2)
    incl, strict = si <= ti, si < ti
    cum = _mask_dot(incl.astype(F32), lw)
    rcum = _mask_dot((si > ti).astype(F32), lw)
    tot = jnp.sum(lw, axis=1, keepdims=True)
    w_inv = jnp.exp(-cum)
    at, rt, bt, kt = a * jnp.exp(cum - lw), r * jnp.exp(cum), b * w_inv, k * w_inv
    l_ab = jnp.where(strict, _hdot(at, bt, _BNT), 0.0)
    l_ak = jnp.where(strict, _hdot(at, kt, _BNT), 0.0)
    t_rb = jnp.where(incl, _hdot(rt, bt, _BNT), 0.0)
    t_rk = jnp.where(incl, _hdot(rt, kt, _BNT), 0.0)
    u = _hdot(at, s0, _BNT) + _hdot(l_ak, v, _BNN)
    p = l_ab
    n_it = int(math.log2(c))
    for it in range(n_it):
        u = u + _hdot(p, u, _BNN)
        if it + 1 < n_it:
            p = _hdot(p, p, _BNN)
    y = _hdot(rt, s0, _BNT) + _hdot(t_rb, u, _BNN) + _hdot(t_rk, v, _BNN)
    e = jnp.exp(rcum)
    s1 = s0 * jnp.exp(tot) + _hdot(u, b * e, _BTN) + _hdot(v, k * e, _BTN)
    return y, s1


WKV_HEADS_PER_STEP = 8


def _first_and_last_step(grid):
    i, j = pl.program_id(0), pl.program_id(1)
    return jnp.logical_and(i == 0, j == 0), jnp.logical_and(i == grid[0] - 1, j == grid[1] - 1)


N_WKV_PARAMS = 7


def _tmix_chunk(s0, r, k, v, wl, al, w0, a0, k_k, k_a, ln_w, ln_b, r_k):
    lw, kmod, a, b = _wkv_prep(k, wl, al, w0, a0, k_k, k_a)
    y, s1 = _wkv_chunk(s0, r, lw, kmod, v, a, b)
    (y2,) = _wkv_post(y, r, kmod, v, ln_w, ln_b, r_k)
    return y2, s1


def _wkv_fwd(r, k, v, wl, al, params, shards=()):
    nh, t, n = r.shape
    nc = t // WKV_CHUNK
    hb = WKV_HEADS_PER_STEP
    grid = (nh // hb, nc)
    ns = len(shards)
    n_in = 5 + N_WKV_PARAMS

    def body(*refs):
        y_ref, s_ref = refs[n_in + ns:n_in + ns + 2]
        state = refs[n_in + 2 * ns + 2]
        if ns:
            start, finish = _gather_plan(refs[n_in:n_in + ns], refs[n_in + ns + 2:n_in + 2 * ns + 2], *refs[n_in + 2 * ns + 3:])
            first, last = _first_and_last_step(grid)
            pl.when(first)(start)

        @pl.when(pl.program_id(1) == 0)
        def _():
            state[...] = jnp.zeros_like(state)

        s0 = state[...]
        s_ref[:, 0] = s0
        y, s1 = _tmix_chunk(s0, *[ref[...] for ref in refs[:n_in]])
        y_ref[...] = y
        state[...] = s1
        if ns:
            pl.when(last)(finish)

    blk = pl.BlockSpec((hb, WKV_CHUNK, n), lambda h, c: (h, c, 0))
    pblk = pl.BlockSpec((hb, 1, n), lambda h, c: (h, 0, 0))
    sblk = pl.BlockSpec((hb, 1, n, n), lambda h, c: (h, c, 0, 0))
    outs = pl.pallas_call(
        body, grid=grid, in_specs=[blk] * 5 + [pblk] * N_WKV_PARAMS + [_ANY] * ns, out_specs=[blk, sblk] + [_ANY] * ns,
        out_shape=[_sds((nh, t, n)), _sds((nh, nc, n, n))] + [_sds((N_DEV,) + s.shape, s.dtype) for s in shards],
        scratch_shapes=[pltpu.VMEM((hb, n, n), F32)] + (_gather_semaphores(ns) if ns else []),
        compiler_params=_params(("arbitrary", "arbitrary")), name="wkv_fwd",
    )(r, k, v, wl, al, *params, *shards)
    return outs[0], outs[1], list(outs[2:])


def _wkv_bwd(r, k, v, wl, al, params, states, dy, chip_sums=()):
    nh, t, n = r.shape
    nc = t // WKV_CHUNK
    hb = WKV_HEADS_PER_STEP
    grid = (nh // hb, nc)
    ns = len(chip_sums)
    n_in = 5 + N_WKV_PARAMS
    n_out = 5 + N_WKV_PARAMS

    def body(*refs):
        s_ref, dy_ref = refs[n_in:n_in + 2]
        out_refs = refs[n_in + 2 + ns:n_in + 2 + ns + n_out]
        dstate = refs[n_in + 2 + 2 * ns + n_out]
        if ns:
            start, finish = _chip_exchange_plan(refs[n_in + 2:n_in + 2 + ns], refs[n_in + 2 + ns + n_out:n_in + 2 + 2 * ns + n_out],
                                                *refs[n_in + 3 + 2 * ns + n_out:])
            first, last = _first_and_last_step(grid)
            pl.when(first)(start)

        @pl.when(pl.program_id(1) == 0)
        def _():
            dstate[...] = jnp.zeros_like(dstate)
            for ref in out_refs[5:]:
                ref[...] = jnp.zeros_like(ref)

        _, pull = jax.vjp(_tmix_chunk, s_ref[:, 0], *[ref[...] for ref in refs[:n_in]])
        grads = pull((dy_ref[...], dstate[...]))
        dstate[...] = grads[0]
        for ref, val in zip(out_refs[:5], grads[1:6]):
            ref[...] = val.astype(ref.dtype)
        for ref, val in zip(out_refs[5:], grads[6:]):
            ref[...] += val
        if ns:
            pl.when(last)(finish)

    blk = pl.BlockSpec((hb, WKV_CHUNK, n), lambda h, c: (h, nc - 1 - c, 0))
    pblk = pl.BlockSpec((hb, 1, n), lambda h, c: (h, 0, 0))
    sblk = pl.BlockSpec((hb, 1, n, n), lambda h, c: (h, nc - 1 - c, 0, 0))
    outs = pl.pallas_call(
        body, grid=grid, in_specs=[blk] * 5 + [pblk] * N_WKV_PARAMS + [sblk, blk] + [_ANY] * ns,
        out_specs=[blk] * 5 + [pblk] * N_WKV_PARAMS + [_ANY] * ns,
        out_shape=[_sds((nh, t, n), BF16)] * 5 + [_sds((nh, 1, n))] * N_WKV_PARAMS + [_sds(p.shape, p.dtype) for p in chip_sums],
        scratch_shapes=[pltpu.VMEM((hb, n, n), F32)] + (_chip_exchange_semaphores(ns) if ns else []),
        compiler_params=_params(("arbitrary", "arbitrary")), name="wkv_bwd",
    )(r, k, v, wl, al, *params, states, dy, *chip_sums)
    return outs[:5], outs[5:n_out], list(outs[n_out:])


def _wkv_prep(k, wl, al, w0, a0, k_k, k_a):
    z = -(w0 + wl)
    softplus = jnp.maximum(z, 0.0) + jnp.log1p(jnp.exp(-jnp.abs(z)))
    lw = -jnp.exp(-softplus - 0.5)
    asig = jax.nn.sigmoid(a0 + al)
    kk = k * k_k
    kk = kk / jnp.maximum(jnp.sqrt(jnp.sum(kk * kk, axis=-1, keepdims=True)), L2_EPS)
    kmod = k * (1.0 + (asig - 1.0) * k_a)
    return lw, kmod, -kk, kk * asig


def _wkv_post(y, r, kmod, v, ln_w, ln_b, r_k):
    mu = jnp.mean(y, axis=-1, keepdims=True)
    var = jnp.mean(jnp.square(y - mu), axis=-1, keepdims=True)
    yn = (y - mu) * lax.rsqrt(var + GN_EPS)
    yn = yn * ln_w + ln_b
    return (yn + jnp.sum(r * kmod * r_k, axis=-1, keepdims=True) * v,)


def _attn_block(nonzero_block, qs, kc, kp, vc, vp, cos_c, sin_c, cos_p, sin_p, q_gain, k_gain, sinks):
    ri = lax.broadcasted_iota(jnp.int32, (HEAD_DIM, HEAD_DIM), 0)
    ci = lax.broadcasted_iota(jnp.int32, (HEAD_DIM, HEAD_DIM), 1)
    half = HEAD_DIM // 2
    rot = jnp.where(ri == ci + half, -1.0, 0.0) + jnp.where(ri + half == ci, 1.0, 0.0)

    def rope(x, cos, sin):
        return x * cos + _hdot(x, rot, _NN) * sin

    kcr = rope(_rms(kc, k_gain), cos_c, sin_c)
    kpr = rope(_rms(kp, k_gain), cos_p, sin_p)
    qi = lax.broadcasted_iota(jnp.int32, (ATT_BLOCK, ATT_BLOCK), 0)
    ki = lax.broadcasted_iota(jnp.int32, (ATT_BLOCK, ATT_BLOCK), 1)
    mask_c = ki <= qi
    mask_p = jnp.logical_and(ki > qi, nonzero_block)
    lane0 = (lax.broadcasted_iota(jnp.int32, (1, 128), 1) == 0).astype(F32)
    outs = []
    for g in range(Q_PER_KV):
        q = rope(_rms(qs[g], q_gain), cos_c, sin_c)
        sc = jnp.where(mask_c, _bdot_nt(q, kcr) * (HEAD_DIM ** -0.5), MASK_VALUE)
        sp = jnp.where(mask_p, _bdot_nt(q, kpr) * (HEAD_DIM ** -0.5), MASK_VALUE)
        sk = jnp.sum(sinks[g] * lane0, axis=1, keepdims=True)
        mx = jnp.maximum(jnp.maximum(jnp.max(sc, axis=1, keepdims=True), jnp.max(sp, axis=1, keepdims=True)), sk)
        mx = lax.stop_gradient(mx)
        ec, ep = jnp.exp(sc - mx), jnp.exp(sp - mx)
        den = jnp.sum(ec, axis=1, keepdims=True) + jnp.sum(ep, axis=1, keepdims=True) + jnp.exp(sk - mx)
        outs.append(_bdot(ec / den, vc) + _bdot(ep / den, vp))
    return tuple(outs)


def _attn_specs(t):
    nb = t // ATT_BLOCK
    prev = lambda n: jnp.maximum(n - 1, 0)
    q_spec = pl.BlockSpec((Q_PER_KV, ATT_BLOCK, HEAD_DIM), lambda g, n: (g, n, 0))
    kv_c = pl.BlockSpec((None, ATT_BLOCK, HEAD_DIM), lambda g, n: (g, n, 0))
    kv_p = pl.BlockSpec((None, ATT_BLOCK, HEAD_DIM), lambda g, n: (g, prev(n), 0))
    tab_c = pl.BlockSpec((ATT_BLOCK, HEAD_DIM), lambda g, n: (n, 0))
    tab_p = pl.BlockSpec((ATT_BLOCK, HEAD_DIM), lambda g, n: (prev(n), 0))
    gain = pl.BlockSpec((1, HEAD_DIM), lambda g, n: (0, 0))
    sink = pl.BlockSpec((Q_PER_KV, 1, 128), lambda g, n: (g, 0, 0))
    return nb, q_spec, kv_c, kv_p, tab_c, tab_p, gain, sink


def _attn_fwd(q, k, v, cos, sin, q_gain, k_gain, sinks):
    t = q.shape[1]
    nb, q_spec, kv_c, kv_p, tab_c, tab_p, gain, sink = _attn_specs(t)

    def body(q_ref, kc, kp, vc, vp, cc, sc, cp, sp, qg, kg, sk, o_ref):
        outs = _attn_block(pl.program_id(1) > 0, [q_ref[g] for g in range(Q_PER_KV)], kc[...], kp[...], vc[...], vp[...],
                           cc[...], sc[...], cp[...], sp[...], qg[...], kg[...], [sk[g] for g in range(Q_PER_KV)])
        for g in range(Q_PER_KV):
            o_ref[g] = outs[g]

    return pl.pallas_call(
        body, grid=(N_KV_HEADS, nb), in_specs=[q_spec, kv_c, kv_p, kv_c, kv_p, tab_c, tab_c, tab_p, tab_p, gain, gain, sink],
        out_specs=q_spec, out_shape=_sds(q.shape), compiler_params=_params(("parallel", "arbitrary")), name="attn_fwd",
    )(q, k, k, v, v, cos, sin, cos, sin, q_gain, k_gain, sinks)


def _attn_bwd(q, k, v, cos, sin, q_gain, k_gain, sinks, do):
    t = q.shape[1]
    nb, q_spec, kv_c, kv_p, tab_c, tab_p, gain, sink = _attn_specs(t)
    nq = Q_PER_KV

    def body(q_ref, kc, kp, vc, vp, cc, sc, cp, sp, qg, kg, sk, do_ref,
             dq_ref, dkc_ref, dkp_ref, dvc_ref, dvp_ref, dqg_ref, dkg_ref, dsk_ref):
        nonzero = pl.program_id(1) > 0
        tabs = (cc[...], sc[...], cp[...], sp[...])

        def f(qs, kcv, kpv, vcv, vpv, qgv, kgv, sks):
            return _attn_block(nonzero, qs, kcv, kpv, vcv, vpv, *tabs, qgv, kgv, sks)

        _, pull = jax.vjp(f, [q_ref[g] for g in range(nq)], kc[...], kp[...], vc[...], vp[...], qg[...], kg[...],
                          [sk[g] for g in range(nq)])
        dqs, dkc, dkp, dvc, dvp, dqg, dkg, dsks = pull(tuple(do_ref[g] for g in range(nq)))
        for g in range(nq):
            dq_ref[g] = dqs[g]
        dkc_ref[...], dkp_ref[...], dvc_ref[...], dvp_ref[...] = dkc, dkp, dvc, dvp
        first = jnp.logical_and(pl.program_id(0) == 0, pl.program_id(1) == 0)

        @pl.when(first)
        def _():
            dqg_ref[...] = jnp.zeros_like(dqg_ref)
            dkg_ref[...] = jnp.zeros_like(dkg_ref)

        @pl.when(pl.program_id(1) == 0)
        def _():
            dsk_ref[...] = jnp.zeros_like(dsk_ref)

        dqg_ref[...] += dqg
        dkg_ref[...] += dkg
        for g in range(nq):
            dsk_ref[g] += dsks[g]

    kv_shape = _sds(k.shape)
    return pl.pallas_call(
        body, grid=(N_KV_HEADS, nb),
        in_specs=[q_spec, kv_c, kv_p, kv_c, kv_p, tab_c, tab_c, tab_p, tab_p, gain, gain, sink, q_spec],
        out_specs=[q_spec, kv_c, kv_c, kv_c, kv_c, gain, gain, sink],
        out_shape=[_sds(q.shape), kv_shape, kv_shape, kv_shape, kv_shape, _sds((1, HEAD_DIM)), _sds((1, HEAD_DIM)), _sds(sinks.shape)],
        compiler_params=_params(("arbitrary", "arbitrary")), name="attn_bwd",
    )(q, k, k, v, v, cos, sin, cos, sin, q_gain, k_gain, sinks, do)


def _time_shift_lerps(x, xs, gain, *mix):
    xn, xsn = _rms(x, gain), _rms(xs, gain)
    xx = xsn - xn
    return tuple(xn + xx * m for m in mix)


def _residual_norm(h, delta, gain):
    hn = h + delta
    return hn, _rms(hn, gain)


def _residual_norm2(h, delta, gain_a, gain_b):
    hn = h + delta
    return hn, _rms(hn, gain_a), _rms(hn, gain_b)


def _relu2(u):
    return jnp.square(jnp.maximum(u, 0.0))


def _sigmoid(z):
    return jax.nn.sigmoid(z)


def _to_heads(x2d, nh):
    t = x2d.shape[0]
    return x2d.reshape(t, nh, HEAD_DIM).transpose(1, 0, 2)


def _from_heads(xh):
    nh, t, n = xh.shape
    return xh.transpose(1, 0, 2).reshape(t, nh * n)


def _shift_down(x):
    return jnp.pad(x[:-1], ((1, 0), (0, 0)))


def _shift_up(x):
    return jnp.pad(x[1:], ((0, 1), (0, 0)))


def _rope_tables(t):
    half = HEAD_DIM // 2
    inv_freq = jnp.power(ROPE_THETA, -jnp.arange(half, dtype=F32) / half)
    ang = jnp.arange(t, dtype=jnp.int32).astype(F32)[:, None] * inv_freq[None, :]
    cos, sin = jnp.cos(ang), jnp.sin(ang)
    return jnp.concatenate([cos, cos], axis=1), jnp.concatenate([sin, sin], axis=1)


def _mlp_fwd(hn, w_up, w_down, layer, up_dev_major):
    t = hn.shape[0]
    if up_dev_major:
        cw = w_up.shape[2]
        u = _mm(hn, w_up, name=f"mlp{layer}_up", dims=(t, D_FF, D_MODEL), tn=cw, tk=D_MODEL,
                b_spec=pl.BlockSpec((None, D_MODEL, cw), lambda i, j, q: (j, q, 0)))
    else:
        u = _mm(hn, w_up, name=f"mlp{layer}_up")
    out = _mm(u, w_down, a_pro=_relu2, name=f"mlp{layer}_down")
    return u, out


def _mlp_bwd(hn, u, dh, w_up, w_down, layer, up_dev_major):
    t = hn.shape[0]
    du = _mm(dh, w_down, tb=True, epi=lambda r, uu: r * (2.0 * jnp.maximum(uu, 0.0)), epi_args=(u,), out_dtype=BF16,
             name=f"mlp{layer}_du")
    d_down = _mm(u, dh, ta=True, a_pro=_relu2, name=f"mlp{layer}_ddown")
    if up_dev_major:
        cw = w_up.shape[2]
        d_up = _mm(hn, du, ta=True, name=f"mlp{layer}_dup", dims=(D_MODEL, D_FF, t), tn=cw,
                   o_spec=pl.BlockSpec((None, 1024, cw), lambda i, j, q: (j, i, 0)), o_shape=(N_DEV, D_MODEL, cw))
        dhn = _mm(du, w_up, tb=True, name=f"mlp{layer}_dhn", dims=(t, D_MODEL, D_FF), tk=cw,
                  b_spec=pl.BlockSpec((None, 1024, cw), lambda i, j, q: (q, j, 0)))
    else:
        d_up = _mm(hn, du, ta=True, name=f"mlp{layer}_dup")
        dhn = _mm(du, w_up, tb=True, name=f"mlp{layer}_dhn")
    return dhn, d_up, d_down


_LATE_WEIGHTS = ("mlp_w_up0", "mlp_w_up1", "mlp_w_down0", "mlp_w_down1", "b_w_q", "b_w_o")
_EARLY_GRADS = _LATE_WEIGHTS + ("a_w_out",)


def _local_step(x, target, w, late_shards=None, up_dev_major=True):
    t = x.shape[0]
    g = {}
    w = dict(w)
    row = lambda: _sds((t, D_MODEL))
    rowb = lambda: _sds((t, D_MODEL), BF16)
    vec = lambda: _sds((1, D_MODEL))

    xs = _shift_down(x)
    mix = [w["a_mix"][i:i + 1] for i in range(6)]
    xr, xk, xv, xw, xa, xg = _rowwise(_time_shift_lerps, [x, xs], [w["a_norm"]] + mix, [rowb()] * 6, tm=256, name="tmix_lerp")
    r = _mm(xr, w["a_w_r"], name="tmix_r")
    k = _mm(xk, w["a_w_k"], name="tmix_k")
    v = _mm(xv, w["a_w_v"], name="tmix_v")
    lw1 = _mm(xw, w["a_w1"], name="tmix_w1")
    wl = _mm(lw1, w["a_w2"], a_pro=jnp.tanh, name="tmix_w2")
    la1 = _mm(xa, w["a_a1"], name="tmix_a1")
    al = _mm(la1, w["a_a2"], name="tmix_a2")
    lg1 = _mm(xg, w["a_g1"], name="tmix_g1")
    gate = _mm(lg1, w["a_g2"], a_pro=_sigmoid, name="tmix_g2")

    hv = lambda name: w[name].reshape(N_HEADS, 1, HEAD_DIM)
    rh, kh, vh, wlh, alh = [_to_heads(z, N_HEADS) for z in (r, k, v, wl, al)]
    wkv_params = [hv(name) for name in ("a_w0", "a_a0", "a_k_k", "a_k_a", "a_ln_x_w", "a_ln_x_b", "a_r_k")]
    y2h, states, gathered = _wkv_fwd(rh, kh, vh, wlh, alh, wkv_params, late_shards or ())
    for name, arr in zip(_LATE_WEIGHTS, gathered):
        w[name] = arr if name.startswith("mlp_w_up") else arr.reshape(N_DEV * arr.shape[1], arr.shape[2])
    y2 = _from_heads(y2h)
    (yg,) = _rowwise(lambda a, b: (a * b,), [y2, gate], [], [rowb()], name="tmix_gate")
    att = _mm(yg, w["a_w_out"], name="tmix_out")

    h1, hn0 = _rowwise(_residual_norm, [x, att], [w["mlp_norm0"]], [row(), rowb()], name="res_norm0")
    u0, m0 = _mlp_fwd(hn0, w["mlp_w_up0"], w["mlp_w_down0"], 0, up_dev_major)

    h2, kvn, qn = _rowwise(_residual_norm2, [h1, m0], [w["kv_norm"], w["b_norm"]], [row(), rowb(), rowb()], name="res_norm_kvq")
    kv = _mm(kvn, w["w_kv"], name="kv_proj")
    q = _mm(qn, w["b_w_q"], name="q_proj")
    half_kv = N_KV_HEADS * HEAD_DIM
    ksh, vsh = _to_heads(kv[:, :half_kv], N_KV_HEADS), _to_heads(kv[:, half_kv:], N_KV_HEADS)
    qh = _to_heads(q, N_HEADS)
    cos, sin = _rope_tables(t)
    sinks = jnp.broadcast_to(w["b_sinks"].reshape(N_HEADS, 1, 1), (N_HEADS, 1, 128))
    oh = _attn_fwd(qh, ksh, vsh, cos, sin, w["b_q_norm"], w["k_norm"], sinks)
    o = _from_heads(oh).astype(BF16)
    att2 = _mm(o, w["b_w_o"], name="attn_out")

    h3, hn1 = _rowwise(_residual_norm, [h2, att2], [w["mlp_norm1"]], [row(), rowb()], name="res_norm1")
    u1, m1 = _mlp_fwd(hn1, w["mlp_w_up1"], w["mlp_w_down1"], 1, up_dev_major)

    def loss_fn(h, m, tg):
        diff = (h + m) - tg
        part = 0.5 * jnp.sum(jnp.mean(jnp.square(diff), axis=-1, keepdims=True), axis=0, keepdims=True)
        return diff * (1.0 / D_MODEL), jnp.broadcast_to(part, (1, 128))

    dh4, loss = _rowwise(loss_fn, [h3, m1, target], [], [row()], [_sds((1, 128))], name="loss")

    def res_norm_bwd(h, dnext, dhn, gain):
        dh, dgain = _vjp_of(lambda hh, gg: (_rms(hh, gg),), 2, (0, 1))(h, gain, dhn)
        return dnext + dh, dgain

    dhn1, g["mlp_w_up1"], g["mlp_w_down1"] = _mlp_bwd(hn1, u1, dh4, w["mlp_w_up1"], w["mlp_w_down1"], 1, up_dev_major)
    dh3, g["mlp_norm1"] = _rowwise(res_norm_bwd, [h3, dh4, dhn1], [w["mlp_norm1"]], [row()], [vec()], name="res_norm1_bwd")

    g["b_w_o"] = _mm(o, dh3, ta=True, name="attn_out_dw")
    do = _mm(dh3, w["b_w_o"], tb=True, name="attn_out_dx")
    dqh, dkc, dkp, dvc, dvp, g["b_q_norm"], g["k_norm"], dsinks = _attn_bwd(
        qh, ksh, vsh, cos, sin, w["b_q_norm"], w["k_norm"], sinks, _to_heads(do, N_HEADS))
    g["b_sinks"] = dsinks[:, 0, 0].reshape(1, N_HEADS)
    dq = _from_heads(dqh)
    g["b_w_q"] = _mm(qn, dq, ta=True, name="q_proj_dw")
    dqn = _mm(dq, w["b_w_q"], tb=True, name="q_proj_dx")
    shift_blk = lambda z: jnp.pad(z[:, ATT_BLOCK:], ((0, 0), (0, ATT_BLOCK), (0, 0)))
    kv_parts = [_from_heads(z) for z in (dkc, shift_blk(dkp), dvc, shift_blk(dvp))]
    (dkv,) = _rowwise(lambda a, b, c, d: (jnp.concatenate([a + b, c + d], axis=1),), kv_parts, [],
                      [_sds((t, 2 * half_kv), BF16)], name="kv_grad_sum")
    g["w_kv"] = _mm(kvn, dkv, ta=True, name="kv_proj_dw")
    dkvn = _mm(dkv, w["w_kv"], tb=True, name="kv_proj_dx")

    def res_norm2_bwd(h, dnext, dna, dnb, gain_a, gain_b):
        dha, dga = _vjp_of(lambda hh, gg: (_rms(hh, gg),), 2, (0, 1))(h, gain_a, dna)
        dhb, dgb = _vjp_of(lambda hh, gg: (_rms(hh, gg),), 2, (0, 1))(h, gain_b, dnb)
        return dnext + dha + dhb, dga, dgb

    dh2, g["kv_norm"], g["b_norm"] = _rowwise(res_norm2_bwd, [h2, dh3, dkvn, dqn], [w["kv_norm"], w["b_norm"]],
                                              [row()], [vec(), vec()], name="res_norm_kvq_bwd")

    dhn0, g["mlp_w_up0"], g["mlp_w_down0"] = _mlp_bwd(hn0, u0, dh2, w["mlp_w_up0"], w["mlp_w_down0"], 0, up_dev_major)
    dh1, g["mlp_norm0"] = _rowwise(res_norm_bwd, [h1, dh2, dhn0], [w["mlp_norm0"]], [row()], [vec()], name="res_norm0_bwd")

    g["a_w_out"] = _mm(yg, dh1, ta=True, name="tmix_out_dw")
    dyg = _mm(dh1, w["a_w_out"], tb=True, name="tmix_out_dx")
    dy2, dgate = _rowwise(lambda d, a, b: (d * b, d * a), [dyg, y2, gate], [], [row(), rowb()], name="tmix_gate_bwd")
    g["a_g2"] = _mm(lg1, dgate, ta=True, a_pro=_sigmoid, name="tmix_g2_dw")

    def dsigmoid(rr, z):
        s = jax.nn.sigmoid(z)
        return rr * s * (1.0 - s)

    dlg1 = _mm(dgate, w["a_g2"], tb=True, epi=dsigmoid, epi_args=(lg1,), out_dtype=BF16, name="tmix_g2_dx")
    g["a_g1"] = _mm(xg, dlg1, ta=True, name="tmix_g1_dw")
    dxg = _mm(dlg1, w["a_g1"], tb=True, name="tmix_g1_dx")

    early_sums = ()
    if late_shards is not None:
        early_sums = _chip_sums([_device_major(name, g[name]) for name in _EARLY_GRADS], _EARLY_GRADS, "early")
    head_grads, param_grads, early_reduced = _wkv_bwd(rh, kh, vh, wlh, alh, wkv_params, states, _to_heads(dy2, N_HEADS), early_sums)
    for name, pg in zip(("a_w0", "a_a0", "a_k_k", "a_k_a", "a_ln_x_w", "a_ln_x_b", "a_r_k"), param_grads):
        g[name] = pg.reshape(1, D_MODEL)
    dr, dk, dv, dwl, dal = [_from_heads(z) for z in head_grads]

    g["a_w_r"] = _mm(xr, dr, ta=True, name="tmix_r_dw")
    g["a_w_k"] = _mm(xk, dk, ta=True, name="tmix_k_dw")
    g["a_w_v"] = _mm(xv, dv, ta=True, name="tmix_v_dw")
    dxr = _mm(dr, w["a_w_r"], tb=True, name="tmix_r_dx")
    dxk = _mm(dk, w["a_w_k"], tb=True, name="tmix_k_dx")
    dxv = _mm(dv, w["a_w_v"], tb=True, name="tmix_v_dx")
    g["a_w2"] = _mm(lw1, dwl, ta=True, a_pro=jnp.tanh, name="tmix_w2_dw")

    def dtanh(rr, z):
        th = jnp.tanh(z)
        return rr * (1.0 - th * th)

    dlw1 = _mm(dwl, w["a_w2"], tb=True, epi=dtanh, epi_args=(lw1,), out_dtype=BF16, name="tmix_w2_dx")
    g["a_w1"] = _mm(xw, dlw1, ta=True, name="tmix_w1_dw")
    dxw = _mm(dlw1, w["a_w1"], tb=True, name="tmix_w1_dx")
    g["a_a2"] = _mm(la1, dal, ta=True, name="tmix_a2_dw")
    dla1 = _mm(dal, w["a_a2"], tb=True, out_dtype=BF16, name="tmix_a2_dx")
    g["a_a1"] = _mm(xa, dla1, ta=True, name="tmix_a1_dw")
    dxa = _mm(dla1, w["a_a1"], tb=True, name="tmix_a1_dx")

    lerp_bwd = _vjp_of(_time_shift_lerps, 9, tuple(range(9)))

    def lerp_bwd_rows(x_, xs_, d0, d1, d2, d3, d4, d5, gain, *mx):
        return lerp_bwd(x_, xs_, gain, *mx, d0, d1, d2, d3, d4, d5)

    outs = _rowwise(lerp_bwd_rows, [x, xs, dxr, dxk, dxv, dxw, dxa, dxg], [w["a_norm"]] + mix, [row(), row()], [vec()] * 7,
                    tm=128, name="tmix_lerp_bwd")
    dx_a, dxs, g["a_norm"] = outs[0], outs[1], outs[2]
    g["a_mix"] = jnp.concatenate(outs[3:9], axis=0)
    (grad_x,) = _rowwise(lambda a, b, c: (a + b + c,), [dh1, dx_a, _shift_up(dxs)], [], [row()], name="grad_x_sum")
    return loss, grad_x, g, (early_reduced if late_shards is not None else None)


def _device_major(name, grad):
    return grad if name.startswith("mlp_w_up") else grad.reshape((N_DEV, grad.shape[0] // N_DEV, grad.shape[1]))


_ANY = pl.BlockSpec(memory_space=pl.ANY)
_MESH_ID = pl.DeviceIdType.MESH
N_PEERS = N_DEV - 1


def _linear(pos):
    return 4 * pos[0] + 2 * pos[1] + pos[2]


def _all_gather(shards, name):
    n = len(shards)

    def body(*refs):
        start, finish = _gather_plan(refs[:n], refs[n:2 * n], *refs[2 * n:])
        start()
        finish()

    return pl.pallas_call(
        body, out_shape=[_sds((N_DEV,) + s.shape, s.dtype) for s in shards], in_specs=[_ANY] * n, out_specs=[_ANY] * n,
        scratch_shapes=_gather_semaphores(n), name=name,
    )(*shards)


def _gather_semaphores(n):
    return [pltpu.SemaphoreType.DMA((n * N_PEERS,)), pltpu.SemaphoreType.DMA((n * N_PEERS,)), pltpu.SemaphoreType.DMA((n,))]


def _gather_plan(ins, outs, send_sems, recv_sems, local_sems):
    n = len(ins)
    x, y, c = lax.axis_index("x"), lax.axis_index("y"), lax.axis_index("c")
    me, sibling = (x, y, c), (x, y, 1 - c)
    chips = [(1 - x, y), (x, 1 - y), (1 - x, 1 - y)]

    def copy(a, k, block, to, src=None):
        dst = outs[a].at[_linear(block)]
        return pltpu.make_async_remote_copy(
            src_ref=dst if src is None else src, dst_ref=dst, send_sem=send_sems.at[a * N_PEERS + k],
            recv_sem=recv_sems.at[a * N_PEERS + k], device_id=to, device_id_type=_MESH_ID)

    def own_copies():
        mine = [pltpu.make_async_copy(ins[a], outs[a].at[_linear(me)], local_sems.at[a]) for a in range(n)]
        first = []
        for a in range(n):
            first.append(copy(a, 0, me, sibling, src=ins[a]))
            first += [copy(a, 1 + j, me, (*chip, c), src=ins[a]) for j, chip in enumerate(chips)]
        return mine, first

    def start():
        mine, first = own_copies()
        for cp in mine + first:
            cp.start()

    def finish():
        mine, first = own_copies()
        passed = []
        for j, chip in enumerate(chips):
            for a in range(n):
                copy(a, 1 + j, (*chip, c), me).wait_recv()
                fwd = copy(a, 4 + j, (*chip, c), sibling)
                fwd.start()
                passed.append(fwd)
        for a in range(n):
            copy(a, 0, sibling, me).wait_recv()
            for j, chip in enumerate(chips):
                copy(a, 4 + j, (*chip, 1 - c), me).wait_recv()
        for cp in first + passed:
            cp.wait_send()
        for cp in mine:
            cp.wait()

    return start, finish


N_CHIPS = 4


def _exchange_with_sibling(parts, name):
    n = len(parts)

    def body(*refs):
        ins, outs = refs[:n], refs[n:2 * n]
        send_sems, recv_sems = refs[2 * n:]
        x, y, c = lax.axis_index("x"), lax.axis_index("y"), lax.axis_index("c")
        copies = []
        for a in range(n):
            for q in range(N_CHIPS):
                cp = pltpu.make_async_remote_copy(
                    src_ref=ins[a].at[2 * q + (1 - c)], dst_ref=outs[a].at[q], send_sem=send_sems.at[a * N_CHIPS + q],
                    recv_sem=recv_sems.at[a * N_CHIPS + q], device_id=(x, y, 1 - c), device_id_type=_MESH_ID)
                cp.start()
                copies.append(cp)
        for cp in copies:
            cp.wait()

    return pl.pallas_call(
        body, out_shape=[_sds((N_CHIPS,) + p.shape[1:], p.dtype) for p in parts], in_specs=[_ANY] * n, out_specs=[_ANY] * n,
        scratch_shapes=[pltpu.SemaphoreType.DMA((n * N_CHIPS,)), pltpu.SemaphoreType.DMA((n * N_CHIPS,))],
        name=name,
    )(*parts)


def _pair_sum(part, recv, core, out_dtype, name):
    _, r, cdim = recv.shape
    tr = max(8, min(r, (1 << 18) // cdim))
    assert r % tr == 0, (name, r, tr)

    def body(core_ref, p_ref, r_ref, o_ref):
        o_ref[...] = (p_ref[...] + r_ref[...]).astype(o_ref.dtype)

    grid_spec = pltpu.PrefetchScalarGridSpec(
        num_scalar_prefetch=1, grid=(N_CHIPS, r // tr),
        in_specs=[pl.BlockSpec((None, None, tr, cdim), lambda q, i, core_ref: (q, core_ref[0], i, 0)),
                  pl.BlockSpec((None, tr, cdim), lambda q, i, core_ref: (q, i, 0))],
        out_specs=pl.BlockSpec((None, tr, cdim), lambda q, i, core_ref: (q, i, 0)))
    return pl.pallas_call(
        body, grid_spec=grid_spec, out_shape=_sds((N_CHIPS, r, cdim), out_dtype),
        compiler_params=_params(("parallel", "parallel")), name=name,
    )(core, part.reshape(N_CHIPS, 2, r, cdim), recv)


def _exchange_between_chips(parts, name):
    n = len(parts)

    def body(*refs):
        start, finish = _chip_exchange_plan(refs[:n], refs[n:2 * n], *refs[2 * n:])
        start()
        finish()

    return pl.pallas_call(
        body, out_shape=[_sds(p.shape, p.dtype) for p in parts], in_specs=[_ANY] * n, out_specs=[_ANY] * n,
        scratch_shapes=_chip_exchange_semaphores(n), name=name,
    )(*parts)


def _chip_exchange_semaphores(n):
    n_other = N_CHIPS - 1
    return [pltpu.SemaphoreType.DMA((n * n_other,)), pltpu.SemaphoreType.DMA((n * n_other,)), pltpu.SemaphoreType.DMA((n,))]


def _chip_exchange_plan(ins, outs, send_sems, recv_sems, local_sems):
    n = len(ins)
    n_other = N_CHIPS - 1
    x, y, c = lax.axis_index("x"), lax.axis_index("y"), lax.axis_index("c")
    my_chip = 2 * x + y

    def all_copies():
        mine = [pltpu.make_async_copy(ins[a].at[my_chip], outs[a].at[my_chip], local_sems.at[a]) for a in range(n)]
        remote = []
        for j, (fx, fy) in enumerate([(1, 0), (0, 1), (1, 1)]):
            px, py = (1 - x if fx else x), (1 - y if fy else y)
            for a in range(n):
                remote.append(pltpu.make_async_remote_copy(
                    src_ref=ins[a].at[2 * px + py], dst_ref=outs[a].at[my_chip], send_sem=send_sems.at[a * n_other + j],
                    recv_sem=recv_sems.at[a * n_other + j], device_id=(px, py, c), device_id_type=_MESH_ID))
        return mine, remote

    def start():
        mine, remote = all_copies()
        for cp in mine + remote:
            cp.start()

    def finish():
        mine, remote = all_copies()
        for cp in remote + mine:
            cp.wait()

    return start, finish


def _chip_sums(parts, names, tag):
    core = lax.axis_index("c").astype(jnp.int32).reshape(1)
    from_sibling = _exchange_with_sibling(parts, name="scatter_grads_sibling_" + tag)
    return [_pair_sum(p, r, core, F32 if nm.startswith("pack") else BF16, name="pair_sum_" + nm)
            for p, r, nm in zip(parts, from_sibling, names)]


def _reduce_scatter(parts, names, tag):
    return _exchange_between_chips(_chip_sums(parts, names, tag), name="scatter_grads_chips_" + tag)


def _adamw(w, m, v, slots, name):
    r, c = w.shape
    ns = slots.shape[0]
    tr = max(8, min(r, (1 << 18) // c))
    assert r % tr == 0, (name, r, tr)

    def body(w_ref, m_ref, v_ref, g_ref, g_out, d_out, m_out, v_out):
        g = g_ref[0].astype(F32)
        for s in range(1, ns):
            g = g + g_ref[s].astype(F32)
        m_new = ADAM_B1 * m_ref[...] + (1.0 - ADAM_B1) * g
        v_new = ADAM_B2 * v_ref[...] + (1.0 - ADAM_B2) * jnp.square(g)
        m_hat = m_new / (1.0 - ADAM_B1 ** ADAM_STEP)
        v_hat = v_new / (1.0 - ADAM_B2 ** ADAM_STEP)
        d_out[...] = -ADAM_LR * (m_hat / (jnp.sqrt(v_hat) + ADAM_EPS) + ADAM_WD * w_ref[...])
        g_out[...], m_out[...], v_out[...] = g, m_new, v_new

    spec = pl.BlockSpec((tr, c), lambda i: (i, 0))
    return pl.pallas_call(
        body, grid=(r // tr,), in_specs=[spec, spec, spec, pl.BlockSpec((ns, tr, c), lambda i: (0, i, 0))],
        out_specs=[spec] * 4, out_shape=[_sds((r, c))] * 4, compiler_params=_params(("parallel",)), name=name,
    )(w, m, v, slots)


_COL_VECTORS = ("a_norm", "a_mix", "a_w0", "a_a0", "a_k_k", "a_k_a", "a_ln_x_w", "a_ln_x_b")
_COL_VEC_ROWS = 16
_COL_ROWS = _COL_VEC_ROWS + 2 * LORA_PAD + 256
_ROW_COLS = 2 * LORA_PAD + 256 + 512
_REPL_ROWS = 8


def _pad_to(a, size, axis):
    widths = [(0, 0)] * a.ndim
    widths[axis] = (0, size - a.shape[axis])
    return jnp.pad(a, widths)


def _pack_cols(p):
    width = p["a_norm"].shape[-1]
    vecs = jnp.concatenate([p[n].reshape(-1, width) for n in _COL_VECTORS], axis=0)
    return jnp.concatenate([_pad_to(vecs, _COL_VEC_ROWS, 0), _pad_to(p["a_w2"].reshape(-1, width), LORA_PAD, 0),
                            _pad_to(p["a_a2"].reshape(-1, width), LORA_PAD, 0), p["a_g2"].reshape(-1, width)], axis=0)


def _unpack_cols(a, lead):
    width = a.shape[-1]
    out, row = {}, 0
    for n in _COL_VECTORS:
        k = 6 if n == "a_mix" else 1
        out[n] = a[row:row + k].reshape(lead + ((6, width) if n == "a_mix" else (width,)))
        row += k
    base = _COL_VEC_ROWS
    out["a_w2"] = a[base:base + 96].reshape(lead + (96, width))
    out["a_a2"] = a[base + LORA_PAD:base + LORA_PAD + 96].reshape(lead + (96, width))
    out["a_g2"] = a[base + 2 * LORA_PAD:].reshape(lead + (256, width))
    return out


def _pack_rows(p):
    rows = p["w_kv"].shape[0]
    return jnp.concatenate([_pad_to(p["a_w1"].reshape(rows, -1), LORA_PAD, 1), _pad_to(p["a_a1"].reshape(rows, -1), LORA_PAD, 1),
                            p["a_g1"].reshape(rows, -1), p["w_kv"]], axis=1)


def _unpack_rows(a, lead):
    rows = a.shape[0]
    return {"a_w1": a[:, :96].reshape(lead + (rows, 96)), "a_a1": a[:, LORA_PAD:LORA_PAD + 96].reshape(lead + (rows, 96)),
            "a_g1": a[:, 2 * LORA_PAD:2 * LORA_PAD + 256].reshape(lead + (rows, 256)), "w_kv": a[:, 2 * LORA_PAD + 256:]}


def _pack_repl(p):
    row = lambda a: _pad_to(a.reshape(1, -1), D_MODEL, 1)
    return jnp.concatenate([p["mlp_norm"].reshape(2, D_MODEL), row(p["kv_norm"]), row(p["b_norm"]), row(p["a_r_k"]),
                            row(p["k_norm"]), row(p["b_q_norm"]), row(p["b_sinks"])], axis=0)


def _unpack_repl(a):
    return {"mlp_norm": a[0:2], "kv_norm": a[2], "b_norm": a[3:4], "a_r_k": a[4].reshape(1, N_HEADS, HEAD_DIM),
            "k_norm": a[5, :HEAD_DIM], "b_q_norm": a[6:7, :HEAD_DIM], "b_sinks": a[7:8, :N_HEADS]}


_WEIGHTS = ("a_norm", "a_mix", "a_w_rkv", "a_w0", "a_w1", "a_w2", "a_a0", "a_a1", "a_a2", "a_g1", "a_g2", "a_k_k", "a_k_a",
            "a_r_k", "a_ln_x_w", "a_ln_x_b", "a_w_out", "mlp_norm", "mlp_w_up", "mlp_w_down", "kv_norm", "w_kv", "k_norm",
            "b_norm", "b_w_q", "b_q_norm", "b_sinks", "b_w_o")


def _big_shards(p):
    return [p["a_w_rkv"][0, 0], p["a_w_rkv"][0, 1], p["a_w_rkv"][0, 2], p["a_w_out"][0], p["mlp_w_up"][0], p["mlp_w_up"][1],
            p["mlp_w_down"][0], p["mlp_w_down"][1], p["b_w_q"][0], p["b_w_o"][0]]


_BIG_NAMES = ("a_w_r", "a_w_k", "a_w_v", "a_w_out", "mlp_w_up0", "mlp_w_up1", "mlp_w_down0", "mlp_w_down1", "b_w_q", "b_w_o")


def kernel(x, a_norm, a_mix, a_w_rkv, a_w0, a_w1, a_w2, a_a0, a_a1, a_a2, a_g1, a_g2, a_k_k, a_k_a, a_r_k, a_ln_x_w,
           a_ln_x_b, a_w_out, mlp_norm, mlp_w_up, mlp_w_down, kv_norm, w_kv, k_norm, b_norm, b_w_q, b_q_norm, b_sinks,
           b_w_o, loss_target, m_a_norm, m_a_mix, m_a_w_rkv, m_a_w0, m_a_w1, m_a_w2, m_a_a0, m_a_a1, m_a_a2, m_a_g1,
           m_a_g2, m_a_k_k, m_a_k_a, m_a_r_k, m_a_ln_x_w, m_a_ln_x_b, m_a_w_out, m_mlp_norm, m_mlp_w_up, m_mlp_w_down,
           m_kv_norm, m_w_kv, m_k_norm, m_b_norm, m_b_w_q, m_b_q_norm, m_b_sinks, m_b_w_o, v_a_norm, v_a_mix, v_a_w_rkv,
           v_a_w0, v_a_w1, v_a_w2, v_a_a0, v_a_a1, v_a_a2, v_a_g1, v_a_g2, v_a_k_k, v_a_k_a, v_a_r_k, v_a_ln_x_w,
           v_a_ln_x_b, v_a_w_out, v_mlp_norm, v_mlp_w_up, v_mlp_w_down, v_kv_norm, v_w_kv, v_k_norm, v_b_norm, v_b_w_q,
           v_b_q_norm, v_b_sinks, v_b_w_o):
    given = locals()
    wts = {n: given[n] for n in _WEIGHTS}
    mom = {n: given["m_" + n] for n in _WEIGHTS}
    var = {n: given["v_" + n] for n in _WEIGHTS}

    cols_w, rows_w, repl_w = _pack_cols(wts), _pack_rows(wts), _pack_repl(wts)
    big_w = _big_shards(wts)
    big_bf16 = dict(zip(_BIG_NAMES, [b.astype(BF16) for b in big_w]))
    first_names = [k for k in _BIG_NAMES if k not in _LATE_WEIGHTS]
    gathered = _all_gather([cols_w, rows_w] + [big_bf16[k] for k in first_names], name="gather_weights")
    full_cols = gathered[0].transpose(1, 0, 2).reshape(_COL_ROWS, D_MODEL)
    full_rows = gathered[1].reshape(D_MODEL, _ROW_COLS)
    w = {}
    w.update({k: v.reshape(v.shape[1:]) for k, v in _unpack_cols(full_cols, (1,)).items()})
    for k in ("a_norm", "a_w0", "a_a0", "a_k_k", "a_k_a", "a_ln_x_w", "a_ln_x_b"):
        w[k] = w[k].reshape(1, D_MODEL)
    for k in ("a_w2", "a_a2"):
        w[k] = _pad_to(w[k], LORA_PAD, 0)
    rows_full = _unpack_rows(full_rows, ())
    w["a_w1"], w["a_a1"] = _pad_to(rows_full["a_w1"], LORA_PAD, 1), _pad_to(rows_full["a_a1"], LORA_PAD, 1)
    w["a_g1"], w["w_kv"] = rows_full["a_g1"], rows_full["w_kv"]
    for k, arr in zip(first_names, gathered[2:]):
        w[k] = arr.reshape(N_DEV * arr.shape[1], arr.shape[2])
    w["mlp_norm0"], w["mlp_norm1"] = mlp_norm[0:1], mlp_norm[1:2]
    w["kv_norm"], w["k_norm"] = kv_norm.reshape(1, D_MODEL), k_norm.reshape(1, HEAD_DIM)
    w["b_norm"], w["b_q_norm"], w["b_sinks"], w["a_r_k"] = b_norm, b_q_norm, b_sinks, a_r_k.reshape(1, D_MODEL)

    loss_local, grad_x, g, early_reduced = _local_step(x[0], loss_target[0], w, [big_bf16[k] for k in _LATE_WEIGHTS])
    loss = lax.psum(loss_local[0, 0], MESH_AXES)

    g_lead = {k: g[k][None] for k in ("a_norm", "a_mix", "a_w0", "a_a0", "a_k_k", "a_k_a", "a_ln_x_w", "a_ln_x_b", "a_g2")}
    g_lead["a_w2"], g_lead["a_a2"] = g["a_w2"][None, :96], g["a_a2"][None, :96]
    g_cols = _pack_cols(g_lead).reshape(_COL_ROWS, N_DEV, D_MODEL // N_DEV).transpose(1, 0, 2)
    g_rows = _pack_rows({"a_w1": g["a_w1"][:, :96], "a_a1": g["a_a1"][:, :96], "a_g1": g["a_g1"], "w_kv": g["w_kv"]})
    g_rows = g_rows.reshape(N_DEV, D_MODEL // N_DEV, _ROW_COLS)
    late_names = tuple(k for k in _BIG_NAMES if k not in _EARLY_GRADS)
    late_reduced = _reduce_scatter([g_cols, g_rows] + [_device_major(k, g[k]) for k in late_names],
                                   ("pack_cols", "pack_rows") + late_names, "late")
    big_reduced = dict(zip(late_names, late_reduced[2:]))
    big_reduced.update(zip(_EARLY_GRADS, early_reduced))
    reduced = list(late_reduced[:2]) + [big_reduced[k] for k in _BIG_NAMES]
    g_repl = _pack_repl({"mlp_norm": jnp.concatenate([g["mlp_norm0"], g["mlp_norm1"]], axis=0), "kv_norm": g["kv_norm"],
                         "b_norm": g["b_norm"], "a_r_k": g["a_r_k"], "k_norm": g["k_norm"], "b_q_norm": g["b_q_norm"],
                         "b_sinks": g["b_sinks"]})
    (repl_slots,) = _all_gather([g_repl], name="gather_replicated_grads")

    res = {}
    cols4 = _adamw(cols_w, _pack_cols(mom), _pack_cols(var), reduced[0], name="adamw_cols")
    rows4 = _adamw(rows_w, _pack_rows(mom), _pack_rows(var), reduced[1], name="adamw_rows")
    repl4 = _adamw(repl_w, _pack_repl(mom), _pack_repl(var), repl_slots, name="adamw_replicated")
    for unpacked in ([_unpack_cols(a, (1,)) for a in cols4], [_unpack_rows(a, (1,)) for a in rows4], [_unpack_repl(a) for a in repl4]):
        for k in unpacked[0]:
            res[k] = tuple(u[k] for u in unpacked)
    big4 = [_adamw(bw, bm, bv, slots, name="adamw_" + k)
            for k, bw, bm, bv, slots in zip(_BIG_NAMES, big_w, _big_shards(mom), _big_shards(var), reduced[2:])]
    res["a_w_rkv"] = tuple(jnp.stack([big4[0][i], big4[1][i], big4[2][i]])[None] for i in range(4))
    res["a_w_out"] = tuple(a[None] for a in big4[3])
    res["mlp_w_up"] = tuple(jnp.stack([big4[4][i], big4[5][i]]) for i in range(4))
    res["mlp_w_down"] = tuple(jnp.stack([big4[6][i], big4[7][i]]) for i in range(4))
    res["b_w_q"] = tuple(a[None] for a in big4[8])
    res["b_w_o"] = tuple(a[None] for a in big4[9])
    res["w_kv"] = tuple(a.reshape(w_kv.shape) for a in res["w_kv"])

    outs = [loss, grad_x[None]]
    for i in range(4):
        outs += [res[n][i].reshape(given[n].shape) for n in _WEIGHTS]
    return tuple(outs)
```

```python
import functools
import math

import jax
import jax.numpy as jnp
from jax import lax
from jax.experimental import pallas as pl
from jax.experimental.pallas import tpu as pltpu

F32 = jnp.float32
BF16 = jnp.bfloat16

D_MODEL = 2048
N_HEADS = 32
HEAD_DIM = 64
N_KV_HEADS = 4
Q_PER_KV = 8
ATT_BLOCK = 128
WKV_CHUNK = 64
LORA_PAD = 128
D_FF = 8192
N_DEV = 8
RMS_EPS = 1e-6
GN_EPS = 64e-5
L2_EPS = 1e-12
ROPE_THETA = 10000.0
ADAM_LR, ADAM_B1, ADAM_B2, ADAM_EPS, ADAM_WD, ADAM_STEP = 0.001, 0.9, 0.999, 1e-08, 0.01, 10
MASK_VALUE = -1e30
VMEM_LIMIT_BYTES = 56 * 1024 * 1024
MESH_AXES = ("x", "y", "c")
HI = lax.Precision.HIGHEST

_NN = (((1,), (0,)), ((), ()))
_NT = (((1,), (1,)), ((), ()))
_TN = (((0,), (0,)), ((), ()))
_BNN = (((2,), (1,)), ((0,), (0,)))
_BNT = (((2,), (2,)), ((0,), (0,)))
_BTN = (((1,), (1,)), ((0,), (0,)))


def _params(sem):
    return pltpu.CompilerParams(dimension_semantics=sem, vmem_limit_bytes=VMEM_LIMIT_BYTES)


def _split2(a):
    hi = a.astype(BF16)
    return hi, (a - hi.astype(F32)).astype(BF16)


def _dot3(a, b, dims):
    ah, al = _split2(a)
    bh, bl = _split2(b)
    d = lambda p, q: lax.dot_general(p, q, dims, preferred_element_type=F32)
    return d(ah, bh) + (d(al, bh) + d(ah, bl))


@functools.partial(jax.custom_vjp, nondiff_argnums=(2,))
def _hdot(a, b, dims=_NN):
    return _dot3(a, b, dims)


def _hdot_fwd(a, b, dims):
    return _dot3(a, b, dims), (a, b)


def _hdot_bwd(dims, res, g):
    a, b = res
    nn, nt, tn = (_NN, _NT, _TN) if dims in (_NN, _NT, _TN) else (_BNN, _BNT, _BTN)
    if dims == nn:
        return _dot3(g, b, nt), _dot3(a, g, tn)
    if dims == nt:
        return _dot3(g, b, nn), _dot3(g, a, tn)
    assert dims == tn
    return _dot3(b, g, nt), _dot3(a, g, nn)


_hdot.defvjp(_hdot_fwd, _hdot_bwd)


def _tri_parts(x):
    hi = x.astype(BF16)
    r1 = x - hi.astype(F32)
    mid = r1.astype(BF16)
    return hi, mid, (r1 - mid.astype(F32)).astype(BF16)


@jax.custom_vjp
def _mask_dot(mask, x):
    mb = mask.astype(BF16)
    p0, p1, p2 = _tri_parts(x)
    d = lambda p: lax.dot_general(mb, p, _BNN, preferred_element_type=F32)
    return d(p0) + (d(p1) + d(p2))


def _mask_dot_fwd(mask, x):
    return _mask_dot(mask, x), mask


def _mask_dot_bwd(mask, g):
    mb = mask.astype(BF16)
    p0, p1, p2 = _tri_parts(g)
    d = lambda p: lax.dot_general(mb, p, _BTN, preferred_element_type=F32)
    return jnp.zeros_like(mask), d(p0) + (d(p1) + d(p2))


_mask_dot.defvjp(_mask_dot_fwd, _mask_dot_bwd)


def _b16dot(a, b, dims):
    return lax.dot_general(a.astype(BF16), b.astype(BF16), dims, preferred_element_type=F32)


@jax.custom_vjp
def _bdot(a, b):
    return _b16dot(a, b, _NN)


def _bdot_fwd(a, b):
    return _b16dot(a, b, _NN), (a, b)


def _bdot_bwd(res, g):
    a, b = res
    return _b16dot(g, b, _NT), _b16dot(a, g, _TN)


_bdot.defvjp(_bdot_fwd, _bdot_bwd)


@jax.custom_vjp
def _bdot_nt(a, b):
    return _b16dot(a, b, _NT)


def _bdot_nt_fwd(a, b):
    return _b16dot(a, b, _NT), (a, b)


def _bdot_nt_bwd(res, g):
    a, b = res
    return _b16dot(g, b, _NN), _b16dot(g, a, _TN)


_bdot_nt.defvjp(_bdot_nt_fwd, _bdot_nt_bwd)


def _rms(x, gain):
    return x * lax.rsqrt(jnp.mean(x * x, axis=-1, keepdims=True) + RMS_EPS) * gain


def _vjp_of(f, n_in, diff):
    def g(*args):
        ins, cts = args[:n_in], args[n_in:]

        def fd(*d):
            full = list(ins)
            for pos, i in enumerate(diff):
                full[i] = d[pos]
            return f(*full)

        _, pull = jax.vjp(fd, *[ins[i] for i in diff])
        return pull(tuple(cts))
    return g


def _mm(a, b, *, name, ta=False, tb=False, a_pro=None, epi=None, epi_args=(), out_dtype=F32,
        tm=1024, tn=1024, tk=2048, dims=None, b_spec=None, o_spec=None, o_shape=None):
    if dims is None:
        m, k = (a.shape[1], a.shape[0]) if ta else a.shape
        n = b.shape[0] if tb else b.shape[1]
    else:
        m, n, k = dims
    tm, tn, tk = min(tm, m), min(tn, n), min(tk, k)
    assert m % tm == 0 and n % tn == 0 and k % tk == 0, (name, m, n, k, tm, tn, tk)
    nk = k // tk
    ne = len(epi_args)
    cdims = (((0 if ta else 1,), (1 if tb else 0,)), ((), ()))

    def body(a_ref, b_ref, *rest):
        e_refs, o_ref, acc = rest[:ne], rest[ne], rest[ne + 1]
        kk = pl.program_id(2)

        @pl.when(kk == 0)
        def _():
            acc[...] = jnp.zeros_like(acc)

        av = a_ref[...]
        if a_pro is not None:
            av = a_pro(av.astype(F32))
        acc[...] += lax.dot_general(av.astype(BF16), b_ref[...].astype(BF16), cdims, preferred_element_type=F32)

        @pl.when(kk == nk - 1)
        def _():
            r = acc[...]
            if epi is not None:
                r = epi(r, *[e[...] for e in e_refs])
            o_ref[...] = r.astype(o_ref.dtype)

    a_spec = pl.BlockSpec((tk, tm), lambda i, j, q: (q, i)) if ta else pl.BlockSpec((tm, tk), lambda i, j, q: (i, q))
    if b_spec is None:
        b_spec = pl.BlockSpec((tn, tk), lambda i, j, q: (j, q)) if tb else pl.BlockSpec((tk, tn), lambda i, j, q: (q, j))
    if o_spec is None:
        o_spec = pl.BlockSpec((tm, tn), lambda i, j, q: (i, j))
        o_shape = (m, n)
    e_specs = [pl.BlockSpec((tm, tn), lambda i, j, q: (i, j)) for _ in epi_args]
    return pl.pallas_call(
        body, grid=(m // tm, n // tn, nk), in_specs=[a_spec, b_spec] + e_specs, out_specs=o_spec,
        out_shape=jax.ShapeDtypeStruct(o_shape, out_dtype), scratch_shapes=[pltpu.VMEM((tm, tn), F32)],
        compiler_params=_params(("parallel", "parallel", "arbitrary")), name=name,
    )(a, b, *epi_args)


def _rowwise(fn, rows, params, out_rows, out_params=(), *, tm=256, name):
    t = rows[0].shape[0]
    tm = min(tm, t)
    assert t % tm == 0
    nr, npar, nor, nop = len(rows), len(params), len(out_rows), len(out_params)

    def body(*refs):
        r, p = refs[:nr], refs[nr:nr + npar]
        o, op = refs[nr + npar:nr + npar + nor], refs[nr + npar + nor:]
        outs = fn(*[x[...] for x in r], *[x[...] for x in p])
        for ref, val in zip(o, outs[:nor]):
            ref[...] = val.astype(ref.dtype)
        if nop:
            @pl.when(pl.program_id(0) == 0)
            def _():
                for ref in op:
                    ref[...] = jnp.zeros_like(ref)

            for ref, val in zip(op, outs[nor:]):
                ref[...] += val.astype(F32)

    in_specs = [pl.BlockSpec((tm, x.shape[1]), lambda i: (i, 0)) for x in rows]
    in_specs += [pl.BlockSpec(p.shape, lambda i: (0, 0)) for p in params]
    out_specs = [pl.BlockSpec((tm, s.shape[1]), lambda i: (i, 0)) for s in out_rows]
    out_specs += [pl.BlockSpec(s.shape, lambda i: (0, 0)) for s in out_params]
    return pl.pallas_call(
        body, grid=(t // tm,), in_specs=in_specs, out_specs=out_specs, out_shape=list(out_rows) + list(out_params),
        compiler_params=_params(("arbitrary",)), name=name,
    )(*rows, *params)


def _sds(shape, dtype=F32):
    return jax.ShapeDtypeStruct(tuple(shape), dtype)


def _doubling_powers(l):
    powers = [l]
    for _ in range(int(math.log2(l.shape[-1])) - 1):
        powers.append(_dot3(powers[-1], powers[-1], _BNN))
    return powers


@jax.custom_vjp
def _unit_lower_solve(l, z):
    u = z
    for p in _doubling_powers(l):
        u = u + _dot3(p, u, _BNN)
    return u


def _unit_lower_solve_fwd(l, z):
    powers = _doubling_powers(l)
    u = z
    for p in powers:
        u = u + _dot3(p, u, _BNN)
    return u, (powers, u)


def _unit_lower_solve_bwd(res, du):
    powers, u = res
    g = du
    for p in powers:
        g = g + _dot3(p, g, _BTN)
    return _dot3(g, u, _BNT), g


_unit_lower_solve.defvjp(_unit_lower_solve_fwd, _unit_lower_solve_bwd)


def _wkv_chunk(s0, r, lw, k, v, a, b):
    nb, c, _ = r.shape
    ti = lax.broadcasted_iota(jnp.int32, (nb, c, c), 1)
    si = lax.broadcasted_iota(jnp.int32, (nb, c, c), 2)
    incl, strict = si <= ti, si < ti
    cum = _mask_dot(incl.astype(F32), lw)
    rcum = _mask_dot((si > ti).astype(F32), lw)
    tot = jnp.sum(lw, axis=1, keepdims=True)
    w_inv = jnp.exp(-cum)
    at, rt, bt, kt = a * jnp.exp(cum - lw), r * jnp.exp(cum), b * w_inv, k * w_inv
    l_ab = jnp.where(strict, _hdot(at, bt, _BNT), 0.0)
    l_ak = jnp.where(strict, _hdot(at, kt, _BNT), 0.0)
    t_rb = jnp.where(incl, _hdot(rt, bt, _BNT), 0.0)
    t_rk = jnp.where(incl, _hdot(rt, kt, _BNT), 0.0)
    u = _unit_lower_solve(l_ab, _hdot(at, s0, _BNT) + _hdot(l_ak, v, _BNN))
    y = _hdot(rt, s0, _BNT) + _hdot(t_rb, u, _BNN) + _hdot(t_rk, v, _BNN)
    e = jnp.exp(rcum)
    s1 = s0 * jnp.exp(tot) + _hdot(u, b * e, _BTN) + _hdot(v, k * e, _BTN)
    return y, s1


WKV_HEADS_PER_STEP = 16


def _first_and_last_step(grid):
    i, j = pl.program_id(0), pl.program_id(1)
    return jnp.logical_and(i == 0, j == 0), jnp.logical_and(i == grid[0] - 1, j == grid[1] - 1)


N_WKV_PARAMS = 7


def _tmix_chunk(s0, r, k, v, wl, al, w0, a0, k_k, k_a, ln_w, ln_b, r_k):
    lw, kmod, a, b = _wkv_prep(k, wl, al, w0, a0, k_k, k_a)
    y, s1 = _wkv_chunk(s0, r, lw, kmod, v, a, b)
    (y2,) = _wkv_post(y, r, kmod, v, ln_w, ln_b, r_k)
    return y2, s1


def _split_heads(x, nh):
    return x.reshape(x.shape[0], nh, HEAD_DIM).transpose(1, 0, 2)


def _merge_heads(xh):
    return jnp.concatenate([xh[h] for h in range(xh.shape[0])], axis=1)


def _wkv_fwd(r, k, v, wl, al, params, shards=()):
    t = r.shape[0]
    nh, n = N_HEADS, HEAD_DIM
    nc = t // WKV_CHUNK
    hb = WKV_HEADS_PER_STEP
    grid = (nh // hb, nc)
    ns = len(shards)
    n_in = 5 + N_WKV_PARAMS

    def body(*refs):
        y_ref, s_ref = refs[n_in + ns:n_in + ns + 2]
        state = refs[n_in + 2 * ns + 2]
        if ns:
            start, finish = _gather_plan(refs[n_in:n_in + ns], refs[n_in + ns + 2:n_in + 2 * ns + 2], *refs[n_in + 2 * ns + 3:])
            first, last = _first_and_last_step(grid)
            pl.when(first)(start)

        @pl.when(pl.program_id(1) == 0)
        def _():
            state[...] = jnp.zeros_like(state)

        s0 = state[...]
        s_ref[:, 0] = s0
        rows = [_split_heads(ref[...], hb) for ref in refs[:5]]
        y, s1 = _tmix_chunk(s0, *rows, *[ref[...] for ref in refs[5:n_in]])
        y_ref[...] = _merge_heads(y)
        state[...] = s1
        if ns:
            pl.when(last)(finish)

    blk = pl.BlockSpec((WKV_CHUNK, hb * n), lambda h, c: (c, h))
    pblk = pl.BlockSpec((hb, 1, n), lambda h, c: (h, 0, 0))
    sblk = pl.BlockSpec((hb, 1, n, n), lambda h, c: (h, c, 0, 0))
    outs = pl.pallas_call(
        body, grid=grid, in_specs=[blk] * 5 + [pblk] * N_WKV_PARAMS + [_ANY] * ns, out_specs=[blk, sblk] + [_ANY] * ns,
        out_shape=[_sds((t, nh * n)), _sds((nh, nc, n, n))] + [_sds((N_DEV,) + s.shape, s.dtype) for s in shards],
        scratch_shapes=[pltpu.VMEM((hb, n, n), F32)] + (_gather_semaphores(ns) if ns else []),
        compiler_params=_params(("arbitrary", "arbitrary")), name="wkv_fwd",
    )(r, k, v, wl, al, *params, *shards)
    return outs[0], outs[1], list(outs[2:])


def _wkv_bwd(r, k, v, wl, al, params, states, dy, chip_sums=()):
    t = r.shape[0]
    nh, n = N_HEADS, HEAD_DIM
    nc = t // WKV_CHUNK
    hb = WKV_HEADS_PER_STEP
    grid = (nh // hb, nc)
    ns = len(chip_sums)
    n_in = 5 + N_WKV_PARAMS
    n_out = 5 + N_WKV_PARAMS

    def body(*refs):
        s_ref, dy_ref = refs[n_in:n_in + 2]
        out_refs = refs[n_in + 2 + ns:n_in + 2 + ns + n_out]
        dstate = refs[n_in + 2 + 2 * ns + n_out]
        if ns:
            start, finish = _chip_exchange_plan(refs[n_in + 2:n_in + 2 + ns], refs[n_in + 2 + ns + n_out:n_in + 2 + 2 * ns + n_out],
                                                *refs[n_in + 3 + 2 * ns + n_out:])
            first, last = _first_and_last_step(grid)
            pl.when(first)(start)

        @pl.when(pl.program_id(1) == 0)
        def _():
            dstate[...] = jnp.zeros_like(dstate)
            for ref in out_refs[5:]:
                ref[...] = jnp.zeros_like(ref)

        rows = [_split_heads(ref[...], hb) for ref in refs[:5]]
        _, pull = jax.vjp(_tmix_chunk, s_ref[:, 0], *rows, *[ref[...] for ref in refs[5:n_in]])
        grads = pull((_split_heads(dy_ref[...], hb), dstate[...]))
        dstate[...] = grads[0]
        for ref, val in zip(out_refs[:5], grads[1:6]):
            ref[...] = _merge_heads(val).astype(ref.dtype)
        for ref, val in zip(out_refs[5:], grads[6:]):
            ref[...] += val
        if ns:
            pl.when(last)(finish)

    blk = pl.BlockSpec((WKV_CHUNK, hb * n), lambda h, c: (nc - 1 - c, h))
    pblk = pl.BlockSpec((hb, 1, n), lambda h, c: (h, 0, 0))
    sblk = pl.BlockSpec((hb, 1, n, n), lambda h, c: (h, nc - 1 - c, 0, 0))
    outs = pl.pallas_call(
        body, grid=grid, in_specs=[blk] * 5 + [pblk] * N_WKV_PARAMS + [sblk, blk] + [_ANY] * ns,
        out_specs=[blk] * 5 + [pblk] * N_WKV_PARAMS + [_ANY] * ns,
        out_shape=[_sds((t, nh * n), BF16)] * 5 + [_sds((nh, 1, n))] * N_WKV_PARAMS + [_sds(p.shape, p.dtype) for p in chip_sums],
        scratch_shapes=[pltpu.VMEM((hb, n, n), F32)] + (_chip_exchange_semaphores(ns) if ns else []),
        compiler_params=_params(("arbitrary", "arbitrary")), name="wkv_bwd",
    )(r, k, v, wl, al, *params, states, dy, *chip_sums)
    return outs[:5], outs[5:n_out], list(outs[n_out:])


def _wkv_prep(k, wl, al, w0, a0, k_k, k_a):
    z = -(w0 + wl)
    softplus = jnp.maximum(z, 0.0) + jnp.log1p(jnp.exp(-jnp.abs(z)))
    lw = -jnp.exp(-softplus - 0.5)
    asig = jax.nn.sigmoid(a0 + al)
    kk = k * k_k
    kk = kk / jnp.maximum(jnp.sqrt(jnp.sum(kk * kk, axis=-1, keepdims=True)), L2_EPS)
    kmod = k * (1.0 + (asig - 1.0) * k_a)
    return lw, kmod, -kk, kk * asig


def _wkv_post(y, r, kmod, v, ln_w, ln_b, r_k):
    mu = jnp.mean(y, axis=-1, keepdims=True)
    var = jnp.mean(jnp.square(y - mu), axis=-1, keepdims=True)
    yn = (y - mu) * lax.rsqrt(var + GN_EPS)
    yn = yn * ln_w + ln_b
    return (yn + jnp.sum(r * kmod * r_k, axis=-1, keepdims=True) * v,)


def _attn_block(nonzero_block, qs, kc, kp, vc, vp, cos_c, sin_c, cos_p, sin_p, q_gain, k_gain, sinks):
    ri = lax.broadcasted_iota(jnp.int32, (HEAD_DIM, HEAD_DIM), 0)
    ci = lax.broadcasted_iota(jnp.int32, (HEAD_DIM, HEAD_DIM), 1)
    half = HEAD_DIM // 2
    rot = jnp.where(ri == ci + half, -1.0, 0.0) + jnp.where(ri + half == ci, 1.0, 0.0)

    def rope(x, cos, sin):
        return x * cos + _hdot(x, rot, _NN) * sin

    kcr = rope(_rms(kc, k_gain), cos_c, sin_c)
    kpr = rope(_rms(kp, k_gain), cos_p, sin_p)
    qi = lax.broadcasted_iota(jnp.int32, (ATT_BLOCK, ATT_BLOCK), 0)
    ki = lax.broadcasted_iota(jnp.int32, (ATT_BLOCK, ATT_BLOCK), 1)
    mask_c = ki <= qi
    mask_p = jnp.logical_and(ki > qi, nonzero_block)
    lane0 = (lax.broadcasted_iota(jnp.int32, (1, 128), 1) == 0).astype(F32)
    outs = []
    for g in range(Q_PER_KV):
        q = rope(_rms(qs[g], q_gain), cos_c, sin_c)
        sc = jnp.where(mask_c, _bdot_nt(q, kcr) * (HEAD_DIM ** -0.5), MASK_VALUE)
        sp = jnp.where(mask_p, _bdot_nt(q, kpr) * (HEAD_DIM ** -0.5), MASK_VALUE)
        sk = jnp.sum(sinks[g] * lane0, axis=1, keepdims=True)
        mx = jnp.maximum(jnp.maximum(jnp.max(sc, axis=1, keepdims=True), jnp.max(sp, axis=1, keepdims=True)), sk)
        mx = lax.stop_gradient(mx)
        ec, ep = jnp.exp(sc - mx), jnp.exp(sp - mx)
        den = jnp.sum(ec, axis=1, keepdims=True) + jnp.sum(ep, axis=1, keepdims=True) + jnp.exp(sk - mx)
        outs.append(_bdot(ec / den, vc) + _bdot(ep / den, vp))
    return tuple(outs)


def _attn_specs(t):
    nb = t // ATT_BLOCK
    prev = lambda n: jnp.maximum(n - 1, 0)
    q_spec = pl.BlockSpec((Q_PER_KV, ATT_BLOCK, HEAD_DIM), lambda g, n: (g, n, 0))
    kv_c = pl.BlockSpec((None, ATT_BLOCK, HEAD_DIM), lambda g, n: (g, n, 0))
    kv_p = pl.BlockSpec((None, ATT_BLOCK, HEAD_DIM), lambda g, n: (g, prev(n), 0))
    tab_c = pl.BlockSpec((ATT_BLOCK, HEAD_DIM), lambda g, n: (n, 0))
    tab_p = pl.BlockSpec((ATT_BLOCK, HEAD_DIM), lambda g, n: (prev(n), 0))
    gain = pl.BlockSpec((1, HEAD_DIM), lambda g, n: (0, 0))
    sink = pl.BlockSpec((Q_PER_KV, 1, 128), lambda g, n: (g, 0, 0))
    return nb, q_spec, kv_c, kv_p, tab_c, tab_p, gain, sink


def _attn_fwd(q, k, v, cos, sin, q_gain, k_gain, sinks):
    t = q.shape[1]
    nb, q_spec, kv_c, kv_p, tab_c, tab_p, gain, sink = _attn_specs(t)

    def body(q_ref, kc, kp, vc, vp, cc, sc, cp, sp, qg, kg, sk, o_ref):
        outs = _attn_block(pl.program_id(1) > 0, [q_ref[g] for g in range(Q_PER_KV)], kc[...], kp[...], vc[...], vp[...],
                           cc[...], sc[...], cp[...], sp[...], qg[...], kg[...], [sk[g] for g in range(Q_PER_KV)])
        for g in range(Q_PER_KV):
            o_ref[g] = outs[g]

    return pl.pallas_call(
        body, grid=(N_KV_HEADS, nb), in_specs=[q_spec, kv_c, kv_p, kv_c, kv_p, tab_c, tab_c, tab_p, tab_p, gain, gain, sink],
        out_specs=q_spec, out_shape=_sds(q.shape), compiler_params=_params(("parallel", "arbitrary")), name="attn_fwd",
    )(q, k, k, v, v, cos, sin, cos, sin, q_gain, k_gain, sinks)


def _attn_bwd(q, k, v, cos, sin, q_gain, k_gain, sinks, do):
    t = q.shape[1]
    nb, q_spec, kv_c, kv_p, tab_c, tab_p, gain, sink = _attn_specs(t)
    nq = Q_PER_KV

    def body(q_ref, kc, kp, vc, vp, cc, sc, cp, sp, qg, kg, sk, do_ref,
             dq_ref, dkc_ref, dkp_ref, dvc_ref, dvp_ref, dqg_ref, dkg_ref, dsk_ref):
        nonzero = pl.program_id(1) > 0
        tabs = (cc[...], sc[...], cp[...], sp[...])

        def f(qs, kcv, kpv, vcv, vpv, qgv, kgv, sks):
            return _attn_block(nonzero, qs, kcv, kpv, vcv, vpv, *tabs, qgv, kgv, sks)

        _, pull = jax.vjp(f, [q_ref[g] for g in range(nq)], kc[...], kp[...], vc[...], vp[...], qg[...], kg[...],
                          [sk[g] for g in range(nq)])
        dqs, dkc, dkp, dvc, dvp, dqg, dkg, dsks = pull(tuple(do_ref[g] for g in range(nq)))
        for g in range(nq):
            dq_ref[g] = dqs[g]
        dkc_ref[...], dkp_ref[...], dvc_ref[...], dvp_ref[...] = dkc, dkp, dvc, dvp
        first = jnp.logical_and(pl.program_id(0) == 0, pl.program_id(1) == 0)

        @pl.when(first)
        def _():
            dqg_ref[...] = jnp.zeros_like(dqg_ref)
            dkg_ref[...] = jnp.zeros_like(dkg_ref)

        @pl.when(pl.program_id(1) == 0)
        def _():
            dsk_ref[...] = jnp.zeros_like(dsk_ref)

        dqg_ref[...] += dqg
        dkg_ref[...] += dkg
        for g in range(nq):
            dsk_ref[g] += dsks[g]

    kv_shape = _sds(k.shape)
    return pl.pallas_call(
        body, grid=(N_KV_HEADS, nb),
        in_specs=[q_spec, kv_c, kv_p, kv_c, kv_p, tab_c, tab_c, tab_p, tab_p, gain, gain, sink, q_spec],
        out_specs=[q_spec, kv_c, kv_c, kv_c, kv_c, gain, gain, sink],
        out_shape=[_sds(q.shape), kv_shape, kv_shape, kv_shape, kv_shape, _sds((1, HEAD_DIM)), _sds((1, HEAD_DIM)), _sds(sinks.shape)],
        compiler_params=_params(("arbitrary", "arbitrary")), name="attn_bwd",
    )(q, k, k, v, v, cos, sin, cos, sin, q_gain, k_gain, sinks, do)


def _time_shift_lerps(x, xs, gain, *mix):
    xn, xsn = _rms(x, gain), _rms(xs, gain)
    xx = xsn - xn
    return tuple(xn + xx * m for m in mix)


def _residual_norm(h, delta, gain):
    hn = h + delta
    return hn, _rms(hn, gain)


def _residual_norm2(h, delta, gain_a, gain_b):
    hn = h + delta
    return hn, _rms(hn, gain_a), _rms(hn, gain_b)


def _relu2(u):
    return jnp.square(jnp.maximum(u, 0.0))


def _sigmoid(z):
    return jax.nn.sigmoid(z)


def _to_heads(x2d, nh):
    t = x2d.shape[0]
    return x2d.reshape(t, nh, HEAD_DIM).transpose(1, 0, 2)


def _from_heads(xh):
    nh, t, n = xh.shape
    return xh.transpose(1, 0, 2).reshape(t, nh * n)


def _shift_down(x):
    return jnp.pad(x[:-1], ((1, 0), (0, 0)))


def _shift_up(x):
    return jnp.pad(x[1:], ((0, 1), (0, 0)))


def _rope_tables(t):
    half = HEAD_DIM // 2
    inv_freq = jnp.power(ROPE_THETA, -jnp.arange(half, dtype=F32) / half)
    ang = jnp.arange(t, dtype=jnp.int32).astype(F32)[:, None] * inv_freq[None, :]
    cos, sin = jnp.cos(ang), jnp.sin(ang)
    return jnp.concatenate([cos, cos], axis=1), jnp.concatenate([sin, sin], axis=1)


def _mlp_fwd(hn, w_up, w_down, layer, up_dev_major):
    t = hn.shape[0]
    if up_dev_major:
        cw = w_up.shape[2]
        u = _mm(hn, w_up, name=f"mlp{layer}_up", dims=(t, D_FF, D_MODEL), tn=cw, tk=D_MODEL,
                b_spec=pl.BlockSpec((None, D_MODEL, cw), lambda i, j, q: (j, q, 0)))
    else:
        u = _mm(hn, w_up, name=f"mlp{layer}_up")
    out = _mm(u, w_down, a_pro=_relu2, name=f"mlp{layer}_down")
    return u, out


def _mlp_bwd(hn, u, dh, w_up, w_down, layer, up_dev_major):
    t = hn.shape[0]
    du = _mm(dh, w_down, tb=True, epi=lambda r, uu: r * (2.0 * jnp.maximum(uu, 0.0)), epi_args=(u,), out_dtype=BF16,
             name=f"mlp{layer}_du")
    d_down = _mm(u, dh, ta=True, a_pro=_relu2, name=f"mlp{layer}_ddown")
    if up_dev_major:
        cw = w_up.shape[2]
        d_up = _mm(hn, du, ta=True, name=f"mlp{layer}_dup", dims=(D_MODEL, D_FF, t), tn=cw,
                   o_spec=pl.BlockSpec((None, 1024, cw), lambda i, j, q: (j, i, 0)), o_shape=(N_DEV, D_MODEL, cw))
        dhn = _mm(du, w_up, tb=True, name=f"mlp{layer}_dhn", dims=(t, D_MODEL, D_FF), tk=cw,
                  b_spec=pl.BlockSpec((None, 1024, cw), lambda i, j, q: (q, j, 0)))
    else:
        d_up = _mm(hn, du, ta=True, name=f"mlp{layer}_dup")
        dhn = _mm(du, w_up, tb=True, name=f"mlp{layer}_dhn")
    return dhn, d_up, d_down


_LATE_WEIGHTS = ("mlp_w_up0", "mlp_w_up1", "mlp_w_down0", "mlp_w_down1", "b_w_q", "b_w_o")
_EARLY_GRADS = _LATE_WEIGHTS + ("a_w_out",)


def _local_step(x, target, w, late_shards=None, up_dev_major=True):
    t = x.shape[0]
    g = {}
    w = dict(w)
    row = lambda: _sds((t, D_MODEL))
    rowb = lambda: _sds((t, D_MODEL), BF16)
    vec = lambda: _sds((1, D_MODEL))

    xs = _shift_down(x)
    mix = [w["a_mix"][i:i + 1] for i in range(6)]
    xr, xk, xv, xw, xa, xg = _rowwise(_time_shift_lerps, [x, xs], [w["a_norm"]] + mix, [rowb()] * 6, tm=256, name="tmix_lerp")
    r = _mm(xr, w["a_w_r"], name="tmix_r")
    k = _mm(xk, w["a_w_k"], name="tmix_k")
    v = _mm(xv, w["a_w_v"], name="tmix_v")
    lw1 = _mm(xw, w["a_w1"], name="tmix_w1")
    wl = _mm(lw1, w["a_w2"], a_pro=jnp.tanh, name="tmix_w2")
    la1 = _mm(xa, w["a_a1"], name="tmix_a1")
    al = _mm(la1, w["a_a2"], name="tmix_a2")
    lg1 = _mm(xg, w["a_g1"], name="tmix_g1")
    gate = _mm(lg1, w["a_g2"], a_pro=_sigmoid, name="tmix_g2")

    hv = lambda name: w[name].reshape(N_HEADS, 1, HEAD_DIM)
    wkv_params = [hv(name) for name in ("a_w0", "a_a0", "a_k_k", "a_k_a", "a_ln_x_w", "a_ln_x_b", "a_r_k")]
    y2, states, gathered = _wkv_fwd(r, k, v, wl, al, wkv_params, late_shards or ())
    for name, arr in zip(_LATE_WEIGHTS, gathered):
        w[name] = arr if name.startswith("mlp_w_up") else arr.reshape(N_DEV * arr.shape[1], arr.shape[2])
    (yg,) = _rowwise(lambda a, b: (a * b,), [y2, gate], [], [rowb()], name="tmix_gate")
    att = _mm(yg, w["a_w_out"], name="tmix_out")

    h1, hn0 = _rowwise(_residual_norm, [x, att], [w["mlp_norm0"]], [row(), rowb()], name="res_norm0")
    u0, m0 = _mlp_fwd(hn0, w["mlp_w_up0"], w["mlp_w_down0"], 0, up_dev_major)

    h2, kvn, qn = _rowwise(_residual_norm2, [h1, m0], [w["kv_norm"], w["b_norm"]], [row(), rowb(), rowb()], name="res_norm_kvq")
    kv = _mm(kvn, w["w_kv"], name="kv_proj")
    q = _mm(qn, w["b_w_q"], name="q_proj")
    half_kv = N_KV_HEADS * HEAD_DIM
    ksh, vsh = _to_heads(kv[:, :half_kv], N_KV_HEADS), _to_heads(kv[:, half_kv:], N_KV_HEADS)
    qh = _to_heads(q, N_HEADS)
    cos, sin = _rope_tables(t)
    sinks = jnp.broadcast_to(w["b_sinks"].reshape(N_HEADS, 1, 1), (N_HEADS, 1, 128))
    oh = _attn_fwd(qh, ksh, vsh, cos, sin, w["b_q_norm"], w["k_norm"], sinks)
    o = _from_heads(oh).astype(BF16)
    att2 = _mm(o, w["b_w_o"], name="attn_out")

    h3, hn1 = _rowwise(_residual_norm, [h2, att2], [w["mlp_norm1"]], [row(), rowb()], name="res_norm1")
    u1, m1 = _mlp_fwd(hn1, w["mlp_w_up1"], w["mlp_w_down1"], 1, up_dev_major)

    def loss_fn(h, m, tg):
        diff = (h + m) - tg
        part = 0.5 * jnp.sum(jnp.mean(jnp.square(diff), axis=-1, keepdims=True), axis=0, keepdims=True)
        return diff * (1.0 / D_MODEL), jnp.broadcast_to(part, (1, 128))

    dh4, loss = _rowwise(loss_fn, [h3, m1, target], [], [row()], [_sds((1, 128))], name="loss")

    def res_norm_bwd(h, dnext, dhn, gain):
        dh, dgain = _vjp_of(lambda hh, gg: (_rms(hh, gg),), 2, (0, 1))(h, gain, dhn)
        return dnext + dh, dgain

    dhn1, g["mlp_w_up1"], g["mlp_w_down1"] = _mlp_bwd(hn1, u1, dh4, w["mlp_w_up1"], w["mlp_w_down1"], 1, up_dev_major)
    dh3, g["mlp_norm1"] = _rowwise(res_norm_bwd, [h3, dh4, dhn1], [w["mlp_norm1"]], [row()], [vec()], name="res_norm1_bwd")

    g["b_w_o"] = _mm(o, dh3, ta=True, name="attn_out_dw")
    do = _mm(dh3, w["b_w_o"], tb=True, name="attn_out_dx")
    dqh, dkc, dkp, dvc, dvp, g["b_q_norm"], g["k_norm"], dsinks = _attn_bwd(
        qh, ksh, vsh, cos, sin, w["b_q_norm"], w["k_norm"], sinks, _to_heads(do, N_HEADS))
    g["b_sinks"] = dsinks[:, 0, 0].reshape(1, N_HEADS)
    dq = _from_heads(dqh)
    g["b_w_q"] = _mm(qn, dq, ta=True, name="q_proj_dw")
    dqn = _mm(dq, w["b_w_q"], tb=True, name="q_proj_dx")
    shift_blk = lambda z: jnp.pad(z[:, ATT_BLOCK:], ((0, 0), (0, ATT_BLOCK), (0, 0)))
    kv_parts = [_from_heads(z) for z in (dkc, shift_blk(dkp), dvc, shift_blk(dvp))]
    (dkv,) = _rowwise(lambda a, b, c, d: (jnp.concatenate([a + b, c + d], axis=1),), kv_parts, [],
                      [_sds((t, 2 * half_kv), BF16)], name="kv_grad_sum")
    g["w_kv"] = _mm(kvn, dkv, ta=True, name="kv_proj_dw")
    dkvn = _mm(dkv, w["w_kv"], tb=True, name="kv_proj_dx")

    def res_norm2_bwd(h, dnext, dna, dnb, gain_a, gain_b):
        dha, dga = _vjp_of(lambda hh, gg: (_rms(hh, gg),), 2, (0, 1))(h, gain_a, dna)
        dhb, dgb = _vjp_of(lambda hh, gg: (_rms(hh, gg),), 2, (0, 1))(h, gain_b, dnb)
        return dnext + dha + dhb, dga, dgb

    dh2, g["kv_norm"], g["b_norm"] = _rowwise(res_norm2_bwd, [h2, dh3, dkvn, dqn], [w["kv_norm"], w["b_norm"]],
                                              [row()], [vec(), vec()], name="res_norm_kvq_bwd")

    dhn0, g["mlp_w_up0"], g["mlp_w_down0"] = _mlp_bwd(hn0, u0, dh2, w["mlp_w_up0"], w["mlp_w_down0"], 0, up_dev_major)
    dh1, g["mlp_norm0"] = _rowwise(res_norm_bwd, [h1, dh2, dhn0], [w["mlp_norm0"]], [row()], [vec()], name="res_norm0_bwd")

    g["a_w_out"] = _mm(yg, dh1, ta=True, name="tmix_out_dw")
    dyg = _mm(dh1, w["a_w_out"], tb=True, name="tmix_out_dx")
    dy2, dgate = _rowwise(lambda d, a, b: (d * b, d * a), [dyg, y2, gate], [], [row(), rowb()], name="tmix_gate_bwd")
    g["a_g2"] = _mm(lg1, dgate, ta=True, a_pro=_sigmoid, name="tmix_g2_dw")

    def dsigmoid(rr, z):
        s = jax.nn.sigmoid(z)
        return rr * s * (1.0 - s)

    dlg1 = _mm(dgate, w["a_g2"], tb=True, epi=dsigmoid, epi_args=(lg1,), out_dtype=BF16, name="tmix_g2_dx")
    g["a_g1"] = _mm(xg, dlg1, ta=True, name="tmix_g1_dw")
    dxg = _mm(dlg1, w["a_g1"], tb=True, name="tmix_g1_dx")

    early_sums = ()
    if late_shards is not None:
        early_sums = _chip_sums([_device_major(name, g[name]) for name in _EARLY_GRADS], _EARLY_GRADS, "early")
    (dr, dk, dv, dwl, dal), param_grads, early_reduced = _wkv_bwd(r, k, v, wl, al, wkv_params, states, dy2, early_sums)
    for name, pg in zip(("a_w0", "a_a0", "a_k_k", "a_k_a", "a_ln_x_w", "a_ln_x_b", "a_r_k"), param_grads):
        g[name] = pg.reshape(1, D_MODEL)

    g["a_w_r"] = _mm(xr, dr, ta=True, name="tmix_r_dw")
    g["a_w_k"] = _mm(xk, dk, ta=True, name="tmix_k_dw")
    g["a_w_v"] = _mm(xv, dv, ta=True, name="tmix_v_dw")
    dxr = _mm(dr, w["a_w_r"], tb=True, name="tmix_r_dx")
    dxk = _mm(dk, w["a_w_k"], tb=True, name="tmix_k_dx")
    dxv = _mm(dv, w["a_w_v"], tb=True, name="tmix_v_dx")
    g["a_w2"] = _mm(lw1, dwl, ta=True, a_pro=jnp.tanh, name="tmix_w2_dw")

    def dtanh(rr, z):
        th = jnp.tanh(z)
        return rr * (1.0 - th * th)

    dlw1 = _mm(dwl, w["a_w2"], tb=True, epi=dtanh, epi_args=(lw1,), out_dtype=BF16, name="tmix_w2_dx")
    g["a_w1"] = _mm(xw, dlw1, ta=True, name="tmix_w1_dw")
    dxw = _mm(dlw1, w["a_w1"], tb=True, name="tmix_w1_dx")
    g["a_a2"] = _mm(la1, dal, ta=True, name="tmix_a2_dw")
    dla1 = _mm(dal, w["a_a2"], tb=True, out_dtype=BF16, name="tmix_a2_dx")
    g["a_a1"] = _mm(xa, dla1, ta=True, name="tmix_a1_dw")
    dxa = _mm(dla1, w["a_a1"], tb=True, name="tmix_a1_dx")

    lerp_bwd = _vjp_of(_time_shift_lerps, 9, tuple(range(9)))

    def lerp_bwd_rows(x_, xs_, d0, d1, d2, d3, d4, d5, gain, *mx):
        return lerp_bwd(x_, xs_, gain, *mx, d0, d1, d2, d3, d4, d5)

    outs = _rowwise(lerp_bwd_rows, [x, xs, dxr, dxk, dxv, dxw, dxa, dxg], [w["a_norm"]] + mix, [row(), row()], [vec()] * 7,
                    tm=128, name="tmix_lerp_bwd")
    dx_a, dxs, g["a_norm"] = outs[0], outs[1], outs[2]
    g["a_mix"] = jnp.concatenate(outs[3:9], axis=0)
    (grad_x,) = _rowwise(lambda a, b, c: (a + b + c,), [dh1, dx_a, _shift_up(dxs)], [], [row()], name="grad_x_sum")
    return loss, grad_x, g, (early_reduced if late_shards is not None else None)


def _device_major(name, grad):
    return grad if name.startswith("mlp_w_up") else grad.reshape((N_DEV, grad.shape[0] // N_DEV, grad.shape[1]))


_ANY = pl.BlockSpec(memory_space=pl.ANY)
_MESH_ID = pl.DeviceIdType.MESH
N_PEERS = N_DEV - 1


def _linear(pos):
    return 4 * pos[0] + 2 * pos[1] + pos[2]


def _all_gather(shards, name):
    n = len(shards)

    def body(*refs):
        start, finish = _gather_plan(refs[:n], refs[n:2 * n], *refs[2 * n:])
        start()
        finish()

    return pl.pallas_call(
        body, out_shape=[_sds((N_DEV,) + s.shape, s.dtype) for s in shards], in_specs=[_ANY] * n, out_specs=[_ANY] * n,
        scratch_shapes=_gather_semaphores(n), name=name,
    )(*shards)


def _gather_semaphores(n):
    return [pltpu.SemaphoreType.DMA((n * N_PEERS,)), pltpu.SemaphoreType.DMA((n * N_PEERS,)), pltpu.SemaphoreType.DMA((n,))]


def _gather_plan(ins, outs, send_sems, recv_sems, local_sems):
    n = len(ins)
    x, y, c = lax.axis_index("x"), lax.axis_index("y"), lax.axis_index("c")
    me, sibling = (x, y, c), (x, y, 1 - c)
    chips = [(1 - x, y), (x, 1 - y), (1 - x, 1 - y)]

    def copy(a, k, block, to, src=None):
        dst = outs[a].at[_linear(block)]
        return pltpu.make_async_remote_copy(
            src_ref=dst if src is None else src, dst_ref=dst, send_sem=send_sems.at[a * N_PEERS + k],
            recv_sem=recv_sems.at[a * N_PEERS + k], device_id=to, device_id_type=_MESH_ID)

    def own_copies():
        mine = [pltpu.make_async_copy(ins[a], outs[a].at[_linear(me)], local_sems.at[a]) for a in range(n)]
        first = []
        for a in range(n):
            first.append(copy(a, 0, me, sibling, src=ins[a]))
            first += [copy(a, 1 + j, me, (*chip, c), src=ins[a]) for j, chip in enumerate(chips)]
        return mine, first

    def start():
        mine, first = own_copies()
        for cp in mine + first:
            cp.start()

    def finish():
        mine, first = own_copies()
        passed = []
        for j, chip in enumerate(chips):
            for a in range(n):
                copy(a, 1 + j, (*chip, c), me).wait_recv()
                fwd = copy(a, 4 + j, (*chip, c), sibling)
                fwd.start()
                passed.append(fwd)
        for a in range(n):
            copy(a, 0, sibling, me).wait_recv()
            for j, chip in enumerate(chips):
                copy(a, 4 + j, (*chip, 1 - c), me).wait_recv()
        for cp in first + passed:
            cp.wait_send()
        for cp in mine:
            cp.wait()

    return start, finish


N_CHIPS = 4


def _exchange_with_sibling(parts, name):
    n = len(parts)

    def body(*refs):
        ins, outs = refs[:n], refs[n:2 * n]
        send_sems, recv_sems = refs[2 * n:]
        x, y, c = lax.axis_index("x"), lax.axis_index("y"), lax.axis_index("c")
        copies = []
        for a in range(n):
            for q in range(N_CHIPS):
                cp = pltpu.make_async_remote_copy(
                    src_ref=ins[a].at[2 * q + (1 - c)], dst_ref=outs[a].at[q], send_sem=send_sems.at[a * N_CHIPS + q],
                    recv_sem=recv_sems.at[a * N_CHIPS + q], device_id=(x, y, 1 - c), device_id_type=_MESH_ID)
                cp.start()
                copies.append(cp)
        for cp in copies:
            cp.wait()

    return pl.pallas_call(
        body, out_shape=[_sds((N_CHIPS,) + p.shape[1:], p.dtype) for p in parts], in_specs=[_ANY] * n, out_specs=[_ANY] * n,
        scratch_shapes=[pltpu.SemaphoreType.DMA((n * N_CHIPS,)), pltpu.SemaphoreType.DMA((n * N_CHIPS,))],
        name=name,
    )(*parts)


def _pair_sum(part, recv, core, out_dtype, name):
    _, r, cdim = recv.shape
    tr = max(8, min(r, (1 << 18) // cdim))
    assert r % tr == 0, (name, r, tr)

    def body(core_ref, p_ref, r_ref, o_ref):
        o_ref[...] = (p_ref[...] + r_ref[...]).astype(o_ref.dtype)

    grid_spec = pltpu.PrefetchScalarGridSpec(
        num_scalar_prefetch=1, grid=(N_CHIPS, r // tr),
        in_specs=[pl.BlockSpec((None, None, tr, cdim), lambda q, i, core_ref: (q, core_ref[0], i, 0)),
                  pl.BlockSpec((None, tr, cdim), lambda q, i, core_ref: (q, i, 0))],
        out_specs=pl.BlockSpec((None, tr, cdim), lambda q, i, core_ref: (q, i, 0)))
    return pl.pallas_call(
        body, grid_spec=grid_spec, out_shape=_sds((N_CHIPS, r, cdim), out_dtype),
        compiler_params=_params(("parallel", "parallel")), name=name,
    )(core, part.reshape(N_CHIPS, 2, r, cdim), recv)


def _exchange_between_chips(parts, name):
    n = len(parts)

    def body(*refs):
        start, finish = _chip_exchange_plan(refs[:n], refs[n:2 * n], *refs[2 * n:])
        start()
        finish()

    return pl.pallas_call(
        body, out_shape=[_sds(p.shape, p.dtype) for p in parts], in_specs=[_ANY] * n, out_specs=[_ANY] * n,
        scratch_shapes=_chip_exchange_semaphores(n), name=name,
    )(*parts)


def _chip_exchange_semaphores(n):
    n_other = N_CHIPS - 1
    return [pltpu.SemaphoreType.DMA((n * n_other,)), pltpu.SemaphoreType.DMA((n * n_other,)), pltpu.SemaphoreType.DMA((n,))]


def _chip_exchange_plan(ins, outs, send_sems, recv_sems, local_sems):
    n = len(ins)
    n_other = N_CHIPS - 1
    x, y, c = lax.axis_index("x"), lax.axis_index("y"), lax.axis_index("c")
    my_chip = 2 * x + y

    def all_copies():
        mine = [pltpu.make_async_copy(ins[a].at[my_chip], outs[a].at[my_chip], local_sems.at[a]) for a in range(n)]
        remote = []
        for j, (fx, fy) in enumerate([(1, 0), (0, 1), (1, 1)]):
            px, py = (1 - x if fx else x), (1 - y if fy else y)
            for a in range(n):
                remote.append(pltpu.make_async_remote_copy(
                    src_ref=ins[a].at[2 * px + py], dst_ref=outs[a].at[my_chip], send_sem=send_sems.at[a * n_other + j],
                    recv_sem=recv_sems.at[a * n_other + j], device_id=(px, py, c), device_id_type=_MESH_ID))
        return mine, remote

    def start():
        mine, remote = all_copies()
        for cp in mine + remote:
            cp.start()

    def finish():
        mine, remote = all_copies()
        for cp in remote + mine:
            cp.wait()

    return start, finish


def _chip_sums(parts, names, tag):
    core = lax.axis_index("c").astype(jnp.int32).reshape(1)
    from_sibling = _exchange_with_sibling(parts, name="scatter_grads_sibling_" + tag)
    return [_pair_sum(p, r, core, F32 if nm.startswith("pack") else BF16, name="pair_sum_" + nm)
            for p, r, nm in zip(parts, from_sibling, names)]


def _reduce_scatter(parts, names, tag):
    return _exchange_between_chips(_chip_sums(parts, names, tag), name="scatter_grads_chips_" + tag)


def _adamw(w, m, v, slots, name):
    r, c = w.shape
    ns = slots.shape[0]
    tr = max(8, min(r, (1 << 18) // c))
    assert r % tr == 0, (name, r, tr)

    def body(w_ref, m_ref, v_ref, g_ref, g_out, d_out, m_out, v_out):
        g = g_ref[0].astype(F32)
        for s in range(1, ns):
            g = g + g_ref[s].astype(F32)
        m_new = ADAM_B1 * m_ref[...] + (1.0 - ADAM_B1) * g
        v_new = ADAM_B2 * v_ref[...] + (1.0 - ADAM_B2) * jnp.square(g)
        m_hat = m_new / (1.0 - ADAM_B1 ** ADAM_STEP)
        v_hat = v_new / (1.0 - ADAM_B2 ** ADAM_STEP)
        d_out[...] = -ADAM_LR * (m_hat / (jnp.sqrt(v_hat) + ADAM_EPS) + ADAM_WD * w_ref[...])
        g_out[...], m_out[...], v_out[...] = g, m_new, v_new

    spec = pl.BlockSpec((tr, c), lambda i: (i, 0))
    return pl.pallas_call(
        body, grid=(r // tr,), in_specs=[spec, spec, spec, pl.BlockSpec((ns, tr, c), lambda i: (0, i, 0))],
        out_specs=[spec] * 4, out_shape=[_sds((r, c))] * 4, compiler_params=_params(("parallel",)), name=name,
    )(w, m, v, slots)


_COL_VECTORS = ("a_norm", "a_mix", "a_w0", "a_a0", "a_k_k", "a_k_a", "a_ln_x_w", "a_ln_x_b")
_COL_VEC_ROWS = 16
_COL_ROWS = _COL_VEC_ROWS + 2 * LORA_PAD + 256
_ROW_COLS = 2 * LORA_PAD + 256 + 512
_REPL_ROWS = 8


def _pad_to(a, size, axis):
    widths = [(0, 0)] * a.ndim
    widths[axis] = (0, size - a.shape[axis])
    return jnp.pad(a, widths)


def _pack_cols(p):
    width = p["a_norm"].shape[-1]
    vecs = jnp.concatenate([p[n].reshape(-1, width) for n in _COL_VECTORS], axis=0)
    return jnp.concatenate([_pad_to(vecs, _COL_VEC_ROWS, 0), _pad_to(p["a_w2"].reshape(-1, width), LORA_PAD, 0),
                            _pad_to(p["a_a2"].reshape(-1, width), LORA_PAD, 0), p["a_g2"].reshape(-1, width)], axis=0)


def _unpack_cols(a, lead):
    width = a.shape[-1]
    out, row = {}, 0
    for n in _COL_VECTORS:
        k = 6 if n == "a_mix" else 1
        out[n] = a[row:row + k].reshape(lead + ((6, width) if n == "a_mix" else (width,)))
        row += k
    base = _COL_VEC_ROWS
    out["a_w2"] = a[base:base + 96].reshape(lead + (96, width))
    out["a_a2"] = a[base + LORA_PAD:base + LORA_PAD + 96].reshape(lead + (96, width))
    out["a_g2"] = a[base + 2 * LORA_PAD:].reshape(lead + (256, width))
    return out


def _pack_rows(p):
    rows = p["w_kv"].shape[0]
    return jnp.concatenate([_pad_to(p["a_w1"].reshape(rows, -1), LORA_PAD, 1), _pad_to(p["a_a1"].reshape(rows, -1), LORA_PAD, 1),
                            p["a_g1"].reshape(rows, -1), p["w_kv"]], axis=1)


def _unpack_rows(a, lead):
    rows = a.shape[0]
    return {"a_w1": a[:, :96].reshape(lead + (rows, 96)), "a_a1": a[:, LORA_PAD:LORA_PAD + 96].reshape(lead + (rows, 96)),
            "a_g1": a[:, 2 * LORA_PAD:2 * LORA_PAD + 256].reshape(lead + (rows, 256)), "w_kv": a[:, 2 * LORA_PAD + 256:]}


def _pack_repl(p):
    row = lambda a: _pad_to(a.reshape(1, -1), D_MODEL, 1)
    return jnp.concatenate([p["mlp_norm"].reshape(2, D_MODEL), row(p["kv_norm"]), row(p["b_norm"]), row(p["a_r_k"]),
                            row(p["k_norm"]), row(p["b_q_norm"]), row(p["b_sinks"])], axis=0)


def _unpack_repl(a):
    return {"mlp_norm": a[0:2], "kv_norm": a[2], "b_norm": a[3:4], "a_r_k": a[4].reshape(1, N_HEADS, HEAD_DIM),
            "k_norm": a[5, :HEAD_DIM], "b_q_norm": a[6:7, :HEAD_DIM], "b_sinks": a[7:8, :N_HEADS]}


_WEIGHTS = ("a_norm", "a_mix", "a_w_rkv", "a_w0", "a_w1", "a_w2", "a_a0", "a_a1", "a_a2", "a_g1", "a_g2", "a_k_k", "a_k_a",
            "a_r_k", "a_ln_x_w", "a_ln_x_b", "a_w_out", "mlp_norm", "mlp_w_up", "mlp_w_down", "kv_norm", "w_kv", "k_norm",
            "b_norm", "b_w_q", "b_q_norm", "b_sinks", "b_w_o")


def _big_shards(p):
    return [p["a_w_rkv"][0, 0], p["a_w_rkv"][0, 1], p["a_w_rkv"][0, 2], p["a_w_out"][0], p["mlp_w_up"][0], p["mlp_w_up"][1],
            p["mlp_w_down"][0], p["mlp_w_down"][1], p["b_w_q"][0], p["b_w_o"][0]]


_BIG_NAMES = ("a_w_r", "a_w_k", "a_w_v", "a_w_out", "mlp_w_up0", "mlp_w_up1", "mlp_w_down0", "mlp_w_down1", "b_w_q", "b_w_o")


def kernel(x, a_norm, a_mix, a_w_rkv, a_w0, a_w1, a_w2, a_a0, a_a1, a_a2, a_g1, a_g2, a_k_k, a_k_a, a_r_k, a_ln_x_w,
           a_ln_x_b, a_w_out, mlp_norm, mlp_w_up, mlp_w_down, kv_norm, w_kv, k_norm, b_norm, b_w_q, b_q_norm, b_sinks,
           b_w_o, loss_target, m_a_norm, m_a_mix, m_a_w_rkv, m_a_w0, m_a_w1, m_a_w2, m_a_a0, m_a_a1, m_a_a2, m_a_g1,
           m_a_g2, m_a_k_k, m_a_k_a, m_a_r_k, m_a_ln_x_w, m_a_ln_x_b, m_a_w_out, m_mlp_norm, m_mlp_w_up, m_mlp_w_down,
           m_kv_norm, m_w_kv, m_k_norm, m_b_norm, m_b_w_q, m_b_q_norm, m_b_sinks, m_b_w_o, v_a_norm, v_a_mix, v_a_w_rkv,
           v_a_w0, v_a_w1, v_a_w2, v_a_a0, v_a_a1, v_a_a2, v_a_g1, v_a_g2, v_a_k_k, v_a_k_a, v_a_r_k, v_a_ln_x_w,
           v_a_ln_x_b, v_a_w_out, v_mlp_norm, v_mlp_w_up, v_mlp_w_down, v_kv_norm, v_w_kv, v_k_norm, v_b_norm, v_b_w_q,
           v_b_q_norm, v_b_sinks, v_b_w_o):
    given = locals()
    wts = {n: given[n] for n in _WEIGHTS}
    mom = {n: given["m_" + n] for n in _WEIGHTS}
    var = {n: given["v_" + n] for n in _WEIGHTS}

    cols_w, rows_w, repl_w = _pack_cols(wts), _pack_rows(wts), _pack_repl(wts)
    big_w = _big_shards(wts)
    big_bf16 = dict(zip(_BIG_NAMES, [b.astype(BF16) for b in big_w]))
    first_names = [k for k in _BIG_NAMES if k not in _LATE_WEIGHTS]
    gathered = _all_gather([cols_w, rows_w] + [big_bf16[k] for k in first_names], name="gather_weights")
    full_cols = gathered[0].transpose(1, 0, 2).reshape(_COL_ROWS, D_MODEL)
    full_rows = gathered[1].reshape(D_MODEL, _ROW_COLS)
    w = {}
    w.update({k: v.reshape(v.shape[1:]) for k, v in _unpack_cols(full_cols, (1,)).items()})
    for k in ("a_norm", "a_w0", "a_a0", "a_k_k", "a_k_a", "a_ln_x_w", "a_ln_x_b"):
        w[k] = w[k].reshape(1, D_MODEL)
    for k in ("a_w2", "a_a2"):
        w[k] = _pad_to(w[k], LORA_PAD, 0)
    rows_full = _unpack_rows(full_rows, ())
    w["a_w1"], w["a_a1"] = _pad_to(rows_full["a_w1"], LORA_PAD, 1), _pad_to(rows_full["a_a1"], LORA_PAD, 1)
    w["a_g1"], w["w_kv"] = rows_full["a_g1"], rows_full["w_kv"]
    for k, arr in zip(first_names, gathered[2:]):
        w[k] = arr.reshape(N_DEV * arr.shape[1], arr.shape[2])
    w["mlp_norm0"], w["mlp_norm1"] = mlp_norm[0:1], mlp_norm[1:2]
    w["kv_norm"], w["k_norm"] = kv_norm.reshape(1, D_MODEL), k_norm.reshape(1, HEAD_DIM)
    w["b_norm"], w["b_q_norm"], w["b_sinks"], w["a_r_k"] = b_norm, b_q_norm, b_sinks, a_r_k.reshape(1, D_MODEL)

    loss_local, grad_x, g, early_reduced = _local_step(x[0], loss_target[0], w, [big_bf16[k] for k in _LATE_WEIGHTS])
    loss = lax.psum(loss_local[0, 0], MESH_AXES)

    g_lead = {k: g[k][None] for k in ("a_norm", "a_mix", "a_w0", "a_a0", "a_k_k", "a_k_a", "a_ln_x_w", "a_ln_x_b", "a_g2")}
    g_lead["a_w2"], g_lead["a_a2"] = g["a_w2"][None, :96], g["a_a2"][None, :96]
    g_cols = _pack_cols(g_lead).reshape(_COL_ROWS, N_DEV, D_MODEL // N_DEV).transpose(1, 0, 2)
    g_rows = _pack_rows({"a_w1": g["a_w1"][:, :96], "a_a1": g["a_a1"][:, :96], "a_g1": g["a_g1"], "w_kv": g["w_kv"]})
    g_rows = g_rows.reshape(N_DEV, D_MODEL // N_DEV, _ROW_COLS)
    late_names = tuple(k for k in _BIG_NAMES if k not in _EARLY_GRADS)
    late_reduced = _reduce_scatter([g_cols, g_rows] + [_device_major(k, g[k]) for k in late_names],
                                   ("pack_cols", "pack_rows") + late_names, "late")
    big_reduced = dict(zip(late_names, late_reduced[2:]))
    big_reduced.update(zip(_EARLY_GRADS, early_reduced))
    reduced = list(late_reduced[:2]) + [big_reduced[k] for k in _BIG_NAMES]
    g_repl = _pack_repl({"mlp_norm": jnp.concatenate([g["mlp_norm0"], g["mlp_norm1"]], axis=0), "kv_norm": g["kv_norm"],
                         "b_norm": g["b_norm"], "a_r_k": g["a_r_k"], "k_norm": g["k_norm"], "b_q_norm": g["b_q_norm"],
                         "b_sinks": g["b_sinks"]})
    (repl_slots,) = _all_gather([g_repl], name="gather_replicated_grads")

    res = {}
    cols4 = _adamw(cols_w, _pack_cols(mom), _pack_cols(var), reduced[0], name="adamw_cols")
    rows4 = _adamw(rows_w, _pack_rows(mom), _pack_rows(var), reduced[1], name="adamw_rows")
    repl4 = _adamw(repl_w, _pack_repl(mom), _pack_repl(var), repl_slots, name="adamw_replicated")
    for unpacked in ([_unpack_cols(a, (1,)) for a in cols4], [_unpack_rows(a, (1,)) for a in rows4], [_unpack_repl(a) for a in repl4]):
        for k in unpacked[0]:
            res[k] = tuple(u[k] for u in unpacked)
    big4 = [_adamw(bw, bm, bv, slots, name="adamw_" + k)
            for k, bw, bm, bv, slots in zip(_BIG_NAMES, big_w, _big_shards(mom), _big_shards(var), reduced[2:])]
    res["a_w_rkv"] = tuple(jnp.stack([big4[0][i], big4[1][i], big4[2][i]])[None] for i in range(4))
    res["a_w_out"] = tuple(a[None] for a in big4[3])
    res["mlp_w_up"] = tuple(jnp.stack([big4[4][i], big4[5][i]]) for i in range(4))
    res["mlp_w_down"] = tuple(jnp.stack([big4[6][i], big4[7][i]]) for i in range(4))
    res["b_w_q"] = tuple(a[None] for a in big4[8])
    res["b_w_o"] = tuple(a[None] for a in big4[9])
    res["w_kv"] = tuple(a.reshape(w_kv.shape) for a in res["w_kv"])

    outs = [loss, grad_x[None]]
    for i in range(4):
        outs += [res[n][i].reshape(given[n].shape) for n in _WEIGHTS]
    return tuple(outs)
```

```python
import functools
import math

import jax
import jax.numpy as jnp
from jax import lax
from jax.experimental import pallas as pl
from jax.experimental.pallas import tpu as pltpu

F32 = jnp.float32
BF16 = jnp.bfloat16

D_MODEL = 2048
N_HEADS = 32
HEAD_DIM = 64
N_KV_HEADS = 4
Q_PER_KV = 8
ATT_BLOCK = 128
WKV_CHUNK = 64
LORA_PAD = 128
D_FF = 8192
N_DEV = 8
RMS_EPS = 1e-6
GN_EPS = 64e-5
L2_EPS = 1e-12
ROPE_THETA = 10000.0
ADAM_LR, ADAM_B1, ADAM_B2, ADAM_EPS, ADAM_WD, ADAM_STEP = 0.001, 0.9, 0.999, 1e-08, 0.01, 10
MASK_VALUE = -1e30
VMEM_LIMIT_BYTES = 56 * 1024 * 1024
MESH_AXES = ("x", "y", "c")
HI = lax.Precision.HIGHEST

_NN = (((1,), (0,)), ((), ()))
_NT = (((1,), (1,)), ((), ()))
_TN = (((0,), (0,)), ((), ()))
_BNN = (((2,), (1,)), ((0,), (0,)))
_BNT = (((2,), (2,)), ((0,), (0,)))
_BTN = (((1,), (1,)), ((0,), (0,)))


def _params(sem):
    return pltpu.CompilerParams(dimension_semantics=sem, vmem_limit_bytes=VMEM_LIMIT_BYTES)


def _split2(a):
    hi = a.astype(BF16)
    return hi, (a - hi.astype(F32)).astype(BF16)


def _dot3(a, b, dims):
    ah, al = _split2(a)
    bh, bl = _split2(b)
    d = lambda p, q: lax.dot_general(p, q, dims, preferred_element_type=F32)
    return d(ah, bh) + (d(al, bh) + d(ah, bl))


@functools.partial(jax.custom_vjp, nondiff_argnums=(2,))
def _hdot(a, b, dims=_NN):
    return _dot3(a, b, dims)


def _hdot_fwd(a, b, dims):
    return _dot3(a, b, dims), (a, b)


def _hdot_bwd(dims, res, g):
    a, b = res
    nn, nt, tn = (_NN, _NT, _TN) if dims in (_NN, _NT, _TN) else (_BNN, _BNT, _BTN)
    if dims == nn:
        return _dot3(g, b, nt), _dot3(a, g, tn)
    if dims == nt:
        return _dot3(g, b, nn), _dot3(g, a, tn)
    assert dims == tn
    return _dot3(b, g, nt), _dot3(a, g, nn)


_hdot.defvjp(_hdot_fwd, _hdot_bwd)


def _tri_parts(x):
    hi = x.astype(BF16)
    r1 = x - hi.astype(F32)
    mid = r1.astype(BF16)
    return hi, mid, (r1 - mid.astype(F32)).astype(BF16)


@jax.custom_vjp
def _mask_dot(mask, x):
    mb = mask.astype(BF16)
    p0, p1, p2 = _tri_parts(x)
    d = lambda p: lax.dot_general(mb, p, _BNN, preferred_element_type=F32)
    return d(p0) + (d(p1) + d(p2))


def _mask_dot_fwd(mask, x):
    return _mask_dot(mask, x), mask


def _mask_dot_bwd(mask, g):
    mb = mask.astype(BF16)
    p0, p1, p2 = _tri_parts(g)
    d = lambda p: lax.dot_general(mb, p, _BTN, preferred_element_type=F32)
    return jnp.zeros_like(mask), d(p0) + (d(p1) + d(p2))


_mask_dot.defvjp(_mask_dot_fwd, _mask_dot_bwd)


def _b16dot(a, b, dims):
    return lax.dot_general(a.astype(BF16), b.astype(BF16), dims, preferred_element_type=F32)


@jax.custom_vjp
def _bdot(a, b):
    return _b16dot(a, b, _NN)


def _bdot_fwd(a, b):
    return _b16dot(a, b, _NN), (a, b)


def _bdot_bwd(res, g):
    a, b = res
    return _b16dot(g, b, _NT), _b16dot(a, g, _TN)


_bdot.defvjp(_bdot_fwd, _bdot_bwd)


@jax.custom_vjp
def _bdot_nt(a, b):
    return _b16dot(a, b, _NT)


def _bdot_nt_fwd(a, b):
    return _b16dot(a, b, _NT), (a, b)


def _bdot_nt_bwd(res, g):
    a, b = res
    return _b16dot(g, b, _NN), _b16dot(g, a, _TN)


_bdot_nt.defvjp(_bdot_nt_fwd, _bdot_nt_bwd)


def _rms(x, gain):
    return x * lax.rsqrt(jnp.mean(x * x, axis=-1, keepdims=True) + RMS_EPS) * gain


def _vjp_of(f, n_in, diff):
    def g(*args):
        ins, cts = args[:n_in], args[n_in:]

        def fd(*d):
            full = list(ins)
            for pos, i in enumerate(diff):
                full[i] = d[pos]
            return f(*full)

        _, pull = jax.vjp(fd, *[ins[i] for i in diff])
        return pull(tuple(cts))
    return g


def _mm(a, b, *, name, ta=False, tb=False, a_pro=None, epi=None, epi_args=(), out_dtype=F32,
        tm=1024, tn=1024, tk=2048, dims=None, b_spec=None, o_spec=None, o_shape=None):
    if dims is None:
        m, k = (a.shape[1], a.shape[0]) if ta else a.shape
        n = b.shape[0] if tb else b.shape[1]
    else:
        m, n, k = dims
    tm, tn, tk = min(tm, m), min(tn, n), min(tk, k)
    assert m % tm == 0 and n % tn == 0 and k % tk == 0, (name, m, n, k, tm, tn, tk)
    nk = k // tk
    ne = len(epi_args)
    cdims = (((0 if ta else 1,), (1 if tb else 0,)), ((), ()))

    def body(a_ref, b_ref, *rest):
        e_refs, o_ref, acc = rest[:ne], rest[ne], rest[ne + 1]
        kk = pl.program_id(2)

        @pl.when(kk == 0)
        def _():
            acc[...] = jnp.zeros_like(acc)

        av = a_ref[...]
        if a_pro is not None:
            av = a_pro(av.astype(F32))
        acc[...] += lax.dot_general(av.astype(BF16), b_ref[...].astype(BF16), cdims, preferred_element_type=F32)

        @pl.when(kk == nk - 1)
        def _():
            r = acc[...]
            if epi is not None:
                r = epi(r, *[e[...] for e in e_refs])
            o_ref[...] = r.astype(o_ref.dtype)

    a_spec = pl.BlockSpec((tk, tm), lambda i, j, q: (q, i)) if ta else pl.BlockSpec((tm, tk), lambda i, j, q: (i, q))
    if b_spec is None:
        b_spec = pl.BlockSpec((tn, tk), lambda i, j, q: (j, q)) if tb else pl.BlockSpec((tk, tn), lambda i, j, q: (q, j))
    if o_spec is None:
        o_spec = pl.BlockSpec((tm, tn), lambda i, j, q: (i, j))
        o_shape = (m, n)
    e_specs = [pl.BlockSpec((tm, tn), lambda i, j, q: (i, j)) for _ in epi_args]
    return pl.pallas_call(
        body, grid=(m // tm, n // tn, nk), in_specs=[a_spec, b_spec] + e_specs, out_specs=o_spec,
        out_shape=jax.ShapeDtypeStruct(o_shape, out_dtype), scratch_shapes=[pltpu.VMEM((tm, tn), F32)],
        compiler_params=_params(("parallel", "parallel", "arbitrary")), name=name,
    )(a, b, *epi_args)


def _rowwise(fn, rows, params, out_rows, out_params=(), *, tm=256, name):
    t = rows[0].shape[0]
    tm = min(tm, t)
    assert t % tm == 0
    nr, npar, nor, nop = len(rows), len(params), len(out_rows), len(out_params)

    def body(*refs):
        r, p = refs[:nr], refs[nr:nr + npar]
        o, op = refs[nr + npar:nr + npar + nor], refs[nr + npar + nor:]
        outs = fn(*[x[...] for x in r], *[x[...] for x in p])
        for ref, val in zip(o, outs[:nor]):
            ref[...] = val.astype(ref.dtype)
        if nop:
            @pl.when(pl.program_id(0) == 0)
            def _():
                for ref in op:
                    ref[...] = jnp.zeros_like(ref)

            for ref, val in zip(op, outs[nor:]):
                ref[...] += val.astype(F32)

    in_specs = [pl.BlockSpec((tm, x.shape[1]), lambda i: (i, 0)) for x in rows]
    in_specs += [pl.BlockSpec(p.shape, lambda i: (0, 0)) for p in params]
    out_specs = [pl.BlockSpec((tm, s.shape[1]), lambda i: (i, 0)) for s in out_rows]
    out_specs += [pl.BlockSpec(s.shape, lambda i: (0, 0)) for s in out_params]
    return pl.pallas_call(
        body, grid=(t // tm,), in_specs=in_specs, out_specs=out_specs, out_shape=list(out_rows) + list(out_params),
        compiler_params=_params(("arbitrary",)), name=name,
    )(*rows, *params)


def _sds(shape, dtype=F32):
    return jax.ShapeDtypeStruct(tuple(shape), dtype)


def _doubling_powers(l):
    powers = [l]
    for _ in range(int(math.log2(l.shape[-1])) - 1):
        powers.append(_dot3(powers[-1], powers[-1], _BNN))
    return powers


@jax.custom_vjp
def _unit_lower_solve(l, z):
    u = z
    for p in _doubling_powers(l):
        u = u + _dot3(p, u, _BNN)
    return u


def _unit_lower_solve_fwd(l, z):
    powers = _doubling_powers(l)
    u = z
    for p in powers:
        u = u + _dot3(p, u, _BNN)
    return u, (powers, u)


def _unit_lower_solve_bwd(res, du):
    powers, u = res
    g = du
    for p in powers:
        g = g + _dot3(p, g, _BTN)
    return _dot3(g, u, _BNT), g


_unit_lower_solve.defvjp(_unit_lower_solve_fwd, _unit_lower_solve_bwd)


def _wkv_chunk(s0, r, lw, k, v, a, b):
    nb, c, _ = r.shape
    ti = lax.broadcasted_iota(jnp.int32, (nb, c, c), 1)
    si = lax.broadcasted_iota(jnp.int32, (nb, c, c), 2)
    incl, strict = si <= ti, si < ti
    cum = _mask_dot(incl.astype(F32), lw)
    rcum = _mask_dot((si > ti).astype(F32), lw)
    tot = jnp.sum(lw, axis=1, keepdims=True)
    w_inv = jnp.exp(-cum)
    at, rt, bt, kt = a * jnp.exp(cum - lw), r * jnp.exp(cum), b * w_inv, k * w_inv
    l_ab = jnp.where(strict, _hdot(at, bt, _BNT), 0.0)
    l_ak = jnp.where(strict, _hdot(at, kt, _BNT), 0.0)
    t_rb = jnp.where(incl, _hdot(rt, bt, _BNT), 0.0)
    t_rk = jnp.where(incl, _hdot(rt, kt, _BNT), 0.0)
    u = _unit_lower_solve(l_ab, _hdot(at, s0, _BNT) + _hdot(l_ak, v, _BNN))
    y = _hdot(rt, s0, _BNT) + _hdot(t_rb, u, _BNN) + _hdot(t_rk, v, _BNN)
    e = jnp.exp(rcum)
    s1 = s0 * jnp.exp(tot) + _hdot(u, b * e, _BTN) + _hdot(v, k * e, _BTN)
    return y, s1


WKV_HEADS_PER_STEP = 16


def _first_and_last_step(grid):
    i, j = pl.program_id(0), pl.program_id(1)
    return jnp.logical_and(i == 0, j == 0), jnp.logical_and(i == grid[0] - 1, j == grid[1] - 1)


N_WKV_PARAMS = 7


def _tmix_chunk(s0, r, k, v, wl, al, w0, a0, k_k, k_a, ln_w, ln_b, r_k):
    lw, kmod, a, b = _wkv_prep(k, wl, al, w0, a0, k_k, k_a)
    y, s1 = _wkv_chunk(s0, r, lw, kmod, v, a, b)
    (y2,) = _wkv_post(y, r, kmod, v, ln_w, ln_b, r_k)
    return y2, s1


def _split_heads(x, nh):
    return x.reshape(x.shape[0], nh, HEAD_DIM).transpose(1, 0, 2)


def _merge_heads(xh):
    return jnp.concatenate([xh[h] for h in range(xh.shape[0])], axis=1)


def _wkv_fwd(r, k, v, wl, al, params, shards=()):
    t = r.shape[0]
    nh, n = N_HEADS, HEAD_DIM
    nc = t // WKV_CHUNK
    hb = WKV_HEADS_PER_STEP
    grid = (nh // hb, nc)
    ns = len(shards)
    n_in = 5 + N_WKV_PARAMS

    def body(*refs):
        y_ref, s_ref = refs[n_in + ns:n_in + ns + 2]
        state = refs[n_in + 2 * ns + 2]
        if ns:
            start, finish = _gather_plan(refs[n_in:n_in + ns], refs[n_in + ns + 2:n_in + 2 * ns + 2], *refs[n_in + 2 * ns + 3:])
            first, last = _first_and_last_step(grid)
            pl.when(first)(start)

        @pl.when(pl.program_id(1) == 0)
        def _():
            state[...] = jnp.zeros_like(state)

        s0 = state[...]
        s_ref[:, 0] = s0
        rows = [_split_heads(ref[...], hb) for ref in refs[:5]]
        y, s1 = _tmix_chunk(s0, *rows, *[ref[...] for ref in refs[5:n_in]])
        y_ref[...] = _merge_heads(y)
        state[...] = s1
        if ns:
            pl.when(last)(finish)

    blk = pl.BlockSpec((WKV_CHUNK, hb * n), lambda h, c: (c, h))
    pblk = pl.BlockSpec((hb, 1, n), lambda h, c: (h, 0, 0))
    sblk = pl.BlockSpec((hb, 1, n, n), lambda h, c: (h, c, 0, 0))
    outs = pl.pallas_call(
        body, grid=grid, in_specs=[blk] * 5 + [pblk] * N_WKV_PARAMS + [_ANY] * ns, out_specs=[blk, sblk] + [_ANY] * ns,
        out_shape=[_sds((t, nh * n)), _sds((nh, nc, n, n))] + [_sds((N_DEV,) + s.shape, s.dtype) for s in shards],
        scratch_shapes=[pltpu.VMEM((hb, n, n), F32)] + (_gather_semaphores(ns) if ns else []),
        compiler_params=_params(("arbitrary", "arbitrary")), name="wkv_fwd",
    )(r, k, v, wl, al, *params, *shards)
    return outs[0], outs[1], list(outs[2:])


def _wkv_bwd(r, k, v, wl, al, params, states, dy, chip_sums=()):
    t = r.shape[0]
    nh, n = N_HEADS, HEAD_DIM
    nc = t // WKV_CHUNK
    hb = WKV_HEADS_PER_STEP
    grid = (nh // hb, nc)
    ns = len(chip_sums)
    n_in = 5 + N_WKV_PARAMS
    n_out = 5 + N_WKV_PARAMS

    def body(*refs):
        s_ref, dy_ref = refs[n_in:n_in + 2]
        out_refs = refs[n_in + 2 + ns:n_in + 2 + ns + n_out]
        dstate = refs[n_in + 2 + 2 * ns + n_out]
        if ns:
            start, finish = _chip_exchange_plan(refs[n_in + 2:n_in + 2 + ns], refs[n_in + 2 + ns + n_out:n_in + 2 + 2 * ns + n_out],
                                                *refs[n_in + 3 + 2 * ns + n_out:])
            first, last = _first_and_last_step(grid)
            pl.when(first)(start)

        @pl.when(pl.program_id(1) == 0)
        def _():
            dstate[...] = jnp.zeros_like(dstate)
            for ref in out_refs[5:]:
                ref[...] = jnp.zeros_like(ref)

        rows = [_split_heads(ref[...], hb) for ref in refs[:5]]
        _, pull = jax.vjp(_tmix_chunk, s_ref[:, 0], *rows, *[ref[...] for ref in refs[5:n_in]])
        grads = pull((_split_heads(dy_ref[...], hb), dstate[...]))
        dstate[...] = grads[0]
        for ref, val in zip(out_refs[:5], grads[1:6]):
            ref[...] = _merge_heads(val).astype(ref.dtype)
        for ref, val in zip(out_refs[5:], grads[6:]):
            ref[...] += val
        if ns:
            pl.when(last)(finish)

    blk = pl.BlockSpec((WKV_CHUNK, hb * n), lambda h, c: (nc - 1 - c, h))
    pblk = pl.BlockSpec((hb, 1, n), lambda h, c: (h, 0, 0))
    sblk = pl.BlockSpec((hb, 1, n, n), lambda h, c: (h, nc - 1 - c, 0, 0))
    outs = pl.pallas_call(
        body, grid=grid, in_specs=[blk] * 5 + [pblk] * N_WKV_PARAMS + [sblk, blk] + [_ANY] * ns,
        out_specs=[blk] * 5 + [pblk] * N_WKV_PARAMS + [_ANY] * ns,
        out_shape=[_sds((t, nh * n), BF16)] * 5 + [_sds((nh, 1, n))] * N_WKV_PARAMS + [_sds(p.shape, p.dtype) for p in chip_sums],
        scratch_shapes=[pltpu.VMEM((hb, n, n), F32)] + (_chip_exchange_semaphores(ns) if ns else []),
        compiler_params=_params(("arbitrary", "arbitrary")), name="wkv_bwd",
    )(r, k, v, wl, al, *params, states, dy, *chip_sums)
    return outs[:5], outs[5:n_out], list(outs[n_out:])


def _wkv_prep(k, wl, al, w0, a0, k_k, k_a):
    z = -(w0 + wl)
    softplus = jnp.maximum(z, 0.0) + jnp.log1p(jnp.exp(-jnp.abs(z)))
    lw = -jnp.exp(-softplus - 0.5)
    asig = jax.nn.sigmoid(a0 + al)
    kk = k * k_k
    kk = kk / jnp.maximum(jnp.sqrt(jnp.sum(kk * kk, axis=-1, keepdims=True)), L2_EPS)
    kmod = k * (1.0 + (asig - 1.0) * k_a)
    return lw, kmod, -kk, kk * asig


def _wkv_post(y, r, kmod, v, ln_w, ln_b, r_k):
    mu = jnp.mean(y, axis=-1, keepdims=True)
    var = jnp.mean(jnp.square(y - mu), axis=-1, keepdims=True)
    yn = (y - mu) * lax.rsqrt(var + GN_EPS)
    yn = yn * ln_w + ln_b
    return (yn + jnp.sum(r * kmod * r_k, axis=-1, keepdims=True) * v,)


def _attn_group(nonzero_block, q, kc, kp, vc, vp, cos_c, sin_c, cos_p, sin_p, q_gain, k_gain, sinks):
    ri = lax.broadcasted_iota(jnp.int32, (HEAD_DIM, HEAD_DIM), 0)
    ci = lax.broadcasted_iota(jnp.int32, (HEAD_DIM, HEAD_DIM), 1)
    half = HEAD_DIM // 2
    rot = jnp.where(ri == ci + half, -1.0, 0.0) + jnp.where(ri + half == ci, 1.0, 0.0)
    rows = Q_PER_KV * ATT_BLOCK

    def rope(x, cos, sin):
        return x * cos + _hdot(x, rot, _NN) * sin

    kcr = rope(_rms(kc, k_gain), cos_c, sin_c)
    kpr = rope(_rms(kp, k_gain), cos_p, sin_p)
    qn = _rms(q, q_gain)
    qr = qn * cos_c + _hdot(qn.reshape(rows, HEAD_DIM), rot, _NN).reshape(q.shape) * sin_c
    q2 = qr.reshape(rows, HEAD_DIM)
    qi = lax.broadcasted_iota(jnp.int32, (1, ATT_BLOCK, ATT_BLOCK), 1)
    ki = lax.broadcasted_iota(jnp.int32, (1, ATT_BLOCK, ATT_BLOCK), 2)
    mask_c = ki <= qi
    mask_p = jnp.logical_and(ki > qi, nonzero_block)
    lane0 = (lax.broadcasted_iota(jnp.int32, (1, 1, 128), 2) == 0).astype(F32)
    shape3 = (Q_PER_KV, ATT_BLOCK, ATT_BLOCK)
    sc = jnp.where(mask_c, (_bdot_nt(q2, kcr) * (HEAD_DIM ** -0.5)).reshape(shape3), MASK_VALUE)
    sp = jnp.where(mask_p, (_bdot_nt(q2, kpr) * (HEAD_DIM ** -0.5)).reshape(shape3), MASK_VALUE)
    sk = jnp.sum(sinks * lane0, axis=2, keepdims=True)
    mx = jnp.maximum(jnp.maximum(jnp.max(sc, axis=2, keepdims=True), jnp.max(sp, axis=2, keepdims=True)), sk)
    mx = lax.stop_gradient(mx)
    ec, ep = jnp.exp(sc - mx), jnp.exp(sp - mx)
    den = jnp.sum(ec, axis=2, keepdims=True) + jnp.sum(ep, axis=2, keepdims=True) + jnp.exp(sk - mx)
    out = _bdot((ec / den).reshape(rows, ATT_BLOCK), vc) + _bdot((ep / den).reshape(rows, ATT_BLOCK), vp)
    return out.reshape(q.shape)


def _attn_rows(nonzero_block, qs, kcs, kps, vcs, vps, tabs, q_gain, k_gain, sinks):
    return [_attn_group(nonzero_block, qs[g], kcs[g], kps[g], vcs[g], vps[g], *tabs, q_gain, k_gain, sinks[g])
            for g in range(N_KV_HEADS)]


def _attn_operands(q_tile, kvc_tile, kvp_tile, sink_ref):
    q3 = _split_heads(q_tile, N_HEADS)
    kvc, kvp = _split_heads(kvc_tile, 2 * N_KV_HEADS), _split_heads(kvp_tile, 2 * N_KV_HEADS)
    groups = range(N_KV_HEADS)
    qs = [q3[Q_PER_KV * g:Q_PER_KV * (g + 1)] for g in groups]
    sinks = [sink_ref[Q_PER_KV * g:Q_PER_KV * (g + 1)] for g in groups]
    return (qs, [kvc[g] for g in groups], [kvp[g] for g in groups], [kvc[N_KV_HEADS + g] for g in groups],
            [kvp[N_KV_HEADS + g] for g in groups], sinks)


def _attn_specs(t):
    nb = t // ATT_BLOCK
    prev = lambda n: jnp.maximum(n - 1, 0)
    kv_width = 2 * N_KV_HEADS * HEAD_DIM
    q_spec = pl.BlockSpec((ATT_BLOCK, D_MODEL), lambda n: (n, 0))
    kv_c = pl.BlockSpec((ATT_BLOCK, kv_width), lambda n: (n, 0))
    kv_p = pl.BlockSpec((ATT_BLOCK, kv_width), lambda n: (prev(n), 0))
    tab_c = pl.BlockSpec((ATT_BLOCK, HEAD_DIM), lambda n: (n, 0))
    tab_p = pl.BlockSpec((ATT_BLOCK, HEAD_DIM), lambda n: (prev(n), 0))
    gain = pl.BlockSpec((1, HEAD_DIM), lambda n: (0, 0))
    sink = pl.BlockSpec((N_HEADS, 1, 128), lambda n: (0, 0, 0))
    return nb, q_spec, kv_c, kv_p, tab_c, tab_p, gain, sink


def _attn_fwd(q, kv, cos, sin, q_gain, k_gain, sinks):
    t = q.shape[0]
    nb, q_spec, kv_c, kv_p, tab_c, tab_p, gain, sink = _attn_specs(t)

    def body(q_ref, kvc, kvp, cc, sc, cp, sp, qg, kg, sk, o_ref):
        qs, kcs, kps, vcs, vps, sinks_ = _attn_operands(q_ref[...], kvc[...], kvp[...], sk)
        outs = _attn_rows(pl.program_id(0) > 0, qs, kcs, kps, vcs, vps, (cc[...], sc[...], cp[...], sp[...]),
                          qg[...], kg[...], sinks_)
        o_ref[...] = _merge_heads(jnp.concatenate(outs, axis=0)).astype(o_ref.dtype)

    return pl.pallas_call(
        body, grid=(nb,), in_specs=[q_spec, kv_c, kv_p, tab_c, tab_c, tab_p, tab_p, gain, gain, sink],
        out_specs=q_spec, out_shape=_sds(q.shape, BF16), compiler_params=_params(("arbitrary",)), name="attn_fwd",
    )(q, kv, kv, cos, sin, cos, sin, q_gain, k_gain, sinks)


def _attn_bwd(q, kv, cos, sin, q_gain, k_gain, sinks, do):
    t = q.shape[0]
    nb, q_spec, kv_c, kv_p, tab_c, tab_p, gain, sink = _attn_specs(t)

    def body(q_ref, kvc, kvp, cc, sc, cp, sp, qg, kg, sk, do_ref, dq_ref, dkvc_ref, dkvp_ref, dqg_ref, dkg_ref, dsk_ref):
        nonzero = pl.program_id(0) > 0
        tabs = (cc[...], sc[...], cp[...], sp[...])
        qs, kcs, kps, vcs, vps, sinks_ = _attn_operands(q_ref[...], kvc[...], kvp[...], sk)

        def f(qs_, kcs_, kps_, vcs_, vps_, qgv, kgv, sks):
            return _attn_rows(nonzero, qs_, kcs_, kps_, vcs_, vps_, tabs, qgv, kgv, sks)

        _, pull = jax.vjp(f, qs, kcs, kps, vcs, vps, qg[...], kg[...], sinks_)
        do3 = _split_heads(do_ref[...], N_HEADS)
        dqs, dkcs, dkps, dvcs, dvps, dqg, dkg, dsks = pull([do3[Q_PER_KV * g:Q_PER_KV * (g + 1)] for g in range(N_KV_HEADS)])
        dq_ref[...] = _merge_heads(jnp.concatenate(dqs, axis=0)).astype(dq_ref.dtype)
        dkvc_ref[...] = jnp.concatenate(dkcs + dvcs, axis=1)
        dkvp_ref[...] = jnp.concatenate(dkps + dvps, axis=1)

        @pl.when(pl.program_id(0) == 0)
        def _():
            dqg_ref[...] = jnp.zeros_like(dqg_ref)
            dkg_ref[...] = jnp.zeros_like(dkg_ref)
            dsk_ref[...] = jnp.zeros_like(dsk_ref)

        dqg_ref[...] += dqg
        dkg_ref[...] += dkg
        for g in range(N_KV_HEADS):
            dsk_ref[Q_PER_KV * g:Q_PER_KV * (g + 1)] += dsks[g]

    return pl.pallas_call(
        body, grid=(nb,), in_specs=[q_spec, kv_c, kv_p, tab_c, tab_c, tab_p, tab_p, gain, gain, sink, q_spec],
        out_specs=[q_spec, kv_c, kv_c, gain, gain, sink],
        out_shape=[_sds(q.shape, BF16), _sds(kv.shape), _sds(kv.shape), _sds((1, HEAD_DIM)), _sds((1, HEAD_DIM)), _sds(sinks.shape)],
        compiler_params=_params(("arbitrary",)), name="attn_bwd",
    )(q, kv, kv, cos, sin, cos, sin, q_gain, k_gain, sinks, do)


def _time_shift_lerps(x, xs, gain, *mix):
    xn, xsn = _rms(x, gain), _rms(xs, gain)
    xx = xsn - xn
    return tuple(xn + xx * m for m in mix)


def _residual_norm(h, delta, gain):
    hn = h + delta
    return hn, _rms(hn, gain)


def _residual_norm2(h, delta, gain_a, gain_b):
    hn = h + delta
    return hn, _rms(hn, gain_a), _rms(hn, gain_b)


def _relu2(u):
    return jnp.square(jnp.maximum(u, 0.0))


def _sigmoid(z):
    return jax.nn.sigmoid(z)


def _shift_down(x):
    return jnp.pad(x[:-1], ((1, 0), (0, 0)))


def _shift_up(x):
    return jnp.pad(x[1:], ((0, 1), (0, 0)))


def _rope_tables(t):
    half = HEAD_DIM // 2
    inv_freq = jnp.power(ROPE_THETA, -jnp.arange(half, dtype=F32) / half)
    ang = jnp.arange(t, dtype=jnp.int32).astype(F32)[:, None] * inv_freq[None, :]
    cos, sin = jnp.cos(ang), jnp.sin(ang)
    return jnp.concatenate([cos, cos], axis=1), jnp.concatenate([sin, sin], axis=1)


def _mlp_fwd(hn, w_up, w_down, layer, up_dev_major):
    t = hn.shape[0]
    if up_dev_major:
        cw = w_up.shape[2]
        u = _mm(hn, w_up, name=f"mlp{layer}_up", dims=(t, D_FF, D_MODEL), tn=cw, tk=D_MODEL,
                b_spec=pl.BlockSpec((None, D_MODEL, cw), lambda i, j, q: (j, q, 0)))
    else:
        u = _mm(hn, w_up, name=f"mlp{layer}_up")
    out = _mm(u, w_down, a_pro=_relu2, name=f"mlp{layer}_down")
    return u, out


def _mlp_bwd(hn, u, dh, w_up, w_down, layer, up_dev_major):
    t = hn.shape[0]
    du = _mm(dh, w_down, tb=True, epi=lambda r, uu: r * (2.0 * jnp.maximum(uu, 0.0)), epi_args=(u,), out_dtype=BF16,
             name=f"mlp{layer}_du")
    d_down = _mm(u, dh, ta=True, a_pro=_relu2, name=f"mlp{layer}_ddown")
    if up_dev_major:
        cw = w_up.shape[2]
        d_up = _mm(hn, du, ta=True, name=f"mlp{layer}_dup", dims=(D_MODEL, D_FF, t), tn=cw,
                   o_spec=pl.BlockSpec((None, 1024, cw), lambda i, j, q: (j, i, 0)), o_shape=(N_DEV, D_MODEL, cw))
        dhn = _mm(du, w_up, tb=True, name=f"mlp{layer}_dhn", dims=(t, D_MODEL, D_FF), tk=cw,
                  b_spec=pl.BlockSpec((None, 1024, cw), lambda i, j, q: (q, j, 0)))
    else:
        d_up = _mm(hn, du, ta=True, name=f"mlp{layer}_dup")
        dhn = _mm(du, w_up, tb=True, name=f"mlp{layer}_dhn")
    return dhn, d_up, d_down


_LATE_WEIGHTS = ("mlp_w_up0", "mlp_w_up1", "mlp_w_down0", "mlp_w_down1", "b_w_q", "b_w_o")
_EARLY_GRADS = _LATE_WEIGHTS + ("a_w_out",)


def _local_step(x, target, w, late_shards=None, up_dev_major=True):
    t = x.shape[0]
    g = {}
    w = dict(w)
    row = lambda: _sds((t, D_MODEL))
    rowb = lambda: _sds((t, D_MODEL), BF16)
    vec = lambda: _sds((1, D_MODEL))

    xs = _shift_down(x)
    mix = [w["a_mix"][i:i + 1] for i in range(6)]
    xr, xk, xv, xw, xa, xg = _rowwise(_time_shift_lerps, [x, xs], [w["a_norm"]] + mix, [rowb()] * 6, tm=256, name="tmix_lerp")
    r = _mm(xr, w["a_w_r"], name="tmix_r")
    k = _mm(xk, w["a_w_k"], name="tmix_k")
    v = _mm(xv, w["a_w_v"], name="tmix_v")
    lw1 = _mm(xw, w["a_w1"], name="tmix_w1")
    wl = _mm(lw1, w["a_w2"], a_pro=jnp.tanh, name="tmix_w2")
    la1 = _mm(xa, w["a_a1"], name="tmix_a1")
    al = _mm(la1, w["a_a2"], name="tmix_a2")
    lg1 = _mm(xg, w["a_g1"], name="tmix_g1")
    gate = _mm(lg1, w["a_g2"], a_pro=_sigmoid, name="tmix_g2")

    hv = lambda name: w[name].reshape(N_HEADS, 1, HEAD_DIM)
    wkv_params = [hv(name) for name in ("a_w0", "a_a0", "a_k_k", "a_k_a", "a_ln_x_w", "a_ln_x_b", "a_r_k")]
    y2, states, gathered = _wkv_fwd(r, k, v, wl, al, wkv_params, late_shards or ())
    for name, arr in zip(_LATE_WEIGHTS, gathered):
        w[name] = arr if name.startswith("mlp_w_up") else arr.reshape(N_DEV * arr.shape[1], arr.shape[2])
    (yg,) = _rowwise(lambda a, b: (a * b,), [y2, gate], [], [rowb()], name="tmix_gate")
    att = _mm(yg, w["a_w_out"], name="tmix_out")

    h1, hn0 = _rowwise(_residual_norm, [x, att], [w["mlp_norm0"]], [row(), rowb()], name="res_norm0")
    u0, m0 = _mlp_fwd(hn0, w["mlp_w_up0"], w["mlp_w_down0"], 0, up_dev_major)

    h2, kvn, qn = _rowwise(_residual_norm2, [h1, m0], [w["kv_norm"], w["b_norm"]], [row(), rowb(), rowb()], name="res_norm_kvq")
    kv = _mm(kvn, w["w_kv"], name="kv_proj")
    q = _mm(qn, w["b_w_q"], name="q_proj")
    cos, sin = _rope_tables(t)
    sinks = jnp.broadcast_to(w["b_sinks"].reshape(N_HEADS, 1, 1), (N_HEADS, 1, 128))
    o = _attn_fwd(q, kv, cos, sin, w["b_q_norm"], w["k_norm"], sinks)
    att2 = _mm(o, w["b_w_o"], name="attn_out")

    h3, hn1 = _rowwise(_residual_norm, [h2, att2], [w["mlp_norm1"]], [row(), rowb()], name="res_norm1")
    u1, m1 = _mlp_fwd(hn1, w["mlp_w_up1"], w["mlp_w_down1"], 1, up_dev_major)

    def loss_fn(h, m, tg):
        diff = (h + m) - tg
        part = 0.5 * jnp.sum(jnp.mean(jnp.square(diff), axis=-1, keepdims=True), axis=0, keepdims=True)
        return diff * (1.0 / D_MODEL), jnp.broadcast_to(part, (1, 128))

    dh4, loss = _rowwise(loss_fn, [h3, m1, target], [], [row()], [_sds((1, 128))], name="loss")

    def res_norm_bwd(h, dnext, dhn, gain):
        dh, dgain = _vjp_of(lambda hh, gg: (_rms(hh, gg),), 2, (0, 1))(h, gain, dhn)
        return dnext + dh, dgain

    dhn1, g["mlp_w_up1"], g["mlp_w_down1"] = _mlp_bwd(hn1, u1, dh4, w["mlp_w_up1"], w["mlp_w_down1"], 1, up_dev_major)
    dh3, g["mlp_norm1"] = _rowwise(res_norm_bwd, [h3, dh4, dhn1], [w["mlp_norm1"]], [row()], [vec()], name="res_norm1_bwd")

    g["b_w_o"] = _mm(o, dh3, ta=True, name="attn_out_dw")
    do = _mm(dh3, w["b_w_o"], tb=True, name="attn_out_dx")
    dq, dkv_own, dkv_prev, g["b_q_norm"], g["k_norm"], dsinks = _attn_bwd(
        q, kv, cos, sin, w["b_q_norm"], w["k_norm"], sinks, do)
    g["b_sinks"] = dsinks[:, 0, 0].reshape(1, N_HEADS)
    g["b_w_q"] = _mm(qn, dq, ta=True, name="q_proj_dw")
    dqn = _mm(dq, w["b_w_q"], tb=True, name="q_proj_dx")
    dkv_prev = jnp.pad(dkv_prev[ATT_BLOCK:], ((0, ATT_BLOCK), (0, 0)))
    (dkv,) = _rowwise(lambda a, b: (a + b,), [dkv_own, dkv_prev], [], [_sds(kv.shape, BF16)], name="kv_grad_sum")
    g["w_kv"] = _mm(kvn, dkv, ta=True, name="kv_proj_dw")
    dkvn = _mm(dkv, w["w_kv"], tb=True, name="kv_proj_dx")

    def res_norm2_bwd(h, dnext, dna, dnb, gain_a, gain_b):
        dha, dga = _vjp_of(lambda hh, gg: (_rms(hh, gg),), 2, (0, 1))(h, gain_a, dna)
        dhb, dgb = _vjp_of(lambda hh, gg: (_rms(hh, gg),), 2, (0, 1))(h, gain_b, dnb)
        return dnext + dha + dhb, dga, dgb

    dh2, g["kv_norm"], g["b_norm"] = _rowwise(res_norm2_bwd, [h2, dh3, dkvn, dqn], [w["kv_norm"], w["b_norm"]],
                                              [row()], [vec(), vec()], name="res_norm_kvq_bwd")

    dhn0, g["mlp_w_up0"], g["mlp_w_down0"] = _mlp_bwd(hn0, u0, dh2, w["mlp_w_up0"], w["mlp_w_down0"], 0, up_dev_major)
    dh1, g["mlp_norm0"] = _rowwise(res_norm_bwd, [h1, dh2, dhn0], [w["mlp_norm0"]], [row()], [vec()], name="res_norm0_bwd")

    g["a_w_out"] = _mm(yg, dh1, ta=True, name="tmix_out_dw")
    dyg = _mm(dh1, w["a_w_out"], tb=True, name="tmix_out_dx")
    dy2, dgate = _rowwise(lambda d, a, b: (d * b, d * a), [dyg, y2, gate], [], [row(), rowb()], name="tmix_gate_bwd")
    g["a_g2"] = _mm(lg1, dgate, ta=True, a_pro=_sigmoid, name="tmix_g2_dw")

    def dsigmoid(rr, z):
        s = jax.nn.sigmoid(z)
        return rr * s * (1.0 - s)

    dlg1 = _mm(dgate, w["a_g2"], tb=True, epi=dsigmoid, epi_args=(lg1,), out_dtype=BF16, name="tmix_g2_dx")
    g["a_g1"] = _mm(xg, dlg1, ta=True, name="tmix_g1_dw")
    dxg = _mm(dlg1, w["a_g1"], tb=True, name="tmix_g1_dx")

    early_sums = ()
    if late_shards is not None:
        early_sums = _chip_sums([_device_major(name, g[name]) for name in _EARLY_GRADS], _EARLY_GRADS, "early")
    (dr, dk, dv, dwl, dal), param_grads, early_reduced = _wkv_bwd(r, k, v, wl, al, wkv_params, states, dy2, early_sums)
    for name, pg in zip(("a_w0", "a_a0", "a_k_k", "a_k_a", "a_ln_x_w", "a_ln_x_b", "a_r_k"), param_grads):
        g[name] = pg.reshape(1, D_MODEL)

    g["a_w_r"] = _mm(xr, dr, ta=True, name="tmix_r_dw")
    g["a_w_k"] = _mm(xk, dk, ta=True, name="tmix_k_dw")
    g["a_w_v"] = _mm(xv, dv, ta=True, name="tmix_v_dw")
    dxr = _mm(dr, w["a_w_r"], tb=True, name="tmix_r_dx")
    dxk = _mm(dk, w["a_w_k"], tb=True, name="tmix_k_dx")
    dxv = _mm(dv, w["a_w_v"], tb=True, name="tmix_v_dx")
    g["a_w2"] = _mm(lw1, dwl, ta=True, a_pro=jnp.tanh, name="tmix_w2_dw")

    def dtanh(rr, z):
        th = jnp.tanh(z)
        return rr * (1.0 - th * th)

    dlw1 = _mm(dwl, w["a_w2"], tb=True, epi=dtanh, epi_args=(lw1,), out_dtype=BF16, name="tmix_w2_dx")
    g["a_w1"] = _mm(xw, dlw1, ta=True, name="tmix_w1_dw")
    dxw = _mm(dlw1, w["a_w1"], tb=True, name="tmix_w1_dx")
    g["a_a2"] = _mm(la1, dal, ta=True, name="tmix_a2_dw")
    dla1 = _mm(dal, w["a_a2"], tb=True, out_dtype=BF16, name="tmix_a2_dx")
    g["a_a1"] = _mm(xa, dla1, ta=True, name="tmix_a1_dw")
    dxa = _mm(dla1, w["a_a1"], tb=True, name="tmix_a1_dx")

    lerp_bwd = _vjp_of(_time_shift_lerps, 9, tuple(range(9)))

    def lerp_bwd_rows(x_, xs_, d0, d1, d2, d3, d4, d5, gain, *mx):
        return lerp_bwd(x_, xs_, gain, *mx, d0, d1, d2, d3, d4, d5)

    outs = _rowwise(lerp_bwd_rows, [x, xs, dxr, dxk, dxv, dxw, dxa, dxg], [w["a_norm"]] + mix, [row(), row()], [vec()] * 7,
                    tm=128, name="tmix_lerp_bwd")
    dx_a, dxs, g["a_norm"] = outs[0], outs[1], outs[2]
    g["a_mix"] = jnp.concatenate(outs[3:9], axis=0)
    (grad_x,) = _rowwise(lambda a, b, c: (a + b + c,), [dh1, dx_a, _shift_up(dxs)], [], [row()], name="grad_x_sum")
    return loss, grad_x, g, (early_reduced if late_shards is not None else None)


def _device_major(name, grad):
    return grad if name.startswith("mlp_w_up") else grad.reshape((N_DEV, grad.shape[0] // N_DEV, grad.shape[1]))


_ANY = pl.BlockSpec(memory_space=pl.ANY)
_MESH_ID = pl.DeviceIdType.MESH
N_PEERS = N_DEV - 1


def _linear(pos):
    return 4 * pos[0] + 2 * pos[1] + pos[2]


def _all_gather(shards, name):
    n = len(shards)

    def body(*refs):
        start, finish = _gather_plan(refs[:n], refs[n:2 * n], *refs[2 * n:])
        start()
        finish()

    return pl.pallas_call(
        body, out_shape=[_sds((N_DEV,) + s.shape, s.dtype) for s in shards], in_specs=[_ANY] * n, out_specs=[_ANY] * n,
        scratch_shapes=_gather_semaphores(n), name=name,
    )(*shards)


def _gather_semaphores(n):
    return [pltpu.SemaphoreType.DMA((n * N_PEERS,)), pltpu.SemaphoreType.DMA((n * N_PEERS,)), pltpu.SemaphoreType.DMA((n,))]


def _gather_plan(ins, outs, send_sems, recv_sems, local_sems):
    n = len(ins)
    x, y, c = lax.axis_index("x"), lax.axis_index("y"), lax.axis_index("c")
    me, sibling = (x, y, c), (x, y, 1 - c)
    chips = [(1 - x, y), (x, 1 - y), (1 - x, 1 - y)]

    def copy(a, k, block, to, src=None):
        dst = outs[a].at[_linear(block)]
        return pltpu.make_async_remote_copy(
            src_ref=dst if src is None else src, dst_ref=dst, send_sem=send_sems.at[a * N_PEERS + k],
            recv_sem=recv_sems.at[a * N_PEERS + k], device_id=to, device_id_type=_MESH_ID)

    def own_copies():
        mine = [pltpu.make_async_copy(ins[a], outs[a].at[_linear(me)], local_sems.at[a]) for a in range(n)]
        first = []
        for a in range(n):
            first.append(copy(a, 0, me, sibling, src=ins[a]))
            first += [copy(a, 1 + j, me, (*chip, c), src=ins[a]) for j, chip in enumerate(chips)]
        return mine, first

    def start():
        mine, first = own_copies()
        for cp in mine + first:
            cp.start()

    def finish():
        mine, first = own_copies()
        passed = []
        for j, chip in enumerate(chips):
            for a in range(n):
                copy(a, 1 + j, (*chip, c), me).wait_recv()
                fwd = copy(a, 4 + j, (*chip, c), sibling)
                fwd.start()
                passed.append(fwd)
        for a in range(n):
            copy(a, 0, sibling, me).wait_recv()
            for j, chip in enumerate(chips):
                copy(a, 4 + j, (*chip, 1 - c), me).wait_recv()
        for cp in first + passed:
            cp.wait_send()
        for cp in mine:
            cp.wait()

    return start, finish


N_CHIPS = 4


def _exchange_with_sibling(parts, name):
    n = len(parts)

    def body(*refs):
        ins, outs = refs[:n], refs[n:2 * n]
        send_sems, recv_sems = refs[2 * n:]
        x, y, c = lax.axis_index("x"), lax.axis_index("y"), lax.axis_index("c")
        copies = []
        for a in range(n):
            for q in range(N_CHIPS):
                cp = pltpu.make_async_remote_copy(
                    src_ref=ins[a].at[2 * q + (1 - c)], dst_ref=outs[a].at[q], send_sem=send_sems.at[a * N_CHIPS + q],
                    recv_sem=recv_sems.at[a * N_CHIPS + q], device_id=(x, y, 1 - c), device_id_type=_MESH_ID)
                cp.start()
                copies.append(cp)
        for cp in copies:
            cp.wait()

    return pl.pallas_call(
        body, out_shape=[_sds((N_CHIPS,) + p.shape[1:], p.dtype) for p in parts], in_specs=[_ANY] * n, out_specs=[_ANY] * n,
        scratch_shapes=[pltpu.SemaphoreType.DMA((n * N_CHIPS,)), pltpu.SemaphoreType.DMA((n * N_CHIPS,))],
        name=name,
    )(*parts)


def _pair_sum(part, recv, core, out_dtype, name):
    _, r, cdim = recv.shape
    tr = max(8, min(r, (1 << 18) // cdim))
    assert r % tr == 0, (name, r, tr)

    def body(core_ref, p_ref, r_ref, o_ref):
        o_ref[...] = (p_ref[...] + r_ref[...]).astype(o_ref.dtype)

    grid_spec = pltpu.PrefetchScalarGridSpec(
        num_scalar_prefetch=1, grid=(N_CHIPS, r // tr),
        in_specs=[pl.BlockSpec((None, None, tr, cdim), lambda q, i, core_ref: (q, core_ref[0], i, 0)),
                  pl.BlockSpec((None, tr, cdim), lambda q, i, core_ref: (q, i, 0))],
        out_specs=pl.BlockSpec((None, tr, cdim), lambda q, i, core_ref: (q, i, 0)))
    return pl.pallas_call(
        body, grid_spec=grid_spec, out_shape=_sds((N_CHIPS, r, cdim), out_dtype),
        compiler_params=_params(("parallel", "parallel")), name=name,
    )(core, part.reshape(N_CHIPS, 2, r, cdim), recv)


def _exchange_between_chips(parts, name):
    n = len(parts)

    def body(*refs):
        start, finish = _chip_exchange_plan(refs[:n], refs[n:2 * n], *refs[2 * n:])
        start()
        finish()

    return pl.pallas_call(
        body, out_shape=[_sds(p.shape, p.dtype) for p in parts], in_specs=[_ANY] * n, out_specs=[_ANY] * n,
        scratch_shapes=_chip_exchange_semaphores(n), name=name,
    )(*parts)


def _chip_exchange_semaphores(n):
    n_other = N_CHIPS - 1
    return [pltpu.SemaphoreType.DMA((n * n_other,)), pltpu.SemaphoreType.DMA((n * n_other,)), pltpu.SemaphoreType.DMA((n,))]


def _chip_exchange_plan(ins, outs, send_sems, recv_sems, local_sems):
    n = len(ins)
    n_other = N_CHIPS - 1
    x, y, c = lax.axis_index("x"), lax.axis_index("y"), lax.axis_index("c")
    my_chip = 2 * x + y

    def all_copies():
        mine = [pltpu.make_async_copy(ins[a].at[my_chip], outs[a].at[my_chip], local_sems.at[a]) for a in range(n)]
        remote = []
        for j, (fx, fy) in enumerate([(1, 0), (0, 1), (1, 1)]):
            px, py = (1 - x if fx else x), (1 - y if fy else y)
            for a in range(n):
                remote.append(pltpu.make_async_remote_copy(
                    src_ref=ins[a].at[2 * px + py], dst_ref=outs[a].at[my_chip], send_sem=send_sems.at[a * n_other + j],
                    recv_sem=recv_sems.at[a * n_other + j], device_id=(px, py, c), device_id_type=_MESH_ID))
        return mine, remote

    def start():
        mine, remote = all_copies()
        for cp in mine + remote:
            cp.start()

    def finish():
        mine, remote = all_copies()
        for cp in remote + mine:
            cp.wait()

    return start, finish


def _chip_sums(parts, names, tag):
    core = lax.axis_index("c").astype(jnp.int32).reshape(1)
    from_sibling = _exchange_with_sibling(parts, name="scatter_grads_sibling_" + tag)
    return [_pair_sum(p, r, core, F32 if nm.startswith("pack") else BF16, name="pair_sum_" + nm)
            for p, r, nm in zip(parts, from_sibling, names)]


def _reduce_scatter(parts, names, tag):
    return _exchange_between_chips(_chip_sums(parts, names, tag), name="scatter_grads_chips_" + tag)


def _adamw(w, m, v, slots, name):
    r, c = w.shape
    ns = slots.shape[0]
    tr = max(8, min(r, (1 << 18) // c))
    assert r % tr == 0, (name, r, tr)

    def body(w_ref, m_ref, v_ref, g_ref, g_out, d_out, m_out, v_out):
        g = g_ref[0].astype(F32)
        for s in range(1, ns):
            g = g + g_ref[s].astype(F32)
        m_new = ADAM_B1 * m_ref[...] + (1.0 - ADAM_B1) * g
        v_new = ADAM_B2 * v_ref[...] + (1.0 - ADAM_B2) * jnp.square(g)
        m_hat = m_new / (1.0 - ADAM_B1 ** ADAM_STEP)
        v_hat = v_new / (1.0 - ADAM_B2 ** ADAM_STEP)
        d_out[...] = -ADAM_LR * (m_hat / (jnp.sqrt(v_hat) + ADAM_EPS) + ADAM_WD * w_ref[...])
        g_out[...], m_out[...], v_out[...] = g, m_new, v_new

    spec = pl.BlockSpec((tr, c), lambda i: (i, 0))
    return pl.pallas_call(
        body, grid=(r // tr,), in_specs=[spec, spec, spec, pl.BlockSpec((ns, tr, c), lambda i: (0, i, 0))],
        out_specs=[spec] * 4, out_shape=[_sds((r, c))] * 4, compiler_params=_params(("parallel",)), name=name,
    )(w, m, v, slots)


_COL_VECTORS = ("a_norm", "a_mix", "a_w0", "a_a0", "a_k_k", "a_k_a", "a_ln_x_w", "a_ln_x_b")
_COL_VEC_ROWS = 16
_COL_ROWS = _COL_VEC_ROWS + 2 * LORA_PAD + 256
_ROW_COLS = 2 * LORA_PAD + 256 + 512
_REPL_ROWS = 8


def _pad_to(a, size, axis):
    widths = [(0, 0)] * a.ndim
    widths[axis] = (0, size - a.shape[axis])
    return jnp.pad(a, widths)


def _pack_cols(p):
    width = p["a_norm"].shape[-1]
    vecs = jnp.concatenate([p[n].reshape(-1, width) for n in _COL_VECTORS], axis=0)
    return jnp.concatenate([_pad_to(vecs, _COL_VEC_ROWS, 0), _pad_to(p["a_w2"].reshape(-1, width), LORA_PAD, 0),
                            _pad_to(p["a_a2"].reshape(-1, width), LORA_PAD, 0), p["a_g2"].reshape(-1, width)], axis=0)


def _unpack_cols(a, lead):
    width = a.shape[-1]
    out, row = {}, 0
    for n in _COL_VECTORS:
        k = 6 if n == "a_mix" else 1
        out[n] = a[row:row + k].reshape(lead + ((6, width) if n == "a_mix" else (width,)))
        row += k
    base = _COL_VEC_ROWS
    out["a_w2"] = a[base:base + 96].reshape(lead + (96, width))
    out["a_a2"] = a[base + LORA_PAD:base + LORA_PAD + 96].reshape(lead + (96, width))
    out["a_g2"] = a[base + 2 * LORA_PAD:].reshape(lead + (256, width))
    return out


def _pack_rows(p):
    rows = p["w_kv"].shape[0]
    return jnp.concatenate([_pad_to(p["a_w1"].reshape(rows, -1), LORA_PAD, 1), _pad_to(p["a_a1"].reshape(rows, -1), LORA_PAD, 1),
                            p["a_g1"].reshape(rows, -1), p["w_kv"]], axis=1)


def _unpack_rows(a, lead):
    rows = a.shape[0]
    return {"a_w1": a[:, :96].reshape(lead + (rows, 96)), "a_a1": a[:, LORA_PAD:LORA_PAD + 96].reshape(lead + (rows, 96)),
            "a_g1": a[:, 2 * LORA_PAD:2 * LORA_PAD + 256].reshape(lead + (rows, 256)), "w_kv": a[:, 2 * LORA_PAD + 256:]}


def _pack_repl(p):
    row = lambda a: _pad_to(a.reshape(1, -1), D_MODEL, 1)
    return jnp.concatenate([p["mlp_norm"].reshape(2, D_MODEL), row(p["kv_norm"]), row(p["b_norm"]), row(p["a_r_k"]),
                            row(p["k_norm"]), row(p["b_q_norm"]), row(p["b_sinks"])], axis=0)


def _unpack_repl(a):
    return {"mlp_norm": a[0:2], "kv_norm": a[2], "b_norm": a[3:4], "a_r_k": a[4].reshape(1, N_HEADS, HEAD_DIM),
            "k_norm": a[5, :HEAD_DIM], "b_q_norm": a[6:7, :HEAD_DIM], "b_sinks": a[7:8, :N_HEADS]}


_WEIGHTS = ("a_norm", "a_mix", "a_w_rkv", "a_w0", "a_w1", "a_w2", "a_a0", "a_a1", "a_a2", "a_g1", "a_g2", "a_k_k", "a_k_a",
            "a_r_k", "a_ln_x_w", "a_ln_x_b", "a_w_out", "mlp_norm", "mlp_w_up", "mlp_w_down", "kv_norm", "w_kv", "k_norm",
            "b_norm", "b_w_q", "b_q_norm", "b_sinks", "b_w_o")


def _big_shards(p):
    return [p["a_w_rkv"][0, 0], p["a_w_rkv"][0, 1], p["a_w_rkv"][0, 2], p["a_w_out"][0], p["mlp_w_up"][0], p["mlp_w_up"][1],
            p["mlp_w_down"][0], p["mlp_w_down"][1], p["b_w_q"][0], p["b_w_o"][0]]


_BIG_NAMES = ("a_w_r", "a_w_k", "a_w_v", "a_w_out", "mlp_w_up0", "mlp_w_up1", "mlp_w_down0", "mlp_w_down1", "b_w_q", "b_w_o")


def kernel(x, a_norm, a_mix, a_w_rkv, a_w0, a_w1, a_w2, a_a0, a_a1, a_a2, a_g1, a_g2, a_k_k, a_k_a, a_r_k, a_ln_x_w,
           a_ln_x_b, a_w_out, mlp_norm, mlp_w_up, mlp_w_down, kv_norm, w_kv, k_norm, b_norm, b_w_q, b_q_norm, b_sinks,
           b_w_o, loss_target, m_a_norm, m_a_mix, m_a_w_rkv, m_a_w0, m_a_w1, m_a_w2, m_a_a0, m_a_a1, m_a_a2, m_a_g1,
           m_a_g2, m_a_k_k, m_a_k_a, m_a_r_k, m_a_ln_x_w, m_a_ln_x_b, m_a_w_out, m_mlp_norm, m_mlp_w_up, m_mlp_w_down,
           m_kv_norm, m_w_kv, m_k_norm, m_b_norm, m_b_w_q, m_b_q_norm, m_b_sinks, m_b_w_o, v_a_norm, v_a_mix, v_a_w_rkv,
           v_a_w0, v_a_w1, v_a_w2, v_a_a0, v_a_a1, v_a_a2, v_a_g1, v_a_g2, v_a_k_k, v_a_k_a, v_a_r_k, v_a_ln_x_w,
           v_a_ln_x_b, v_a_w_out, v_mlp_norm, v_mlp_w_up, v_mlp_w_down, v_kv_norm, v_w_kv, v_k_norm, v_b_norm, v_b_w_q,
           v_b_q_norm, v_b_sinks, v_b_w_o):
    given = locals()
    wts = {n: given[n] for n in _WEIGHTS}
    mom = {n: given["m_" + n] for n in _WEIGHTS}
    var = {n: given["v_" + n] for n in _WEIGHTS}

    cols_w, rows_w, repl_w = _pack_cols(wts), _pack_rows(wts), _pack_repl(wts)
    big_w = _big_shards(wts)
    big_bf16 = dict(zip(_BIG_NAMES, [b.astype(BF16) for b in big_w]))
    first_names = [k for k in _BIG_NAMES if k not in _LATE_WEIGHTS]
    gathered = _all_gather([cols_w, rows_w] + [big_bf16[k] for k in first_names], name="gather_weights")
    full_cols = gathered[0].transpose(1, 0, 2).reshape(_COL_ROWS, D_MODEL)
    full_rows = gathered[1].reshape(D_MODEL, _ROW_COLS)
    w = {}
    w.update({k: v.reshape(v.shape[1:]) for k, v in _unpack_cols(full_cols, (1,)).items()})
    for k in ("a_norm", "a_w0", "a_a0", "a_k_k", "a_k_a", "a_ln_x_w", "a_ln_x_b"):
        w[k] = w[k].reshape(1, D_MODEL)
    for k in ("a_w2", "a_a2"):
        w[k] = _pad_to(w[k], LORA_PAD, 0)
    rows_full = _unpack_rows(full_rows, ())
    w["a_w1"], w["a_a1"] = _pad_to(rows_full["a_w1"], LORA_PAD, 1), _pad_to(rows_full["a_a1"], LORA_PAD, 1)
    w["a_g1"], w["w_kv"] = rows_full["a_g1"], rows_full["w_kv"]
    for k, arr in zip(first_names, gathered[2:]):
        w[k] = arr.reshape(N_DEV * arr.shape[1], arr.shape[2])
    w["mlp_norm0"], w["mlp_norm1"] = mlp_norm[0:1], mlp_norm[1:2]
    w["kv_norm"], w["k_norm"] = kv_norm.reshape(1, D_MODEL), k_norm.reshape(1, HEAD_DIM)
    w["b_norm"], w["b_q_norm"], w["b_sinks"], w["a_r_k"] = b_norm, b_q_norm, b_sinks, a_r_k.reshape(1, D_MODEL)

    loss_local, grad_x, g, early_reduced = _local_step(x[0], loss_target[0], w, [big_bf16[k] for k in _LATE_WEIGHTS])
    loss = lax.psum(loss_local[0, 0], MESH_AXES)

    g_lead = {k: g[k][None] for k in ("a_norm", "a_mix", "a_w0", "a_a0", "a_k_k", "a_k_a", "a_ln_x_w", "a_ln_x_b", "a_g2")}
    g_lead["a_w2"], g_lead["a_a2"] = g["a_w2"][None, :96], g["a_a2"][None, :96]
    g_cols = _pack_cols(g_lead).reshape(_COL_ROWS, N_DEV, D_MODEL // N_DEV).transpose(1, 0, 2)
    g_rows = _pack_rows({"a_w1": g["a_w1"][:, :96], "a_a1": g["a_a1"][:, :96], "a_g1": g["a_g1"], "w_kv": g["w_kv"]})
    g_rows = g_rows.reshape(N_DEV, D_MODEL // N_DEV, _ROW_COLS)
    late_names = tuple(k for k in _BIG_NAMES if k not in _EARLY_GRADS)
    late_reduced = _reduce_scatter([g_cols, g_rows] + [_device_major(k, g[k]) for k in late_names],
                                   ("pack_cols", "pack_rows") + late_names, "late")
    big_reduced = dict(zip(late_names, late_reduced[2:]))
    big_reduced.update(zip(_EARLY_GRADS, early_reduced))
    reduced = list(late_reduced[:2]) + [big_reduced[k] for k in _BIG_NAMES]
    g_repl = _pack_repl({"mlp_norm": jnp.concatenate([g["mlp_norm0"], g["mlp_norm1"]], axis=0), "kv_norm": g["kv_norm"],
                         "b_norm": g["b_norm"], "a_r_k": g["a_r_k"], "k_norm": g["k_norm"], "b_q_norm": g["b_q_norm"],
                         "b_sinks": g["b_sinks"]})
    (repl_slots,) = _all_gather([g_repl], name="gather_replicated_grads")

    res = {}
    cols4 = _adamw(cols_w, _pack_cols(mom), _pack_cols(var), reduced[0], name="adamw_cols")
    rows4 = _adamw(rows_w, _pack_rows(mom), _pack_rows(var), reduced[1], name="adamw_rows")
    repl4 = _adamw(repl_w, _pack_repl(mom), _pack_repl(var), repl_slots, name="adamw_replicated")
    for unpacked in ([_unpack_cols(a, (1,)) for a in cols4], [_unpack_rows(a, (1,)) for a in rows4], [_unpack_repl(a) for a in repl4]):
        for k in unpacked[0]:
            res[k] = tuple(u[k] for u in unpacked)
    big4 = [_adamw(bw, bm, bv, slots, name="adamw_" + k)
            for k, bw, bm, bv, slots in zip(_BIG_NAMES, big_w, _big_shards(mom), _big_shards(var), reduced[2:])]
    res["a_w_rkv"] = tuple(jnp.stack([big4[0][i], big4[1][i], big4[2][i]])[None] for i in range(4))
    res["a_w_out"] = tuple(a[None] for a in big4[3])
    res["mlp_w_up"] = tuple(jnp.stack([big4[4][i], big4[5][i]]) for i in range(4))
    res["mlp_w_down"] = tuple(jnp.stack([big4[6][i], big4[7][i]]) for i in range(4))
    res["b_w_q"] = tuple(a[None] for a in big4[8])
    res["b_w_o"] = tuple(a[None] for a in big4[9])
    res["w_kv"] = tuple(a.reshape(w_kv.shape) for a in res["w_kv"])

    outs = [loss, grad_x[None]]
    for i in range(4):
        outs += [res[n][i].reshape(given[n].shape) for n in _WEIGHTS]
    return tuple(outs)
```

```python
import functools
import math

import jax
import jax.numpy as jnp
from jax import lax
from jax.experimental import pallas as pl
from jax.experimental.pallas import tpu as pltpu

F32 = jnp.float32
BF16 = jnp.bfloat16

D_MODEL = 2048
N_HEADS = 32
HEAD_DIM = 64
N_KV_HEADS = 4
Q_PER_KV = 8
ATT_BLOCK = 128
WKV_CHUNK = 64
LORA_PAD = 128
D_FF = 8192
N_DEV = 8
RMS_EPS = 1e-6
GN_EPS = 64e-5
L2_EPS = 1e-12
ROPE_THETA = 10000.0
ADAM_LR, ADAM_B1, ADAM_B2, ADAM_EPS, ADAM_WD, ADAM_STEP = 0.001, 0.9, 0.999, 1e-08, 0.01, 10
MASK_VALUE = -1e30
VMEM_LIMIT_BYTES = 56 * 1024 * 1024
MESH_AXES = ("x", "y", "c")
HI = lax.Precision.HIGHEST

_NN = (((1,), (0,)), ((), ()))
_NT = (((1,), (1,)), ((), ()))
_TN = (((0,), (0,)), ((), ()))
_BNN = (((2,), (1,)), ((0,), (0,)))
_BNT = (((2,), (2,)), ((0,), (0,)))
_BTN = (((1,), (1,)), ((0,), (0,)))


def _params(sem):
    return pltpu.CompilerParams(dimension_semantics=sem, vmem_limit_bytes=VMEM_LIMIT_BYTES)


def _split2(a):
    hi = a.astype(BF16)
    return hi, (a - hi.astype(F32)).astype(BF16)


def _dot3(a, b, dims):
    ah, al = _split2(a)
    bh, bl = _split2(b)
    d = lambda p, q: lax.dot_general(p, q, dims, preferred_element_type=F32)
    return d(ah, bh) + (d(al, bh) + d(ah, bl))


@functools.partial(jax.custom_vjp, nondiff_argnums=(2,))
def _hdot(a, b, dims=_NN):
    return _dot3(a, b, dims)


def _hdot_fwd(a, b, dims):
    return _dot3(a, b, dims), (a, b)


def _hdot_bwd(dims, res, g):
    a, b = res
    nn, nt, tn = (_NN, _NT, _TN) if dims in (_NN, _NT, _TN) else (_BNN, _BNT, _BTN)
    if dims == nn:
        return _dot3(g, b, nt), _dot3(a, g, tn)
    if dims == nt:
        return _dot3(g, b, nn), _dot3(g, a, tn)
    assert dims == tn
    return _dot3(b, g, nt), _dot3(a, g, nn)


_hdot.defvjp(_hdot_fwd, _hdot_bwd)


def _tri_parts(x):
    hi = x.astype(BF16)
    r1 = x - hi.astype(F32)
    mid = r1.astype(BF16)
    return hi, mid, (r1 - mid.astype(F32)).astype(BF16)


@jax.custom_vjp
def _mask_dot(mask, x):
    mb = mask.astype(BF16)
    p0, p1, p2 = _tri_parts(x)
    d = lambda p: lax.dot_general(mb, p, _BNN, preferred_element_type=F32)
    return d(p0) + (d(p1) + d(p2))


def _mask_dot_fwd(mask, x):
    return _mask_dot(mask, x), mask


def _mask_dot_bwd(mask, g):
    mb = mask.astype(BF16)
    p0, p1, p2 = _tri_parts(g)
    d = lambda p: lax.dot_general(mb, p, _BTN, preferred_element_type=F32)
    return jnp.zeros_like(mask), d(p0) + (d(p1) + d(p2))


_mask_dot.defvjp(_mask_dot_fwd, _mask_dot_bwd)


def _b16dot(a, b, dims):
    return lax.dot_general(a.astype(BF16), b.astype(BF16), dims, preferred_element_type=F32)


@jax.custom_vjp
def _bdot(a, b):
    return _b16dot(a, b, _NN)


def _bdot_fwd(a, b):
    return _b16dot(a, b, _NN), (a, b)


def _bdot_bwd(res, g):
    a, b = res
    return _b16dot(g, b, _NT), _b16dot(a, g, _TN)


_bdot.defvjp(_bdot_fwd, _bdot_bwd)


@jax.custom_vjp
def _bdot_nt(a, b):
    return _b16dot(a, b, _NT)


def _bdot_nt_fwd(a, b):
    return _b16dot(a, b, _NT), (a, b)


def _bdot_nt_bwd(res, g):
    a, b = res
    return _b16dot(g, b, _NN), _b16dot(g, a, _TN)


_bdot_nt.defvjp(_bdot_nt_fwd, _bdot_nt_bwd)


def _rms(x, gain):
    return x * lax.rsqrt(jnp.mean(x * x, axis=-1, keepdims=True) + RMS_EPS) * gain


def _vjp_of(f, n_in, diff):
    def g(*args):
        ins, cts = args[:n_in], args[n_in:]

        def fd(*d):
            full = list(ins)
            for pos, i in enumerate(diff):
                full[i] = d[pos]
            return f(*full)

        _, pull = jax.vjp(fd, *[ins[i] for i in diff])
        return pull(tuple(cts))
    return g


def _mm(a, b, *, name, ta=False, tb=False, a_pro=None, epi=None, epi_args=(), out_dtype=F32,
        tm=1024, tn=1024, tk=2048, dims=None, b_spec=None, o_spec=None, o_shape=None):
    if dims is None:
        m, k = (a.shape[1], a.shape[0]) if ta else a.shape
        n = b.shape[0] if tb else b.shape[1]
    else:
        m, n, k = dims
    tm, tn, tk = min(tm, m), min(tn, n), min(tk, k)
    assert m % tm == 0 and n % tn == 0 and k % tk == 0, (name, m, n, k, tm, tn, tk)
    nk = k // tk
    ne = len(epi_args)
    cdims = (((0 if ta else 1,), (1 if tb else 0,)), ((), ()))

    def body(a_ref, b_ref, *rest):
        e_refs, o_ref, acc = rest[:ne], rest[ne], rest[ne + 1]
        kk = pl.program_id(2)

        @pl.when(kk == 0)
        def _():
            acc[...] = jnp.zeros_like(acc)

        av = a_ref[...]
        if a_pro is not None:
            av = a_pro(av.astype(F32))
        acc[...] += lax.dot_general(av.astype(BF16), b_ref[...].astype(BF16), cdims, preferred_element_type=F32)

        @pl.when(kk == nk - 1)
        def _():
            r = acc[...]
            if epi is not None:
                r = epi(r, *[e[...] for e in e_refs])
            o_ref[...] = r.astype(o_ref.dtype)

    a_spec = pl.BlockSpec((tk, tm), lambda i, j, q: (q, i)) if ta else pl.BlockSpec((tm, tk), lambda i, j, q: (i, q))
    if b_spec is None:
        b_spec = pl.BlockSpec((tn, tk), lambda i, j, q: (j, q)) if tb else pl.BlockSpec((tk, tn), lambda i, j, q: (q, j))
    if o_spec is None:
        o_spec = pl.BlockSpec((tm, tn), lambda i, j, q: (i, j))
        o_shape = (m, n)
    e_specs = [pl.BlockSpec((tm, tn), lambda i, j, q: (i, j)) for _ in epi_args]
    return pl.pallas_call(
        body, grid=(m // tm, n // tn, nk), in_specs=[a_spec, b_spec] + e_specs, out_specs=o_spec,
        out_shape=jax.ShapeDtypeStruct(o_shape, out_dtype), scratch_shapes=[pltpu.VMEM((tm, tn), F32)],
        compiler_params=_params(("parallel", "parallel", "arbitrary")), name=name,
    )(a, b, *epi_args)


def _rowwise(fn, rows, params, out_rows, out_params=(), *, tm=256, name):
    t = rows[0].shape[0]
    tm = min(tm, t)
    assert t % tm == 0
    nr, npar, nor, nop = len(rows), len(params), len(out_rows), len(out_params)

    def body(*refs):
        r, p = refs[:nr], refs[nr:nr + npar]
        o, op = refs[nr + npar:nr + npar + nor], refs[nr + npar + nor:]
        outs = fn(*[x[...] for x in r], *[x[...] for x in p])
        for ref, val in zip(o, outs[:nor]):
            ref[...] = val.astype(ref.dtype)
        if nop:
            @pl.when(pl.program_id(0) == 0)
            def _():
                for ref in op:
                    ref[...] = jnp.zeros_like(ref)

            for ref, val in zip(op, outs[nor:]):
                ref[...] += val.astype(F32)

    in_specs = [pl.BlockSpec((tm, x.shape[1]), lambda i: (i, 0)) for x in rows]
    in_specs += [pl.BlockSpec(p.shape, lambda i: (0, 0)) for p in params]
    out_specs = [pl.BlockSpec((tm, s.shape[1]), lambda i: (i, 0)) for s in out_rows]
    out_specs += [pl.BlockSpec(s.shape, lambda i: (0, 0)) for s in out_params]
    return pl.pallas_call(
        body, grid=(t // tm,), in_specs=in_specs, out_specs=out_specs, out_shape=list(out_rows) + list(out_params),
        compiler_params=_params(("arbitrary",)), name=name,
    )(*rows, *params)


def _sds(shape, dtype=F32):
    return jax.ShapeDtypeStruct(tuple(shape), dtype)


def _doubling_powers(l):
    powers = [l]
    for _ in range(int(math.log2(l.shape[-1])) - 1):
        powers.append(_dot3(powers[-1], powers[-1], _BNN))
    return powers


@jax.custom_vjp
def _unit_lower_solve(l, z):
    u = z
    for p in _doubling_powers(l):
        u = u + _dot3(p, u, _BNN)
    return u


def _unit_lower_solve_fwd(l, z):
    powers = _doubling_powers(l)
    u = z
    for p in powers:
        u = u + _dot3(p, u, _BNN)
    return u, (powers, u)


def _unit_lower_solve_bwd(res, du):
    powers, u = res
    g = du
    for p in powers:
        g = g + _dot3(p, g, _BTN)
    return _dot3(g, u, _BNT), g


_unit_lower_solve.defvjp(_unit_lower_solve_fwd, _unit_lower_solve_bwd)


def _wkv_chunk(s0, r, lw, k, v, a, b):
    nb, c, _ = r.shape
    ti = lax.broadcasted_iota(jnp.int32, (nb, c, c), 1)
    si = lax.broadcasted_iota(jnp.int32, (nb, c, c), 2)
    incl, strict = si <= ti, si < ti
    cum = _mask_dot(incl.astype(F32), lw)
    rcum = _mask_dot((si > ti).astype(F32), lw)
    tot = jnp.sum(lw, axis=1, keepdims=True)
    w_inv = jnp.exp(-cum)
    at, rt, bt, kt = a * jnp.exp(cum - lw), r * jnp.exp(cum), b * w_inv, k * w_inv
    l_ab = jnp.where(strict, _hdot(at, bt, _BNT), 0.0)
    l_ak = jnp.where(strict, _hdot(at, kt, _BNT), 0.0)
    t_rb = jnp.where(incl, _hdot(rt, bt, _BNT), 0.0)
    t_rk = jnp.where(incl, _hdot(rt, kt, _BNT), 0.0)
    u = _unit_lower_solve(l_ab, _hdot(at, s0, _BNT) + _hdot(l_ak, v, _BNN))
    y = _hdot(rt, s0, _BNT) + _hdot(t_rb, u, _BNN) + _hdot(t_rk, v, _BNN)
    e = jnp.exp(rcum)
    s1 = s0 * jnp.exp(tot) + _hdot(u, b * e, _BTN) + _hdot(v, k * e, _BTN)
    return y, s1


WKV_HEADS_PER_STEP = 16


def _first_and_last_step(grid):
    i, j = pl.program_id(0), pl.program_id(1)
    return jnp.logical_and(i == 0, j == 0), jnp.logical_and(i == grid[0] - 1, j == grid[1] - 1)


N_WKV_PARAMS = 7


def _tmix_chunk(s0, r, k, v, wl, al, w0, a0, k_k, k_a, ln_w, ln_b, r_k):
    lw, kmod, a, b = _wkv_prep(k, wl, al, w0, a0, k_k, k_a)
    y, s1 = _wkv_chunk(s0, r, lw, kmod, v, a, b)
    (y2,) = _wkv_post(y, r, kmod, v, ln_w, ln_b, r_k)
    return y2, s1


def _split_heads(x, nh):
    return x.reshape(x.shape[0], nh, HEAD_DIM).transpose(1, 0, 2)


def _merge_heads(xh):
    return jnp.concatenate([xh[h] for h in range(xh.shape[0])], axis=1)


def _wkv_fwd(r, k, v, wl, al, params, shards=()):
    t = r.shape[0]
    nh, n = N_HEADS, HEAD_DIM
    nc = t // WKV_CHUNK
    hb = WKV_HEADS_PER_STEP
    grid = (nh // hb, nc)
    ns = len(shards)
    n_in = 5 + N_WKV_PARAMS

    def body(*refs):
        y_ref, s_ref = refs[n_in + ns:n_in + ns + 2]
        state = refs[n_in + 2 * ns + 2]
        if ns:
            start, relay, finish = _gather_plan(refs[n_in:n_in + ns], refs[n_in + ns + 2:n_in + 2 * ns + 2], *refs[n_in + 2 * ns + 3:])
            first, last = _first_and_last_step(grid)
            pl.when(first)(start)
            pl.when(jnp.logical_and(pl.program_id(0) == 0, pl.program_id(1) == (3 * nc) // 4))(relay)

        @pl.when(pl.program_id(1) == 0)
        def _():
            state[...] = jnp.zeros_like(state)

        s0 = state[...]
        s_ref[:, 0] = s0
        rows = [_split_heads(ref[...], hb) for ref in refs[:5]]
        y, s1 = _tmix_chunk(s0, *rows, *[ref[...] for ref in refs[5:n_in]])
        y_ref[...] = _merge_heads(y)
        state[...] = s1
        if ns:
            pl.when(last)(finish)

    blk = pl.BlockSpec((WKV_CHUNK, hb * n), lambda h, c: (c, h))
    pblk = pl.BlockSpec((hb, 1, n), lambda h, c: (h, 0, 0))
    sblk = pl.BlockSpec((hb, 1, n, n), lambda h, c: (h, c, 0, 0))
    outs = pl.pallas_call(
        body, grid=grid, in_specs=[blk] * 5 + [pblk] * N_WKV_PARAMS + [_ANY] * ns, out_specs=[blk, sblk] + [_ANY] * ns,
        out_shape=[_sds((t, nh * n)), _sds((nh, nc, n, n))] + [_sds((N_DEV,) + s.shape, s.dtype) for s in shards],
        scratch_shapes=[pltpu.VMEM((hb, n, n), F32)] + (_gather_semaphores(ns) if ns else []),
        compiler_params=_params(("arbitrary", "arbitrary")), name="wkv_fwd",
    )(r, k, v, wl, al, *params, *shards)
    return outs[0], outs[1], list(outs[2:])


def _wkv_bwd(r, k, v, wl, al, params, states, dy, chip_sums=()):
    t = r.shape[0]
    nh, n = N_HEADS, HEAD_DIM
    nc = t // WKV_CHUNK
    hb = WKV_HEADS_PER_STEP
    grid = (nh // hb, nc)
    ns = len(chip_sums)
    n_in = 5 + N_WKV_PARAMS
    n_out = 5 + N_WKV_PARAMS

    def body(*refs):
        s_ref, dy_ref = refs[n_in:n_in + 2]
        out_refs = refs[n_in + 2 + ns:n_in + 2 + ns + n_out]
        dstate = refs[n_in + 2 + 2 * ns + n_out]
        if ns:
            start, finish = _chip_exchange_plan(refs[n_in + 2:n_in + 2 + ns], refs[n_in + 2 + ns + n_out:n_in + 2 + 2 * ns + n_out],
                                                *refs[n_in + 3 + 2 * ns + n_out:])
            first, last = _first_and_last_step(grid)
            pl.when(first)(start)

        @pl.when(pl.program_id(1) == 0)
        def _():
            dstate[...] = jnp.zeros_like(dstate)
            for ref in out_refs[5:]:
                ref[...] = jnp.zeros_like(ref)

        rows = [_split_heads(ref[...], hb) for ref in refs[:5]]
        _, pull = jax.vjp(_tmix_chunk, s_ref[:, 0], *rows, *[ref[...] for ref in refs[5:n_in]])
        grads = pull((_split_heads(dy_ref[...], hb), dstate[...]))
        dstate[...] = grads[0]
        for ref, val in zip(out_refs[:5], grads[1:6]):
            ref[...] = _merge_heads(val).astype(ref.dtype)
        for ref, val in zip(out_refs[5:], grads[6:]):
            ref[...] += val
        if ns:
            pl.when(last)(finish)

    blk = pl.BlockSpec((WKV_CHUNK, hb * n), lambda h, c: (nc - 1 - c, h))
    pblk = pl.BlockSpec((hb, 1, n), lambda h, c: (h, 0, 0))
    sblk = pl.BlockSpec((hb, 1, n, n), lambda h, c: (h, nc - 1 - c, 0, 0))
    outs = pl.pallas_call(
        body, grid=grid, in_specs=[blk] * 5 + [pblk] * N_WKV_PARAMS + [sblk, blk] + [_ANY] * ns,
        out_specs=[blk] * 5 + [pblk] * N_WKV_PARAMS + [_ANY] * ns,
        out_shape=[_sds((t, nh * n), BF16)] * 5 + [_sds((nh, 1, n))] * N_WKV_PARAMS + [_sds(p.shape, p.dtype) for p in chip_sums],
        scratch_shapes=[pltpu.VMEM((hb, n, n), F32)] + (_chip_exchange_semaphores(ns) if ns else []),
        compiler_params=_params(("arbitrary", "arbitrary")), name="wkv_bwd",
    )(r, k, v, wl, al, *params, states, dy, *chip_sums)
    return outs[:5], outs[5:n_out], list(outs[n_out:])


def _wkv_prep(k, wl, al, w0, a0, k_k, k_a):
    z = -(w0 + wl)
    softplus = jnp.maximum(z, 0.0) + jnp.log1p(jnp.exp(-jnp.abs(z)))
    lw = -jnp.exp(-softplus - 0.5)
    asig = jax.nn.sigmoid(a0 + al)
    kk = k * k_k
    kk = kk / jnp.maximum(jnp.sqrt(jnp.sum(kk * kk, axis=-1, keepdims=True)), L2_EPS)
    kmod = k * (1.0 + (asig - 1.0) * k_a)
    return lw, kmod, -kk, kk * asig


def _wkv_post(y, r, kmod, v, ln_w, ln_b, r_k):
    mu = jnp.mean(y, axis=-1, keepdims=True)
    var = jnp.mean(jnp.square(y - mu), axis=-1, keepdims=True)
    yn = (y - mu) * lax.rsqrt(var + GN_EPS)
    yn = yn * ln_w + ln_b
    return (yn + jnp.sum(r * kmod * r_k, axis=-1, keepdims=True) * v,)


def _attn_group(nonzero_block, q, kc, kp, vc, vp, cos_c, sin_c, cos_p, sin_p, q_gain, k_gain, sinks):
    ri = lax.broadcasted_iota(jnp.int32, (HEAD_DIM, HEAD_DIM), 0)
    ci = lax.broadcasted_iota(jnp.int32, (HEAD_DIM, HEAD_DIM), 1)
    half = HEAD_DIM // 2
    rot = jnp.where(ri == ci + half, -1.0, 0.0) + jnp.where(ri + half == ci, 1.0, 0.0)
    rows = Q_PER_KV * ATT_BLOCK

    def rope(x, cos, sin):
        return x * cos + _hdot(x, rot, _NN) * sin

    kcr = rope(_rms(kc, k_gain), cos_c, sin_c)
    kpr = rope(_rms(kp, k_gain), cos_p, sin_p)
    qn = _rms(q, q_gain)
    qr = qn * cos_c + _hdot(qn.reshape(rows, HEAD_DIM), rot, _NN).reshape(q.shape) * sin_c
    q2 = qr.reshape(rows, HEAD_DIM)
    qi = lax.broadcasted_iota(jnp.int32, (1, ATT_BLOCK, ATT_BLOCK), 1)
    ki = lax.broadcasted_iota(jnp.int32, (1, ATT_BLOCK, ATT_BLOCK), 2)
    mask_c = ki <= qi
    mask_p = jnp.logical_and(ki > qi, nonzero_block)
    lane0 = (lax.broadcasted_iota(jnp.int32, (1, 1, 128), 2) == 0).astype(F32)
    shape3 = (Q_PER_KV, ATT_BLOCK, ATT_BLOCK)
    sc = jnp.where(mask_c, (_bdot_nt(q2, kcr) * (HEAD_DIM ** -0.5)).reshape(shape3), MASK_VALUE)
    sp = jnp.where(mask_p, (_bdot_nt(q2, kpr) * (HEAD_DIM ** -0.5)).reshape(shape3), MASK_VALUE)
    sk = jnp.sum(sinks * lane0, axis=2, keepdims=True)
    mx = jnp.maximum(jnp.maximum(jnp.max(sc, axis=2, keepdims=True), jnp.max(sp, axis=2, keepdims=True)), sk)
    mx = lax.stop_gradient(mx)
    ec, ep = jnp.exp(sc - mx), jnp.exp(sp - mx)
    den = jnp.sum(ec, axis=2, keepdims=True) + jnp.sum(ep, axis=2, keepdims=True) + jnp.exp(sk - mx)
    out = _bdot((ec / den).reshape(rows, ATT_BLOCK), vc) + _bdot((ep / den).reshape(rows, ATT_BLOCK), vp)
    return out.reshape(q.shape)


def _attn_rows(nonzero_block, qs, kcs, kps, vcs, vps, tabs, q_gain, k_gain, sinks):
    return [_attn_group(nonzero_block, qs[g], kcs[g], kps[g], vcs[g], vps[g], *tabs, q_gain, k_gain, sinks[g])
            for g in range(N_KV_HEADS)]


def _attn_operands(q_tile, kvc_tile, kvp_tile, sink_ref):
    q3 = _split_heads(q_tile, N_HEADS)
    kvc, kvp = _split_heads(kvc_tile, 2 * N_KV_HEADS), _split_heads(kvp_tile, 2 * N_KV_HEADS)
    groups = range(N_KV_HEADS)
    qs = [q3[Q_PER_KV * g:Q_PER_KV * (g + 1)] for g in groups]
    sinks = [sink_ref[Q_PER_KV * g:Q_PER_KV * (g + 1)] for g in groups]
    return (qs, [kvc[g] for g in groups], [kvp[g] for g in groups], [kvc[N_KV_HEADS + g] for g in groups],
            [kvp[N_KV_HEADS + g] for g in groups], sinks)


def _attn_specs(t):
    nb = t // ATT_BLOCK
    prev = lambda n: jnp.maximum(n - 1, 0)
    kv_width = 2 * N_KV_HEADS * HEAD_DIM
    q_spec = pl.BlockSpec((ATT_BLOCK, D_MODEL), lambda n: (n, 0))
    kv_c = pl.BlockSpec((ATT_BLOCK, kv_width), lambda n: (n, 0))
    kv_p = pl.BlockSpec((ATT_BLOCK, kv_width), lambda n: (prev(n), 0))
    tab_c = pl.BlockSpec((ATT_BLOCK, HEAD_DIM), lambda n: (n, 0))
    tab_p = pl.BlockSpec((ATT_BLOCK, HEAD_DIM), lambda n: (prev(n), 0))
    gain = pl.BlockSpec((1, HEAD_DIM), lambda n: (0, 0))
    sink = pl.BlockSpec((N_HEADS, 1, 128), lambda n: (0, 0, 0))
    return nb, q_spec, kv_c, kv_p, tab_c, tab_p, gain, sink


def _attn_fwd(q, kv, cos, sin, q_gain, k_gain, sinks):
    t = q.shape[0]
    nb, q_spec, kv_c, kv_p, tab_c, tab_p, gain, sink = _attn_specs(t)

    def body(q_ref, kvc, kvp, cc, sc, cp, sp, qg, kg, sk, o_ref):
        qs, kcs, kps, vcs, vps, sinks_ = _attn_operands(q_ref[...], kvc[...], kvp[...], sk)
        outs = _attn_rows(pl.program_id(0) > 0, qs, kcs, kps, vcs, vps, (cc[...], sc[...], cp[...], sp[...]),
                          qg[...], kg[...], sinks_)
        o_ref[...] = _merge_heads(jnp.concatenate(outs, axis=0)).astype(o_ref.dtype)

    return pl.pallas_call(
        body, grid=(nb,), in_specs=[q_spec, kv_c, kv_p, tab_c, tab_c, tab_p, tab_p, gain, gain, sink],
        out_specs=q_spec, out_shape=_sds(q.shape, BF16), compiler_params=_params(("arbitrary",)), name="attn_fwd",
    )(q, kv, kv, cos, sin, cos, sin, q_gain, k_gain, sinks)


def _attn_bwd(q, kv, cos, sin, q_gain, k_gain, sinks, do, parts=()):
    t = q.shape[0]
    nb, q_spec, kv_c, kv_p, tab_c, tab_p, gain, sink = _attn_specs(t)
    ns = len(parts)

    def body(*refs):
        q_ref, kvc, kvp, cc, sc, cp, sp, qg, kg, sk, do_ref = refs[:11]
        dq_ref, dkvc_ref, dkvp_ref, dqg_ref, dkg_ref, dsk_ref = refs[11 + ns:17 + ns]
        if ns:
            start, finish = _sibling_exchange_plan(refs[11:11 + ns], refs[17 + ns:17 + 2 * ns], *refs[17 + 2 * ns:])
            pl.when(pl.program_id(0) == 0)(start)
        nonzero = pl.program_id(0) > 0
        tabs = (cc[...], sc[...], cp[...], sp[...])
        qs, kcs, kps, vcs, vps, sinks_ = _attn_operands(q_ref[...], kvc[...], kvp[...], sk)

        def f(qs_, kcs_, kps_, vcs_, vps_, qgv, kgv, sks):
            return _attn_rows(nonzero, qs_, kcs_, kps_, vcs_, vps_, tabs, qgv, kgv, sks)

        _, pull = jax.vjp(f, qs, kcs, kps, vcs, vps, qg[...], kg[...], sinks_)
        do3 = _split_heads(do_ref[...], N_HEADS)
        dqs, dkcs, dkps, dvcs, dvps, dqg, dkg, dsks = pull([do3[Q_PER_KV * g:Q_PER_KV * (g + 1)] for g in range(N_KV_HEADS)])
        dq_ref[...] = _merge_heads(jnp.concatenate(dqs, axis=0)).astype(dq_ref.dtype)
        dkvc_ref[...] = jnp.concatenate(dkcs + dvcs, axis=1)
        dkvp_ref[...] = jnp.concatenate(dkps + dvps, axis=1)

        @pl.when(pl.program_id(0) == 0)
        def _():
            dqg_ref[...] = jnp.zeros_like(dqg_ref)
            dkg_ref[...] = jnp.zeros_like(dkg_ref)
            dsk_ref[...] = jnp.zeros_like(dsk_ref)

        dqg_ref[...] += dqg
        dkg_ref[...] += dkg
        for g in range(N_KV_HEADS):
            dsk_ref[Q_PER_KV * g:Q_PER_KV * (g + 1)] += dsks[g]
        if ns:
            pl.when(pl.program_id(0) == nb - 1)(finish)

    outs = pl.pallas_call(
        body, grid=(nb,), in_specs=[q_spec, kv_c, kv_p, tab_c, tab_c, tab_p, tab_p, gain, gain, sink, q_spec] + [_ANY] * ns,
        out_specs=[q_spec, kv_c, kv_c, gain, gain, sink] + [_ANY] * ns,
        out_shape=[_sds(q.shape, BF16), _sds(kv.shape), _sds(kv.shape), _sds((1, HEAD_DIM)), _sds((1, HEAD_DIM)),
                   _sds(sinks.shape)] + _sibling_exchange_shapes(parts),
        scratch_shapes=_sibling_exchange_semaphores(ns) if ns else [],
        compiler_params=_params(("arbitrary",)), name="attn_bwd",
    )(q, kv, kv, cos, sin, cos, sin, q_gain, k_gain, sinks, do, *parts)
    return outs[:6], list(outs[6:])


def _time_shift_lerps(x, xs, gain, *mix):
    xn, xsn = _rms(x, gain), _rms(xs, gain)
    xx = xsn - xn
    return tuple(xn + xx * m for m in mix)


def _residual_norm(h, delta, gain):
    hn = h + delta
    return hn, _rms(hn, gain)


def _residual_norm2(h, delta, gain_a, gain_b):
    hn = h + delta
    return hn, _rms(hn, gain_a), _rms(hn, gain_b)


def _relu2(u):
    return jnp.square(jnp.maximum(u, 0.0))


def _sigmoid(z):
    return jax.nn.sigmoid(z)


def _shift_down(x):
    return jnp.pad(x[:-1], ((1, 0), (0, 0)))


def _shift_up(x):
    return jnp.pad(x[1:], ((0, 1), (0, 0)))


def _rope_tables(t):
    half = HEAD_DIM // 2
    inv_freq = jnp.power(ROPE_THETA, -jnp.arange(half, dtype=F32) / half)
    ang = jnp.arange(t, dtype=jnp.int32).astype(F32)[:, None] * inv_freq[None, :]
    cos, sin = jnp.cos(ang), jnp.sin(ang)
    return jnp.concatenate([cos, cos], axis=1), jnp.concatenate([sin, sin], axis=1)


def _mlp_fwd(hn, w_up, w_down, layer, up_dev_major):
    t = hn.shape[0]
    if up_dev_major:
        cw = w_up.shape[2]
        u = _mm(hn, w_up, name=f"mlp{layer}_up", dims=(t, D_FF, D_MODEL), tn=cw, tk=D_MODEL,
                b_spec=pl.BlockSpec((None, D_MODEL, cw), lambda i, j, q: (j, q, 0)))
    else:
        u = _mm(hn, w_up, name=f"mlp{layer}_up")
    out = _mm(u, w_down, a_pro=_relu2, name=f"mlp{layer}_down")
    return u, out


def _mlp_bwd(hn, u, dh, w_up, w_down, layer, up_dev_major):
    t = hn.shape[0]
    du = _mm(dh, w_down, tb=True, epi=lambda r, uu: r * (2.0 * jnp.maximum(uu, 0.0)), epi_args=(u,), out_dtype=BF16,
             name=f"mlp{layer}_du")
    d_down = _mm(u, dh, ta=True, a_pro=_relu2, name=f"mlp{layer}_ddown")
    if up_dev_major:
        cw = w_up.shape[2]
        d_up = _mm(hn, du, ta=True, name=f"mlp{layer}_dup", dims=(D_MODEL, D_FF, t), tn=cw,
                   o_spec=pl.BlockSpec((None, 1024, cw), lambda i, j, q: (j, i, 0)), o_shape=(N_DEV, D_MODEL, cw))
        dhn = _mm(du, w_up, tb=True, name=f"mlp{layer}_dhn", dims=(t, D_MODEL, D_FF), tk=cw,
                  b_spec=pl.BlockSpec((None, 1024, cw), lambda i, j, q: (q, j, 0)))
    else:
        d_up = _mm(hn, du, ta=True, name=f"mlp{layer}_dup")
        dhn = _mm(du, w_up, tb=True, name=f"mlp{layer}_dhn")
    return dhn, d_up, d_down


_LATE_WEIGHTS = ("mlp_w_up0", "mlp_w_up1", "mlp_w_down0", "mlp_w_down1", "b_w_q", "b_w_o")
_EARLY_GRADS = _LATE_WEIGHTS + ("a_w_out",)
_EARLY_GRADS_A = ("mlp_w_up1", "mlp_w_down1", "b_w_o")
_EARLY_GRADS_B = tuple(name for name in _EARLY_GRADS if name not in _EARLY_GRADS_A)


def _local_step(x, target, w, late_shards=None, up_dev_major=True):
    t = x.shape[0]
    g = {}
    w = dict(w)
    row = lambda: _sds((t, D_MODEL))
    rowb = lambda: _sds((t, D_MODEL), BF16)
    vec = lambda: _sds((1, D_MODEL))

    xs = _shift_down(x)
    mix = [w["a_mix"][i:i + 1] for i in range(6)]
    xr, xk, xv, xw, xa, xg = _rowwise(_time_shift_lerps, [x, xs], [w["a_norm"]] + mix, [rowb()] * 6, tm=256, name="tmix_lerp")
    r = _mm(xr, w["a_w_r"], name="tmix_r")
    k = _mm(xk, w["a_w_k"], name="tmix_k")
    v = _mm(xv, w["a_w_v"], name="tmix_v")
    lw1 = _mm(xw, w["a_w1"], name="tmix_w1")
    wl = _mm(lw1, w["a_w2"], a_pro=jnp.tanh, name="tmix_w2")
    la1 = _mm(xa, w["a_a1"], name="tmix_a1")
    al = _mm(la1, w["a_a2"], name="tmix_a2")
    lg1 = _mm(xg, w["a_g1"], name="tmix_g1")
    gate = _mm(lg1, w["a_g2"], a_pro=_sigmoid, name="tmix_g2")

    hv = lambda name: w[name].reshape(N_HEADS, 1, HEAD_DIM)
    wkv_params = [hv(name) for name in ("a_w0", "a_a0", "a_k_k", "a_k_a", "a_ln_x_w", "a_ln_x_b", "a_r_k")]
    y2, states, gathered = _wkv_fwd(r, k, v, wl, al, wkv_params, late_shards or ())
    for name, arr in zip(_LATE_WEIGHTS, gathered):
        w[name] = arr if name.startswith("mlp_w_up") else arr.reshape(N_DEV * arr.shape[1], arr.shape[2])
    (yg,) = _rowwise(lambda a, b: (a * b,), [y2, gate], [], [rowb()], name="tmix_gate")
    att = _mm(yg, w["a_w_out"], name="tmix_out")

    h1, hn0 = _rowwise(_residual_norm, [x, att], [w["mlp_norm0"]], [row(), rowb()], name="res_norm0")
    u0, m0 = _mlp_fwd(hn0, w["mlp_w_up0"], w["mlp_w_down0"], 0, up_dev_major)

    h2, kvn, qn = _rowwise(_residual_norm2, [h1, m0], [w["kv_norm"], w["b_norm"]], [row(), rowb(), rowb()], name="res_norm_kvq")
    kv = _mm(kvn, w["w_kv"], name="kv_proj")
    q = _mm(qn, w["b_w_q"], name="q_proj")
    cos, sin = _rope_tables(t)
    sinks = jnp.broadcast_to(w["b_sinks"].reshape(N_HEADS, 1, 1), (N_HEADS, 1, 128))
    o = _attn_fwd(q, kv, cos, sin, w["b_q_norm"], w["k_norm"], sinks)
    att2 = _mm(o, w["b_w_o"], name="attn_out")

    h3, hn1 = _rowwise(_residual_norm, [h2, att2], [w["mlp_norm1"]], [row(), rowb()], name="res_norm1")
    u1, m1 = _mlp_fwd(hn1, w["mlp_w_up1"], w["mlp_w_down1"], 1, up_dev_major)

    def loss_fn(h, m, tg):
        diff = (h + m) - tg
        part = 0.5 * jnp.sum(jnp.mean(jnp.square(diff), axis=-1, keepdims=True), axis=0, keepdims=True)
        return diff * (1.0 / D_MODEL), jnp.broadcast_to(part, (1, 128))

    dh4, loss = _rowwise(loss_fn, [h3, m1, target], [], [row()], [_sds((1, 128))], name="loss")

    def res_norm_bwd(h, dnext, dhn, gain):
        dh, dgain = _vjp_of(lambda hh, gg: (_rms(hh, gg),), 2, (0, 1))(h, gain, dhn)
        return dnext + dh, dgain

    dhn1, g["mlp_w_up1"], g["mlp_w_down1"] = _mlp_bwd(hn1, u1, dh4, w["mlp_w_up1"], w["mlp_w_down1"], 1, up_dev_major)
    dh3, g["mlp_norm1"] = _rowwise(res_norm_bwd, [h3, dh4, dhn1], [w["mlp_norm1"]], [row()], [vec()], name="res_norm1_bwd")

    g["b_w_o"] = _mm(o, dh3, ta=True, name="attn_out_dw")
    do = _mm(dh3, w["b_w_o"], tb=True, name="attn_out_dx")
    parts_a = [_device_major(name, g[name]) for name in _EARLY_GRADS_A] if late_shards is not None else []
    (dq, dkv_own, dkv_prev, g["b_q_norm"], g["k_norm"], dsinks), from_sibling_a = _attn_bwd(
        q, kv, cos, sin, w["b_q_norm"], w["k_norm"], sinks, do, parts_a)
    g["b_sinks"] = dsinks[:, 0, 0].reshape(1, N_HEADS)
    g["b_w_q"] = _mm(qn, dq, ta=True, name="q_proj_dw")
    dqn = _mm(dq, w["b_w_q"], tb=True, name="q_proj_dx")
    dkv_prev = jnp.pad(dkv_prev[ATT_BLOCK:], ((0, ATT_BLOCK), (0, 0)))
    (dkv,) = _rowwise(lambda a, b: (a + b,), [dkv_own, dkv_prev], [], [_sds(kv.shape, BF16)], name="kv_grad_sum")
    g["w_kv"] = _mm(kvn, dkv, ta=True, name="kv_proj_dw")
    dkvn = _mm(dkv, w["w_kv"], tb=True, name="kv_proj_dx")

    def res_norm2_bwd(h, dnext, dna, dnb, gain_a, gain_b):
        dha, dga = _vjp_of(lambda hh, gg: (_rms(hh, gg),), 2, (0, 1))(h, gain_a, dna)
        dhb, dgb = _vjp_of(lambda hh, gg: (_rms(hh, gg),), 2, (0, 1))(h, gain_b, dnb)
        return dnext + dha + dhb, dga, dgb

    dh2, g["kv_norm"], g["b_norm"] = _rowwise(res_norm2_bwd, [h2, dh3, dkvn, dqn], [w["kv_norm"], w["b_norm"]],
                                              [row()], [vec(), vec()], name="res_norm_kvq_bwd")

    dhn0, g["mlp_w_up0"], g["mlp_w_down0"] = _mlp_bwd(hn0, u0, dh2, w["mlp_w_up0"], w["mlp_w_down0"], 0, up_dev_major)
    dh1, g["mlp_norm0"] = _rowwise(res_norm_bwd, [h1, dh2, dhn0], [w["mlp_norm0"]], [row()], [vec()], name="res_norm0_bwd")

    g["a_w_out"] = _mm(yg, dh1, ta=True, name="tmix_out_dw")
    dyg = _mm(dh1, w["a_w_out"], tb=True, name="tmix_out_dx")
    dy2, dgate = _rowwise(lambda d, a, b: (d * b, d * a), [dyg, y2, gate], [], [row(), rowb()], name="tmix_gate_bwd")
    g["a_g2"] = _mm(lg1, dgate, ta=True, a_pro=_sigmoid, name="tmix_g2_dw")

    def dsigmoid(rr, z):
        s = jax.nn.sigmoid(z)
        return rr * s * (1.0 - s)

    dlg1 = _mm(dgate, w["a_g2"], tb=True, epi=dsigmoid, epi_args=(lg1,), out_dtype=BF16, name="tmix_g2_dx")
    g["a_g1"] = _mm(xg, dlg1, ta=True, name="tmix_g1_dw")
    dxg = _mm(dlg1, w["a_g1"], tb=True, name="tmix_g1_dx")

    early_sums = ()
    if late_shards is not None:
        sums = dict(zip(_EARLY_GRADS_A, _pair_sums(parts_a, from_sibling_a, _EARLY_GRADS_A)))
        sums.update(zip(_EARLY_GRADS_B, _chip_sums([_device_major(name, g[name]) for name in _EARLY_GRADS_B], _EARLY_GRADS_B, "early")))
        early_sums = [sums[name] for name in _EARLY_GRADS]
    (dr, dk, dv, dwl, dal), param_grads, early_reduced = _wkv_bwd(r, k, v, wl, al, wkv_params, states, dy2, early_sums)
    for name, pg in zip(("a_w0", "a_a0", "a_k_k", "a_k_a", "a_ln_x_w", "a_ln_x_b", "a_r_k"), param_grads):
        g[name] = pg.reshape(1, D_MODEL)

    g["a_w_r"] = _mm(xr, dr, ta=True, name="tmix_r_dw")
    g["a_w_k"] = _mm(xk, dk, ta=True, name="tmix_k_dw")
    g["a_w_v"] = _mm(xv, dv, ta=True, name="tmix_v_dw")
    dxr = _mm(dr, w["a_w_r"], tb=True, name="tmix_r_dx")
    dxk = _mm(dk, w["a_w_k"], tb=True, name="tmix_k_dx")
    dxv = _mm(dv, w["a_w_v"], tb=True, name="tmix_v_dx")
    g["a_w2"] = _mm(lw1, dwl, ta=True, a_pro=jnp.tanh, name="tmix_w2_dw")

    def dtanh(rr, z):
        th = jnp.tanh(z)
        return rr * (1.0 - th * th)

    dlw1 = _mm(dwl, w["a_w2"], tb=True, epi=dtanh, epi_args=(lw1,), out_dtype=BF16, name="tmix_w2_dx")
    g["a_w1"] = _mm(xw, dlw1, ta=True, name="tmix_w1_dw")
    dxw = _mm(dlw1, w["a_w1"], tb=True, name="tmix_w1_dx")
    g["a_a2"] = _mm(la1, dal, ta=True, name="tmix_a2_dw")
    dla1 = _mm(dal, w["a_a2"], tb=True, out_dtype=BF16, name="tmix_a2_dx")
    g["a_a1"] = _mm(xa, dla1, ta=True, name="tmix_a1_dw")
    dxa = _mm(dla1, w["a_a1"], tb=True, name="tmix_a1_dx")

    lerp_bwd = _vjp_of(_time_shift_lerps, 9, tuple(range(9)))

    def lerp_bwd_rows(x_, xs_, d0, d1, d2, d3, d4, d5, gain, *mx):
        return lerp_bwd(x_, xs_, gain, *mx, d0, d1, d2, d3, d4, d5)

    outs = _rowwise(lerp_bwd_rows, [x, xs, dxr, dxk, dxv, dxw, dxa, dxg], [w["a_norm"]] + mix, [row(), row()], [vec()] * 7,
                    tm=128, name="tmix_lerp_bwd")
    dx_a, dxs, g["a_norm"] = outs[0], outs[1], outs[2]
    g["a_mix"] = jnp.concatenate(outs[3:9], axis=0)
    (grad_x,) = _rowwise(lambda a, b, c: (a + b + c,), [dh1, dx_a, _shift_up(dxs)], [], [row()], name="grad_x_sum")
    return loss, grad_x, g, (early_reduced if late_shards is not None else None)


def _device_major(name, grad):
    return grad if name.startswith("mlp_w_up") else grad.reshape((N_DEV, grad.shape[0] // N_DEV, grad.shape[1]))


_ANY = pl.BlockSpec(memory_space=pl.ANY)
_MESH_ID = pl.DeviceIdType.MESH
N_PEERS = N_DEV - 1


def _linear(pos):
    return 4 * pos[0] + 2 * pos[1] + pos[2]


def _all_gather(shards, name):
    n = len(shards)

    def body(*refs):
        start, relay, finish = _gather_plan(refs[:n], refs[n:2 * n], *refs[2 * n:])
        start()
        relay()
        finish()

    return pl.pallas_call(
        body, out_shape=[_sds((N_DEV,) + s.shape, s.dtype) for s in shards], in_specs=[_ANY] * n, out_specs=[_ANY] * n,
        scratch_shapes=_gather_semaphores(n), name=name,
    )(*shards)


GATHER_COPIES = 8


def _gather_semaphores(n):
    return [pltpu.SemaphoreType.DMA((n * GATHER_COPIES,)), pltpu.SemaphoreType.DMA((n * GATHER_COPIES,)),
            pltpu.SemaphoreType.DMA((n,))]


def _gather_plan(ins, outs, send_sems, recv_sems, local_sems):
    n = len(ins)
    x, y, c = lax.axis_index("x"), lax.axis_index("y"), lax.axis_index("c")
    me, sibling = (x, y, c), (x, y, 1 - c)
    x_nbr, y_nbr, diag = (1 - x, y, c), (x, 1 - y, c), (1 - x, 1 - y, c)
    other = lambda pos: (pos[0], pos[1], 1 - c)

    def halves(a):
        rows = ins[a].shape[0]
        if rows % 32:
            return (0, rows), None
        return (0, rows // 2), (rows // 2, rows // 2)

    def copy(a, k, block, to, src=None, rows=None):
        dst = outs[a].at[_linear(block)]
        if rows is not None:
            dst = dst.at[pl.ds(rows[0], rows[1])]
        return pltpu.make_async_remote_copy(
            src_ref=dst if src is None else src, dst_ref=dst, send_sem=send_sems.at[a * GATHER_COPIES + k],
            recv_sem=recv_sems.at[a * GATHER_COPIES + k], device_id=to, device_id_type=_MESH_ID)

    def own_copies():
        mine = [pltpu.make_async_copy(ins[a], outs[a].at[_linear(me)], local_sems.at[a]) for a in range(n)]
        sent = []
        for a in range(n):
            sent += [copy(a, 0, me, sibling, src=ins[a]), copy(a, 1, me, x_nbr, src=ins[a]), copy(a, 2, me, y_nbr, src=ins[a])]
        return mine, sent

    def relayed_copies():
        sent = []
        for a in range(n):
            first, second = halves(a)
            sent += [copy(a, 3, x_nbr, y_nbr, rows=first), copy(a, 5, x_nbr, sibling), copy(a, 6, y_nbr, sibling)]
            if second is not None:
                sent.append(copy(a, 4, y_nbr, x_nbr, rows=second))
        return sent

    def start():
        mine, sent = own_copies()
        for cp in mine + sent:
            cp.start()

    def relay():
        for a in range(n):
            first, second = halves(a)
            copy(a, 1, x_nbr, me).wait_recv()
            copy(a, 3, x_nbr, y_nbr, rows=first).start()
            copy(a, 5, x_nbr, sibling).start()
        for a in range(n):
            first, second = halves(a)
            copy(a, 2, y_nbr, me).wait_recv()
            if second is not None:
                copy(a, 4, y_nbr, x_nbr, rows=second).start()
            copy(a, 6, y_nbr, sibling).start()

    def finish():
        mine, sent = own_copies()
        sent += relayed_copies()
        for a in range(n):
            first, second = halves(a)
            copy(a, 3, diag, me, rows=first).wait_recv()
            if second is not None:
                copy(a, 4, diag, me, rows=second).wait_recv()
            last = copy(a, 7, diag, sibling)
            last.start()
            sent.append(last)
        for a in range(n):
            copy(a, 0, other(me), me).wait_recv()
            copy(a, 5, other(x_nbr), me).wait_recv()
            copy(a, 6, other(y_nbr), me).wait_recv()
            copy(a, 7, other(diag), me).wait_recv()
        for cp in sent:
            cp.wait_send()
        for cp in mine:
            cp.wait()

    return start, relay, finish


N_CHIPS = 4


def _exchange_with_sibling(parts, name):
    n = len(parts)

    def body(*refs):
        start, finish = _sibling_exchange_plan(refs[:n], refs[n:2 * n], *refs[2 * n:])
        start()
        finish()

    return pl.pallas_call(
        body, out_shape=_sibling_exchange_shapes(parts), in_specs=[_ANY] * n, out_specs=[_ANY] * n,
        scratch_shapes=_sibling_exchange_semaphores(n), name=name,
    )(*parts)


def _sibling_exchange_shapes(parts):
    return [_sds((N_CHIPS,) + p.shape[1:], p.dtype) for p in parts]


def _sibling_exchange_semaphores(n):
    return [pltpu.SemaphoreType.DMA((n * N_CHIPS,)), pltpu.SemaphoreType.DMA((n * N_CHIPS,))]


def _sibling_exchange_plan(ins, outs, send_sems, recv_sems):
    n = len(ins)
    x, y, c = lax.axis_index("x"), lax.axis_index("y"), lax.axis_index("c")

    def all_copies():
        return [pltpu.make_async_remote_copy(
            src_ref=ins[a].at[2 * q + (1 - c)], dst_ref=outs[a].at[q], send_sem=send_sems.at[a * N_CHIPS + q],
            recv_sem=recv_sems.at[a * N_CHIPS + q], device_id=(x, y, 1 - c), device_id_type=_MESH_ID)
            for a in range(n) for q in range(N_CHIPS)]

    def start():
        for cp in all_copies():
            cp.start()

    def finish():
        for cp in all_copies():
            cp.wait()

    return start, finish


def _pair_sum(part, recv, core, out_dtype, name):
    _, r, cdim = recv.shape
    tr = max(8, min(r, (1 << 18) // cdim))
    assert r % tr == 0, (name, r, tr)

    def body(core_ref, p_ref, r_ref, o_ref):
        o_ref[...] = (p_ref[...] + r_ref[...]).astype(o_ref.dtype)

    grid_spec = pltpu.PrefetchScalarGridSpec(
        num_scalar_prefetch=1, grid=(N_CHIPS, r // tr),
        in_specs=[pl.BlockSpec((None, None, tr, cdim), lambda q, i, core_ref: (q, core_ref[0], i, 0)),
                  pl.BlockSpec((None, tr, cdim), lambda q, i, core_ref: (q, i, 0))],
        out_specs=pl.BlockSpec((None, tr, cdim), lambda q, i, core_ref: (q, i, 0)))
    return pl.pallas_call(
        body, grid_spec=grid_spec, out_shape=_sds((N_CHIPS, r, cdim), out_dtype),
        compiler_params=_params(("parallel", "parallel")), name=name,
    )(core, part.reshape(N_CHIPS, 2, r, cdim), recv)


def _exchange_between_chips(parts, name):
    n = len(parts)

    def body(*refs):
        start, finish = _chip_exchange_plan(refs[:n], refs[n:2 * n], *refs[2 * n:])
        start()
        finish()

    return pl.pallas_call(
        body, out_shape=[_sds(p.shape, p.dtype) for p in parts], in_specs=[_ANY] * n, out_specs=[_ANY] * n,
        scratch_shapes=_chip_exchange_semaphores(n), name=name,
    )(*parts)


def _chip_exchange_semaphores(n):
    n_other = N_CHIPS - 1
    return [pltpu.SemaphoreType.DMA((n * n_other,)), pltpu.SemaphoreType.DMA((n * n_other,)), pltpu.SemaphoreType.DMA((n,))]


def _chip_exchange_plan(ins, outs, send_sems, recv_sems, local_sems):
    n = len(ins)
    n_other = N_CHIPS - 1
    x, y, c = lax.axis_index("x"), lax.axis_index("y"), lax.axis_index("c")
    my_chip = 2 * x + y

    def all_copies():
        mine = [pltpu.make_async_copy(ins[a].at[my_chip], outs[a].at[my_chip], local_sems.at[a]) for a in range(n)]
        remote = []
        for j, (fx, fy) in enumerate([(1, 0), (0, 1), (1, 1)]):
            px, py = (1 - x if fx else x), (1 - y if fy else y)
            for a in range(n):
                remote.append(pltpu.make_async_remote_copy(
                    src_ref=ins[a].at[2 * px + py], dst_ref=outs[a].at[my_chip], send_sem=send_sems.at[a * n_other + j],
                    recv_sem=recv_sems.at[a * n_other + j], device_id=(px, py, c), device_id_type=_MESH_ID))
        return mine, remote

    def start():
        mine, remote = all_copies()
        for cp in mine + remote:
            cp.start()

    def finish():
        mine, remote = all_copies()
        for cp in remote + mine:
            cp.wait()

    return start, finish


def _chip_sums(parts, names, tag):
    return _pair_sums(parts, _exchange_with_sibling(parts, name="scatter_grads_sibling_" + tag), names)


def _pair_sums(parts, from_sibling, names):
    core = lax.axis_index("c").astype(jnp.int32).reshape(1)
    return [_pair_sum(p, r, core, F32 if nm.startswith("pack") else BF16, name="pair_sum_" + nm)
            for p, r, nm in zip(parts, from_sibling, names)]


def _reduce_scatter(parts, names, tag):
    return _exchange_between_chips(_chip_sums(parts, names, tag), name="scatter_grads_chips_" + tag)


def _adamw(w, m, v, slots, name):
    r, c = w.shape
    ns = slots.shape[0]
    tr = max(8, min(r, (1 << 18) // c))
    assert r % tr == 0, (name, r, tr)

    def body(w_ref, m_ref, v_ref, g_ref, g_out, d_out, m_out, v_out):
        g = g_ref[0].astype(F32)
        for s in range(1, ns):
            g = g + g_ref[s].astype(F32)
        m_new = ADAM_B1 * m_ref[...] + (1.0 - ADAM_B1) * g
        v_new = ADAM_B2 * v_ref[...] + (1.0 - ADAM_B2) * jnp.square(g)
        m_hat = m_new / (1.0 - ADAM_B1 ** ADAM_STEP)
        v_hat = v_new / (1.0 - ADAM_B2 ** ADAM_STEP)
        d_out[...] = -ADAM_LR * (m_hat / (jnp.sqrt(v_hat) + ADAM_EPS) + ADAM_WD * w_ref[...])
        g_out[...], m_out[...], v_out[...] = g, m_new, v_new

    spec = pl.BlockSpec((tr, c), lambda i: (i, 0))
    return pl.pallas_call(
        body, grid=(r // tr,), in_specs=[spec, spec, spec, pl.BlockSpec((ns, tr, c), lambda i: (0, i, 0))],
        out_specs=[spec] * 4, out_shape=[_sds((r, c))] * 4, compiler_params=_params(("parallel",)), name=name,
    )(w, m, v, slots)


_COL_VECTORS = ("a_norm", "a_mix", "a_w0", "a_a0", "a_k_k", "a_k_a", "a_ln_x_w", "a_ln_x_b")
_COL_VEC_ROWS = 16
_COL_ROWS = _COL_VEC_ROWS + 2 * LORA_PAD + 256
_ROW_COLS = 2 * LORA_PAD + 256 + 512
_REPL_ROWS = 8


def _pad_to(a, size, axis):
    widths = [(0, 0)] * a.ndim
    widths[axis] = (0, size - a.shape[axis])
    return jnp.pad(a, widths)


def _pack_cols(p):
    width = p["a_norm"].shape[-1]
    vecs = jnp.concatenate([p[n].reshape(-1, width) for n in _COL_VECTORS], axis=0)
    return jnp.concatenate([_pad_to(vecs, _COL_VEC_ROWS, 0), _pad_to(p["a_w2"].reshape(-1, width), LORA_PAD, 0),
                            _pad_to(p["a_a2"].reshape(-1, width), LORA_PAD, 0), p["a_g2"].reshape(-1, width)], axis=0)


def _unpack_cols(a, lead):
    width = a.shape[-1]
    out, row = {}, 0
    for n in _COL_VECTORS:
        k = 6 if n == "a_mix" else 1
        out[n] = a[row:row + k].reshape(lead + ((6, width) if n == "a_mix" else (width,)))
        row += k
    base = _COL_VEC_ROWS
    out["a_w2"] = a[base:base + 96].reshape(lead + (96, width))
    out["a_a2"] = a[base + LORA_PAD:base + LORA_PAD + 96].reshape(lead + (96, width))
    out["a_g2"] = a[base + 2 * LORA_PAD:].reshape(lead + (256, width))
    return out


def _pack_rows(p):
    rows = p["w_kv"].shape[0]
    return jnp.concatenate([_pad_to(p["a_w1"].reshape(rows, -1), LORA_PAD, 1), _pad_to(p["a_a1"].reshape(rows, -1), LORA_PAD, 1),
                            p["a_g1"].reshape(rows, -1), p["w_kv"]], axis=1)


def _unpack_rows(a, lead):
    rows = a.shape[0]
    return {"a_w1": a[:, :96].reshape(lead + (rows, 96)), "a_a1": a[:, LORA_PAD:LORA_PAD + 96].reshape(lead + (rows, 96)),
            "a_g1": a[:, 2 * LORA_PAD:2 * LORA_PAD + 256].reshape(lead + (rows, 256)), "w_kv": a[:, 2 * LORA_PAD + 256:]}


def _pack_repl(p):
    row = lambda a: _pad_to(a.reshape(1, -1), D_MODEL, 1)
    return jnp.concatenate([p["mlp_norm"].reshape(2, D_MODEL), row(p["kv_norm"]), row(p["b_norm"]), row(p["a_r_k"]),
                            row(p["k_norm"]), row(p["b_q_norm"]), row(p["b_sinks"])], axis=0)


def _unpack_repl(a):
    return {"mlp_norm": a[0:2], "kv_norm": a[2], "b_norm": a[3:4], "a_r_k": a[4].reshape(1, N_HEADS, HEAD_DIM),
            "k_norm": a[5, :HEAD_DIM], "b_q_norm": a[6:7, :HEAD_DIM], "b_sinks": a[7:8, :N_HEADS]}


_WEIGHTS = ("a_norm", "a_mix", "a_w_rkv", "a_w0", "a_w1", "a_w2", "a_a0", "a_a1", "a_a2", "a_g1", "a_g2", "a_k_k", "a_k_a",
            "a_r_k", "a_ln_x_w", "a_ln_x_b", "a_w_out", "mlp_norm", "mlp_w_up", "mlp_w_down", "kv_norm", "w_kv", "k_norm",
            "b_norm", "b_w_q", "b_q_norm", "b_sinks", "b_w_o")


def _big_shards(p):
    return [p["a_w_rkv"][0, 0], p["a_w_rkv"][0, 1], p["a_w_rkv"][0, 2], p["a_w_out"][0], p["mlp_w_up"][0], p["mlp_w_up"][1],
            p["mlp_w_down"][0], p["mlp_w_down"][1], p["b_w_q"][0], p["b_w_o"][0]]


_BIG_NAMES = ("a_w_r", "a_w_k", "a_w_v", "a_w_out", "mlp_w_up0", "mlp_w_up1", "mlp_w_down0", "mlp_w_down1", "b_w_q", "b_w_o")


def kernel(x, a_norm, a_mix, a_w_rkv, a_w0, a_w1, a_w2, a_a0, a_a1, a_a2, a_g1, a_g2, a_k_k, a_k_a, a_r_k, a_ln_x_w,
           a_ln_x_b, a_w_out, mlp_norm, mlp_w_up, mlp_w_down, kv_norm, w_kv, k_norm, b_norm, b_w_q, b_q_norm, b_sinks,
           b_w_o, loss_target, m_a_norm, m_a_mix, m_a_w_rkv, m_a_w0, m_a_w1, m_a_w2, m_a_a0, m_a_a1, m_a_a2, m_a_g1,
           m_a_g2, m_a_k_k, m_a_k_a, m_a_r_k, m_a_ln_x_w, m_a_ln_x_b, m_a_w_out, m_mlp_norm, m_mlp_w_up, m_mlp_w_down,
           m_kv_norm, m_w_kv, m_k_norm, m_b_norm, m_b_w_q, m_b_q_norm, m_b_sinks, m_b_w_o, v_a_norm, v_a_mix, v_a_w_rkv,
           v_a_w0, v_a_w1, v_a_w2, v_a_a0, v_a_a1, v_a_a2, v_a_g1, v_a_g2, v_a_k_k, v_a_k_a, v_a_r_k, v_a_ln_x_w,
           v_a_ln_x_b, v_a_w_out, v_mlp_norm, v_mlp_w_up, v_mlp_w_down, v_kv_norm, v_w_kv, v_k_norm, v_b_norm, v_b_w_q,
           v_b_q_norm, v_b_sinks, v_b_w_o):
    given = locals()
    wts = {n: given[n] for n in _WEIGHTS}
    mom = {n: given["m_" + n] for n in _WEIGHTS}
    var = {n: given["v_" + n] for n in _WEIGHTS}

    cols_w, rows_w, repl_w = _pack_cols(wts), _pack_rows(wts), _pack_repl(wts)
    big_w = _big_shards(wts)
    big_bf16 = dict(zip(_BIG_NAMES, [b.astype(BF16) for b in big_w]))
    first_names = [k for k in _BIG_NAMES if k not in _LATE_WEIGHTS]
    gathered = _all_gather([cols_w, rows_w] + [big_bf16[k] for k in first_names], name="gather_weights")
    full_cols = gathered[0].transpose(1, 0, 2).reshape(_COL_ROWS, D_MODEL)
    full_rows = gathered[1].reshape(D_MODEL, _ROW_COLS)
    w = {}
    w.update({k: v.reshape(v.shape[1:]) for k, v in _unpack_cols(full_cols, (1,)).items()})
    for k in ("a_norm", "a_w0", "a_a0", "a_k_k", "a_k_a", "a_ln_x_w", "a_ln_x_b"):
        w[k] = w[k].reshape(1, D_MODEL)
    for k in ("a_w2", "a_a2"):
        w[k] = _pad_to(w[k], LORA_PAD, 0)
    rows_full = _unpack_rows(full_rows, ())
    w["a_w1"], w["a_a1"] = _pad_to(rows_full["a_w1"], LORA_PAD, 1), _pad_to(rows_full["a_a1"], LORA_PAD, 1)
    w["a_g1"], w["w_kv"] = rows_full["a_g1"], rows_full["w_kv"]
    for k, arr in zip(first_names, gathered[2:]):
        w[k] = arr.reshape(N_DEV * arr.shape[1], arr.shape[2])
    w["mlp_norm0"], w["mlp_norm1"] = mlp_norm[0:1], mlp_norm[1:2]
    w["kv_norm"], w["k_norm"] = kv_norm.reshape(1, D_MODEL), k_norm.reshape(1, HEAD_DIM)
    w["b_norm"], w["b_q_norm"], w["b_sinks"], w["a_r_k"] = b_norm, b_q_norm, b_sinks, a_r_k.reshape(1, D_MODEL)

    loss_local, grad_x, g, early_reduced = _local_step(x[0], loss_target[0], w, [big_bf16[k] for k in _LATE_WEIGHTS])
    loss = lax.psum(loss_local[0, 0], MESH_AXES)

    g_lead = {k: g[k][None] for k in ("a_norm", "a_mix", "a_w0", "a_a0", "a_k_k", "a_k_a", "a_ln_x_w", "a_ln_x_b", "a_g2")}
    g_lead["a_w2"], g_lead["a_a2"] = g["a_w2"][None, :96], g["a_a2"][None, :96]
    g_cols = _pack_cols(g_lead).reshape(_COL_ROWS, N_DEV, D_MODEL // N_DEV).transpose(1, 0, 2)
    g_rows = _pack_rows({"a_w1": g["a_w1"][:, :96], "a_a1": g["a_a1"][:, :96], "a_g1": g["a_g1"], "w_kv": g["w_kv"]})
    g_rows = g_rows.reshape(N_DEV, D_MODEL // N_DEV, _ROW_COLS)
    late_names = tuple(k for k in _BIG_NAMES if k not in _EARLY_GRADS)
    late_reduced = _reduce_scatter([g_cols, g_rows] + [_device_major(k, g[k]) for k in late_names],
                                   ("pack_cols", "pack_rows") + late_names, "late")
    big_reduced = dict(zip(late_names, late_reduced[2:]))
    big_reduced.update(zip(_EARLY_GRADS, early_reduced))
    reduced = list(late_reduced[:2]) + [big_reduced[k] for k in _BIG_NAMES]
    g_repl = _pack_repl({"mlp_norm": jnp.concatenate([g["mlp_norm0"], g["mlp_norm1"]], axis=0), "kv_norm": g["kv_norm"],
                         "b_norm": g["b_norm"], "a_r_k": g["a_r_k"], "k_norm": g["k_norm"], "b_q_norm": g["b_q_norm"],
                         "b_sinks": g["b_sinks"]})
    (repl_slots,) = _all_gather([g_repl], name="gather_replicated_grads")

    res = {}
    cols4 = _adamw(cols_w, _pack_cols(mom), _pack_cols(var), reduced[0], name="adamw_cols")
    rows4 = _adamw(rows_w, _pack_rows(mom), _pack_rows(var), reduced[1], name="adamw_rows")
    repl4 = _adamw(repl_w, _pack_repl(mom), _pack_repl(var), repl_slots, name="adamw_replicated")
    for unpacked in ([_unpack_cols(a, (1,)) for a in cols4], [_unpack_rows(a, (1,)) for a in rows4], [_unpack_repl(a) for a in repl4]):
        for k in unpacked[0]:
            res[k] = tuple(u[k] for u in unpacked)
    big4 = [_adamw(bw, bm, bv, slots, name="adamw_" + k)
            for k, bw, bm, bv, slots in zip(_BIG_NAMES, big_w, _big_shards(mom), _big_shards(var), reduced[2:])]
    res["a_w_rkv"] = tuple(jnp.stack([big4[0][i], big4[1][i], big4[2][i]])[None] for i in range(4))
    res["a_w_out"] = tuple(a[None] for a in big4[3])
    res["mlp_w_up"] = tuple(jnp.stack([big4[4][i], big4[5][i]]) for i in range(4))
    res["mlp_w_down"] = tuple(jnp.stack([big4[6][i], big4[7][i]]) for i in range(4))
    res["b_w_q"] = tuple(a[None] for a in big4[8])
    res["b_w_o"] = tuple(a[None] for a in big4[9])
    res["w_kv"] = tuple(a.reshape(w_kv.shape) for a in res["w_kv"])

    outs = [loss, grad_x[None]]
    for i in range(4):
        outs += [res[n][i].reshape(given[n].shape) for n in _WEIGHTS]
    return tuple(outs)
```

```python
import functools
import math

import jax
import jax.numpy as jnp
from jax import lax
from jax.experimental import pallas as pl
from jax.experimental.pallas import tpu as pltpu

F32 = jnp.float32
BF16 = jnp.bfloat16

D_MODEL = 2048
N_HEADS = 32
HEAD_DIM = 64
N_KV_HEADS = 4
Q_PER_KV = 8
ATT_BLOCK = 128
WKV_CHUNK = 64
LORA_PAD = 128
D_FF = 8192
N_DEV = 8
RMS_EPS = 1e-6
GN_EPS = 64e-5
L2_EPS = 1e-12
ROPE_THETA = 10000.0
ADAM_LR, ADAM_B1, ADAM_B2, ADAM_EPS, ADAM_WD, ADAM_STEP = 0.001, 0.9, 0.999, 1e-08, 0.01, 10
MASK_VALUE = -1e30
VMEM_LIMIT_BYTES = 56 * 1024 * 1024
MESH_AXES = ("x", "y", "c")
HI = lax.Precision.HIGHEST

_NN = (((1,), (0,)), ((), ()))
_NT = (((1,), (1,)), ((), ()))
_TN = (((0,), (0,)), ((), ()))
_BNN = (((2,), (1,)), ((0,), (0,)))
_BNT = (((2,), (2,)), ((0,), (0,)))
_BTN = (((1,), (1,)), ((0,), (0,)))


def _params(sem):
    return pltpu.CompilerParams(dimension_semantics=sem, vmem_limit_bytes=VMEM_LIMIT_BYTES)


def _split2(a):
    hi = a.astype(BF16)
    return hi, (a - hi.astype(F32)).astype(BF16)


def _dot3(a, b, dims):
    ah, al = _split2(a)
    bh, bl = _split2(b)
    d = lambda p, q: lax.dot_general(p, q, dims, preferred_element_type=F32)
    return d(ah, bh) + (d(al, bh) + d(ah, bl))


@functools.partial(jax.custom_vjp, nondiff_argnums=(2,))
def _hdot(a, b, dims=_NN):
    return _dot3(a, b, dims)


def _hdot_fwd(a, b, dims):
    return _dot3(a, b, dims), (a, b)


def _hdot_bwd(dims, res, g):
    a, b = res
    nn, nt, tn = (_NN, _NT, _TN) if dims in (_NN, _NT, _TN) else (_BNN, _BNT, _BTN)
    if dims == nn:
        return _dot3(g, b, nt), _dot3(a, g, tn)
    if dims == nt:
        return _dot3(g, b, nn), _dot3(g, a, tn)
    assert dims == tn
    return _dot3(b, g, nt), _dot3(a, g, nn)


_hdot.defvjp(_hdot_fwd, _hdot_bwd)


def _tri_parts(x):
    hi = x.astype(BF16)
    r1 = x - hi.astype(F32)
    mid = r1.astype(BF16)
    return hi, mid, (r1 - mid.astype(F32)).astype(BF16)


@jax.custom_vjp
def _mask_dot(mask, x):
    mb = mask.astype(BF16)
    p0, p1, p2 = _tri_parts(x)
    d = lambda p: lax.dot_general(mb, p, _BNN, preferred_element_type=F32)
    return d(p0) + (d(p1) + d(p2))


def _mask_dot_fwd(mask, x):
    return _mask_dot(mask, x), mask


def _mask_dot_bwd(mask, g):
    mb = mask.astype(BF16)
    p0, p1, p2 = _tri_parts(g)
    d = lambda p: lax.dot_general(mb, p, _BTN, preferred_element_type=F32)
    return jnp.zeros_like(mask), d(p0) + (d(p1) + d(p2))


_mask_dot.defvjp(_mask_dot_fwd, _mask_dot_bwd)


def _b16dot(a, b, dims):
    return lax.dot_general(a.astype(BF16), b.astype(BF16), dims, preferred_element_type=F32)


@jax.custom_vjp
def _bdot(a, b):
    return _b16dot(a, b, _NN)


def _bdot_fwd(a, b):
    return _b16dot(a, b, _NN), (a, b)


def _bdot_bwd(res, g):
    a, b = res
    return _b16dot(g, b, _NT), _b16dot(a, g, _TN)


_bdot.defvjp(_bdot_fwd, _bdot_bwd)


@jax.custom_vjp
def _bdot_nt(a, b):
    return _b16dot(a, b, _NT)


def _bdot_nt_fwd(a, b):
    return _b16dot(a, b, _NT), (a, b)


def _bdot_nt_bwd(res, g):
    a, b = res
    return _b16dot(g, b, _NN), _b16dot(g, a, _TN)


_bdot_nt.defvjp(_bdot_nt_fwd, _bdot_nt_bwd)


def _rms(x, gain):
    return x * lax.rsqrt(jnp.mean(x * x, axis=-1, keepdims=True) + RMS_EPS) * gain


def _vjp_of(f, n_in, diff):
    def g(*args):
        ins, cts = args[:n_in], args[n_in:]

        def fd(*d):
            full = list(ins)
            for pos, i in enumerate(diff):
                full[i] = d[pos]
            return f(*full)

        _, pull = jax.vjp(fd, *[ins[i] for i in diff])
        return pull(tuple(cts))
    return g


def _mm(a, b, *, name, ta=False, tb=False, a_pro=None, epi=None, epi_args=(), out_dtype=F32,
        tm=1024, tn=1024, tk=2048, dims=None, b_spec=None, o_spec=None, o_shape=None):
    if dims is None:
        m, k = (a.shape[1], a.shape[0]) if ta else a.shape
        n = b.shape[0] if tb else b.shape[1]
    else:
        m, n, k = dims
    tm, tn, tk = min(tm, m), min(tn, n), min(tk, k)
    assert m % tm == 0 and n % tn == 0 and k % tk == 0, (name, m, n, k, tm, tn, tk)
    nk = k // tk
    ne = len(epi_args)
    cdims = (((0 if ta else 1,), (1 if tb else 0,)), ((), ()))

    def body(a_ref, b_ref, *rest):
        e_refs, o_ref, acc = rest[:ne], rest[ne], rest[ne + 1]
        kk = pl.program_id(2)

        @pl.when(kk == 0)
        def _():
            acc[...] = jnp.zeros_like(acc)

        av = a_ref[...]
        if a_pro is not None:
            av = a_pro(av.astype(F32))
        acc[...] += lax.dot_general(av.astype(BF16), b_ref[...].astype(BF16), cdims, preferred_element_type=F32)

        @pl.when(kk == nk - 1)
        def _():
            r = acc[...]
            if epi is not None:
                r = epi(r, *[e[...] for e in e_refs])
            o_ref[...] = r.astype(o_ref.dtype)

    a_spec = pl.BlockSpec((tk, tm), lambda i, j, q: (q, i)) if ta else pl.BlockSpec((tm, tk), lambda i, j, q: (i, q))
    if b_spec is None:
        b_spec = pl.BlockSpec((tn, tk), lambda i, j, q: (j, q)) if tb else pl.BlockSpec((tk, tn), lambda i, j, q: (q, j))
    if o_spec is None:
        o_spec = pl.BlockSpec((tm, tn), lambda i, j, q: (i, j))
        o_shape = (m, n)
    e_specs = [pl.BlockSpec((tm, tn), lambda i, j, q: (i, j)) for _ in epi_args]
    return pl.pallas_call(
        body, grid=(m // tm, n // tn, nk), in_specs=[a_spec, b_spec] + e_specs, out_specs=o_spec,
        out_shape=jax.ShapeDtypeStruct(o_shape, out_dtype), scratch_shapes=[pltpu.VMEM((tm, tn), F32)],
        compiler_params=_params(("parallel", "parallel", "arbitrary")), name=name,
    )(a, b, *epi_args)


def _rowwise(fn, rows, params, out_rows, out_params=(), *, tm=256, name):
    t = rows[0].shape[0]
    tm = min(tm, t)
    assert t % tm == 0
    nr, npar, nor, nop = len(rows), len(params), len(out_rows), len(out_params)

    def body(*refs):
        r, p = refs[:nr], refs[nr:nr + npar]
        o, op = refs[nr + npar:nr + npar + nor], refs[nr + npar + nor:]
        outs = fn(*[x[...] for x in r], *[x[...] for x in p])
        for ref, val in zip(o, outs[:nor]):
            ref[...] = val.astype(ref.dtype)
        if nop:
            @pl.when(pl.program_id(0) == 0)
            def _():
                for ref in op:
                    ref[...] = jnp.zeros_like(ref)

            for ref, val in zip(op, outs[nor:]):
                ref[...] += val.astype(F32)

    in_specs = [pl.BlockSpec((tm, x.shape[1]), lambda i: (i, 0)) for x in rows]
    in_specs += [pl.BlockSpec(p.shape, lambda i: (0, 0)) for p in params]
    out_specs = [pl.BlockSpec((tm, s.shape[1]), lambda i: (i, 0)) for s in out_rows]
    out_specs += [pl.BlockSpec(s.shape, lambda i: (0, 0)) for s in out_params]
    return pl.pallas_call(
        body, grid=(t // tm,), in_specs=in_specs, out_specs=out_specs, out_shape=list(out_rows) + list(out_params),
        compiler_params=_params(("arbitrary",)), name=name,
    )(*rows, *params)


def _sds(shape, dtype=F32):
    return jax.ShapeDtypeStruct(tuple(shape), dtype)


def _doubling_powers(l):
    powers = [l]
    for _ in range(int(math.log2(l.shape[-1])) - 1):
        powers.append(_dot3(powers[-1], powers[-1], _BNN))
    return powers


@jax.custom_vjp
def _unit_lower_solve(l, z):
    u = z
    for p in _doubling_powers(l):
        u = u + _dot3(p, u, _BNN)
    return u


def _unit_lower_solve_fwd(l, z):
    powers = _doubling_powers(l)
    u = z
    for p in powers:
        u = u + _dot3(p, u, _BNN)
    return u, (powers, u)


def _unit_lower_solve_bwd(res, du):
    powers, u = res
    g = du
    for p in powers:
        g = g + _dot3(p, g, _BTN)
    return _dot3(g, u, _BNT), g


_unit_lower_solve.defvjp(_unit_lower_solve_fwd, _unit_lower_solve_bwd)


def _wkv_chunk(s0, r, lw, k, v, a, b):
    nb, c, _ = r.shape
    ti = lax.broadcasted_iota(jnp.int32, (nb, c, c), 1)
    si = lax.broadcasted_iota(jnp.int32, (nb, c, c), 2)
    incl, strict = si <= ti, si < ti
    cum = _mask_dot(incl.astype(F32), lw)
    rcum = _mask_dot((si > ti).astype(F32), lw)
    tot = jnp.sum(lw, axis=1, keepdims=True)
    w_inv = jnp.exp(-cum)
    at, rt, bt, kt = a * jnp.exp(cum - lw), r * jnp.exp(cum), b * w_inv, k * w_inv
    l_ab = jnp.where(strict, _hdot(at, bt, _BNT), 0.0)
    l_ak = jnp.where(strict, _hdot(at, kt, _BNT), 0.0)
    t_rb = jnp.where(incl, _hdot(rt, bt, _BNT), 0.0)
    t_rk = jnp.where(incl, _hdot(rt, kt, _BNT), 0.0)
    u = _unit_lower_solve(l_ab, _hdot(at, s0, _BNT) + _hdot(l_ak, v, _BNN))
    y = _hdot(rt, s0, _BNT) + _hdot(t_rb, u, _BNN) + _hdot(t_rk, v, _BNN)
    e = jnp.exp(rcum)
    s1 = s0 * jnp.exp(tot) + _hdot(u, b * e, _BTN) + _hdot(v, k * e, _BTN)
    return y, s1


WKV_HEADS_PER_STEP = 16


def _first_and_last_step(grid):
    i, j = pl.program_id(0), pl.program_id(1)
    return jnp.logical_and(i == 0, j == 0), jnp.logical_and(i == grid[0] - 1, j == grid[1] - 1)


N_WKV_PARAMS = 7


def _tmix_chunk(s0, r, k, v, wl, al, w0, a0, k_k, k_a, ln_w, ln_b, r_k):
    lw, kmod, a, b = _wkv_prep(k, wl, al, w0, a0, k_k, k_a)
    y, s1 = _wkv_chunk(s0, r, lw, kmod, v, a, b)
    (y2,) = _wkv_post(y, r, kmod, v, ln_w, ln_b, r_k)
    return y2, s1


def _split_heads(x, nh):
    return x.reshape(x.shape[0], nh, HEAD_DIM).transpose(1, 0, 2)


def _merge_heads(xh):
    return jnp.concatenate([xh[h] for h in range(xh.shape[0])], axis=1)


def _wkv_fwd(r, k, v, wl, al, params, shards=()):
    t = r.shape[0]
    nh, n = N_HEADS, HEAD_DIM
    nc = t // WKV_CHUNK
    hb = WKV_HEADS_PER_STEP
    grid = (nh // hb, nc)
    ns = len(shards)
    n_in = 5 + N_WKV_PARAMS

    def body(*refs):
        y_ref, s_ref = refs[n_in + ns:n_in + ns + 2]
        state = refs[n_in + 2 * ns + 2]
        if ns:
            start, relay, finish = _gather_plan(refs[n_in:n_in + ns], refs[n_in + ns + 2:n_in + 2 * ns + 2], *refs[n_in + 2 * ns + 3:])
            first, last = _first_and_last_step(grid)
            pl.when(first)(start)
            relay_step = (5 * grid[0] * nc) // 8
            pl.when(jnp.logical_and(pl.program_id(0) == relay_step // nc, pl.program_id(1) == relay_step % nc))(relay)

        @pl.when(pl.program_id(1) == 0)
        def _():
            state[...] = jnp.zeros_like(state)

        s0 = state[...]
        s_ref[:, 0] = s0
        rows = [_split_heads(ref[...], hb) for ref in refs[:5]]
        y, s1 = _tmix_chunk(s0, *rows, *[ref[...] for ref in refs[5:n_in]])
        y_ref[...] = _merge_heads(y)
        state[...] = s1
        if ns:
            pl.when(last)(finish)

    blk = pl.BlockSpec((WKV_CHUNK, hb * n), lambda h, c: (c, h))
    pblk = pl.BlockSpec((hb, 1, n), lambda h, c: (h, 0, 0))
    sblk = pl.BlockSpec((hb, 1, n, n), lambda h, c: (h, c, 0, 0))
    outs = pl.pallas_call(
        body, grid=grid, in_specs=[blk] * 5 + [pblk] * N_WKV_PARAMS + [_ANY] * ns, out_specs=[blk, sblk] + [_ANY] * ns,
        out_shape=[_sds((t, nh * n)), _sds((nh, nc, n, n))] + [_sds((N_DEV,) + s.shape, s.dtype) for s in shards],
        scratch_shapes=[pltpu.VMEM((hb, n, n), F32)] + (_gather_semaphores(ns) if ns else []),
        compiler_params=_params(("arbitrary", "arbitrary")), name="wkv_fwd",
    )(r, k, v, wl, al, *params, *shards)
    return outs[0], outs[1], list(outs[2:])


def _wkv_bwd(r, k, v, wl, al, params, states, dy, chip_sums=()):
    t = r.shape[0]
    nh, n = N_HEADS, HEAD_DIM
    nc = t // WKV_CHUNK
    hb = WKV_HEADS_PER_STEP
    grid = (nh // hb, nc)
    ns = len(chip_sums)
    n_in = 5 + N_WKV_PARAMS
    n_out = 5 + N_WKV_PARAMS

    def body(*refs):
        s_ref, dy_ref = refs[n_in:n_in + 2]
        out_refs = refs[n_in + 2 + ns:n_in + 2 + ns + n_out]
        dstate = refs[n_in + 2 + 2 * ns + n_out]
        if ns:
            start, finish = _chip_exchange_plan(refs[n_in + 2:n_in + 2 + ns], refs[n_in + 2 + ns + n_out:n_in + 2 + 2 * ns + n_out],
                                                *refs[n_in + 3 + 2 * ns + n_out:])
            first, last = _first_and_last_step(grid)
            pl.when(first)(start)

        @pl.when(pl.program_id(1) == 0)
        def _():
            dstate[...] = jnp.zeros_like(dstate)
            for ref in out_refs[5:]:
                ref[...] = jnp.zeros_like(ref)

        rows = [_split_heads(ref[...], hb) for ref in refs[:5]]
        _, pull = jax.vjp(_tmix_chunk, s_ref[:, 0], *rows, *[ref[...] for ref in refs[5:n_in]])
        grads = pull((_split_heads(dy_ref[...], hb), dstate[...]))
        dstate[...] = grads[0]
        for ref, val in zip(out_refs[:5], grads[1:6]):
            ref[...] = _merge_heads(val).astype(ref.dtype)
        for ref, val in zip(out_refs[5:], grads[6:]):
            ref[...] += val
        if ns:
            pl.when(last)(finish)

    blk = pl.BlockSpec((WKV_CHUNK, hb * n), lambda h, c: (nc - 1 - c, h))
    pblk = pl.BlockSpec((hb, 1, n), lambda h, c: (h, 0, 0))
    sblk = pl.BlockSpec((hb, 1, n, n), lambda h, c: (h, nc - 1 - c, 0, 0))
    outs = pl.pallas_call(
        body, grid=grid, in_specs=[blk] * 5 + [pblk] * N_WKV_PARAMS + [sblk, blk] + [_ANY] * ns,
        out_specs=[blk] * 5 + [pblk] * N_WKV_PARAMS + [_ANY] * ns,
        out_shape=[_sds((t, nh * n), BF16)] * 5 + [_sds((nh, 1, n))] * N_WKV_PARAMS + [_sds(p.shape, p.dtype) for p in chip_sums],
        scratch_shapes=[pltpu.VMEM((hb, n, n), F32)] + (_chip_exchange_semaphores(ns) if ns else []),
        compiler_params=_params(("arbitrary", "arbitrary")), name="wkv_bwd",
    )(r, k, v, wl, al, *params, states, dy, *chip_sums)
    return outs[:5], outs[5:n_out], list(outs[n_out:])


def _wkv_prep(k, wl, al, w0, a0, k_k, k_a):
    z = -(w0 + wl)
    softplus = jnp.maximum(z, 0.0) + jnp.log1p(jnp.exp(-jnp.abs(z)))
    lw = -jnp.exp(-softplus - 0.5)
    asig = jax.nn.sigmoid(a0 + al)
    kk = k * k_k
    kk = kk / jnp.maximum(jnp.sqrt(jnp.sum(kk * kk, axis=-1, keepdims=True)), L2_EPS)
    kmod = k * (1.0 + (asig - 1.0) * k_a)
    return lw, kmod, -kk, kk * asig


def _wkv_post(y, r, kmod, v, ln_w, ln_b, r_k):
    mu = jnp.mean(y, axis=-1, keepdims=True)
    var = jnp.mean(jnp.square(y - mu), axis=-1, keepdims=True)
    yn = (y - mu) * lax.rsqrt(var + GN_EPS)
    yn = yn * ln_w + ln_b
    return (yn + jnp.sum(r * kmod * r_k, axis=-1, keepdims=True) * v,)


def _attn_group(nonzero_block, q, kc, kp, vc, vp, cos_c, sin_c, cos_p, sin_p, q_gain, k_gain, sinks):
    ri = lax.broadcasted_iota(jnp.int32, (HEAD_DIM, HEAD_DIM), 0)
    ci = lax.broadcasted_iota(jnp.int32, (HEAD_DIM, HEAD_DIM), 1)
    half = HEAD_DIM // 2
    rot = jnp.where(ri == ci + half, -1.0, 0.0) + jnp.where(ri + half == ci, 1.0, 0.0)
    rows = Q_PER_KV * ATT_BLOCK

    def rope(x, cos, sin):
        return x * cos + _hdot(x, rot, _NN) * sin

    kcr = rope(_rms(kc, k_gain), cos_c, sin_c)
    kpr = rope(_rms(kp, k_gain), cos_p, sin_p)
    qn = _rms(q, q_gain)
    qr = qn * cos_c + _hdot(qn.reshape(rows, HEAD_DIM), rot, _NN).reshape(q.shape) * sin_c
    q2 = qr.reshape(rows, HEAD_DIM)
    qi = lax.broadcasted_iota(jnp.int32, (1, ATT_BLOCK, ATT_BLOCK), 1)
    ki = lax.broadcasted_iota(jnp.int32, (1, ATT_BLOCK, ATT_BLOCK), 2)
    mask_c = ki <= qi
    mask_p = jnp.logical_and(ki > qi, nonzero_block)
    lane0 = (lax.broadcasted_iota(jnp.int32, (1, 1, 128), 2) == 0).astype(F32)
    shape3 = (Q_PER_KV, ATT_BLOCK, ATT_BLOCK)
    sc = jnp.where(mask_c, (_bdot_nt(q2, kcr) * (HEAD_DIM ** -0.5)).reshape(shape3), MASK_VALUE)
    sp = jnp.where(mask_p, (_bdot_nt(q2, kpr) * (HEAD_DIM ** -0.5)).reshape(shape3), MASK_VALUE)
    sk = jnp.sum(sinks * lane0, axis=2, keepdims=True)
    mx = jnp.maximum(jnp.maximum(jnp.max(sc, axis=2, keepdims=True), jnp.max(sp, axis=2, keepdims=True)), sk)
    mx = lax.stop_gradient(mx)
    ec, ep = jnp.exp(sc - mx), jnp.exp(sp - mx)
    den = jnp.sum(ec, axis=2, keepdims=True) + jnp.sum(ep, axis=2, keepdims=True) + jnp.exp(sk - mx)
    out = _bdot((ec / den).reshape(rows, ATT_BLOCK), vc) + _bdot((ep / den).reshape(rows, ATT_BLOCK), vp)
    return out.reshape(q.shape)


def _attn_rows(nonzero_block, qs, kcs, kps, vcs, vps, tabs, q_gain, k_gain, sinks):
    return [_attn_group(nonzero_block, qs[g], kcs[g], kps[g], vcs[g], vps[g], *tabs, q_gain, k_gain, sinks[g])
            for g in range(N_KV_HEADS)]


def _attn_operands(q_tile, kvc_tile, kvp_tile, sink_ref):
    q3 = _split_heads(q_tile, N_HEADS)
    kvc, kvp = _split_heads(kvc_tile, 2 * N_KV_HEADS), _split_heads(kvp_tile, 2 * N_KV_HEADS)
    groups = range(N_KV_HEADS)
    qs = [q3[Q_PER_KV * g:Q_PER_KV * (g + 1)] for g in groups]
    sinks = [sink_ref[Q_PER_KV * g:Q_PER_KV * (g + 1)] for g in groups]
    return (qs, [kvc[g] for g in groups], [kvp[g] for g in groups], [kvc[N_KV_HEADS + g] for g in groups],
            [kvp[N_KV_HEADS + g] for g in groups], sinks)


def _attn_specs(t):
    nb = t // ATT_BLOCK
    prev = lambda n: jnp.maximum(n - 1, 0)
    kv_width = 2 * N_KV_HEADS * HEAD_DIM
    q_spec = pl.BlockSpec((ATT_BLOCK, D_MODEL), lambda n: (n, 0))
    kv_c = pl.BlockSpec((ATT_BLOCK, kv_width), lambda n: (n, 0))
    kv_p = pl.BlockSpec((ATT_BLOCK, kv_width), lambda n: (prev(n), 0))
    tab_c = pl.BlockSpec((ATT_BLOCK, HEAD_DIM), lambda n: (n, 0))
    tab_p = pl.BlockSpec((ATT_BLOCK, HEAD_DIM), lambda n: (prev(n), 0))
    gain = pl.BlockSpec((1, HEAD_DIM), lambda n: (0, 0))
    sink = pl.BlockSpec((N_HEADS, 1, 128), lambda n: (0, 0, 0))
    return nb, q_spec, kv_c, kv_p, tab_c, tab_p, gain, sink


def _attn_fwd(q, kv, cos, sin, q_gain, k_gain, sinks):
    t = q.shape[0]
    nb, q_spec, kv_c, kv_p, tab_c, tab_p, gain, sink = _attn_specs(t)

    def body(q_ref, kvc, kvp, cc, sc, cp, sp, qg, kg, sk, o_ref):
        qs, kcs, kps, vcs, vps, sinks_ = _attn_operands(q_ref[...], kvc[...], kvp[...], sk)
        outs = _attn_rows(pl.program_id(0) > 0, qs, kcs, kps, vcs, vps, (cc[...], sc[...], cp[...], sp[...]),
                          qg[...], kg[...], sinks_)
        o_ref[...] = _merge_heads(jnp.concatenate(outs, axis=0)).astype(o_ref.dtype)

    return pl.pallas_call(
        body, grid=(nb,), in_specs=[q_spec, kv_c, kv_p, tab_c, tab_c, tab_p, tab_p, gain, gain, sink],
        out_specs=q_spec, out_shape=_sds(q.shape, BF16), compiler_params=_params(("arbitrary",)), name="attn_fwd",
    )(q, kv, kv, cos, sin, cos, sin, q_gain, k_gain, sinks)


def _attn_bwd(q, kv, cos, sin, q_gain, k_gain, sinks, do, parts=()):
    t = q.shape[0]
    nb, q_spec, kv_c, kv_p, tab_c, tab_p, gain, sink = _attn_specs(t)
    ns = len(parts)

    def body(*refs):
        q_ref, kvc, kvp, cc, sc, cp, sp, qg, kg, sk, do_ref = refs[:11]
        dq_ref, dkvc_ref, dkvp_ref, dqg_ref, dkg_ref, dsk_ref = refs[11 + ns:17 + ns]
        if ns:
            start, finish = _sibling_exchange_plan(refs[11:11 + ns], refs[17 + ns:17 + 2 * ns], *refs[17 + 2 * ns:])
            pl.when(pl.program_id(0) == 0)(start)
        nonzero = pl.program_id(0) > 0
        tabs = (cc[...], sc[...], cp[...], sp[...])
        qs, kcs, kps, vcs, vps, sinks_ = _attn_operands(q_ref[...], kvc[...], kvp[...], sk)

        def f(qs_, kcs_, kps_, vcs_, vps_, qgv, kgv, sks):
            return _attn_rows(nonzero, qs_, kcs_, kps_, vcs_, vps_, tabs, qgv, kgv, sks)

        _, pull = jax.vjp(f, qs, kcs, kps, vcs, vps, qg[...], kg[...], sinks_)
        do3 = _split_heads(do_ref[...], N_HEADS)
        dqs, dkcs, dkps, dvcs, dvps, dqg, dkg, dsks = pull([do3[Q_PER_KV * g:Q_PER_KV * (g + 1)] for g in range(N_KV_HEADS)])
        dq_ref[...] = _merge_heads(jnp.concatenate(dqs, axis=0)).astype(dq_ref.dtype)
        dkvc_ref[...] = jnp.concatenate(dkcs + dvcs, axis=1)
        dkvp_ref[...] = jnp.concatenate(dkps + dvps, axis=1)

        @pl.when(pl.program_id(0) == 0)
        def _():
            dqg_ref[...] = jnp.zeros_like(dqg_ref)
            dkg_ref[...] = jnp.zeros_like(dkg_ref)
            dsk_ref[...] = jnp.zeros_like(dsk_ref)

        dqg_ref[...] += dqg
        dkg_ref[...] += dkg
        for g in range(N_KV_HEADS):
            dsk_ref[Q_PER_KV * g:Q_PER_KV * (g + 1)] += dsks[g]
        if ns:
            pl.when(pl.program_id(0) == nb - 1)(finish)

    outs = pl.pallas_call(
        body, grid=(nb,), in_specs=[q_spec, kv_c, kv_p, tab_c, tab_c, tab_p, tab_p, gain, gain, sink, q_spec] + [_ANY] * ns,
        out_specs=[q_spec, kv_c, kv_c, gain, gain, sink] + [_ANY] * ns,
        out_shape=[_sds(q.shape, BF16), _sds(kv.shape), _sds(kv.shape), _sds((1, HEAD_DIM)), _sds((1, HEAD_DIM)),
                   _sds(sinks.shape)] + _sibling_exchange_shapes(parts),
        scratch_shapes=_sibling_exchange_semaphores(ns) if ns else [],
        compiler_params=_params(("arbitrary",)), name="attn_bwd",
    )(q, kv, kv, cos, sin, cos, sin, q_gain, k_gain, sinks, do, *parts)
    return outs[:6], list(outs[6:])


def _time_shift_lerps(x, xs, gain, *mix):
    xn, xsn = _rms(x, gain), _rms(xs, gain)
    xx = xsn - xn
    return tuple(xn + xx * m for m in mix)


def _residual_norm(h, delta, gain):
    hn = h + delta
    return hn, _rms(hn, gain)


def _residual_norm2(h, delta, gain_a, gain_b):
    hn = h + delta
    return hn, _rms(hn, gain_a), _rms(hn, gain_b)


def _relu2(u):
    return jnp.square(jnp.maximum(u, 0.0))


def _sigmoid(z):
    return jax.nn.sigmoid(z)


def _shift_down(x):
    return jnp.pad(x[:-1], ((1, 0), (0, 0)))


def _shift_up(x):
    return jnp.pad(x[1:], ((0, 1), (0, 0)))


def _rope_tables(t):
    half = HEAD_DIM // 2
    inv_freq = jnp.power(ROPE_THETA, -jnp.arange(half, dtype=F32) / half)
    ang = jnp.arange(t, dtype=jnp.int32).astype(F32)[:, None] * inv_freq[None, :]
    cos, sin = jnp.cos(ang), jnp.sin(ang)
    return jnp.concatenate([cos, cos], axis=1), jnp.concatenate([sin, sin], axis=1)


def _mlp_fwd(hn, w_up, w_down, layer, up_dev_major):
    t = hn.shape[0]
    if up_dev_major:
        cw = w_up.shape[2]
        u = _mm(hn, w_up, name=f"mlp{layer}_up", dims=(t, D_FF, D_MODEL), tn=cw, tk=D_MODEL,
                b_spec=pl.BlockSpec((None, D_MODEL, cw), lambda i, j, q: (j, q, 0)))
    else:
        u = _mm(hn, w_up, name=f"mlp{layer}_up")
    out = _mm(u, w_down, a_pro=_relu2, name=f"mlp{layer}_down")
    return u, out


def _mlp_bwd(hn, u, dh, w_up, w_down, layer, up_dev_major):
    t = hn.shape[0]
    du = _mm(dh, w_down, tb=True, epi=lambda r, uu: r * (2.0 * jnp.maximum(uu, 0.0)), epi_args=(u,), out_dtype=BF16,
             name=f"mlp{layer}_du")
    d_down = _mm(u, dh, ta=True, a_pro=_relu2, name=f"mlp{layer}_ddown")
    if up_dev_major:
        cw = w_up.shape[2]
        d_up = _mm(hn, du, ta=True, name=f"mlp{layer}_dup", dims=(D_MODEL, D_FF, t), tn=cw,
                   o_spec=pl.BlockSpec((None, 1024, cw), lambda i, j, q: (j, i, 0)), o_shape=(N_DEV, D_MODEL, cw))
        dhn = _mm(du, w_up, tb=True, name=f"mlp{layer}_dhn", dims=(t, D_MODEL, D_FF), tk=cw,
                  b_spec=pl.BlockSpec((None, 1024, cw), lambda i, j, q: (q, j, 0)))
    else:
        d_up = _mm(hn, du, ta=True, name=f"mlp{layer}_dup")
        dhn = _mm(du, w_up, tb=True, name=f"mlp{layer}_dhn")
    return dhn, d_up, d_down


_LATE_WEIGHTS = ("mlp_w_up0", "mlp_w_up1", "mlp_w_down0", "mlp_w_down1", "b_w_q", "b_w_o")
_EARLY_GRADS = _LATE_WEIGHTS + ("a_w_out",)
_EARLY_GRADS_A = ("mlp_w_up1", "mlp_w_down1", "b_w_o")
_EARLY_GRADS_B = tuple(name for name in _EARLY_GRADS if name not in _EARLY_GRADS_A)


def _local_step(x, target, w, late_shards=None, up_dev_major=True):
    t = x.shape[0]
    g = {}
    w = dict(w)
    row = lambda: _sds((t, D_MODEL))
    rowb = lambda: _sds((t, D_MODEL), BF16)
    vec = lambda: _sds((1, D_MODEL))

    xs = _shift_down(x)
    mix = [w["a_mix"][i:i + 1] for i in range(6)]
    xr, xk, xv, xw, xa, xg = _rowwise(_time_shift_lerps, [x, xs], [w["a_norm"]] + mix, [rowb()] * 6, tm=256, name="tmix_lerp")
    r = _mm(xr, w["a_w_r"], name="tmix_r")
    k = _mm(xk, w["a_w_k"], name="tmix_k")
    v = _mm(xv, w["a_w_v"], name="tmix_v")
    lw1 = _mm(xw, w["a_w1"], name="tmix_w1")
    wl = _mm(lw1, w["a_w2"], a_pro=jnp.tanh, name="tmix_w2")
    la1 = _mm(xa, w["a_a1"], name="tmix_a1")
    al = _mm(la1, w["a_a2"], name="tmix_a2")
    lg1 = _mm(xg, w["a_g1"], name="tmix_g1")
    gate = _mm(lg1, w["a_g2"], a_pro=_sigmoid, name="tmix_g2")

    hv = lambda name: w[name].reshape(N_HEADS, 1, HEAD_DIM)
    wkv_params = [hv(name) for name in ("a_w0", "a_a0", "a_k_k", "a_k_a", "a_ln_x_w", "a_ln_x_b", "a_r_k")]
    y2, states, gathered = _wkv_fwd(r, k, v, wl, al, wkv_params, late_shards or ())
    for name, arr in zip(_LATE_WEIGHTS, gathered):
        w[name] = arr if name.startswith("mlp_w_up") else arr.reshape(N_DEV * arr.shape[1], arr.shape[2])
    (yg,) = _rowwise(lambda a, b: (a * b,), [y2, gate], [], [rowb()], name="tmix_gate")
    att = _mm(yg, w["a_w_out"], name="tmix_out")

    h1, hn0 = _rowwise(_residual_norm, [x, att], [w["mlp_norm0"]], [row(), rowb()], name="res_norm0")
    u0, m0 = _mlp_fwd(hn0, w["mlp_w_up0"], w["mlp_w_down0"], 0, up_dev_major)

    h2, kvn, qn = _rowwise(_residual_norm2, [h1, m0], [w["kv_norm"], w["b_norm"]], [row(), rowb(), rowb()], name="res_norm_kvq")
    kv = _mm(kvn, w["w_kv"], name="kv_proj")
    q = _mm(qn, w["b_w_q"], name="q_proj")
    cos, sin = _rope_tables(t)
    sinks = jnp.broadcast_to(w["b_sinks"].reshape(N_HEADS, 1, 1), (N_HEADS, 1, 128))
    o = _attn_fwd(q, kv, cos, sin, w["b_q_norm"], w["k_norm"], sinks)
    att2 = _mm(o, w["b_w_o"], name="attn_out")

    h3, hn1 = _rowwise(_residual_norm, [h2, att2], [w["mlp_norm1"]], [row(), rowb()], name="res_norm1")
    u1, m1 = _mlp_fwd(hn1, w["mlp_w_up1"], w["mlp_w_down1"], 1, up_dev_major)

    def loss_fn(h, m, tg):
        diff = (h + m) - tg
        part = 0.5 * jnp.sum(jnp.mean(jnp.square(diff), axis=-1, keepdims=True), axis=0, keepdims=True)
        dh = diff * (1.0 / D_MODEL)
        return dh, dh, jnp.broadcast_to(part, (1, 128))

    dh4, dh4b, loss = _rowwise(loss_fn, [h3, m1, target], [], [row(), rowb()], [_sds((1, 128))], name="loss")

    def res_norm_bwd(h, dnext, dhn, gain):
        dh, dgain = _vjp_of(lambda hh, gg: (_rms(hh, gg),), 2, (0, 1))(h, gain, dhn)
        return dnext + dh, dnext + dh, dgain

    dhn1, g["mlp_w_up1"], g["mlp_w_down1"] = _mlp_bwd(hn1, u1, dh4b, w["mlp_w_up1"], w["mlp_w_down1"], 1, up_dev_major)
    dh3, dh3b, g["mlp_norm1"] = _rowwise(res_norm_bwd, [h3, dh4, dhn1], [w["mlp_norm1"]], [row(), rowb()], [vec()], name="res_norm1_bwd")

    g["b_w_o"] = _mm(o, dh3b, ta=True, name="attn_out_dw")
    do = _mm(dh3b, w["b_w_o"], tb=True, name="attn_out_dx")
    parts_a = [_device_major(name, g[name]) for name in _EARLY_GRADS_A] if late_shards is not None else []
    (dq, dkv_own, dkv_prev, g["b_q_norm"], g["k_norm"], dsinks), from_sibling_a = _attn_bwd(
        q, kv, cos, sin, w["b_q_norm"], w["k_norm"], sinks, do, parts_a)
    g["b_sinks"] = dsinks[:, 0, 0].reshape(1, N_HEADS)
    g["b_w_q"] = _mm(qn, dq, ta=True, name="q_proj_dw")
    dqn = _mm(dq, w["b_w_q"], tb=True, name="q_proj_dx")
    dkv_prev = jnp.pad(dkv_prev[ATT_BLOCK:], ((0, ATT_BLOCK), (0, 0)))
    (dkv,) = _rowwise(lambda a, b: (a + b,), [dkv_own, dkv_prev], [], [_sds(kv.shape, BF16)], name="kv_grad_sum")
    g["w_kv"] = _mm(kvn, dkv, ta=True, name="kv_proj_dw")
    dkvn = _mm(dkv, w["w_kv"], tb=True, name="kv_proj_dx")

    def res_norm2_bwd(h, dnext, dna, dnb, gain_a, gain_b):
        dha, dga = _vjp_of(lambda hh, gg: (_rms(hh, gg),), 2, (0, 1))(h, gain_a, dna)
        dhb, dgb = _vjp_of(lambda hh, gg: (_rms(hh, gg),), 2, (0, 1))(h, gain_b, dnb)
        return dnext + dha + dhb, dnext + dha + dhb, dga, dgb

    dh2, dh2b, g["kv_norm"], g["b_norm"] = _rowwise(res_norm2_bwd, [h2, dh3, dkvn, dqn], [w["kv_norm"], w["b_norm"]],
                                                    [row(), rowb()], [vec(), vec()], name="res_norm_kvq_bwd")

    dhn0, g["mlp_w_up0"], g["mlp_w_down0"] = _mlp_bwd(hn0, u0, dh2b, w["mlp_w_up0"], w["mlp_w_down0"], 0, up_dev_major)
    dh1, dh1b, g["mlp_norm0"] = _rowwise(res_norm_bwd, [h1, dh2, dhn0], [w["mlp_norm0"]], [row(), rowb()], [vec()], name="res_norm0_bwd")

    g["a_w_out"] = _mm(yg, dh1b, ta=True, name="tmix_out_dw")
    dyg = _mm(dh1b, w["a_w_out"], tb=True, name="tmix_out_dx")
    dy2, dgate = _rowwise(lambda d, a, b: (d * b, d * a), [dyg, y2, gate], [], [row(), rowb()], name="tmix_gate_bwd")
    g["a_g2"] = _mm(lg1, dgate, ta=True, a_pro=_sigmoid, name="tmix_g2_dw")

    def dsigmoid(rr, z):
        s = jax.nn.sigmoid(z)
        return rr * s * (1.0 - s)

    dlg1 = _mm(dgate, w["a_g2"], tb=True, epi=dsigmoid, epi_args=(lg1,), out_dtype=BF16, name="tmix_g2_dx")
    g["a_g1"] = _mm(xg, dlg1, ta=True, name="tmix_g1_dw")
    dxg = _mm(dlg1, w["a_g1"], tb=True, name="tmix_g1_dx")

    early_sums = ()
    if late_shards is not None:
        sums = dict(zip(_EARLY_GRADS_A, _pair_sums(parts_a, from_sibling_a, _EARLY_GRADS_A)))
        sums.update(zip(_EARLY_GRADS_B, _chip_sums([_device_major(name, g[name]) for name in _EARLY_GRADS_B], _EARLY_GRADS_B, "early")))
        early_sums = [sums[name] for name in _EARLY_GRADS]
    (dr, dk, dv, dwl, dal), param_grads, early_reduced = _wkv_bwd(r, k, v, wl, al, wkv_params, states, dy2, early_sums)
    for name, pg in zip(("a_w0", "a_a0", "a_k_k", "a_k_a", "a_ln_x_w", "a_ln_x_b", "a_r_k"), param_grads):
        g[name] = pg.reshape(1, D_MODEL)

    g["a_w_r"] = _mm(xr, dr, ta=True, name="tmix_r_dw")
    g["a_w_k"] = _mm(xk, dk, ta=True, name="tmix_k_dw")
    g["a_w_v"] = _mm(xv, dv, ta=True, name="tmix_v_dw")
    dxr = _mm(dr, w["a_w_r"], tb=True, name="tmix_r_dx")
    dxk = _mm(dk, w["a_w_k"], tb=True, name="tmix_k_dx")
    dxv = _mm(dv, w["a_w_v"], tb=True, name="tmix_v_dx")
    g["a_w2"] = _mm(lw1, dwl, ta=True, a_pro=jnp.tanh, name="tmix_w2_dw")

    def dtanh(rr, z):
        th = jnp.tanh(z)
        return rr * (1.0 - th * th)

    dlw1 = _mm(dwl, w["a_w2"], tb=True, epi=dtanh, epi_args=(lw1,), out_dtype=BF16, name="tmix_w2_dx")
    g["a_w1"] = _mm(xw, dlw1, ta=True, name="tmix_w1_dw")
    dxw = _mm(dlw1, w["a_w1"], tb=True, name="tmix_w1_dx")
    g["a_a2"] = _mm(la1, dal, ta=True, name="tmix_a2_dw")
    dla1 = _mm(dal, w["a_a2"], tb=True, out_dtype=BF16, name="tmix_a2_dx")
    g["a_a1"] = _mm(xa, dla1, ta=True, name="tmix_a1_dw")
    dxa = _mm(dla1, w["a_a1"], tb=True, name="tmix_a1_dx")

    lerp_bwd = _vjp_of(_time_shift_lerps, 9, tuple(range(9)))

    def lerp_bwd_rows(x_, xs_, d0, d1, d2, d3, d4, d5, gain, *mx):
        return lerp_bwd(x_, xs_, gain, *mx, d0, d1, d2, d3, d4, d5)

    outs = _rowwise(lerp_bwd_rows, [x, xs, dxr, dxk, dxv, dxw, dxa, dxg], [w["a_norm"]] + mix, [row(), row()], [vec()] * 7,
                    tm=128, name="tmix_lerp_bwd")
    dx_a, dxs, g["a_norm"] = outs[0], outs[1], outs[2]
    g["a_mix"] = jnp.concatenate(outs[3:9], axis=0)
    (grad_x,) = _rowwise(lambda a, b, c: (a + b + c,), [dh1, dx_a, _shift_up(dxs)], [], [row()], name="grad_x_sum")
    return loss, grad_x, g, (early_reduced if late_shards is not None else None)


def _device_major(name, grad):
    return grad if name.startswith("mlp_w_up") else grad.reshape((N_DEV, grad.shape[0] // N_DEV, grad.shape[1]))


_ANY = pl.BlockSpec(memory_space=pl.ANY)
_MESH_ID = pl.DeviceIdType.MESH
N_PEERS = N_DEV - 1


def _linear(pos):
    return 4 * pos[0] + 2 * pos[1] + pos[2]


def _all_gather(shards, name):
    n = len(shards)

    def body(*refs):
        start, relay, finish = _gather_plan(refs[:n], refs[n:2 * n], *refs[2 * n:])
        start()
        relay()
        finish()

    return pl.pallas_call(
        body, out_shape=[_sds((N_DEV,) + s.shape, s.dtype) for s in shards], in_specs=[_ANY] * n, out_specs=[_ANY] * n,
        scratch_shapes=_gather_semaphores(n), name=name,
    )(*shards)


GATHER_COPIES = 8


def _gather_semaphores(n):
    return [pltpu.SemaphoreType.DMA((n * GATHER_COPIES,)), pltpu.SemaphoreType.DMA((n * GATHER_COPIES,)),
            pltpu.SemaphoreType.DMA((n,))]


def _gather_plan(ins, outs, send_sems, recv_sems, local_sems):
    n = len(ins)
    x, y, c = lax.axis_index("x"), lax.axis_index("y"), lax.axis_index("c")
    me, sibling = (x, y, c), (x, y, 1 - c)
    x_nbr, y_nbr, diag = (1 - x, y, c), (x, 1 - y, c), (1 - x, 1 - y, c)
    other = lambda pos: (pos[0], pos[1], 1 - c)

    def halves(a):
        rows = ins[a].shape[0]
        if rows % 32:
            return (0, rows), None
        return (0, rows // 2), (rows // 2, rows // 2)

    def copy(a, k, block, to, src=None, rows=None):
        dst = outs[a].at[_linear(block)]
        if rows is not None:
            dst = dst.at[pl.ds(rows[0], rows[1])]
        return pltpu.make_async_remote_copy(
            src_ref=dst if src is None else src, dst_ref=dst, send_sem=send_sems.at[a * GATHER_COPIES + k],
            recv_sem=recv_sems.at[a * GATHER_COPIES + k], device_id=to, device_id_type=_MESH_ID)

    def own_copies():
        mine = [pltpu.make_async_copy(ins[a], outs[a].at[_linear(me)], local_sems.at[a]) for a in range(n)]
        sent = []
        for a in range(n):
            sent += [copy(a, 0, me, sibling, src=ins[a]), copy(a, 1, me, x_nbr, src=ins[a]), copy(a, 2, me, y_nbr, src=ins[a])]
        return mine, sent

    def relayed_copies():
        sent = []
        for a in range(n):
            first, second = halves(a)
            sent += [copy(a, 3, x_nbr, y_nbr, rows=first), copy(a, 5, x_nbr, sibling), copy(a, 6, y_nbr, sibling)]
            if second is not None:
                sent.append(copy(a, 4, y_nbr, x_nbr, rows=second))
        return sent

    def start():
        mine, sent = own_copies()
        for cp in mine + sent:
            cp.start()

    def relay():
        for a in range(n):
            first, second = halves(a)
            copy(a, 1, x_nbr, me).wait_recv()
            copy(a, 3, x_nbr, y_nbr, rows=first).start()
            copy(a, 5, x_nbr, sibling).start()
        for a in range(n):
            first, second = halves(a)
            copy(a, 2, y_nbr, me).wait_recv()
            if second is not None:
                copy(a, 4, y_nbr, x_nbr, rows=second).start()
            copy(a, 6, y_nbr, sibling).start()

    def finish():
        mine, sent = own_copies()
        sent += relayed_copies()
        for a in range(n):
            first, second = halves(a)
            copy(a, 3, diag, me, rows=first).wait_recv()
            if second is not None:
                copy(a, 4, diag, me, rows=second).wait_recv()
            last = copy(a, 7, diag, sibling)
            last.start()
            sent.append(last)
        for a in range(n):
            copy(a, 0, other(me), me).wait_recv()
            copy(a, 5, other(x_nbr), me).wait_recv()
            copy(a, 6, other(y_nbr), me).wait_recv()
            copy(a, 7, other(diag), me).wait_recv()
        for cp in sent:
            cp.wait_send()
        for cp in mine:
            cp.wait()

    return start, relay, finish


N_CHIPS = 4


def _exchange_with_sibling(parts, name):
    n = len(parts)

    def body(*refs):
        start, finish = _sibling_exchange_plan(refs[:n], refs[n:2 * n], *refs[2 * n:])
        start()
        finish()

    return pl.pallas_call(
        body, out_shape=_sibling_exchange_shapes(parts), in_specs=[_ANY] * n, out_specs=[_ANY] * n,
        scratch_shapes=_sibling_exchange_semaphores(n), name=name,
    )(*parts)


def _sibling_exchange_shapes(parts):
    return [_sds((N_CHIPS,) + p.shape[1:], p.dtype) for p in parts]


def _sibling_exchange_semaphores(n):
    return [pltpu.SemaphoreType.DMA((n * N_CHIPS,)), pltpu.SemaphoreType.DMA((n * N_CHIPS,))]


def _sibling_exchange_plan(ins, outs, send_sems, recv_sems):
    n = len(ins)
    x, y, c = lax.axis_index("x"), lax.axis_index("y"), lax.axis_index("c")

    def all_copies():
        return [pltpu.make_async_remote_copy(
            src_ref=ins[a].at[2 * q + (1 - c)], dst_ref=outs[a].at[q], send_sem=send_sems.at[a * N_CHIPS + q],
            recv_sem=recv_sems.at[a * N_CHIPS + q], device_id=(x, y, 1 - c), device_id_type=_MESH_ID)
            for a in range(n) for q in range(N_CHIPS)]

    def start():
        for cp in all_copies():
            cp.start()

    def finish():
        for cp in all_copies():
            cp.wait()

    return start, finish


def _pair_sum(part, recv, core, out_dtype, name):
    _, r, cdim = recv.shape
    tr = max(8, min(r, (1 << 18) // cdim))
    assert r % tr == 0, (name, r, tr)

    def body(core_ref, p_ref, r_ref, o_ref):
        o_ref[...] = (p_ref[...] + r_ref[...]).astype(o_ref.dtype)

    grid_spec = pltpu.PrefetchScalarGridSpec(
        num_scalar_prefetch=1, grid=(N_CHIPS, r // tr),
        in_specs=[pl.BlockSpec((None, None, tr, cdim), lambda q, i, core_ref: (q, core_ref[0], i, 0)),
                  pl.BlockSpec((None, tr, cdim), lambda q, i, core_ref: (q, i, 0))],
        out_specs=pl.BlockSpec((None, tr, cdim), lambda q, i, core_ref: (q, i, 0)))
    return pl.pallas_call(
        body, grid_spec=grid_spec, out_shape=_sds((N_CHIPS, r, cdim), out_dtype),
        compiler_params=_params(("parallel", "parallel")), name=name,
    )(core, part.reshape(N_CHIPS, 2, r, cdim), recv)


def _exchange_between_chips(parts, name):
    n = len(parts)

    def body(*refs):
        start, finish = _chip_exchange_plan(refs[:n], refs[n:2 * n], *refs[2 * n:])
        start()
        finish()

    return pl.pallas_call(
        body, out_shape=[_sds(p.shape, p.dtype) for p in parts], in_specs=[_ANY] * n, out_specs=[_ANY] * n,
        scratch_shapes=_chip_exchange_semaphores(n), name=name,
    )(*parts)


def _chip_exchange_semaphores(n):
    n_other = N_CHIPS - 1
    return [pltpu.SemaphoreType.DMA((n * n_other,)), pltpu.SemaphoreType.DMA((n * n_other,)), pltpu.SemaphoreType.DMA((n,))]


def _chip_exchange_plan(ins, outs, send_sems, recv_sems, local_sems):
    n = len(ins)
    n_other = N_CHIPS - 1
    x, y, c = lax.axis_index("x"), lax.axis_index("y"), lax.axis_index("c")
    my_chip = 2 * x + y

    def all_copies():
        mine = [pltpu.make_async_copy(ins[a].at[my_chip], outs[a].at[my_chip], local_sems.at[a]) for a in range(n)]
        remote = []
        for j, (fx, fy) in enumerate([(1, 0), (0, 1), (1, 1)]):
            px, py = (1 - x if fx else x), (1 - y if fy else y)
            for a in range(n):
                remote.append(pltpu.make_async_remote_copy(
                    src_ref=ins[a].at[2 * px + py], dst_ref=outs[a].at[my_chip], send_sem=send_sems.at[a * n_other + j],
                    recv_sem=recv_sems.at[a * n_other + j], device_id=(px, py, c), device_id_type=_MESH_ID))
        return mine, remote

    def start():
        mine, remote = all_copies()
        for cp in mine + remote:
            cp.start()

    def finish():
        mine, remote = all_copies()
        for cp in remote + mine:
            cp.wait()

    return start, finish


def _chip_sums(parts, names, tag):
    return _pair_sums(parts, _exchange_with_sibling(parts, name="scatter_grads_sibling_" + tag), names)


def _pair_sums(parts, from_sibling, names):
    core = lax.axis_index("c").astype(jnp.int32).reshape(1)
    return [_pair_sum(p, r, core, F32 if nm.startswith("pack") else BF16, name="pair_sum_" + nm)
            for p, r, nm in zip(parts, from_sibling, names)]


def _reduce_scatter(parts, names, tag):
    return _exchange_between_chips(_chip_sums(parts, names, tag), name="scatter_grads_chips_" + tag)


def _adamw(w, m, v, slots, name):
    r, c = w.shape
    ns = slots.shape[0]
    tr = max(8, min(r, (1 << 18) // c))
    assert r % tr == 0, (name, r, tr)

    def body(w_ref, m_ref, v_ref, g_ref, g_out, d_out, m_out, v_out):
        g = g_ref[0].astype(F32)
        for s in range(1, ns):
            g = g + g_ref[s].astype(F32)
        m_new = ADAM_B1 * m_ref[...] + (1.0 - ADAM_B1) * g
        v_new = ADAM_B2 * v_ref[...] + (1.0 - ADAM_B2) * jnp.square(g)
        m_hat = m_new / (1.0 - ADAM_B1 ** ADAM_STEP)
        v_hat = v_new / (1.0 - ADAM_B2 ** ADAM_STEP)
        d_out[...] = -ADAM_LR * (m_hat / (jnp.sqrt(v_hat) + ADAM_EPS) + ADAM_WD * w_ref[...])
        g_out[...], m_out[...], v_out[...] = g, m_new, v_new

    spec = pl.BlockSpec((tr, c), lambda i: (i, 0))
    return pl.pallas_call(
        body, grid=(r // tr,), in_specs=[spec, spec, spec, pl.BlockSpec((ns, tr, c), lambda i: (0, i, 0))],
        out_specs=[spec] * 4, out_shape=[_sds((r, c))] * 4, compiler_params=_params(("parallel",)), name=name,
    )(w, m, v, slots)


_COL_VECTORS = ("a_norm", "a_mix", "a_w0", "a_a0", "a_k_k", "a_k_a", "a_ln_x_w", "a_ln_x_b")
_COL_VEC_ROWS = 16
_COL_ROWS = _COL_VEC_ROWS + 2 * LORA_PAD + 256
_ROW_COLS = 2 * LORA_PAD + 256 + 512
_REPL_ROWS = 8


def _pad_to(a, size, axis):
    widths = [(0, 0)] * a.ndim
    widths[axis] = (0, size - a.shape[axis])
    return jnp.pad(a, widths)


def _pack_cols(p):
    width = p["a_norm"].shape[-1]
    vecs = jnp.concatenate([p[n].reshape(-1, width) for n in _COL_VECTORS], axis=0)
    return jnp.concatenate([_pad_to(vecs, _COL_VEC_ROWS, 0), _pad_to(p["a_w2"].reshape(-1, width), LORA_PAD, 0),
                            _pad_to(p["a_a2"].reshape(-1, width), LORA_PAD, 0), p["a_g2"].reshape(-1, width)], axis=0)


def _unpack_cols(a, lead):
    width = a.shape[-1]
    out, row = {}, 0
    for n in _COL_VECTORS:
        k = 6 if n == "a_mix" else 1
        out[n] = a[row:row + k].reshape(lead + ((6, width) if n == "a_mix" else (width,)))
        row += k
    base = _COL_VEC_ROWS
    out["a_w2"] = a[base:base + 96].reshape(lead + (96, width))
    out["a_a2"] = a[base + LORA_PAD:base + LORA_PAD + 96].reshape(lead + (96, width))
    out["a_g2"] = a[base + 2 * LORA_PAD:].reshape(lead + (256, width))
    return out


def _pack_rows(p):
    rows = p["w_kv"].shape[0]
    return jnp.concatenate([_pad_to(p["a_w1"].reshape(rows, -1), LORA_PAD, 1), _pad_to(p["a_a1"].reshape(rows, -1), LORA_PAD, 1),
                            p["a_g1"].reshape(rows, -1), p["w_kv"]], axis=1)


def _unpack_rows(a, lead):
    rows = a.shape[0]
    return {"a_w1": a[:, :96].reshape(lead + (rows, 96)), "a_a1": a[:, LORA_PAD:LORA_PAD + 96].reshape(lead + (rows, 96)),
            "a_g1": a[:, 2 * LORA_PAD:2 * LORA_PAD + 256].reshape(lead + (rows, 256)), "w_kv": a[:, 2 * LORA_PAD + 256:]}


def _pack_repl(p):
    row = lambda a: _pad_to(a.reshape(1, -1), D_MODEL, 1)
    return jnp.concatenate([p["mlp_norm"].reshape(2, D_MODEL), row(p["kv_norm"]), row(p["b_norm"]), row(p["a_r_k"]),
                            row(p["k_norm"]), row(p["b_q_norm"]), row(p["b_sinks"])], axis=0)


def _unpack_repl(a):
    return {"mlp_norm": a[0:2], "kv_norm": a[2], "b_norm": a[3:4], "a_r_k": a[4].reshape(1, N_HEADS, HEAD_DIM),
            "k_norm": a[5, :HEAD_DIM], "b_q_norm": a[6:7, :HEAD_DIM], "b_sinks": a[7:8, :N_HEADS]}


_WEIGHTS = ("a_norm", "a_mix", "a_w_rkv", "a_w0", "a_w1", "a_w2", "a_a0", "a_a1", "a_a2", "a_g1", "a_g2", "a_k_k", "a_k_a",
            "a_r_k", "a_ln_x_w", "a_ln_x_b", "a_w_out", "mlp_norm", "mlp_w_up", "mlp_w_down", "kv_norm", "w_kv", "k_norm",
            "b_norm", "b_w_q", "b_q_norm", "b_sinks", "b_w_o")


def _big_shards(p):
    return [p["a_w_rkv"][0, 0], p["a_w_rkv"][0, 1], p["a_w_rkv"][0, 2], p["a_w_out"][0], p["mlp_w_up"][0], p["mlp_w_up"][1],
            p["mlp_w_down"][0], p["mlp_w_down"][1], p["b_w_q"][0], p["b_w_o"][0]]


_BIG_NAMES = ("a_w_r", "a_w_k", "a_w_v", "a_w_out", "mlp_w_up0", "mlp_w_up1", "mlp_w_down0", "mlp_w_down1", "b_w_q", "b_w_o")


def kernel(x, a_norm, a_mix, a_w_rkv, a_w0, a_w1, a_w2, a_a0, a_a1, a_a2, a_g1, a_g2, a_k_k, a_k_a, a_r_k, a_ln_x_w,
           a_ln_x_b, a_w_out, mlp_norm, mlp_w_up, mlp_w_down, kv_norm, w_kv, k_norm, b_norm, b_w_q, b_q_norm, b_sinks,
           b_w_o, loss_target, m_a_norm, m_a_mix, m_a_w_rkv, m_a_w0, m_a_w1, m_a_w2, m_a_a0, m_a_a1, m_a_a2, m_a_g1,
           m_a_g2, m_a_k_k, m_a_k_a, m_a_r_k, m_a_ln_x_w, m_a_ln_x_b, m_a_w_out, m_mlp_norm, m_mlp_w_up, m_mlp_w_down,
           m_kv_norm, m_w_kv, m_k_norm, m_b_norm, m_b_w_q, m_b_q_norm, m_b_sinks, m_b_w_o, v_a_norm, v_a_mix, v_a_w_rkv,
           v_a_w0, v_a_w1, v_a_w2, v_a_a0, v_a_a1, v_a_a2, v_a_g1, v_a_g2, v_a_k_k, v_a_k_a, v_a_r_k, v_a_ln_x_w,
           v_a_ln_x_b, v_a_w_out, v_mlp_norm, v_mlp_w_up, v_mlp_w_down, v_kv_norm, v_w_kv, v_k_norm, v_b_norm, v_b_w_q,
           v_b_q_norm, v_b_sinks, v_b_w_o):
    given = locals()
    wts = {n: given[n] for n in _WEIGHTS}
    mom = {n: given["m_" + n] for n in _WEIGHTS}
    var = {n: given["v_" + n] for n in _WEIGHTS}

    cols_w, rows_w, repl_w = _pack_cols(wts), _pack_rows(wts), _pack_repl(wts)
    big_w = _big_shards(wts)
    big_bf16 = dict(zip(_BIG_NAMES, [b.astype(BF16) for b in big_w]))
    first_names = [k for k in _BIG_NAMES if k not in _LATE_WEIGHTS]
    gathered = _all_gather([cols_w, rows_w] + [big_bf16[k] for k in first_names], name="gather_weights")
    full_cols = gathered[0].transpose(1, 0, 2).reshape(_COL_ROWS, D_MODEL)
    full_rows = gathered[1].reshape(D_MODEL, _ROW_COLS)
    w = {}
    w.update({k: v.reshape(v.shape[1:]) for k, v in _unpack_cols(full_cols, (1,)).items()})
    for k in ("a_norm", "a_w0", "a_a0", "a_k_k", "a_k_a", "a_ln_x_w", "a_ln_x_b"):
        w[k] = w[k].reshape(1, D_MODEL)
    for k in ("a_w2", "a_a2"):
        w[k] = _pad_to(w[k], LORA_PAD, 0)
    rows_full = _unpack_rows(full_rows, ())
    w["a_w1"], w["a_a1"] = _pad_to(rows_full["a_w1"], LORA_PAD, 1), _pad_to(rows_full["a_a1"], LORA_PAD, 1)
    w["a_g1"], w["w_kv"] = rows_full["a_g1"], rows_full["w_kv"]
    for k, arr in zip(first_names, gathered[2:]):
        w[k] = arr.reshape(N_DEV * arr.shape[1], arr.shape[2])
    w["mlp_norm0"], w["mlp_norm1"] = mlp_norm[0:1], mlp_norm[1:2]
    w["kv_norm"], w["k_norm"] = kv_norm.reshape(1, D_MODEL), k_norm.reshape(1, HEAD_DIM)
    w["b_norm"], w["b_q_norm"], w["b_sinks"], w["a_r_k"] = b_norm, b_q_norm, b_sinks, a_r_k.reshape(1, D_MODEL)

    loss_local, grad_x, g, early_reduced = _local_step(x[0], loss_target[0], w, [big_bf16[k] for k in _LATE_WEIGHTS])
    loss = lax.psum(loss_local[0, 0], MESH_AXES)

    g_lead = {k: g[k][None] for k in ("a_norm", "a_mix", "a_w0", "a_a0", "a_k_k", "a_k_a", "a_ln_x_w", "a_ln_x_b", "a_g2")}
    g_lead["a_w2"], g_lead["a_a2"] = g["a_w2"][None, :96], g["a_a2"][None, :96]
    g_cols = _pack_cols(g_lead).reshape(_COL_ROWS, N_DEV, D_MODEL // N_DEV).transpose(1, 0, 2)
    g_rows = _pack_rows({"a_w1": g["a_w1"][:, :96], "a_a1": g["a_a1"][:, :96], "a_g1": g["a_g1"], "w_kv": g["w_kv"]})
    g_rows = g_rows.reshape(N_DEV, D_MODEL // N_DEV, _ROW_COLS)
    late_names = tuple(k for k in _BIG_NAMES if k not in _EARLY_GRADS)
    late_reduced = _reduce_scatter([g_cols, g_rows] + [_device_major(k, g[k]) for k in late_names],
                                   ("pack_cols", "pack_rows") + late_names, "late")
    big_reduced = dict(zip(late_names, late_reduced[2:]))
    big_reduced.update(zip(_EARLY_GRADS, early_reduced))
    reduced = list(late_reduced[:2]) + [big_reduced[k] for k in _BIG_NAMES]
    g_repl = _pack_repl({"mlp_norm": jnp.concatenate([g["mlp_norm0"], g["mlp_norm1"]], axis=0), "kv_norm": g["kv_norm"],
                         "b_norm": g["b_norm"], "a_r_k": g["a_r_k"], "k_norm": g["k_norm"], "b_q_norm": g["b_q_norm"],
                         "b_sinks": g["b_sinks"]})
    (repl_slots,) = _all_gather([g_repl], name="gather_replicated_grads")

    res = {}
    cols4 = _adamw(cols_w, _pack_cols(mom), _pack_cols(var), reduced[0], name="adamw_cols")
    rows4 = _adamw(rows_w, _pack_rows(mom), _pack_rows(var), reduced[1], name="adamw_rows")
    repl4 = _adamw(repl_w, _pack_repl(mom), _pack_repl(var), repl_slots, name="adamw_replicated")
    for unpacked in ([_unpack_cols(a, (1,)) for a in cols4], [_unpack_rows(a, (1,)) for a in rows4], [_unpack_repl(a) for a in repl4]):
        for k in unpacked[0]:
            res[k] = tuple(u[k] for u in unpacked)
    big4 = [_adamw(bw, bm, bv, slots, name="adamw_" + k)
            for k, bw, bm, bv, slots in zip(_BIG_NAMES, big_w, _big_shards(mom), _big_shards(var), reduced[2:])]
    res["a_w_rkv"] = tuple(jnp.stack([big4[0][i], big4[1][i], big4[2][i]])[None] for i in range(4))
    res["a_w_out"] = tuple(a[None] for a in big4[3])
    res["mlp_w_up"] = tuple(jnp.stack([big4[4][i], big4[5][i]]) for i in range(4))
    res["mlp_w_down"] = tuple(jnp.stack([big4[6][i], big4[7][i]]) for i in range(4))
    res["b_w_q"] = tuple(a[None] for a in big4[8])
    res["b_w_o"] = tuple(a[None] for a in big4[9])
    res["w_kv"] = tuple(a.reshape(w_kv.shape) for a in res["w_kv"])

    outs = [loss, grad_x[None]]
    for i in range(4):
        outs += [res[n][i].reshape(given[n].shape) for n in _WEIGHTS]
    return tuple(outs)
```

```python
import functools
import math

import jax
import jax.numpy as jnp
from jax import lax
from jax.experimental import pallas as pl
from jax.experimental.pallas import tpu as pltpu

F32 = jnp.float32
BF16 = jnp.bfloat16

D_MODEL = 2048
N_HEADS = 32
HEAD_DIM = 64
N_KV_HEADS = 4
Q_PER_KV = 8
ATT_BLOCK = 128
WKV_CHUNK = 64
LORA_PAD = 128
D_FF = 8192
N_DEV = 8
RMS_EPS = 1e-6
GN_EPS = 64e-5
L2_EPS = 1e-12
ROPE_THETA = 10000.0
ADAM_LR, ADAM_B1, ADAM_B2, ADAM_EPS, ADAM_WD, ADAM_STEP = 0.001, 0.9, 0.999, 1e-08, 0.01, 10
MASK_VALUE = -1e30
VMEM_LIMIT_BYTES = 56 * 1024 * 1024
MESH_AXES = ("x", "y", "c")
HI = lax.Precision.HIGHEST

_NN = (((1,), (0,)), ((), ()))
_NT = (((1,), (1,)), ((), ()))
_TN = (((0,), (0,)), ((), ()))
_BNN = (((2,), (1,)), ((0,), (0,)))
_BNT = (((2,), (2,)), ((0,), (0,)))
_BTN = (((1,), (1,)), ((0,), (0,)))


def _params(sem):
    return pltpu.CompilerParams(dimension_semantics=sem, vmem_limit_bytes=VMEM_LIMIT_BYTES)


def _split2(a):
    hi = a.astype(BF16)
    return hi, (a - hi.astype(F32)).astype(BF16)


def _dot3(a, b, dims):
    ah, al = _split2(a)
    bh, bl = _split2(b)
    d = lambda p, q: lax.dot_general(p, q, dims, preferred_element_type=F32)
    return d(ah, bh) + (d(al, bh) + d(ah, bl))


@functools.partial(jax.custom_vjp, nondiff_argnums=(2,))
def _hdot(a, b, dims=_NN):
    return _dot3(a, b, dims)


def _hdot_fwd(a, b, dims):
    return _dot3(a, b, dims), (a, b)


def _hdot_bwd(dims, res, g):
    a, b = res
    nn, nt, tn = (_NN, _NT, _TN) if dims in (_NN, _NT, _TN) else (_BNN, _BNT, _BTN)
    if dims == nn:
        return _dot3(g, b, nt), _dot3(a, g, tn)
    if dims == nt:
        return _dot3(g, b, nn), _dot3(g, a, tn)
    assert dims == tn
    return _dot3(b, g, nt), _dot3(a, g, nn)


_hdot.defvjp(_hdot_fwd, _hdot_bwd)


def _tri_parts(x):
    hi = x.astype(BF16)
    r1 = x - hi.astype(F32)
    mid = r1.astype(BF16)
    return hi, mid, (r1 - mid.astype(F32)).astype(BF16)


@jax.custom_vjp
def _mask_dot(mask, x):
    mb = mask.astype(BF16)
    p0, p1, p2 = _tri_parts(x)
    d = lambda p: lax.dot_general(mb, p, _BNN, preferred_element_type=F32)
    return d(p0) + (d(p1) + d(p2))


def _mask_dot_fwd(mask, x):
    return _mask_dot(mask, x), mask


def _mask_dot_bwd(mask, g):
    mb = mask.astype(BF16)
    p0, p1, p2 = _tri_parts(g)
    d = lambda p: lax.dot_general(mb, p, _BTN, preferred_element_type=F32)
    return jnp.zeros_like(mask), d(p0) + (d(p1) + d(p2))


_mask_dot.defvjp(_mask_dot_fwd, _mask_dot_bwd)


def _b16dot(a, b, dims):
    return lax.dot_general(a.astype(BF16), b.astype(BF16), dims, preferred_element_type=F32)


@jax.custom_vjp
def _bdot(a, b):
    return _b16dot(a, b, _NN)


def _bdot_fwd(a, b):
    return _b16dot(a, b, _NN), (a, b)


def _bdot_bwd(res, g):
    a, b = res
    return _b16dot(g, b, _NT), _b16dot(a, g, _TN)


_bdot.defvjp(_bdot_fwd, _bdot_bwd)


@jax.custom_vjp
def _bdot_nt(a, b):
    return _b16dot(a, b, _NT)


def _bdot_nt_fwd(a, b):
    return _b16dot(a, b, _NT), (a, b)


def _bdot_nt_bwd(res, g):
    a, b = res
    return _b16dot(g, b, _NN), _b16dot(g, a, _TN)


_bdot_nt.defvjp(_bdot_nt_fwd, _bdot_nt_bwd)


def _rms(x, gain):
    return x * lax.rsqrt(jnp.mean(x * x, axis=-1, keepdims=True) + RMS_EPS) * gain


def _vjp_of(f, n_in, diff):
    def g(*args):
        ins, cts = args[:n_in], args[n_in:]

        def fd(*d):
            full = list(ins)
            for pos, i in enumerate(diff):
                full[i] = d[pos]
            return f(*full)

        _, pull = jax.vjp(fd, *[ins[i] for i in diff])
        return pull(tuple(cts))
    return g


def _mm(a, b, *, name, ta=False, tb=False, a_pro=None, epi=None, epi_args=(), out_dtype=F32,
        tm=1024, tn=1024, tk=2048, dims=None, b_spec=None, o_spec=None, o_shape=None):
    if dims is None:
        m, k = (a.shape[1], a.shape[0]) if ta else a.shape
        n = b.shape[0] if tb else b.shape[1]
    else:
        m, n, k = dims
    tm, tn, tk = min(tm, m), min(tn, n), min(tk, k)
    assert m % tm == 0 and n % tn == 0 and k % tk == 0, (name, m, n, k, tm, tn, tk)
    nk = k // tk
    ne = len(epi_args)
    cdims = (((0 if ta else 1,), (1 if tb else 0,)), ((), ()))

    def body(a_ref, b_ref, *rest):
        e_refs, o_ref, acc = rest[:ne], rest[ne], rest[ne + 1]
        kk = pl.program_id(2)

        @pl.when(kk == 0)
        def _():
            acc[...] = jnp.zeros_like(acc)

        av = a_ref[...]
        if a_pro is not None:
            av = a_pro(av.astype(F32))
        acc[...] += lax.dot_general(av.astype(BF16), b_ref[...].astype(BF16), cdims, preferred_element_type=F32)

        @pl.when(kk == nk - 1)
        def _():
            r = acc[...]
            if epi is not None:
                r = epi(r, *[e[...] for e in e_refs])
            o_ref[...] = r.astype(o_ref.dtype)

    a_spec = pl.BlockSpec((tk, tm), lambda i, j, q: (q, i)) if ta else pl.BlockSpec((tm, tk), lambda i, j, q: (i, q))
    if b_spec is None:
        b_spec = pl.BlockSpec((tn, tk), lambda i, j, q: (j, q)) if tb else pl.BlockSpec((tk, tn), lambda i, j, q: (q, j))
    if o_spec is None:
        o_spec = pl.BlockSpec((tm, tn), lambda i, j, q: (i, j))
        o_shape = (m, n)
    e_specs = [pl.BlockSpec((tm, tn), lambda i, j, q: (i, j)) for _ in epi_args]
    return pl.pallas_call(
        body, grid=(m // tm, n // tn, nk), in_specs=[a_spec, b_spec] + e_specs, out_specs=o_spec,
        out_shape=jax.ShapeDtypeStruct(o_shape, out_dtype), scratch_shapes=[pltpu.VMEM((tm, tn), F32)],
        compiler_params=_params(("parallel", "parallel", "arbitrary")), name=name,
    )(a, b, *epi_args)


def _rowwise(fn, rows, params, out_rows, out_params=(), *, tm=256, name):
    t = rows[0].shape[0]
    tm = min(tm, t)
    assert t % tm == 0
    nr, npar, nor, nop = len(rows), len(params), len(out_rows), len(out_params)

    def body(*refs):
        r, p = refs[:nr], refs[nr:nr + npar]
        o, op = refs[nr + npar:nr + npar + nor], refs[nr + npar + nor:]
        outs = fn(*[x[...] for x in r], *[x[...] for x in p])
        for ref, val in zip(o, outs[:nor]):
            ref[...] = val.astype(ref.dtype)
        if nop:
            @pl.when(pl.program_id(0) == 0)
            def _():
                for ref in op:
                    ref[...] = jnp.zeros_like(ref)

            for ref, val in zip(op, outs[nor:]):
                ref[...] += val.astype(F32)

    in_specs = [pl.BlockSpec((tm, x.shape[1]), lambda i: (i, 0)) for x in rows]
    in_specs += [pl.BlockSpec(p.shape, lambda i: (0, 0)) for p in params]
    out_specs = [pl.BlockSpec((tm, s.shape[1]), lambda i: (i, 0)) for s in out_rows]
    out_specs += [pl.BlockSpec(s.shape, lambda i: (0, 0)) for s in out_params]
    return pl.pallas_call(
        body, grid=(t // tm,), in_specs=in_specs, out_specs=out_specs, out_shape=list(out_rows) + list(out_params),
        compiler_params=_params(("arbitrary",)), name=name,
    )(*rows, *params)


def _sds(shape, dtype=F32):
    return jax.ShapeDtypeStruct(tuple(shape), dtype)


def _doubling_powers(l):
    powers = [l]
    for _ in range(int(math.log2(l.shape[-1])) - 1):
        powers.append(_dot3(powers[-1], powers[-1], _BNN))
    return powers


@jax.custom_vjp
def _unit_lower_solve(l, z):
    u = z
    for p in _doubling_powers(l):
        u = u + _dot3(p, u, _BNN)
    return u


def _unit_lower_solve_fwd(l, z):
    powers = _doubling_powers(l)
    u = z
    for p in powers:
        u = u + _dot3(p, u, _BNN)
    return u, (powers, u)


def _unit_lower_solve_bwd(res, du):
    powers, u = res
    g = du
    for p in powers:
        g = g + _dot3(p, g, _BTN)
    return _dot3(g, u, _BNT), g


_unit_lower_solve.defvjp(_unit_lower_solve_fwd, _unit_lower_solve_bwd)


def _wkv_chunk(s0, r, lw, k, v, a, b):
    nb, c, _ = r.shape
    ti = lax.broadcasted_iota(jnp.int32, (nb, c, c), 1)
    si = lax.broadcasted_iota(jnp.int32, (nb, c, c), 2)
    incl, strict = si <= ti, si < ti
    cum = _mask_dot(incl.astype(F32), lw)
    tot = jnp.sum(lw, axis=1, keepdims=True)
    rcum = tot - cum
    w_inv = jnp.exp(-cum)
    at, rt, bt, kt = a * jnp.exp(cum - lw), r * jnp.exp(cum), b * w_inv, k * w_inv
    l_ab = jnp.where(strict, _hdot(at, bt, _BNT), 0.0)
    l_ak = jnp.where(strict, _hdot(at, kt, _BNT), 0.0)
    t_rb = jnp.where(incl, _hdot(rt, bt, _BNT), 0.0)
    t_rk = jnp.where(incl, _hdot(rt, kt, _BNT), 0.0)
    u = _unit_lower_solve(l_ab, _hdot(at, s0, _BNT) + _hdot(l_ak, v, _BNN))
    y = _hdot(rt, s0, _BNT) + _hdot(t_rb, u, _BNN) + _hdot(t_rk, v, _BNN)
    e = jnp.exp(rcum)
    s1 = s0 * jnp.exp(tot) + _hdot(u, b * e, _BTN) + _hdot(v, k * e, _BTN)
    return y, s1


WKV_HEADS_PER_STEP = 16


def _first_and_last_step(grid):
    i, j = pl.program_id(0), pl.program_id(1)
    return jnp.logical_and(i == 0, j == 0), jnp.logical_and(i == grid[0] - 1, j == grid[1] - 1)


N_WKV_PARAMS = 7


def _tmix_chunk(s0, r, k, v, wl, al, w0, a0, k_k, k_a, ln_w, ln_b, r_k):
    lw, kmod, a, b = _wkv_prep(k, wl, al, w0, a0, k_k, k_a)
    y, s1 = _wkv_chunk(s0, r, lw, kmod, v, a, b)
    (y2,) = _wkv_post(y, r, kmod, v, ln_w, ln_b, r_k)
    return y2, s1


def _split_heads(x, nh):
    return x.reshape(x.shape[0], nh, HEAD_DIM).transpose(1, 0, 2)


def _merge_heads(xh):
    return jnp.concatenate([xh[h] for h in range(xh.shape[0])], axis=1)


def _wkv_fwd(r, k, v, wl, al, params, shards=()):
    t = r.shape[0]
    nh, n = N_HEADS, HEAD_DIM
    nc = t // WKV_CHUNK
    hb = WKV_HEADS_PER_STEP
    grid = (nh // hb, nc)
    ns = len(shards)
    n_in = 5 + N_WKV_PARAMS

    def body(*refs):
        y_ref, s_ref = refs[n_in + ns:n_in + ns + 2]
        state = refs[n_in + 2 * ns + 2]
        if ns:
            start, relay, finish = _gather_plan(refs[n_in:n_in + ns], refs[n_in + ns + 2:n_in + 2 * ns + 2], *refs[n_in + 2 * ns + 3:])
            first, last = _first_and_last_step(grid)
            pl.when(first)(start)
            relay_step = (5 * grid[0] * nc) // 8
            pl.when(jnp.logical_and(pl.program_id(0) == relay_step // nc, pl.program_id(1) == relay_step % nc))(relay)

        @pl.when(pl.program_id(1) == 0)
        def _():
            state[...] = jnp.zeros_like(state)

        s0 = state[...]
        s_ref[:, 0] = s0
        rows = [_split_heads(ref[...], hb) for ref in refs[:5]]
        y, s1 = _tmix_chunk(s0, *rows, *[ref[...] for ref in refs[5:n_in]])
        y_ref[...] = _merge_heads(y)
        state[...] = s1
        if ns:
            pl.when(last)(finish)

    blk = pl.BlockSpec((WKV_CHUNK, hb * n), lambda h, c: (c, h))
    pblk = pl.BlockSpec((hb, 1, n), lambda h, c: (h, 0, 0))
    sblk = pl.BlockSpec((hb, 1, n, n), lambda h, c: (h, c, 0, 0))
    outs = pl.pallas_call(
        body, grid=grid, in_specs=[blk] * 5 + [pblk] * N_WKV_PARAMS + [_ANY] * ns, out_specs=[blk, sblk] + [_ANY] * ns,
        out_shape=[_sds((t, nh * n)), _sds((nh, nc, n, n))] + [_sds((N_DEV,) + s.shape, s.dtype) for s in shards],
        scratch_shapes=[pltpu.VMEM((hb, n, n), F32)] + (_gather_semaphores(ns) if ns else []),
        compiler_params=_params(("arbitrary", "arbitrary")), name="wkv_fwd",
    )(r, k, v, wl, al, *params, *shards)
    return outs[0], outs[1], list(outs[2:])


def _wkv_bwd(r, k, v, wl, al, params, states, dy, chip_sums=()):
    t = r.shape[0]
    nh, n = N_HEADS, HEAD_DIM
    nc = t // WKV_CHUNK
    hb = WKV_HEADS_PER_STEP
    grid = (nh // hb, nc)
    ns = len(chip_sums)
    n_in = 5 + N_WKV_PARAMS
    n_out = 5 + N_WKV_PARAMS

    def body(*refs):
        s_ref, dy_ref = refs[n_in:n_in + 2]
        out_refs = refs[n_in + 2 + ns:n_in + 2 + ns + n_out]
        dstate = refs[n_in + 2 + 2 * ns + n_out]
        if ns:
            start, finish = _chip_exchange_plan(refs[n_in + 2:n_in + 2 + ns], refs[n_in + 2 + ns + n_out:n_in + 2 + 2 * ns + n_out],
                                                *refs[n_in + 3 + 2 * ns + n_out:])
            first, last = _first_and_last_step(grid)
            pl.when(first)(start)

        @pl.when(pl.program_id(1) == 0)
        def _():
            dstate[...] = jnp.zeros_like(dstate)
            for ref in out_refs[5:]:
                ref[...] = jnp.zeros_like(ref)

        rows = [_split_heads(ref[...], hb) for ref in refs[:5]]
        _, pull = jax.vjp(_tmix_chunk, s_ref[:, 0], *rows, *[ref[...] for ref in refs[5:n_in]])
        grads = pull((_split_heads(dy_ref[...], hb), dstate[...]))
        dstate[...] = grads[0]
        for ref, val in zip(out_refs[:5], grads[1:6]):
            ref[...] = _merge_heads(val).astype(ref.dtype)
        for ref, val in zip(out_refs[5:], grads[6:]):
            ref[...] += val
        if ns:
            pl.when(last)(finish)

    blk = pl.BlockSpec((WKV_CHUNK, hb * n), lambda h, c: (nc - 1 - c, h))
    pblk = pl.BlockSpec((hb, 1, n), lambda h, c: (h, 0, 0))
    sblk = pl.BlockSpec((hb, 1, n, n), lambda h, c: (h, nc - 1 - c, 0, 0))
    outs = pl.pallas_call(
        body, grid=grid, in_specs=[blk] * 5 + [pblk] * N_WKV_PARAMS + [sblk, blk] + [_ANY] * ns,
        out_specs=[blk] * 5 + [pblk] * N_WKV_PARAMS + [_ANY] * ns,
        out_shape=[_sds((t, nh * n), BF16)] * 5 + [_sds((nh, 1, n))] * N_WKV_PARAMS + [_sds(p.shape, p.dtype) for p in chip_sums],
        scratch_shapes=[pltpu.VMEM((hb, n, n), F32)] + (_chip_exchange_semaphores(ns) if ns else []),
        compiler_params=_params(("arbitrary", "arbitrary")), name="wkv_bwd",
    )(r, k, v, wl, al, *params, states, dy, *chip_sums)
    return outs[:5], outs[5:n_out], list(outs[n_out:])


def _wkv_prep(k, wl, al, w0, a0, k_k, k_a):
    z = -(w0 + wl)
    softplus = jnp.maximum(z, 0.0) + jnp.log1p(jnp.exp(-jnp.abs(z)))
    lw = -jnp.exp(-softplus - 0.5)
    asig = jax.nn.sigmoid(a0 + al)
    kk = k * k_k
    kk = kk / jnp.maximum(jnp.sqrt(jnp.sum(kk * kk, axis=-1, keepdims=True)), L2_EPS)
    kmod = k * (1.0 + (asig - 1.0) * k_a)
    return lw, kmod, -kk, kk * asig


def _wkv_post(y, r, kmod, v, ln_w, ln_b, r_k):
    mu = jnp.mean(y, axis=-1, keepdims=True)
    var = jnp.mean(jnp.square(y - mu), axis=-1, keepdims=True)
    yn = (y - mu) * lax.rsqrt(var + GN_EPS)
    yn = yn * ln_w + ln_b
    return (yn + jnp.sum(r * kmod * r_k, axis=-1, keepdims=True) * v,)


def _attn_group(nonzero_block, q, kc, kp, vc, vp, cos_c, sin_c, cos_p, sin_p, q_gain, k_gain, sinks):
    ri = lax.broadcasted_iota(jnp.int32, (HEAD_DIM, HEAD_DIM), 0)
    ci = lax.broadcasted_iota(jnp.int32, (HEAD_DIM, HEAD_DIM), 1)
    half = HEAD_DIM // 2
    rot = jnp.where(ri == ci + half, -1.0, 0.0) + jnp.where(ri + half == ci, 1.0, 0.0)
    rows = Q_PER_KV * ATT_BLOCK

    def rope(x, cos, sin):
        return x * cos + _hdot(x, rot, _NN) * sin

    kcr = rope(_rms(kc, k_gain), cos_c, sin_c)
    kpr = rope(_rms(kp, k_gain), cos_p, sin_p)
    qn = _rms(q, q_gain)
    qr = qn * cos_c + _hdot(qn.reshape(rows, HEAD_DIM), rot, _NN).reshape(q.shape) * sin_c
    q2 = qr.reshape(rows, HEAD_DIM)
    qi = lax.broadcasted_iota(jnp.int32, (1, ATT_BLOCK, ATT_BLOCK), 1)
    ki = lax.broadcasted_iota(jnp.int32, (1, ATT_BLOCK, ATT_BLOCK), 2)
    mask_c = ki <= qi
    mask_p = jnp.logical_and(ki > qi, nonzero_block)
    lane0 = (lax.broadcasted_iota(jnp.int32, (1, 1, 128), 2) == 0).astype(F32)
    shape3 = (Q_PER_KV, ATT_BLOCK, ATT_BLOCK)
    sc = jnp.where(mask_c, (_bdot_nt(q2, kcr) * (HEAD_DIM ** -0.5)).reshape(shape3), MASK_VALUE)
    sp = jnp.where(mask_p, (_bdot_nt(q2, kpr) * (HEAD_DIM ** -0.5)).reshape(shape3), MASK_VALUE)
    sk = jnp.sum(sinks * lane0, axis=2, keepdims=True)
    mx = jnp.maximum(jnp.maximum(jnp.max(sc, axis=2, keepdims=True), jnp.max(sp, axis=2, keepdims=True)), sk)
    mx = lax.stop_gradient(mx)
    ec, ep = jnp.exp(sc - mx), jnp.exp(sp - mx)
    den = jnp.sum(ec, axis=2, keepdims=True) + jnp.sum(ep, axis=2, keepdims=True) + jnp.exp(sk - mx)
    out = _bdot((ec / den).reshape(rows, ATT_BLOCK), vc) + _bdot((ep / den).reshape(rows, ATT_BLOCK), vp)
    return out.reshape(q.shape)


def _attn_rows(nonzero_block, qs, kcs, kps, vcs, vps, tabs, q_gain, k_gain, sinks):
    return [_attn_group(nonzero_block, qs[g], kcs[g], kps[g], vcs[g], vps[g], *tabs, q_gain, k_gain, sinks[g])
            for g in range(N_KV_HEADS)]


def _attn_operands(q_tile, kvc_tile, kvp_tile, sink_ref):
    q3 = _split_heads(q_tile, N_HEADS)
    kvc, kvp = _split_heads(kvc_tile, 2 * N_KV_HEADS), _split_heads(kvp_tile, 2 * N_KV_HEADS)
    groups = range(N_KV_HEADS)
    qs = [q3[Q_PER_KV * g:Q_PER_KV * (g + 1)] for g in groups]
    sinks = [sink_ref[Q_PER_KV * g:Q_PER_KV * (g + 1)] for g in groups]
    return (qs, [kvc[g] for g in groups], [kvp[g] for g in groups], [kvc[N_KV_HEADS + g] for g in groups],
            [kvp[N_KV_HEADS + g] for g in groups], sinks)


def _attn_specs(t):
    nb = t // ATT_BLOCK
    prev = lambda n: jnp.maximum(n - 1, 0)
    kv_width = 2 * N_KV_HEADS * HEAD_DIM
    q_spec = pl.BlockSpec((ATT_BLOCK, D_MODEL), lambda n: (n, 0))
    kv_c = pl.BlockSpec((ATT_BLOCK, kv_width), lambda n: (n, 0))
    kv_p = pl.BlockSpec((ATT_BLOCK, kv_width), lambda n: (prev(n), 0))
    tab_c = pl.BlockSpec((ATT_BLOCK, HEAD_DIM), lambda n: (n, 0))
    tab_p = pl.BlockSpec((ATT_BLOCK, HEAD_DIM), lambda n: (prev(n), 0))
    gain = pl.BlockSpec((1, HEAD_DIM), lambda n: (0, 0))
    sink = pl.BlockSpec((N_HEADS, 1, 128), lambda n: (0, 0, 0))
    return nb, q_spec, kv_c, kv_p, tab_c, tab_p, gain, sink


def _attn_fwd(q, kv, cos, sin, q_gain, k_gain, sinks):
    t = q.shape[0]
    nb, q_spec, kv_c, kv_p, tab_c, tab_p, gain, sink = _attn_specs(t)

    def body(q_ref, kvc, kvp, cc, sc, cp, sp, qg, kg, sk, o_ref):
        qs, kcs, kps, vcs, vps, sinks_ = _attn_operands(q_ref[...], kvc[...], kvp[...], sk)
        outs = _attn_rows(pl.program_id(0) > 0, qs, kcs, kps, vcs, vps, (cc[...], sc[...], cp[...], sp[...]),
                          qg[...], kg[...], sinks_)
        o_ref[...] = _merge_heads(jnp.concatenate(outs, axis=0)).astype(o_ref.dtype)

    return pl.pallas_call(
        body, grid=(nb,), in_specs=[q_spec, kv_c, kv_p, tab_c, tab_c, tab_p, tab_p, gain, gain, sink],
        out_specs=q_spec, out_shape=_sds(q.shape, BF16), compiler_params=_params(("arbitrary",)), name="attn_fwd",
    )(q, kv, kv, cos, sin, cos, sin, q_gain, k_gain, sinks)


def _attn_bwd(q, kv, cos, sin, q_gain, k_gain, sinks, do, parts=()):
    t = q.shape[0]
    nb, q_spec, kv_c, kv_p, tab_c, tab_p, gain, sink = _attn_specs(t)
    ns = len(parts)

    def body(*refs):
        q_ref, kvc, kvp, cc, sc, cp, sp, qg, kg, sk, do_ref = refs[:11]
        dq_ref, dkvc_ref, dkvp_ref, dqg_ref, dkg_ref, dsk_ref = refs[11 + ns:17 + ns]
        if ns:
            start, finish = _sibling_exchange_plan(refs[11:11 + ns], refs[17 + ns:17 + 2 * ns], *refs[17 + 2 * ns:])
            pl.when(pl.program_id(0) == 0)(start)
        nonzero = pl.program_id(0) > 0
        tabs = (cc[...], sc[...], cp[...], sp[...])
        qs, kcs, kps, vcs, vps, sinks_ = _attn_operands(q_ref[...], kvc[...], kvp[...], sk)

        def f(qs_, kcs_, kps_, vcs_, vps_, qgv, kgv, sks):
            return _attn_rows(nonzero, qs_, kcs_, kps_, vcs_, vps_, tabs, qgv, kgv, sks)

        _, pull = jax.vjp(f, qs, kcs, kps, vcs, vps, qg[...], kg[...], sinks_)
        do3 = _split_heads(do_ref[...], N_HEADS)
        dqs, dkcs, dkps, dvcs, dvps, dqg, dkg, dsks = pull([do3[Q_PER_KV * g:Q_PER_KV * (g + 1)] for g in range(N_KV_HEADS)])
        dq_ref[...] = _merge_heads(jnp.concatenate(dqs, axis=0)).astype(dq_ref.dtype)
        dkvc_ref[...] = jnp.concatenate(dkcs + dvcs, axis=1)
        dkvp_ref[...] = jnp.concatenate(dkps + dvps, axis=1)

        @pl.when(pl.program_id(0) == 0)
        def _():
            dqg_ref[...] = jnp.zeros_like(dqg_ref)
            dkg_ref[...] = jnp.zeros_like(dkg_ref)
            dsk_ref[...] = jnp.zeros_like(dsk_ref)

        dqg_ref[...] += dqg
        dkg_ref[...] += dkg
        for g in range(N_KV_HEADS):
            dsk_ref[Q_PER_KV * g:Q_PER_KV * (g + 1)] += dsks[g]
        if ns:
            pl.when(pl.program_id(0) == nb - 1)(finish)

    outs = pl.pallas_call(
        body, grid=(nb,), in_specs=[q_spec, kv_c, kv_p, tab_c, tab_c, tab_p, tab_p, gain, gain, sink, q_spec] + [_ANY] * ns,
        out_specs=[q_spec, kv_c, kv_c, gain, gain, sink] + [_ANY] * ns,
        out_shape=[_sds(q.shape, BF16), _sds(kv.shape), _sds(kv.shape), _sds((1, HEAD_DIM)), _sds((1, HEAD_DIM)),
                   _sds(sinks.shape)] + _sibling_exchange_shapes(parts),
        scratch_shapes=_sibling_exchange_semaphores(ns) if ns else [],
        compiler_params=_params(("arbitrary",)), name="attn_bwd",
    )(q, kv, kv, cos, sin, cos, sin, q_gain, k_gain, sinks, do, *parts)
    return outs[:6], list(outs[6:])


def _time_shift_lerps(x, xs, gain, *mix):
    xn, xsn = _rms(x, gain), _rms(xs, gain)
    xx = xsn - xn
    return tuple(xn + xx * m for m in mix)


def _residual_norm(h, delta, gain):
    hn = h + delta
    return hn, _rms(hn, gain)


def _residual_norm2(h, delta, gain_a, gain_b):
    hn = h + delta
    return hn, _rms(hn, gain_a), _rms(hn, gain_b)


def _relu2(u):
    return jnp.square(jnp.maximum(u, 0.0))


def _sigmoid(z):
    return jax.nn.sigmoid(z)


def _shift_down(x):
    return jnp.pad(x[:-1], ((1, 0), (0, 0)))


def _shift_up(x):
    return jnp.pad(x[1:], ((0, 1), (0, 0)))


def _rope_tables(t):
    half = HEAD_DIM // 2
    inv_freq = jnp.power(ROPE_THETA, -jnp.arange(half, dtype=F32) / half)
    ang = jnp.arange(t, dtype=jnp.int32).astype(F32)[:, None] * inv_freq[None, :]
    cos, sin = jnp.cos(ang), jnp.sin(ang)
    return jnp.concatenate([cos, cos], axis=1), jnp.concatenate([sin, sin], axis=1)


def _mlp_fwd(hn, w_up, w_down, layer, up_dev_major):
    t = hn.shape[0]
    if up_dev_major:
        cw = w_up.shape[2]
        u = _mm(hn, w_up, name=f"mlp{layer}_up", dims=(t, D_FF, D_MODEL), tn=cw, tk=D_MODEL,
                b_spec=pl.BlockSpec((None, D_MODEL, cw), lambda i, j, q: (j, q, 0)))
    else:
        u = _mm(hn, w_up, name=f"mlp{layer}_up")
    out = _mm(u, w_down, a_pro=_relu2, name=f"mlp{layer}_down")
    return u, out


def _mlp_bwd(hn, u, dh, w_up, w_down, layer, up_dev_major):
    t = hn.shape[0]
    du = _mm(dh, w_down, tb=True, epi=lambda r, uu: r * (2.0 * jnp.maximum(uu, 0.0)), epi_args=(u,), out_dtype=BF16,
             name=f"mlp{layer}_du")
    d_down = _mm(u, dh, ta=True, a_pro=_relu2, name=f"mlp{layer}_ddown")
    if up_dev_major:
        cw = w_up.shape[2]
        d_up = _mm(hn, du, ta=True, name=f"mlp{layer}_dup", dims=(D_MODEL, D_FF, t), tn=cw,
                   o_spec=pl.BlockSpec((None, 1024, cw), lambda i, j, q: (j, i, 0)), o_shape=(N_DEV, D_MODEL, cw))
        dhn = _mm(du, w_up, tb=True, name=f"mlp{layer}_dhn", dims=(t, D_MODEL, D_FF), tk=cw,
                  b_spec=pl.BlockSpec((None, 1024, cw), lambda i, j, q: (q, j, 0)))
    else:
        d_up = _mm(hn, du, ta=True, name=f"mlp{layer}_dup")
        dhn = _mm(du, w_up, tb=True, name=f"mlp{layer}_dhn")
    return dhn, d_up, d_down


_LATE_WEIGHTS = ("mlp_w_up0", "mlp_w_up1", "mlp_w_down0", "mlp_w_down1", "b_w_q", "b_w_o")
_LATE_GATHER = _LATE_WEIGHTS + ("a_w_out", "a_g1", "w_kv")
_EARLY_GRADS = _LATE_WEIGHTS + ("a_w_out",)
_EARLY_GRADS_A = ("mlp_w_up1", "mlp_w_down1", "b_w_o")
_EARLY_GRADS_B = tuple(name for name in _EARLY_GRADS if name not in _EARLY_GRADS_A)


def _local_step(x, target, w, late_shards=None, up_dev_major=True):
    t = x.shape[0]
    g = {}
    w = dict(w)
    row = lambda: _sds((t, D_MODEL))
    rowb = lambda: _sds((t, D_MODEL), BF16)
    vec = lambda: _sds((1, D_MODEL))

    xs = _shift_down(x)
    mix = [w["a_mix"][i:i + 1] for i in range(6)]
    xr, xk, xv, xw, xa, xg = _rowwise(_time_shift_lerps, [x, xs], [w["a_norm"]] + mix, [rowb()] * 6, tm=256, name="tmix_lerp")
    r = _mm(xr, w["a_w_r"], name="tmix_r")
    k = _mm(xk, w["a_w_k"], name="tmix_k")
    v = _mm(xv, w["a_w_v"], name="tmix_v")
    lw1 = _mm(xw, w["a_w1"], name="tmix_w1")
    wl = _mm(lw1, w["a_w2"], a_pro=jnp.tanh, name="tmix_w2")
    la1 = _mm(xa, w["a_a1"], name="tmix_a1")
    al = _mm(la1, w["a_a2"], name="tmix_a2")

    hv = lambda name: w[name].reshape(N_HEADS, 1, HEAD_DIM)
    wkv_params = [hv(name) for name in ("a_w0", "a_a0", "a_k_k", "a_k_a", "a_ln_x_w", "a_ln_x_b", "a_r_k")]
    y2, states, gathered = _wkv_fwd(r, k, v, wl, al, wkv_params, late_shards or ())
    for name, arr in zip(_LATE_GATHER, gathered):
        w[name] = arr if name.startswith("mlp_w_up") else arr.reshape(N_DEV * arr.shape[1], arr.shape[2])
    lg1 = _mm(xg, w["a_g1"], name="tmix_g1")
    gate = _mm(lg1, w["a_g2"], a_pro=_sigmoid, name="tmix_g2")
    (yg,) = _rowwise(lambda a, b: (a * b,), [y2, gate], [], [rowb()], name="tmix_gate")
    att = _mm(yg, w["a_w_out"], name="tmix_out")

    h1, hn0 = _rowwise(_residual_norm, [x, att], [w["mlp_norm0"]], [row(), rowb()], name="res_norm0")
    u0, m0 = _mlp_fwd(hn0, w["mlp_w_up0"], w["mlp_w_down0"], 0, up_dev_major)

    h2, kvn, qn = _rowwise(_residual_norm2, [h1, m0], [w["kv_norm"], w["b_norm"]], [row(), rowb(), rowb()], name="res_norm_kvq")
    kv = _mm(kvn, w["w_kv"], name="kv_proj")
    q = _mm(qn, w["b_w_q"], name="q_proj")
    cos, sin = _rope_tables(t)
    sinks = jnp.broadcast_to(w["b_sinks"].reshape(N_HEADS, 1, 1), (N_HEADS, 1, 128))
    o = _attn_fwd(q, kv, cos, sin, w["b_q_norm"], w["k_norm"], sinks)
    att2 = _mm(o, w["b_w_o"], name="attn_out")

    h3, hn1 = _rowwise(_residual_norm, [h2, att2], [w["mlp_norm1"]], [row(), rowb()], name="res_norm1")
    u1, m1 = _mlp_fwd(hn1, w["mlp_w_up1"], w["mlp_w_down1"], 1, up_dev_major)

    def loss_fn(h, m, tg):
        diff = (h + m) - tg
        part = 0.5 * jnp.sum(jnp.mean(jnp.square(diff), axis=-1, keepdims=True), axis=0, keepdims=True)
        dh = diff * (1.0 / D_MODEL)
        return dh, dh, jnp.broadcast_to(part, (1, 128))

    dh4, dh4b, loss = _rowwise(loss_fn, [h3, m1, target], [], [row(), rowb()], [_sds((1, 128))], name="loss")

    def res_norm_bwd(h, dnext, dhn, gain):
        dh, dgain = _vjp_of(lambda hh, gg: (_rms(hh, gg),), 2, (0, 1))(h, gain, dhn)
        return dnext + dh, dnext + dh, dgain

    dhn1, g["mlp_w_up1"], g["mlp_w_down1"] = _mlp_bwd(hn1, u1, dh4b, w["mlp_w_up1"], w["mlp_w_down1"], 1, up_dev_major)
    dh3, dh3b, g["mlp_norm1"] = _rowwise(res_norm_bwd, [h3, dh4, dhn1], [w["mlp_norm1"]], [row(), rowb()], [vec()], name="res_norm1_bwd")

    g["b_w_o"] = _mm(o, dh3b, ta=True, name="attn_out_dw")
    do = _mm(dh3b, w["b_w_o"], tb=True, name="attn_out_dx")
    parts_a = [_device_major(name, g[name]) for name in _EARLY_GRADS_A] if late_shards is not None else []
    (dq, dkv_own, dkv_prev, g["b_q_norm"], g["k_norm"], dsinks), from_sibling_a = _attn_bwd(
        q, kv, cos, sin, w["b_q_norm"], w["k_norm"], sinks, do, parts_a)
    g["b_sinks"] = dsinks[:, 0, 0].reshape(1, N_HEADS)
    g["b_w_q"] = _mm(qn, dq, ta=True, name="q_proj_dw")
    dqn = _mm(dq, w["b_w_q"], tb=True, name="q_proj_dx")
    dkv_prev = jnp.pad(dkv_prev[ATT_BLOCK:], ((0, ATT_BLOCK), (0, 0)))
    (dkv,) = _rowwise(lambda a, b: (a + b,), [dkv_own, dkv_prev], [], [_sds(kv.shape, BF16)], name="kv_grad_sum")
    g["w_kv"] = _mm(kvn, dkv, ta=True, name="kv_proj_dw")
    dkvn = _mm(dkv, w["w_kv"], tb=True, name="kv_proj_dx")

    def res_norm2_bwd(h, dnext, dna, dnb, gain_a, gain_b):
        dha, dga = _vjp_of(lambda hh, gg: (_rms(hh, gg),), 2, (0, 1))(h, gain_a, dna)
        dhb, dgb = _vjp_of(lambda hh, gg: (_rms(hh, gg),), 2, (0, 1))(h, gain_b, dnb)
        return dnext + dha + dhb, dnext + dha + dhb, dga, dgb

    dh2, dh2b, g["kv_norm"], g["b_norm"] = _rowwise(res_norm2_bwd, [h2, dh3, dkvn, dqn], [w["kv_norm"], w["b_norm"]],
                                                    [row(), rowb()], [vec(), vec()], name="res_norm_kvq_bwd")

    dhn0, g["mlp_w_up0"], g["mlp_w_down0"] = _mlp_bwd(hn0, u0, dh2b, w["mlp_w_up0"], w["mlp_w_down0"], 0, up_dev_major)
    dh1, dh1b, g["mlp_norm0"] = _rowwise(res_norm_bwd, [h1, dh2, dhn0], [w["mlp_norm0"]], [row(), rowb()], [vec()], name="res_norm0_bwd")

    g["a_w_out"] = _mm(yg, dh1b, ta=True, name="tmix_out_dw")
    dyg = _mm(dh1b, w["a_w_out"], tb=True, name="tmix_out_dx")
    dy2, dgate = _rowwise(lambda d, a, b: (d * b, d * a), [dyg, y2, gate], [], [row(), rowb()], name="tmix_gate_bwd")
    g["a_g2"] = _mm(lg1, dgate, ta=True, a_pro=_sigmoid, name="tmix_g2_dw")

    def dsigmoid(rr, z):
        s = jax.nn.sigmoid(z)
        return rr * s * (1.0 - s)

    dlg1 = _mm(dgate, w["a_g2"], tb=True, epi=dsigmoid, epi_args=(lg1,), out_dtype=BF16, name="tmix_g2_dx")
    g["a_g1"] = _mm(xg, dlg1, ta=True, name="tmix_g1_dw")
    dxg = _mm(dlg1, w["a_g1"], tb=True, name="tmix_g1_dx")

    early_sums = ()
    if late_shards is not None:
        sums = dict(zip(_EARLY_GRADS_A, _pair_sums(parts_a, from_sibling_a, _EARLY_GRADS_A)))
        sums.update(zip(_EARLY_GRADS_B, _chip_sums([_device_major(name, g[name]) for name in _EARLY_GRADS_B], _EARLY_GRADS_B, "early")))
        early_sums = [sums[name] for name in _EARLY_GRADS]
    (dr, dk, dv, dwl, dal), param_grads, early_reduced = _wkv_bwd(r, k, v, wl, al, wkv_params, states, dy2, early_sums)
    for name, pg in zip(("a_w0", "a_a0", "a_k_k", "a_k_a", "a_ln_x_w", "a_ln_x_b", "a_r_k"), param_grads):
        g[name] = pg.reshape(1, D_MODEL)

    g["a_w_r"] = _mm(xr, dr, ta=True, name="tmix_r_dw")
    g["a_w_k"] = _mm(xk, dk, ta=True, name="tmix_k_dw")
    g["a_w_v"] = _mm(xv, dv, ta=True, name="tmix_v_dw")
    dxr = _mm(dr, w["a_w_r"], tb=True, name="tmix_r_dx")
    dxk = _mm(dk, w["a_w_k"], tb=True, name="tmix_k_dx")
    dxv = _mm(dv, w["a_w_v"], tb=True, name="tmix_v_dx")
    g["a_w2"] = _mm(lw1, dwl, ta=True, a_pro=jnp.tanh, name="tmix_w2_dw")

    def dtanh(rr, z):
        th = jnp.tanh(z)
        return rr * (1.0 - th * th)

    dlw1 = _mm(dwl, w["a_w2"], tb=True, epi=dtanh, epi_args=(lw1,), out_dtype=BF16, name="tmix_w2_dx")
    g["a_w1"] = _mm(xw, dlw1, ta=True, name="tmix_w1_dw")
    dxw = _mm(dlw1, w["a_w1"], tb=True, name="tmix_w1_dx")
    g["a_a2"] = _mm(la1, dal, ta=True, name="tmix_a2_dw")
    dla1 = _mm(dal, w["a_a2"], tb=True, out_dtype=BF16, name="tmix_a2_dx")
    g["a_a1"] = _mm(xa, dla1, ta=True, name="tmix_a1_dw")
    dxa = _mm(dla1, w["a_a1"], tb=True, name="tmix_a1_dx")

    lerp_bwd = _vjp_of(_time_shift_lerps, 9, tuple(range(9)))

    def lerp_bwd_rows(x_, xs_, d0, d1, d2, d3, d4, d5, gain, *mx):
        return lerp_bwd(x_, xs_, gain, *mx, d0, d1, d2, d3, d4, d5)

    outs = _rowwise(lerp_bwd_rows, [x, xs, dxr, dxk, dxv, dxw, dxa, dxg], [w["a_norm"]] + mix, [row(), row()], [vec()] * 7,
                    tm=128, name="tmix_lerp_bwd")
    dx_a, dxs, g["a_norm"] = outs[0], outs[1], outs[2]
    g["a_mix"] = jnp.concatenate(outs[3:9], axis=0)
    (grad_x,) = _rowwise(lambda a, b, c: (a + b + c,), [dh1, dx_a, _shift_up(dxs)], [], [row()], name="grad_x_sum")
    return loss, grad_x, g, (early_reduced if late_shards is not None else None)


def _device_major(name, grad):
    return grad if name.startswith("mlp_w_up") else grad.reshape((N_DEV, grad.shape[0] // N_DEV, grad.shape[1]))


_ANY = pl.BlockSpec(memory_space=pl.ANY)
_MESH_ID = pl.DeviceIdType.MESH
N_PEERS = N_DEV - 1


def _linear(pos):
    return 4 * pos[0] + 2 * pos[1] + pos[2]


def _all_gather(shards, name):
    n = len(shards)

    def body(*refs):
        start, relay, finish = _gather_plan(refs[:n], refs[n:2 * n], *refs[2 * n:])
        start()
        relay()
        finish()

    return pl.pallas_call(
        body, out_shape=[_sds((N_DEV,) + s.shape, s.dtype) for s in shards], in_specs=[_ANY] * n, out_specs=[_ANY] * n,
        scratch_shapes=_gather_semaphores(n), name=name,
    )(*shards)


GATHER_COPIES = 8


def _gather_semaphores(n):
    return [pltpu.SemaphoreType.DMA((n * GATHER_COPIES,)), pltpu.SemaphoreType.DMA((n * GATHER_COPIES,)),
            pltpu.SemaphoreType.DMA((n,))]


def _gather_plan(ins, outs, send_sems, recv_sems, local_sems):
    n = len(ins)
    x, y, c = lax.axis_index("x"), lax.axis_index("y"), lax.axis_index("c")
    me, sibling = (x, y, c), (x, y, 1 - c)
    x_nbr, y_nbr, diag = (1 - x, y, c), (x, 1 - y, c), (1 - x, 1 - y, c)
    other = lambda pos: (pos[0], pos[1], 1 - c)

    def halves(a):
        rows = ins[a].shape[0]
        if rows % 32:
            return (0, rows), None
        return (0, rows // 2), (rows // 2, rows // 2)

    def copy(a, k, block, to, src=None, rows=None):
        dst = outs[a].at[_linear(block)]
        if rows is not None:
            dst = dst.at[pl.ds(rows[0], rows[1])]
        return pltpu.make_async_remote_copy(
            src_ref=dst if src is None else src, dst_ref=dst, send_sem=send_sems.at[a * GATHER_COPIES + k],
            recv_sem=recv_sems.at[a * GATHER_COPIES + k], device_id=to, device_id_type=_MESH_ID)

    def own_copies():
        mine = [pltpu.make_async_copy(ins[a], outs[a].at[_linear(me)], local_sems.at[a]) for a in range(n)]
        sent = []
        for a in range(n):
            sent += [copy(a, 0, me, sibling, src=ins[a]), copy(a, 1, me, x_nbr, src=ins[a]), copy(a, 2, me, y_nbr, src=ins[a])]
        return mine, sent

    def relayed_copies():
        sent = []
        for a in range(n):
            first, second = halves(a)
            sent += [copy(a, 3, x_nbr, y_nbr, rows=first), copy(a, 5, x_nbr, sibling), copy(a, 6, y_nbr, sibling)]
            if second is not None:
                sent.append(copy(a, 4, y_nbr, x_nbr, rows=second))
        return sent

    def start():
        mine, sent = own_copies()
        for cp in mine + sent:
            cp.start()

    def relay():
        for a in range(n):
            first, second = halves(a)
            copy(a, 1, x_nbr, me).wait_recv()
            copy(a, 3, x_nbr, y_nbr, rows=first).start()
            copy(a, 5, x_nbr, sibling).start()
        for a in range(n):
            first, second = halves(a)
            copy(a, 2, y_nbr, me).wait_recv()
            if second is not None:
                copy(a, 4, y_nbr, x_nbr, rows=second).start()
            copy(a, 6, y_nbr, sibling).start()

    def finish():
        mine, sent = own_copies()
        sent += relayed_copies()
        for a in range(n):
            first, second = halves(a)
            copy(a, 3, diag, me, rows=first).wait_recv()
            if second is not None:
                copy(a, 4, diag, me, rows=second).wait_recv()
            last = copy(a, 7, diag, sibling)
            last.start()
            sent.append(last)
        for a in range(n):
            copy(a, 0, other(me), me).wait_recv()
            copy(a, 5, other(x_nbr), me).wait_recv()
            copy(a, 6, other(y_nbr), me).wait_recv()
            copy(a, 7, other(diag), me).wait_recv()
        for cp in sent:
            cp.wait_send()
        for cp in mine:
            cp.wait()

    return start, relay, finish


N_CHIPS = 4


def _exchange_with_sibling(parts, name):
    n = len(parts)

    def body(*refs):
        start, finish = _sibling_exchange_plan(refs[:n], refs[n:2 * n], *refs[2 * n:])
        start()
        finish()

    return pl.pallas_call(
        body, out_shape=_sibling_exchange_shapes(parts), in_specs=[_ANY] * n, out_specs=[_ANY] * n,
        scratch_shapes=_sibling_exchange_semaphores(n), name=name,
    )(*parts)


def _sibling_exchange_shapes(parts):
    return [_sds((N_CHIPS,) + p.shape[1:], p.dtype) for p in parts]


def _sibling_exchange_semaphores(n):
    return [pltpu.SemaphoreType.DMA((n * N_CHIPS,)), pltpu.SemaphoreType.DMA((n * N_CHIPS,))]


def _sibling_exchange_plan(ins, outs, send_sems, recv_sems):
    n = len(ins)
    x, y, c = lax.axis_index("x"), lax.axis_index("y"), lax.axis_index("c")

    def all_copies():
        return [pltpu.make_async_remote_copy(
            src_ref=ins[a].at[2 * q + (1 - c)], dst_ref=outs[a].at[q], send_sem=send_sems.at[a * N_CHIPS + q],
            recv_sem=recv_sems.at[a * N_CHIPS + q], device_id=(x, y, 1 - c), device_id_type=_MESH_ID)
            for a in range(n) for q in range(N_CHIPS)]

    def start():
        for cp in all_copies():
            cp.start()

    def finish():
        for cp in all_copies():
            cp.wait()

    return start, finish


def _pair_sum(part, recv, core, out_dtype, name):
    _, r, cdim = recv.shape
    tr = max(8, min(r, (1 << 18) // cdim))
    assert r % tr == 0, (name, r, tr)

    def body(core_ref, p_ref, r_ref, o_ref):
        o_ref[...] = (p_ref[...] + r_ref[...]).astype(o_ref.dtype)

    grid_spec = pltpu.PrefetchScalarGridSpec(
        num_scalar_prefetch=1, grid=(N_CHIPS, r // tr),
        in_specs=[pl.BlockSpec((None, None, tr, cdim), lambda q, i, core_ref: (q, core_ref[0], i, 0)),
                  pl.BlockSpec((None, tr, cdim), lambda q, i, core_ref: (q, i, 0))],
        out_specs=pl.BlockSpec((None, tr, cdim), lambda q, i, core_ref: (q, i, 0)))
    return pl.pallas_call(
        body, grid_spec=grid_spec, out_shape=_sds((N_CHIPS, r, cdim), out_dtype),
        compiler_params=_params(("parallel", "parallel")), name=name,
    )(core, part.reshape(N_CHIPS, 2, r, cdim), recv)


def _exchange_between_chips(parts, name):
    n = len(parts)

    def body(*refs):
        start, finish = _chip_exchange_plan(refs[:n], refs[n:2 * n], *refs[2 * n:])
        start()
        finish()

    return pl.pallas_call(
        body, out_shape=[_sds(p.shape, p.dtype) for p in parts], in_specs=[_ANY] * n, out_specs=[_ANY] * n,
        scratch_shapes=_chip_exchange_semaphores(n), name=name,
    )(*parts)


def _chip_exchange_semaphores(n):
    n_other = N_CHIPS - 1
    return [pltpu.SemaphoreType.DMA((n * n_other,)), pltpu.SemaphoreType.DMA((n * n_other,)), pltpu.SemaphoreType.DMA((n,))]


def _chip_exchange_plan(ins, outs, send_sems, recv_sems, local_sems):
    n = len(ins)
    n_other = N_CHIPS - 1
    x, y, c = lax.axis_index("x"), lax.axis_index("y"), lax.axis_index("c")
    my_chip = 2 * x + y

    def all_copies():
        mine = [pltpu.make_async_copy(ins[a].at[my_chip], outs[a].at[my_chip], local_sems.at[a]) for a in range(n)]
        remote = []
        for j, (fx, fy) in enumerate([(1, 0), (0, 1), (1, 1)]):
            px, py = (1 - x if fx else x), (1 - y if fy else y)
            for a in range(n):
                remote.append(pltpu.make_async_remote_copy(
                    src_ref=ins[a].at[2 * px + py], dst_ref=outs[a].at[my_chip], send_sem=send_sems.at[a * n_other + j],
                    recv_sem=recv_sems.at[a * n_other + j], device_id=(px, py, c), device_id_type=_MESH_ID))
        return mine, remote

    def start():
        mine, remote = all_copies()
        for cp in mine + remote:
            cp.start()

    def finish():
        mine, remote = all_copies()
        for cp in remote + mine:
            cp.wait()

    return start, finish


def _chip_sums(parts, names, tag):
    return _pair_sums(parts, _exchange_with_sibling(parts, name="scatter_grads_sibling_" + tag), names)


def _pair_sums(parts, from_sibling, names):
    core = lax.axis_index("c").astype(jnp.int32).reshape(1)
    return [_pair_sum(p, r, core, F32 if nm.startswith("pack") else BF16, name="pair_sum_" + nm)
            for p, r, nm in zip(parts, from_sibling, names)]


def _reduce_scatter(parts, names, tag):
    return _exchange_between_chips(_chip_sums(parts, names, tag), name="scatter_grads_chips_" + tag)


def _adamw(w, m, v, slots, name, layer=0, n_layers=1, into=None):
    r, c = w.shape
    ns = slots.shape[0]
    tr = max(8, min(r, (1 << 18) // c))
    assert r % tr == 0, (name, r, tr)

    def body(w_ref, m_ref, v_ref, g_ref, *rest):
        g_out, d_out, m_out, v_out = rest[-4:]
        g = g_ref[0].astype(F32)
        for s in range(1, ns):
            g = g + g_ref[s].astype(F32)
        m_new = ADAM_B1 * m_ref[...] + (1.0 - ADAM_B1) * g
        v_new = ADAM_B2 * v_ref[...] + (1.0 - ADAM_B2) * jnp.square(g)
        m_hat = m_new / (1.0 - ADAM_B1 ** ADAM_STEP)
        v_hat = v_new / (1.0 - ADAM_B2 ** ADAM_STEP)
        d_out[...] = -ADAM_LR * (m_hat / (jnp.sqrt(v_hat) + ADAM_EPS) + ADAM_WD * w_ref[...])
        g_out[...], m_out[...], v_out[...] = g, m_new, v_new

    spec = pl.BlockSpec((tr, c), lambda i: (i, 0))
    in_specs = [spec, spec, spec, pl.BlockSpec((ns, tr, c), lambda i: (0, i, 0))]
    if n_layers == 1:
        out_spec, out_shape, earlier, aliases = spec, _sds((r, c)), [], {}
    else:
        out_spec, out_shape = pl.BlockSpec((None, tr, c), lambda i: (layer, i, 0)), _sds((n_layers, r, c))
        earlier = list(into) if into is not None else []
        aliases = {4 + i: i for i in range(len(earlier))}
    return pl.pallas_call(
        body, grid=(r // tr,), in_specs=in_specs + [_ANY] * len(earlier), out_specs=[out_spec] * 4, out_shape=[out_shape] * 4,
        input_output_aliases=aliases, compiler_params=_params(("parallel",)), name=name,
    )(w, m, v, slots, *earlier)


_COL_VECTORS = ("a_norm", "a_mix", "a_w0", "a_a0", "a_k_k", "a_k_a", "a_ln_x_w", "a_ln_x_b")
_COL_VEC_ROWS = 16
_COL_ROWS = _COL_VEC_ROWS + 2 * LORA_PAD + 256
_ROW_COLS = 2 * LORA_PAD + 256 + 512
_REPL_ROWS = 8


def _pad_to(a, size, axis):
    widths = [(0, 0)] * a.ndim
    widths[axis] = (0, size - a.shape[axis])
    return jnp.pad(a, widths)


def _pack_cols(p):
    width = p["a_norm"].shape[-1]
    vecs = jnp.concatenate([p[n].reshape(-1, width) for n in _COL_VECTORS], axis=0)
    return jnp.concatenate([_pad_to(vecs, _COL_VEC_ROWS, 0), _pad_to(p["a_w2"].reshape(-1, width), LORA_PAD, 0),
                            _pad_to(p["a_a2"].reshape(-1, width), LORA_PAD, 0), p["a_g2"].reshape(-1, width)], axis=0)


def _unpack_cols(a, lead):
    width = a.shape[-1]
    out, row = {}, 0
    for n in _COL_VECTORS:
        k = 6 if n == "a_mix" else 1
        out[n] = a[row:row + k].reshape(lead + ((6, width) if n == "a_mix" else (width,)))
        row += k
    base = _COL_VEC_ROWS
    out["a_w2"] = a[base:base + 96].reshape(lead + (96, width))
    out["a_a2"] = a[base + LORA_PAD:base + LORA_PAD + 96].reshape(lead + (96, width))
    out["a_g2"] = a[base + 2 * LORA_PAD:].reshape(lead + (256, width))
    return out


def _pack_rows(p):
    rows = p["w_kv"].shape[0]
    return jnp.concatenate([_pad_to(p["a_w1"].reshape(rows, -1), LORA_PAD, 1), _pad_to(p["a_a1"].reshape(rows, -1), LORA_PAD, 1),
                            p["a_g1"].reshape(rows, -1), p["w_kv"]], axis=1)


def _unpack_rows(a, lead):
    rows = a.shape[0]
    return {"a_w1": a[:, :96].reshape(lead + (rows, 96)), "a_a1": a[:, LORA_PAD:LORA_PAD + 96].reshape(lead + (rows, 96)),
            "a_g1": a[:, 2 * LORA_PAD:2 * LORA_PAD + 256].reshape(lead + (rows, 256)), "w_kv": a[:, 2 * LORA_PAD + 256:]}


def _pack_repl(p):
    row = lambda a: _pad_to(a.reshape(1, -1), D_MODEL, 1)
    return jnp.concatenate([p["mlp_norm"].reshape(2, D_MODEL), row(p["kv_norm"]), row(p["b_norm"]), row(p["a_r_k"]),
                            row(p["k_norm"]), row(p["b_q_norm"]), row(p["b_sinks"])], axis=0)


def _unpack_repl(a):
    return {"mlp_norm": a[0:2], "kv_norm": a[2], "b_norm": a[3:4], "a_r_k": a[4].reshape(1, N_HEADS, HEAD_DIM),
            "k_norm": a[5, :HEAD_DIM], "b_q_norm": a[6:7, :HEAD_DIM], "b_sinks": a[7:8, :N_HEADS]}


_WEIGHTS = ("a_norm", "a_mix", "a_w_rkv", "a_w0", "a_w1", "a_w2", "a_a0", "a_a1", "a_a2", "a_g1", "a_g2", "a_k_k", "a_k_a",
            "a_r_k", "a_ln_x_w", "a_ln_x_b", "a_w_out", "mlp_norm", "mlp_w_up", "mlp_w_down", "kv_norm", "w_kv", "k_norm",
            "b_norm", "b_w_q", "b_q_norm", "b_sinks", "b_w_o")


def _big_shards(p):
    return [p["a_w_rkv"][0, 0], p["a_w_rkv"][0, 1], p["a_w_rkv"][0, 2], p["a_w_out"][0], p["mlp_w_up"][0], p["mlp_w_up"][1],
            p["mlp_w_down"][0], p["mlp_w_down"][1], p["b_w_q"][0], p["b_w_o"][0]]


_BIG_NAMES = ("a_w_r", "a_w_k", "a_w_v", "a_w_out", "mlp_w_up0", "mlp_w_up1", "mlp_w_down0", "mlp_w_down1", "b_w_q", "b_w_o")


def kernel(x, a_norm, a_mix, a_w_rkv, a_w0, a_w1, a_w2, a_a0, a_a1, a_a2, a_g1, a_g2, a_k_k, a_k_a, a_r_k, a_ln_x_w,
           a_ln_x_b, a_w_out, mlp_norm, mlp_w_up, mlp_w_down, kv_norm, w_kv, k_norm, b_norm, b_w_q, b_q_norm, b_sinks,
           b_w_o, loss_target, m_a_norm, m_a_mix, m_a_w_rkv, m_a_w0, m_a_w1, m_a_w2, m_a_a0, m_a_a1, m_a_a2, m_a_g1,
           m_a_g2, m_a_k_k, m_a_k_a, m_a_r_k, m_a_ln_x_w, m_a_ln_x_b, m_a_w_out, m_mlp_norm, m_mlp_w_up, m_mlp_w_down,
           m_kv_norm, m_w_kv, m_k_norm, m_b_norm, m_b_w_q, m_b_q_norm, m_b_sinks, m_b_w_o, v_a_norm, v_a_mix, v_a_w_rkv,
           v_a_w0, v_a_w1, v_a_w2, v_a_a0, v_a_a1, v_a_a2, v_a_g1, v_a_g2, v_a_k_k, v_a_k_a, v_a_r_k, v_a_ln_x_w,
           v_a_ln_x_b, v_a_w_out, v_mlp_norm, v_mlp_w_up, v_mlp_w_down, v_kv_norm, v_w_kv, v_k_norm, v_b_norm, v_b_w_q,
           v_b_q_norm, v_b_sinks, v_b_w_o):
    given = locals()
    wts = {n: given[n] for n in _WEIGHTS}
    mom = {n: given["m_" + n] for n in _WEIGHTS}
    var = {n: given["v_" + n] for n in _WEIGHTS}

    cols_w, rows_w, repl_w = _pack_cols(wts), _pack_rows(wts), _pack_repl(wts)
    big_w = _big_shards(wts)
    big_bf16 = dict(zip(_BIG_NAMES, [b.astype(BF16) for b in big_w]))
    first_names = [k for k in _BIG_NAMES if k not in _LATE_GATHER]
    lora_in = rows_w[:, :2 * LORA_PAD]
    gathered = _all_gather([cols_w, lora_in] + [big_bf16[k] for k in first_names], name="gather_weights")
    full_cols = gathered[0].transpose(1, 0, 2).reshape(_COL_ROWS, D_MODEL)
    full_lora_in = gathered[1].reshape(D_MODEL, 2 * LORA_PAD)
    w = {}
    w.update({k: v.reshape(v.shape[1:]) for k, v in _unpack_cols(full_cols, (1,)).items()})
    for k in ("a_norm", "a_w0", "a_a0", "a_k_k", "a_k_a", "a_ln_x_w", "a_ln_x_b"):
        w[k] = w[k].reshape(1, D_MODEL)
    for k in ("a_w2", "a_a2"):
        w[k] = _pad_to(w[k], LORA_PAD, 0)
    w["a_w1"], w["a_a1"] = full_lora_in[:, :LORA_PAD], full_lora_in[:, LORA_PAD:]
    for k, arr in zip(first_names, gathered[2:]):
        w[k] = arr.reshape(N_DEV * arr.shape[1], arr.shape[2])
    late_shards = [big_bf16[k] for k in _LATE_WEIGHTS] + [big_bf16["a_w_out"], a_g1[0], w_kv]
    w["mlp_norm0"], w["mlp_norm1"] = mlp_norm[0:1], mlp_norm[1:2]
    w["kv_norm"], w["k_norm"] = kv_norm.reshape(1, D_MODEL), k_norm.reshape(1, HEAD_DIM)
    w["b_norm"], w["b_q_norm"], w["b_sinks"], w["a_r_k"] = b_norm, b_q_norm, b_sinks, a_r_k.reshape(1, D_MODEL)

    loss_local, grad_x, g, early_reduced = _local_step(x[0], loss_target[0], w, late_shards)
    loss = lax.psum(loss_local[0, 0], MESH_AXES)

    g_lead = {k: g[k][None] for k in ("a_norm", "a_mix", "a_w0", "a_a0", "a_k_k", "a_k_a", "a_ln_x_w", "a_ln_x_b", "a_g2")}
    g_lead["a_w2"], g_lead["a_a2"] = g["a_w2"][None, :96], g["a_a2"][None, :96]
    g_cols = _pack_cols(g_lead).reshape(_COL_ROWS, N_DEV, D_MODEL // N_DEV).transpose(1, 0, 2)
    g_rows = _pack_rows({"a_w1": g["a_w1"][:, :96], "a_a1": g["a_a1"][:, :96], "a_g1": g["a_g1"], "w_kv": g["w_kv"]})
    g_rows = g_rows.reshape(N_DEV, D_MODEL // N_DEV, _ROW_COLS)
    late_names = tuple(k for k in _BIG_NAMES if k not in _EARLY_GRADS)
    late_reduced = _reduce_scatter([g_cols, g_rows] + [_device_major(k, g[k]) for k in late_names],
                                   ("pack_cols", "pack_rows") + late_names, "late")
    big_reduced = dict(zip(late_names, late_reduced[2:]))
    big_reduced.update(zip(_EARLY_GRADS, early_reduced))
    reduced = list(late_reduced[:2]) + [big_reduced[k] for k in _BIG_NAMES]
    g_repl = _pack_repl({"mlp_norm": jnp.concatenate([g["mlp_norm0"], g["mlp_norm1"]], axis=0), "kv_norm": g["kv_norm"],
                         "b_norm": g["b_norm"], "a_r_k": g["a_r_k"], "k_norm": g["k_norm"], "b_q_norm": g["b_q_norm"],
                         "b_sinks": g["b_sinks"]})
    (repl_slots,) = _all_gather([g_repl], name="gather_replicated_grads")

    res = {}
    cols4 = _adamw(cols_w, _pack_cols(mom), _pack_cols(var), reduced[0], name="adamw_cols")
    rows4 = _adamw(rows_w, _pack_rows(mom), _pack_rows(var), reduced[1], name="adamw_rows")
    repl4 = _adamw(repl_w, _pack_repl(mom), _pack_repl(var), repl_slots, name="adamw_replicated")
    for unpacked in ([_unpack_cols(a, (1,)) for a in cols4], [_unpack_rows(a, (1,)) for a in rows4], [_unpack_repl(a) for a in repl4]):
        for k in unpacked[0]:
            res[k] = tuple(u[k] for u in unpacked)
    stacked = {"a_w_r": (0, 3), "a_w_k": (1, 3), "a_w_v": (2, 3), "mlp_w_up0": (0, 2), "mlp_w_up1": (1, 2),
               "mlp_w_down0": (0, 2), "mlp_w_down1": (1, 2)}
    big4, earlier = {}, None
    for k, bw, bm, bv, slots in zip(_BIG_NAMES, big_w, _big_shards(mom), _big_shards(var), reduced[2:]):
        layer, n_layers = stacked.get(k, (0, 1))
        earlier = _adamw(bw, bm, bv, slots, name="adamw_" + k, layer=layer, n_layers=n_layers, into=earlier if layer else None)
        big4[k] = earlier
    res["a_w_rkv"] = tuple(a[None] for a in big4["a_w_v"])
    res["a_w_out"] = tuple(a[None] for a in big4["a_w_out"])
    res["mlp_w_up"], res["mlp_w_down"] = tuple(big4["mlp_w_up1"]), tuple(big4["mlp_w_down1"])
    res["b_w_q"] = tuple(a[None] for a in big4["b_w_q"])
    res["b_w_o"] = tuple(a[None] for a in big4["b_w_o"])
    res["w_kv"] = tuple(a.reshape(w_kv.shape) for a in res["w_kv"])

    outs = [loss, grad_x[None]]
    for i in range(4):
        outs += [res[n][i].reshape(given[n].shape) for n in _WEIGHTS]
    return tuple(outs)
```

```python
import functools
import math

import jax
import jax.numpy as jnp
from jax import lax
from jax.experimental import pallas as pl
from jax.experimental.pallas import tpu as pltpu

F32 = jnp.float32
BF16 = jnp.bfloat16

D_MODEL = 2048
N_HEADS = 32
HEAD_DIM = 64
N_KV_HEADS = 4
Q_PER_KV = 8
ATT_BLOCK = 128
WKV_CHUNK = 64
LORA_PAD = 128
D_FF = 8192
N_DEV = 8
RMS_EPS = 1e-6
GN_EPS = 64e-5
L2_EPS = 1e-12
ROPE_THETA = 10000.0
ADAM_LR, ADAM_B1, ADAM_B2, ADAM_EPS, ADAM_WD, ADAM_STEP = 0.001, 0.9, 0.999, 1e-08, 0.01, 10
MASK_VALUE = -1e30
VMEM_LIMIT_BYTES = 56 * 1024 * 1024
MESH_AXES = ("x", "y", "c")
HI = lax.Precision.HIGHEST

_NN = (((1,), (0,)), ((), ()))
_NT = (((1,), (1,)), ((), ()))
_TN = (((0,), (0,)), ((), ()))
_BNN = (((2,), (1,)), ((0,), (0,)))
_BNT = (((2,), (2,)), ((0,), (0,)))
_BTN = (((1,), (1,)), ((0,), (0,)))


def _params(sem):
    return pltpu.CompilerParams(dimension_semantics=sem, vmem_limit_bytes=VMEM_LIMIT_BYTES)


def _split2(a):
    hi = a.astype(BF16)
    return hi, (a - hi.astype(F32)).astype(BF16)


def _dot3(a, b, dims):
    ah, al = _split2(a)
    bh, bl = _split2(b)
    d = lambda p, q: lax.dot_general(p, q, dims, preferred_element_type=F32)
    return d(ah, bh) + (d(al, bh) + d(ah, bl))


@functools.partial(jax.custom_vjp, nondiff_argnums=(2,))
def _hdot(a, b, dims=_NN):
    return _dot3(a, b, dims)


def _hdot_fwd(a, b, dims):
    return _dot3(a, b, dims), (a, b)


def _hdot_bwd(dims, res, g):
    a, b = res
    nn, nt, tn = (_NN, _NT, _TN) if dims in (_NN, _NT, _TN) else (_BNN, _BNT, _BTN)
    if dims == nn:
        return _dot3(g, b, nt), _dot3(a, g, tn)
    if dims == nt:
        return _dot3(g, b, nn), _dot3(g, a, tn)
    assert dims == tn
    return _dot3(b, g, nt), _dot3(a, g, nn)


_hdot.defvjp(_hdot_fwd, _hdot_bwd)


def _tri_parts(x):
    hi = x.astype(BF16)
    r1 = x - hi.astype(F32)
    mid = r1.astype(BF16)
    return hi, mid, (r1 - mid.astype(F32)).astype(BF16)


@jax.custom_vjp
def _mask_dot(mask, x):
    mb = mask.astype(BF16)
    p0, p1, p2 = _tri_parts(x)
    d = lambda p: lax.dot_general(mb, p, _BNN, preferred_element_type=F32)
    return d(p0) + (d(p1) + d(p2))


def _mask_dot_fwd(mask, x):
    return _mask_dot(mask, x), mask


def _mask_dot_bwd(mask, g):
    mb = mask.astype(BF16)
    p0, p1, p2 = _tri_parts(g)
    d = lambda p: lax.dot_general(mb, p, _BTN, preferred_element_type=F32)
    return jnp.zeros_like(mask), d(p0) + (d(p1) + d(p2))


_mask_dot.defvjp(_mask_dot_fwd, _mask_dot_bwd)


def _b16dot(a, b, dims):
    return lax.dot_general(a.astype(BF16), b.astype(BF16), dims, preferred_element_type=F32)


@jax.custom_vjp
def _bdot(a, b):
    return _b16dot(a, b, _NN)


def _bdot_fwd(a, b):
    return _b16dot(a, b, _NN), (a, b)


def _bdot_bwd(res, g):
    a, b = res
    return _b16dot(g, b, _NT), _b16dot(a, g, _TN)


_bdot.defvjp(_bdot_fwd, _bdot_bwd)


@jax.custom_vjp
def _bdot_nt(a, b):
    return _b16dot(a, b, _NT)


def _bdot_nt_fwd(a, b):
    return _b16dot(a, b, _NT), (a, b)


def _bdot_nt_bwd(res, g):
    a, b = res
    return _b16dot(g, b, _NN), _b16dot(g, a, _TN)


_bdot_nt.defvjp(_bdot_nt_fwd, _bdot_nt_bwd)


def _rms(x, gain):
    return x * lax.rsqrt(jnp.mean(x * x, axis=-1, keepdims=True) + RMS_EPS) * gain


def _vjp_of(f, n_in, diff):
    def g(*args):
        ins, cts = args[:n_in], args[n_in:]

        def fd(*d):
            full = list(ins)
            for pos, i in enumerate(diff):
                full[i] = d[pos]
            return f(*full)

        _, pull = jax.vjp(fd, *[ins[i] for i in diff])
        return pull(tuple(cts))
    return g


def _mm(a, b, *, name, ta=False, tb=False, a_pro=None, epi=None, epi_args=(), out_dtype=F32,
        tm=1024, tn=1024, tk=2048, dims=None, b_spec=None, o_spec=None, o_shape=None):
    if dims is None:
        m, k = (a.shape[1], a.shape[0]) if ta else a.shape
        n = b.shape[0] if tb else b.shape[1]
    else:
        m, n, k = dims
    tm, tn, tk = min(tm, m), min(tn, n), min(tk, k)
    assert m % tm == 0 and n % tn == 0 and k % tk == 0, (name, m, n, k, tm, tn, tk)
    nk = k // tk
    ne = len(epi_args)
    cdims = (((0 if ta else 1,), (1 if tb else 0,)), ((), ()))

    def body(a_ref, b_ref, *rest):
        e_refs, o_ref, acc = rest[:ne], rest[ne], rest[ne + 1]
        kk = pl.program_id(2)

        @pl.when(kk == 0)
        def _():
            acc[...] = jnp.zeros_like(acc)

        av = a_ref[...]
        if a_pro is not None:
            av = a_pro(av.astype(F32))
        acc[...] += lax.dot_general(av.astype(BF16), b_ref[...].astype(BF16), cdims, preferred_element_type=F32)

        @pl.when(kk == nk - 1)
        def _():
            r = acc[...]
            if epi is not None:
                r = epi(r, *[e[...] for e in e_refs])
            o_ref[...] = r.astype(o_ref.dtype)

    a_spec = pl.BlockSpec((tk, tm), lambda i, j, q: (q, i)) if ta else pl.BlockSpec((tm, tk), lambda i, j, q: (i, q))
    if b_spec is None:
        b_spec = pl.BlockSpec((tn, tk), lambda i, j, q: (j, q)) if tb else pl.BlockSpec((tk, tn), lambda i, j, q: (q, j))
    if o_spec is None:
        o_spec = pl.BlockSpec((tm, tn), lambda i, j, q: (i, j))
        o_shape = (m, n)
    e_specs = [pl.BlockSpec((tm, tn), lambda i, j, q: (i, j)) for _ in epi_args]
    return pl.pallas_call(
        body, grid=(m // tm, n // tn, nk), in_specs=[a_spec, b_spec] + e_specs, out_specs=o_spec,
        out_shape=jax.ShapeDtypeStruct(o_shape, out_dtype), scratch_shapes=[pltpu.VMEM((tm, tn), F32)],
        compiler_params=_params(("parallel", "parallel", "arbitrary")), name=name,
    )(a, b, *epi_args)


def _rowwise(fn, rows, params, out_rows, out_params=(), *, tm=256, name):
    t = rows[0].shape[0]
    tm = min(tm, t)
    assert t % tm == 0
    nr, npar, nor, nop = len(rows), len(params), len(out_rows), len(out_params)

    def body(*refs):
        r, p = refs[:nr], refs[nr:nr + npar]
        o, op = refs[nr + npar:nr + npar + nor], refs[nr + npar + nor:]
        outs = fn(*[x[...] for x in r], *[x[...] for x in p])
        for ref, val in zip(o, outs[:nor]):
            ref[...] = val.astype(ref.dtype)
        if nop:
            @pl.when(pl.program_id(0) == 0)
            def _():
                for ref in op:
                    ref[...] = jnp.zeros_like(ref)

            for ref, val in zip(op, outs[nor:]):
                ref[...] += val.astype(F32)

    in_specs = [pl.BlockSpec((tm, x.shape[1]), lambda i: (i, 0)) for x in rows]
    in_specs += [pl.BlockSpec(p.shape, lambda i: (0, 0)) for p in params]
    out_specs = [pl.BlockSpec((tm, s.shape[1]), lambda i: (i, 0)) for s in out_rows]
    out_specs += [pl.BlockSpec(s.shape, lambda i: (0, 0)) for s in out_params]
    return pl.pallas_call(
        body, grid=(t // tm,), in_specs=in_specs, out_specs=out_specs, out_shape=list(out_rows) + list(out_params),
        compiler_params=_params(("arbitrary",)), name=name,
    )(*rows, *params)


def _sds(shape, dtype=F32):
    return jax.ShapeDtypeStruct(tuple(shape), dtype)


def _doubling_powers(l):
    powers = [l]
    for _ in range(int(math.log2(l.shape[-1])) - 1):
        powers.append(_dot3(powers[-1], powers[-1], _BNN))
    return powers


@jax.custom_vjp
def _unit_lower_solve(l, z):
    u = z
    for p in _doubling_powers(l):
        u = u + _dot3(p, u, _BNN)
    return u


def _unit_lower_solve_fwd(l, z):
    powers = _doubling_powers(l)
    u = z
    for p in powers:
        u = u + _dot3(p, u, _BNN)
    return u, (powers, u)


def _unit_lower_solve_bwd(res, du):
    powers, u = res
    g = du
    for p in powers:
        g = g + _dot3(p, g, _BTN)
    return _dot3(g, u, _BNT), g


_unit_lower_solve.defvjp(_unit_lower_solve_fwd, _unit_lower_solve_bwd)


def _wkv_chunk(s0, r, lw, k, v, a, b):
    nb, c, _ = r.shape
    ti = lax.broadcasted_iota(jnp.int32, (nb, c, c), 1)
    si = lax.broadcasted_iota(jnp.int32, (nb, c, c), 2)
    incl, strict = si <= ti, si < ti
    cum = _mask_dot(incl.astype(F32), lw)
    tot = jnp.sum(lw, axis=1, keepdims=True)
    rcum = tot - cum
    w_inv = jnp.exp(-cum)
    at, rt, bt, kt = a * jnp.exp(cum - lw), r * jnp.exp(cum), b * w_inv, k * w_inv
    l_ab = jnp.where(strict, _hdot(at, bt, _BNT), 0.0)
    l_ak = jnp.where(strict, _hdot(at, kt, _BNT), 0.0)
    t_rb = jnp.where(incl, _hdot(rt, bt, _BNT), 0.0)
    t_rk = jnp.where(incl, _hdot(rt, kt, _BNT), 0.0)
    u = _unit_lower_solve(l_ab, _hdot(at, s0, _BNT) + _hdot(l_ak, v, _BNN))
    y = _hdot(rt, s0, _BNT) + _hdot(t_rb, u, _BNN) + _hdot(t_rk, v, _BNN)
    e = jnp.exp(rcum)
    s1 = s0 * jnp.exp(tot) + _hdot(u, b * e, _BTN) + _hdot(v, k * e, _BTN)
    return y, s1


WKV_HEADS_PER_STEP = 16


def _first_and_last_step(grid):
    i, j = pl.program_id(0), pl.program_id(1)
    return jnp.logical_and(i == 0, j == 0), jnp.logical_and(i == grid[0] - 1, j == grid[1] - 1)


N_WKV_PARAMS = 7


def _tmix_chunk(s0, r, k, v, wl, al, w0, a0, k_k, k_a, ln_w, ln_b, r_k):
    lw, kmod, a, b = _wkv_prep(k, wl, al, w0, a0, k_k, k_a)
    y, s1 = _wkv_chunk(s0, r, lw, kmod, v, a, b)
    (y2,) = _wkv_post(y, r, kmod, v, ln_w, ln_b, r_k)
    return y2, s1


def _split_heads(x, nh):
    return x.reshape(x.shape[0], nh, HEAD_DIM).transpose(1, 0, 2)


def _merge_heads(xh):
    return jnp.concatenate([xh[h] for h in range(xh.shape[0])], axis=1)


def _wkv_fwd(r, k, v, wl, al, params, shards=()):
    t = r.shape[0]
    nh, n = N_HEADS, HEAD_DIM
    nc = t // WKV_CHUNK
    hb = WKV_HEADS_PER_STEP
    grid = (nh // hb, nc)
    ns = len(shards)
    n_in = 5 + N_WKV_PARAMS

    def body(*refs):
        y_ref, s_ref = refs[n_in + ns:n_in + ns + 2]
        state = refs[n_in + 2 * ns + 2]
        if ns:
            start, relay, finish = _gather_plan(refs[n_in:n_in + ns], refs[n_in + ns + 2:n_in + 2 * ns + 2], *refs[n_in + 2 * ns + 3:])
            first, last = _first_and_last_step(grid)
            pl.when(first)(start)
            relay_step = (5 * grid[0] * nc) // 8
            pl.when(jnp.logical_and(pl.program_id(0) == relay_step // nc, pl.program_id(1) == relay_step % nc))(relay)

        @pl.when(pl.program_id(1) == 0)
        def _():
            state[...] = jnp.zeros_like(state)

        s0 = state[...]
        s_ref[:, 0] = s0
        rows = [_split_heads(ref[...], hb) for ref in refs[:5]]
        y, s1 = _tmix_chunk(s0, *rows, *[ref[...] for ref in refs[5:n_in]])
        y_ref[...] = _merge_heads(y)
        state[...] = s1
        if ns:
            pl.when(last)(finish)

    blk = pl.BlockSpec((WKV_CHUNK, hb * n), lambda h, c: (c, h))
    pblk = pl.BlockSpec((hb, 1, n), lambda h, c: (h, 0, 0))
    sblk = pl.BlockSpec((hb, 1, n, n), lambda h, c: (h, c, 0, 0))
    outs = pl.pallas_call(
        body, grid=grid, in_specs=[blk] * 5 + [pblk] * N_WKV_PARAMS + [_ANY] * ns, out_specs=[blk, sblk] + [_ANY] * ns,
        out_shape=[_sds((t, nh * n)), _sds((nh, nc, n, n))] + [_sds((N_DEV,) + s.shape, s.dtype) for s in shards],
        scratch_shapes=[pltpu.VMEM((hb, n, n), F32)] + (_gather_semaphores(ns) if ns else []),
        compiler_params=_params(("arbitrary", "arbitrary")), name="wkv_fwd",
    )(r, k, v, wl, al, *params, *shards)
    return outs[0], outs[1], list(outs[2:])


def _wkv_bwd(r, k, v, wl, al, params, states, dy, chip_sums=()):
    t = r.shape[0]
    nh, n = N_HEADS, HEAD_DIM
    nc = t // WKV_CHUNK
    hb = WKV_HEADS_PER_STEP
    grid = (nh // hb, nc)
    ns = len(chip_sums)
    n_in = 5 + N_WKV_PARAMS
    n_out = 5 + N_WKV_PARAMS

    def body(*refs):
        s_ref, dy_ref = refs[n_in:n_in + 2]
        out_refs = refs[n_in + 2 + ns:n_in + 2 + ns + n_out]
        dstate = refs[n_in + 2 + 2 * ns + n_out]
        if ns:
            start, finish = _chip_exchange_plan(refs[n_in + 2:n_in + 2 + ns], refs[n_in + 2 + ns + n_out:n_in + 2 + 2 * ns + n_out],
                                                *refs[n_in + 3 + 2 * ns + n_out:])
            first, last = _first_and_last_step(grid)
            pl.when(first)(start)

        @pl.when(pl.program_id(1) == 0)
        def _():
            dstate[...] = jnp.zeros_like(dstate)
            for ref in out_refs[5:]:
                ref[...] = jnp.zeros_like(ref)

        rows = [_split_heads(ref[...], hb) for ref in refs[:5]]
        _, pull = jax.vjp(_tmix_chunk, s_ref[:, 0], *rows, *[ref[...] for ref in refs[5:n_in]])
        grads = pull((_split_heads(dy_ref[...], hb), dstate[...]))
        dstate[...] = grads[0]
        for ref, val in zip(out_refs[:5], grads[1:6]):
            ref[...] = _merge_heads(val).astype(ref.dtype)
        for ref, val in zip(out_refs[5:], grads[6:]):
            ref[...] += val
        if ns:
            pl.when(last)(finish)

    blk = pl.BlockSpec((WKV_CHUNK, hb * n), lambda h, c: (nc - 1 - c, h))
    pblk = pl.BlockSpec((hb, 1, n), lambda h, c: (h, 0, 0))
    sblk = pl.BlockSpec((hb, 1, n, n), lambda h, c: (h, nc - 1 - c, 0, 0))
    outs = pl.pallas_call(
        body, grid=grid, in_specs=[blk] * 5 + [pblk] * N_WKV_PARAMS + [sblk, blk] + [_ANY] * ns,
        out_specs=[blk] * 5 + [pblk] * N_WKV_PARAMS + [_ANY] * ns,
        out_shape=[_sds((t, nh * n), BF16)] * 5 + [_sds((nh, 1, n))] * N_WKV_PARAMS + [_sds(p.shape, p.dtype) for p in chip_sums],
        scratch_shapes=[pltpu.VMEM((hb, n, n), F32)] + (_chip_exchange_semaphores(ns) if ns else []),
        compiler_params=_params(("arbitrary", "arbitrary")), name="wkv_bwd",
    )(r, k, v, wl, al, *params, states, dy, *chip_sums)
    return outs[:5], outs[5:n_out], list(outs[n_out:])


def _wkv_prep(k, wl, al, w0, a0, k_k, k_a):
    z = -(w0 + wl)
    softplus = jnp.maximum(z, 0.0) + jnp.log1p(jnp.exp(-jnp.abs(z)))
    lw = -jnp.exp(-softplus - 0.5)
    asig = jax.nn.sigmoid(a0 + al)
    kk = k * k_k
    kk = kk / jnp.maximum(jnp.sqrt(jnp.sum(kk * kk, axis=-1, keepdims=True)), L2_EPS)
    kmod = k * (1.0 + (asig - 1.0) * k_a)
    return lw, kmod, -kk, kk * asig


def _wkv_post(y, r, kmod, v, ln_w, ln_b, r_k):
    mu = jnp.mean(y, axis=-1, keepdims=True)
    var = jnp.mean(jnp.square(y - mu), axis=-1, keepdims=True)
    yn = (y - mu) * lax.rsqrt(var + GN_EPS)
    yn = yn * ln_w + ln_b
    return (yn + jnp.sum(r * kmod * r_k, axis=-1, keepdims=True) * v,)


def _attn_group(nonzero_block, q, kc, kp, vc, vp, cos_c, sin_c, cos_p, sin_p, q_gain, k_gain, sinks):
    ri = lax.broadcasted_iota(jnp.int32, (HEAD_DIM, HEAD_DIM), 0)
    ci = lax.broadcasted_iota(jnp.int32, (HEAD_DIM, HEAD_DIM), 1)
    half = HEAD_DIM // 2
    rot = jnp.where(ri == ci + half, -1.0, 0.0) + jnp.where(ri + half == ci, 1.0, 0.0)
    rows = Q_PER_KV * ATT_BLOCK

    def rope(x, cos, sin):
        return x * cos + _hdot(x, rot, _NN) * sin

    kcr = rope(_rms(kc, k_gain), cos_c, sin_c)
    kpr = rope(_rms(kp, k_gain), cos_p, sin_p)
    qn = _rms(q, q_gain)
    qr = qn * cos_c + _hdot(qn.reshape(rows, HEAD_DIM), rot, _NN).reshape(q.shape) * sin_c
    q2 = qr.reshape(rows, HEAD_DIM)
    qi = lax.broadcasted_iota(jnp.int32, (1, ATT_BLOCK, ATT_BLOCK), 1)
    ki = lax.broadcasted_iota(jnp.int32, (1, ATT_BLOCK, ATT_BLOCK), 2)
    mask_c = ki <= qi
    mask_p = jnp.logical_and(ki > qi, nonzero_block)
    lane0 = (lax.broadcasted_iota(jnp.int32, (1, 1, 128), 2) == 0).astype(F32)
    shape3 = (Q_PER_KV, ATT_BLOCK, ATT_BLOCK)
    sc = jnp.where(mask_c, (_bdot_nt(q2, kcr) * (HEAD_DIM ** -0.5)).reshape(shape3), MASK_VALUE)
    sp = jnp.where(mask_p, (_bdot_nt(q2, kpr) * (HEAD_DIM ** -0.5)).reshape(shape3), MASK_VALUE)
    sk = jnp.sum(sinks * lane0, axis=2, keepdims=True)
    mx = jnp.maximum(jnp.maximum(jnp.max(sc, axis=2, keepdims=True), jnp.max(sp, axis=2, keepdims=True)), sk)
    mx = lax.stop_gradient(mx)
    ec, ep = jnp.exp(sc - mx), jnp.exp(sp - mx)
    den = jnp.sum(ec, axis=2, keepdims=True) + jnp.sum(ep, axis=2, keepdims=True) + jnp.exp(sk - mx)
    out = _bdot((ec / den).reshape(rows, ATT_BLOCK), vc) + _bdot((ep / den).reshape(rows, ATT_BLOCK), vp)
    return out.reshape(q.shape)


def _attn_rows(nonzero_block, qs, kcs, kps, vcs, vps, tabs, q_gain, k_gain, sinks):
    return [_attn_group(nonzero_block, qs[g], kcs[g], kps[g], vcs[g], vps[g], *tabs, q_gain, k_gain, sinks[g])
            for g in range(N_KV_HEADS)]


def _attn_operands(q_tile, kvc_tile, kvp_tile, sink_ref):
    q3 = _split_heads(q_tile, N_HEADS)
    kvc, kvp = _split_heads(kvc_tile, 2 * N_KV_HEADS), _split_heads(kvp_tile, 2 * N_KV_HEADS)
    groups = range(N_KV_HEADS)
    qs = [q3[Q_PER_KV * g:Q_PER_KV * (g + 1)] for g in groups]
    sinks = [sink_ref[Q_PER_KV * g:Q_PER_KV * (g + 1)] for g in groups]
    return (qs, [kvc[g] for g in groups], [kvp[g] for g in groups], [kvc[N_KV_HEADS + g] for g in groups],
            [kvp[N_KV_HEADS + g] for g in groups], sinks)


def _attn_specs(t):
    nb = t // ATT_BLOCK
    prev = lambda n: jnp.maximum(n - 1, 0)
    kv_width = 2 * N_KV_HEADS * HEAD_DIM
    q_spec = pl.BlockSpec((ATT_BLOCK, D_MODEL), lambda n: (n, 0))
    kv_c = pl.BlockSpec((ATT_BLOCK, kv_width), lambda n: (n, 0))
    kv_p = pl.BlockSpec((ATT_BLOCK, kv_width), lambda n: (prev(n), 0))
    tab_c = pl.BlockSpec((ATT_BLOCK, HEAD_DIM), lambda n: (n, 0))
    tab_p = pl.BlockSpec((ATT_BLOCK, HEAD_DIM), lambda n: (prev(n), 0))
    gain = pl.BlockSpec((1, HEAD_DIM), lambda n: (0, 0))
    sink = pl.BlockSpec((N_HEADS, 1, 128), lambda n: (0, 0, 0))
    return nb, q_spec, kv_c, kv_p, tab_c, tab_p, gain, sink


def _attn_fwd(q, kv, cos, sin, q_gain, k_gain, sinks):
    t = q.shape[0]
    nb, q_spec, kv_c, kv_p, tab_c, tab_p, gain, sink = _attn_specs(t)

    def body(q_ref, kvc, kvp, cc, sc, cp, sp, qg, kg, sk, o_ref):
        qs, kcs, kps, vcs, vps, sinks_ = _attn_operands(q_ref[...], kvc[...], kvp[...], sk)
        outs = _attn_rows(pl.program_id(0) > 0, qs, kcs, kps, vcs, vps, (cc[...], sc[...], cp[...], sp[...]),
                          qg[...], kg[...], sinks_)
        o_ref[...] = _merge_heads(jnp.concatenate(outs, axis=0)).astype(o_ref.dtype)

    return pl.pallas_call(
        body, grid=(nb,), in_specs=[q_spec, kv_c, kv_p, tab_c, tab_c, tab_p, tab_p, gain, gain, sink],
        out_specs=q_spec, out_shape=_sds(q.shape, BF16), compiler_params=_params(("arbitrary",)), name="attn_fwd",
    )(q, kv, kv, cos, sin, cos, sin, q_gain, k_gain, sinks)


def _attn_bwd(q, kv, cos, sin, q_gain, k_gain, sinks, do, parts=()):
    t = q.shape[0]
    nb, q_spec, kv_c, kv_p, tab_c, tab_p, gain, sink = _attn_specs(t)
    ns = len(parts)

    def body(*refs):
        q_ref, kvc, kvp, cc, sc, cp, sp, qg, kg, sk, do_ref = refs[:11]
        dq_ref, dkvc_ref, dkvp_ref, dqg_ref, dkg_ref, dsk_ref = refs[11 + ns:17 + ns]
        if ns:
            start, finish = _sibling_exchange_plan(refs[11:11 + ns], refs[17 + ns:17 + 2 * ns], *refs[17 + 2 * ns:])
            pl.when(pl.program_id(0) == 0)(start)
        nonzero = pl.program_id(0) > 0
        tabs = (cc[...], sc[...], cp[...], sp[...])
        qs, kcs, kps, vcs, vps, sinks_ = _attn_operands(q_ref[...], kvc[...], kvp[...], sk)

        def f(qs_, kcs_, kps_, vcs_, vps_, qgv, kgv, sks):
            return _attn_rows(nonzero, qs_, kcs_, kps_, vcs_, vps_, tabs, qgv, kgv, sks)

        _, pull = jax.vjp(f, qs, kcs, kps, vcs, vps, qg[...], kg[...], sinks_)
        do3 = _split_heads(do_ref[...], N_HEADS)
        dqs, dkcs, dkps, dvcs, dvps, dqg, dkg, dsks = pull([do3[Q_PER_KV * g:Q_PER_KV * (g + 1)] for g in range(N_KV_HEADS)])
        dq_ref[...] = _merge_heads(jnp.concatenate(dqs, axis=0)).astype(dq_ref.dtype)
        dkvc_ref[...] = jnp.concatenate(dkcs + dvcs, axis=1)
        dkvp_ref[...] = jnp.concatenate(dkps + dvps, axis=1)

        @pl.when(pl.program_id(0) == 0)
        def _():
            dqg_ref[...] = jnp.zeros_like(dqg_ref)
            dkg_ref[...] = jnp.zeros_like(dkg_ref)
            dsk_ref[...] = jnp.zeros_like(dsk_ref)

        dqg_ref[...] += dqg
        dkg_ref[...] += dkg
        for g in range(N_KV_HEADS):
            dsk_ref[Q_PER_KV * g:Q_PER_KV * (g + 1)] += dsks[g]
        if ns:
            pl.when(pl.program_id(0) == nb - 1)(finish)

    outs = pl.pallas_call(
        body, grid=(nb,), in_specs=[q_spec, kv_c, kv_p, tab_c, tab_c, tab_p, tab_p, gain, gain, sink, q_spec] + [_ANY] * ns,
        out_specs=[q_spec, kv_c, kv_c, gain, gain, sink] + [_ANY] * ns,
        out_shape=[_sds(q.shape, BF16), _sds(kv.shape), _sds(kv.shape), _sds((1, HEAD_DIM)), _sds((1, HEAD_DIM)),
                   _sds(sinks.shape)] + _sibling_exchange_shapes(parts),
        scratch_shapes=_sibling_exchange_semaphores(ns) if ns else [],
        compiler_params=_params(("arbitrary",)), name="attn_bwd",
    )(q, kv, kv, cos, sin, cos, sin, q_gain, k_gain, sinks, do, *parts)
    return outs[:6], list(outs[6:])


def _time_shift_lerps(x, xs, gain, *mix):
    xn, xsn = _rms(x, gain), _rms(xs, gain)
    xx = xsn - xn
    return tuple(xn + xx * m for m in mix)


def _residual_norm(h, delta, gain):
    hn = h + delta
    return hn, _rms(hn, gain)


def _residual_norm2(h, delta, gain_a, gain_b):
    hn = h + delta
    return hn, _rms(hn, gain_a), _rms(hn, gain_b)


def _relu2(u):
    return jnp.square(jnp.maximum(u, 0.0))


def _sigmoid(z):
    return jax.nn.sigmoid(z)


def _shift_down(x):
    return jnp.pad(x[:-1], ((1, 0), (0, 0)))


def _shift_up(x):
    return jnp.pad(x[1:], ((0, 1), (0, 0)))


def _rope_tables(t):
    half = HEAD_DIM // 2
    inv_freq = jnp.power(ROPE_THETA, -jnp.arange(half, dtype=F32) / half)
    ang = jnp.arange(t, dtype=jnp.int32).astype(F32)[:, None] * inv_freq[None, :]
    cos, sin = jnp.cos(ang), jnp.sin(ang)
    return jnp.concatenate([cos, cos], axis=1), jnp.concatenate([sin, sin], axis=1)


def _mlp_fwd(hn, w_up, w_down, layer, up_dev_major):
    t = hn.shape[0]
    if up_dev_major:
        cw = w_up.shape[2]
        u = _mm(hn, w_up, name=f"mlp{layer}_up", dims=(t, D_FF, D_MODEL), tn=cw, tk=D_MODEL,
                b_spec=pl.BlockSpec((None, D_MODEL, cw), lambda i, j, q: (j, q, 0)))
    else:
        u = _mm(hn, w_up, name=f"mlp{layer}_up")
    out = _mm(u, w_down, a_pro=_relu2, name=f"mlp{layer}_down")
    return u, out


PAIR_TILE = 1024


def _mm_pair_reduce(a, b, *, device_axis, a_pro=None, name):
    t = a.shape[0]
    tile = PAIR_TILE
    n_other = D_MODEL // tile
    n_tiles = N_CHIPS * n_other
    core = lax.axis_index("c").astype(jnp.int32).reshape(1)
    shape = (N_CHIPS, D_MODEL, tile) if device_axis == "n" else (N_CHIPS, tile, D_MODEL)

    def body(core_ref, a_ref, b_ref, out_ref, recv_hbm, send_buf, recv_tile, send_sems, recv_sems, local_sem):
        phase, q, o = pl.program_id(0), pl.program_id(1), pl.program_id(2)
        sibling = (lax.axis_index("x"), lax.axis_index("y"), 1 - core_ref[0])
        tile_no = q * n_other + o
        slot = lax.rem(tile_no, 2)
        av = a_ref[...]
        if a_pro is not None:
            av = a_pro(av.astype(F32))
        acc = lax.dot_general(av.astype(BF16), b_ref[...].astype(BF16), _TN, preferred_element_type=F32)

        def send(number, buffer):
            return pltpu.make_async_remote_copy(
                src_ref=send_buf.at[buffer], dst_ref=recv_hbm.at[number // n_other, lax.rem(number, n_other)],
                send_sem=send_sems.at[number], recv_sem=recv_sems.at[number], device_id=sibling, device_id_type=_MESH_ID)

        @pl.when(phase == 0)
        def _():
            @pl.when(tile_no >= 2)
            def _():
                send(tile_no - 2, slot).wait_send()

            send_buf[slot] = acc
            send(tile_no, slot).start()

        @pl.when(phase == 1)
        def _():
            @pl.when(tile_no == 0)
            def _():
                send(n_tiles - 2, 0).wait_send()
                send(n_tiles - 1, 1).wait_send()

            send(tile_no, slot).wait_recv()
            landed = pltpu.make_async_copy(recv_hbm.at[q, o], recv_tile, local_sem)
            landed.start()
            landed.wait()
            out_ref[...] = (acc + recv_tile[...]).astype(out_ref.dtype)

    def device(phase, q, core_ref):
        return 2 * q + jnp.where(phase == 0, 1 - core_ref[0], core_ref[0])

    if device_axis == "n":
        a_spec = pl.BlockSpec((t, tile), lambda p, q, o, c: (0, o))
        b_spec = pl.BlockSpec((t, tile), lambda p, q, o, c: (0, device(p, q, c)))
        o_spec = pl.BlockSpec((None, tile, tile), lambda p, q, o, c: (jnp.where(p == 0, 0, q), jnp.where(p == 0, 0, o), 0))
    else:
        a_spec = pl.BlockSpec((t, tile), lambda p, q, o, c: (0, device(p, q, c)))
        b_spec = pl.BlockSpec((t, tile), lambda p, q, o, c: (0, o))
        o_spec = pl.BlockSpec((None, tile, tile), lambda p, q, o, c: (jnp.where(p == 0, 0, q), 0, jnp.where(p == 0, 0, o)))
    grid_spec = pltpu.PrefetchScalarGridSpec(
        num_scalar_prefetch=1, grid=(2, N_CHIPS, n_other), in_specs=[a_spec, b_spec], out_specs=[o_spec, _ANY],
        scratch_shapes=[pltpu.VMEM((2, tile, tile), F32), pltpu.VMEM((tile, tile), F32),
                        pltpu.SemaphoreType.DMA((n_tiles,)), pltpu.SemaphoreType.DMA((n_tiles,)), pltpu.SemaphoreType.DMA(())])
    sums, _ = pl.pallas_call(
        body, grid_spec=grid_spec, out_shape=[_sds(shape, BF16), _sds((N_CHIPS, n_other, tile, tile))],
        compiler_params=_params(("arbitrary", "arbitrary", "arbitrary")), name=name,
    )(core, a, b)
    return sums


def _mlp_bwd(hn, u, dh, w_up, w_down, layer, up_dev_major, pair_reduce=False):
    t = hn.shape[0]
    du = _mm(dh, w_down, tb=True, epi=lambda r, uu: r * (2.0 * jnp.maximum(uu, 0.0)), epi_args=(u,), out_dtype=BF16,
             name=f"mlp{layer}_du")
    if pair_reduce:
        cw = w_up.shape[2]
        d_down = _mm_pair_reduce(u, dh, device_axis="m", a_pro=_relu2, name=f"mlp{layer}_ddown")
        d_up = _mm_pair_reduce(hn, du, device_axis="n", name=f"mlp{layer}_dup")
        dhn = _mm(du, w_up, tb=True, name=f"mlp{layer}_dhn", dims=(t, D_MODEL, D_FF), tk=cw,
                  b_spec=pl.BlockSpec((None, 1024, cw), lambda i, j, q: (q, j, 0)))
        return dhn, d_up, d_down
    d_down = _mm(u, dh, ta=True, a_pro=_relu2, name=f"mlp{layer}_ddown")
    if up_dev_major:
        cw = w_up.shape[2]
        d_up = _mm(hn, du, ta=True, name=f"mlp{layer}_dup", dims=(D_MODEL, D_FF, t), tn=cw,
                   o_spec=pl.BlockSpec((None, 1024, cw), lambda i, j, q: (j, i, 0)), o_shape=(N_DEV, D_MODEL, cw))
        dhn = _mm(du, w_up, tb=True, name=f"mlp{layer}_dhn", dims=(t, D_MODEL, D_FF), tk=cw,
                  b_spec=pl.BlockSpec((None, 1024, cw), lambda i, j, q: (q, j, 0)))
    else:
        d_up = _mm(hn, du, ta=True, name=f"mlp{layer}_dup")
        dhn = _mm(du, w_up, tb=True, name=f"mlp{layer}_dhn")
    return dhn, d_up, d_down


_LATE_WEIGHTS = ("mlp_w_up0", "mlp_w_up1", "mlp_w_down0", "mlp_w_down1", "b_w_q", "b_w_o")
_LATE_GATHER = _LATE_WEIGHTS + ("a_w_out", "a_g1", "w_kv")
_EARLY_GRADS = _LATE_WEIGHTS + ("a_w_out",)
_EARLY_GRADS_A = ("b_w_o",)
_EARLY_GRADS_B = ("b_w_q", "a_w_out")


def _local_step(x, target, w, late_shards=None, up_dev_major=True):
    t = x.shape[0]
    g = {}
    w = dict(w)
    row = lambda: _sds((t, D_MODEL))
    rowb = lambda: _sds((t, D_MODEL), BF16)
    vec = lambda: _sds((1, D_MODEL))

    xs = _shift_down(x)
    mix = [w["a_mix"][i:i + 1] for i in range(6)]
    xr, xk, xv, xw, xa, xg = _rowwise(_time_shift_lerps, [x, xs], [w["a_norm"]] + mix, [rowb()] * 6, tm=256, name="tmix_lerp")
    r = _mm(xr, w["a_w_r"], name="tmix_r")
    k = _mm(xk, w["a_w_k"], name="tmix_k")
    v = _mm(xv, w["a_w_v"], name="tmix_v")
    lw1 = _mm(xw, w["a_w1"], name="tmix_w1")
    wl = _mm(lw1, w["a_w2"], a_pro=jnp.tanh, name="tmix_w2")
    la1 = _mm(xa, w["a_a1"], name="tmix_a1")
    al = _mm(la1, w["a_a2"], name="tmix_a2")

    hv = lambda name: w[name].reshape(N_HEADS, 1, HEAD_DIM)
    wkv_params = [hv(name) for name in ("a_w0", "a_a0", "a_k_k", "a_k_a", "a_ln_x_w", "a_ln_x_b", "a_r_k")]
    y2, states, gathered = _wkv_fwd(r, k, v, wl, al, wkv_params, late_shards or ())
    for name, arr in zip(_LATE_GATHER, gathered):
        w[name] = arr if name.startswith("mlp_w_up") else arr.reshape(N_DEV * arr.shape[1], arr.shape[2])
    lg1 = _mm(xg, w["a_g1"], name="tmix_g1")
    gate = _mm(lg1, w["a_g2"], a_pro=_sigmoid, name="tmix_g2")
    (yg,) = _rowwise(lambda a, b: (a * b,), [y2, gate], [], [rowb()], name="tmix_gate")
    att = _mm(yg, w["a_w_out"], name="tmix_out")

    h1, hn0 = _rowwise(_residual_norm, [x, att], [w["mlp_norm0"]], [row(), rowb()], name="res_norm0")
    u0, m0 = _mlp_fwd(hn0, w["mlp_w_up0"], w["mlp_w_down0"], 0, up_dev_major)

    h2, kvn, qn = _rowwise(_residual_norm2, [h1, m0], [w["kv_norm"], w["b_norm"]], [row(), rowb(), rowb()], name="res_norm_kvq")
    kv = _mm(kvn, w["w_kv"], name="kv_proj")
    q = _mm(qn, w["b_w_q"], name="q_proj")
    cos, sin = _rope_tables(t)
    sinks = jnp.broadcast_to(w["b_sinks"].reshape(N_HEADS, 1, 1), (N_HEADS, 1, 128))
    o = _attn_fwd(q, kv, cos, sin, w["b_q_norm"], w["k_norm"], sinks)
    att2 = _mm(o, w["b_w_o"], name="attn_out")

    h3, hn1 = _rowwise(_residual_norm, [h2, att2], [w["mlp_norm1"]], [row(), rowb()], name="res_norm1")
    u1, m1 = _mlp_fwd(hn1, w["mlp_w_up1"], w["mlp_w_down1"], 1, up_dev_major)

    def loss_fn(h, m, tg):
        diff = (h + m) - tg
        part = 0.5 * jnp.sum(jnp.mean(jnp.square(diff), axis=-1, keepdims=True), axis=0, keepdims=True)
        dh = diff * (1.0 / D_MODEL)
        return dh, dh, jnp.broadcast_to(part, (1, 128))

    dh4, dh4b, loss = _rowwise(loss_fn, [h3, m1, target], [], [row(), rowb()], [_sds((1, 128))], name="loss")

    def res_norm_bwd(h, dnext, dhn, gain):
        dh, dgain = _vjp_of(lambda hh, gg: (_rms(hh, gg),), 2, (0, 1))(h, gain, dhn)
        return dnext + dh, dnext + dh, dgain

    distributed = late_shards is not None
    dhn1, g["mlp_w_up1"], g["mlp_w_down1"] = _mlp_bwd(hn1, u1, dh4b, w["mlp_w_up1"], w["mlp_w_down1"], 1, up_dev_major, distributed)
    dh3, dh3b, g["mlp_norm1"] = _rowwise(res_norm_bwd, [h3, dh4, dhn1], [w["mlp_norm1"]], [row(), rowb()], [vec()], name="res_norm1_bwd")

    g["b_w_o"] = _mm(o, dh3b, ta=True, name="attn_out_dw")
    do = _mm(dh3b, w["b_w_o"], tb=True, name="attn_out_dx")
    parts_a = [_device_major(name, g[name]) for name in _EARLY_GRADS_A] if late_shards is not None else []
    (dq, dkv_own, dkv_prev, g["b_q_norm"], g["k_norm"], dsinks), from_sibling_a = _attn_bwd(
        q, kv, cos, sin, w["b_q_norm"], w["k_norm"], sinks, do, parts_a)
    g["b_sinks"] = dsinks[:, 0, 0].reshape(1, N_HEADS)
    g["b_w_q"] = _mm(qn, dq, ta=True, name="q_proj_dw")
    dqn = _mm(dq, w["b_w_q"], tb=True, name="q_proj_dx")
    dkv_prev = jnp.pad(dkv_prev[ATT_BLOCK:], ((0, ATT_BLOCK), (0, 0)))
    (dkv,) = _rowwise(lambda a, b: (a + b,), [dkv_own, dkv_prev], [], [_sds(kv.shape, BF16)], name="kv_grad_sum")
    g["w_kv"] = _mm(kvn, dkv, ta=True, name="kv_proj_dw")
    dkvn = _mm(dkv, w["w_kv"], tb=True, name="kv_proj_dx")

    def res_norm2_bwd(h, dnext, dna, dnb, gain_a, gain_b):
        dha, dga = _vjp_of(lambda hh, gg: (_rms(hh, gg),), 2, (0, 1))(h, gain_a, dna)
        dhb, dgb = _vjp_of(lambda hh, gg: (_rms(hh, gg),), 2, (0, 1))(h, gain_b, dnb)
        return dnext + dha + dhb, dnext + dha + dhb, dga, dgb

    dh2, dh2b, g["kv_norm"], g["b_norm"] = _rowwise(res_norm2_bwd, [h2, dh3, dkvn, dqn], [w["kv_norm"], w["b_norm"]],
                                                    [row(), rowb()], [vec(), vec()], name="res_norm_kvq_bwd")

    dhn0, g["mlp_w_up0"], g["mlp_w_down0"] = _mlp_bwd(hn0, u0, dh2b, w["mlp_w_up0"], w["mlp_w_down0"], 0, up_dev_major, distributed)
    dh1, dh1b, g["mlp_norm0"] = _rowwise(res_norm_bwd, [h1, dh2, dhn0], [w["mlp_norm0"]], [row(), rowb()], [vec()], name="res_norm0_bwd")

    g["a_w_out"] = _mm(yg, dh1b, ta=True, name="tmix_out_dw")
    dyg = _mm(dh1b, w["a_w_out"], tb=True, name="tmix_out_dx")
    dy2, dgate = _rowwise(lambda d, a, b: (d * b, d * a), [dyg, y2, gate], [], [row(), rowb()], name="tmix_gate_bwd")
    g["a_g2"] = _mm(lg1, dgate, ta=True, a_pro=_sigmoid, name="tmix_g2_dw")

    def dsigmoid(rr, z):
        s = jax.nn.sigmoid(z)
        return rr * s * (1.0 - s)

    dlg1 = _mm(dgate, w["a_g2"], tb=True, epi=dsigmoid, epi_args=(lg1,), out_dtype=BF16, name="tmix_g2_dx")
    g["a_g1"] = _mm(xg, dlg1, ta=True, name="tmix_g1_dw")
    dxg = _mm(dlg1, w["a_g1"], tb=True, name="tmix_g1_dx")

    early_sums = ()
    if late_shards is not None:
        sums = dict(zip(_EARLY_GRADS_A, _pair_sums(parts_a, from_sibling_a, _EARLY_GRADS_A)))
        sums.update(zip(_EARLY_GRADS_B, _chip_sums([_device_major(name, g[name]) for name in _EARLY_GRADS_B], _EARLY_GRADS_B, "early")))
        early_sums = [sums[name] if name in sums else g[name] for name in _EARLY_GRADS]
    (dr, dk, dv, dwl, dal), param_grads, early_reduced = _wkv_bwd(r, k, v, wl, al, wkv_params, states, dy2, early_sums)
    for name, pg in zip(("a_w0", "a_a0", "a_k_k", "a_k_a", "a_ln_x_w", "a_ln_x_b", "a_r_k"), param_grads):
        g[name] = pg.reshape(1, D_MODEL)

    g["a_w_r"] = _mm(xr, dr, ta=True, name="tmix_r_dw")
    g["a_w_k"] = _mm(xk, dk, ta=True, name="tmix_k_dw")
    g["a_w_v"] = _mm(xv, dv, ta=True, name="tmix_v_dw")
    dxr = _mm(dr, w["a_w_r"], tb=True, name="tmix_r_dx")
    dxk = _mm(dk, w["a_w_k"], tb=True, name="tmix_k_dx")
    dxv = _mm(dv, w["a_w_v"], tb=True, name="tmix_v_dx")
    g["a_w2"] = _mm(lw1, dwl, ta=True, a_pro=jnp.tanh, name="tmix_w2_dw")

    def dtanh(rr, z):
        th = jnp.tanh(z)
        return rr * (1.0 - th * th)

    dlw1 = _mm(dwl, w["a_w2"], tb=True, epi=dtanh, epi_args=(lw1,), out_dtype=BF16, name="tmix_w2_dx")
    g["a_w1"] = _mm(xw, dlw1, ta=True, name="tmix_w1_dw")
    dxw = _mm(dlw1, w["a_w1"], tb=True, name="tmix_w1_dx")
    g["a_a2"] = _mm(la1, dal, ta=True, name="tmix_a2_dw")
    dla1 = _mm(dal, w["a_a2"], tb=True, out_dtype=BF16, name="tmix_a2_dx")
    g["a_a1"] = _mm(xa, dla1, ta=True, name="tmix_a1_dw")
    dxa = _mm(dla1, w["a_a1"], tb=True, name="tmix_a1_dx")

    lerp_bwd = _vjp_of(_time_shift_lerps, 9, tuple(range(9)))

    def lerp_bwd_rows(x_, xs_, d0, d1, d2, d3, d4, d5, gain, *mx):
        return lerp_bwd(x_, xs_, gain, *mx, d0, d1, d2, d3, d4, d5)

    outs = _rowwise(lerp_bwd_rows, [x, xs, dxr, dxk, dxv, dxw, dxa, dxg], [w["a_norm"]] + mix, [row(), row()], [vec()] * 7,
                    tm=128, name="tmix_lerp_bwd")
    dx_a, dxs, g["a_norm"] = outs[0], outs[1], outs[2]
    g["a_mix"] = jnp.concatenate(outs[3:9], axis=0)
    (grad_x,) = _rowwise(lambda a, b, c: (a + b + c,), [dh1, dx_a, _shift_up(dxs)], [], [row()], name="grad_x_sum")
    return loss, grad_x, g, (early_reduced if late_shards is not None else None)


def _device_major(name, grad):
    return grad if name.startswith("mlp_w_up") else grad.reshape((N_DEV, grad.shape[0] // N_DEV, grad.shape[1]))


_ANY = pl.BlockSpec(memory_space=pl.ANY)
_MESH_ID = pl.DeviceIdType.MESH
N_PEERS = N_DEV - 1


def _linear(pos):
    return 4 * pos[0] + 2 * pos[1] + pos[2]


def _all_gather(shards, name):
    n = len(shards)

    def body(*refs):
        start, relay, finish = _gather_plan(refs[:n], refs[n:2 * n], *refs[2 * n:])
        start()
        relay()
        finish()

    return pl.pallas_call(
        body, out_shape=[_sds((N_DEV,) + s.shape, s.dtype) for s in shards], in_specs=[_ANY] * n, out_specs=[_ANY] * n,
        scratch_shapes=_gather_semaphores(n), name=name,
    )(*shards)


GATHER_COPIES = 8


def _gather_semaphores(n):
    return [pltpu.SemaphoreType.DMA((n * GATHER_COPIES,)), pltpu.SemaphoreType.DMA((n * GATHER_COPIES,)),
            pltpu.SemaphoreType.DMA((n,))]


def _gather_plan(ins, outs, send_sems, recv_sems, local_sems):
    n = len(ins)
    x, y, c = lax.axis_index("x"), lax.axis_index("y"), lax.axis_index("c")
    me, sibling = (x, y, c), (x, y, 1 - c)
    x_nbr, y_nbr, diag = (1 - x, y, c), (x, 1 - y, c), (1 - x, 1 - y, c)
    other = lambda pos: (pos[0], pos[1], 1 - c)

    def halves(a):
        rows = ins[a].shape[0]
        if rows % 32:
            return (0, rows), None
        return (0, rows // 2), (rows // 2, rows // 2)

    def copy(a, k, block, to, src=None, rows=None):
        dst = outs[a].at[_linear(block)]
        if rows is not None:
            dst = dst.at[pl.ds(rows[0], rows[1])]
        return pltpu.make_async_remote_copy(
            src_ref=dst if src is None else src, dst_ref=dst, send_sem=send_sems.at[a * GATHER_COPIES + k],
            recv_sem=recv_sems.at[a * GATHER_COPIES + k], device_id=to, device_id_type=_MESH_ID)

    def own_copies():
        mine = [pltpu.make_async_copy(ins[a], outs[a].at[_linear(me)], local_sems.at[a]) for a in range(n)]
        sent = []
        for a in range(n):
            sent += [copy(a, 0, me, sibling, src=ins[a]), copy(a, 1, me, x_nbr, src=ins[a]), copy(a, 2, me, y_nbr, src=ins[a])]
        return mine, sent

    def relayed_copies():
        sent = []
        for a in range(n):
            first, second = halves(a)
            sent += [copy(a, 3, x_nbr, y_nbr, rows=first), copy(a, 5, x_nbr, sibling), copy(a, 6, y_nbr, sibling)]
            if second is not None:
                sent.append(copy(a, 4, y_nbr, x_nbr, rows=second))
        return sent

    def start():
        mine, sent = own_copies()
        for cp in mine + sent:
            cp.start()

    def relay():
        for a in range(n):
            first, second = halves(a)
            copy(a, 1, x_nbr, me).wait_recv()
            copy(a, 3, x_nbr, y_nbr, rows=first).start()
            copy(a, 5, x_nbr, sibling).start()
        for a in range(n):
            first, second = halves(a)
            copy(a, 2, y_nbr, me).wait_recv()
            if second is not None:
                copy(a, 4, y_nbr, x_nbr, rows=second).start()
            copy(a, 6, y_nbr, sibling).start()

    def finish():
        mine, sent = own_copies()
        sent += relayed_copies()
        for a in range(n):
            first, second = halves(a)
            copy(a, 3, diag, me, rows=first).wait_recv()
            if second is not None:
                copy(a, 4, diag, me, rows=second).wait_recv()
            last = copy(a, 7, diag, sibling)
            last.start()
            sent.append(last)
        for a in range(n):
            copy(a, 0, other(me), me).wait_recv()
            copy(a, 5, other(x_nbr), me).wait_recv()
            copy(a, 6, other(y_nbr), me).wait_recv()
            copy(a, 7, other(diag), me).wait_recv()
        for cp in sent:
            cp.wait_send()
        for cp in mine:
            cp.wait()

    return start, relay, finish


N_CHIPS = 4


def _exchange_with_sibling(parts, name):
    n = len(parts)

    def body(*refs):
        start, finish = _sibling_exchange_plan(refs[:n], refs[n:2 * n], *refs[2 * n:])
        start()
        finish()

    return pl.pallas_call(
        body, out_shape=_sibling_exchange_shapes(parts), in_specs=[_ANY] * n, out_specs=[_ANY] * n,
        scratch_shapes=_sibling_exchange_semaphores(n), name=name,
    )(*parts)


def _sibling_exchange_shapes(parts):
    return [_sds((N_CHIPS,) + p.shape[1:], p.dtype) for p in parts]


def _sibling_exchange_semaphores(n):
    return [pltpu.SemaphoreType.DMA((n * N_CHIPS,)), pltpu.SemaphoreType.DMA((n * N_CHIPS,))]


def _sibling_exchange_plan(ins, outs, send_sems, recv_sems):
    n = len(ins)
    x, y, c = lax.axis_index("x"), lax.axis_index("y"), lax.axis_index("c")

    def all_copies():
        return [pltpu.make_async_remote_copy(
            src_ref=ins[a].at[2 * q + (1 - c)], dst_ref=outs[a].at[q], send_sem=send_sems.at[a * N_CHIPS + q],
            recv_sem=recv_sems.at[a * N_CHIPS + q], device_id=(x, y, 1 - c), device_id_type=_MESH_ID)
            for a in range(n) for q in range(N_CHIPS)]

    def start():
        for cp in all_copies():
            cp.start()

    def finish():
        for cp in all_copies():
            cp.wait()

    return start, finish


def _pair_sum(part, recv, core, out_dtype, name):
    _, r, cdim = recv.shape
    tr = max(8, min(r, (1 << 18) // cdim))
    assert r % tr == 0, (name, r, tr)

    def body(core_ref, p_ref, r_ref, o_ref):
        o_ref[...] = (p_ref[...] + r_ref[...]).astype(o_ref.dtype)

    grid_spec = pltpu.PrefetchScalarGridSpec(
        num_scalar_prefetch=1, grid=(N_CHIPS, r // tr),
        in_specs=[pl.BlockSpec((None, None, tr, cdim), lambda q, i, core_ref: (q, core_ref[0], i, 0)),
                  pl.BlockSpec((None, tr, cdim), lambda q, i, core_ref: (q, i, 0))],
        out_specs=pl.BlockSpec((None, tr, cdim), lambda q, i, core_ref: (q, i, 0)))
    return pl.pallas_call(
        body, grid_spec=grid_spec, out_shape=_sds((N_CHIPS, r, cdim), out_dtype),
        compiler_params=_params(("parallel", "parallel")), name=name,
    )(core, part.reshape(N_CHIPS, 2, r, cdim), recv)


def _exchange_between_chips(parts, name):
    n = len(parts)

    def body(*refs):
        start, finish = _chip_exchange_plan(refs[:n], refs[n:2 * n], *refs[2 * n:])
        start()
        finish()

    return pl.pallas_call(
        body, out_shape=[_sds(p.shape, p.dtype) for p in parts], in_specs=[_ANY] * n, out_specs=[_ANY] * n,
        scratch_shapes=_chip_exchange_semaphores(n), name=name,
    )(*parts)


def _chip_exchange_semaphores(n):
    n_other = N_CHIPS - 1
    return [pltpu.SemaphoreType.DMA((n * n_other,)), pltpu.SemaphoreType.DMA((n * n_other,)), pltpu.SemaphoreType.DMA((n,))]


def _chip_exchange_plan(ins, outs, send_sems, recv_sems, local_sems):
    n = len(ins)
    n_other = N_CHIPS - 1
    x, y, c = lax.axis_index("x"), lax.axis_index("y"), lax.axis_index("c")
    my_chip = 2 * x + y

    def all_copies():
        mine = [pltpu.make_async_copy(ins[a].at[my_chip], outs[a].at[my_chip], local_sems.at[a]) for a in range(n)]
        remote = []
        for j, (fx, fy) in enumerate([(1, 0), (0, 1), (1, 1)]):
            px, py = (1 - x if fx else x), (1 - y if fy else y)
            for a in range(n):
                remote.append(pltpu.make_async_remote_copy(
                    src_ref=ins[a].at[2 * px + py], dst_ref=outs[a].at[my_chip], send_sem=send_sems.at[a * n_other + j],
                    recv_sem=recv_sems.at[a * n_other + j], device_id=(px, py, c), device_id_type=_MESH_ID))
        return mine, remote

    def start():
        mine, remote = all_copies()
        for cp in mine + remote:
            cp.start()

    def finish():
        mine, remote = all_copies()
        for cp in remote + mine:
            cp.wait()

    return start, finish


def _chip_sums(parts, names, tag):
    return _pair_sums(parts, _exchange_with_sibling(parts, name="scatter_grads_sibling_" + tag), names)


def _pair_sums(parts, from_sibling, names):
    core = lax.axis_index("c").astype(jnp.int32).reshape(1)
    return [_pair_sum(p, r, core, F32 if nm.startswith("pack") else BF16, name="pair_sum_" + nm)
            for p, r, nm in zip(parts, from_sibling, names)]


def _reduce_scatter(parts, names, tag):
    return _exchange_between_chips(_chip_sums(parts, names, tag), name="scatter_grads_chips_" + tag)


def _adamw(w, m, v, slots, name, layer=0, n_layers=1, into=None):
    r, c = w.shape
    ns = slots.shape[0]
    tr = max(8, min(r, (1 << 18) // c))
    assert r % tr == 0, (name, r, tr)

    def body(w_ref, m_ref, v_ref, g_ref, *rest):
        g_out, d_out, m_out, v_out = rest[-4:]
        g = g_ref[0].astype(F32)
        for s in range(1, ns):
            g = g + g_ref[s].astype(F32)
        m_new = ADAM_B1 * m_ref[...] + (1.0 - ADAM_B1) * g
        v_new = ADAM_B2 * v_ref[...] + (1.0 - ADAM_B2) * jnp.square(g)
        m_hat = m_new / (1.0 - ADAM_B1 ** ADAM_STEP)
        v_hat = v_new / (1.0 - ADAM_B2 ** ADAM_STEP)
        d_out[...] = -ADAM_LR * (m_hat / (jnp.sqrt(v_hat) + ADAM_EPS) + ADAM_WD * w_ref[...])
        g_out[...], m_out[...], v_out[...] = g, m_new, v_new

    spec = pl.BlockSpec((tr, c), lambda i: (i, 0))
    in_specs = [spec, spec, spec, pl.BlockSpec((ns, tr, c), lambda i: (0, i, 0))]
    if n_layers == 1:
        out_spec, out_shape, earlier, aliases = spec, _sds((r, c)), [], {}
    else:
        out_spec, out_shape = pl.BlockSpec((None, tr, c), lambda i: (layer, i, 0)), _sds((n_layers, r, c))
        earlier = list(into) if into is not None else []
        aliases = {4 + i: i for i in range(len(earlier))}
    return pl.pallas_call(
        body, grid=(r // tr,), in_specs=in_specs + [_ANY] * len(earlier), out_specs=[out_spec] * 4, out_shape=[out_shape] * 4,
        input_output_aliases=aliases, compiler_params=_params(("parallel",)), name=name,
    )(w, m, v, slots, *earlier)


_COL_VECTORS = ("a_norm", "a_mix", "a_w0", "a_a0", "a_k_k", "a_k_a", "a_ln_x_w", "a_ln_x_b")
_COL_VEC_ROWS = 16
_COL_ROWS = _COL_VEC_ROWS + 2 * LORA_PAD + 256
_ROW_COLS = 2 * LORA_PAD + 256 + 512
_REPL_ROWS = 8


def _pad_to(a, size, axis):
    widths = [(0, 0)] * a.ndim
    widths[axis] = (0, size - a.shape[axis])
    return jnp.pad(a, widths)


def _pack_cols(p):
    width = p["a_norm"].shape[-1]
    vecs = jnp.concatenate([p[n].reshape(-1, width) for n in _COL_VECTORS], axis=0)
    return jnp.concatenate([_pad_to(vecs, _COL_VEC_ROWS, 0), _pad_to(p["a_w2"].reshape(-1, width), LORA_PAD, 0),
                            _pad_to(p["a_a2"].reshape(-1, width), LORA_PAD, 0), p["a_g2"].reshape(-1, width)], axis=0)


def _unpack_cols(a, lead):
    width = a.shape[-1]
    out, row = {}, 0
    for n in _COL_VECTORS:
        k = 6 if n == "a_mix" else 1
        out[n] = a[row:row + k].reshape(lead + ((6, width) if n == "a_mix" else (width,)))
        row += k
    base = _COL_VEC_ROWS
    out["a_w2"] = a[base:base + 96].reshape(lead + (96, width))
    out["a_a2"] = a[base + LORA_PAD:base + LORA_PAD + 96].reshape(lead + (96, width))
    out["a_g2"] = a[base + 2 * LORA_PAD:].reshape(lead + (256, width))
    return out


def _pack_rows(p):
    rows = p["w_kv"].shape[0]
    return jnp.concatenate([_pad_to(p["a_w1"].reshape(rows, -1), LORA_PAD, 1), _pad_to(p["a_a1"].reshape(rows, -1), LORA_PAD, 1),
                            p["a_g1"].reshape(rows, -1), p["w_kv"]], axis=1)


def _unpack_rows(a, lead):
    rows = a.shape[0]
    return {"a_w1": a[:, :96].reshape(lead + (rows, 96)), "a_a1": a[:, LORA_PAD:LORA_PAD + 96].reshape(lead + (rows, 96)),
            "a_g1": a[:, 2 * LORA_PAD:2 * LORA_PAD + 256].reshape(lead + (rows, 256)), "w_kv": a[:, 2 * LORA_PAD + 256:]}


def _pack_repl(p):
    row = lambda a: _pad_to(a.reshape(1, -1), D_MODEL, 1)
    return jnp.concatenate([p["mlp_norm"].reshape(2, D_MODEL), row(p["kv_norm"]), row(p["b_norm"]), row(p["a_r_k"]),
                            row(p["k_norm"]), row(p["b_q_norm"]), row(p["b_sinks"])], axis=0)


def _unpack_repl(a):
    return {"mlp_norm": a[0:2], "kv_norm": a[2], "b_norm": a[3:4], "a_r_k": a[4].reshape(1, N_HEADS, HEAD_DIM),
            "k_norm": a[5, :HEAD_DIM], "b_q_norm": a[6:7, :HEAD_DIM], "b_sinks": a[7:8, :N_HEADS]}


_WEIGHTS = ("a_norm", "a_mix", "a_w_rkv", "a_w0", "a_w1", "a_w2", "a_a0", "a_a1", "a_a2", "a_g1", "a_g2", "a_k_k", "a_k_a",
            "a_r_k", "a_ln_x_w", "a_ln_x_b", "a_w_out", "mlp_norm", "mlp_w_up", "mlp_w_down", "kv_norm", "w_kv", "k_norm",
            "b_norm", "b_w_q", "b_q_norm", "b_sinks", "b_w_o")


def _big_shards(p):
    return [p["a_w_rkv"][0, 0], p["a_w_rkv"][0, 1], p["a_w_rkv"][0, 2], p["a_w_out"][0], p["mlp_w_up"][0], p["mlp_w_up"][1],
            p["mlp_w_down"][0], p["mlp_w_down"][1], p["b_w_q"][0], p["b_w_o"][0]]


_BIG_NAMES = ("a_w_r", "a_w_k", "a_w_v", "a_w_out", "mlp_w_up0", "mlp_w_up1", "mlp_w_down0", "mlp_w_down1", "b_w_q", "b_w_o")


def kernel(x, a_norm, a_mix, a_w_rkv, a_w0, a_w1, a_w2, a_a0, a_a1, a_a2, a_g1, a_g2, a_k_k, a_k_a, a_r_k, a_ln_x_w,
           a_ln_x_b, a_w_out, mlp_norm, mlp_w_up, mlp_w_down, kv_norm, w_kv, k_norm, b_norm, b_w_q, b_q_norm, b_sinks,
           b_w_o, loss_target, m_a_norm, m_a_mix, m_a_w_rkv, m_a_w0, m_a_w1, m_a_w2, m_a_a0, m_a_a1, m_a_a2, m_a_g1,
           m_a_g2, m_a_k_k, m_a_k_a, m_a_r_k, m_a_ln_x_w, m_a_ln_x_b, m_a_w_out, m_mlp_norm, m_mlp_w_up, m_mlp_w_down,
           m_kv_norm, m_w_kv, m_k_norm, m_b_norm, m_b_w_q, m_b_q_norm, m_b_sinks, m_b_w_o, v_a_norm, v_a_mix, v_a_w_rkv,
           v_a_w0, v_a_w1, v_a_w2, v_a_a0, v_a_a1, v_a_a2, v_a_g1, v_a_g2, v_a_k_k, v_a_k_a, v_a_r_k, v_a_ln_x_w,
           v_a_ln_x_b, v_a_w_out, v_mlp_norm, v_mlp_w_up, v_mlp_w_down, v_kv_norm, v_w_kv, v_k_norm, v_b_norm, v_b_w_q,
           v_b_q_norm, v_b_sinks, v_b_w_o):
    given = locals()
    wts = {n: given[n] for n in _WEIGHTS}
    mom = {n: given["m_" + n] for n in _WEIGHTS}
    var = {n: given["v_" + n] for n in _WEIGHTS}

    cols_w, rows_w, repl_w = _pack_cols(wts), _pack_rows(wts), _pack_repl(wts)
    big_w = _big_shards(wts)
    big_bf16 = dict(zip(_BIG_NAMES, [b.astype(BF16) for b in big_w]))
    first_names = [k for k in _BIG_NAMES if k not in _LATE_GATHER]
    lora_in = rows_w[:, :2 * LORA_PAD]
    gathered = _all_gather([cols_w, lora_in] + [big_bf16[k] for k in first_names], name="gather_weights")
    full_cols = gathered[0].transpose(1, 0, 2).reshape(_COL_ROWS, D_MODEL)
    full_lora_in = gathered[1].reshape(D_MODEL, 2 * LORA_PAD)
    w = {}
    w.update({k: v.reshape(v.shape[1:]) for k, v in _unpack_cols(full_cols, (1,)).items()})
    for k in ("a_norm", "a_w0", "a_a0", "a_k_k", "a_k_a", "a_ln_x_w", "a_ln_x_b"):
        w[k] = w[k].reshape(1, D_MODEL)
    for k in ("a_w2", "a_a2"):
        w[k] = _pad_to(w[k], LORA_PAD, 0)
    w["a_w1"], w["a_a1"] = full_lora_in[:, :LORA_PAD], full_lora_in[:, LORA_PAD:]
    for k, arr in zip(first_names, gathered[2:]):
        w[k] = arr.reshape(N_DEV * arr.shape[1], arr.shape[2])
    late_shards = [big_bf16[k] for k in _LATE_WEIGHTS] + [big_bf16["a_w_out"], a_g1[0], w_kv]
    w["mlp_norm0"], w["mlp_norm1"] = mlp_norm[0:1], mlp_norm[1:2]
    w["kv_norm"], w["k_norm"] = kv_norm.reshape(1, D_MODEL), k_norm.reshape(1, HEAD_DIM)
    w["b_norm"], w["b_q_norm"], w["b_sinks"], w["a_r_k"] = b_norm, b_q_norm, b_sinks, a_r_k.reshape(1, D_MODEL)

    loss_local, grad_x, g, early_reduced = _local_step(x[0], loss_target[0], w, late_shards)
    loss = lax.psum(loss_local[0, 0], MESH_AXES)

    g_lead = {k: g[k][None] for k in ("a_norm", "a_mix", "a_w0", "a_a0", "a_k_k", "a_k_a", "a_ln_x_w", "a_ln_x_b", "a_g2")}
    g_lead["a_w2"], g_lead["a_a2"] = g["a_w2"][None, :96], g["a_a2"][None, :96]
    g_cols = _pack_cols(g_lead).reshape(_COL_ROWS, N_DEV, D_MODEL // N_DEV).transpose(1, 0, 2)
    g_rows = _pack_rows({"a_w1": g["a_w1"][:, :96], "a_a1": g["a_a1"][:, :96], "a_g1": g["a_g1"], "w_kv": g["w_kv"]})
    g_rows = g_rows.reshape(N_DEV, D_MODEL // N_DEV, _ROW_COLS)
    late_names = tuple(k for k in _BIG_NAMES if k not in _EARLY_GRADS)
    late_reduced = _reduce_scatter([g_cols, g_rows] + [_device_major(k, g[k]) for k in late_names],
                                   ("pack_cols", "pack_rows") + late_names, "late")
    big_reduced = dict(zip(late_names, late_reduced[2:]))
    big_reduced.update(zip(_EARLY_GRADS, early_reduced))
    reduced = list(late_reduced[:2]) + [big_reduced[k] for k in _BIG_NAMES]
    g_repl = _pack_repl({"mlp_norm": jnp.concatenate([g["mlp_norm0"], g["mlp_norm1"]], axis=0), "kv_norm": g["kv_norm"],
                         "b_norm": g["b_norm"], "a_r_k": g["a_r_k"], "k_norm": g["k_norm"], "b_q_norm": g["b_q_norm"],
                         "b_sinks": g["b_sinks"]})
    (repl_slots,) = _all_gather([g_repl], name="gather_replicated_grads")

    res = {}
    cols4 = _adamw(cols_w, _pack_cols(mom), _pack_cols(var), reduced[0], name="adamw_cols")
    rows4 = _adamw(rows_w, _pack_rows(mom), _pack_rows(var), reduced[1], name="adamw_rows")
    repl4 = _adamw(repl_w, _pack_repl(mom), _pack_repl(var), repl_slots, name="adamw_replicated")
    for unpacked in ([_unpack_cols(a, (1,)) for a in cols4], [_unpack_rows(a, (1,)) for a in rows4], [_unpack_repl(a) for a in repl4]):
        for k in unpacked[0]:
            res[k] = tuple(u[k] for u in unpacked)
    stacked = {"a_w_r": (0, 3), "a_w_k": (1, 3), "a_w_v": (2, 3), "mlp_w_up0": (0, 2), "mlp_w_up1": (1, 2),
               "mlp_w_down0": (0, 2), "mlp_w_down1": (1, 2)}
    big4, earlier = {}, None
    for k, bw, bm, bv, slots in zip(_BIG_NAMES, big_w, _big_shards(mom), _big_shards(var), reduced[2:]):
        layer, n_layers = stacked.get(k, (0, 1))
        earlier = _adamw(bw, bm, bv, slots, name="adamw_" + k, layer=layer, n_layers=n_layers, into=earlier if layer else None)
        big4[k] = earlier
    res["a_w_rkv"] = tuple(a[None] for a in big4["a_w_v"])
    res["a_w_out"] = tuple(a[None] for a in big4["a_w_out"])
    res["mlp_w_up"], res["mlp_w_down"] = tuple(big4["mlp_w_up1"]), tuple(big4["mlp_w_down1"])
    res["b_w_q"] = tuple(a[None] for a in big4["b_w_q"])
    res["b_w_o"] = tuple(a[None] for a in big4["b_w_o"])
    res["w_kv"] = tuple(a.reshape(w_kv.shape) for a in res["w_kv"])

    outs = [loss, grad_x[None]]
    for i in range(4):
        outs += [res[n][i].reshape(given[n].shape) for n in _WEIGHTS]
    return tuple(outs)
```

```python
import functools
import math

import jax
import jax.numpy as jnp
from jax import lax
from jax.experimental import pallas as pl
from jax.experimental.pallas import tpu as pltpu

F32 = jnp.float32
BF16 = jnp.bfloat16

D_MODEL = 2048
N_HEADS = 32
HEAD_DIM = 64
N_KV_HEADS = 4
Q_PER_KV = 8
ATT_BLOCK = 128
WKV_CHUNK = 64
LORA_PAD = 128
D_FF = 8192
N_DEV = 8
RMS_EPS = 1e-6
GN_EPS = 64e-5
L2_EPS = 1e-12
ROPE_THETA = 10000.0
ADAM_LR, ADAM_B1, ADAM_B2, ADAM_EPS, ADAM_WD, ADAM_STEP = 0.001, 0.9, 0.999, 1e-08, 0.01, 10
MASK_VALUE = -1e30
VMEM_LIMIT_BYTES = 56 * 1024 * 1024
MESH_AXES = ("x", "y", "c")

_NN = (((1,), (0,)), ((), ()))
_NT = (((1,), (1,)), ((), ()))
_TN = (((0,), (0,)), ((), ()))
_BNN = (((2,), (1,)), ((0,), (0,)))
_BNT = (((2,), (2,)), ((0,), (0,)))
_BTN = (((1,), (1,)), ((0,), (0,)))


def _params(sem):
    return pltpu.CompilerParams(dimension_semantics=sem, vmem_limit_bytes=VMEM_LIMIT_BYTES)


def _split2(a):
    hi = a.astype(BF16)
    return hi, (a - hi.astype(F32)).astype(BF16)


def _dot3(a, b, dims):
    ah, al = _split2(a)
    bh, bl = _split2(b)
    d = lambda p, q: lax.dot_general(p, q, dims, preferred_element_type=F32)
    return d(ah, bh) + (d(al, bh) + d(ah, bl))


@functools.partial(jax.custom_vjp, nondiff_argnums=(2,))
def _hdot(a, b, dims=_NN):
    return _dot3(a, b, dims)


def _hdot_fwd(a, b, dims):
    return _dot3(a, b, dims), (a, b)


def _hdot_bwd(dims, res, g):
    a, b = res
    nn, nt, tn = (_NN, _NT, _TN) if dims in (_NN, _NT, _TN) else (_BNN, _BNT, _BTN)
    if dims == nn:
        return _dot3(g, b, nt), _dot3(a, g, tn)
    if dims == nt:
        return _dot3(g, b, nn), _dot3(g, a, tn)
    assert dims == tn
    return _dot3(b, g, nt), _dot3(a, g, nn)


_hdot.defvjp(_hdot_fwd, _hdot_bwd)


def _tri_parts(x):
    hi = x.astype(BF16)
    r1 = x - hi.astype(F32)
    mid = r1.astype(BF16)
    return hi, mid, (r1 - mid.astype(F32)).astype(BF16)


@jax.custom_vjp
def _mask_dot(mask, x):
    mb = mask.astype(BF16)
    p0, p1, p2 = _tri_parts(x)
    d = lambda p: lax.dot_general(mb, p, _BNN, preferred_element_type=F32)
    return d(p0) + (d(p1) + d(p2))


def _mask_dot_fwd(mask, x):
    return _mask_dot(mask, x), mask


def _mask_dot_bwd(mask, g):
    mb = mask.astype(BF16)
    p0, p1, p2 = _tri_parts(g)
    d = lambda p: lax.dot_general(mb, p, _BTN, preferred_element_type=F32)
    return jnp.zeros_like(mask), d(p0) + (d(p1) + d(p2))


_mask_dot.defvjp(_mask_dot_fwd, _mask_dot_bwd)


def _b16dot(a, b, dims):
    return lax.dot_general(a.astype(BF16), b.astype(BF16), dims, preferred_element_type=F32)


@jax.custom_vjp
def _bdot(a, b):
    return _b16dot(a, b, _NN)


def _bdot_fwd(a, b):
    return _b16dot(a, b, _NN), (a, b)


def _bdot_bwd(res, g):
    a, b = res
    return _b16dot(g, b, _NT), _b16dot(a, g, _TN)


_bdot.defvjp(_bdot_fwd, _bdot_bwd)


@jax.custom_vjp
def _bdot_nt(a, b):
    return _b16dot(a, b, _NT)


def _bdot_nt_fwd(a, b):
    return _b16dot(a, b, _NT), (a, b)


def _bdot_nt_bwd(res, g):
    a, b = res
    return _b16dot(g, b, _NN), _b16dot(g, a, _TN)


_bdot_nt.defvjp(_bdot_nt_fwd, _bdot_nt_bwd)


def _rms(x, gain):
    return x * lax.rsqrt(jnp.mean(x * x, axis=-1, keepdims=True) + RMS_EPS) * gain


def _vjp_of(f, n_in, diff):
    def g(*args):
        ins, cts = args[:n_in], args[n_in:]

        def fd(*d):
            full = list(ins)
            for pos, i in enumerate(diff):
                full[i] = d[pos]
            return f(*full)

        _, pull = jax.vjp(fd, *[ins[i] for i in diff])
        return pull(tuple(cts))
    return g


def _mm(a, b, *, name, ta=False, tb=False, a_pro=None, epi=None, epi_args=(), out_dtype=F32,
        tm=1024, tn=1024, tk=2048, dims=None, b_spec=None, o_spec=None, o_shape=None):
    if dims is None:
        m, k = (a.shape[1], a.shape[0]) if ta else a.shape
        n = b.shape[0] if tb else b.shape[1]
    else:
        m, n, k = dims
    tm, tn, tk = min(tm, m), min(tn, n), min(tk, k)
    assert m % tm == 0 and n % tn == 0 and k % tk == 0, (name, m, n, k, tm, tn, tk)
    nk = k // tk
    ne = len(epi_args)
    cdims = (((0 if ta else 1,), (1 if tb else 0,)), ((), ()))

    def body(a_ref, b_ref, *rest):
        e_refs, o_ref, acc = rest[:ne], rest[ne], rest[ne + 1]
        kk = pl.program_id(2)

        @pl.when(kk == 0)
        def _():
            acc[...] = jnp.zeros_like(acc)

        av = a_ref[...]
        if a_pro is not None:
            av = a_pro(av.astype(F32))
        acc[...] += lax.dot_general(av.astype(BF16), b_ref[...].astype(BF16), cdims, preferred_element_type=F32)

        @pl.when(kk == nk - 1)
        def _():
            r = acc[...]
            if epi is not None:
                r = epi(r, *[e[...] for e in e_refs])
            o_ref[...] = r.astype(o_ref.dtype)

    a_spec = pl.BlockSpec((tk, tm), lambda i, j, q: (q, i)) if ta else pl.BlockSpec((tm, tk), lambda i, j, q: (i, q))
    if b_spec is None:
        b_spec = pl.BlockSpec((tn, tk), lambda i, j, q: (j, q)) if tb else pl.BlockSpec((tk, tn), lambda i, j, q: (q, j))
    if o_spec is None:
        o_spec = pl.BlockSpec((tm, tn), lambda i, j, q: (i, j))
        o_shape = (m, n)
    e_specs = [pl.BlockSpec((tm, tn), lambda i, j, q: (i, j)) for _ in epi_args]
    return pl.pallas_call(
        body, grid=(m // tm, n // tn, nk), in_specs=[a_spec, b_spec] + e_specs, out_specs=o_spec,
        out_shape=jax.ShapeDtypeStruct(o_shape, out_dtype), scratch_shapes=[pltpu.VMEM((tm, tn), F32)],
        compiler_params=_params(("parallel", "parallel", "arbitrary")), name=name,
    )(a, b, *epi_args)


def _rowwise(fn, rows, params, out_rows, out_params=(), *, tm=256, name):
    t = rows[0].shape[0]
    tm = min(tm, t)
    assert t % tm == 0
    nr, npar, nor, nop = len(rows), len(params), len(out_rows), len(out_params)

    def body(*refs):
        r, p = refs[:nr], refs[nr:nr + npar]
        o, op = refs[nr + npar:nr + npar + nor], refs[nr + npar + nor:]
        outs = fn(*[x[...] for x in r], *[x[...] for x in p])
        for ref, val in zip(o, outs[:nor]):
            ref[...] = val.astype(ref.dtype)
        if nop:
            @pl.when(pl.program_id(0) == 0)
            def _():
                for ref in op:
                    ref[...] = jnp.zeros_like(ref)

            for ref, val in zip(op, outs[nor:]):
                ref[...] += val.astype(F32)

    in_specs = [pl.BlockSpec((tm, x.shape[1]), lambda i: (i, 0)) for x in rows]
    in_specs += [pl.BlockSpec(p.shape, lambda i: (0, 0)) for p in params]
    out_specs = [pl.BlockSpec((tm, s.shape[1]), lambda i: (i, 0)) for s in out_rows]
    out_specs += [pl.BlockSpec(s.shape, lambda i: (0, 0)) for s in out_params]
    return pl.pallas_call(
        body, grid=(t // tm,), in_specs=in_specs, out_specs=out_specs, out_shape=list(out_rows) + list(out_params),
        compiler_params=_params(("arbitrary",)), name=name,
    )(*rows, *params)


def _sds(shape, dtype=F32):
    return jax.ShapeDtypeStruct(tuple(shape), dtype)


def _doubling_powers(l):
    powers = [l]
    for _ in range(int(math.log2(l.shape[-1])) - 1):
        powers.append(_dot3(powers[-1], powers[-1], _BNN))
    return powers


@jax.custom_vjp
def _unit_lower_solve(l, z):
    u = z
    for p in _doubling_powers(l):
        u = u + _dot3(p, u, _BNN)
    return u


def _unit_lower_solve_fwd(l, z):
    powers = _doubling_powers(l)
    u = z
    for p in powers:
        u = u + _dot3(p, u, _BNN)
    return u, (powers, u)


def _unit_lower_solve_bwd(res, du):
    powers, u = res
    g = du
    for p in powers:
        g = g + _dot3(p, g, _BTN)
    return _dot3(g, u, _BNT), g


_unit_lower_solve.defvjp(_unit_lower_solve_fwd, _unit_lower_solve_bwd)


def _wkv_chunk(s0, r, lw, k, v, a, b):
    nb, c, _ = r.shape
    ti = lax.broadcasted_iota(jnp.int32, (nb, c, c), 1)
    si = lax.broadcasted_iota(jnp.int32, (nb, c, c), 2)
    incl, strict = si <= ti, si < ti
    cum = _mask_dot(incl.astype(F32), lw)
    tot = jnp.sum(lw, axis=1, keepdims=True)
    rcum = tot - cum
    w_inv = jnp.exp(-cum)
    at, rt, bt, kt = a * jnp.exp(cum - lw), r * jnp.exp(cum), b * w_inv, k * w_inv
    l_ab = jnp.where(strict, _hdot(at, bt, _BNT), 0.0)
    l_ak = jnp.where(strict, _hdot(at, kt, _BNT), 0.0)
    t_rb = jnp.where(incl, _hdot(rt, bt, _BNT), 0.0)
    t_rk = jnp.where(incl, _hdot(rt, kt, _BNT), 0.0)
    u = _unit_lower_solve(l_ab, _hdot(at, s0, _BNT) + _hdot(l_ak, v, _BNN))
    y = _hdot(rt, s0, _BNT) + _hdot(t_rb, u, _BNN) + _hdot(t_rk, v, _BNN)
    e = jnp.exp(rcum)
    s1 = s0 * jnp.exp(tot) + _hdot(u, b * e, _BTN) + _hdot(v, k * e, _BTN)
    return y, s1


WKV_HEADS_PER_STEP = 16


def _first_and_last_step(grid):
    i, j = pl.program_id(0), pl.program_id(1)
    return jnp.logical_and(i == 0, j == 0), jnp.logical_and(i == grid[0] - 1, j == grid[1] - 1)


N_WKV_PARAMS = 7


def _tmix_chunk(s0, r, k, v, wl, al, w0, a0, k_k, k_a, ln_w, ln_b, r_k):
    lw, kmod, a, b = _wkv_prep(k, wl, al, w0, a0, k_k, k_a)
    y, s1 = _wkv_chunk(s0, r, lw, kmod, v, a, b)
    (y2,) = _wkv_post(y, r, kmod, v, ln_w, ln_b, r_k)
    return y2, s1


def _split_heads(x, nh):
    return x.reshape(x.shape[0], nh, HEAD_DIM).transpose(1, 0, 2)


def _merge_heads(xh):
    return jnp.concatenate([xh[h] for h in range(xh.shape[0])], axis=1)


def _wkv_fwd(r, k, v, wl, al, params, shards=()):
    t = r.shape[0]
    nh, n = N_HEADS, HEAD_DIM
    nc = t // WKV_CHUNK
    hb = WKV_HEADS_PER_STEP
    grid = (nh // hb, nc)
    ns = len(shards)
    n_in = 5 + N_WKV_PARAMS

    def body(*refs):
        y_ref, s_ref = refs[n_in + ns:n_in + ns + 2]
        state = refs[n_in + 2 * ns + 2]
        if ns:
            start, relay, finish = _gather_plan(refs[n_in:n_in + ns], refs[n_in + ns + 2:n_in + 2 * ns + 2], *refs[n_in + 2 * ns + 3:])
            first, last = _first_and_last_step(grid)
            pl.when(first)(start)
            relay_step = (5 * grid[0] * nc) // 8
            pl.when(jnp.logical_and(pl.program_id(0) == relay_step // nc, pl.program_id(1) == relay_step % nc))(relay)

        @pl.when(pl.program_id(1) == 0)
        def _():
            state[...] = jnp.zeros_like(state)

        s0 = state[...]
        s_ref[:, 0] = s0
        rows = [_split_heads(ref[...], hb) for ref in refs[:5]]
        y, s1 = _tmix_chunk(s0, *rows, *[ref[...] for ref in refs[5:n_in]])
        y_ref[...] = _merge_heads(y)
        state[...] = s1
        if ns:
            pl.when(last)(finish)

    blk = pl.BlockSpec((WKV_CHUNK, hb * n), lambda h, c: (c, h))
    pblk = pl.BlockSpec((hb, 1, n), lambda h, c: (h, 0, 0))
    sblk = pl.BlockSpec((hb, 1, n, n), lambda h, c: (h, c, 0, 0))
    outs = pl.pallas_call(
        body, grid=grid, in_specs=[blk] * 5 + [pblk] * N_WKV_PARAMS + [_ANY] * ns, out_specs=[blk, sblk] + [_ANY] * ns,
        out_shape=[_sds((t, nh * n)), _sds((nh, nc, n, n))] + [_sds((N_DEV,) + s.shape, s.dtype) for s in shards],
        scratch_shapes=[pltpu.VMEM((hb, n, n), F32)] + (_gather_semaphores(ns) if ns else []),
        compiler_params=_params(("arbitrary", "arbitrary")), name="wkv_fwd",
    )(r, k, v, wl, al, *params, *shards)
    return outs[0], outs[1], list(outs[2:])


def _wkv_bwd(r, k, v, wl, al, params, states, dy, chip_sums=()):
    t = r.shape[0]
    nh, n = N_HEADS, HEAD_DIM
    nc = t // WKV_CHUNK
    hb = WKV_HEADS_PER_STEP
    grid = (nh // hb, nc)
    ns = len(chip_sums)
    n_in = 5 + N_WKV_PARAMS
    n_out = 5 + N_WKV_PARAMS

    def body(*refs):
        s_ref, dy_ref = refs[n_in:n_in + 2]
        out_refs = refs[n_in + 2 + ns:n_in + 2 + ns + n_out]
        dstate = refs[n_in + 2 + 2 * ns + n_out]
        if ns:
            start, finish = _chip_exchange_plan(refs[n_in + 2:n_in + 2 + ns], refs[n_in + 2 + ns + n_out:n_in + 2 + 2 * ns + n_out],
                                                *refs[n_in + 3 + 2 * ns + n_out:])
            first, last = _first_and_last_step(grid)
            pl.when(first)(start)

        @pl.when(pl.program_id(1) == 0)
        def _():
            dstate[...] = jnp.zeros_like(dstate)
            for ref in out_refs[5:]:
                ref[...] = jnp.zeros_like(ref)

        rows = [_split_heads(ref[...], hb) for ref in refs[:5]]
        _, pull = jax.vjp(_tmix_chunk, s_ref[:, 0], *rows, *[ref[...] for ref in refs[5:n_in]])
        grads = pull((_split_heads(dy_ref[...], hb), dstate[...]))
        dstate[...] = grads[0]
        for ref, val in zip(out_refs[:5], grads[1:6]):
            ref[...] = _merge_heads(val).astype(ref.dtype)
        for ref, val in zip(out_refs[5:], grads[6:]):
            ref[...] += val
        if ns:
            pl.when(last)(finish)

    blk = pl.BlockSpec((WKV_CHUNK, hb * n), lambda h, c: (nc - 1 - c, h))
    pblk = pl.BlockSpec((hb, 1, n), lambda h, c: (h, 0, 0))
    sblk = pl.BlockSpec((hb, 1, n, n), lambda h, c: (h, nc - 1 - c, 0, 0))
    outs = pl.pallas_call(
        body, grid=grid, in_specs=[blk] * 5 + [pblk] * N_WKV_PARAMS + [sblk, blk] + [_ANY] * ns,
        out_specs=[blk] * 5 + [pblk] * N_WKV_PARAMS + [_ANY] * ns,
        out_shape=[_sds((t, nh * n), BF16)] * 5 + [_sds((nh, 1, n))] * N_WKV_PARAMS + [_sds(p.shape, p.dtype) for p in chip_sums],
        scratch_shapes=[pltpu.VMEM((hb, n, n), F32)] + (_chip_exchange_semaphores(ns) if ns else []),
        compiler_params=_params(("arbitrary", "arbitrary")), name="wkv_bwd",
    )(r, k, v, wl, al, *params, states, dy, *chip_sums)
    return outs[:5], outs[5:n_out], list(outs[n_out:])


def _wkv_prep(k, wl, al, w0, a0, k_k, k_a):
    z = -(w0 + wl)
    softplus = jnp.maximum(z, 0.0) + jnp.log1p(jnp.exp(-jnp.abs(z)))
    lw = -jnp.exp(-softplus - 0.5)
    asig = jax.nn.sigmoid(a0 + al)
    kk = k * k_k
    kk = kk / jnp.maximum(jnp.sqrt(jnp.sum(kk * kk, axis=-1, keepdims=True)), L2_EPS)
    kmod = k * (1.0 + (asig - 1.0) * k_a)
    return lw, kmod, -kk, kk * asig


def _wkv_post(y, r, kmod, v, ln_w, ln_b, r_k):
    mu = jnp.mean(y, axis=-1, keepdims=True)
    var = jnp.mean(jnp.square(y - mu), axis=-1, keepdims=True)
    yn = (y - mu) * lax.rsqrt(var + GN_EPS)
    yn = yn * ln_w + ln_b
    return (yn + jnp.sum(r * kmod * r_k, axis=-1, keepdims=True) * v,)


def _attn_group(nonzero_block, q, kc, kp, vc, vp, cos_c, sin_c, cos_p, sin_p, q_gain, k_gain, sinks):
    ri = lax.broadcasted_iota(jnp.int32, (HEAD_DIM, HEAD_DIM), 0)
    ci = lax.broadcasted_iota(jnp.int32, (HEAD_DIM, HEAD_DIM), 1)
    half = HEAD_DIM // 2
    rot = jnp.where(ri == ci + half, -1.0, 0.0) + jnp.where(ri + half == ci, 1.0, 0.0)
    rows = Q_PER_KV * ATT_BLOCK

    def rope(x, cos, sin):
        return x * cos + _hdot(x, rot, _NN) * sin

    kcr = rope(_rms(kc, k_gain), cos_c, sin_c)
    kpr = rope(_rms(kp, k_gain), cos_p, sin_p)
    qn = _rms(q, q_gain)
    qr = qn * cos_c + _hdot(qn.reshape(rows, HEAD_DIM), rot, _NN).reshape(q.shape) * sin_c
    q2 = qr.reshape(rows, HEAD_DIM)
    qi = lax.broadcasted_iota(jnp.int32, (1, ATT_BLOCK, ATT_BLOCK), 1)
    ki = lax.broadcasted_iota(jnp.int32, (1, ATT_BLOCK, ATT_BLOCK), 2)
    mask_c = ki <= qi
    mask_p = jnp.logical_and(ki > qi, nonzero_block)
    lane0 = (lax.broadcasted_iota(jnp.int32, (1, 1, 128), 2) == 0).astype(F32)
    shape3 = (Q_PER_KV, ATT_BLOCK, ATT_BLOCK)
    sc = jnp.where(mask_c, (_bdot_nt(q2, kcr) * (HEAD_DIM ** -0.5)).reshape(shape3), MASK_VALUE)
    sp = jnp.where(mask_p, (_bdot_nt(q2, kpr) * (HEAD_DIM ** -0.5)).reshape(shape3), MASK_VALUE)
    sk = jnp.sum(sinks * lane0, axis=2, keepdims=True)
    mx = jnp.maximum(jnp.maximum(jnp.max(sc, axis=2, keepdims=True), jnp.max(sp, axis=2, keepdims=True)), sk)
    mx = lax.stop_gradient(mx)
    ec, ep = jnp.exp(sc - mx), jnp.exp(sp - mx)
    den = jnp.sum(ec, axis=2, keepdims=True) + jnp.sum(ep, axis=2, keepdims=True) + jnp.exp(sk - mx)
    out = _bdot((ec / den).reshape(rows, ATT_BLOCK), vc) + _bdot((ep / den).reshape(rows, ATT_BLOCK), vp)
    return out.reshape(q.shape)


def _attn_rows(nonzero_block, qs, kcs, kps, vcs, vps, tabs, q_gain, k_gain, sinks):
    return [_attn_group(nonzero_block, qs[g], kcs[g], kps[g], vcs[g], vps[g], *tabs, q_gain, k_gain, sinks[g])
            for g in range(N_KV_HEADS)]


def _attn_operands(q_tile, kvc_tile, kvp_tile, sink_ref):
    q3 = _split_heads(q_tile, N_HEADS)
    kvc, kvp = _split_heads(kvc_tile, 2 * N_KV_HEADS), _split_heads(kvp_tile, 2 * N_KV_HEADS)
    groups = range(N_KV_HEADS)
    qs = [q3[Q_PER_KV * g:Q_PER_KV * (g + 1)] for g in groups]
    sinks = [sink_ref[Q_PER_KV * g:Q_PER_KV * (g + 1)] for g in groups]
    return (qs, [kvc[g] for g in groups], [kvp[g] for g in groups], [kvc[N_KV_HEADS + g] for g in groups],
            [kvp[N_KV_HEADS + g] for g in groups], sinks)


def _attn_specs(t):
    nb = t // ATT_BLOCK
    prev = lambda n: jnp.maximum(n - 1, 0)
    kv_width = 2 * N_KV_HEADS * HEAD_DIM
    q_spec = pl.BlockSpec((ATT_BLOCK, D_MODEL), lambda n: (n, 0))
    kv_c = pl.BlockSpec((ATT_BLOCK, kv_width), lambda n: (n, 0))
    kv_p = pl.BlockSpec((ATT_BLOCK, kv_width), lambda n: (prev(n), 0))
    tab_c = pl.BlockSpec((ATT_BLOCK, HEAD_DIM), lambda n: (n, 0))
    tab_p = pl.BlockSpec((ATT_BLOCK, HEAD_DIM), lambda n: (prev(n), 0))
    gain = pl.BlockSpec((1, HEAD_DIM), lambda n: (0, 0))
    sink = pl.BlockSpec((N_HEADS, 1, 128), lambda n: (0, 0, 0))
    return nb, q_spec, kv_c, kv_p, tab_c, tab_p, gain, sink


def _attn_fwd(q, kv, cos, sin, q_gain, k_gain, sinks):
    t = q.shape[0]
    nb, q_spec, kv_c, kv_p, tab_c, tab_p, gain, sink = _attn_specs(t)

    def body(q_ref, kvc, kvp, cc, sc, cp, sp, qg, kg, sk, o_ref):
        qs, kcs, kps, vcs, vps, sinks_ = _attn_operands(q_ref[...], kvc[...], kvp[...], sk)
        outs = _attn_rows(pl.program_id(0) > 0, qs, kcs, kps, vcs, vps, (cc[...], sc[...], cp[...], sp[...]),
                          qg[...], kg[...], sinks_)
        o_ref[...] = _merge_heads(jnp.concatenate(outs, axis=0)).astype(o_ref.dtype)

    return pl.pallas_call(
        body, grid=(nb,), in_specs=[q_spec, kv_c, kv_p, tab_c, tab_c, tab_p, tab_p, gain, gain, sink],
        out_specs=q_spec, out_shape=_sds(q.shape, BF16), compiler_params=_params(("arbitrary",)), name="attn_fwd",
    )(q, kv, kv, cos, sin, cos, sin, q_gain, k_gain, sinks)


def _attn_bwd(q, kv, cos, sin, q_gain, k_gain, sinks, do):
    t = q.shape[0]
    nb, q_spec, kv_c, kv_p, tab_c, tab_p, gain, sink = _attn_specs(t)

    def body(q_ref, kvc, kvp, cc, sc, cp, sp, qg, kg, sk, do_ref, dq_ref, dkvc_ref, dkvp_ref, dqg_ref, dkg_ref, dsk_ref):
        nonzero = pl.program_id(0) > 0
        tabs = (cc[...], sc[...], cp[...], sp[...])
        qs, kcs, kps, vcs, vps, sinks_ = _attn_operands(q_ref[...], kvc[...], kvp[...], sk)

        def f(qs_, kcs_, kps_, vcs_, vps_, qgv, kgv, sks):
            return _attn_rows(nonzero, qs_, kcs_, kps_, vcs_, vps_, tabs, qgv, kgv, sks)

        _, pull = jax.vjp(f, qs, kcs, kps, vcs, vps, qg[...], kg[...], sinks_)
        do3 = _split_heads(do_ref[...], N_HEADS)
        dqs, dkcs, dkps, dvcs, dvps, dqg, dkg, dsks = pull([do3[Q_PER_KV * g:Q_PER_KV * (g + 1)] for g in range(N_KV_HEADS)])
        dq_ref[...] = _merge_heads(jnp.concatenate(dqs, axis=0)).astype(dq_ref.dtype)
        dkvc_ref[...] = jnp.concatenate(dkcs + dvcs, axis=1)
        dkvp_ref[...] = jnp.concatenate(dkps + dvps, axis=1)

        @pl.when(pl.program_id(0) == 0)
        def _():
            dqg_ref[...] = jnp.zeros_like(dqg_ref)
            dkg_ref[...] = jnp.zeros_like(dkg_ref)
            dsk_ref[...] = jnp.zeros_like(dsk_ref)

        dqg_ref[...] += dqg
        dkg_ref[...] += dkg
        for g in range(N_KV_HEADS):
            dsk_ref[Q_PER_KV * g:Q_PER_KV * (g + 1)] += dsks[g]

    return pl.pallas_call(
        body, grid=(nb,), in_specs=[q_spec, kv_c, kv_p, tab_c, tab_c, tab_p, tab_p, gain, gain, sink, q_spec],
        out_specs=[q_spec, kv_c, kv_c, gain, gain, sink],
        out_shape=[_sds(q.shape, BF16), _sds(kv.shape), _sds(kv.shape), _sds((1, HEAD_DIM)), _sds((1, HEAD_DIM)), _sds(sinks.shape)],
        compiler_params=_params(("arbitrary",)), name="attn_bwd",
    )(q, kv, kv, cos, sin, cos, sin, q_gain, k_gain, sinks, do)


def _time_shift_lerps(x, xs, gain, *mix):
    xn, xsn = _rms(x, gain), _rms(xs, gain)
    xx = xsn - xn
    return tuple(xn + xx * m for m in mix)


def _residual_norm(h, delta, gain):
    hn = h + delta
    return hn, _rms(hn, gain)


def _residual_norm2(h, delta, gain_a, gain_b):
    hn = h + delta
    return hn, _rms(hn, gain_a), _rms(hn, gain_b)


def _relu2(u):
    return jnp.square(jnp.maximum(u, 0.0))


def _sigmoid(z):
    return jax.nn.sigmoid(z)


def _shift_down(x):
    return jnp.pad(x[:-1], ((1, 0), (0, 0)))


def _shift_up(x):
    return jnp.pad(x[1:], ((0, 1), (0, 0)))


def _rope_tables(t):
    half = HEAD_DIM // 2
    inv_freq = jnp.power(ROPE_THETA, -jnp.arange(half, dtype=F32) / half)
    ang = jnp.arange(t, dtype=jnp.int32).astype(F32)[:, None] * inv_freq[None, :]
    cos, sin = jnp.cos(ang), jnp.sin(ang)
    return jnp.concatenate([cos, cos], axis=1), jnp.concatenate([sin, sin], axis=1)


def _mlp_fwd(hn, w_up, w_down, layer, up_dev_major):
    t = hn.shape[0]
    if up_dev_major:
        cw = w_up.shape[2]
        u = _mm(hn, w_up, name=f"mlp{layer}_up", dims=(t, D_FF, D_MODEL), tn=cw, tk=D_MODEL,
                b_spec=pl.BlockSpec((None, D_MODEL, cw), lambda i, j, q: (j, q, 0)))
    else:
        u = _mm(hn, w_up, name=f"mlp{layer}_up")
    out = _mm(u, w_down, a_pro=_relu2, name=f"mlp{layer}_down")
    return u, out


def _mm_pair_reduce(a, b, *, device_axis, tile, a_pro=None, name):
    t = a.shape[0]
    tr, tc = tile
    other = (b.shape[1] // tc) if device_axis == "m" else (a.shape[1] // tr)
    n_tiles = N_CHIPS * other
    core = lax.axis_index("c").astype(jnp.int32).reshape(1)
    shape = (N_CHIPS, tr, other * tc) if device_axis == "m" else (N_CHIPS, other * tr, tc)

    def body(core_ref, a_ref, b_ref, out_ref, recv_hbm, send_buf, recv_tile, send_sems, recv_sems, local_sem):
        phase, q, o = pl.program_id(0), pl.program_id(1), pl.program_id(2)
        sibling = (lax.axis_index("x"), lax.axis_index("y"), 1 - core_ref[0])
        tile_no = q * other + o
        slot = lax.rem(tile_no, 2)

        def send(number, buffer):
            return pltpu.make_async_remote_copy(
                src_ref=send_buf.at[buffer], dst_ref=recv_hbm.at[number // other, lax.rem(number, other)],
                send_sem=send_sems.at[number], recv_sem=recv_sems.at[number], device_id=sibling, device_id_type=_MESH_ID)

        def landed():
            return pltpu.make_async_copy(recv_hbm.at[q, o], recv_tile, local_sem)

        @pl.when(phase == 1)
        def _():
            @pl.when(tile_no == 0)
            def _():
                send(n_tiles - 2, lax.rem(n_tiles - 2, 2)).wait_send()
                send(n_tiles - 1, lax.rem(n_tiles - 1, 2)).wait_send()

            send(tile_no, slot).wait_recv()
            landed().start()

        av = a_ref[...]
        if a_pro is not None:
            av = a_pro(av.astype(F32))
        acc = lax.dot_general(av.astype(BF16), b_ref[...].astype(BF16), _TN, preferred_element_type=F32)

        @pl.when(phase == 0)
        def _():
            @pl.when(tile_no >= 2)
            def _():
                send(tile_no - 2, slot).wait_send()

            send_buf[slot] = acc
            send(tile_no, slot).start()

        @pl.when(phase == 1)
        def _():
            landed().wait()
            out_ref[...] = (acc + recv_tile[...]).astype(out_ref.dtype)

    def device(phase, q, core_ref):
        return 2 * q + jnp.where(phase == 0, 1 - core_ref[0], core_ref[0])

    parked = lambda p, i: jnp.where(p == 0, 0, i)
    if device_axis == "m":
        a_spec = pl.BlockSpec((t, tr), lambda p, q, o, c: (0, device(p, q, c)))
        b_spec = pl.BlockSpec((t, tc), lambda p, q, o, c: (0, o))
        o_spec = pl.BlockSpec((None, tr, tc), lambda p, q, o, c: (parked(p, q), 0, parked(p, o)))
    else:
        a_spec = pl.BlockSpec((t, tr), lambda p, q, o, c: (0, o))
        b_spec = pl.BlockSpec((t, tc), lambda p, q, o, c: (0, device(p, q, c)))
        o_spec = pl.BlockSpec((None, tr, tc), lambda p, q, o, c: (parked(p, q), parked(p, o), 0))
    grid_spec = pltpu.PrefetchScalarGridSpec(
        num_scalar_prefetch=1, grid=(2, N_CHIPS, other), in_specs=[a_spec, b_spec], out_specs=[o_spec, _ANY],
        scratch_shapes=[pltpu.VMEM((2, tr, tc), F32), pltpu.VMEM((tr, tc), F32),
                        pltpu.SemaphoreType.DMA((n_tiles,)), pltpu.SemaphoreType.DMA((n_tiles,)), pltpu.SemaphoreType.DMA(())])
    sums, _ = pl.pallas_call(
        body, grid_spec=grid_spec, out_shape=[_sds(shape, BF16), _sds((N_CHIPS, other, tr, tc))],
        compiler_params=_params(("arbitrary", "arbitrary", "arbitrary")), name=name,
    )(core, a, b)
    return sums


MLP_PAIR_TILE = (1024, 1024)
SQUARE_PAIR_TILE = (D_MODEL // N_DEV, D_MODEL)


def _mlp_bwd(hn, u, dh, w_up, w_down, layer, up_dev_major, pair_reduce=False):
    t = hn.shape[0]
    du = _mm(dh, w_down, tb=True, epi=lambda r, uu: r * (2.0 * jnp.maximum(uu, 0.0)), epi_args=(u,), out_dtype=BF16,
             name=f"mlp{layer}_du")
    if pair_reduce:
        cw = w_up.shape[2]
        d_down = _mm_pair_reduce(u, dh, device_axis="m", tile=MLP_PAIR_TILE, a_pro=_relu2, name=f"mlp{layer}_ddown")
        d_up = _mm_pair_reduce(hn, du, device_axis="n", tile=MLP_PAIR_TILE, name=f"mlp{layer}_dup")
        dhn = _mm(du, w_up, tb=True, name=f"mlp{layer}_dhn", dims=(t, D_MODEL, D_FF), tk=cw,
                  b_spec=pl.BlockSpec((None, 1024, cw), lambda i, j, q: (q, j, 0)))
        return dhn, d_up, d_down
    d_down = _mm(u, dh, ta=True, a_pro=_relu2, name=f"mlp{layer}_ddown")
    if up_dev_major:
        cw = w_up.shape[2]
        d_up = _mm(hn, du, ta=True, name=f"mlp{layer}_dup", dims=(D_MODEL, D_FF, t), tn=cw,
                   o_spec=pl.BlockSpec((None, 1024, cw), lambda i, j, q: (j, i, 0)), o_shape=(N_DEV, D_MODEL, cw))
        dhn = _mm(du, w_up, tb=True, name=f"mlp{layer}_dhn", dims=(t, D_MODEL, D_FF), tk=cw,
                  b_spec=pl.BlockSpec((None, 1024, cw), lambda i, j, q: (q, j, 0)))
    else:
        d_up = _mm(hn, du, ta=True, name=f"mlp{layer}_dup")
        dhn = _mm(du, w_up, tb=True, name=f"mlp{layer}_dhn")
    return dhn, d_up, d_down


_LATE_WEIGHTS = ("mlp_w_up0", "mlp_w_up1", "mlp_w_down0", "mlp_w_down1", "b_w_q", "b_w_o")
_LATE_GATHER = _LATE_WEIGHTS + ("a_w_out", "a_g1", "w_kv")
_EARLY_GRADS = _LATE_WEIGHTS + ("a_w_out",)


def _local_step(x, target, w, late_shards=None, up_dev_major=True):
    t = x.shape[0]
    g = {}
    w = dict(w)
    row = lambda: _sds((t, D_MODEL))
    rowb = lambda: _sds((t, D_MODEL), BF16)
    vec = lambda: _sds((1, D_MODEL))

    xs = _shift_down(x)
    mix = [w["a_mix"][i:i + 1] for i in range(6)]
    xr, xk, xv, xw, xa, xg = _rowwise(_time_shift_lerps, [x, xs], [w["a_norm"]] + mix, [rowb()] * 6, tm=256, name="tmix_lerp")
    r = _mm(xr, w["a_w_r"], name="tmix_r")
    k = _mm(xk, w["a_w_k"], name="tmix_k")
    v = _mm(xv, w["a_w_v"], name="tmix_v")
    lw1 = _mm(xw, w["a_w1"], name="tmix_w1")
    wl = _mm(lw1, w["a_w2"], a_pro=jnp.tanh, name="tmix_w2")
    la1 = _mm(xa, w["a_a1"], name="tmix_a1")
    al = _mm(la1, w["a_a2"], name="tmix_a2")

    hv = lambda name: w[name].reshape(N_HEADS, 1, HEAD_DIM)
    wkv_params = [hv(name) for name in ("a_w0", "a_a0", "a_k_k", "a_k_a", "a_ln_x_w", "a_ln_x_b", "a_r_k")]
    y2, states, gathered = _wkv_fwd(r, k, v, wl, al, wkv_params, late_shards or ())
    for name, arr in zip(_LATE_GATHER, gathered):
        w[name] = arr if name.startswith("mlp_w_up") else arr.reshape(N_DEV * arr.shape[1], arr.shape[2])
    lg1 = _mm(xg, w["a_g1"], name="tmix_g1")
    gate = _mm(lg1, w["a_g2"], a_pro=_sigmoid, name="tmix_g2")
    (yg,) = _rowwise(lambda a, b: (a * b,), [y2, gate], [], [rowb()], name="tmix_gate")
    att = _mm(yg, w["a_w_out"], name="tmix_out")

    h1, hn0 = _rowwise(_residual_norm, [x, att], [w["mlp_norm0"]], [row(), rowb()], name="res_norm0")
    u0, m0 = _mlp_fwd(hn0, w["mlp_w_up0"], w["mlp_w_down0"], 0, up_dev_major)

    h2, kvn, qn = _rowwise(_residual_norm2, [h1, m0], [w["kv_norm"], w["b_norm"]], [row(), rowb(), rowb()], name="res_norm_kvq")
    kv = _mm(kvn, w["w_kv"], name="kv_proj")
    q = _mm(qn, w["b_w_q"], name="q_proj")
    cos, sin = _rope_tables(t)
    sinks = jnp.broadcast_to(w["b_sinks"].reshape(N_HEADS, 1, 1), (N_HEADS, 1, 128))
    o = _attn_fwd(q, kv, cos, sin, w["b_q_norm"], w["k_norm"], sinks)
    att2 = _mm(o, w["b_w_o"], name="attn_out")

    h3, hn1 = _rowwise(_residual_norm, [h2, att2], [w["mlp_norm1"]], [row(), rowb()], name="res_norm1")
    u1, m1 = _mlp_fwd(hn1, w["mlp_w_up1"], w["mlp_w_down1"], 1, up_dev_major)

    def loss_fn(h, m, tg):
        diff = (h + m) - tg
        part = 0.5 * jnp.sum(jnp.mean(jnp.square(diff), axis=-1, keepdims=True), axis=0, keepdims=True)
        dh = diff * (1.0 / D_MODEL)
        return dh, dh, jnp.broadcast_to(part, (1, 128))

    dh4, dh4b, loss = _rowwise(loss_fn, [h3, m1, target], [], [row(), rowb()], [_sds((1, 128))], name="loss")

    def res_norm_bwd(h, dnext, dhn, gain):
        dh, dgain = _vjp_of(lambda hh, gg: (_rms(hh, gg),), 2, (0, 1))(h, gain, dhn)
        return dnext + dh, dnext + dh, dgain

    distributed = late_shards is not None

    def square_dw(a_, b_, name):
        if distributed:
            return _mm_pair_reduce(a_, b_, device_axis="m", tile=SQUARE_PAIR_TILE, name=name)
        return _mm(a_, b_, ta=True, name=name)

    dhn1, g["mlp_w_up1"], g["mlp_w_down1"] = _mlp_bwd(hn1, u1, dh4b, w["mlp_w_up1"], w["mlp_w_down1"], 1, up_dev_major, distributed)
    dh3, dh3b, g["mlp_norm1"] = _rowwise(res_norm_bwd, [h3, dh4, dhn1], [w["mlp_norm1"]], [row(), rowb()], [vec()], name="res_norm1_bwd")

    g["b_w_o"] = square_dw(o, dh3b, "attn_out_dw")
    do = _mm(dh3b, w["b_w_o"], tb=True, name="attn_out_dx")
    dq, dkv_own, dkv_prev, g["b_q_norm"], g["k_norm"], dsinks = _attn_bwd(
        q, kv, cos, sin, w["b_q_norm"], w["k_norm"], sinks, do)
    g["b_sinks"] = dsinks[:, 0, 0].reshape(1, N_HEADS)
    g["b_w_q"] = square_dw(qn, dq, "q_proj_dw")
    dqn = _mm(dq, w["b_w_q"], tb=True, name="q_proj_dx")
    dkv_prev = jnp.pad(dkv_prev[ATT_BLOCK:], ((0, ATT_BLOCK), (0, 0)))
    (dkv,) = _rowwise(lambda a, b: (a + b,), [dkv_own, dkv_prev], [], [_sds(kv.shape, BF16)], name="kv_grad_sum")
    g["w_kv"] = _mm(kvn, dkv, ta=True, name="kv_proj_dw")
    dkvn = _mm(dkv, w["w_kv"], tb=True, name="kv_proj_dx")

    def res_norm2_bwd(h, dnext, dna, dnb, gain_a, gain_b):
        dha, dga = _vjp_of(lambda hh, gg: (_rms(hh, gg),), 2, (0, 1))(h, gain_a, dna)
        dhb, dgb = _vjp_of(lambda hh, gg: (_rms(hh, gg),), 2, (0, 1))(h, gain_b, dnb)
        return dnext + dha + dhb, dnext + dha + dhb, dga, dgb

    dh2, dh2b, g["kv_norm"], g["b_norm"] = _rowwise(res_norm2_bwd, [h2, dh3, dkvn, dqn], [w["kv_norm"], w["b_norm"]],
                                                    [row(), rowb()], [vec(), vec()], name="res_norm_kvq_bwd")

    dhn0, g["mlp_w_up0"], g["mlp_w_down0"] = _mlp_bwd(hn0, u0, dh2b, w["mlp_w_up0"], w["mlp_w_down0"], 0, up_dev_major, distributed)
    dh1, dh1b, g["mlp_norm0"] = _rowwise(res_norm_bwd, [h1, dh2, dhn0], [w["mlp_norm0"]], [row(), rowb()], [vec()], name="res_norm0_bwd")

    g["a_w_out"] = square_dw(yg, dh1b, "tmix_out_dw")
    dyg = _mm(dh1b, w["a_w_out"], tb=True, name="tmix_out_dx")
    dy2, dgate = _rowwise(lambda d, a, b: (d * b, d * a), [dyg, y2, gate], [], [row(), rowb()], name="tmix_gate_bwd")
    g["a_g2"] = _mm(lg1, dgate, ta=True, a_pro=_sigmoid, name="tmix_g2_dw")

    def dsigmoid(rr, z):
        s = jax.nn.sigmoid(z)
        return rr * s * (1.0 - s)

    dlg1 = _mm(dgate, w["a_g2"], tb=True, epi=dsigmoid, epi_args=(lg1,), out_dtype=BF16, name="tmix_g2_dx")
    g["a_g1"] = _mm(xg, dlg1, ta=True, name="tmix_g1_dw")
    dxg = _mm(dlg1, w["a_g1"], tb=True, name="tmix_g1_dx")

    early_sums = [g[name] for name in _EARLY_GRADS] if distributed else ()
    (dr, dk, dv, dwl, dal), param_grads, early_reduced = _wkv_bwd(r, k, v, wl, al, wkv_params, states, dy2, early_sums)
    for name, pg in zip(("a_w0", "a_a0", "a_k_k", "a_k_a", "a_ln_x_w", "a_ln_x_b", "a_r_k"), param_grads):
        g[name] = pg.reshape(1, D_MODEL)

    g["a_w_r"] = square_dw(xr, dr, "tmix_r_dw")
    g["a_w_k"] = square_dw(xk, dk, "tmix_k_dw")
    g["a_w_v"] = square_dw(xv, dv, "tmix_v_dw")
    dxr = _mm(dr, w["a_w_r"], tb=True, name="tmix_r_dx")
    dxk = _mm(dk, w["a_w_k"], tb=True, name="tmix_k_dx")
    dxv = _mm(dv, w["a_w_v"], tb=True, name="tmix_v_dx")
    g["a_w2"] = _mm(lw1, dwl, ta=True, a_pro=jnp.tanh, name="tmix_w2_dw")

    def dtanh(rr, z):
        th = jnp.tanh(z)
        return rr * (1.0 - th * th)

    dlw1 = _mm(dwl, w["a_w2"], tb=True, epi=dtanh, epi_args=(lw1,), out_dtype=BF16, name="tmix_w2_dx")
    g["a_w1"] = _mm(xw, dlw1, ta=True, name="tmix_w1_dw")
    dxw = _mm(dlw1, w["a_w1"], tb=True, name="tmix_w1_dx")
    g["a_a2"] = _mm(la1, dal, ta=True, name="tmix_a2_dw")
    dla1 = _mm(dal, w["a_a2"], tb=True, out_dtype=BF16, name="tmix_a2_dx")
    g["a_a1"] = _mm(xa, dla1, ta=True, name="tmix_a1_dw")
    dxa = _mm(dla1, w["a_a1"], tb=True, name="tmix_a1_dx")

    lerp_bwd = _vjp_of(_time_shift_lerps, 9, tuple(range(9)))

    def lerp_bwd_rows(x_, xs_, d0, d1, d2, d3, d4, d5, gain, *mx):
        return lerp_bwd(x_, xs_, gain, *mx, d0, d1, d2, d3, d4, d5)

    outs = _rowwise(lerp_bwd_rows, [x, xs, dxr, dxk, dxv, dxw, dxa, dxg], [w["a_norm"]] + mix, [row(), row()], [vec()] * 7,
                    tm=128, name="tmix_lerp_bwd")
    dx_a, dxs, g["a_norm"] = outs[0], outs[1], outs[2]
    g["a_mix"] = jnp.concatenate(outs[3:9], axis=0)
    (grad_x,) = _rowwise(lambda a, b, c: (a + b + c,), [dh1, dx_a, _shift_up(dxs)], [], [row()], name="grad_x_sum")
    return loss, grad_x, g, (early_reduced if late_shards is not None else None)


_ANY = pl.BlockSpec(memory_space=pl.ANY)
_MESH_ID = pl.DeviceIdType.MESH


def _linear(pos):
    return 4 * pos[0] + 2 * pos[1] + pos[2]


def _all_gather(shards, name):
    n = len(shards)

    def body(*refs):
        start, relay, finish = _gather_plan(refs[:n], refs[n:2 * n], *refs[2 * n:])
        start()
        relay()
        finish()

    return pl.pallas_call(
        body, out_shape=[_sds((N_DEV,) + s.shape, s.dtype) for s in shards], in_specs=[_ANY] * n, out_specs=[_ANY] * n,
        scratch_shapes=_gather_semaphores(n), name=name,
    )(*shards)


GATHER_COPIES = 8


def _gather_semaphores(n):
    return [pltpu.SemaphoreType.DMA((n * GATHER_COPIES,)), pltpu.SemaphoreType.DMA((n * GATHER_COPIES,)),
            pltpu.SemaphoreType.DMA((n,))]


def _gather_plan(ins, outs, send_sems, recv_sems, local_sems):
    n = len(ins)
    x, y, c = lax.axis_index("x"), lax.axis_index("y"), lax.axis_index("c")
    me, sibling = (x, y, c), (x, y, 1 - c)
    x_nbr, y_nbr, diag = (1 - x, y, c), (x, 1 - y, c), (1 - x, 1 - y, c)
    other = lambda pos: (pos[0], pos[1], 1 - c)

    def halves(a):
        rows = ins[a].shape[0]
        if rows % 32:
            return (0, rows), None
        return (0, rows // 2), (rows // 2, rows // 2)

    def copy(a, k, block, to, src=None, rows=None):
        dst = outs[a].at[_linear(block)]
        if rows is not None:
            dst = dst.at[pl.ds(rows[0], rows[1])]
        return pltpu.make_async_remote_copy(
            src_ref=dst if src is None else src, dst_ref=dst, send_sem=send_sems.at[a * GATHER_COPIES + k],
            recv_sem=recv_sems.at[a * GATHER_COPIES + k], device_id=to, device_id_type=_MESH_ID)

    def own_copies():
        mine = [pltpu.make_async_copy(ins[a], outs[a].at[_linear(me)], local_sems.at[a]) for a in range(n)]
        sent = []
        for a in range(n):
            sent += [copy(a, 0, me, sibling, src=ins[a]), copy(a, 1, me, x_nbr, src=ins[a]), copy(a, 2, me, y_nbr, src=ins[a])]
        return mine, sent

    def relayed_copies():
        sent = []
        for a in range(n):
            first, second = halves(a)
            sent += [copy(a, 3, x_nbr, y_nbr, rows=first), copy(a, 5, x_nbr, sibling), copy(a, 6, y_nbr, sibling)]
            if second is not None:
                sent.append(copy(a, 4, y_nbr, x_nbr, rows=second))
        return sent

    def start():
        mine, sent = own_copies()
        for cp in mine + sent:
            cp.start()

    def relay():
        for a in range(n):
            first, second = halves(a)
            copy(a, 1, x_nbr, me).wait_recv()
            copy(a, 3, x_nbr, y_nbr, rows=first).start()
            copy(a, 5, x_nbr, sibling).start()
        for a in range(n):
            first, second = halves(a)
            copy(a, 2, y_nbr, me).wait_recv()
            if second is not None:
                copy(a, 4, y_nbr, x_nbr, rows=second).start()
            copy(a, 6, y_nbr, sibling).start()

    def finish():
        mine, sent = own_copies()
        sent += relayed_copies()
        for a in range(n):
            first, second = halves(a)
            copy(a, 3, diag, me, rows=first).wait_recv()
            if second is not None:
                copy(a, 4, diag, me, rows=second).wait_recv()
            last = copy(a, 7, diag, sibling)
            last.start()
            sent.append(last)
        for a in range(n):
            copy(a, 0, other(me), me).wait_recv()
            copy(a, 5, other(x_nbr), me).wait_recv()
            copy(a, 6, other(y_nbr), me).wait_recv()
            copy(a, 7, other(diag), me).wait_recv()
        for cp in sent:
            cp.wait_send()
        for cp in mine:
            cp.wait()

    return start, relay, finish


N_CHIPS = 4


def _exchange_with_sibling(parts, name):
    n = len(parts)

    def body(*refs):
        ins, outs = refs[:n], refs[n:2 * n]
        send_sems, recv_sems = refs[2 * n:]
        x, y, c = lax.axis_index("x"), lax.axis_index("y"), lax.axis_index("c")
        copies = [pltpu.make_async_remote_copy(
            src_ref=ins[a].at[2 * q + (1 - c)], dst_ref=outs[a].at[q], send_sem=send_sems.at[a * N_CHIPS + q],
            recv_sem=recv_sems.at[a * N_CHIPS + q], device_id=(x, y, 1 - c), device_id_type=_MESH_ID)
            for a in range(n) for q in range(N_CHIPS)]
        for cp in copies:
            cp.start()
        for cp in copies:
            cp.wait()

    return pl.pallas_call(
        body, out_shape=[_sds((N_CHIPS,) + p.shape[1:], p.dtype) for p in parts], in_specs=[_ANY] * n, out_specs=[_ANY] * n,
        scratch_shapes=[pltpu.SemaphoreType.DMA((n * N_CHIPS,)), pltpu.SemaphoreType.DMA((n * N_CHIPS,))], name=name,
    )(*parts)


def _pair_sum(part, recv, core, out_dtype, name):
    _, r, cdim = recv.shape
    tr = max(8, min(r, (1 << 18) // cdim))
    assert r % tr == 0, (name, r, tr)

    def body(core_ref, p_ref, r_ref, o_ref):
        o_ref[...] = (p_ref[...] + r_ref[...]).astype(o_ref.dtype)

    grid_spec = pltpu.PrefetchScalarGridSpec(
        num_scalar_prefetch=1, grid=(N_CHIPS, r // tr),
        in_specs=[pl.BlockSpec((None, None, tr, cdim), lambda q, i, core_ref: (q, core_ref[0], i, 0)),
                  pl.BlockSpec((None, tr, cdim), lambda q, i, core_ref: (q, i, 0))],
        out_specs=pl.BlockSpec((None, tr, cdim), lambda q, i, core_ref: (q, i, 0)))
    return pl.pallas_call(
        body, grid_spec=grid_spec, out_shape=_sds((N_CHIPS, r, cdim), out_dtype),
        compiler_params=_params(("parallel", "parallel")), name=name,
    )(core, part.reshape(N_CHIPS, 2, r, cdim), recv)


def _exchange_between_chips(parts, name):
    n = len(parts)

    def body(*refs):
        start, finish = _chip_exchange_plan(refs[:n], refs[n:2 * n], *refs[2 * n:])
        start()
        finish()

    return pl.pallas_call(
        body, out_shape=[_sds(p.shape, p.dtype) for p in parts], in_specs=[_ANY] * n, out_specs=[_ANY] * n,
        scratch_shapes=_chip_exchange_semaphores(n), name=name,
    )(*parts)


def _chip_exchange_semaphores(n):
    n_other = N_CHIPS - 1
    return [pltpu.SemaphoreType.DMA((n * n_other,)), pltpu.SemaphoreType.DMA((n * n_other,)), pltpu.SemaphoreType.DMA((n,))]


def _chip_exchange_plan(ins, outs, send_sems, recv_sems, local_sems):
    n = len(ins)
    n_other = N_CHIPS - 1
    x, y, c = lax.axis_index("x"), lax.axis_index("y"), lax.axis_index("c")
    my_chip = 2 * x + y

    def all_copies():
        mine = [pltpu.make_async_copy(ins[a].at[my_chip], outs[a].at[my_chip], local_sems.at[a]) for a in range(n)]
        remote = []
        for j, (fx, fy) in enumerate([(1, 0), (0, 1), (1, 1)]):
            px, py = (1 - x if fx else x), (1 - y if fy else y)
            for a in range(n):
                remote.append(pltpu.make_async_remote_copy(
                    src_ref=ins[a].at[2 * px + py], dst_ref=outs[a].at[my_chip], send_sem=send_sems.at[a * n_other + j],
                    recv_sem=recv_sems.at[a * n_other + j], device_id=(px, py, c), device_id_type=_MESH_ID))
        return mine, remote

    def start():
        mine, remote = all_copies()
        for cp in mine + remote:
            cp.start()

    def finish():
        mine, remote = all_copies()
        for cp in remote + mine:
            cp.wait()

    return start, finish


def _chip_sums(parts, names, tag):
    from_sibling = _exchange_with_sibling(parts, name="scatter_grads_sibling_" + tag)
    core = lax.axis_index("c").astype(jnp.int32).reshape(1)
    return [_pair_sum(p, r, core, F32 if nm.startswith("pack") else BF16, name="pair_sum_" + nm)
            for p, r, nm in zip(parts, from_sibling, names)]


def _adamw(w, m, v, slots, name, layer=0, n_layers=1, into=None):
    r, c = w.shape
    ns = slots.shape[0]
    tr = max(8, min(r, (1 << 18) // c))
    assert r % tr == 0, (name, r, tr)

    def body(w_ref, m_ref, v_ref, g_ref, *rest):
        g_out, d_out, m_out, v_out = rest[-4:]
        g = g_ref[0].astype(F32)
        for s in range(1, ns):
            g = g + g_ref[s].astype(F32)
        m_new = ADAM_B1 * m_ref[...] + (1.0 - ADAM_B1) * g
        v_new = ADAM_B2 * v_ref[...] + (1.0 - ADAM_B2) * jnp.square(g)
        m_hat = m_new / (1.0 - ADAM_B1 ** ADAM_STEP)
        v_hat = v_new / (1.0 - ADAM_B2 ** ADAM_STEP)
        d_out[...] = -ADAM_LR * (m_hat / (jnp.sqrt(v_hat) + ADAM_EPS) + ADAM_WD * w_ref[...])
        g_out[...], m_out[...], v_out[...] = g, m_new, v_new

    spec = pl.BlockSpec((tr, c), lambda i: (i, 0))
    in_specs = [spec, spec, spec, pl.BlockSpec((ns, tr, c), lambda i: (0, i, 0))]
    if n_layers == 1:
        out_spec, out_shape, earlier, aliases = spec, _sds((r, c)), [], {}
    else:
        out_spec, out_shape = pl.BlockSpec((None, tr, c), lambda i: (layer, i, 0)), _sds((n_layers, r, c))
        earlier = list(into) if into is not None else []
        aliases = {4 + i: i for i in range(len(earlier))}
    return pl.pallas_call(
        body, grid=(r // tr,), in_specs=in_specs + [_ANY] * len(earlier), out_specs=[out_spec] * 4, out_shape=[out_shape] * 4,
        input_output_aliases=aliases, compiler_params=_params(("parallel",)), name=name,
    )(w, m, v, slots, *earlier)


_COL_VECTORS = ("a_norm", "a_mix", "a_w0", "a_a0", "a_k_k", "a_k_a", "a_ln_x_w", "a_ln_x_b")
_COL_VEC_ROWS = 16
_COL_ROWS = _COL_VEC_ROWS + 2 * LORA_PAD + 256
_ROW_COLS = 2 * LORA_PAD + 256 + 512
_REPL_ROWS = 8


def _pad_to(a, size, axis):
    widths = [(0, 0)] * a.ndim
    widths[axis] = (0, size - a.shape[axis])
    return jnp.pad(a, widths)


def _pack_cols(p):
    width = p["a_norm"].shape[-1]
    vecs = jnp.concatenate([p[n].reshape(-1, width) for n in _COL_VECTORS], axis=0)
    return jnp.concatenate([_pad_to(vecs, _COL_VEC_ROWS, 0), _pad_to(p["a_w2"].reshape(-1, width), LORA_PAD, 0),
                            _pad_to(p["a_a2"].reshape(-1, width), LORA_PAD, 0), p["a_g2"].reshape(-1, width)], axis=0)


def _unpack_cols(a, lead):
    width = a.shape[-1]
    out, row = {}, 0
    for n in _COL_VECTORS:
        k = 6 if n == "a_mix" else 1
        out[n] = a[row:row + k].reshape(lead + ((6, width) if n == "a_mix" else (width,)))
        row += k
    base = _COL_VEC_ROWS
    out["a_w2"] = a[base:base + 96].reshape(lead + (96, width))
    out["a_a2"] = a[base + LORA_PAD:base + LORA_PAD + 96].reshape(lead + (96, width))
    out["a_g2"] = a[base + 2 * LORA_PAD:].reshape(lead + (256, width))
    return out


def _pack_rows(p):
    rows = p["w_kv"].shape[0]
    return jnp.concatenate([_pad_to(p["a_w1"].reshape(rows, -1), LORA_PAD, 1), _pad_to(p["a_a1"].reshape(rows, -1), LORA_PAD, 1),
                            p["a_g1"].reshape(rows, -1), p["w_kv"]], axis=1)


def _unpack_rows(a, lead):
    rows = a.shape[0]
    return {"a_w1": a[:, :96].reshape(lead + (rows, 96)), "a_a1": a[:, LORA_PAD:LORA_PAD + 96].reshape(lead + (rows, 96)),
            "a_g1": a[:, 2 * LORA_PAD:2 * LORA_PAD + 256].reshape(lead + (rows, 256)), "w_kv": a[:, 2 * LORA_PAD + 256:]}


def _pack_repl(p):
    row = lambda a: _pad_to(a.reshape(1, -1), D_MODEL, 1)
    return jnp.concatenate([p["mlp_norm"].reshape(2, D_MODEL), row(p["kv_norm"]), row(p["b_norm"]), row(p["a_r_k"]),
                            row(p["k_norm"]), row(p["b_q_norm"]), row(p["b_sinks"])], axis=0)


def _unpack_repl(a):
    return {"mlp_norm": a[0:2], "kv_norm": a[2], "b_norm": a[3:4], "a_r_k": a[4].reshape(1, N_HEADS, HEAD_DIM),
            "k_norm": a[5, :HEAD_DIM], "b_q_norm": a[6:7, :HEAD_DIM], "b_sinks": a[7:8, :N_HEADS]}


_WEIGHTS = ("a_norm", "a_mix", "a_w_rkv", "a_w0", "a_w1", "a_w2", "a_a0", "a_a1", "a_a2", "a_g1", "a_g2", "a_k_k", "a_k_a",
            "a_r_k", "a_ln_x_w", "a_ln_x_b", "a_w_out", "mlp_norm", "mlp_w_up", "mlp_w_down", "kv_norm", "w_kv", "k_norm",
            "b_norm", "b_w_q", "b_q_norm", "b_sinks", "b_w_o")


def _big_shards(p):
    return [p["a_w_rkv"][0, 0], p["a_w_rkv"][0, 1], p["a_w_rkv"][0, 2], p["a_w_out"][0], p["mlp_w_up"][0], p["mlp_w_up"][1],
            p["mlp_w_down"][0], p["mlp_w_down"][1], p["b_w_q"][0], p["b_w_o"][0]]


_BIG_NAMES = ("a_w_r", "a_w_k", "a_w_v", "a_w_out", "mlp_w_up0", "mlp_w_up1", "mlp_w_down0", "mlp_w_down1", "b_w_q", "b_w_o")


def kernel(x, a_norm, a_mix, a_w_rkv, a_w0, a_w1, a_w2, a_a0, a_a1, a_a2, a_g1, a_g2, a_k_k, a_k_a, a_r_k, a_ln_x_w,
           a_ln_x_b, a_w_out, mlp_norm, mlp_w_up, mlp_w_down, kv_norm, w_kv, k_norm, b_norm, b_w_q, b_q_norm, b_sinks,
           b_w_o, loss_target, m_a_norm, m_a_mix, m_a_w_rkv, m_a_w0, m_a_w1, m_a_w2, m_a_a0, m_a_a1, m_a_a2, m_a_g1,
           m_a_g2, m_a_k_k, m_a_k_a, m_a_r_k, m_a_ln_x_w, m_a_ln_x_b, m_a_w_out, m_mlp_norm, m_mlp_w_up, m_mlp_w_down,
           m_kv_norm, m_w_kv, m_k_norm, m_b_norm, m_b_w_q, m_b_q_norm, m_b_sinks, m_b_w_o, v_a_norm, v_a_mix, v_a_w_rkv,
           v_a_w0, v_a_w1, v_a_w2, v_a_a0, v_a_a1, v_a_a2, v_a_g1, v_a_g2, v_a_k_k, v_a_k_a, v_a_r_k, v_a_ln_x_w,
           v_a_ln_x_b, v_a_w_out, v_mlp_norm, v_mlp_w_up, v_mlp_w_down, v_kv_norm, v_w_kv, v_k_norm, v_b_norm, v_b_w_q,
           v_b_q_norm, v_b_sinks, v_b_w_o):
    given = locals()
    wts = {n: given[n] for n in _WEIGHTS}
    mom = {n: given["m_" + n] for n in _WEIGHTS}
    var = {n: given["v_" + n] for n in _WEIGHTS}

    cols_w, rows_w, repl_w = _pack_cols(wts), _pack_rows(wts), _pack_repl(wts)
    big_w = _big_shards(wts)
    big_bf16 = dict(zip(_BIG_NAMES, [b.astype(BF16) for b in big_w]))
    first_names = [k for k in _BIG_NAMES if k not in _LATE_GATHER]
    lora_in = rows_w[:, :2 * LORA_PAD]
    gathered = _all_gather([cols_w, lora_in] + [big_bf16[k] for k in first_names], name="gather_weights")
    full_cols = gathered[0].transpose(1, 0, 2).reshape(_COL_ROWS, D_MODEL)
    full_lora_in = gathered[1].reshape(D_MODEL, 2 * LORA_PAD)
    w = {}
    w.update({k: v.reshape(v.shape[1:]) for k, v in _unpack_cols(full_cols, (1,)).items()})
    for k in ("a_norm", "a_w0", "a_a0", "a_k_k", "a_k_a", "a_ln_x_w", "a_ln_x_b"):
        w[k] = w[k].reshape(1, D_MODEL)
    for k in ("a_w2", "a_a2"):
        w[k] = _pad_to(w[k], LORA_PAD, 0)
    w["a_w1"], w["a_a1"] = full_lora_in[:, :LORA_PAD], full_lora_in[:, LORA_PAD:]
    for k, arr in zip(first_names, gathered[2:]):
        w[k] = arr.reshape(N_DEV * arr.shape[1], arr.shape[2])
    late_shards = [big_bf16[k] for k in _LATE_WEIGHTS] + [big_bf16["a_w_out"], a_g1[0], w_kv]
    w["mlp_norm0"], w["mlp_norm1"] = mlp_norm[0:1], mlp_norm[1:2]
    w["kv_norm"], w["k_norm"] = kv_norm.reshape(1, D_MODEL), k_norm.reshape(1, HEAD_DIM)
    w["b_norm"], w["b_q_norm"], w["b_sinks"], w["a_r_k"] = b_norm, b_q_norm, b_sinks, a_r_k.reshape(1, D_MODEL)

    loss_local, grad_x, g, early_reduced = _local_step(x[0], loss_target[0], w, late_shards)
    loss = lax.psum(loss_local[0, 0], MESH_AXES)

    g_lead = {k: g[k][None] for k in ("a_norm", "a_mix", "a_w0", "a_a0", "a_k_k", "a_k_a", "a_ln_x_w", "a_ln_x_b", "a_g2")}
    g_lead["a_w2"], g_lead["a_a2"] = g["a_w2"][None, :96], g["a_a2"][None, :96]
    g_cols = _pack_cols(g_lead).reshape(_COL_ROWS, N_DEV, D_MODEL // N_DEV).transpose(1, 0, 2)
    g_rows = _pack_rows({"a_w1": g["a_w1"][:, :96], "a_a1": g["a_a1"][:, :96], "a_g1": g["a_g1"], "w_kv": g["w_kv"]})
    g_rows = g_rows.reshape(N_DEV, D_MODEL // N_DEV, _ROW_COLS)
    late_names = tuple(k for k in _BIG_NAMES if k not in _EARLY_GRADS)
    pack_sums = _chip_sums([g_cols, g_rows], ("pack_cols", "pack_rows"), "late")
    late_reduced = _exchange_between_chips(pack_sums + [g[k] for k in late_names], name="scatter_grads_chips_late")
    big_reduced = dict(zip(late_names, late_reduced[2:]))
    big_reduced.update(zip(_EARLY_GRADS, early_reduced))
    reduced = list(late_reduced[:2]) + [big_reduced[k] for k in _BIG_NAMES]
    g_repl = _pack_repl({"mlp_norm": jnp.concatenate([g["mlp_norm0"], g["mlp_norm1"]], axis=0), "kv_norm": g["kv_norm"],
                         "b_norm": g["b_norm"], "a_r_k": g["a_r_k"], "k_norm": g["k_norm"], "b_q_norm": g["b_q_norm"],
                         "b_sinks": g["b_sinks"]})
    (repl_slots,) = _all_gather([g_repl], name="gather_replicated_grads")

    res = {}
    cols4 = _adamw(cols_w, _pack_cols(mom), _pack_cols(var), reduced[0], name="adamw_cols")
    rows4 = _adamw(rows_w, _pack_rows(mom), _pack_rows(var), reduced[1], name="adamw_rows")
    repl4 = _adamw(repl_w, _pack_repl(mom), _pack_repl(var), repl_slots, name="adamw_replicated")
    for unpacked in ([_unpack_cols(a, (1,)) for a in cols4], [_unpack_rows(a, (1,)) for a in rows4], [_unpack_repl(a) for a in repl4]):
        for k in unpacked[0]:
            res[k] = tuple(u[k] for u in unpacked)
    stacked = {"a_w_r": (0, 3), "a_w_k": (1, 3), "a_w_v": (2, 3), "mlp_w_up0": (0, 2), "mlp_w_up1": (1, 2),
               "mlp_w_down0": (0, 2), "mlp_w_down1": (1, 2)}
    big4, earlier = {}, None
    for k, bw, bm, bv, slots in zip(_BIG_NAMES, big_w, _big_shards(mom), _big_shards(var), reduced[2:]):
        layer, n_layers = stacked.get(k, (0, 1))
        earlier = _adamw(bw, bm, bv, slots, name="adamw_" + k, layer=layer, n_layers=n_layers, into=earlier if layer else None)
        big4[k] = earlier
    res["a_w_rkv"] = tuple(a[None] for a in big4["a_w_v"])
    res["a_w_out"] = tuple(a[None] for a in big4["a_w_out"])
    res["mlp_w_up"], res["mlp_w_down"] = tuple(big4["mlp_w_up1"]), tuple(big4["mlp_w_down1"])
    res["b_w_q"] = tuple(a[None] for a in big4["b_w_q"])
    res["b_w_o"] = tuple(a[None] for a in big4["b_w_o"])
    res["w_kv"] = tuple(a.reshape(w_kv.shape) for a in res["w_kv"])

    outs = [loss, grad_x[None]]
    for i in range(4):
        outs += [res[n][i].reshape(given[n].shape) for n in _WEIGHTS]
    return tuple(outs)
```

```python
import functools
import math

import jax
import jax.numpy as jnp
from jax import lax
from jax.experimental import pallas as pl
from jax.experimental.pallas import tpu as pltpu

F32 = jnp.float32
BF16 = jnp.bfloat16

D_MODEL = 2048
N_HEADS = 32
HEAD_DIM = 64
N_KV_HEADS = 4
Q_PER_KV = 8
ATT_BLOCK = 128
WKV_CHUNK = 64
LORA_PAD = 128
D_FF = 8192
N_DEV = 8
RMS_EPS = 1e-6
GN_EPS = 64e-5
L2_EPS = 1e-12
ROPE_THETA = 10000.0
ADAM_LR, ADAM_B1, ADAM_B2, ADAM_EPS, ADAM_WD, ADAM_STEP = 0.001, 0.9, 0.999, 1e-08, 0.01, 10
MASK_VALUE = -1e30
VMEM_LIMIT_BYTES = 56 * 1024 * 1024
MESH_AXES = ("x", "y", "c")

_NN = (((1,), (0,)), ((), ()))
_NT = (((1,), (1,)), ((), ()))
_TN = (((0,), (0,)), ((), ()))
_BNN = (((2,), (1,)), ((0,), (0,)))
_BNT = (((2,), (2,)), ((0,), (0,)))
_BTN = (((1,), (1,)), ((0,), (0,)))


def _params(sem):
    return pltpu.CompilerParams(dimension_semantics=sem, vmem_limit_bytes=VMEM_LIMIT_BYTES)


def _split2(a):
    hi = a.astype(BF16)
    return hi, (a - hi.astype(F32)).astype(BF16)


def _dot3(a, b, dims):
    ah, al = _split2(a)
    bh, bl = _split2(b)
    d = lambda p, q: lax.dot_general(p, q, dims, preferred_element_type=F32)
    return d(ah, bh) + (d(al, bh) + d(ah, bl))


@functools.partial(jax.custom_vjp, nondiff_argnums=(2,))
def _hdot(a, b, dims=_NN):
    return _dot3(a, b, dims)


def _hdot_fwd(a, b, dims):
    return _dot3(a, b, dims), (a, b)


def _hdot_bwd(dims, res, g):
    a, b = res
    nn, nt, tn = (_NN, _NT, _TN) if dims in (_NN, _NT, _TN) else (_BNN, _BNT, _BTN)
    if dims == nn:
        return _dot3(g, b, nt), _dot3(a, g, tn)
    if dims == nt:
        return _dot3(g, b, nn), _dot3(g, a, tn)
    assert dims == tn
    return _dot3(b, g, nt), _dot3(a, g, nn)


_hdot.defvjp(_hdot_fwd, _hdot_bwd)


def _tri_parts(x):
    hi = x.astype(BF16)
    r1 = x - hi.astype(F32)
    mid = r1.astype(BF16)
    return hi, mid, (r1 - mid.astype(F32)).astype(BF16)


@jax.custom_vjp
def _mask_dot(mask, x):
    mb = mask.astype(BF16)
    p0, p1, p2 = _tri_parts(x)
    d = lambda p: lax.dot_general(mb, p, _BNN, preferred_element_type=F32)
    return d(p0) + (d(p1) + d(p2))


def _mask_dot_fwd(mask, x):
    return _mask_dot(mask, x), mask


def _mask_dot_bwd(mask, g):
    mb = mask.astype(BF16)
    p0, p1, p2 = _tri_parts(g)
    d = lambda p: lax.dot_general(mb, p, _BTN, preferred_element_type=F32)
    return jnp.zeros_like(mask), d(p0) + (d(p1) + d(p2))


_mask_dot.defvjp(_mask_dot_fwd, _mask_dot_bwd)


def _b16dot(a, b, dims):
    return lax.dot_general(a.astype(BF16), b.astype(BF16), dims, preferred_element_type=F32)


@jax.custom_vjp
def _bdot(a, b):
    return _b16dot(a, b, _NN)


def _bdot_fwd(a, b):
    return _b16dot(a, b, _NN), (a, b)


def _bdot_bwd(res, g):
    a, b = res
    return _b16dot(g, b, _NT), _b16dot(a, g, _TN)


_bdot.defvjp(_bdot_fwd, _bdot_bwd)


@jax.custom_vjp
def _bdot_nt(a, b):
    return _b16dot(a, b, _NT)


def _bdot_nt_fwd(a, b):
    return _b16dot(a, b, _NT), (a, b)


def _bdot_nt_bwd(res, g):
    a, b = res
    return _b16dot(g, b, _NN), _b16dot(g, a, _TN)


_bdot_nt.defvjp(_bdot_nt_fwd, _bdot_nt_bwd)


def _rms(x, gain):
    return x * lax.rsqrt(jnp.mean(x * x, axis=-1, keepdims=True) + RMS_EPS) * gain


def _vjp_of(f, n_in, diff):
    def g(*args):
        ins, cts = args[:n_in], args[n_in:]

        def fd(*d):
            full = list(ins)
            for pos, i in enumerate(diff):
                full[i] = d[pos]
            return f(*full)

        _, pull = jax.vjp(fd, *[ins[i] for i in diff])
        return pull(tuple(cts))
    return g


def _mm(a, b, *, name, ta=False, tb=False, a_pro=None, epi=None, epi_args=(), out_dtype=F32,
        tm=1024, tn=1024, tk=2048, dims=None, b_spec=None, o_spec=None, o_shape=None):
    if dims is None:
        m, k = (a.shape[1], a.shape[0]) if ta else a.shape
        n = b.shape[0] if tb else b.shape[1]
    else:
        m, n, k = dims
    tm, tn, tk = min(tm, m), min(tn, n), min(tk, k)
    assert m % tm == 0 and n % tn == 0 and k % tk == 0, (name, m, n, k, tm, tn, tk)
    nk = k // tk
    ne = len(epi_args)
    cdims = (((0 if ta else 1,), (1 if tb else 0,)), ((), ()))

    def body(a_ref, b_ref, *rest):
        e_refs, o_ref, acc = rest[:ne], rest[ne], rest[ne + 1]
        kk = pl.program_id(2)

        @pl.when(kk == 0)
        def _():
            acc[...] = jnp.zeros_like(acc)

        av = a_ref[...]
        if a_pro is not None:
            av = a_pro(av.astype(F32))
        acc[...] += lax.dot_general(av.astype(BF16), b_ref[...].astype(BF16), cdims, preferred_element_type=F32)

        @pl.when(kk == nk - 1)
        def _():
            r = acc[...]
            if epi is not None:
                r = epi(r, *[e[...] for e in e_refs])
            o_ref[...] = r.astype(o_ref.dtype)

    a_spec = pl.BlockSpec((tk, tm), lambda i, j, q: (q, i)) if ta else pl.BlockSpec((tm, tk), lambda i, j, q: (i, q))
    if b_spec is None:
        b_spec = pl.BlockSpec((tn, tk), lambda i, j, q: (j, q)) if tb else pl.BlockSpec((tk, tn), lambda i, j, q: (q, j))
    if o_spec is None:
        o_spec = pl.BlockSpec((tm, tn), lambda i, j, q: (i, j))
        o_shape = (m, n)
    e_specs = [pl.BlockSpec((tm, tn), lambda i, j, q: (i, j)) for _ in epi_args]
    return pl.pallas_call(
        body, grid=(m // tm, n // tn, nk), in_specs=[a_spec, b_spec] + e_specs, out_specs=o_spec,
        out_shape=jax.ShapeDtypeStruct(o_shape, out_dtype), scratch_shapes=[pltpu.VMEM((tm, tn), F32)],
        compiler_params=_params(("parallel", "parallel", "arbitrary")), name=name,
    )(a, b, *epi_args)


def _rowwise(fn, rows, params, out_rows, out_params=(), *, tm=256, name):
    t = rows[0].shape[0]
    tm = min(tm, t)
    assert t % tm == 0
    nr, npar, nor, nop = len(rows), len(params), len(out_rows), len(out_params)

    def body(*refs):
        r, p = refs[:nr], refs[nr:nr + npar]
        o, op = refs[nr + npar:nr + npar + nor], refs[nr + npar + nor:]
        outs = fn(*[x[...] for x in r], *[x[...] for x in p])
        for ref, val in zip(o, outs[:nor]):
            ref[...] = val.astype(ref.dtype)
        if nop:
            @pl.when(pl.program_id(0) == 0)
            def _():
                for ref in op:
                    ref[...] = jnp.zeros_like(ref)

            for ref, val in zip(op, outs[nor:]):
                ref[...] += val.astype(F32)

    in_specs = [pl.BlockSpec((tm, x.shape[1]), lambda i: (i, 0)) for x in rows]
    in_specs += [pl.BlockSpec(p.shape, lambda i: (0, 0)) for p in params]
    out_specs = [pl.BlockSpec((tm, s.shape[1]), lambda i: (i, 0)) for s in out_rows]
    out_specs += [pl.BlockSpec(s.shape, lambda i: (0, 0)) for s in out_params]
    return pl.pallas_call(
        body, grid=(t // tm,), in_specs=in_specs, out_specs=out_specs, out_shape=list(out_rows) + list(out_params),
        compiler_params=_params(("arbitrary",)), name=name,
    )(*rows, *params)


def _sds(shape, dtype=F32):
    return jax.ShapeDtypeStruct(tuple(shape), dtype)


def _doubling_powers(l):
    powers = [l]
    for _ in range(int(math.log2(l.shape[-1])) - 1):
        powers.append(_dot3(powers[-1], powers[-1], _BNN))
    return powers


@jax.custom_vjp
def _unit_lower_solve(l, z):
    u = z
    for p in _doubling_powers(l):
        u = u + _dot3(p, u, _BNN)
    return u


def _unit_lower_solve_fwd(l, z):
    powers = _doubling_powers(l)
    u = z
    for p in powers:
        u = u + _dot3(p, u, _BNN)
    return u, (powers, u)


def _unit_lower_solve_bwd(res, du):
    powers, u = res
    g = du
    for p in powers:
        g = g + _dot3(p, g, _BTN)
    return _dot3(g, u, _BNT), g


_unit_lower_solve.defvjp(_unit_lower_solve_fwd, _unit_lower_solve_bwd)


def _wkv_chunk(s0, r, lw, k, v, a, b):
    nb, c, _ = r.shape
    ti = lax.broadcasted_iota(jnp.int32, (nb, c, c), 1)
    si = lax.broadcasted_iota(jnp.int32, (nb, c, c), 2)
    incl, strict = si <= ti, si < ti
    cum = _mask_dot(incl.astype(F32), lw)
    tot = jnp.sum(lw, axis=1, keepdims=True)
    rcum = tot - cum
    w_inv = jnp.exp(-cum)
    at, rt, bt, kt = a * jnp.exp(cum - lw), r * jnp.exp(cum), b * w_inv, k * w_inv
    l_ab = jnp.where(strict, _hdot(at, bt, _BNT), 0.0)
    l_ak = jnp.where(strict, _hdot(at, kt, _BNT), 0.0)
    t_rb = jnp.where(incl, _hdot(rt, bt, _BNT), 0.0)
    t_rk = jnp.where(incl, _hdot(rt, kt, _BNT), 0.0)
    u = _unit_lower_solve(l_ab, _hdot(at, s0, _BNT) + _hdot(l_ak, v, _BNN))
    y = _hdot(rt, s0, _BNT) + _hdot(t_rb, u, _BNN) + _hdot(t_rk, v, _BNN)
    e = jnp.exp(rcum)
    s1 = s0 * jnp.exp(tot) + _hdot(u, b * e, _BTN) + _hdot(v, k * e, _BTN)
    return y, s1


WKV_HEADS_PER_STEP = 16


def _first_and_last_step(grid):
    i, j = pl.program_id(0), pl.program_id(1)
    return jnp.logical_and(i == 0, j == 0), jnp.logical_and(i == grid[0] - 1, j == grid[1] - 1)


N_WKV_PARAMS = 7


def _tmix_chunk(s0, r, k, v, wl, al, w0, a0, k_k, k_a, ln_w, ln_b, r_k):
    lw, kmod, a, b = _wkv_prep(k, wl, al, w0, a0, k_k, k_a)
    y, s1 = _wkv_chunk(s0, r, lw, kmod, v, a, b)
    (y2,) = _wkv_post(y, r, kmod, v, ln_w, ln_b, r_k)
    return y2, s1


def _split_heads(x, nh):
    return x.reshape(x.shape[0], nh, HEAD_DIM).transpose(1, 0, 2)


def _merge_heads(xh):
    return jnp.concatenate([xh[h] for h in range(xh.shape[0])], axis=1)


def _wkv_fwd(r, k, v, wl, al, params, shards=()):
    t = r.shape[0]
    nh, n = N_HEADS, HEAD_DIM
    nc = t // WKV_CHUNK
    hb = WKV_HEADS_PER_STEP
    grid = (nh // hb, nc)
    ns = len(shards)
    n_in = 5 + N_WKV_PARAMS

    def body(*refs):
        y_ref, s_ref = refs[n_in + ns:n_in + ns + 2]
        state = refs[n_in + 2 * ns + 2]
        if ns:
            start, relay, finish = _gather_plan(refs[n_in:n_in + ns], refs[n_in + ns + 2:n_in + 2 * ns + 2], *refs[n_in + 2 * ns + 3:])
            first, last = _first_and_last_step(grid)
            pl.when(first)(start)
            relay_step = (5 * grid[0] * nc) // 8
            pl.when(jnp.logical_and(pl.program_id(0) == relay_step // nc, pl.program_id(1) == relay_step % nc))(relay)

        @pl.when(pl.program_id(1) == 0)
        def _():
            state[...] = jnp.zeros_like(state)

        s0 = state[...]
        s_ref[:, 0] = s0
        rows = [_split_heads(ref[...], hb) for ref in refs[:5]]
        y, s1 = _tmix_chunk(s0, *rows, *[ref[...] for ref in refs[5:n_in]])
        y_ref[...] = _merge_heads(y)
        state[...] = s1
        if ns:
            pl.when(last)(finish)

    blk = pl.BlockSpec((WKV_CHUNK, hb * n), lambda h, c: (c, h))
    pblk = pl.BlockSpec((hb, 1, n), lambda h, c: (h, 0, 0))
    sblk = pl.BlockSpec((hb, 1, n, n), lambda h, c: (h, c, 0, 0))
    outs = pl.pallas_call(
        body, grid=grid, in_specs=[blk] * 5 + [pblk] * N_WKV_PARAMS + [_ANY] * ns, out_specs=[blk, sblk] + [_ANY] * ns,
        out_shape=[_sds((t, nh * n)), _sds((nh, nc, n, n))] + [_sds((N_DEV,) + s.shape, s.dtype) for s in shards],
        scratch_shapes=[pltpu.VMEM((hb, n, n), F32)] + (_gather_semaphores(ns) if ns else []),
        compiler_params=_params(("arbitrary", "arbitrary")), name="wkv_fwd",
    )(r, k, v, wl, al, *params, *shards)
    return outs[0], outs[1], list(outs[2:])


def _wkv_bwd(r, k, v, wl, al, params, states, dy, chip_sums=()):
    t = r.shape[0]
    nh, n = N_HEADS, HEAD_DIM
    nc = t // WKV_CHUNK
    hb = WKV_HEADS_PER_STEP
    grid = (nh // hb, nc)
    ns = len(chip_sums)
    n_in = 5 + N_WKV_PARAMS
    n_out = 5 + N_WKV_PARAMS

    def body(*refs):
        s_ref, dy_ref = refs[n_in:n_in + 2]
        out_refs = refs[n_in + 2 + ns:n_in + 2 + ns + n_out]
        dstate = refs[n_in + 2 + 2 * ns + n_out]
        if ns:
            start, finish = _chip_exchange_plan(refs[n_in + 2:n_in + 2 + ns], refs[n_in + 2 + ns + n_out:n_in + 2 + 2 * ns + n_out],
                                                *refs[n_in + 3 + 2 * ns + n_out:])
            first, last = _first_and_last_step(grid)
            pl.when(first)(start)

        @pl.when(pl.program_id(1) == 0)
        def _():
            dstate[...] = jnp.zeros_like(dstate)
            for ref in out_refs[5:]:
                ref[...] = jnp.zeros_like(ref)

        rows = [_split_heads(ref[...], hb) for ref in refs[:5]]
        _, pull = jax.vjp(_tmix_chunk, s_ref[:, 0], *rows, *[ref[...] for ref in refs[5:n_in]])
        grads = pull((_split_heads(dy_ref[...], hb), dstate[...]))
        dstate[...] = grads[0]
        for ref, val in zip(out_refs[:5], grads[1:6]):
            ref[...] = _merge_heads(val).astype(ref.dtype)
        for ref, val in zip(out_refs[5:], grads[6:]):
            ref[...] += val
        if ns:
            pl.when(last)(finish)

    blk = pl.BlockSpec((WKV_CHUNK, hb * n), lambda h, c: (nc - 1 - c, h))
    pblk = pl.BlockSpec((hb, 1, n), lambda h, c: (h, 0, 0))
    sblk = pl.BlockSpec((hb, 1, n, n), lambda h, c: (h, nc - 1 - c, 0, 0))
    outs = pl.pallas_call(
        body, grid=grid, in_specs=[blk] * 5 + [pblk] * N_WKV_PARAMS + [sblk, blk] + [_ANY] * ns,
        out_specs=[blk] * 5 + [pblk] * N_WKV_PARAMS + [_ANY] * ns,
        out_shape=[_sds((t, nh * n), BF16)] * 5 + [_sds((nh, 1, n))] * N_WKV_PARAMS + [_sds(p.shape, p.dtype) for p in chip_sums],
        scratch_shapes=[pltpu.VMEM((hb, n, n), F32)] + (_chip_exchange_semaphores(ns) if ns else []),
        compiler_params=_params(("arbitrary", "arbitrary")), name="wkv_bwd",
    )(r, k, v, wl, al, *params, states, dy, *chip_sums)
    return outs[:5], outs[5:n_out], list(outs[n_out:])


def _wkv_prep(k, wl, al, w0, a0, k_k, k_a):
    z = -(w0 + wl)
    softplus = jnp.maximum(z, 0.0) + jnp.log1p(jnp.exp(-jnp.abs(z)))
    lw = -jnp.exp(-softplus - 0.5)
    asig = jax.nn.sigmoid(a0 + al)
    kk = k * k_k
    kk = kk / jnp.maximum(jnp.sqrt(jnp.sum(kk * kk, axis=-1, keepdims=True)), L2_EPS)
    kmod = k * (1.0 + (asig - 1.0) * k_a)
    return lw, kmod, -kk, kk * asig


def _wkv_post(y, r, kmod, v, ln_w, ln_b, r_k):
    mu = jnp.mean(y, axis=-1, keepdims=True)
    var = jnp.mean(jnp.square(y - mu), axis=-1, keepdims=True)
    yn = (y - mu) * lax.rsqrt(var + GN_EPS)
    yn = yn * ln_w + ln_b
    return (yn + jnp.sum(r * kmod * r_k, axis=-1, keepdims=True) * v,)


def _attn_group(nonzero_block, q, kc, kp, vc, vp, cos_c, sin_c, cos_p, sin_p, q_gain, k_gain, sinks):
    ri = lax.broadcasted_iota(jnp.int32, (HEAD_DIM, HEAD_DIM), 0)
    ci = lax.broadcasted_iota(jnp.int32, (HEAD_DIM, HEAD_DIM), 1)
    half = HEAD_DIM // 2
    rot = jnp.where(ri == ci + half, -1.0, 0.0) + jnp.where(ri + half == ci, 1.0, 0.0)
    rows = Q_PER_KV * ATT_BLOCK

    def rope(x, cos, sin):
        return x * cos + _hdot(x, rot, _NN) * sin

    kcr = rope(_rms(kc, k_gain), cos_c, sin_c)
    kpr = rope(_rms(kp, k_gain), cos_p, sin_p)
    qn = _rms(q, q_gain)
    qr = qn * cos_c + _hdot(qn.reshape(rows, HEAD_DIM), rot, _NN).reshape(q.shape) * sin_c
    q2 = qr.reshape(rows, HEAD_DIM)
    qi = lax.broadcasted_iota(jnp.int32, (1, ATT_BLOCK, ATT_BLOCK), 1)
    ki = lax.broadcasted_iota(jnp.int32, (1, ATT_BLOCK, ATT_BLOCK), 2)
    mask_c = ki <= qi
    mask_p = jnp.logical_and(ki > qi, nonzero_block)
    lane0 = (lax.broadcasted_iota(jnp.int32, (1, 1, 128), 2) == 0).astype(F32)
    shape3 = (Q_PER_KV, ATT_BLOCK, ATT_BLOCK)
    sc = jnp.where(mask_c, (_bdot_nt(q2, kcr) * (HEAD_DIM ** -0.5)).reshape(shape3), MASK_VALUE)
    sp = jnp.where(mask_p, (_bdot_nt(q2, kpr) * (HEAD_DIM ** -0.5)).reshape(shape3), MASK_VALUE)
    sk = jnp.sum(sinks * lane0, axis=2, keepdims=True)
    mx = jnp.maximum(jnp.maximum(jnp.max(sc, axis=2, keepdims=True), jnp.max(sp, axis=2, keepdims=True)), sk)
    mx = lax.stop_gradient(mx)
    ec, ep = jnp.exp(sc - mx), jnp.exp(sp - mx)
    den = jnp.sum(ec, axis=2, keepdims=True) + jnp.sum(ep, axis=2, keepdims=True) + jnp.exp(sk - mx)
    out = _bdot((ec / den).reshape(rows, ATT_BLOCK), vc) + _bdot((ep / den).reshape(rows, ATT_BLOCK), vp)
    return out.reshape(q.shape)


def _attn_rows(nonzero_block, qs, kcs, kps, vcs, vps, tabs, q_gain, k_gain, sinks):
    return [_attn_group(nonzero_block, qs[g], kcs[g], kps[g], vcs[g], vps[g], *tabs, q_gain, k_gain, sinks[g])
            for g in range(N_KV_HEADS)]


def _attn_operands(q_tile, kvc_tile, kvp_tile, sink_ref):
    q3 = _split_heads(q_tile, N_HEADS)
    kvc, kvp = _split_heads(kvc_tile, 2 * N_KV_HEADS), _split_heads(kvp_tile, 2 * N_KV_HEADS)
    groups = range(N_KV_HEADS)
    qs = [q3[Q_PER_KV * g:Q_PER_KV * (g + 1)] for g in groups]
    sinks = [sink_ref[Q_PER_KV * g:Q_PER_KV * (g + 1)] for g in groups]
    return (qs, [kvc[g] for g in groups], [kvp[g] for g in groups], [kvc[N_KV_HEADS + g] for g in groups],
            [kvp[N_KV_HEADS + g] for g in groups], sinks)


def _attn_specs(t):
    nb = t // ATT_BLOCK
    prev = lambda n: jnp.maximum(n - 1, 0)
    kv_width = 2 * N_KV_HEADS * HEAD_DIM
    q_spec = pl.BlockSpec((ATT_BLOCK, D_MODEL), lambda n: (n, 0))
    kv_c = pl.BlockSpec((ATT_BLOCK, kv_width), lambda n: (n, 0))
    kv_p = pl.BlockSpec((ATT_BLOCK, kv_width), lambda n: (prev(n), 0))
    tab_c = pl.BlockSpec((ATT_BLOCK, HEAD_DIM), lambda n: (n, 0))
    tab_p = pl.BlockSpec((ATT_BLOCK, HEAD_DIM), lambda n: (prev(n), 0))
    gain = pl.BlockSpec((1, HEAD_DIM), lambda n: (0, 0))
    sink = pl.BlockSpec((N_HEADS, 1, 128), lambda n: (0, 0, 0))
    return nb, q_spec, kv_c, kv_p, tab_c, tab_p, gain, sink


def _attn_fwd(q, kv, cos, sin, q_gain, k_gain, sinks):
    t = q.shape[0]
    nb, q_spec, kv_c, kv_p, tab_c, tab_p, gain, sink = _attn_specs(t)

    def body(q_ref, kvc, kvp, cc, sc, cp, sp, qg, kg, sk, o_ref):
        qs, kcs, kps, vcs, vps, sinks_ = _attn_operands(q_ref[...], kvc[...], kvp[...], sk)
        outs = _attn_rows(pl.program_id(0) > 0, qs, kcs, kps, vcs, vps, (cc[...], sc[...], cp[...], sp[...]),
                          qg[...], kg[...], sinks_)
        o_ref[...] = _merge_heads(jnp.concatenate(outs, axis=0)).astype(o_ref.dtype)

    return pl.pallas_call(
        body, grid=(nb,), in_specs=[q_spec, kv_c, kv_p, tab_c, tab_c, tab_p, tab_p, gain, gain, sink],
        out_specs=q_spec, out_shape=_sds(q.shape, BF16), compiler_params=_params(("arbitrary",)), name="attn_fwd",
    )(q, kv, kv, cos, sin, cos, sin, q_gain, k_gain, sinks)


def _attn_bwd(q, kv, cos, sin, q_gain, k_gain, sinks, do):
    t = q.shape[0]
    nb, q_spec, kv_c, kv_p, tab_c, tab_p, gain, sink = _attn_specs(t)

    def body(q_ref, kvc, kvp, cc, sc, cp, sp, qg, kg, sk, do_ref, dq_ref, dkvc_ref, dkvp_ref, dqg_ref, dkg_ref, dsk_ref):
        nonzero = pl.program_id(0) > 0
        tabs = (cc[...], sc[...], cp[...], sp[...])
        qs, kcs, kps, vcs, vps, sinks_ = _attn_operands(q_ref[...], kvc[...], kvp[...], sk)

        def f(qs_, kcs_, kps_, vcs_, vps_, qgv, kgv, sks):
            return _attn_rows(nonzero, qs_, kcs_, kps_, vcs_, vps_, tabs, qgv, kgv, sks)

        _, pull = jax.vjp(f, qs, kcs, kps, vcs, vps, qg[...], kg[...], sinks_)
        do3 = _split_heads(do_ref[...], N_HEADS)
        dqs, dkcs, dkps, dvcs, dvps, dqg, dkg, dsks = pull([do3[Q_PER_KV * g:Q_PER_KV * (g + 1)] for g in range(N_KV_HEADS)])
        dq_ref[...] = _merge_heads(jnp.concatenate(dqs, axis=0)).astype(dq_ref.dtype)
        dkvc_ref[...] = jnp.concatenate(dkcs + dvcs, axis=1)
        dkvp_ref[...] = jnp.concatenate(dkps + dvps, axis=1)

        @pl.when(pl.program_id(0) == 0)
        def _():
            dqg_ref[...] = jnp.zeros_like(dqg_ref)
            dkg_ref[...] = jnp.zeros_like(dkg_ref)
            dsk_ref[...] = jnp.zeros_like(dsk_ref)

        dqg_ref[...] += dqg
        dkg_ref[...] += dkg
        for g in range(N_KV_HEADS):
            dsk_ref[Q_PER_KV * g:Q_PER_KV * (g + 1)] += dsks[g]

    return pl.pallas_call(
        body, grid=(nb,), in_specs=[q_spec, kv_c, kv_p, tab_c, tab_c, tab_p, tab_p, gain, gain, sink, q_spec],
        out_specs=[q_spec, kv_c, kv_c, gain, gain, sink],
        out_shape=[_sds(q.shape, BF16), _sds(kv.shape), _sds(kv.shape), _sds((1, HEAD_DIM)), _sds((1, HEAD_DIM)), _sds(sinks.shape)],
        compiler_params=_params(("arbitrary",)), name="attn_bwd",
    )(q, kv, kv, cos, sin, cos, sin, q_gain, k_gain, sinks, do)


def _time_shift_lerps(x, xs, gain, *mix):
    xn, xsn = _rms(x, gain), _rms(xs, gain)
    xx = xsn - xn
    return tuple(xn + xx * m for m in mix)


def _residual_norm(h, delta, gain):
    hn = h + delta
    return hn, _rms(hn, gain)


def _residual_norm2(h, delta, gain_a, gain_b):
    hn = h + delta
    return hn, _rms(hn, gain_a), _rms(hn, gain_b)


def _relu2(u):
    return jnp.square(jnp.maximum(u, 0.0))


def _sigmoid(z):
    return jax.nn.sigmoid(z)


def _shift_down(x):
    return jnp.pad(x[:-1], ((1, 0), (0, 0)))


def _shift_up(x):
    return jnp.pad(x[1:], ((0, 1), (0, 0)))


def _rope_tables(t):
    half = HEAD_DIM // 2
    inv_freq = jnp.power(ROPE_THETA, -jnp.arange(half, dtype=F32) / half)
    ang = jnp.arange(t, dtype=jnp.int32).astype(F32)[:, None] * inv_freq[None, :]
    cos, sin = jnp.cos(ang), jnp.sin(ang)
    return jnp.concatenate([cos, cos], axis=1), jnp.concatenate([sin, sin], axis=1)


def _mlp_fwd(hn, w_up, w_down, layer, up_dev_major):
    t = hn.shape[0]
    if up_dev_major:
        cw = w_up.shape[2]
        u = _mm(hn, w_up, name=f"mlp{layer}_up", dims=(t, D_FF, D_MODEL), tn=cw, tk=D_MODEL,
                b_spec=pl.BlockSpec((None, D_MODEL, cw), lambda i, j, q: (j, q, 0)))
    else:
        u = _mm(hn, w_up, name=f"mlp{layer}_up")
    out = _mm(u, w_down, a_pro=_relu2, name=f"mlp{layer}_down")
    return u, out


def _mm_pair_reduce(a, b, *, device_axis, tile, a_pro=None, name):
    t = a.shape[0]
    tr, tc = tile
    other = (b.shape[1] // tc) if device_axis == "m" else (a.shape[1] // tr)
    n_tiles = N_CHIPS * other
    core = lax.axis_index("c").astype(jnp.int32).reshape(1)
    shape = (N_CHIPS, tr, other * tc) if device_axis == "m" else (N_CHIPS, other * tr, tc)

    def body(core_ref, a_ref, b_ref, out_ref, recv_hbm, send_buf, recv_tile, send_sems, recv_sems, local_sem):
        phase, q, o = pl.program_id(0), pl.program_id(1), pl.program_id(2)
        sibling = (lax.axis_index("x"), lax.axis_index("y"), 1 - core_ref[0])
        tile_no = q * other + o
        slot = lax.rem(tile_no, 2)

        def send(number, buffer):
            return pltpu.make_async_remote_copy(
                src_ref=send_buf.at[buffer], dst_ref=recv_hbm.at[number // other, lax.rem(number, other)],
                send_sem=send_sems.at[number], recv_sem=recv_sems.at[number], device_id=sibling, device_id_type=_MESH_ID)

        def landed():
            return pltpu.make_async_copy(recv_hbm.at[q, o], recv_tile, local_sem)

        @pl.when(phase == 1)
        def _():
            @pl.when(tile_no == 0)
            def _():
                send(n_tiles - 2, lax.rem(n_tiles - 2, 2)).wait_send()
                send(n_tiles - 1, lax.rem(n_tiles - 1, 2)).wait_send()

            send(tile_no, slot).wait_recv()
            landed().start()

        av = a_ref[...]
        if a_pro is not None:
            av = a_pro(av.astype(F32))
        acc = lax.dot_general(av.astype(BF16), b_ref[...].astype(BF16), _TN, preferred_element_type=F32)

        @pl.when(phase == 0)
        def _():
            @pl.when(tile_no >= 2)
            def _():
                send(tile_no - 2, slot).wait_send()

            send_buf[slot] = acc.astype(send_buf.dtype)
            send(tile_no, slot).start()

        @pl.when(phase == 1)
        def _():
            landed().wait()
            out_ref[...] = (acc + recv_tile[...].astype(F32)).astype(out_ref.dtype)

    def device(phase, q, core_ref):
        return 2 * q + jnp.where(phase == 0, 1 - core_ref[0], core_ref[0])

    parked = lambda p, i: jnp.where(p == 0, 0, i)
    if device_axis == "m":
        a_spec = pl.BlockSpec((t, tr), lambda p, q, o, c: (0, device(p, q, c)))
        b_spec = pl.BlockSpec((t, tc), lambda p, q, o, c: (0, o))
        o_spec = pl.BlockSpec((None, tr, tc), lambda p, q, o, c: (parked(p, q), 0, parked(p, o)))
    else:
        a_spec = pl.BlockSpec((t, tr), lambda p, q, o, c: (0, o))
        b_spec = pl.BlockSpec((t, tc), lambda p, q, o, c: (0, device(p, q, c)))
        o_spec = pl.BlockSpec((None, tr, tc), lambda p, q, o, c: (parked(p, q), parked(p, o), 0))
    grid_spec = pltpu.PrefetchScalarGridSpec(
        num_scalar_prefetch=1, grid=(2, N_CHIPS, other), in_specs=[a_spec, b_spec], out_specs=[o_spec, _ANY],
        scratch_shapes=[pltpu.VMEM((2, tr, tc), BF16), pltpu.VMEM((tr, tc), BF16),
                        pltpu.SemaphoreType.DMA((n_tiles,)), pltpu.SemaphoreType.DMA((n_tiles,)), pltpu.SemaphoreType.DMA(())])
    sums, _ = pl.pallas_call(
        body, grid_spec=grid_spec, out_shape=[_sds(shape, BF16), _sds((N_CHIPS, other, tr, tc), BF16)],
        compiler_params=_params(("arbitrary", "arbitrary", "arbitrary")), name=name,
    )(core, a, b)
    return sums


MLP_PAIR_TILE = (1024, 1024)
SQUARE_PAIR_TILE = (D_MODEL // N_DEV, D_MODEL)


def _mlp_bwd(hn, u, dh, w_up, w_down, layer, up_dev_major, pair_reduce=False):
    t = hn.shape[0]
    du = _mm(dh, w_down, tb=True, epi=lambda r, uu: r * (2.0 * jnp.maximum(uu, 0.0)), epi_args=(u,), out_dtype=BF16,
             name=f"mlp{layer}_du")
    if pair_reduce:
        cw = w_up.shape[2]
        d_down = _mm_pair_reduce(u, dh, device_axis="m", tile=MLP_PAIR_TILE, a_pro=_relu2, name=f"mlp{layer}_ddown")
        d_up = _mm_pair_reduce(hn, du, device_axis="n", tile=MLP_PAIR_TILE, name=f"mlp{layer}_dup")
        dhn = _mm(du, w_up, tb=True, name=f"mlp{layer}_dhn", dims=(t, D_MODEL, D_FF), tk=cw,
                  b_spec=pl.BlockSpec((None, 1024, cw), lambda i, j, q: (q, j, 0)))
        return dhn, d_up, d_down
    d_down = _mm(u, dh, ta=True, a_pro=_relu2, name=f"mlp{layer}_ddown")
    if up_dev_major:
        cw = w_up.shape[2]
        d_up = _mm(hn, du, ta=True, name=f"mlp{layer}_dup", dims=(D_MODEL, D_FF, t), tn=cw,
                   o_spec=pl.BlockSpec((None, 1024, cw), lambda i, j, q: (j, i, 0)), o_shape=(N_DEV, D_MODEL, cw))
        dhn = _mm(du, w_up, tb=True, name=f"mlp{layer}_dhn", dims=(t, D_MODEL, D_FF), tk=cw,
                  b_spec=pl.BlockSpec((None, 1024, cw), lambda i, j, q: (q, j, 0)))
    else:
        d_up = _mm(hn, du, ta=True, name=f"mlp{layer}_dup")
        dhn = _mm(du, w_up, tb=True, name=f"mlp{layer}_dhn")
    return dhn, d_up, d_down


_LATE_WEIGHTS = ("mlp_w_up0", "mlp_w_up1", "mlp_w_down0", "mlp_w_down1", "b_w_q", "b_w_o")
_LATE_GATHER = _LATE_WEIGHTS + ("a_w_out", "a_g1", "w_kv")
_EARLY_GRADS = _LATE_WEIGHTS + ("a_w_out",)


def _local_step(x, target, w, late_shards=None, up_dev_major=True):
    t = x.shape[0]
    g = {}
    w = dict(w)
    row = lambda: _sds((t, D_MODEL))
    rowb = lambda: _sds((t, D_MODEL), BF16)
    vec = lambda: _sds((1, D_MODEL))

    xs = _shift_down(x)
    mix = [w["a_mix"][i:i + 1] for i in range(6)]
    xr, xk, xv, xw, xa, xg = _rowwise(_time_shift_lerps, [x, xs], [w["a_norm"]] + mix, [rowb()] * 6, tm=256, name="tmix_lerp")
    r = _mm(xr, w["a_w_r"], name="tmix_r")
    k = _mm(xk, w["a_w_k"], name="tmix_k")
    v = _mm(xv, w["a_w_v"], name="tmix_v")
    lw1 = _mm(xw, w["a_w1"], name="tmix_w1")
    wl = _mm(lw1, w["a_w2"], a_pro=jnp.tanh, name="tmix_w2")
    la1 = _mm(xa, w["a_a1"], name="tmix_a1")
    al = _mm(la1, w["a_a2"], name="tmix_a2")

    hv = lambda name: w[name].reshape(N_HEADS, 1, HEAD_DIM)
    wkv_params = [hv(name) for name in ("a_w0", "a_a0", "a_k_k", "a_k_a", "a_ln_x_w", "a_ln_x_b", "a_r_k")]
    y2, states, gathered = _wkv_fwd(r, k, v, wl, al, wkv_params, late_shards or ())
    for name, arr in zip(_LATE_GATHER, gathered):
        w[name] = arr if name.startswith("mlp_w_up") else arr.reshape(N_DEV * arr.shape[1], arr.shape[2])
    lg1 = _mm(xg, w["a_g1"], name="tmix_g1")
    gate = _mm(lg1, w["a_g2"], a_pro=_sigmoid, name="tmix_g2")
    (yg,) = _rowwise(lambda a, b: (a * b,), [y2, gate], [], [rowb()], name="tmix_gate")
    att = _mm(yg, w["a_w_out"], name="tmix_out")

    h1, hn0 = _rowwise(_residual_norm, [x, att], [w["mlp_norm0"]], [row(), rowb()], name="res_norm0")
    u0, m0 = _mlp_fwd(hn0, w["mlp_w_up0"], w["mlp_w_down0"], 0, up_dev_major)

    h2, kvn, qn = _rowwise(_residual_norm2, [h1, m0], [w["kv_norm"], w["b_norm"]], [row(), rowb(), rowb()], name="res_norm_kvq")
    kv = _mm(kvn, w["w_kv"], name="kv_proj")
    q = _mm(qn, w["b_w_q"], name="q_proj")
    cos, sin = _rope_tables(t)
    sinks = jnp.broadcast_to(w["b_sinks"].reshape(N_HEADS, 1, 1), (N_HEADS, 1, 128))
    o = _attn_fwd(q, kv, cos, sin, w["b_q_norm"], w["k_norm"], sinks)
    att2 = _mm(o, w["b_w_o"], name="attn_out")

    h3, hn1 = _rowwise(_residual_norm, [h2, att2], [w["mlp_norm1"]], [row(), rowb()], name="res_norm1")
    u1, m1 = _mlp_fwd(hn1, w["mlp_w_up1"], w["mlp_w_down1"], 1, up_dev_major)

    def loss_fn(h, m, tg):
        diff = (h + m) - tg
        part = 0.5 * jnp.sum(jnp.mean(jnp.square(diff), axis=-1, keepdims=True), axis=0, keepdims=True)
        dh = diff * (1.0 / D_MODEL)
        return dh, dh, jnp.broadcast_to(part, (1, 128))

    dh4, dh4b, loss = _rowwise(loss_fn, [h3, m1, target], [], [row(), rowb()], [_sds((1, 128))], name="loss")

    def res_norm_bwd(h, dnext, dhn, gain):
        dh, dgain = _vjp_of(lambda hh, gg: (_rms(hh, gg),), 2, (0, 1))(h, gain, dhn)
        return dnext + dh, dnext + dh, dgain

    distributed = late_shards is not None

    def square_dw(a_, b_, name):
        if distributed:
            return _mm_pair_reduce(a_, b_, device_axis="m", tile=SQUARE_PAIR_TILE, name=name)
        return _mm(a_, b_, ta=True, name=name)

    dhn1, g["mlp_w_up1"], g["mlp_w_down1"] = _mlp_bwd(hn1, u1, dh4b, w["mlp_w_up1"], w["mlp_w_down1"], 1, up_dev_major, distributed)
    dh3, dh3b, g["mlp_norm1"] = _rowwise(res_norm_bwd, [h3, dh4, dhn1], [w["mlp_norm1"]], [row(), rowb()], [vec()], name="res_norm1_bwd")

    g["b_w_o"] = square_dw(o, dh3b, "attn_out_dw")
    do = _mm(dh3b, w["b_w_o"], tb=True, name="attn_out_dx")
    dq, dkv_own, dkv_prev, g["b_q_norm"], g["k_norm"], dsinks = _attn_bwd(
        q, kv, cos, sin, w["b_q_norm"], w["k_norm"], sinks, do)
    g["b_sinks"] = dsinks[:, 0, 0].reshape(1, N_HEADS)
    g["b_w_q"] = square_dw(qn, dq, "q_proj_dw")
    dqn = _mm(dq, w["b_w_q"], tb=True, name="q_proj_dx")
    dkv_prev = jnp.pad(dkv_prev[ATT_BLOCK:], ((0, ATT_BLOCK), (0, 0)))
    (dkv,) = _rowwise(lambda a, b: (a + b,), [dkv_own, dkv_prev], [], [_sds(kv.shape, BF16)], name="kv_grad_sum")
    g["w_kv"] = _mm(kvn, dkv, ta=True, name="kv_proj_dw")
    dkvn = _mm(dkv, w["w_kv"], tb=True, name="kv_proj_dx")

    def res_norm2_bwd(h, dnext, dna, dnb, gain_a, gain_b):
        dha, dga = _vjp_of(lambda hh, gg: (_rms(hh, gg),), 2, (0, 1))(h, gain_a, dna)
        dhb, dgb = _vjp_of(lambda hh, gg: (_rms(hh, gg),), 2, (0, 1))(h, gain_b, dnb)
        return dnext + dha + dhb, dnext + dha + dhb, dga, dgb

    dh2, dh2b, g["kv_norm"], g["b_norm"] = _rowwise(res_norm2_bwd, [h2, dh3, dkvn, dqn], [w["kv_norm"], w["b_norm"]],
                                                    [row(), rowb()], [vec(), vec()], name="res_norm_kvq_bwd")

    dhn0, g["mlp_w_up0"], g["mlp_w_down0"] = _mlp_bwd(hn0, u0, dh2b, w["mlp_w_up0"], w["mlp_w_down0"], 0, up_dev_major, distributed)
    dh1, dh1b, g["mlp_norm0"] = _rowwise(res_norm_bwd, [h1, dh2, dhn0], [w["mlp_norm0"]], [row(), rowb()], [vec()], name="res_norm0_bwd")

    g["a_w_out"] = square_dw(yg, dh1b, "tmix_out_dw")
    dyg = _mm(dh1b, w["a_w_out"], tb=True, name="tmix_out_dx")
    dy2, dgate = _rowwise(lambda d, a, b: (d * b, d * a), [dyg, y2, gate], [], [row(), rowb()], name="tmix_gate_bwd")
    g["a_g2"] = _mm(lg1, dgate, ta=True, a_pro=_sigmoid, name="tmix_g2_dw")

    def dsigmoid(rr, z):
        s = jax.nn.sigmoid(z)
        return rr * s * (1.0 - s)

    dlg1 = _mm(dgate, w["a_g2"], tb=True, epi=dsigmoid, epi_args=(lg1,), out_dtype=BF16, name="tmix_g2_dx")
    g["a_g1"] = _mm(xg, dlg1, ta=True, name="tmix_g1_dw")
    dxg = _mm(dlg1, w["a_g1"], tb=True, name="tmix_g1_dx")

    early_sums = [g[name] for name in _EARLY_GRADS] if distributed else ()
    (dr, dk, dv, dwl, dal), param_grads, early_reduced = _wkv_bwd(r, k, v, wl, al, wkv_params, states, dy2, early_sums)
    for name, pg in zip(("a_w0", "a_a0", "a_k_k", "a_k_a", "a_ln_x_w", "a_ln_x_b", "a_r_k"), param_grads):
        g[name] = pg.reshape(1, D_MODEL)

    g["a_w_r"] = square_dw(xr, dr, "tmix_r_dw")
    g["a_w_k"] = square_dw(xk, dk, "tmix_k_dw")
    g["a_w_v"] = square_dw(xv, dv, "tmix_v_dw")
    dxr = _mm(dr, w["a_w_r"], tb=True, name="tmix_r_dx")
    dxk = _mm(dk, w["a_w_k"], tb=True, name="tmix_k_dx")
    dxv = _mm(dv, w["a_w_v"], tb=True, name="tmix_v_dx")
    g["a_w2"] = _mm(lw1, dwl, ta=True, a_pro=jnp.tanh, name="tmix_w2_dw")

    def dtanh(rr, z):
        th = jnp.tanh(z)
        return rr * (1.0 - th * th)

    dlw1 = _mm(dwl, w["a_w2"], tb=True, epi=dtanh, epi_args=(lw1,), out_dtype=BF16, name="tmix_w2_dx")
    g["a_w1"] = _mm(xw, dlw1, ta=True, name="tmix_w1_dw")
    dxw = _mm(dlw1, w["a_w1"], tb=True, name="tmix_w1_dx")
    g["a_a2"] = _mm(la1, dal, ta=True, name="tmix_a2_dw")
    dla1 = _mm(dal, w["a_a2"], tb=True, out_dtype=BF16, name="tmix_a2_dx")
    g["a_a1"] = _mm(xa, dla1, ta=True, name="tmix_a1_dw")
    dxa = _mm(dla1, w["a_a1"], tb=True, name="tmix_a1_dx")

    lerp_bwd = _vjp_of(_time_shift_lerps, 9, tuple(range(9)))

    def lerp_bwd_rows(x_, xs_, d0, d1, d2, d3, d4, d5, gain, *mx):
        return lerp_bwd(x_, xs_, gain, *mx, d0, d1, d2, d3, d4, d5)

    outs = _rowwise(lerp_bwd_rows, [x, xs, dxr, dxk, dxv, dxw, dxa, dxg], [w["a_norm"]] + mix, [row(), row()], [vec()] * 7,
                    tm=128, name="tmix_lerp_bwd")
    dx_a, dxs, g["a_norm"] = outs[0], outs[1], outs[2]
    g["a_mix"] = jnp.concatenate(outs[3:9], axis=0)
    (grad_x,) = _rowwise(lambda a, b, c: (a + b + c,), [dh1, dx_a, _shift_up(dxs)], [], [row()], name="grad_x_sum")
    return loss, grad_x, g, (early_reduced if late_shards is not None else None)


_ANY = pl.BlockSpec(memory_space=pl.ANY)
_MESH_ID = pl.DeviceIdType.MESH


def _linear(pos):
    return 4 * pos[0] + 2 * pos[1] + pos[2]


def _all_gather(shards, name):
    n = len(shards)

    def body(*refs):
        start, relay, finish = _gather_plan(refs[:n], refs[n:2 * n], *refs[2 * n:])
        start()
        relay()
        finish()

    return pl.pallas_call(
        body, out_shape=[_sds((N_DEV,) + s.shape, s.dtype) for s in shards], in_specs=[_ANY] * n, out_specs=[_ANY] * n,
        scratch_shapes=_gather_semaphores(n), name=name,
    )(*shards)


GATHER_COPIES = 8


def _gather_semaphores(n):
    return [pltpu.SemaphoreType.DMA((n * GATHER_COPIES,)), pltpu.SemaphoreType.DMA((n * GATHER_COPIES,)),
            pltpu.SemaphoreType.DMA((n,))]


def _gather_plan(ins, outs, send_sems, recv_sems, local_sems):
    n = len(ins)
    x, y, c = lax.axis_index("x"), lax.axis_index("y"), lax.axis_index("c")
    me, sibling = (x, y, c), (x, y, 1 - c)
    x_nbr, y_nbr, diag = (1 - x, y, c), (x, 1 - y, c), (1 - x, 1 - y, c)
    other = lambda pos: (pos[0], pos[1], 1 - c)

    def halves(a):
        rows = ins[a].shape[0]
        if rows % 32:
            return (0, rows), None
        return (0, rows // 2), (rows // 2, rows // 2)

    def copy(a, k, block, to, src=None, rows=None):
        dst = outs[a].at[_linear(block)]
        if rows is not None:
            dst = dst.at[pl.ds(rows[0], rows[1])]
        return pltpu.make_async_remote_copy(
            src_ref=dst if src is None else src, dst_ref=dst, send_sem=send_sems.at[a * GATHER_COPIES + k],
            recv_sem=recv_sems.at[a * GATHER_COPIES + k], device_id=to, device_id_type=_MESH_ID)

    def own_copies():
        mine = [pltpu.make_async_copy(ins[a], outs[a].at[_linear(me)], local_sems.at[a]) for a in range(n)]
        sent = []
        for a in range(n):
            sent += [copy(a, 0, me, sibling, src=ins[a]), copy(a, 1, me, x_nbr, src=ins[a]), copy(a, 2, me, y_nbr, src=ins[a])]
        return mine, sent

    def relayed_copies():
        sent = []
        for a in range(n):
            first, second = halves(a)
            sent += [copy(a, 3, x_nbr, y_nbr, rows=first), copy(a, 5, x_nbr, sibling), copy(a, 6, y_nbr, sibling)]
            if second is not None:
                sent.append(copy(a, 4, y_nbr, x_nbr, rows=second))
        return sent

    def start():
        mine, sent = own_copies()
        for cp in mine + sent:
            cp.start()

    def relay():
        for a in range(n):
            first, second = halves(a)
            copy(a, 1, x_nbr, me).wait_recv()
            copy(a, 3, x_nbr, y_nbr, rows=first).start()
            copy(a, 5, x_nbr, sibling).start()
        for a in range(n):
            first, second = halves(a)
            copy(a, 2, y_nbr, me).wait_recv()
            if second is not None:
                copy(a, 4, y_nbr, x_nbr, rows=second).start()
            copy(a, 6, y_nbr, sibling).start()

    def finish():
        mine, sent = own_copies()
        sent += relayed_copies()
        for a in range(n):
            first, second = halves(a)
            copy(a, 3, diag, me, rows=first).wait_recv()
            if second is not None:
                copy(a, 4, diag, me, rows=second).wait_recv()
            last = copy(a, 7, diag, sibling)
            last.start()
            sent.append(last)
        for a in range(n):
            copy(a, 0, other(me), me).wait_recv()
            copy(a, 5, other(x_nbr), me).wait_recv()
            copy(a, 6, other(y_nbr), me).wait_recv()
            copy(a, 7, other(diag), me).wait_recv()
        for cp in sent:
            cp.wait_send()
        for cp in mine:
            cp.wait()

    return start, relay, finish


N_CHIPS = 4


def _exchange_with_sibling(parts, name):
    n = len(parts)

    def body(*refs):
        ins, outs = refs[:n], refs[n:2 * n]
        send_sems, recv_sems = refs[2 * n:]
        x, y, c = lax.axis_index("x"), lax.axis_index("y"), lax.axis_index("c")
        copies = [pltpu.make_async_remote_copy(
            src_ref=ins[a].at[2 * q + (1 - c)], dst_ref=outs[a].at[q], send_sem=send_sems.at[a * N_CHIPS + q],
            recv_sem=recv_sems.at[a * N_CHIPS + q], device_id=(x, y, 1 - c), device_id_type=_MESH_ID)
            for a in range(n) for q in range(N_CHIPS)]
        for cp in copies:
            cp.start()
        for cp in copies:
            cp.wait()

    return pl.pallas_call(
        body, out_shape=[_sds((N_CHIPS,) + p.shape[1:], p.dtype) for p in parts], in_specs=[_ANY] * n, out_specs=[_ANY] * n,
        scratch_shapes=[pltpu.SemaphoreType.DMA((n * N_CHIPS,)), pltpu.SemaphoreType.DMA((n * N_CHIPS,))], name=name,
    )(*parts)


def _pair_sum(part, recv, core, out_dtype, name):
    _, r, cdim = recv.shape
    tr = max(8, min(r, (1 << 18) // cdim))
    assert r % tr == 0, (name, r, tr)

    def body(core_ref, p_ref, r_ref, o_ref):
        o_ref[...] = (p_ref[...] + r_ref[...]).astype(o_ref.dtype)

    grid_spec = pltpu.PrefetchScalarGridSpec(
        num_scalar_prefetch=1, grid=(N_CHIPS, r // tr),
        in_specs=[pl.BlockSpec((None, None, tr, cdim), lambda q, i, core_ref: (q, core_ref[0], i, 0)),
                  pl.BlockSpec((None, tr, cdim), lambda q, i, core_ref: (q, i, 0))],
        out_specs=pl.BlockSpec((None, tr, cdim), lambda q, i, core_ref: (q, i, 0)))
    return pl.pallas_call(
        body, grid_spec=grid_spec, out_shape=_sds((N_CHIPS, r, cdim), out_dtype),
        compiler_params=_params(("parallel", "parallel")), name=name,
    )(core, part.reshape(N_CHIPS, 2, r, cdim), recv)


def _exchange_between_chips(parts, name):
    n = len(parts)

    def body(*refs):
        start, finish = _chip_exchange_plan(refs[:n], refs[n:2 * n], *refs[2 * n:])
        start()
        finish()

    return pl.pallas_call(
        body, out_shape=[_sds(p.shape, p.dtype) for p in parts], in_specs=[_ANY] * n, out_specs=[_ANY] * n,
        scratch_shapes=_chip_exchange_semaphores(n), name=name,
    )(*parts)


def _chip_exchange_semaphores(n):
    n_other = N_CHIPS - 1
    return [pltpu.SemaphoreType.DMA((n * n_other,)), pltpu.SemaphoreType.DMA((n * n_other,)), pltpu.SemaphoreType.DMA((n,))]


def _chip_exchange_plan(ins, outs, send_sems, recv_sems, local_sems):
    n = len(ins)
    n_other = N_CHIPS - 1
    x, y, c = lax.axis_index("x"), lax.axis_index("y"), lax.axis_index("c")
    my_chip = 2 * x + y

    def all_copies():
        mine = [pltpu.make_async_copy(ins[a].at[my_chip], outs[a].at[my_chip], local_sems.at[a]) for a in range(n)]
        remote = []
        for j, (fx, fy) in enumerate([(1, 0), (0, 1), (1, 1)]):
            px, py = (1 - x if fx else x), (1 - y if fy else y)
            for a in range(n):
                remote.append(pltpu.make_async_remote_copy(
                    src_ref=ins[a].at[2 * px + py], dst_ref=outs[a].at[my_chip], send_sem=send_sems.at[a * n_other + j],
                    recv_sem=recv_sems.at[a * n_other + j], device_id=(px, py, c), device_id_type=_MESH_ID))
        return mine, remote

    def start():
        mine, remote = all_copies()
        for cp in mine + remote:
            cp.start()

    def finish():
        mine, remote = all_copies()
        for cp in remote + mine:
            cp.wait()

    return start, finish


def _chip_sums(parts, names, tag):
    from_sibling = _exchange_with_sibling(parts, name="scatter_grads_sibling_" + tag)
    core = lax.axis_index("c").astype(jnp.int32).reshape(1)
    return [_pair_sum(p, r, core, F32 if nm.startswith("pack") else BF16, name="pair_sum_" + nm)
            for p, r, nm in zip(parts, from_sibling, names)]


def _adamw(w, m, v, slots, name, layer=0, n_layers=1, into=None):
    r, c = w.shape[-2:]
    ns = slots.shape[0]
    tr = max(8, min(r, (1 << 18) // c))
    assert r % tr == 0, (name, r, tr)

    def body(w_ref, m_ref, v_ref, g_ref, *rest):
        g_out, d_out, m_out, v_out = rest[-4:]
        g = g_ref[0].astype(F32)
        for s in range(1, ns):
            g = g + g_ref[s].astype(F32)
        m_new = ADAM_B1 * m_ref[...] + (1.0 - ADAM_B1) * g
        v_new = ADAM_B2 * v_ref[...] + (1.0 - ADAM_B2) * jnp.square(g)
        m_hat = m_new / (1.0 - ADAM_B1 ** ADAM_STEP)
        v_hat = v_new / (1.0 - ADAM_B2 ** ADAM_STEP)
        d_out[...] = -ADAM_LR * (m_hat / (jnp.sqrt(v_hat) + ADAM_EPS) + ADAM_WD * w_ref[...])
        g_out[...], m_out[...], v_out[...] = g, m_new, v_new

    if n_layers == 1:
        spec, out_shape, earlier, aliases = pl.BlockSpec((tr, c), lambda i: (i, 0)), _sds((r, c)), [], {}
    else:
        spec, out_shape = pl.BlockSpec((None, tr, c), lambda i: (layer, i, 0)), _sds((n_layers, r, c))
        earlier = list(into) if into is not None else []
        aliases = {4 + i: i for i in range(len(earlier))}
    in_specs = [spec, spec, spec, pl.BlockSpec((ns, tr, c), lambda i: (0, i, 0))]
    return pl.pallas_call(
        body, grid=(r // tr,), in_specs=in_specs + [_ANY] * len(earlier), out_specs=[spec] * 4, out_shape=[out_shape] * 4,
        input_output_aliases=aliases, compiler_params=_params(("parallel",)), name=name,
    )(w, m, v, slots, *earlier)


_COL_VECTORS = ("a_norm", "a_mix", "a_w0", "a_a0", "a_k_k", "a_k_a", "a_ln_x_w", "a_ln_x_b")
_COL_VEC_ROWS = 16
_COL_ROWS = _COL_VEC_ROWS + 2 * LORA_PAD + 256
_ROW_COLS = 2 * LORA_PAD + 256 + 512
_REPL_ROWS = 8


def _pad_to(a, size, axis):
    widths = [(0, 0)] * a.ndim
    widths[axis] = (0, size - a.shape[axis])
    return jnp.pad(a, widths)


def _pack_cols(p):
    width = p["a_norm"].shape[-1]
    vecs = jnp.concatenate([p[n].reshape(-1, width) for n in _COL_VECTORS], axis=0)
    return jnp.concatenate([_pad_to(vecs, _COL_VEC_ROWS, 0), _pad_to(p["a_w2"].reshape(-1, width), LORA_PAD, 0),
                            _pad_to(p["a_a2"].reshape(-1, width), LORA_PAD, 0), p["a_g2"].reshape(-1, width)], axis=0)


def _unpack_cols(a, lead):
    width = a.shape[-1]
    out, row = {}, 0
    for n in _COL_VECTORS:
        k = 6 if n == "a_mix" else 1
        out[n] = a[row:row + k].reshape(lead + ((6, width) if n == "a_mix" else (width,)))
        row += k
    base = _COL_VEC_ROWS
    out["a_w2"] = a[base:base + 96].reshape(lead + (96, width))
    out["a_a2"] = a[base + LORA_PAD:base + LORA_PAD + 96].reshape(lead + (96, width))
    out["a_g2"] = a[base + 2 * LORA_PAD:].reshape(lead + (256, width))
    return out


def _pack_rows(p):
    rows = p["w_kv"].shape[0]
    return jnp.concatenate([_pad_to(p["a_w1"].reshape(rows, -1), LORA_PAD, 1), _pad_to(p["a_a1"].reshape(rows, -1), LORA_PAD, 1),
                            p["a_g1"].reshape(rows, -1), p["w_kv"]], axis=1)


def _unpack_rows(a, lead):
    rows = a.shape[0]
    return {"a_w1": a[:, :96].reshape(lead + (rows, 96)), "a_a1": a[:, LORA_PAD:LORA_PAD + 96].reshape(lead + (rows, 96)),
            "a_g1": a[:, 2 * LORA_PAD:2 * LORA_PAD + 256].reshape(lead + (rows, 256)), "w_kv": a[:, 2 * LORA_PAD + 256:]}


def _pack_repl(p):
    row = lambda a: _pad_to(a.reshape(1, -1), D_MODEL, 1)
    return jnp.concatenate([p["mlp_norm"].reshape(2, D_MODEL), row(p["kv_norm"]), row(p["b_norm"]), row(p["a_r_k"]),
                            row(p["k_norm"]), row(p["b_q_norm"]), row(p["b_sinks"])], axis=0)


def _unpack_repl(a):
    return {"mlp_norm": a[0:2], "kv_norm": a[2], "b_norm": a[3:4], "a_r_k": a[4].reshape(1, N_HEADS, HEAD_DIM),
            "k_norm": a[5, :HEAD_DIM], "b_q_norm": a[6:7, :HEAD_DIM], "b_sinks": a[7:8, :N_HEADS]}


_WEIGHTS = ("a_norm", "a_mix", "a_w_rkv", "a_w0", "a_w1", "a_w2", "a_a0", "a_a1", "a_a2", "a_g1", "a_g2", "a_k_k", "a_k_a",
            "a_r_k", "a_ln_x_w", "a_ln_x_b", "a_w_out", "mlp_norm", "mlp_w_up", "mlp_w_down", "kv_norm", "w_kv", "k_norm",
            "b_norm", "b_w_q", "b_q_norm", "b_sinks", "b_w_o")


def _big_shards(p):
    return [p["a_w_rkv"][0, 0], p["a_w_rkv"][0, 1], p["a_w_rkv"][0, 2], p["a_w_out"][0], p["mlp_w_up"][0], p["mlp_w_up"][1],
            p["mlp_w_down"][0], p["mlp_w_down"][1], p["b_w_q"][0], p["b_w_o"][0]]


_BIG_NAMES = ("a_w_r", "a_w_k", "a_w_v", "a_w_out", "mlp_w_up0", "mlp_w_up1", "mlp_w_down0", "mlp_w_down1", "b_w_q", "b_w_o")


def kernel(x, a_norm, a_mix, a_w_rkv, a_w0, a_w1, a_w2, a_a0, a_a1, a_a2, a_g1, a_g2, a_k_k, a_k_a, a_r_k, a_ln_x_w,
           a_ln_x_b, a_w_out, mlp_norm, mlp_w_up, mlp_w_down, kv_norm, w_kv, k_norm, b_norm, b_w_q, b_q_norm, b_sinks,
           b_w_o, loss_target, m_a_norm, m_a_mix, m_a_w_rkv, m_a_w0, m_a_w1, m_a_w2, m_a_a0, m_a_a1, m_a_a2, m_a_g1,
           m_a_g2, m_a_k_k, m_a_k_a, m_a_r_k, m_a_ln_x_w, m_a_ln_x_b, m_a_w_out, m_mlp_norm, m_mlp_w_up, m_mlp_w_down,
           m_kv_norm, m_w_kv, m_k_norm, m_b_norm, m_b_w_q, m_b_q_norm, m_b_sinks, m_b_w_o, v_a_norm, v_a_mix, v_a_w_rkv,
           v_a_w0, v_a_w1, v_a_w2, v_a_a0, v_a_a1, v_a_a2, v_a_g1, v_a_g2, v_a_k_k, v_a_k_a, v_a_r_k, v_a_ln_x_w,
           v_a_ln_x_b, v_a_w_out, v_mlp_norm, v_mlp_w_up, v_mlp_w_down, v_kv_norm, v_w_kv, v_k_norm, v_b_norm, v_b_w_q,
           v_b_q_norm, v_b_sinks, v_b_w_o):
    given = locals()
    wts = {n: given[n] for n in _WEIGHTS}
    mom = {n: given["m_" + n] for n in _WEIGHTS}
    var = {n: given["v_" + n] for n in _WEIGHTS}

    cols_w, rows_w, repl_w = _pack_cols(wts), _pack_rows(wts), _pack_repl(wts)
    big_w = _big_shards(wts)
    big_bf16 = dict(zip(_BIG_NAMES, [b.astype(BF16) for b in big_w]))
    first_names = [k for k in _BIG_NAMES if k not in _LATE_GATHER]
    lora_in = rows_w[:, :2 * LORA_PAD]
    gathered = _all_gather([cols_w, lora_in] + [big_bf16[k] for k in first_names], name="gather_weights")
    full_cols = gathered[0].transpose(1, 0, 2).reshape(_COL_ROWS, D_MODEL)
    full_lora_in = gathered[1].reshape(D_MODEL, 2 * LORA_PAD)
    w = {}
    w.update({k: v.reshape(v.shape[1:]) for k, v in _unpack_cols(full_cols, (1,)).items()})
    for k in ("a_norm", "a_w0", "a_a0", "a_k_k", "a_k_a", "a_ln_x_w", "a_ln_x_b"):
        w[k] = w[k].reshape(1, D_MODEL)
    for k in ("a_w2", "a_a2"):
        w[k] = _pad_to(w[k], LORA_PAD, 0)
    w["a_w1"], w["a_a1"] = full_lora_in[:, :LORA_PAD], full_lora_in[:, LORA_PAD:]
    for k, arr in zip(first_names, gathered[2:]):
        w[k] = arr.reshape(N_DEV * arr.shape[1], arr.shape[2])
    late_shards = [big_bf16[k] for k in _LATE_WEIGHTS] + [big_bf16["a_w_out"], a_g1[0], w_kv]
    w["mlp_norm0"], w["mlp_norm1"] = mlp_norm[0:1], mlp_norm[1:2]
    w["kv_norm"], w["k_norm"] = kv_norm.reshape(1, D_MODEL), k_norm.reshape(1, HEAD_DIM)
    w["b_norm"], w["b_q_norm"], w["b_sinks"], w["a_r_k"] = b_norm, b_q_norm, b_sinks, a_r_k.reshape(1, D_MODEL)

    loss_local, grad_x, g, early_reduced = _local_step(x[0], loss_target[0], w, late_shards)
    loss = lax.psum(loss_local[0, 0], MESH_AXES)

    g_lead = {k: g[k][None] for k in ("a_norm", "a_mix", "a_w0", "a_a0", "a_k_k", "a_k_a", "a_ln_x_w", "a_ln_x_b", "a_g2")}
    g_lead["a_w2"], g_lead["a_a2"] = g["a_w2"][None, :96], g["a_a2"][None, :96]
    g_cols = _pack_cols(g_lead).reshape(_COL_ROWS, N_DEV, D_MODEL // N_DEV).transpose(1, 0, 2)
    g_rows = _pack_rows({"a_w1": g["a_w1"][:, :96], "a_a1": g["a_a1"][:, :96], "a_g1": g["a_g1"], "w_kv": g["w_kv"]})
    g_rows = g_rows.reshape(N_DEV, D_MODEL // N_DEV, _ROW_COLS)
    late_names = tuple(k for k in _BIG_NAMES if k not in _EARLY_GRADS)
    pack_sums = _chip_sums([g_cols, g_rows], ("pack_cols", "pack_rows"), "late")
    late_reduced = _exchange_between_chips(pack_sums + [g[k] for k in late_names], name="scatter_grads_chips_late")
    big_reduced = dict(zip(late_names, late_reduced[2:]))
    big_reduced.update(zip(_EARLY_GRADS, early_reduced))
    reduced = list(late_reduced[:2]) + [big_reduced[k] for k in _BIG_NAMES]
    g_repl = _pack_repl({"mlp_norm": jnp.concatenate([g["mlp_norm0"], g["mlp_norm1"]], axis=0), "kv_norm": g["kv_norm"],
                         "b_norm": g["b_norm"], "a_r_k": g["a_r_k"], "k_norm": g["k_norm"], "b_q_norm": g["b_q_norm"],
                         "b_sinks": g["b_sinks"]})
    (repl_slots,) = _all_gather([g_repl], name="gather_replicated_grads")

    res = {}
    cols4 = _adamw(cols_w, _pack_cols(mom), _pack_cols(var), reduced[0], name="adamw_cols")
    rows4 = _adamw(rows_w, _pack_rows(mom), _pack_rows(var), reduced[1], name="adamw_rows")
    repl4 = _adamw(repl_w, _pack_repl(mom), _pack_repl(var), repl_slots, name="adamw_replicated")
    for unpacked in ([_unpack_cols(a, (1,)) for a in cols4], [_unpack_rows(a, (1,)) for a in rows4], [_unpack_repl(a) for a in repl4]):
        for k in unpacked[0]:
            res[k] = tuple(u[k] for u in unpacked)
    stacked = {"a_w_r": ("a_w_rkv", 0), "a_w_k": ("a_w_rkv", 1), "a_w_v": ("a_w_rkv", 2), "mlp_w_up0": ("mlp_w_up", 0),
               "mlp_w_up1": ("mlp_w_up", 1), "mlp_w_down0": ("mlp_w_down", 0), "mlp_w_down1": ("mlp_w_down", 1)}
    big4, earlier = {}, None
    for k, bw, bm, bv, slots in zip(_BIG_NAMES, big_w, _big_shards(mom), _big_shards(var), reduced[2:]):
        if k in stacked:
            src, layer = stacked[k]
            as_layers = lambda a: a.reshape((-1,) + a.shape[-2:])
            earlier = _adamw(as_layers(wts[src]), as_layers(mom[src]), as_layers(var[src]), slots, name="adamw_" + k,
                             layer=layer, n_layers=as_layers(wts[src]).shape[0], into=earlier if layer else None)
        else:
            earlier = _adamw(bw, bm, bv, slots, name="adamw_" + k)
        big4[k] = earlier
    res["a_w_rkv"] = tuple(a[None] for a in big4["a_w_v"])
    res["a_w_out"] = tuple(a[None] for a in big4["a_w_out"])
    res["mlp_w_up"], res["mlp_w_down"] = tuple(big4["mlp_w_up1"]), tuple(big4["mlp_w_down1"])
    res["b_w_q"] = tuple(a[None] for a in big4["b_w_q"])
    res["b_w_o"] = tuple(a[None] for a in big4["b_w_o"])
    res["w_kv"] = tuple(a.reshape(w_kv.shape) for a in res["w_kv"])

    outs = [loss, grad_x[None]]
    for i in range(4):
        outs += [res[n][i].reshape(given[n].shape) for n in _WEIGHTS]
    return tuple(outs)
```

```python
import functools
import math

import jax
import jax.numpy as jnp
from jax import lax
from jax.experimental import pallas as pl
from jax.experimental.pallas import tpu as pltpu

F32 = jnp.float32
BF16 = jnp.bfloat16

D_MODEL = 2048
N_HEADS = 32
HEAD_DIM = 64
N_KV_HEADS = 4
Q_PER_KV = 8
ATT_BLOCK = 128
WKV_CHUNK = 64
LORA_PAD = 128
D_FF = 8192
N_DEV = 8
RMS_EPS = 1e-6
GN_EPS = 64e-5
L2_EPS = 1e-12
ROPE_THETA = 10000.0
ADAM_LR, ADAM_B1, ADAM_B2, ADAM_EPS, ADAM_WD, ADAM_STEP = 0.001, 0.9, 0.999, 1e-08, 0.01, 10
MASK_VALUE = -1e30
VMEM_LIMIT_BYTES = 56 * 1024 * 1024
MM_TILE_M, MM_TILE_N, MM_TILE_K = 1024, 1024, 2048
STREAM_BLOCK_ELEMENTS = 1 << 18
BF16_SUBLANES = 16
MESH_AXES = ("x", "y", "c")

_NN = (((1,), (0,)), ((), ()))
_NT = (((1,), (1,)), ((), ()))
_TN = (((0,), (0,)), ((), ()))
_BNN = (((2,), (1,)), ((0,), (0,)))
_BNT = (((2,), (2,)), ((0,), (0,)))
_BTN = (((1,), (1,)), ((0,), (0,)))


def _params(sem):
    return pltpu.CompilerParams(dimension_semantics=sem, vmem_limit_bytes=VMEM_LIMIT_BYTES)


def _split2(a):
    hi = a.astype(BF16)
    return hi, (a - hi.astype(F32)).astype(BF16)


def _dot3(a, b, dims):
    ah, al = _split2(a)
    bh, bl = _split2(b)
    d = lambda p, q: lax.dot_general(p, q, dims, preferred_element_type=F32)
    return d(ah, bh) + (d(al, bh) + d(ah, bl))


@functools.partial(jax.custom_vjp, nondiff_argnums=(2,))
def _hdot(a, b, dims=_NN):
    return _dot3(a, b, dims)


def _hdot_fwd(a, b, dims):
    return _dot3(a, b, dims), (a, b)


def _hdot_bwd(dims, res, g):
    a, b = res
    nn, nt, tn = (_NN, _NT, _TN) if dims in (_NN, _NT, _TN) else (_BNN, _BNT, _BTN)
    if dims == nn:
        return _dot3(g, b, nt), _dot3(a, g, tn)
    if dims == nt:
        return _dot3(g, b, nn), _dot3(g, a, tn)
    assert dims == tn
    return _dot3(b, g, nt), _dot3(a, g, nn)


_hdot.defvjp(_hdot_fwd, _hdot_bwd)


def _tri_parts(x):
    hi = x.astype(BF16)
    r1 = x - hi.astype(F32)
    mid = r1.astype(BF16)
    return hi, mid, (r1 - mid.astype(F32)).astype(BF16)


@jax.custom_vjp
def _mask_dot(mask, x):
    mb = mask.astype(BF16)
    p0, p1, p2 = _tri_parts(x)
    d = lambda p: lax.dot_general(mb, p, _BNN, preferred_element_type=F32)
    return d(p0) + (d(p1) + d(p2))


def _mask_dot_fwd(mask, x):
    return _mask_dot(mask, x), mask


def _mask_dot_bwd(mask, g):
    mb = mask.astype(BF16)
    p0, p1, p2 = _tri_parts(g)
    d = lambda p: lax.dot_general(mb, p, _BTN, preferred_element_type=F32)
    return jnp.zeros_like(mask), d(p0) + (d(p1) + d(p2))


_mask_dot.defvjp(_mask_dot_fwd, _mask_dot_bwd)


def _b16dot(a, b, dims):
    return lax.dot_general(a.astype(BF16), b.astype(BF16), dims, preferred_element_type=F32)


@jax.custom_vjp
def _bdot(a, b):
    return _b16dot(a, b, _NN)


def _bdot_fwd(a, b):
    return _b16dot(a, b, _NN), (a, b)


def _bdot_bwd(res, g):
    a, b = res
    return _b16dot(g, b, _NT), _b16dot(a, g, _TN)


_bdot.defvjp(_bdot_fwd, _bdot_bwd)


@jax.custom_vjp
def _bdot_nt(a, b):
    return _b16dot(a, b, _NT)


def _bdot_nt_fwd(a, b):
    return _b16dot(a, b, _NT), (a, b)


def _bdot_nt_bwd(res, g):
    a, b = res
    return _b16dot(g, b, _NN), _b16dot(g, a, _TN)


_bdot_nt.defvjp(_bdot_nt_fwd, _bdot_nt_bwd)


def _rms(x, gain):
    return x * lax.rsqrt(jnp.mean(x * x, axis=-1, keepdims=True) + RMS_EPS) * gain


def _vjp_of(f, n_in, diff):
    def g(*args):
        ins, cts = args[:n_in], args[n_in:]

        def fd(*d):
            full = list(ins)
            for pos, i in enumerate(diff):
                full[i] = d[pos]
            return f(*full)

        _, pull = jax.vjp(fd, *[ins[i] for i in diff])
        return pull(tuple(cts))
    return g


def _mm(a, b, *, name, ta=False, tb=False, a_pro=None, epi=None, epi_args=(), out_dtype=F32,
        tm=MM_TILE_M, tn=MM_TILE_N, tk=MM_TILE_K, dims=None, b_spec=None, o_spec=None, o_shape=None):
    if dims is None:
        m, k = (a.shape[1], a.shape[0]) if ta else a.shape
        n = b.shape[0] if tb else b.shape[1]
    else:
        m, n, k = dims
    tm, tn, tk = min(tm, m), min(tn, n), min(tk, k)
    assert m % tm == 0 and n % tn == 0 and k % tk == 0, (name, m, n, k, tm, tn, tk)
    nk = k // tk
    ne = len(epi_args)
    cdims = (((0 if ta else 1,), (1 if tb else 0,)), ((), ()))

    def body(a_ref, b_ref, *rest):
        e_refs, o_ref, acc = rest[:ne], rest[ne], rest[ne + 1]
        kk = pl.program_id(2)

        @pl.when(kk == 0)
        def _():
            acc[...] = jnp.zeros_like(acc)

        av = a_ref[...]
        if a_pro is not None:
            av = a_pro(av.astype(F32))
        acc[...] += lax.dot_general(av.astype(BF16), b_ref[...].astype(BF16), cdims, preferred_element_type=F32)

        @pl.when(kk == nk - 1)
        def _():
            r = acc[...]
            if epi is not None:
                r = epi(r, *[e[...] for e in e_refs])
            o_ref[...] = r.astype(o_ref.dtype)

    a_spec = pl.BlockSpec((tk, tm), lambda i, j, q: (q, i)) if ta else pl.BlockSpec((tm, tk), lambda i, j, q: (i, q))
    if b_spec is None:
        b_spec = pl.BlockSpec((tn, tk), lambda i, j, q: (j, q)) if tb else pl.BlockSpec((tk, tn), lambda i, j, q: (q, j))
    if o_spec is None:
        o_spec = pl.BlockSpec((tm, tn), lambda i, j, q: (i, j))
        o_shape = (m, n)
    e_specs = [pl.BlockSpec((tm, tn), lambda i, j, q: (i, j)) for _ in epi_args]
    return pl.pallas_call(
        body, grid=(m // tm, n // tn, nk), in_specs=[a_spec, b_spec] + e_specs, out_specs=o_spec,
        out_shape=jax.ShapeDtypeStruct(o_shape, out_dtype), scratch_shapes=[pltpu.VMEM((tm, tn), F32)],
        compiler_params=_params(("parallel", "parallel", "arbitrary")), name=name,
    )(a, b, *epi_args)


def _rowwise(fn, rows, params, out_rows, out_params=(), *, tm=256, name, chip_sums=()):
    t = rows[0].shape[0]
    tm = min(tm, t)
    assert t % tm == 0
    nr, npar, nor, nop, ns = len(rows), len(params), len(out_rows), len(out_params), len(chip_sums)
    steps = t // tm

    def body(*refs):
        r, p = refs[:nr], refs[nr:nr + npar]
        o, op = refs[nr + npar + ns:nr + npar + ns + nor], refs[nr + npar + ns + nor:nr + npar + ns + nor + nop]
        if ns:
            start, finish = _chip_exchange_plan(refs[nr + npar:nr + npar + ns], refs[nr + npar + ns + nor + nop:nr + npar + 2 * ns + nor + nop],
                                                *refs[nr + npar + 2 * ns + nor + nop:])
            pl.when(pl.program_id(0) == 0)(start)
        outs = fn(*[x[...] for x in r], *[x[...] for x in p])
        for ref, val in zip(o, outs[:nor]):
            ref[...] = val.astype(ref.dtype)
        if nop:
            @pl.when(pl.program_id(0) == 0)
            def _():
                for ref in op:
                    ref[...] = jnp.zeros_like(ref)

            for ref, val in zip(op, outs[nor:]):
                ref[...] += val.astype(F32)
        if ns:
            pl.when(pl.program_id(0) == steps - 1)(finish)

    in_specs = [pl.BlockSpec((tm, x.shape[1]), lambda i: (i, 0)) for x in rows]
    in_specs += [pl.BlockSpec(p.shape, lambda i: (0, 0)) for p in params]
    out_specs = [pl.BlockSpec((tm, s.shape[1]), lambda i: (i, 0)) for s in out_rows]
    out_specs += [pl.BlockSpec(s.shape, lambda i: (0, 0)) for s in out_params]
    return pl.pallas_call(
        body, grid=(steps,), in_specs=in_specs + [_ANY] * ns, out_specs=out_specs + [_ANY] * ns,
        out_shape=list(out_rows) + list(out_params) + [_sds(c.shape, c.dtype) for c in chip_sums],
        scratch_shapes=_chip_exchange_semaphores(ns) if ns else [],
        compiler_params=_params(("arbitrary",)), name=name,
    )(*rows, *params, *chip_sums)


def _sds(shape, dtype=F32):
    return jax.ShapeDtypeStruct(tuple(shape), dtype)


def _doubling_powers(l):
    powers = [l]
    for _ in range(int(math.log2(l.shape[-1])) - 1):
        powers.append(_dot3(powers[-1], powers[-1], _BNN))
    return powers


@jax.custom_vjp
def _unit_lower_solve(l, z):
    u = z
    for p in _doubling_powers(l):
        u = u + _dot3(p, u, _BNN)
    return u


def _unit_lower_solve_fwd(l, z):
    powers = _doubling_powers(l)
    u = z
    for p in powers:
        u = u + _dot3(p, u, _BNN)
    return u, (powers, u)


def _unit_lower_solve_bwd(res, du):
    powers, u = res
    g = du
    for p in powers:
        g = g + _dot3(p, g, _BTN)
    return _dot3(g, u, _BNT), g


_unit_lower_solve.defvjp(_unit_lower_solve_fwd, _unit_lower_solve_bwd)


def _wkv_chunk(s0, r, lw, k, v, a, b):
    nb, c, _ = r.shape
    ti = lax.broadcasted_iota(jnp.int32, (nb, c, c), 1)
    si = lax.broadcasted_iota(jnp.int32, (nb, c, c), 2)
    incl, strict = si <= ti, si < ti
    cum = _mask_dot(incl.astype(F32), lw)
    tot = jnp.sum(lw, axis=1, keepdims=True)
    rcum = tot - cum
    w_inv = jnp.exp(-cum)
    at, rt, bt, kt = a * jnp.exp(cum - lw), r * jnp.exp(cum), b * w_inv, k * w_inv
    l_ab = jnp.where(strict, _hdot(at, bt, _BNT), 0.0)
    l_ak = jnp.where(strict, _hdot(at, kt, _BNT), 0.0)
    t_rb = jnp.where(incl, _hdot(rt, bt, _BNT), 0.0)
    t_rk = jnp.where(incl, _hdot(rt, kt, _BNT), 0.0)
    u = _unit_lower_solve(l_ab, _hdot(at, s0, _BNT) + _hdot(l_ak, v, _BNN))
    y = _hdot(rt, s0, _BNT) + _hdot(t_rb, u, _BNN) + _hdot(t_rk, v, _BNN)
    e = jnp.exp(rcum)
    s1 = s0 * jnp.exp(tot) + _hdot(u, b * e, _BTN) + _hdot(v, k * e, _BTN)
    return y, s1


WKV_HEADS_PER_STEP = 16
GATHER_RELAY_AT = 0.625


def _first_and_last_step(grid):
    i, j = pl.program_id(0), pl.program_id(1)
    return jnp.logical_and(i == 0, j == 0), jnp.logical_and(i == grid[0] - 1, j == grid[1] - 1)


N_WKV_PARAMS = 7


def _tmix_chunk(s0, r, k, v, wl, al, w0, a0, k_k, k_a, ln_w, ln_b, r_k):
    lw, kmod, a, b = _wkv_prep(k, wl, al, w0, a0, k_k, k_a)
    y, s1 = _wkv_chunk(s0, r, lw, kmod, v, a, b)
    (y2,) = _wkv_post(y, r, kmod, v, ln_w, ln_b, r_k)
    return y2, s1


def _split_heads(x, nh):
    return x.reshape(x.shape[0], nh, HEAD_DIM).transpose(1, 0, 2)


def _merge_heads(xh):
    return jnp.concatenate([xh[h] for h in range(xh.shape[0])], axis=1)


def _wkv_fwd(r, k, v, wl, al, params, shards=()):
    t = r.shape[0]
    nh, n = N_HEADS, HEAD_DIM
    nc = t // WKV_CHUNK
    hb = WKV_HEADS_PER_STEP
    grid = (nh // hb, nc)
    ns = len(shards)
    n_in = 5 + N_WKV_PARAMS

    def body(*refs):
        y_ref, s_ref = refs[n_in + ns:n_in + ns + 2]
        state = refs[n_in + 2 * ns + 2]
        if ns:
            start, relay, finish = _gather_plan(refs[n_in:n_in + ns], refs[n_in + ns + 2:n_in + 2 * ns + 2], *refs[n_in + 2 * ns + 3:])
            first, last = _first_and_last_step(grid)
            pl.when(first)(start)
            relay_step = int(GATHER_RELAY_AT * grid[0] * nc)
            pl.when(jnp.logical_and(pl.program_id(0) == relay_step // nc, pl.program_id(1) == relay_step % nc))(relay)

        @pl.when(pl.program_id(1) == 0)
        def _():
            state[...] = jnp.zeros_like(state)

        s0 = state[...]
        s_ref[:, 0] = s0
        rows = [_split_heads(ref[...], hb) for ref in refs[:5]]
        y, s1 = _tmix_chunk(s0, *rows, *[ref[...] for ref in refs[5:n_in]])
        y_ref[...] = _merge_heads(y)
        state[...] = s1
        if ns:
            pl.when(last)(finish)

    blk = pl.BlockSpec((WKV_CHUNK, hb * n), lambda h, c: (c, h))
    pblk = pl.BlockSpec((hb, 1, n), lambda h, c: (h, 0, 0))
    sblk = pl.BlockSpec((hb, 1, n, n), lambda h, c: (h, c, 0, 0))
    outs = pl.pallas_call(
        body, grid=grid, in_specs=[blk] * 5 + [pblk] * N_WKV_PARAMS + [_ANY] * ns, out_specs=[blk, sblk] + [_ANY] * ns,
        out_shape=[_sds((t, nh * n)), _sds((nh, nc, n, n))] + [_sds((N_DEV,) + s.shape, s.dtype) for s in shards],
        scratch_shapes=[pltpu.VMEM((hb, n, n), F32)] + (_gather_semaphores(ns) if ns else []),
        compiler_params=_params(("arbitrary", "arbitrary")), name="wkv_fwd",
    )(r, k, v, wl, al, *params, *shards)
    return outs[0], outs[1], list(outs[2:])


def _wkv_bwd(r, k, v, wl, al, params, states, dy, chip_sums=()):
    t = r.shape[0]
    nh, n = N_HEADS, HEAD_DIM
    nc = t // WKV_CHUNK
    hb = WKV_HEADS_PER_STEP
    grid = (nh // hb, nc)
    ns = len(chip_sums)
    n_in = 5 + N_WKV_PARAMS
    n_out = 5 + N_WKV_PARAMS

    def body(*refs):
        s_ref, dy_ref = refs[n_in:n_in + 2]
        out_refs = refs[n_in + 2 + ns:n_in + 2 + ns + n_out]
        dstate = refs[n_in + 2 + 2 * ns + n_out]
        if ns:
            start, finish = _chip_exchange_plan(refs[n_in + 2:n_in + 2 + ns], refs[n_in + 2 + ns + n_out:n_in + 2 + 2 * ns + n_out],
                                                *refs[n_in + 3 + 2 * ns + n_out:])
            first, last = _first_and_last_step(grid)
            pl.when(first)(start)

        @pl.when(pl.program_id(1) == 0)
        def _():
            dstate[...] = jnp.zeros_like(dstate)
            for ref in out_refs[5:]:
                ref[...] = jnp.zeros_like(ref)

        rows = [_split_heads(ref[...], hb) for ref in refs[:5]]
        _, pull = jax.vjp(_tmix_chunk, s_ref[:, 0], *rows, *[ref[...] for ref in refs[5:n_in]])
        grads = pull((_split_heads(dy_ref[...], hb), dstate[...]))
        dstate[...] = grads[0]
        for ref, val in zip(out_refs[:5], grads[1:6]):
            ref[...] = _merge_heads(val).astype(ref.dtype)
        for ref, val in zip(out_refs[5:], grads[6:]):
            ref[...] += val
        if ns:
            pl.when(last)(finish)

    blk = pl.BlockSpec((WKV_CHUNK, hb * n), lambda h, c: (nc - 1 - c, h))
    pblk = pl.BlockSpec((hb, 1, n), lambda h, c: (h, 0, 0))
    sblk = pl.BlockSpec((hb, 1, n, n), lambda h, c: (h, nc - 1 - c, 0, 0))
    outs = pl.pallas_call(
        body, grid=grid, in_specs=[blk] * 5 + [pblk] * N_WKV_PARAMS + [sblk, blk] + [_ANY] * ns,
        out_specs=[blk] * 5 + [pblk] * N_WKV_PARAMS + [_ANY] * ns,
        out_shape=[_sds((t, nh * n), BF16)] * 5 + [_sds((nh, 1, n))] * N_WKV_PARAMS + [_sds(p.shape, p.dtype) for p in chip_sums],
        scratch_shapes=[pltpu.VMEM((hb, n, n), F32)] + (_chip_exchange_semaphores(ns) if ns else []),
        compiler_params=_params(("arbitrary", "arbitrary")), name="wkv_bwd",
    )(r, k, v, wl, al, *params, states, dy, *chip_sums)
    return outs[:5], outs[5:n_out], list(outs[n_out:])


def _wkv_prep(k, wl, al, w0, a0, k_k, k_a):
    z = -(w0 + wl)
    softplus = jnp.maximum(z, 0.0) + jnp.log1p(jnp.exp(-jnp.abs(z)))
    lw = -jnp.exp(-softplus - 0.5)
    asig = jax.nn.sigmoid(a0 + al)
    kk = k * k_k
    kk = kk / jnp.maximum(jnp.sqrt(jnp.sum(kk * kk, axis=-1, keepdims=True)), L2_EPS)
    kmod = k * (1.0 + (asig - 1.0) * k_a)
    return lw, kmod, -kk, kk * asig


def _wkv_post(y, r, kmod, v, ln_w, ln_b, r_k):
    mu = jnp.mean(y, axis=-1, keepdims=True)
    var = jnp.mean(jnp.square(y - mu), axis=-1, keepdims=True)
    yn = (y - mu) * lax.rsqrt(var + GN_EPS)
    yn = yn * ln_w + ln_b
    return (yn + jnp.sum(r * kmod * r_k, axis=-1, keepdims=True) * v,)


def _attn_group(nonzero_block, q, kc, kp, vc, vp, cos_c, sin_c, cos_p, sin_p, q_gain, k_gain, sinks):
    ri = lax.broadcasted_iota(jnp.int32, (HEAD_DIM, HEAD_DIM), 0)
    ci = lax.broadcasted_iota(jnp.int32, (HEAD_DIM, HEAD_DIM), 1)
    half = HEAD_DIM // 2
    rot = jnp.where(ri == ci + half, -1.0, 0.0) + jnp.where(ri + half == ci, 1.0, 0.0)
    rows = Q_PER_KV * ATT_BLOCK

    def rope(x, cos, sin):
        return x * cos + _hdot(x, rot, _NN) * sin

    kcr = rope(_rms(kc, k_gain), cos_c, sin_c)
    kpr = rope(_rms(kp, k_gain), cos_p, sin_p)
    qn = _rms(q, q_gain)
    qr = qn * cos_c + _hdot(qn.reshape(rows, HEAD_DIM), rot, _NN).reshape(q.shape) * sin_c
    q2 = qr.reshape(rows, HEAD_DIM)
    qi = lax.broadcasted_iota(jnp.int32, (1, ATT_BLOCK, ATT_BLOCK), 1)
    ki = lax.broadcasted_iota(jnp.int32, (1, ATT_BLOCK, ATT_BLOCK), 2)
    mask_c = ki <= qi
    mask_p = jnp.logical_and(ki > qi, nonzero_block)
    lane0 = (lax.broadcasted_iota(jnp.int32, (1, 1, 128), 2) == 0).astype(F32)
    shape3 = (Q_PER_KV, ATT_BLOCK, ATT_BLOCK)
    sc = jnp.where(mask_c, (_bdot_nt(q2, kcr) * (HEAD_DIM ** -0.5)).reshape(shape3), MASK_VALUE)
    sp = jnp.where(mask_p, (_bdot_nt(q2, kpr) * (HEAD_DIM ** -0.5)).reshape(shape3), MASK_VALUE)
    sk = jnp.sum(sinks * lane0, axis=2, keepdims=True)
    mx = jnp.maximum(jnp.maximum(jnp.max(sc, axis=2, keepdims=True), jnp.max(sp, axis=2, keepdims=True)), sk)
    mx = lax.stop_gradient(mx)
    ec, ep = jnp.exp(sc - mx), jnp.exp(sp - mx)
    den = jnp.sum(ec, axis=2, keepdims=True) + jnp.sum(ep, axis=2, keepdims=True) + jnp.exp(sk - mx)
    out = _bdot((ec / den).reshape(rows, ATT_BLOCK), vc) + _bdot((ep / den).reshape(rows, ATT_BLOCK), vp)
    return out.reshape(q.shape)


def _attn_rows(nonzero_block, qs, kcs, kps, vcs, vps, tabs, q_gain, k_gain, sinks):
    return [_attn_group(nonzero_block, qs[g], kcs[g], kps[g], vcs[g], vps[g], *tabs, q_gain, k_gain, sinks[g])
            for g in range(N_KV_HEADS)]


def _attn_operands(q_tile, kvc_tile, kvp_tile, sink_ref):
    q3 = _split_heads(q_tile, N_HEADS)
    kvc, kvp = _split_heads(kvc_tile, 2 * N_KV_HEADS), _split_heads(kvp_tile, 2 * N_KV_HEADS)
    groups = range(N_KV_HEADS)
    qs = [q3[Q_PER_KV * g:Q_PER_KV * (g + 1)] for g in groups]
    sinks = [sink_ref[Q_PER_KV * g:Q_PER_KV * (g + 1)] for g in groups]
    return (qs, [kvc[g] for g in groups], [kvp[g] for g in groups], [kvc[N_KV_HEADS + g] for g in groups],
            [kvp[N_KV_HEADS + g] for g in groups], sinks)


def _attn_specs(t):
    nb = t // ATT_BLOCK
    prev = lambda n: jnp.maximum(n - 1, 0)
    kv_width = 2 * N_KV_HEADS * HEAD_DIM
    q_spec = pl.BlockSpec((ATT_BLOCK, D_MODEL), lambda n: (n, 0))
    kv_c = pl.BlockSpec((ATT_BLOCK, kv_width), lambda n: (n, 0))
    kv_p = pl.BlockSpec((ATT_BLOCK, kv_width), lambda n: (prev(n), 0))
    tab_c = pl.BlockSpec((ATT_BLOCK, HEAD_DIM), lambda n: (n, 0))
    tab_p = pl.BlockSpec((ATT_BLOCK, HEAD_DIM), lambda n: (prev(n), 0))
    gain = pl.BlockSpec((1, HEAD_DIM), lambda n: (0, 0))
    sink = pl.BlockSpec((N_HEADS, 1, 128), lambda n: (0, 0, 0))
    return nb, q_spec, kv_c, kv_p, tab_c, tab_p, gain, sink


def _attn_fwd(q, kv, cos, sin, q_gain, k_gain, sinks):
    t = q.shape[0]
    nb, q_spec, kv_c, kv_p, tab_c, tab_p, gain, sink = _attn_specs(t)

    def body(q_ref, kvc, kvp, cc, sc, cp, sp, qg, kg, sk, o_ref):
        qs, kcs, kps, vcs, vps, sinks_ = _attn_operands(q_ref[...], kvc[...], kvp[...], sk)
        outs = _attn_rows(pl.program_id(0) > 0, qs, kcs, kps, vcs, vps, (cc[...], sc[...], cp[...], sp[...]),
                          qg[...], kg[...], sinks_)
        o_ref[...] = _merge_heads(jnp.concatenate(outs, axis=0)).astype(o_ref.dtype)

    return pl.pallas_call(
        body, grid=(nb,), in_specs=[q_spec, kv_c, kv_p, tab_c, tab_c, tab_p, tab_p, gain, gain, sink],
        out_specs=q_spec, out_shape=_sds(q.shape, BF16), compiler_params=_params(("arbitrary",)), name="attn_fwd",
    )(q, kv, kv, cos, sin, cos, sin, q_gain, k_gain, sinks)


def _attn_bwd(q, kv, cos, sin, q_gain, k_gain, sinks, do):
    t = q.shape[0]
    nb, q_spec, kv_c, kv_p, tab_c, tab_p, gain, sink = _attn_specs(t)

    def body(q_ref, kvc, kvp, cc, sc, cp, sp, qg, kg, sk, do_ref, dq_ref, dkvc_ref, dkvp_ref, dqg_ref, dkg_ref, dsk_ref):
        nonzero = pl.program_id(0) > 0
        tabs = (cc[...], sc[...], cp[...], sp[...])
        qs, kcs, kps, vcs, vps, sinks_ = _attn_operands(q_ref[...], kvc[...], kvp[...], sk)

        def f(qs_, kcs_, kps_, vcs_, vps_, qgv, kgv, sks):
            return _attn_rows(nonzero, qs_, kcs_, kps_, vcs_, vps_, tabs, qgv, kgv, sks)

        _, pull = jax.vjp(f, qs, kcs, kps, vcs, vps, qg[...], kg[...], sinks_)
        do3 = _split_heads(do_ref[...], N_HEADS)
        dqs, dkcs, dkps, dvcs, dvps, dqg, dkg, dsks = pull([do3[Q_PER_KV * g:Q_PER_KV * (g + 1)] for g in range(N_KV_HEADS)])
        dq_ref[...] = _merge_heads(jnp.concatenate(dqs, axis=0)).astype(dq_ref.dtype)
        dkvc_ref[...] = jnp.concatenate(dkcs + dvcs, axis=1)
        dkvp_ref[...] = jnp.concatenate(dkps + dvps, axis=1)

        @pl.when(pl.program_id(0) == 0)
        def _():
            dqg_ref[...] = jnp.zeros_like(dqg_ref)
            dkg_ref[...] = jnp.zeros_like(dkg_ref)
            dsk_ref[...] = jnp.zeros_like(dsk_ref)

        dqg_ref[...] += dqg
        dkg_ref[...] += dkg
        for g in range(N_KV_HEADS):
            dsk_ref[Q_PER_KV * g:Q_PER_KV * (g + 1)] += dsks[g]

    return pl.pallas_call(
        body, grid=(nb,), in_specs=[q_spec, kv_c, kv_p, tab_c, tab_c, tab_p, tab_p, gain, gain, sink, q_spec],
        out_specs=[q_spec, kv_c, kv_c, gain, gain, sink],
        out_shape=[_sds(q.shape, BF16), _sds(kv.shape), _sds(kv.shape), _sds((1, HEAD_DIM)), _sds((1, HEAD_DIM)), _sds(sinks.shape)],
        compiler_params=_params(("arbitrary",)), name="attn_bwd",
    )(q, kv, kv, cos, sin, cos, sin, q_gain, k_gain, sinks, do)


def _time_shift_lerps(x, xs, gain, *mix):
    xn, xsn = _rms(x, gain), _rms(xs, gain)
    xx = xsn - xn
    return tuple(xn + xx * m for m in mix)


def _residual_norm(h, delta, gain):
    hn = h + delta
    return hn, _rms(hn, gain)


def _residual_norm2(h, delta, gain_a, gain_b):
    hn = h + delta
    return hn, _rms(hn, gain_a), _rms(hn, gain_b)


def _relu2(u):
    return jnp.square(jnp.maximum(u, 0.0))


def _sigmoid(z):
    return jax.nn.sigmoid(z)


def _shift_down(x):
    return jnp.pad(x[:-1], ((1, 0), (0, 0)))


def _shift_up(x):
    return jnp.pad(x[1:], ((0, 1), (0, 0)))


def _rope_tables(t):
    half = HEAD_DIM // 2
    inv_freq = jnp.power(ROPE_THETA, -jnp.arange(half, dtype=F32) / half)
    ang = jnp.arange(t, dtype=jnp.int32).astype(F32)[:, None] * inv_freq[None, :]
    cos, sin = jnp.cos(ang), jnp.sin(ang)
    return jnp.concatenate([cos, cos], axis=1), jnp.concatenate([sin, sin], axis=1)


def _mlp_fwd(hn, w_up, w_down, layer, up_dev_major):
    t = hn.shape[0]
    if up_dev_major:
        cw = w_up.shape[2]
        u = _mm(hn, w_up, name=f"mlp{layer}_up", dims=(t, D_FF, D_MODEL), tn=cw, tk=D_MODEL,
                b_spec=pl.BlockSpec((None, D_MODEL, cw), lambda i, j, q: (j, q, 0)))
    else:
        u = _mm(hn, w_up, name=f"mlp{layer}_up")
    out = _mm(u, w_down, a_pro=_relu2, name=f"mlp{layer}_down")
    return u, out


def _mm_pair_reduce(a, b, *, device_axis, tile, a_pro=None, name):
    t = a.shape[0]
    tr, tc = tile
    other = (b.shape[1] // tc) if device_axis == "m" else (a.shape[1] // tr)
    n_tiles = N_CHIPS * other
    core = lax.axis_index("c").astype(jnp.int32).reshape(1)
    shape = (N_CHIPS, tr, other * tc) if device_axis == "m" else (N_CHIPS, other * tr, tc)

    def body(core_ref, a_ref, b_ref, out_ref, recv_hbm, send_buf, recv_tile, send_sems, recv_sems, local_sem):
        phase, q, o = pl.program_id(0), pl.program_id(1), pl.program_id(2)
        sibling = (lax.axis_index("x"), lax.axis_index("y"), 1 - core_ref[0])
        tile_no = q * other + o
        slot = lax.rem(tile_no, 2)

        def send(number, buffer):
            return pltpu.make_async_remote_copy(
                src_ref=send_buf.at[buffer], dst_ref=recv_hbm.at[number // other, lax.rem(number, other)],
                send_sem=send_sems.at[number], recv_sem=recv_sems.at[number], device_id=sibling, device_id_type=_MESH_ID)

        def landed():
            return pltpu.make_async_copy(recv_hbm.at[q, o], recv_tile, local_sem)

        @pl.when(phase == 1)
        def _():
            @pl.when(tile_no == 0)
            def _():
                send(n_tiles - 2, lax.rem(n_tiles - 2, 2)).wait_send()
                send(n_tiles - 1, lax.rem(n_tiles - 1, 2)).wait_send()

            send(tile_no, slot).wait_recv()
            landed().start()

        av = a_ref[...]
        if a_pro is not None:
            av = a_pro(av.astype(F32))
        acc = lax.dot_general(av.astype(BF16), b_ref[...].astype(BF16), _TN, preferred_element_type=F32)

        @pl.when(phase == 0)
        def _():
            @pl.when(tile_no >= 2)
            def _():
                send(tile_no - 2, slot).wait_send()

            send_buf[slot] = acc.astype(send_buf.dtype)
            send(tile_no, slot).start()

        @pl.when(phase == 1)
        def _():
            landed().wait()
            out_ref[...] = (acc + recv_tile[...].astype(F32)).astype(out_ref.dtype)

    def device(phase, q, core_ref):
        return 2 * q + jnp.where(phase == 0, 1 - core_ref[0], core_ref[0])

    parked = lambda p, i: jnp.where(p == 0, 0, i)
    if device_axis == "m":
        a_spec = pl.BlockSpec((t, tr), lambda p, q, o, c: (0, device(p, q, c)))
        b_spec = pl.BlockSpec((t, tc), lambda p, q, o, c: (0, o))
        o_spec = pl.BlockSpec((None, tr, tc), lambda p, q, o, c: (parked(p, q), 0, parked(p, o)))
    else:
        a_spec = pl.BlockSpec((t, tr), lambda p, q, o, c: (0, o))
        b_spec = pl.BlockSpec((t, tc), lambda p, q, o, c: (0, device(p, q, c)))
        o_spec = pl.BlockSpec((None, tr, tc), lambda p, q, o, c: (parked(p, q), parked(p, o), 0))
    grid_spec = pltpu.PrefetchScalarGridSpec(
        num_scalar_prefetch=1, grid=(2, N_CHIPS, other), in_specs=[a_spec, b_spec], out_specs=[o_spec, _ANY],
        scratch_shapes=[pltpu.VMEM((2, tr, tc), BF16), pltpu.VMEM((tr, tc), BF16),
                        pltpu.SemaphoreType.DMA((n_tiles,)), pltpu.SemaphoreType.DMA((n_tiles,)), pltpu.SemaphoreType.DMA(())])
    sums, _ = pl.pallas_call(
        body, grid_spec=grid_spec, out_shape=[_sds(shape, BF16), _sds((N_CHIPS, other, tr, tc), BF16)],
        compiler_params=_params(("arbitrary", "arbitrary", "arbitrary")), name=name,
    )(core, a, b)
    return sums


MLP_PAIR_TILE = (1024, 1024)
SQUARE_PAIR_TILE = (D_MODEL // N_DEV, D_MODEL)


def _mlp_bwd(hn, u, dh, w_up, w_down, layer, up_dev_major, pair_reduce=False):
    t = hn.shape[0]
    du = _mm(dh, w_down, tb=True, epi=lambda r, uu: r * (2.0 * jnp.maximum(uu, 0.0)), epi_args=(u,), out_dtype=BF16,
             name=f"mlp{layer}_du")
    if pair_reduce:
        cw = w_up.shape[2]
        d_down = _mm_pair_reduce(u, dh, device_axis="m", tile=MLP_PAIR_TILE, a_pro=_relu2, name=f"mlp{layer}_ddown")
        d_up = _mm_pair_reduce(hn, du, device_axis="n", tile=MLP_PAIR_TILE, name=f"mlp{layer}_dup")
        dhn = _mm(du, w_up, tb=True, name=f"mlp{layer}_dhn", dims=(t, D_MODEL, D_FF), tk=cw,
                  b_spec=pl.BlockSpec((None, MM_TILE_N, cw), lambda i, j, q: (q, j, 0)))
        return dhn, d_up, d_down
    d_down = _mm(u, dh, ta=True, a_pro=_relu2, name=f"mlp{layer}_ddown")
    if up_dev_major:
        cw = w_up.shape[2]
        d_up = _mm(hn, du, ta=True, name=f"mlp{layer}_dup", dims=(D_MODEL, D_FF, t), tn=cw,
                   o_spec=pl.BlockSpec((None, MM_TILE_M, cw), lambda i, j, q: (j, i, 0)), o_shape=(N_DEV, D_MODEL, cw))
        dhn = _mm(du, w_up, tb=True, name=f"mlp{layer}_dhn", dims=(t, D_MODEL, D_FF), tk=cw,
                  b_spec=pl.BlockSpec((None, MM_TILE_N, cw), lambda i, j, q: (q, j, 0)))
    else:
        d_up = _mm(hn, du, ta=True, name=f"mlp{layer}_dup")
        dhn = _mm(du, w_up, tb=True, name=f"mlp{layer}_dhn")
    return dhn, d_up, d_down


_LATE_WEIGHTS = ("mlp_w_up0", "mlp_w_up1", "mlp_w_down0", "mlp_w_down1", "b_w_q", "b_w_o")
_LATE_GATHER = _LATE_WEIGHTS + ("a_w_out", "a_g1", "w_kv")
_EARLY_GRADS = _LATE_WEIGHTS + ("a_w_out",)
_RKV_GRADS = ("a_w_r", "a_w_k", "a_w_v")


def _local_step(x, target, w, late_shards=None, up_dev_major=True):
    t = x.shape[0]
    g = {}
    w = dict(w)
    row = lambda: _sds((t, D_MODEL))
    rowb = lambda: _sds((t, D_MODEL), BF16)
    vec = lambda: _sds((1, D_MODEL))

    xs = _shift_down(x)
    mix = [w["a_mix"][i:i + 1] for i in range(6)]
    xr, xk, xv, xw, xa, xg = _rowwise(_time_shift_lerps, [x, xs], [w["a_norm"]] + mix, [rowb()] * 6, tm=256, name="tmix_lerp")
    r = _mm(xr, w["a_w_r"], name="tmix_r")
    k = _mm(xk, w["a_w_k"], name="tmix_k")
    v = _mm(xv, w["a_w_v"], name="tmix_v")
    lw1 = _mm(xw, w["a_w1"], name="tmix_w1")
    wl = _mm(lw1, w["a_w2"], a_pro=jnp.tanh, name="tmix_w2")
    la1 = _mm(xa, w["a_a1"], name="tmix_a1")
    al = _mm(la1, w["a_a2"], name="tmix_a2")

    hv = lambda name: w[name].reshape(N_HEADS, 1, HEAD_DIM)
    wkv_params = [hv(name) for name in ("a_w0", "a_a0", "a_k_k", "a_k_a", "a_ln_x_w", "a_ln_x_b", "a_r_k")]
    y2, states, gathered = _wkv_fwd(r, k, v, wl, al, wkv_params, late_shards or ())
    for name, arr in zip(_LATE_GATHER, gathered):
        w[name] = arr if name.startswith("mlp_w_up") else arr.reshape(N_DEV * arr.shape[1], arr.shape[2])
    lg1 = _mm(xg, w["a_g1"], name="tmix_g1")
    gate = _mm(lg1, w["a_g2"], a_pro=_sigmoid, name="tmix_g2")
    (yg,) = _rowwise(lambda a, b: (a * b,), [y2, gate], [], [rowb()], name="tmix_gate")
    att = _mm(yg, w["a_w_out"], name="tmix_out")

    h1, hn0 = _rowwise(_residual_norm, [x, att], [w["mlp_norm0"]], [row(), rowb()], name="res_norm0")
    u0, m0 = _mlp_fwd(hn0, w["mlp_w_up0"], w["mlp_w_down0"], 0, up_dev_major)

    h2, kvn, qn = _rowwise(_residual_norm2, [h1, m0], [w["kv_norm"], w["b_norm"]], [row(), rowb(), rowb()], name="res_norm_kvq")
    kv = _mm(kvn, w["w_kv"], name="kv_proj")
    q = _mm(qn, w["b_w_q"], name="q_proj")
    cos, sin = _rope_tables(t)
    sinks = jnp.broadcast_to(w["b_sinks"].reshape(N_HEADS, 1, 1), (N_HEADS, 1, 128))
    o = _attn_fwd(q, kv, cos, sin, w["b_q_norm"], w["k_norm"], sinks)
    att2 = _mm(o, w["b_w_o"], name="attn_out")

    h3, hn1 = _rowwise(_residual_norm, [h2, att2], [w["mlp_norm1"]], [row(), rowb()], name="res_norm1")
    u1, m1 = _mlp_fwd(hn1, w["mlp_w_up1"], w["mlp_w_down1"], 1, up_dev_major)

    def loss_fn(h, m, tg):
        diff = (h + m) - tg
        part = 0.5 * jnp.sum(jnp.mean(jnp.square(diff), axis=-1, keepdims=True), axis=0, keepdims=True)
        dh = diff * (1.0 / D_MODEL)
        return dh, dh, jnp.broadcast_to(part, (1, 128))

    dh4, dh4b, loss = _rowwise(loss_fn, [h3, m1, target], [], [row(), rowb()], [_sds((1, 128))], name="loss")

    def res_norm_bwd(h, dnext, dhn, gain):
        dh, dgain = _vjp_of(lambda hh, gg: (_rms(hh, gg),), 2, (0, 1))(h, gain, dhn)
        return dnext + dh, dnext + dh, dgain

    distributed = late_shards is not None

    def square_dw(a_, b_, name):
        if distributed:
            return _mm_pair_reduce(a_, b_, device_axis="m", tile=SQUARE_PAIR_TILE, name=name)
        return _mm(a_, b_, ta=True, name=name)

    dhn1, g["mlp_w_up1"], g["mlp_w_down1"] = _mlp_bwd(hn1, u1, dh4b, w["mlp_w_up1"], w["mlp_w_down1"], 1, up_dev_major, distributed)
    dh3, dh3b, g["mlp_norm1"] = _rowwise(res_norm_bwd, [h3, dh4, dhn1], [w["mlp_norm1"]], [row(), rowb()], [vec()], name="res_norm1_bwd")

    g["b_w_o"] = square_dw(o, dh3b, "attn_out_dw")
    do = _mm(dh3b, w["b_w_o"], tb=True, name="attn_out_dx")
    dq, dkv_own, dkv_prev, g["b_q_norm"], g["k_norm"], dsinks = _attn_bwd(
        q, kv, cos, sin, w["b_q_norm"], w["k_norm"], sinks, do)
    g["b_sinks"] = dsinks[:, 0, 0].reshape(1, N_HEADS)
    g["b_w_q"] = square_dw(qn, dq, "q_proj_dw")
    dqn = _mm(dq, w["b_w_q"], tb=True, name="q_proj_dx")
    dkv_prev = jnp.pad(dkv_prev[ATT_BLOCK:], ((0, ATT_BLOCK), (0, 0)))
    (dkv,) = _rowwise(lambda a, b: (a + b,), [dkv_own, dkv_prev], [], [_sds(kv.shape, BF16)], name="kv_grad_sum")
    g["w_kv"] = _mm(kvn, dkv, ta=True, name="kv_proj_dw")
    dkvn = _mm(dkv, w["w_kv"], tb=True, name="kv_proj_dx")

    def res_norm2_bwd(h, dnext, dna, dnb, gain_a, gain_b):
        dha, dga = _vjp_of(lambda hh, gg: (_rms(hh, gg),), 2, (0, 1))(h, gain_a, dna)
        dhb, dgb = _vjp_of(lambda hh, gg: (_rms(hh, gg),), 2, (0, 1))(h, gain_b, dnb)
        return dnext + dha + dhb, dnext + dha + dhb, dga, dgb

    dh2, dh2b, g["kv_norm"], g["b_norm"] = _rowwise(res_norm2_bwd, [h2, dh3, dkvn, dqn], [w["kv_norm"], w["b_norm"]],
                                                    [row(), rowb()], [vec(), vec()], name="res_norm_kvq_bwd")

    dhn0, g["mlp_w_up0"], g["mlp_w_down0"] = _mlp_bwd(hn0, u0, dh2b, w["mlp_w_up0"], w["mlp_w_down0"], 0, up_dev_major, distributed)
    dh1, dh1b, g["mlp_norm0"] = _rowwise(res_norm_bwd, [h1, dh2, dhn0], [w["mlp_norm0"]], [row(), rowb()], [vec()], name="res_norm0_bwd")

    g["a_w_out"] = square_dw(yg, dh1b, "tmix_out_dw")
    dyg = _mm(dh1b, w["a_w_out"], tb=True, name="tmix_out_dx")
    dy2, dgate = _rowwise(lambda d, a, b: (d * b, d * a), [dyg, y2, gate], [], [row(), rowb()], name="tmix_gate_bwd")
    g["a_g2"] = _mm(lg1, dgate, ta=True, a_pro=_sigmoid, name="tmix_g2_dw")

    def dsigmoid(rr, z):
        s = jax.nn.sigmoid(z)
        return rr * s * (1.0 - s)

    dlg1 = _mm(dgate, w["a_g2"], tb=True, epi=dsigmoid, epi_args=(lg1,), out_dtype=BF16, name="tmix_g2_dx")
    g["a_g1"] = _mm(xg, dlg1, ta=True, name="tmix_g1_dw")
    dxg = _mm(dlg1, w["a_g1"], tb=True, name="tmix_g1_dx")

    early_sums = [g[name] for name in _EARLY_GRADS] if distributed else ()
    (dr, dk, dv, dwl, dal), param_grads, early_reduced = _wkv_bwd(r, k, v, wl, al, wkv_params, states, dy2, early_sums)
    for name, pg in zip(("a_w0", "a_a0", "a_k_k", "a_k_a", "a_ln_x_w", "a_ln_x_b", "a_r_k"), param_grads):
        g[name] = pg.reshape(1, D_MODEL)

    g["a_w_r"] = square_dw(xr, dr, "tmix_r_dw")
    g["a_w_k"] = square_dw(xk, dk, "tmix_k_dw")
    g["a_w_v"] = square_dw(xv, dv, "tmix_v_dw")
    dxr = _mm(dr, w["a_w_r"], tb=True, name="tmix_r_dx")
    dxk = _mm(dk, w["a_w_k"], tb=True, name="tmix_k_dx")
    dxv = _mm(dv, w["a_w_v"], tb=True, name="tmix_v_dx")
    g["a_w2"] = _mm(lw1, dwl, ta=True, a_pro=jnp.tanh, name="tmix_w2_dw")

    def dtanh(rr, z):
        th = jnp.tanh(z)
        return rr * (1.0 - th * th)

    dlw1 = _mm(dwl, w["a_w2"], tb=True, epi=dtanh, epi_args=(lw1,), out_dtype=BF16, name="tmix_w2_dx")
    g["a_w1"] = _mm(xw, dlw1, ta=True, name="tmix_w1_dw")
    dxw = _mm(dlw1, w["a_w1"], tb=True, name="tmix_w1_dx")
    g["a_a2"] = _mm(la1, dal, ta=True, name="tmix_a2_dw")
    dla1 = _mm(dal, w["a_a2"], tb=True, out_dtype=BF16, name="tmix_a2_dx")
    g["a_a1"] = _mm(xa, dla1, ta=True, name="tmix_a1_dw")
    dxa = _mm(dla1, w["a_a1"], tb=True, name="tmix_a1_dx")

    lerp_bwd = _vjp_of(_time_shift_lerps, 9, tuple(range(9)))

    def lerp_bwd_rows(x_, xs_, d0, d1, d2, d3, d4, d5, gain, *mx):
        return lerp_bwd(x_, xs_, gain, *mx, d0, d1, d2, d3, d4, d5)

    rkv_sums = [g[name] for name in _RKV_GRADS] if distributed else []
    outs = _rowwise(lerp_bwd_rows, [x, xs, dxr, dxk, dxv, dxw, dxa, dxg], [w["a_norm"]] + mix, [row(), row()], [vec()] * 7,
                    tm=128, name="tmix_lerp_bwd", chip_sums=rkv_sums)
    dx_a, dxs, g["a_norm"] = outs[0], outs[1], outs[2]
    g["a_mix"] = jnp.concatenate(outs[3:9], axis=0)
    (grad_x,) = _rowwise(lambda a, b, c: (a + b + c,), [dh1, dx_a, _shift_up(dxs)], [], [row()], name="grad_x_sum")
    reduced = dict(zip(_EARLY_GRADS + _RKV_GRADS, list(early_reduced) + list(outs[9:]))) if distributed else None
    return loss, grad_x, g, reduced


_ANY = pl.BlockSpec(memory_space=pl.ANY)
_MESH_ID = pl.DeviceIdType.MESH


def _linear(pos):
    return 4 * pos[0] + 2 * pos[1] + pos[2]


def _all_gather(shards, name):
    n = len(shards)

    def body(*refs):
        start, relay, finish = _gather_plan(refs[:n], refs[n:2 * n], *refs[2 * n:])
        start()
        relay()
        finish()

    return pl.pallas_call(
        body, out_shape=[_sds((N_DEV,) + s.shape, s.dtype) for s in shards], in_specs=[_ANY] * n, out_specs=[_ANY] * n,
        scratch_shapes=_gather_semaphores(n), name=name,
    )(*shards)


GATHER_COPIES = 8


def _gather_semaphores(n):
    return [pltpu.SemaphoreType.DMA((n * GATHER_COPIES,)), pltpu.SemaphoreType.DMA((n * GATHER_COPIES,)),
            pltpu.SemaphoreType.DMA((n,))]


def _gather_plan(ins, outs, send_sems, recv_sems, local_sems):
    n = len(ins)
    x, y, c = lax.axis_index("x"), lax.axis_index("y"), lax.axis_index("c")
    me, sibling = (x, y, c), (x, y, 1 - c)
    x_nbr, y_nbr, diag = (1 - x, y, c), (x, 1 - y, c), (1 - x, 1 - y, c)
    other = lambda pos: (pos[0], pos[1], 1 - c)

    def halves(a):
        rows = ins[a].shape[0]
        if rows % (2 * BF16_SUBLANES):
            return (0, rows), None
        return (0, rows // 2), (rows // 2, rows // 2)

    def copy(a, k, block, to, src=None, rows=None):
        dst = outs[a].at[_linear(block)]
        if rows is not None:
            dst = dst.at[pl.ds(rows[0], rows[1])]
        return pltpu.make_async_remote_copy(
            src_ref=dst if src is None else src, dst_ref=dst, send_sem=send_sems.at[a * GATHER_COPIES + k],
            recv_sem=recv_sems.at[a * GATHER_COPIES + k], device_id=to, device_id_type=_MESH_ID)

    def own_copies():
        mine = [pltpu.make_async_copy(ins[a], outs[a].at[_linear(me)], local_sems.at[a]) for a in range(n)]
        sent = []
        for a in range(n):
            sent += [copy(a, 0, me, sibling, src=ins[a]), copy(a, 1, me, x_nbr, src=ins[a]), copy(a, 2, me, y_nbr, src=ins[a])]
        return mine, sent

    def relayed_copies():
        sent = []
        for a in range(n):
            first, second = halves(a)
            sent += [copy(a, 3, x_nbr, y_nbr, rows=first), copy(a, 5, x_nbr, sibling), copy(a, 6, y_nbr, sibling)]
            if second is not None:
                sent.append(copy(a, 4, y_nbr, x_nbr, rows=second))
        return sent

    def start():
        mine, sent = own_copies()
        for cp in mine + sent:
            cp.start()

    def relay():
        for a in range(n):
            first, second = halves(a)
            copy(a, 1, x_nbr, me).wait_recv()
            copy(a, 3, x_nbr, y_nbr, rows=first).start()
            copy(a, 5, x_nbr, sibling).start()
        for a in range(n):
            first, second = halves(a)
            copy(a, 2, y_nbr, me).wait_recv()
            if second is not None:
                copy(a, 4, y_nbr, x_nbr, rows=second).start()
            copy(a, 6, y_nbr, sibling).start()

    def finish():
        mine, sent = own_copies()
        sent += relayed_copies()
        for a in range(n):
            first, second = halves(a)
            copy(a, 3, diag, me, rows=first).wait_recv()
            if second is not None:
                copy(a, 4, diag, me, rows=second).wait_recv()
            last = copy(a, 7, diag, sibling)
            last.start()
            sent.append(last)
        for a in range(n):
            copy(a, 0, other(me), me).wait_recv()
            copy(a, 5, other(x_nbr), me).wait_recv()
            copy(a, 6, other(y_nbr), me).wait_recv()
            copy(a, 7, other(diag), me).wait_recv()
        for cp in sent:
            cp.wait_send()
        for cp in mine:
            cp.wait()

    return start, relay, finish


N_CHIPS = 4


def _exchange_with_sibling(parts, name):
    n = len(parts)

    def body(*refs):
        ins, outs = refs[:n], refs[n:2 * n]
        send_sems, recv_sems = refs[2 * n:]
        x, y, c = lax.axis_index("x"), lax.axis_index("y"), lax.axis_index("c")
        copies = [pltpu.make_async_remote_copy(
            src_ref=ins[a].at[2 * q + (1 - c)], dst_ref=outs[a].at[q], send_sem=send_sems.at[a * N_CHIPS + q],
            recv_sem=recv_sems.at[a * N_CHIPS + q], device_id=(x, y, 1 - c), device_id_type=_MESH_ID)
            for a in range(n) for q in range(N_CHIPS)]
        for cp in copies:
            cp.start()
        for cp in copies:
            cp.wait()

    return pl.pallas_call(
        body, out_shape=[_sds((N_CHIPS,) + p.shape[1:], p.dtype) for p in parts], in_specs=[_ANY] * n, out_specs=[_ANY] * n,
        scratch_shapes=[pltpu.SemaphoreType.DMA((n * N_CHIPS,)), pltpu.SemaphoreType.DMA((n * N_CHIPS,))], name=name,
    )(*parts)


def _pair_sum(part, recv, core, out_dtype, name):
    _, r, cdim = recv.shape
    tr = max(8, min(r, STREAM_BLOCK_ELEMENTS // cdim))
    assert r % tr == 0, (name, r, tr)

    def body(core_ref, p_ref, r_ref, o_ref):
        o_ref[...] = (p_ref[...] + r_ref[...]).astype(o_ref.dtype)

    grid_spec = pltpu.PrefetchScalarGridSpec(
        num_scalar_prefetch=1, grid=(N_CHIPS, r // tr),
        in_specs=[pl.BlockSpec((None, None, tr, cdim), lambda q, i, core_ref: (q, core_ref[0], i, 0)),
                  pl.BlockSpec((None, tr, cdim), lambda q, i, core_ref: (q, i, 0))],
        out_specs=pl.BlockSpec((None, tr, cdim), lambda q, i, core_ref: (q, i, 0)))
    return pl.pallas_call(
        body, grid_spec=grid_spec, out_shape=_sds((N_CHIPS, r, cdim), out_dtype),
        compiler_params=_params(("parallel", "parallel")), name=name,
    )(core, part.reshape(N_CHIPS, 2, r, cdim), recv)


def _exchange_between_chips(parts, name):
    n = len(parts)

    def body(*refs):
        start, finish = _chip_exchange_plan(refs[:n], refs[n:2 * n], *refs[2 * n:])
        start()
        finish()

    return pl.pallas_call(
        body, out_shape=[_sds(p.shape, p.dtype) for p in parts], in_specs=[_ANY] * n, out_specs=[_ANY] * n,
        scratch_shapes=_chip_exchange_semaphores(n), name=name,
    )(*parts)


def _chip_exchange_semaphores(n):
    n_other = N_CHIPS - 1
    return [pltpu.SemaphoreType.DMA((n * n_other,)), pltpu.SemaphoreType.DMA((n * n_other,)), pltpu.SemaphoreType.DMA((n,))]


def _chip_exchange_plan(ins, outs, send_sems, recv_sems, local_sems):
    n = len(ins)
    n_other = N_CHIPS - 1
    x, y, c = lax.axis_index("x"), lax.axis_index("y"), lax.axis_index("c")
    my_chip = 2 * x + y

    def all_copies():
        mine = [pltpu.make_async_copy(ins[a].at[my_chip], outs[a].at[my_chip], local_sems.at[a]) for a in range(n)]
        remote = []
        for j, (fx, fy) in enumerate([(1, 0), (0, 1), (1, 1)]):
            px, py = (1 - x if fx else x), (1 - y if fy else y)
            for a in range(n):
                remote.append(pltpu.make_async_remote_copy(
                    src_ref=ins[a].at[2 * px + py], dst_ref=outs[a].at[my_chip], send_sem=send_sems.at[a * n_other + j],
                    recv_sem=recv_sems.at[a * n_other + j], device_id=(px, py, c), device_id_type=_MESH_ID))
        return mine, remote

    def start():
        mine, remote = all_copies()
        for cp in mine + remote:
            cp.start()

    def finish():
        mine, remote = all_copies()
        for cp in remote + mine:
            cp.wait()

    return start, finish


def _chip_sums(parts, names, tag):
    from_sibling = _exchange_with_sibling(parts, name="scatter_grads_sibling_" + tag)
    core = lax.axis_index("c").astype(jnp.int32).reshape(1)
    return [_pair_sum(p, r, core, F32 if nm.startswith("pack") else BF16, name="pair_sum_" + nm)
            for p, r, nm in zip(parts, from_sibling, names)]


def _adamw(w, m, v, slots, name, layer=0, n_layers=1, into=None):
    r, c = w.shape[-2:]
    ns = slots.shape[0]
    tr = max(8, min(r, STREAM_BLOCK_ELEMENTS // c))
    assert r % tr == 0, (name, r, tr)

    def body(w_ref, m_ref, v_ref, g_ref, *rest):
        g_out, d_out, m_out, v_out = rest[-4:]
        g = g_ref[0].astype(F32)
        for s in range(1, ns):
            g = g + g_ref[s].astype(F32)
        m_new = ADAM_B1 * m_ref[...] + (1.0 - ADAM_B1) * g
        v_new = ADAM_B2 * v_ref[...] + (1.0 - ADAM_B2) * jnp.square(g)
        m_hat = m_new / (1.0 - ADAM_B1 ** ADAM_STEP)
        v_hat = v_new / (1.0 - ADAM_B2 ** ADAM_STEP)
        d_out[...] = -ADAM_LR * (m_hat / (jnp.sqrt(v_hat) + ADAM_EPS) + ADAM_WD * w_ref[...])
        g_out[...], m_out[...], v_out[...] = g, m_new, v_new

    if n_layers == 1:
        spec, out_shape, earlier, aliases = pl.BlockSpec((tr, c), lambda i: (i, 0)), _sds((r, c)), [], {}
    else:
        spec, out_shape = pl.BlockSpec((None, tr, c), lambda i: (layer, i, 0)), _sds((n_layers, r, c))
        earlier = list(into) if into is not None else []
        aliases = {4 + i: i for i in range(len(earlier))}
    in_specs = [spec, spec, spec, pl.BlockSpec((ns, tr, c), lambda i: (0, i, 0))]
    return pl.pallas_call(
        body, grid=(r // tr,), in_specs=in_specs + [_ANY] * len(earlier), out_specs=[spec] * 4, out_shape=[out_shape] * 4,
        input_output_aliases=aliases, compiler_params=_params(("parallel",)), name=name,
    )(w, m, v, slots, *earlier)


_COL_VECTORS = ("a_norm", "a_mix", "a_w0", "a_a0", "a_k_k", "a_k_a", "a_ln_x_w", "a_ln_x_b")
_COL_VEC_ROWS = 16
_COL_ROWS = _COL_VEC_ROWS + 2 * LORA_PAD + 256
_ROW_COLS = 2 * LORA_PAD + 256 + 512
_REPL_ROWS = 8


def _pad_to(a, size, axis):
    widths = [(0, 0)] * a.ndim
    widths[axis] = (0, size - a.shape[axis])
    return jnp.pad(a, widths)


def _pack_cols(p):
    width = p["a_norm"].shape[-1]
    vecs = jnp.concatenate([p[n].reshape(-1, width) for n in _COL_VECTORS], axis=0)
    return jnp.concatenate([_pad_to(vecs, _COL_VEC_ROWS, 0), _pad_to(p["a_w2"].reshape(-1, width), LORA_PAD, 0),
                            _pad_to(p["a_a2"].reshape(-1, width), LORA_PAD, 0), p["a_g2"].reshape(-1, width)], axis=0)


def _unpack_cols(a, lead):
    width = a.shape[-1]
    out, row = {}, 0
    for n in _COL_VECTORS:
        k = 6 if n == "a_mix" else 1
        out[n] = a[row:row + k].reshape(lead + ((6, width) if n == "a_mix" else (width,)))
        row += k
    base = _COL_VEC_ROWS
    out["a_w2"] = a[base:base + 96].reshape(lead + (96, width))
    out["a_a2"] = a[base + LORA_PAD:base + LORA_PAD + 96].reshape(lead + (96, width))
    out["a_g2"] = a[base + 2 * LORA_PAD:].reshape(lead + (256, width))
    return out


def _pack_rows(p):
    rows = p["w_kv"].shape[0]
    return jnp.concatenate([_pad_to(p["a_w1"].reshape(rows, -1), LORA_PAD, 1), _pad_to(p["a_a1"].reshape(rows, -1), LORA_PAD, 1),
                            p["a_g1"].reshape(rows, -1), p["w_kv"]], axis=1)


def _unpack_rows(a, lead):
    rows = a.shape[0]
    return {"a_w1": a[:, :96].reshape(lead + (rows, 96)), "a_a1": a[:, LORA_PAD:LORA_PAD + 96].reshape(lead + (rows, 96)),
            "a_g1": a[:, 2 * LORA_PAD:2 * LORA_PAD + 256].reshape(lead + (rows, 256)), "w_kv": a[:, 2 * LORA_PAD + 256:]}


def _pack_repl(p):
    row = lambda a: _pad_to(a.reshape(1, -1), D_MODEL, 1)
    return jnp.concatenate([p["mlp_norm"].reshape(2, D_MODEL), row(p["kv_norm"]), row(p["b_norm"]), row(p["a_r_k"]),
                            row(p["k_norm"]), row(p["b_q_norm"]), row(p["b_sinks"])], axis=0)


def _unpack_repl(a):
    return {"mlp_norm": a[0:2], "kv_norm": a[2], "b_norm": a[3:4], "a_r_k": a[4].reshape(1, N_HEADS, HEAD_DIM),
            "k_norm": a[5, :HEAD_DIM], "b_q_norm": a[6:7, :HEAD_DIM], "b_sinks": a[7:8, :N_HEADS]}


_WEIGHTS = ("a_norm", "a_mix", "a_w_rkv", "a_w0", "a_w1", "a_w2", "a_a0", "a_a1", "a_a2", "a_g1", "a_g2", "a_k_k", "a_k_a",
            "a_r_k", "a_ln_x_w", "a_ln_x_b", "a_w_out", "mlp_norm", "mlp_w_up", "mlp_w_down", "kv_norm", "w_kv", "k_norm",
            "b_norm", "b_w_q", "b_q_norm", "b_sinks", "b_w_o")


def _big_shards(p):
    return [p["a_w_rkv"][0, 0], p["a_w_rkv"][0, 1], p["a_w_rkv"][0, 2], p["a_w_out"][0], p["mlp_w_up"][0], p["mlp_w_up"][1],
            p["mlp_w_down"][0], p["mlp_w_down"][1], p["b_w_q"][0], p["b_w_o"][0]]


_BIG_NAMES = ("a_w_r", "a_w_k", "a_w_v", "a_w_out", "mlp_w_up0", "mlp_w_up1", "mlp_w_down0", "mlp_w_down1", "b_w_q", "b_w_o")


def kernel(x, a_norm, a_mix, a_w_rkv, a_w0, a_w1, a_w2, a_a0, a_a1, a_a2, a_g1, a_g2, a_k_k, a_k_a, a_r_k, a_ln_x_w,
           a_ln_x_b, a_w_out, mlp_norm, mlp_w_up, mlp_w_down, kv_norm, w_kv, k_norm, b_norm, b_w_q, b_q_norm, b_sinks,
           b_w_o, loss_target, m_a_norm, m_a_mix, m_a_w_rkv, m_a_w0, m_a_w1, m_a_w2, m_a_a0, m_a_a1, m_a_a2, m_a_g1,
           m_a_g2, m_a_k_k, m_a_k_a, m_a_r_k, m_a_ln_x_w, m_a_ln_x_b, m_a_w_out, m_mlp_norm, m_mlp_w_up, m_mlp_w_down,
           m_kv_norm, m_w_kv, m_k_norm, m_b_norm, m_b_w_q, m_b_q_norm, m_b_sinks, m_b_w_o, v_a_norm, v_a_mix, v_a_w_rkv,
           v_a_w0, v_a_w1, v_a_w2, v_a_a0, v_a_a1, v_a_a2, v_a_g1, v_a_g2, v_a_k_k, v_a_k_a, v_a_r_k, v_a_ln_x_w,
           v_a_ln_x_b, v_a_w_out, v_mlp_norm, v_mlp_w_up, v_mlp_w_down, v_kv_norm, v_w_kv, v_k_norm, v_b_norm, v_b_w_q,
           v_b_q_norm, v_b_sinks, v_b_w_o):
    given = locals()
    wts = {n: given[n] for n in _WEIGHTS}
    mom = {n: given["m_" + n] for n in _WEIGHTS}
    var = {n: given["v_" + n] for n in _WEIGHTS}

    cols_w, rows_w, repl_w = _pack_cols(wts), _pack_rows(wts), _pack_repl(wts)
    big_w = _big_shards(wts)
    big_bf16 = dict(zip(_BIG_NAMES, [b.astype(BF16) for b in big_w]))
    first_names = [k for k in _BIG_NAMES if k not in _LATE_GATHER]
    lora_in = rows_w[:, :2 * LORA_PAD]
    gathered = _all_gather([cols_w, lora_in] + [big_bf16[k] for k in first_names], name="gather_weights")
    full_cols = gathered[0].transpose(1, 0, 2).reshape(_COL_ROWS, D_MODEL)
    full_lora_in = gathered[1].reshape(D_MODEL, 2 * LORA_PAD)
    w = {}
    w.update({k: v.reshape(v.shape[1:]) for k, v in _unpack_cols(full_cols, (1,)).items()})
    for k in ("a_norm", "a_w0", "a_a0", "a_k_k", "a_k_a", "a_ln_x_w", "a_ln_x_b"):
        w[k] = w[k].reshape(1, D_MODEL)
    for k in ("a_w2", "a_a2"):
        w[k] = _pad_to(w[k], LORA_PAD, 0)
    w["a_w1"], w["a_a1"] = full_lora_in[:, :LORA_PAD], full_lora_in[:, LORA_PAD:]
    for k, arr in zip(first_names, gathered[2:]):
        w[k] = arr.reshape(N_DEV * arr.shape[1], arr.shape[2])
    late_shards = [big_bf16[k] for k in _LATE_WEIGHTS] + [big_bf16["a_w_out"], a_g1[0], w_kv]
    w["mlp_norm0"], w["mlp_norm1"] = mlp_norm[0:1], mlp_norm[1:2]
    w["kv_norm"], w["k_norm"] = kv_norm.reshape(1, D_MODEL), k_norm.reshape(1, HEAD_DIM)
    w["b_norm"], w["b_q_norm"], w["b_sinks"], w["a_r_k"] = b_norm, b_q_norm, b_sinks, a_r_k.reshape(1, D_MODEL)

    loss_local, grad_x, g, big_reduced = _local_step(x[0], loss_target[0], w, late_shards)
    loss = lax.psum(loss_local[0, 0], MESH_AXES)

    g_lead = {k: g[k][None] for k in ("a_norm", "a_mix", "a_w0", "a_a0", "a_k_k", "a_k_a", "a_ln_x_w", "a_ln_x_b", "a_g2")}
    g_lead["a_w2"], g_lead["a_a2"] = g["a_w2"][None, :96], g["a_a2"][None, :96]
    g_cols = _pack_cols(g_lead).reshape(_COL_ROWS, N_DEV, D_MODEL // N_DEV).transpose(1, 0, 2)
    g_rows = _pack_rows({"a_w1": g["a_w1"][:, :96], "a_a1": g["a_a1"][:, :96], "a_g1": g["a_g1"], "w_kv": g["w_kv"]})
    g_rows = g_rows.reshape(N_DEV, D_MODEL // N_DEV, _ROW_COLS)
    pack_sums = _chip_sums([g_cols, g_rows], ("pack_cols", "pack_rows"), "late")
    reduced = list(_exchange_between_chips(pack_sums, name="scatter_grads_chips_late")) + [big_reduced[k] for k in _BIG_NAMES]
    g_repl = _pack_repl({"mlp_norm": jnp.concatenate([g["mlp_norm0"], g["mlp_norm1"]], axis=0), "kv_norm": g["kv_norm"],
                         "b_norm": g["b_norm"], "a_r_k": g["a_r_k"], "k_norm": g["k_norm"], "b_q_norm": g["b_q_norm"],
                         "b_sinks": g["b_sinks"]})
    (repl_slots,) = _all_gather([g_repl], name="gather_replicated_grads")

    res = {}
    cols4 = _adamw(cols_w, _pack_cols(mom), _pack_cols(var), reduced[0], name="adamw_cols")
    rows4 = _adamw(rows_w, _pack_rows(mom), _pack_rows(var), reduced[1], name="adamw_rows")
    repl4 = _adamw(repl_w, _pack_repl(mom), _pack_repl(var), repl_slots, name="adamw_replicated")
    for unpacked in ([_unpack_cols(a, (1,)) for a in cols4], [_unpack_rows(a, (1,)) for a in rows4], [_unpack_repl(a) for a in repl4]):
        for k in unpacked[0]:
            res[k] = tuple(u[k] for u in unpacked)
    stacked = {"a_w_r": ("a_w_rkv", 0), "a_w_k": ("a_w_rkv", 1), "a_w_v": ("a_w_rkv", 2), "mlp_w_up0": ("mlp_w_up", 0),
               "mlp_w_up1": ("mlp_w_up", 1), "mlp_w_down0": ("mlp_w_down", 0), "mlp_w_down1": ("mlp_w_down", 1)}
    big4, earlier = {}, None
    for k, bw, bm, bv, slots in zip(_BIG_NAMES, big_w, _big_shards(mom), _big_shards(var), reduced[2:]):
        if k in stacked:
            src, layer = stacked[k]
            as_layers = lambda a: a.reshape((-1,) + a.shape[-2:])
            earlier = _adamw(as_layers(wts[src]), as_layers(mom[src]), as_layers(var[src]), slots, name="adamw_" + k,
                             layer=layer, n_layers=as_layers(wts[src]).shape[0], into=earlier if layer else None)
        else:
            earlier = _adamw(bw, bm, bv, slots, name="adamw_" + k)
        big4[k] = earlier
    res["a_w_rkv"] = tuple(a[None] for a in big4["a_w_v"])
    res["a_w_out"] = tuple(a[None] for a in big4["a_w_out"])
    res["mlp_w_up"], res["mlp_w_down"] = tuple(big4["mlp_w_up1"]), tuple(big4["mlp_w_down1"])
    res["b_w_q"] = tuple(a[None] for a in big4["b_w_q"])
    res["b_w_o"] = tuple(a[None] for a in big4["b_w_o"])
    res["w_kv"] = tuple(a.reshape(w_kv.shape) for a in res["w_kv"])

    outs = [loss, grad_x[None]]
    for i in range(4):
        outs += [res[n][i].reshape(given[n].shape) for n in _WEIGHTS]
    return tuple(outs)
```

```python
import functools
import math

import jax
import jax.numpy as jnp
from jax import lax
from jax.experimental import pallas as pl
from jax.experimental.pallas import tpu as pltpu

F32 = jnp.float32
BF16 = jnp.bfloat16

D_MODEL = 2048
N_HEADS = 32
HEAD_DIM = 64
N_KV_HEADS = 4
Q_PER_KV = 8
ATT_BLOCK = 128
WKV_CHUNK = 64
LORA_PAD = 128
D_FF = 8192
N_DEV = 8
RMS_EPS = 1e-6
GN_EPS = 64e-5
L2_EPS = 1e-12
ROPE_THETA = 10000.0
ADAM_LR, ADAM_B1, ADAM_B2, ADAM_EPS, ADAM_WD, ADAM_STEP = 0.001, 0.9, 0.999, 1e-08, 0.01, 10
MASK_VALUE = -1e30
VMEM_LIMIT_BYTES = 56 * 1024 * 1024
MM_TILE_M, MM_TILE_N, MM_TILE_K = 1024, 1024, 2048
STREAM_BLOCK_ELEMENTS = 1 << 18
BF16_SUBLANES = 16
MESH_AXES = ("x", "y", "c")

_NN = (((1,), (0,)), ((), ()))
_NT = (((1,), (1,)), ((), ()))
_TN = (((0,), (0,)), ((), ()))
_BNN = (((2,), (1,)), ((0,), (0,)))
_BNT = (((2,), (2,)), ((0,), (0,)))
_BTN = (((1,), (1,)), ((0,), (0,)))


def _params(sem):
    return pltpu.CompilerParams(dimension_semantics=sem, vmem_limit_bytes=VMEM_LIMIT_BYTES)


def _split2(a):
    hi = a.astype(BF16)
    return hi, (a - hi.astype(F32)).astype(BF16)


def _dot3(a, b, dims):
    ah, al = _split2(a)
    bh, bl = _split2(b)
    d = lambda p, q: lax.dot_general(p, q, dims, preferred_element_type=F32)
    return d(ah, bh) + (d(al, bh) + d(ah, bl))


@functools.partial(jax.custom_vjp, nondiff_argnums=(2,))
def _hdot(a, b, dims=_NN):
    return _dot3(a, b, dims)


def _hdot_fwd(a, b, dims):
    return _dot3(a, b, dims), (a, b)


def _hdot_bwd(dims, res, g):
    a, b = res
    nn, nt, tn = (_NN, _NT, _TN) if dims in (_NN, _NT, _TN) else (_BNN, _BNT, _BTN)
    if dims == nn:
        return _dot3(g, b, nt), _dot3(a, g, tn)
    if dims == nt:
        return _dot3(g, b, nn), _dot3(g, a, tn)
    assert dims == tn
    return _dot3(b, g, nt), _dot3(a, g, nn)


_hdot.defvjp(_hdot_fwd, _hdot_bwd)


def _tri_parts(x):
    hi = x.astype(BF16)
    r1 = x - hi.astype(F32)
    mid = r1.astype(BF16)
    return hi, mid, (r1 - mid.astype(F32)).astype(BF16)


@jax.custom_vjp
def _mask_dot(mask, x):
    mb = mask.astype(BF16)
    p0, p1, p2 = _tri_parts(x)
    d = lambda p: lax.dot_general(mb, p, _BNN, preferred_element_type=F32)
    return d(p0) + (d(p1) + d(p2))


def _mask_dot_fwd(mask, x):
    return _mask_dot(mask, x), mask


def _mask_dot_bwd(mask, g):
    mb = mask.astype(BF16)
    p0, p1, p2 = _tri_parts(g)
    d = lambda p: lax.dot_general(mb, p, _BTN, preferred_element_type=F32)
    return jnp.zeros_like(mask), d(p0) + (d(p1) + d(p2))


_mask_dot.defvjp(_mask_dot_fwd, _mask_dot_bwd)


def _b16dot(a, b, dims):
    return lax.dot_general(a.astype(BF16), b.astype(BF16), dims, preferred_element_type=F32)


@jax.custom_vjp
def _bdot(a, b):
    return _b16dot(a, b, _NN)


def _bdot_fwd(a, b):
    return _b16dot(a, b, _NN), (a, b)


def _bdot_bwd(res, g):
    a, b = res
    return _b16dot(g, b, _NT), _b16dot(a, g, _TN)


_bdot.defvjp(_bdot_fwd, _bdot_bwd)


@jax.custom_vjp
def _bdot_nt(a, b):
    return _b16dot(a, b, _NT)


def _bdot_nt_fwd(a, b):
    return _b16dot(a, b, _NT), (a, b)


def _bdot_nt_bwd(res, g):
    a, b = res
    return _b16dot(g, b, _NN), _b16dot(g, a, _TN)


_bdot_nt.defvjp(_bdot_nt_fwd, _bdot_nt_bwd)


def _rms(x, gain):
    return x * lax.rsqrt(jnp.mean(x * x, axis=-1, keepdims=True) + RMS_EPS) * gain


def _vjp_of(f, n_in, diff):
    def g(*args):
        ins, cts = args[:n_in], args[n_in:]

        def fd(*d):
            full = list(ins)
            for pos, i in enumerate(diff):
                full[i] = d[pos]
            return f(*full)

        _, pull = jax.vjp(fd, *[ins[i] for i in diff])
        return pull(tuple(cts))
    return g


def _mm(a, b, *, name, ta=False, tb=False, a_pro=None, epi=None, epi_args=(), out_dtype=F32,
        tm=MM_TILE_M, tn=MM_TILE_N, tk=MM_TILE_K, dims=None, b_spec=None, o_spec=None, o_shape=None):
    if dims is None:
        m, k = (a.shape[1], a.shape[0]) if ta else a.shape
        n = b.shape[0] if tb else b.shape[1]
    else:
        m, n, k = dims
    tm, tn, tk = min(tm, m), min(tn, n), min(tk, k)
    assert m % tm == 0 and n % tn == 0 and k % tk == 0, (name, m, n, k, tm, tn, tk)
    nk = k // tk
    ne = len(epi_args)
    cdims = (((0 if ta else 1,), (1 if tb else 0,)), ((), ()))

    def body(a_ref, b_ref, *rest):
        e_refs, o_ref, acc = rest[:ne], rest[ne], rest[ne + 1]
        kk = pl.program_id(2)

        @pl.when(kk == 0)
        def _():
            acc[...] = jnp.zeros_like(acc)

        av = a_ref[...]
        if a_pro is not None:
            av = a_pro(av.astype(F32))
        acc[...] += lax.dot_general(av.astype(BF16), b_ref[...].astype(BF16), cdims, preferred_element_type=F32)

        @pl.when(kk == nk - 1)
        def _():
            r = acc[...]
            if epi is not None:
                r = epi(r, *[e[...] for e in e_refs])
            o_ref[...] = r.astype(o_ref.dtype)

    a_spec = pl.BlockSpec((tk, tm), lambda i, j, q: (q, i)) if ta else pl.BlockSpec((tm, tk), lambda i, j, q: (i, q))
    if b_spec is None:
        b_spec = pl.BlockSpec((tn, tk), lambda i, j, q: (j, q)) if tb else pl.BlockSpec((tk, tn), lambda i, j, q: (q, j))
    if o_spec is None:
        o_spec = pl.BlockSpec((tm, tn), lambda i, j, q: (i, j))
        o_shape = (m, n)
    e_specs = [pl.BlockSpec((tm, tn), lambda i, j, q: (i, j)) for _ in epi_args]
    return pl.pallas_call(
        body, grid=(m // tm, n // tn, nk), in_specs=[a_spec, b_spec] + e_specs, out_specs=o_spec,
        out_shape=jax.ShapeDtypeStruct(o_shape, out_dtype), scratch_shapes=[pltpu.VMEM((tm, tn), F32)],
        compiler_params=_params(("parallel", "parallel", "arbitrary")), name=name,
    )(a, b, *epi_args)


def _rowwise(fn, rows, params, out_rows, out_params=(), *, tm=256, name, chip_sums=()):
    t = rows[0].shape[0]
    tm = min(tm, t)
    assert t % tm == 0
    nr, npar, nor, nop, ns = len(rows), len(params), len(out_rows), len(out_params), len(chip_sums)
    steps = t // tm

    def body(*refs):
        r, p = refs[:nr], refs[nr:nr + npar]
        o, op = refs[nr + npar + ns:nr + npar + ns + nor], refs[nr + npar + ns + nor:nr + npar + ns + nor + nop]
        if ns:
            start, finish = _chip_exchange_plan(refs[nr + npar:nr + npar + ns], refs[nr + npar + ns + nor + nop:nr + npar + 2 * ns + nor + nop],
                                                *refs[nr + npar + 2 * ns + nor + nop:])
            pl.when(pl.program_id(0) == 0)(start)
        outs = fn(*[x[...] for x in r], *[x[...] for x in p])
        for ref, val in zip(o, outs[:nor]):
            ref[...] = val.astype(ref.dtype)
        if nop:
            @pl.when(pl.program_id(0) == 0)
            def _():
                for ref in op:
                    ref[...] = jnp.zeros_like(ref)

            for ref, val in zip(op, outs[nor:]):
                ref[...] += val.astype(F32)
        if ns:
            pl.when(pl.program_id(0) == steps - 1)(finish)

    in_specs = [pl.BlockSpec((tm, x.shape[1]), lambda i: (i, 0)) for x in rows]
    in_specs += [pl.BlockSpec(p.shape, lambda i: (0, 0)) for p in params]
    out_specs = [pl.BlockSpec((tm, s.shape[1]), lambda i: (i, 0)) for s in out_rows]
    out_specs += [pl.BlockSpec(s.shape, lambda i: (0, 0)) for s in out_params]
    return pl.pallas_call(
        body, grid=(steps,), in_specs=in_specs + [_ANY] * ns, out_specs=out_specs + [_ANY] * ns,
        out_shape=list(out_rows) + list(out_params) + [_sds(c.shape, c.dtype) for c in chip_sums],
        scratch_shapes=_chip_exchange_semaphores(ns) if ns else [],
        compiler_params=_params(("arbitrary",)), name=name,
    )(*rows, *params, *chip_sums)


def _sds(shape, dtype=F32):
    return jax.ShapeDtypeStruct(tuple(shape), dtype)


def _doubling_powers(l):
    powers = [l]
    for _ in range(int(math.log2(l.shape[-1])) - 1):
        powers.append(_dot3(powers[-1], powers[-1], _BNN))
    return powers


@jax.custom_vjp
def _unit_lower_solve(l, z):
    u = z
    for p in _doubling_powers(l):
        u = u + _dot3(p, u, _BNN)
    return u


def _unit_lower_solve_fwd(l, z):
    powers = _doubling_powers(l)
    u = z
    for p in powers:
        u = u + _dot3(p, u, _BNN)
    return u, (powers, u)


def _unit_lower_solve_bwd(res, du):
    powers, u = res
    g = du
    for p in powers:
        g = g + _dot3(p, g, _BTN)
    return _dot3(g, u, _BNT), g


_unit_lower_solve.defvjp(_unit_lower_solve_fwd, _unit_lower_solve_bwd)


def _wkv_chunk(s0, r, lw, k, v, a, b):
    nb, c, _ = r.shape
    ti = lax.broadcasted_iota(jnp.int32, (nb, c, c), 1)
    si = lax.broadcasted_iota(jnp.int32, (nb, c, c), 2)
    incl, strict = si <= ti, si < ti
    cum = _mask_dot(incl.astype(F32), lw)
    tot = jnp.sum(lw, axis=1, keepdims=True)
    rcum = tot - cum
    w_inv = jnp.exp(-cum)
    at, rt, bt, kt = a * jnp.exp(cum - lw), r * jnp.exp(cum), b * w_inv, k * w_inv
    l_ab = jnp.where(strict, _hdot(at, bt, _BNT), 0.0)
    l_ak = jnp.where(strict, _hdot(at, kt, _BNT), 0.0)
    t_rb = jnp.where(incl, _hdot(rt, bt, _BNT), 0.0)
    t_rk = jnp.where(incl, _hdot(rt, kt, _BNT), 0.0)
    u = _unit_lower_solve(l_ab, _hdot(at, s0, _BNT) + _hdot(l_ak, v, _BNN))
    y = _hdot(rt, s0, _BNT) + _hdot(t_rb, u, _BNN) + _hdot(t_rk, v, _BNN)
    e = jnp.exp(rcum)
    s1 = s0 * jnp.exp(tot) + _hdot(u, b * e, _BTN) + _hdot(v, k * e, _BTN)
    return y, s1


WKV_HEADS_PER_STEP = 16
GATHER_RELAY_AT = 0.625


def _first_and_last_step(grid):
    i, j = pl.program_id(0), pl.program_id(1)
    return jnp.logical_and(i == 0, j == 0), jnp.logical_and(i == grid[0] - 1, j == grid[1] - 1)


N_WKV_PARAMS = 7


def _tmix_chunk(s0, r, k, v, wl, al, w0, a0, k_k, k_a, ln_w, ln_b, r_k):
    lw, kmod, a, b = _wkv_prep(k, wl, al, w0, a0, k_k, k_a)
    y, s1 = _wkv_chunk(s0, r, lw, kmod, v, a, b)
    (y2,) = _wkv_post(y, r, kmod, v, ln_w, ln_b, r_k)
    return y2, s1


def _split_heads(x, nh):
    return x.reshape(x.shape[0], nh, HEAD_DIM).transpose(1, 0, 2)


def _merge_heads(xh):
    return jnp.concatenate([xh[h] for h in range(xh.shape[0])], axis=1)


def _wkv_fwd(r, k, v, wl, al, params, shards=()):
    t = r.shape[0]
    nh, n = N_HEADS, HEAD_DIM
    nc = t // WKV_CHUNK
    hb = WKV_HEADS_PER_STEP
    grid = (nh // hb, nc)
    ns = len(shards)
    n_in = 5 + N_WKV_PARAMS

    def body(*refs):
        y_ref, s_ref = refs[n_in + ns:n_in + ns + 2]
        state = refs[n_in + 2 * ns + 2]
        if ns:
            start, relay, finish = _gather_plan(refs[n_in:n_in + ns], refs[n_in + ns + 2:n_in + 2 * ns + 2], *refs[n_in + 2 * ns + 3:])
            first, last = _first_and_last_step(grid)
            pl.when(first)(start)
            relay_step = int(GATHER_RELAY_AT * grid[0] * nc)
            pl.when(jnp.logical_and(pl.program_id(0) == relay_step // nc, pl.program_id(1) == relay_step % nc))(relay)

        @pl.when(pl.program_id(1) == 0)
        def _():
            state[...] = jnp.zeros_like(state)

        s0 = state[...]
        s_ref[:, 0] = s0
        rows = [_split_heads(ref[...], hb) for ref in refs[:5]]
        y, s1 = _tmix_chunk(s0, *rows, *[ref[...] for ref in refs[5:n_in]])
        y_ref[...] = _merge_heads(y)
        state[...] = s1
        if ns:
            pl.when(last)(finish)

    blk = pl.BlockSpec((WKV_CHUNK, hb * n), lambda h, c: (c, h))
    pblk = pl.BlockSpec((hb, 1, n), lambda h, c: (h, 0, 0))
    sblk = pl.BlockSpec((hb, 1, n, n), lambda h, c: (h, c, 0, 0))
    outs = pl.pallas_call(
        body, grid=grid, in_specs=[blk] * 5 + [pblk] * N_WKV_PARAMS + [_ANY] * ns, out_specs=[blk, sblk] + [_ANY] * ns,
        out_shape=[_sds((t, nh * n)), _sds((nh, nc, n, n))] + [_sds((N_DEV,) + s.shape, s.dtype) for s in shards],
        scratch_shapes=[pltpu.VMEM((hb, n, n), F32)] + (_gather_semaphores(ns) if ns else []),
        compiler_params=_params(("arbitrary", "arbitrary")), name="wkv_fwd",
    )(r, k, v, wl, al, *params, *shards)
    return outs[0], outs[1], list(outs[2:])


def _wkv_bwd(r, k, v, wl, al, params, states, dy, chip_sums=()):
    t = r.shape[0]
    nh, n = N_HEADS, HEAD_DIM
    nc = t // WKV_CHUNK
    hb = WKV_HEADS_PER_STEP
    grid = (nh // hb, nc)
    ns = len(chip_sums)
    n_in = 5 + N_WKV_PARAMS
    n_out = 5 + N_WKV_PARAMS

    def body(*refs):
        s_ref, dy_ref = refs[n_in:n_in + 2]
        out_refs = refs[n_in + 2 + ns:n_in + 2 + ns + n_out]
        dstate = refs[n_in + 2 + 2 * ns + n_out]
        if ns:
            start, finish = _chip_exchange_plan(refs[n_in + 2:n_in + 2 + ns], refs[n_in + 2 + ns + n_out:n_in + 2 + 2 * ns + n_out],
                                                *refs[n_in + 3 + 2 * ns + n_out:])
            first, last = _first_and_last_step(grid)
            pl.when(first)(start)

        @pl.when(pl.program_id(1) == 0)
        def _():
            dstate[...] = jnp.zeros_like(dstate)
            for ref in out_refs[5:]:
                ref[...] = jnp.zeros_like(ref)

        rows = [_split_heads(ref[...], hb) for ref in refs[:5]]
        _, pull = jax.vjp(_tmix_chunk, s_ref[:, 0], *rows, *[ref[...] for ref in refs[5:n_in]])
        grads = pull((_split_heads(dy_ref[...], hb), dstate[...]))
        dstate[...] = grads[0]
        for ref, val in zip(out_refs[:5], grads[1:6]):
            ref[...] = _merge_heads(val).astype(ref.dtype)
        for ref, val in zip(out_refs[5:], grads[6:]):
            ref[...] += val
        if ns:
            pl.when(last)(finish)

    blk = pl.BlockSpec((WKV_CHUNK, hb * n), lambda h, c: (nc - 1 - c, h))
    pblk = pl.BlockSpec((hb, 1, n), lambda h, c: (h, 0, 0))
    sblk = pl.BlockSpec((hb, 1, n, n), lambda h, c: (h, nc - 1 - c, 0, 0))
    outs = pl.pallas_call(
        body, grid=grid, in_specs=[blk] * 5 + [pblk] * N_WKV_PARAMS + [sblk, blk] + [_ANY] * ns,
        out_specs=[blk] * 5 + [pblk] * N_WKV_PARAMS + [_ANY] * ns,
        out_shape=[_sds((t, nh * n), BF16)] * 5 + [_sds((nh, 1, n))] * N_WKV_PARAMS + [_sds(p.shape, p.dtype) for p in chip_sums],
        scratch_shapes=[pltpu.VMEM((hb, n, n), F32)] + (_chip_exchange_semaphores(ns) if ns else []),
        compiler_params=_params(("arbitrary", "arbitrary")), name="wkv_bwd",
    )(r, k, v, wl, al, *params, states, dy, *chip_sums)
    return outs[:5], outs[5:n_out], list(outs[n_out:])


def _wkv_prep(k, wl, al, w0, a0, k_k, k_a):
    z = -(w0 + wl)
    softplus = jnp.maximum(z, 0.0) + jnp.log1p(jnp.exp(-jnp.abs(z)))
    lw = -jnp.exp(-softplus - 0.5)
    asig = jax.nn.sigmoid(a0 + al)
    kk = k * k_k
    kk = kk / jnp.maximum(jnp.sqrt(jnp.sum(kk * kk, axis=-1, keepdims=True)), L2_EPS)
    kmod = k * (1.0 + (asig - 1.0) * k_a)
    return lw, kmod, -kk, kk * asig


def _wkv_post(y, r, kmod, v, ln_w, ln_b, r_k):
    mu = jnp.mean(y, axis=-1, keepdims=True)
    var = jnp.mean(jnp.square(y - mu), axis=-1, keepdims=True)
    yn = (y - mu) * lax.rsqrt(var + GN_EPS)
    yn = yn * ln_w + ln_b
    return (yn + jnp.sum(r * kmod * r_k, axis=-1, keepdims=True) * v,)


def _attn_group(nonzero_block, q, kc, kp, vc, vp, cos_c, sin_c, cos_p, sin_p, q_gain, k_gain, sinks):
    ri = lax.broadcasted_iota(jnp.int32, (HEAD_DIM, HEAD_DIM), 0)
    ci = lax.broadcasted_iota(jnp.int32, (HEAD_DIM, HEAD_DIM), 1)
    half = HEAD_DIM // 2
    rot = jnp.where(ri == ci + half, -1.0, 0.0) + jnp.where(ri + half == ci, 1.0, 0.0)
    rows = Q_PER_KV * ATT_BLOCK

    def rope(x, cos, sin):
        return x * cos + _hdot(x, rot, _NN) * sin

    kcr = rope(_rms(kc, k_gain), cos_c, sin_c)
    kpr = rope(_rms(kp, k_gain), cos_p, sin_p)
    qn = _rms(q, q_gain)
    qr = qn * cos_c + _hdot(qn.reshape(rows, HEAD_DIM), rot, _NN).reshape(q.shape) * sin_c
    q2 = qr.reshape(rows, HEAD_DIM)
    qi = lax.broadcasted_iota(jnp.int32, (1, ATT_BLOCK, ATT_BLOCK), 1)
    ki = lax.broadcasted_iota(jnp.int32, (1, ATT_BLOCK, ATT_BLOCK), 2)
    mask_c = ki <= qi
    mask_p = jnp.logical_and(ki > qi, nonzero_block)
    lane0 = (lax.broadcasted_iota(jnp.int32, (1, 1, 128), 2) == 0).astype(F32)
    shape3 = (Q_PER_KV, ATT_BLOCK, ATT_BLOCK)
    sc = jnp.where(mask_c, (_bdot_nt(q2, kcr) * (HEAD_DIM ** -0.5)).reshape(shape3), MASK_VALUE)
    sp = jnp.where(mask_p, (_bdot_nt(q2, kpr) * (HEAD_DIM ** -0.5)).reshape(shape3), MASK_VALUE)
    sk = jnp.sum(sinks * lane0, axis=2, keepdims=True)
    mx = jnp.maximum(jnp.maximum(jnp.max(sc, axis=2, keepdims=True), jnp.max(sp, axis=2, keepdims=True)), sk)
    mx = lax.stop_gradient(mx)
    ec, ep = jnp.exp(sc - mx), jnp.exp(sp - mx)
    den = jnp.sum(ec, axis=2, keepdims=True) + jnp.sum(ep, axis=2, keepdims=True) + jnp.exp(sk - mx)
    out = _bdot((ec / den).reshape(rows, ATT_BLOCK), vc) + _bdot((ep / den).reshape(rows, ATT_BLOCK), vp)
    return out.reshape(q.shape)


def _attn_rows(nonzero_block, qs, kcs, kps, vcs, vps, tabs, q_gain, k_gain, sinks):
    return [_attn_group(nonzero_block, qs[g], kcs[g], kps[g], vcs[g], vps[g], *tabs, q_gain, k_gain, sinks[g])
            for g in range(N_KV_HEADS)]


def _attn_operands(q_tile, kvc_tile, kvp_tile, sink_ref):
    q3 = _split_heads(q_tile, N_HEADS)
    kvc, kvp = _split_heads(kvc_tile, 2 * N_KV_HEADS), _split_heads(kvp_tile, 2 * N_KV_HEADS)
    groups = range(N_KV_HEADS)
    qs = [q3[Q_PER_KV * g:Q_PER_KV * (g + 1)] for g in groups]
    sinks = [sink_ref[Q_PER_KV * g:Q_PER_KV * (g + 1)] for g in groups]
    return (qs, [kvc[g] for g in groups], [kvp[g] for g in groups], [kvc[N_KV_HEADS + g] for g in groups],
            [kvp[N_KV_HEADS + g] for g in groups], sinks)


def _attn_specs(t):
    nb = t // ATT_BLOCK
    prev = lambda n: jnp.maximum(n - 1, 0)
    kv_width = 2 * N_KV_HEADS * HEAD_DIM
    q_spec = pl.BlockSpec((ATT_BLOCK, D_MODEL), lambda n: (n, 0))
    kv_c = pl.BlockSpec((ATT_BLOCK, kv_width), lambda n: (n, 0))
    kv_p = pl.BlockSpec((ATT_BLOCK, kv_width), lambda n: (prev(n), 0))
    tab_c = pl.BlockSpec((ATT_BLOCK, HEAD_DIM), lambda n: (n, 0))
    tab_p = pl.BlockSpec((ATT_BLOCK, HEAD_DIM), lambda n: (prev(n), 0))
    gain = pl.BlockSpec((1, HEAD_DIM), lambda n: (0, 0))
    sink = pl.BlockSpec((N_HEADS, 1, 128), lambda n: (0, 0, 0))
    return nb, q_spec, kv_c, kv_p, tab_c, tab_p, gain, sink


def _attn_fwd(q, kv, cos, sin, q_gain, k_gain, sinks):
    t = q.shape[0]
    nb, q_spec, kv_c, kv_p, tab_c, tab_p, gain, sink = _attn_specs(t)

    def body(q_ref, kvc, kvp, cc, sc, cp, sp, qg, kg, sk, o_ref):
        qs, kcs, kps, vcs, vps, sinks_ = _attn_operands(q_ref[...], kvc[...], kvp[...], sk)
        outs = _attn_rows(pl.program_id(0) > 0, qs, kcs, kps, vcs, vps, (cc[...], sc[...], cp[...], sp[...]),
                          qg[...], kg[...], sinks_)
        o_ref[...] = _merge_heads(jnp.concatenate(outs, axis=0)).astype(o_ref.dtype)

    return pl.pallas_call(
        body, grid=(nb,), in_specs=[q_spec, kv_c, kv_p, tab_c, tab_c, tab_p, tab_p, gain, gain, sink],
        out_specs=q_spec, out_shape=_sds(q.shape, BF16), compiler_params=_params(("arbitrary",)), name="attn_fwd",
    )(q, kv, kv, cos, sin, cos, sin, q_gain, k_gain, sinks)


def _attn_bwd(q, kv, cos, sin, q_gain, k_gain, sinks, do):
    t = q.shape[0]
    nb, q_spec, kv_c, kv_p, tab_c, tab_p, gain, sink = _attn_specs(t)

    def body(q_ref, kvc, kvp, cc, sc, cp, sp, qg, kg, sk, do_ref, dq_ref, dkvc_ref, dkvp_ref, dqg_ref, dkg_ref, dsk_ref):
        nonzero = pl.program_id(0) > 0
        tabs = (cc[...], sc[...], cp[...], sp[...])
        qs, kcs, kps, vcs, vps, sinks_ = _attn_operands(q_ref[...], kvc[...], kvp[...], sk)

        def f(qs_, kcs_, kps_, vcs_, vps_, qgv, kgv, sks):
            return _attn_rows(nonzero, qs_, kcs_, kps_, vcs_, vps_, tabs, qgv, kgv, sks)

        _, pull = jax.vjp(f, qs, kcs, kps, vcs, vps, qg[...], kg[...], sinks_)
        do3 = _split_heads(do_ref[...], N_HEADS)
        dqs, dkcs, dkps, dvcs, dvps, dqg, dkg, dsks = pull([do3[Q_PER_KV * g:Q_PER_KV * (g + 1)] for g in range(N_KV_HEADS)])
        dq_ref[...] = _merge_heads(jnp.concatenate(dqs, axis=0)).astype(dq_ref.dtype)
        dkvc_ref[...] = jnp.concatenate(dkcs + dvcs, axis=1)
        dkvp_ref[...] = jnp.concatenate(dkps + dvps, axis=1)

        @pl.when(pl.program_id(0) == 0)
        def _():
            dqg_ref[...] = jnp.zeros_like(dqg_ref)
            dkg_ref[...] = jnp.zeros_like(dkg_ref)
            dsk_ref[...] = jnp.zeros_like(dsk_ref)

        dqg_ref[...] += dqg
        dkg_ref[...] += dkg
        for g in range(N_KV_HEADS):
            dsk_ref[Q_PER_KV * g:Q_PER_KV * (g + 1)] += dsks[g]

    return pl.pallas_call(
        body, grid=(nb,), in_specs=[q_spec, kv_c, kv_p, tab_c, tab_c, tab_p, tab_p, gain, gain, sink, q_spec],
        out_specs=[q_spec, kv_c, kv_c, gain, gain, sink],
        out_shape=[_sds(q.shape, BF16), _sds(kv.shape), _sds(kv.shape), _sds((1, HEAD_DIM)), _sds((1, HEAD_DIM)), _sds(sinks.shape)],
        compiler_params=_params(("arbitrary",)), name="attn_bwd",
    )(q, kv, kv, cos, sin, cos, sin, q_gain, k_gain, sinks, do)


def _time_shift_lerps(x, xs, gain, *mix):
    xn, xsn = _rms(x, gain), _rms(xs, gain)
    xx = xsn - xn
    return tuple(xn + xx * m for m in mix)


def _residual_norm(h, delta, gain):
    hn = h + delta
    return hn, _rms(hn, gain)


def _residual_norm2(h, delta, gain_a, gain_b):
    hn = h + delta
    return hn, _rms(hn, gain_a), _rms(hn, gain_b)


def _relu2(u):
    return jnp.square(jnp.maximum(u, 0.0))


def _sigmoid(z):
    return jax.nn.sigmoid(z)


def _shift_down(x):
    return jnp.pad(x[:-1], ((1, 0), (0, 0)))


def _shift_up(x):
    return jnp.pad(x[1:], ((0, 1), (0, 0)))


def _rope_tables(t):
    half = HEAD_DIM // 2
    inv_freq = jnp.power(ROPE_THETA, -jnp.arange(half, dtype=F32) / half)
    ang = jnp.arange(t, dtype=jnp.int32).astype(F32)[:, None] * inv_freq[None, :]
    cos, sin = jnp.cos(ang), jnp.sin(ang)
    return jnp.concatenate([cos, cos], axis=1), jnp.concatenate([sin, sin], axis=1)


def _mlp_fwd(hn, w_up, w_down, layer, up_dev_major):
    t = hn.shape[0]
    if up_dev_major:
        cw = w_up.shape[2]
        u = _mm(hn, w_up, name=f"mlp{layer}_up", dims=(t, D_FF, D_MODEL), tn=cw, tk=D_MODEL,
                b_spec=pl.BlockSpec((None, D_MODEL, cw), lambda i, j, q: (j, q, 0)))
    else:
        u = _mm(hn, w_up, name=f"mlp{layer}_up")
    out = _mm(u, w_down, a_pro=_relu2, name=f"mlp{layer}_down")
    return u, out


def _mm_pair_reduce(a, b, *, device_axis, tile, a_pro=None, name):
    t = a.shape[0]
    tr, tc = tile
    other = (b.shape[1] // tc) if device_axis == "m" else (a.shape[1] // tr)
    n_tiles = N_CHIPS * other
    core = lax.axis_index("c").astype(jnp.int32).reshape(1)
    shape = (N_CHIPS, tr, other * tc) if device_axis == "m" else (N_CHIPS, other * tr, tc)

    inner = other if device_axis == "m" else N_CHIPS

    def body(core_ref, a_ref, b_ref, out_ref, recv_hbm, send_buf, recv_tile, send_sems, recv_sems, local_sem):
        phase, major, minor = pl.program_id(0), pl.program_id(1), pl.program_id(2)
        q, o = (major, minor) if device_axis == "m" else (minor, major)
        sibling = (lax.axis_index("x"), lax.axis_index("y"), 1 - core_ref[0])
        tile_no = major * inner + minor
        slot = lax.rem(tile_no, 2)

        def send(number, buffer):
            hi, lo = number // inner, lax.rem(number, inner)
            chip, tile_o = (hi, lo) if device_axis == "m" else (lo, hi)
            return pltpu.make_async_remote_copy(
                src_ref=send_buf.at[buffer], dst_ref=recv_hbm.at[chip, tile_o],
                send_sem=send_sems.at[number], recv_sem=recv_sems.at[number], device_id=sibling, device_id_type=_MESH_ID)

        def landed():
            return pltpu.make_async_copy(recv_hbm.at[q, o], recv_tile, local_sem)

        @pl.when(phase == 1)
        def _():
            @pl.when(tile_no == 0)
            def _():
                send(n_tiles - 2, lax.rem(n_tiles - 2, 2)).wait_send()
                send(n_tiles - 1, lax.rem(n_tiles - 1, 2)).wait_send()

            send(tile_no, slot).wait_recv()
            landed().start()

        av = a_ref[...]
        if a_pro is not None:
            av = a_pro(av.astype(F32))
        acc = lax.dot_general(av.astype(BF16), b_ref[...].astype(BF16), _TN, preferred_element_type=F32)

        @pl.when(phase == 0)
        def _():
            @pl.when(tile_no >= 2)
            def _():
                send(tile_no - 2, slot).wait_send()

            send_buf[slot] = acc.astype(send_buf.dtype)
            send(tile_no, slot).start()

        @pl.when(phase == 1)
        def _():
            landed().wait()
            out_ref[...] = (acc + recv_tile[...].astype(F32)).astype(out_ref.dtype)

    def device(phase, q, core_ref):
        return 2 * q + jnp.where(phase == 0, 1 - core_ref[0], core_ref[0])

    parked = lambda p, i: jnp.where(p == 0, 0, i)
    if device_axis == "m":
        a_spec = pl.BlockSpec((t, tr), lambda p, q, o, c: (0, device(p, q, c)))
        b_spec = pl.BlockSpec((t, tc), lambda p, q, o, c: (0, o))
        o_spec = pl.BlockSpec((None, tr, tc), lambda p, q, o, c: (parked(p, q), 0, parked(p, o)))
    else:
        a_spec = pl.BlockSpec((t, tr), lambda p, o, q, c: (0, o))
        b_spec = pl.BlockSpec((t, tc), lambda p, o, q, c: (0, device(p, q, c)))
        o_spec = pl.BlockSpec((None, tr, tc), lambda p, o, q, c: (parked(p, q), parked(p, o), 0))
    grid_spec = pltpu.PrefetchScalarGridSpec(
        num_scalar_prefetch=1, grid=(2, n_tiles // inner, inner), in_specs=[a_spec, b_spec], out_specs=[o_spec, _ANY],
        scratch_shapes=[pltpu.VMEM((2, tr, tc), BF16), pltpu.VMEM((tr, tc), BF16),
                        pltpu.SemaphoreType.DMA((n_tiles,)), pltpu.SemaphoreType.DMA((n_tiles,)), pltpu.SemaphoreType.DMA(())])
    sums, _ = pl.pallas_call(
        body, grid_spec=grid_spec, out_shape=[_sds(shape, BF16), _sds((N_CHIPS, other, tr, tc), BF16)],
        compiler_params=_params(("arbitrary", "arbitrary", "arbitrary")), name=name,
    )(core, a, b)
    return sums


MLP_PAIR_TILE = (1024, 1024)
SQUARE_PAIR_TILE = (D_MODEL // N_DEV, D_MODEL)


def _mlp_bwd(hn, u, dh, w_up, w_down, layer, up_dev_major, pair_reduce=False):
    t = hn.shape[0]
    du = _mm(dh, w_down, tb=True, epi=lambda r, uu: r * (2.0 * jnp.maximum(uu, 0.0)), epi_args=(u,), out_dtype=BF16,
             name=f"mlp{layer}_du")
    if pair_reduce:
        cw = w_up.shape[2]
        d_down = _mm_pair_reduce(u, dh, device_axis="m", tile=MLP_PAIR_TILE, a_pro=_relu2, name=f"mlp{layer}_ddown")
        d_up = _mm_pair_reduce(hn, du, device_axis="n", tile=MLP_PAIR_TILE, name=f"mlp{layer}_dup")
        dhn = _mm(du, w_up, tb=True, name=f"mlp{layer}_dhn", dims=(t, D_MODEL, D_FF), tk=cw,
                  b_spec=pl.BlockSpec((None, MM_TILE_N, cw), lambda i, j, q: (q, j, 0)))
        return dhn, d_up, d_down
    d_down = _mm(u, dh, ta=True, a_pro=_relu2, name=f"mlp{layer}_ddown")
    if up_dev_major:
        cw = w_up.shape[2]
        d_up = _mm(hn, du, ta=True, name=f"mlp{layer}_dup", dims=(D_MODEL, D_FF, t), tn=cw,
                   o_spec=pl.BlockSpec((None, MM_TILE_M, cw), lambda i, j, q: (j, i, 0)), o_shape=(N_DEV, D_MODEL, cw))
        dhn = _mm(du, w_up, tb=True, name=f"mlp{layer}_dhn", dims=(t, D_MODEL, D_FF), tk=cw,
                  b_spec=pl.BlockSpec((None, MM_TILE_N, cw), lambda i, j, q: (q, j, 0)))
    else:
        d_up = _mm(hn, du, ta=True, name=f"mlp{layer}_dup")
        dhn = _mm(du, w_up, tb=True, name=f"mlp{layer}_dhn")
    return dhn, d_up, d_down


_LATE_WEIGHTS = ("mlp_w_up0", "mlp_w_up1", "mlp_w_down0", "mlp_w_down1", "b_w_q", "b_w_o")
_LATE_GATHER = _LATE_WEIGHTS + ("a_w_out", "a_g1", "w_kv")
_EARLY_GRADS = _LATE_WEIGHTS + ("a_w_out",)
_RKV_GRADS = ("a_w_r", "a_w_k", "a_w_v")


def _local_step(x, target, w, late_shards=None, up_dev_major=True):
    t = x.shape[0]
    g = {}
    w = dict(w)
    row = lambda: _sds((t, D_MODEL))
    rowb = lambda: _sds((t, D_MODEL), BF16)
    vec = lambda: _sds((1, D_MODEL))

    xs = _shift_down(x)
    mix = [w["a_mix"][i:i + 1] for i in range(6)]
    xr, xk, xv, xw, xa, xg = _rowwise(_time_shift_lerps, [x, xs], [w["a_norm"]] + mix, [rowb()] * 6, tm=256, name="tmix_lerp")
    r = _mm(xr, w["a_w_r"], name="tmix_r")
    k = _mm(xk, w["a_w_k"], name="tmix_k")
    v = _mm(xv, w["a_w_v"], name="tmix_v")
    lw1 = _mm(xw, w["a_w1"], name="tmix_w1")
    wl = _mm(lw1, w["a_w2"], a_pro=jnp.tanh, name="tmix_w2")
    la1 = _mm(xa, w["a_a1"], name="tmix_a1")
    al = _mm(la1, w["a_a2"], name="tmix_a2")

    hv = lambda name: w[name].reshape(N_HEADS, 1, HEAD_DIM)
    wkv_params = [hv(name) for name in ("a_w0", "a_a0", "a_k_k", "a_k_a", "a_ln_x_w", "a_ln_x_b", "a_r_k")]
    y2, states, gathered = _wkv_fwd(r, k, v, wl, al, wkv_params, late_shards or ())
    for name, arr in zip(_LATE_GATHER, gathered):
        w[name] = arr if name.startswith("mlp_w_up") else arr.reshape(N_DEV * arr.shape[1], arr.shape[2])
    lg1 = _mm(xg, w["a_g1"], name="tmix_g1")
    gate = _mm(lg1, w["a_g2"], a_pro=_sigmoid, name="tmix_g2")
    (yg,) = _rowwise(lambda a, b: (a * b,), [y2, gate], [], [rowb()], name="tmix_gate")
    att = _mm(yg, w["a_w_out"], name="tmix_out")

    h1, hn0 = _rowwise(_residual_norm, [x, att], [w["mlp_norm0"]], [row(), rowb()], name="res_norm0")
    u0, m0 = _mlp_fwd(hn0, w["mlp_w_up0"], w["mlp_w_down0"], 0, up_dev_major)

    h2, kvn, qn = _rowwise(_residual_norm2, [h1, m0], [w["kv_norm"], w["b_norm"]], [row(), rowb(), rowb()], name="res_norm_kvq")
    kv = _mm(kvn, w["w_kv"], name="kv_proj")
    q = _mm(qn, w["b_w_q"], name="q_proj")
    cos, sin = _rope_tables(t)
    sinks = jnp.broadcast_to(w["b_sinks"].reshape(N_HEADS, 1, 1), (N_HEADS, 1, 128))
    o = _attn_fwd(q, kv, cos, sin, w["b_q_norm"], w["k_norm"], sinks)
    att2 = _mm(o, w["b_w_o"], name="attn_out")

    h3, hn1 = _rowwise(_residual_norm, [h2, att2], [w["mlp_norm1"]], [row(), rowb()], name="res_norm1")
    u1, m1 = _mlp_fwd(hn1, w["mlp_w_up1"], w["mlp_w_down1"], 1, up_dev_major)

    def loss_fn(h, m, tg):
        diff = (h + m) - tg
        part = 0.5 * jnp.sum(jnp.mean(jnp.square(diff), axis=-1, keepdims=True), axis=0, keepdims=True)
        dh = diff * (1.0 / D_MODEL)
        return dh, dh, jnp.broadcast_to(part, (1, 128))

    dh4, dh4b, loss = _rowwise(loss_fn, [h3, m1, target], [], [row(), rowb()], [_sds((1, 128))], name="loss")

    def res_norm_bwd(h, dnext, dhn, gain):
        dh, dgain = _vjp_of(lambda hh, gg: (_rms(hh, gg),), 2, (0, 1))(h, gain, dhn)
        return dnext + dh, dnext + dh, dgain

    distributed = late_shards is not None

    def square_dw(a_, b_, name):
        if distributed:
            return _mm_pair_reduce(a_, b_, device_axis="m", tile=SQUARE_PAIR_TILE, name=name)
        return _mm(a_, b_, ta=True, name=name)

    dhn1, g["mlp_w_up1"], g["mlp_w_down1"] = _mlp_bwd(hn1, u1, dh4b, w["mlp_w_up1"], w["mlp_w_down1"], 1, up_dev_major, distributed)
    dh3, dh3b, g["mlp_norm1"] = _rowwise(res_norm_bwd, [h3, dh4, dhn1], [w["mlp_norm1"]], [row(), rowb()], [vec()], name="res_norm1_bwd")

    g["b_w_o"] = square_dw(o, dh3b, "attn_out_dw")
    do = _mm(dh3b, w["b_w_o"], tb=True, name="attn_out_dx")
    dq, dkv_own, dkv_prev, g["b_q_norm"], g["k_norm"], dsinks = _attn_bwd(
        q, kv, cos, sin, w["b_q_norm"], w["k_norm"], sinks, do)
    g["b_sinks"] = dsinks[:, 0, 0].reshape(1, N_HEADS)
    g["b_w_q"] = square_dw(qn, dq, "q_proj_dw")
    dqn = _mm(dq, w["b_w_q"], tb=True, name="q_proj_dx")
    dkv_prev = jnp.pad(dkv_prev[ATT_BLOCK:], ((0, ATT_BLOCK), (0, 0)))
    (dkv,) = _rowwise(lambda a, b: (a + b,), [dkv_own, dkv_prev], [], [_sds(kv.shape, BF16)], name="kv_grad_sum")
    g["w_kv"] = _mm(kvn, dkv, ta=True, name="kv_proj_dw")
    dkvn = _mm(dkv, w["w_kv"], tb=True, name="kv_proj_dx")

    def res_norm2_bwd(h, dnext, dna, dnb, gain_a, gain_b):
        dha, dga = _vjp_of(lambda hh, gg: (_rms(hh, gg),), 2, (0, 1))(h, gain_a, dna)
        dhb, dgb = _vjp_of(lambda hh, gg: (_rms(hh, gg),), 2, (0, 1))(h, gain_b, dnb)
        return dnext + dha + dhb, dnext + dha + dhb, dga, dgb

    dh2, dh2b, g["kv_norm"], g["b_norm"] = _rowwise(res_norm2_bwd, [h2, dh3, dkvn, dqn], [w["kv_norm"], w["b_norm"]],
                                                    [row(), rowb()], [vec(), vec()], name="res_norm_kvq_bwd")

    dhn0, g["mlp_w_up0"], g["mlp_w_down0"] = _mlp_bwd(hn0, u0, dh2b, w["mlp_w_up0"], w["mlp_w_down0"], 0, up_dev_major, distributed)
    dh1, dh1b, g["mlp_norm0"] = _rowwise(res_norm_bwd, [h1, dh2, dhn0], [w["mlp_norm0"]], [row(), rowb()], [vec()], name="res_norm0_bwd")

    g["a_w_out"] = square_dw(yg, dh1b, "tmix_out_dw")
    dyg = _mm(dh1b, w["a_w_out"], tb=True, name="tmix_out_dx")
    dy2, dgate = _rowwise(lambda d, a, b: (d * b, d * a), [dyg, y2, gate], [], [row(), rowb()], name="tmix_gate_bwd")
    g["a_g2"] = _mm(lg1, dgate, ta=True, a_pro=_sigmoid, name="tmix_g2_dw")

    def dsigmoid(rr, z):
        s = jax.nn.sigmoid(z)
        return rr * s * (1.0 - s)

    dlg1 = _mm(dgate, w["a_g2"], tb=True, epi=dsigmoid, epi_args=(lg1,), out_dtype=BF16, name="tmix_g2_dx")
    g["a_g1"] = _mm(xg, dlg1, ta=True, name="tmix_g1_dw")
    dxg = _mm(dlg1, w["a_g1"], tb=True, name="tmix_g1_dx")

    early_sums = [g[name] for name in _EARLY_GRADS] if distributed else ()
    (dr, dk, dv, dwl, dal), param_grads, early_reduced = _wkv_bwd(r, k, v, wl, al, wkv_params, states, dy2, early_sums)
    for name, pg in zip(("a_w0", "a_a0", "a_k_k", "a_k_a", "a_ln_x_w", "a_ln_x_b", "a_r_k"), param_grads):
        g[name] = pg.reshape(1, D_MODEL)

    g["a_w_r"] = square_dw(xr, dr, "tmix_r_dw")
    g["a_w_k"] = square_dw(xk, dk, "tmix_k_dw")
    g["a_w_v"] = square_dw(xv, dv, "tmix_v_dw")
    dxr = _mm(dr, w["a_w_r"], tb=True, name="tmix_r_dx")
    dxk = _mm(dk, w["a_w_k"], tb=True, name="tmix_k_dx")
    dxv = _mm(dv, w["a_w_v"], tb=True, name="tmix_v_dx")
    g["a_w2"] = _mm(lw1, dwl, ta=True, a_pro=jnp.tanh, name="tmix_w2_dw")

    def dtanh(rr, z):
        th = jnp.tanh(z)
        return rr * (1.0 - th * th)

    dlw1 = _mm(dwl, w["a_w2"], tb=True, epi=dtanh, epi_args=(lw1,), out_dtype=BF16, name="tmix_w2_dx")
    g["a_w1"] = _mm(xw, dlw1, ta=True, name="tmix_w1_dw")
    dxw = _mm(dlw1, w["a_w1"], tb=True, name="tmix_w1_dx")
    g["a_a2"] = _mm(la1, dal, ta=True, name="tmix_a2_dw")
    dla1 = _mm(dal, w["a_a2"], tb=True, out_dtype=BF16, name="tmix_a2_dx")
    g["a_a1"] = _mm(xa, dla1, ta=True, name="tmix_a1_dw")
    dxa = _mm(dla1, w["a_a1"], tb=True, name="tmix_a1_dx")

    lerp_bwd = _vjp_of(_time_shift_lerps, 9, tuple(range(9)))

    def lerp_bwd_rows(x_, xs_, d0, d1, d2, d3, d4, d5, gain, *mx):
        return lerp_bwd(x_, xs_, gain, *mx, d0, d1, d2, d3, d4, d5)

    rkv_sums = [g[name] for name in _RKV_GRADS] if distributed else []
    outs = _rowwise(lerp_bwd_rows, [x, xs, dxr, dxk, dxv, dxw, dxa, dxg], [w["a_norm"]] + mix, [row(), row()], [vec()] * 7,
                    tm=128, name="tmix_lerp_bwd", chip_sums=rkv_sums)
    dx_a, dxs, g["a_norm"] = outs[0], outs[1], outs[2]
    g["a_mix"] = jnp.concatenate(outs[3:9], axis=0)
    (grad_x,) = _rowwise(lambda a, b, c: (a + b + c,), [dh1, dx_a, _shift_up(dxs)], [], [row()], name="grad_x_sum")
    reduced = dict(zip(_EARLY_GRADS + _RKV_GRADS, list(early_reduced) + list(outs[9:]))) if distributed else None
    return loss, grad_x, g, reduced


_ANY = pl.BlockSpec(memory_space=pl.ANY)
_MESH_ID = pl.DeviceIdType.MESH


def _linear(pos):
    return 4 * pos[0] + 2 * pos[1] + pos[2]


def _all_gather(shards, name):
    n = len(shards)

    def body(*refs):
        start, relay, finish = _gather_plan(refs[:n], refs[n:2 * n], *refs[2 * n:])
        start()
        relay()
        finish()

    return pl.pallas_call(
        body, out_shape=[_sds((N_DEV,) + s.shape, s.dtype) for s in shards], in_specs=[_ANY] * n, out_specs=[_ANY] * n,
        scratch_shapes=_gather_semaphores(n), name=name,
    )(*shards)


GATHER_COPIES = 8


def _gather_semaphores(n):
    return [pltpu.SemaphoreType.DMA((n * GATHER_COPIES,)), pltpu.SemaphoreType.DMA((n * GATHER_COPIES,)),
            pltpu.SemaphoreType.DMA((n,))]


def _gather_plan(ins, outs, send_sems, recv_sems, local_sems):
    n = len(ins)
    x, y, c = lax.axis_index("x"), lax.axis_index("y"), lax.axis_index("c")
    me, sibling = (x, y, c), (x, y, 1 - c)
    x_nbr, y_nbr, diag = (1 - x, y, c), (x, 1 - y, c), (1 - x, 1 - y, c)
    other = lambda pos: (pos[0], pos[1], 1 - c)

    def halves(a):
        rows = ins[a].shape[0]
        if rows % (2 * BF16_SUBLANES):
            return (0, rows), None
        return (0, rows // 2), (rows // 2, rows // 2)

    def copy(a, k, block, to, src=None, rows=None):
        dst = outs[a].at[_linear(block)]
        if rows is not None:
            dst = dst.at[pl.ds(rows[0], rows[1])]
        return pltpu.make_async_remote_copy(
            src_ref=dst if src is None else src, dst_ref=dst, send_sem=send_sems.at[a * GATHER_COPIES + k],
            recv_sem=recv_sems.at[a * GATHER_COPIES + k], device_id=to, device_id_type=_MESH_ID)

    def own_copies():
        mine = [pltpu.make_async_copy(ins[a], outs[a].at[_linear(me)], local_sems.at[a]) for a in range(n)]
        sent = []
        for a in range(n):
            sent += [copy(a, 0, me, sibling, src=ins[a]), copy(a, 1, me, x_nbr, src=ins[a]), copy(a, 2, me, y_nbr, src=ins[a])]
        return mine, sent

    def relayed_copies():
        sent = []
        for a in range(n):
            first, second = halves(a)
            sent += [copy(a, 3, x_nbr, y_nbr, rows=first), copy(a, 5, x_nbr, sibling), copy(a, 6, y_nbr, sibling)]
            if second is not None:
                sent.append(copy(a, 4, y_nbr, x_nbr, rows=second))
        return sent

    def start():
        mine, sent = own_copies()
        for cp in mine + sent:
            cp.start()

    def relay():
        for a in range(n):
            first, second = halves(a)
            copy(a, 1, x_nbr, me).wait_recv()
            copy(a, 3, x_nbr, y_nbr, rows=first).start()
            copy(a, 5, x_nbr, sibling).start()
        for a in range(n):
            first, second = halves(a)
            copy(a, 2, y_nbr, me).wait_recv()
            if second is not None:
                copy(a, 4, y_nbr, x_nbr, rows=second).start()
            copy(a, 6, y_nbr, sibling).start()

    def finish():
        mine, sent = own_copies()
        sent += relayed_copies()
        for a in range(n):
            first, second = halves(a)
            copy(a, 3, diag, me, rows=first).wait_recv()
            if second is not None:
                copy(a, 4, diag, me, rows=second).wait_recv()
            last = copy(a, 7, diag, sibling)
            last.start()
            sent.append(last)
        for a in range(n):
            copy(a, 0, other(me), me).wait_recv()
            copy(a, 5, other(x_nbr), me).wait_recv()
            copy(a, 6, other(y_nbr), me).wait_recv()
            copy(a, 7, other(diag), me).wait_recv()
        for cp in sent:
            cp.wait_send()
        for cp in mine:
            cp.wait()

    return start, relay, finish


N_CHIPS = 4


def _exchange_with_sibling(parts, name):
    n = len(parts)

    def body(*refs):
        ins, outs = refs[:n], refs[n:2 * n]
        send_sems, recv_sems = refs[2 * n:]
        x, y, c = lax.axis_index("x"), lax.axis_index("y"), lax.axis_index("c")
        copies = [pltpu.make_async_remote_copy(
            src_ref=ins[a].at[2 * q + (1 - c)], dst_ref=outs[a].at[q], send_sem=send_sems.at[a * N_CHIPS + q],
            recv_sem=recv_sems.at[a * N_CHIPS + q], device_id=(x, y, 1 - c), device_id_type=_MESH_ID)
            for a in range(n) for q in range(N_CHIPS)]
        for cp in copies:
            cp.start()
        for cp in copies:
            cp.wait()

    return pl.pallas_call(
        body, out_shape=[_sds((N_CHIPS,) + p.shape[1:], p.dtype) for p in parts], in_specs=[_ANY] * n, out_specs=[_ANY] * n,
        scratch_shapes=[pltpu.SemaphoreType.DMA((n * N_CHIPS,)), pltpu.SemaphoreType.DMA((n * N_CHIPS,))], name=name,
    )(*parts)


def _pair_sum(part, recv, core, out_dtype, name):
    _, r, cdim = recv.shape
    tr = max(8, min(r, STREAM_BLOCK_ELEMENTS // cdim))
    assert r % tr == 0, (name, r, tr)

    def body(core_ref, p_ref, r_ref, o_ref):
        o_ref[...] = (p_ref[...] + r_ref[...]).astype(o_ref.dtype)

    grid_spec = pltpu.PrefetchScalarGridSpec(
        num_scalar_prefetch=1, grid=(N_CHIPS, r // tr),
        in_specs=[pl.BlockSpec((None, None, tr, cdim), lambda q, i, core_ref: (q, core_ref[0], i, 0)),
                  pl.BlockSpec((None, tr, cdim), lambda q, i, core_ref: (q, i, 0))],
        out_specs=pl.BlockSpec((None, tr, cdim), lambda q, i, core_ref: (q, i, 0)))
    return pl.pallas_call(
        body, grid_spec=grid_spec, out_shape=_sds((N_CHIPS, r, cdim), out_dtype),
        compiler_params=_params(("parallel", "parallel")), name=name,
    )(core, part.reshape(N_CHIPS, 2, r, cdim), recv)


def _exchange_between_chips(parts, name):
    n = len(parts)

    def body(*refs):
        start, finish = _chip_exchange_plan(refs[:n], refs[n:2 * n], *refs[2 * n:])
        start()
        finish()

    return pl.pallas_call(
        body, out_shape=[_sds(p.shape, p.dtype) for p in parts], in_specs=[_ANY] * n, out_specs=[_ANY] * n,
        scratch_shapes=_chip_exchange_semaphores(n), name=name,
    )(*parts)


def _chip_exchange_semaphores(n):
    n_other = N_CHIPS - 1
    return [pltpu.SemaphoreType.DMA((n * n_other,)), pltpu.SemaphoreType.DMA((n * n_other,)), pltpu.SemaphoreType.DMA((n,))]


def _chip_exchange_plan(ins, outs, send_sems, recv_sems, local_sems):
    n = len(ins)
    n_other = N_CHIPS - 1
    x, y, c = lax.axis_index("x"), lax.axis_index("y"), lax.axis_index("c")
    my_chip = 2 * x + y

    def all_copies():
        mine = [pltpu.make_async_copy(ins[a].at[my_chip], outs[a].at[my_chip], local_sems.at[a]) for a in range(n)]
        remote = []
        for j, (fx, fy) in enumerate([(1, 0), (0, 1), (1, 1)]):
            px, py = (1 - x if fx else x), (1 - y if fy else y)
            for a in range(n):
                remote.append(pltpu.make_async_remote_copy(
                    src_ref=ins[a].at[2 * px + py], dst_ref=outs[a].at[my_chip], send_sem=send_sems.at[a * n_other + j],
                    recv_sem=recv_sems.at[a * n_other + j], device_id=(px, py, c), device_id_type=_MESH_ID))
        return mine, remote

    def start():
        mine, remote = all_copies()
        for cp in mine + remote:
            cp.start()

    def finish():
        mine, remote = all_copies()
        for cp in remote + mine:
            cp.wait()

    return start, finish


def _chip_sums(parts, names, tag):
    from_sibling = _exchange_with_sibling(parts, name="scatter_grads_sibling_" + tag)
    core = lax.axis_index("c").astype(jnp.int32).reshape(1)
    return [_pair_sum(p, r, core, F32 if nm.startswith("pack") else BF16, name="pair_sum_" + nm)
            for p, r, nm in zip(parts, from_sibling, names)]


def _adamw(w, m, v, slots, name, layer=0, n_layers=1, into=None):
    r, c = w.shape[-2:]
    ns = slots.shape[0]
    tr = max(8, min(r, STREAM_BLOCK_ELEMENTS // c))
    assert r % tr == 0, (name, r, tr)

    def body(w_ref, m_ref, v_ref, g_ref, *rest):
        g_out, d_out, m_out, v_out = rest[-4:]
        g = g_ref[0].astype(F32)
        for s in range(1, ns):
            g = g + g_ref[s].astype(F32)
        m_new = ADAM_B1 * m_ref[...] + (1.0 - ADAM_B1) * g
        v_new = ADAM_B2 * v_ref[...] + (1.0 - ADAM_B2) * jnp.square(g)
        m_hat = m_new / (1.0 - ADAM_B1 ** ADAM_STEP)
        v_hat = v_new / (1.0 - ADAM_B2 ** ADAM_STEP)
        d_out[...] = -ADAM_LR * (m_hat / (jnp.sqrt(v_hat) + ADAM_EPS) + ADAM_WD * w_ref[...])
        g_out[...], m_out[...], v_out[...] = g, m_new, v_new

    if n_layers == 1:
        spec, out_shape, earlier, aliases = pl.BlockSpec((tr, c), lambda i: (i, 0)), _sds((r, c)), [], {}
    else:
        spec, out_shape = pl.BlockSpec((None, tr, c), lambda i: (layer, i, 0)), _sds((n_layers, r, c))
        earlier = list(into) if into is not None else []
        aliases = {4 + i: i for i in range(len(earlier))}
    in_specs = [spec, spec, spec, pl.BlockSpec((ns, tr, c), lambda i: (0, i, 0))]
    return pl.pallas_call(
        body, grid=(r // tr,), in_specs=in_specs + [_ANY] * len(earlier), out_specs=[spec] * 4, out_shape=[out_shape] * 4,
        input_output_aliases=aliases, compiler_params=_params(("parallel",)), name=name,
    )(w, m, v, slots, *earlier)


_COL_VECTORS = ("a_norm", "a_mix", "a_w0", "a_a0", "a_k_k", "a_k_a", "a_ln_x_w", "a_ln_x_b")
_COL_VEC_ROWS = 16
_COL_ROWS = _COL_VEC_ROWS + 2 * LORA_PAD + 256
_ROW_COLS = 2 * LORA_PAD + 256 + 512
_REPL_ROWS = 8


def _pad_to(a, size, axis):
    widths = [(0, 0)] * a.ndim
    widths[axis] = (0, size - a.shape[axis])
    return jnp.pad(a, widths)


def _pack_cols(p):
    width = p["a_norm"].shape[-1]
    vecs = jnp.concatenate([p[n].reshape(-1, width) for n in _COL_VECTORS], axis=0)
    return jnp.concatenate([_pad_to(vecs, _COL_VEC_ROWS, 0), _pad_to(p["a_w2"].reshape(-1, width), LORA_PAD, 0),
                            _pad_to(p["a_a2"].reshape(-1, width), LORA_PAD, 0), p["a_g2"].reshape(-1, width)], axis=0)


def _unpack_cols(a, lead):
    width = a.shape[-1]
    out, row = {}, 0
    for n in _COL_VECTORS:
        k = 6 if n == "a_mix" else 1
        out[n] = a[row:row + k].reshape(lead + ((6, width) if n == "a_mix" else (width,)))
        row += k
    base = _COL_VEC_ROWS
    out["a_w2"] = a[base:base + 96].reshape(lead + (96, width))
    out["a_a2"] = a[base + LORA_PAD:base + LORA_PAD + 96].reshape(lead + (96, width))
    out["a_g2"] = a[base + 2 * LORA_PAD:].reshape(lead + (256, width))
    return out


def _pack_rows(p):
    rows = p["w_kv"].shape[0]
    return jnp.concatenate([_pad_to(p["a_w1"].reshape(rows, -1), LORA_PAD, 1), _pad_to(p["a_a1"].reshape(rows, -1), LORA_PAD, 1),
                            p["a_g1"].reshape(rows, -1), p["w_kv"]], axis=1)


def _unpack_rows(a, lead):
    rows = a.shape[0]
    return {"a_w1": a[:, :96].reshape(lead + (rows, 96)), "a_a1": a[:, LORA_PAD:LORA_PAD + 96].reshape(lead + (rows, 96)),
            "a_g1": a[:, 2 * LORA_PAD:2 * LORA_PAD + 256].reshape(lead + (rows, 256)), "w_kv": a[:, 2 * LORA_PAD + 256:]}


def _pack_repl(p):
    row = lambda a: _pad_to(a.reshape(1, -1), D_MODEL, 1)
    return jnp.concatenate([p["mlp_norm"].reshape(2, D_MODEL), row(p["kv_norm"]), row(p["b_norm"]), row(p["a_r_k"]),
                            row(p["k_norm"]), row(p["b_q_norm"]), row(p["b_sinks"])], axis=0)


def _unpack_repl(a):
    return {"mlp_norm": a[0:2], "kv_norm": a[2], "b_norm": a[3:4], "a_r_k": a[4].reshape(1, N_HEADS, HEAD_DIM),
            "k_norm": a[5, :HEAD_DIM], "b_q_norm": a[6:7, :HEAD_DIM], "b_sinks": a[7:8, :N_HEADS]}


_WEIGHTS = ("a_norm", "a_mix", "a_w_rkv", "a_w0", "a_w1", "a_w2", "a_a0", "a_a1", "a_a2", "a_g1", "a_g2", "a_k_k", "a_k_a",
            "a_r_k", "a_ln_x_w", "a_ln_x_b", "a_w_out", "mlp_norm", "mlp_w_up", "mlp_w_down", "kv_norm", "w_kv", "k_norm",
            "b_norm", "b_w_q", "b_q_norm", "b_sinks", "b_w_o")


def _big_shards(p):
    return [p["a_w_rkv"][0, 0], p["a_w_rkv"][0, 1], p["a_w_rkv"][0, 2], p["a_w_out"][0], p["mlp_w_up"][0], p["mlp_w_up"][1],
            p["mlp_w_down"][0], p["mlp_w_down"][1], p["b_w_q"][0], p["b_w_o"][0]]


_BIG_NAMES = ("a_w_r", "a_w_k", "a_w_v", "a_w_out", "mlp_w_up0", "mlp_w_up1", "mlp_w_down0", "mlp_w_down1", "b_w_q", "b_w_o")


def kernel(x, a_norm, a_mix, a_w_rkv, a_w0, a_w1, a_w2, a_a0, a_a1, a_a2, a_g1, a_g2, a_k_k, a_k_a, a_r_k, a_ln_x_w,
           a_ln_x_b, a_w_out, mlp_norm, mlp_w_up, mlp_w_down, kv_norm, w_kv, k_norm, b_norm, b_w_q, b_q_norm, b_sinks,
           b_w_o, loss_target, m_a_norm, m_a_mix, m_a_w_rkv, m_a_w0, m_a_w1, m_a_w2, m_a_a0, m_a_a1, m_a_a2, m_a_g1,
           m_a_g2, m_a_k_k, m_a_k_a, m_a_r_k, m_a_ln_x_w, m_a_ln_x_b, m_a_w_out, m_mlp_norm, m_mlp_w_up, m_mlp_w_down,
           m_kv_norm, m_w_kv, m_k_norm, m_b_norm, m_b_w_q, m_b_q_norm, m_b_sinks, m_b_w_o, v_a_norm, v_a_mix, v_a_w_rkv,
           v_a_w0, v_a_w1, v_a_w2, v_a_a0, v_a_a1, v_a_a2, v_a_g1, v_a_g2, v_a_k_k, v_a_k_a, v_a_r_k, v_a_ln_x_w,
           v_a_ln_x_b, v_a_w_out, v_mlp_norm, v_mlp_w_up, v_mlp_w_down, v_kv_norm, v_w_kv, v_k_norm, v_b_norm, v_b_w_q,
           v_b_q_norm, v_b_sinks, v_b_w_o):
    given = locals()
    wts = {n: given[n] for n in _WEIGHTS}
    mom = {n: given["m_" + n] for n in _WEIGHTS}
    var = {n: given["v_" + n] for n in _WEIGHTS}

    cols_w, rows_w, repl_w = _pack_cols(wts), _pack_rows(wts), _pack_repl(wts)
    big_w = _big_shards(wts)
    big_bf16 = dict(zip(_BIG_NAMES, [b.astype(BF16) for b in big_w]))
    first_names = [k for k in _BIG_NAMES if k not in _LATE_GATHER]
    lora_in = rows_w[:, :2 * LORA_PAD]
    gathered = _all_gather([cols_w, lora_in] + [big_bf16[k] for k in first_names], name="gather_weights")
    full_cols = gathered[0].transpose(1, 0, 2).reshape(_COL_ROWS, D_MODEL)
    full_lora_in = gathered[1].reshape(D_MODEL, 2 * LORA_PAD)
    w = {}
    w.update({k: v.reshape(v.shape[1:]) for k, v in _unpack_cols(full_cols, (1,)).items()})
    for k in ("a_norm", "a_w0", "a_a0", "a_k_k", "a_k_a", "a_ln_x_w", "a_ln_x_b"):
        w[k] = w[k].reshape(1, D_MODEL)
    for k in ("a_w2", "a_a2"):
        w[k] = _pad_to(w[k], LORA_PAD, 0)
    w["a_w1"], w["a_a1"] = full_lora_in[:, :LORA_PAD], full_lora_in[:, LORA_PAD:]
    for k, arr in zip(first_names, gathered[2:]):
        w[k] = arr.reshape(N_DEV * arr.shape[1], arr.shape[2])
    late_shards = [big_bf16[k] for k in _LATE_WEIGHTS] + [big_bf16["a_w_out"], a_g1[0], w_kv]
    w["mlp_norm0"], w["mlp_norm1"] = mlp_norm[0:1], mlp_norm[1:2]
    w["kv_norm"], w["k_norm"] = kv_norm.reshape(1, D_MODEL), k_norm.reshape(1, HEAD_DIM)
    w["b_norm"], w["b_q_norm"], w["b_sinks"], w["a_r_k"] = b_norm, b_q_norm, b_sinks, a_r_k.reshape(1, D_MODEL)

    loss_local, grad_x, g, big_reduced = _local_step(x[0], loss_target[0], w, late_shards)
    loss = lax.psum(loss_local[0, 0], MESH_AXES)

    g_lead = {k: g[k][None] for k in ("a_norm", "a_mix", "a_w0", "a_a0", "a_k_k", "a_k_a", "a_ln_x_w", "a_ln_x_b", "a_g2")}
    g_lead["a_w2"], g_lead["a_a2"] = g["a_w2"][None, :96], g["a_a2"][None, :96]
    g_cols = _pack_cols(g_lead).reshape(_COL_ROWS, N_DEV, D_MODEL // N_DEV).transpose(1, 0, 2)
    g_rows = _pack_rows({"a_w1": g["a_w1"][:, :96], "a_a1": g["a_a1"][:, :96], "a_g1": g["a_g1"], "w_kv": g["w_kv"]})
    g_rows = g_rows.reshape(N_DEV, D_MODEL // N_DEV, _ROW_COLS)
    pack_sums = _chip_sums([g_cols, g_rows], ("pack_cols", "pack_rows"), "late")
    reduced = list(_exchange_between_chips(pack_sums, name="scatter_grads_chips_late")) + [big_reduced[k] for k in _BIG_NAMES]
    g_repl = _pack_repl({"mlp_norm": jnp.concatenate([g["mlp_norm0"], g["mlp_norm1"]], axis=0), "kv_norm": g["kv_norm"],
                         "b_norm": g["b_norm"], "a_r_k": g["a_r_k"], "k_norm": g["k_norm"], "b_q_norm": g["b_q_norm"],
                         "b_sinks": g["b_sinks"]})
    (repl_slots,) = _all_gather([g_repl], name="gather_replicated_grads")

    res = {}
    cols4 = _adamw(cols_w, _pack_cols(mom), _pack_cols(var), reduced[0], name="adamw_cols")
    rows4 = _adamw(rows_w, _pack_rows(mom), _pack_rows(var), reduced[1], name="adamw_rows")
    repl4 = _adamw(repl_w, _pack_repl(mom), _pack_repl(var), repl_slots, name="adamw_replicated")
    for unpacked in ([_unpack_cols(a, (1,)) for a in cols4], [_unpack_rows(a, (1,)) for a in rows4], [_unpack_repl(a) for a in repl4]):
        for k in unpacked[0]:
            res[k] = tuple(u[k] for u in unpacked)
    stacked = {"a_w_r": ("a_w_rkv", 0), "a_w_k": ("a_w_rkv", 1), "a_w_v": ("a_w_rkv", 2), "mlp_w_up0": ("mlp_w_up", 0),
               "mlp_w_up1": ("mlp_w_up", 1), "mlp_w_down0": ("mlp_w_down", 0), "mlp_w_down1": ("mlp_w_down", 1)}
    big4, earlier = {}, None
    for k, bw, bm, bv, slots in zip(_BIG_NAMES, big_w, _big_shards(mom), _big_shards(var), reduced[2:]):
        if k in stacked:
            src, layer = stacked[k]
            as_layers = lambda a: a.reshape((-1,) + a.shape[-2:])
            earlier = _adamw(as_layers(wts[src]), as_layers(mom[src]), as_layers(var[src]), slots, name="adamw_" + k,
                             layer=layer, n_layers=as_layers(wts[src]).shape[0], into=earlier if layer else None)
        else:
            earlier = _adamw(bw, bm, bv, slots, name="adamw_" + k)
        big4[k] = earlier
    res["a_w_rkv"] = tuple(a[None] for a in big4["a_w_v"])
    res["a_w_out"] = tuple(a[None] for a in big4["a_w_out"])
    res["mlp_w_up"], res["mlp_w_down"] = tuple(big4["mlp_w_up1"]), tuple(big4["mlp_w_down1"])
    res["b_w_q"] = tuple(a[None] for a in big4["b_w_q"])
    res["b_w_o"] = tuple(a[None] for a in big4["b_w_o"])
    res["w_kv"] = tuple(a.reshape(w_kv.shape) for a in res["w_kv"])

    outs = [loss, grad_x[None]]
    for i in range(4):
        outs += [res[n][i].reshape(given[n].shape) for n in _WEIGHTS]
    return tuple(outs)
```

```python
import functools
import math

import jax
import jax.numpy as jnp
from jax import lax
from jax.experimental import pallas as pl
from jax.experimental.pallas import tpu as pltpu

F32 = jnp.float32
BF16 = jnp.bfloat16

D_MODEL = 2048
N_HEADS = 32
HEAD_DIM = 64
N_KV_HEADS = 4
Q_PER_KV = 8
ATT_BLOCK = 128
WKV_CHUNK = 64
LORA_PAD = 128
D_FF = 8192
N_DEV = 8
RMS_EPS = 1e-6
GN_EPS = 64e-5
L2_EPS = 1e-12
ROPE_THETA = 10000.0
ADAM_LR, ADAM_B1, ADAM_B2, ADAM_EPS, ADAM_WD, ADAM_STEP = 0.001, 0.9, 0.999, 1e-08, 0.01, 10
MASK_VALUE = -1e30
VMEM_LIMIT_BYTES = 56 * 1024 * 1024
MM_TILE_M, MM_TILE_N, MM_TILE_K = 1024, 1024, 2048
STREAM_BLOCK_ELEMENTS = 1 << 18
BF16_SUBLANES = 16
MESH_AXES = ("x", "y", "c")

_NN = (((1,), (0,)), ((), ()))
_NT = (((1,), (1,)), ((), ()))
_TN = (((0,), (0,)), ((), ()))
_BNN = (((2,), (1,)), ((0,), (0,)))
_BNT = (((2,), (2,)), ((0,), (0,)))
_BTN = (((1,), (1,)), ((0,), (0,)))


def _params(sem):
    return pltpu.CompilerParams(dimension_semantics=sem, vmem_limit_bytes=VMEM_LIMIT_BYTES)


def _split2(a):
    hi = a.astype(BF16)
    return hi, (a - hi.astype(F32)).astype(BF16)


def _dot3(a, b, dims):
    ah, al = _split2(a)
    bh, bl = _split2(b)
    (ca,), (cb,) = dims[0]
    return lax.dot_general(jnp.concatenate([ah, al, ah], axis=ca), jnp.concatenate([bh, bh, bl], axis=cb), dims,
                           preferred_element_type=F32)


@functools.partial(jax.custom_vjp, nondiff_argnums=(2,))
def _hdot(a, b, dims=_NN):
    return _dot3(a, b, dims)


def _hdot_fwd(a, b, dims):
    return _dot3(a, b, dims), (a, b)


def _hdot_bwd(dims, res, g):
    a, b = res
    nn, nt, tn = (_NN, _NT, _TN) if dims in (_NN, _NT, _TN) else (_BNN, _BNT, _BTN)
    if dims == nn:
        return _dot3(g, b, nt), _dot3(a, g, tn)
    if dims == nt:
        return _dot3(g, b, nn), _dot3(g, a, tn)
    assert dims == tn
    return _dot3(b, g, nt), _dot3(a, g, nn)


_hdot.defvjp(_hdot_fwd, _hdot_bwd)


def _tri_parts(x):
    hi = x.astype(BF16)
    r1 = x - hi.astype(F32)
    mid = r1.astype(BF16)
    return hi, mid, (r1 - mid.astype(F32)).astype(BF16)


@jax.custom_vjp
def _mask_dot(mask, x):
    mb = mask.astype(BF16)
    return lax.dot_general(jnp.concatenate([mb, mb, mb], axis=2), jnp.concatenate(_tri_parts(x), axis=1), _BNN,
                           preferred_element_type=F32)


def _mask_dot_fwd(mask, x):
    return _mask_dot(mask, x), mask


def _mask_dot_bwd(mask, g):
    mb = mask.astype(BF16)
    return jnp.zeros_like(mask), lax.dot_general(jnp.concatenate([mb, mb, mb], axis=1), jnp.concatenate(_tri_parts(g), axis=1),
                                                 _BTN, preferred_element_type=F32)


_mask_dot.defvjp(_mask_dot_fwd, _mask_dot_bwd)


def _b16dot(a, b, dims):
    return lax.dot_general(a.astype(BF16), b.astype(BF16), dims, preferred_element_type=F32)


@jax.custom_vjp
def _bdot(a, b):
    return _b16dot(a, b, _NN)


def _bdot_fwd(a, b):
    return _b16dot(a, b, _NN), (a, b)


def _bdot_bwd(res, g):
    a, b = res
    return _b16dot(g, b, _NT), _b16dot(a, g, _TN)


_bdot.defvjp(_bdot_fwd, _bdot_bwd)


@jax.custom_vjp
def _bdot_nt(a, b):
    return _b16dot(a, b, _NT)


def _bdot_nt_fwd(a, b):
    return _b16dot(a, b, _NT), (a, b)


def _bdot_nt_bwd(res, g):
    a, b = res
    return _b16dot(g, b, _NN), _b16dot(g, a, _TN)


_bdot_nt.defvjp(_bdot_nt_fwd, _bdot_nt_bwd)


def _rms(x, gain):
    return x * lax.rsqrt(jnp.mean(x * x, axis=-1, keepdims=True) + RMS_EPS) * gain


def _vjp_of(f, n_in, diff):
    def g(*args):
        ins, cts = args[:n_in], args[n_in:]

        def fd(*d):
            full = list(ins)
            for pos, i in enumerate(diff):
                full[i] = d[pos]
            return f(*full)

        _, pull = jax.vjp(fd, *[ins[i] for i in diff])
        return pull(tuple(cts))
    return g


def _mm(a, b, *, name, ta=False, tb=False, a_pro=None, epi=None, epi_args=(), out_dtype=F32,
        tm=MM_TILE_M, tn=MM_TILE_N, tk=MM_TILE_K, dims=None, b_spec=None, o_spec=None, o_shape=None):
    if dims is None:
        m, k = (a.shape[1], a.shape[0]) if ta else a.shape
        n = b.shape[0] if tb else b.shape[1]
    else:
        m, n, k = dims
    tm, tn, tk = min(tm, m), min(tn, n), min(tk, k)
    assert m % tm == 0 and n % tn == 0 and k % tk == 0, (name, m, n, k, tm, tn, tk)
    nk = k // tk
    ne = len(epi_args)
    cdims = (((0 if ta else 1,), (1 if tb else 0,)), ((), ()))

    def body(a_ref, b_ref, *rest):
        e_refs, o_ref, acc = rest[:ne], rest[ne], rest[ne + 1]
        kk = pl.program_id(2)

        @pl.when(kk == 0)
        def _():
            acc[...] = jnp.zeros_like(acc)

        av = a_ref[...]
        if a_pro is not None:
            av = a_pro(av.astype(F32))
        acc[...] += lax.dot_general(av.astype(BF16), b_ref[...].astype(BF16), cdims, preferred_element_type=F32)

        @pl.when(kk == nk - 1)
        def _():
            r = acc[...]
            if epi is not None:
                r = epi(r, *[e[...] for e in e_refs])
            o_ref[...] = r.astype(o_ref.dtype)

    a_spec = pl.BlockSpec((tk, tm), lambda i, j, q: (q, i)) if ta else pl.BlockSpec((tm, tk), lambda i, j, q: (i, q))
    if b_spec is None:
        b_spec = pl.BlockSpec((tn, tk), lambda i, j, q: (j, q)) if tb else pl.BlockSpec((tk, tn), lambda i, j, q: (q, j))
    if o_spec is None:
        o_spec = pl.BlockSpec((tm, tn), lambda i, j, q: (i, j))
        o_shape = (m, n)
    e_specs = [pl.BlockSpec((tm, tn), lambda i, j, q: (i, j)) for _ in epi_args]
    return pl.pallas_call(
        body, grid=(m // tm, n // tn, nk), in_specs=[a_spec, b_spec] + e_specs, out_specs=o_spec,
        out_shape=jax.ShapeDtypeStruct(o_shape, out_dtype), scratch_shapes=[pltpu.VMEM((tm, tn), F32)],
        compiler_params=_params(("parallel", "parallel", "arbitrary")), name=name,
    )(a, b, *epi_args)


def _rowwise(fn, rows, params, out_rows, out_params=(), *, tm=256, name, chip_sums=()):
    t = rows[0].shape[0]
    tm = min(tm, t)
    assert t % tm == 0
    nr, npar, nor, nop, ns = len(rows), len(params), len(out_rows), len(out_params), len(chip_sums)
    steps = t // tm

    def body(*refs):
        r, p = refs[:nr], refs[nr:nr + npar]
        o, op = refs[nr + npar + ns:nr + npar + ns + nor], refs[nr + npar + ns + nor:nr + npar + ns + nor + nop]
        if ns:
            start, finish = _chip_exchange_plan(refs[nr + npar:nr + npar + ns], refs[nr + npar + ns + nor + nop:nr + npar + 2 * ns + nor + nop],
                                                *refs[nr + npar + 2 * ns + nor + nop:])
            pl.when(pl.program_id(0) == 0)(start)
        outs = fn(*[x[...] for x in r], *[x[...] for x in p])
        for ref, val in zip(o, outs[:nor]):
            ref[...] = val.astype(ref.dtype)
        if nop:
            @pl.when(pl.program_id(0) == 0)
            def _():
                for ref in op:
                    ref[...] = jnp.zeros_like(ref)

            for ref, val in zip(op, outs[nor:]):
                ref[...] += val.astype(F32)
        if ns:
            pl.when(pl.program_id(0) == steps - 1)(finish)

    in_specs = [pl.BlockSpec((tm, x.shape[1]), lambda i: (i, 0)) for x in rows]
    in_specs += [pl.BlockSpec(p.shape, lambda i: (0, 0)) for p in params]
    out_specs = [pl.BlockSpec((tm, s.shape[1]), lambda i: (i, 0)) for s in out_rows]
    out_specs += [pl.BlockSpec(s.shape, lambda i: (0, 0)) for s in out_params]
    return pl.pallas_call(
        body, grid=(steps,), in_specs=in_specs + [_ANY] * ns, out_specs=out_specs + [_ANY] * ns,
        out_shape=list(out_rows) + list(out_params) + [_sds(c.shape, c.dtype) for c in chip_sums],
        scratch_shapes=_chip_exchange_semaphores(ns) if ns else [],
        compiler_params=_params(("arbitrary",)), name=name,
    )(*rows, *params, *chip_sums)


def _sds(shape, dtype=F32):
    return jax.ShapeDtypeStruct(tuple(shape), dtype)


def _doubling_powers(l):
    powers = [l]
    for _ in range(int(math.log2(l.shape[-1])) - 1):
        powers.append(_dot3(powers[-1], powers[-1], _BNN))
    return powers


@jax.custom_vjp
def _unit_lower_solve(l, z):
    u = z
    for p in _doubling_powers(l):
        u = u + _dot3(p, u, _BNN)
    return u


def _unit_lower_solve_fwd(l, z):
    powers = _doubling_powers(l)
    u = z
    for p in powers:
        u = u + _dot3(p, u, _BNN)
    return u, (powers, u)


def _unit_lower_solve_bwd(res, du):
    powers, u = res
    g = du
    for p in powers:
        g = g + _dot3(p, g, _BTN)
    return _dot3(g, u, _BNT), g


_unit_lower_solve.defvjp(_unit_lower_solve_fwd, _unit_lower_solve_bwd)


def _wkv_chunk(s0, r, lw, k, v, a, b):
    nb, c, _ = r.shape
    ti = lax.broadcasted_iota(jnp.int32, (nb, c, c), 1)
    si = lax.broadcasted_iota(jnp.int32, (nb, c, c), 2)
    incl, strict = si <= ti, si < ti
    cum = _mask_dot(incl.astype(F32), lw)
    tot = jnp.sum(lw, axis=1, keepdims=True)
    rcum = tot - cum
    w_inv = jnp.exp(-cum)
    at, rt, bt, kt = a * jnp.exp(cum - lw), r * jnp.exp(cum), b * w_inv, k * w_inv
    l_ab = jnp.where(strict, _hdot(at, bt, _BNT), 0.0)
    l_ak = jnp.where(strict, _hdot(at, kt, _BNT), 0.0)
    t_rb = jnp.where(incl, _hdot(rt, bt, _BNT), 0.0)
    t_rk = jnp.where(incl, _hdot(rt, kt, _BNT), 0.0)
    u = _unit_lower_solve(l_ab, _hdot(at, s0, _BNT) + _hdot(l_ak, v, _BNN))
    y = _hdot(rt, s0, _BNT) + _hdot(t_rb, u, _BNN) + _hdot(t_rk, v, _BNN)
    e = jnp.exp(rcum)
    s1 = s0 * jnp.exp(tot) + _hdot(u, b * e, _BTN) + _hdot(v, k * e, _BTN)
    return y, s1


WKV_HEADS_PER_STEP = 16
GATHER_RELAY_AT = 0.625


def _first_and_last_step(grid):
    i, j = pl.program_id(0), pl.program_id(1)
    return jnp.logical_and(i == 0, j == 0), jnp.logical_and(i == grid[0] - 1, j == grid[1] - 1)


N_WKV_PARAMS = 7


def _tmix_chunk(s0, r, k, v, wl, al, w0, a0, k_k, k_a, ln_w, ln_b, r_k):
    lw, kmod, a, b = _wkv_prep(k, wl, al, w0, a0, k_k, k_a)
    y, s1 = _wkv_chunk(s0, r, lw, kmod, v, a, b)
    (y2,) = _wkv_post(y, r, kmod, v, ln_w, ln_b, r_k)
    return y2, s1


def _split_heads(x, nh):
    return x.reshape(x.shape[0], nh, HEAD_DIM).transpose(1, 0, 2)


def _merge_heads(xh):
    return jnp.concatenate([xh[h] for h in range(xh.shape[0])], axis=1)


def _wkv_fwd(r, k, v, wl, al, params, shards=()):
    t = r.shape[0]
    nh, n = N_HEADS, HEAD_DIM
    nc = t // WKV_CHUNK
    hb = WKV_HEADS_PER_STEP
    grid = (nh // hb, nc)
    ns = len(shards)
    n_in = 5 + N_WKV_PARAMS

    def body(*refs):
        y_ref, s_ref = refs[n_in + ns:n_in + ns + 2]
        state = refs[n_in + 2 * ns + 2]
        if ns:
            start, relay, finish = _gather_plan(refs[n_in:n_in + ns], refs[n_in + ns + 2:n_in + 2 * ns + 2], *refs[n_in + 2 * ns + 3:])
            first, last = _first_and_last_step(grid)
            pl.when(first)(start)
            relay_step = int(GATHER_RELAY_AT * grid[0] * nc)
            pl.when(jnp.logical_and(pl.program_id(0) == relay_step // nc, pl.program_id(1) == relay_step % nc))(relay)

        @pl.when(pl.program_id(1) == 0)
        def _():
            state[...] = jnp.zeros_like(state)

        s0 = state[...]
        s_ref[:, 0] = s0
        rows = [_split_heads(ref[...], hb) for ref in refs[:5]]
        y, s1 = _tmix_chunk(s0, *rows, *[ref[...] for ref in refs[5:n_in]])
        y_ref[...] = _merge_heads(y)
        state[...] = s1
        if ns:
            pl.when(last)(finish)

    blk = pl.BlockSpec((WKV_CHUNK, hb * n), lambda h, c: (c, h))
    pblk = pl.BlockSpec((hb, 1, n), lambda h, c: (h, 0, 0))
    sblk = pl.BlockSpec((hb, 1, n, n), lambda h, c: (h, c, 0, 0))
    outs = pl.pallas_call(
        body, grid=grid, in_specs=[blk] * 5 + [pblk] * N_WKV_PARAMS + [_ANY] * ns, out_specs=[blk, sblk] + [_ANY] * ns,
        out_shape=[_sds((t, nh * n)), _sds((nh, nc, n, n))] + [_sds((N_DEV,) + s.shape, s.dtype) for s in shards],
        scratch_shapes=[pltpu.VMEM((hb, n, n), F32)] + (_gather_semaphores(ns) if ns else []),
        compiler_params=_params(("arbitrary", "arbitrary")), name="wkv_fwd",
    )(r, k, v, wl, al, *params, *shards)
    return outs[0], outs[1], list(outs[2:])


def _wkv_bwd(r, k, v, wl, al, params, states, dy, chip_sums=()):
    t = r.shape[0]
    nh, n = N_HEADS, HEAD_DIM
    nc = t // WKV_CHUNK
    hb = WKV_HEADS_PER_STEP
    grid = (nh // hb, nc)
    ns = len(chip_sums)
    n_in = 5 + N_WKV_PARAMS
    n_out = 5 + N_WKV_PARAMS

    def body(*refs):
        s_ref, dy_ref = refs[n_in:n_in + 2]
        out_refs = refs[n_in + 2 + ns:n_in + 2 + ns + n_out]
        dstate = refs[n_in + 2 + 2 * ns + n_out]
        if ns:
            start, finish = _chip_exchange_plan(refs[n_in + 2:n_in + 2 + ns], refs[n_in + 2 + ns + n_out:n_in + 2 + 2 * ns + n_out],
                                                *refs[n_in + 3 + 2 * ns + n_out:])
            first, last = _first_and_last_step(grid)
            pl.when(first)(start)

        @pl.when(pl.program_id(1) == 0)
        def _():
            dstate[...] = jnp.zeros_like(dstate)
            for ref in out_refs[5:]:
                ref[...] = jnp.zeros_like(ref)

        rows = [_split_heads(ref[...], hb) for ref in refs[:5]]
        _, pull = jax.vjp(_tmix_chunk, s_ref[:, 0], *rows, *[ref[...] for ref in refs[5:n_in]])
        grads = pull((_split_heads(dy_ref[...], hb), dstate[...]))
        dstate[...] = grads[0]
        for ref, val in zip(out_refs[:5], grads[1:6]):
            ref[...] = _merge_heads(val).astype(ref.dtype)
        for ref, val in zip(out_refs[5:], grads[6:]):
            ref[...] += val
        if ns:
            pl.when(last)(finish)

    blk = pl.BlockSpec((WKV_CHUNK, hb * n), lambda h, c: (nc - 1 - c, h))
    pblk = pl.BlockSpec((hb, 1, n), lambda h, c: (h, 0, 0))
    sblk = pl.BlockSpec((hb, 1, n, n), lambda h, c: (h, nc - 1 - c, 0, 0))
    outs = pl.pallas_call(
        body, grid=grid, in_specs=[blk] * 5 + [pblk] * N_WKV_PARAMS + [sblk, blk] + [_ANY] * ns,
        out_specs=[blk] * 5 + [pblk] * N_WKV_PARAMS + [_ANY] * ns,
        out_shape=[_sds((t, nh * n), BF16)] * 5 + [_sds((nh, 1, n))] * N_WKV_PARAMS + [_sds(p.shape, p.dtype) for p in chip_sums],
        scratch_shapes=[pltpu.VMEM((hb, n, n), F32)] + (_chip_exchange_semaphores(ns) if ns else []),
        compiler_params=_params(("arbitrary", "arbitrary")), name="wkv_bwd",
    )(r, k, v, wl, al, *params, states, dy, *chip_sums)
    return outs[:5], outs[5:n_out], list(outs[n_out:])


def _wkv_prep(k, wl, al, w0, a0, k_k, k_a):
    z = -(w0 + wl)
    softplus = jnp.maximum(z, 0.0) + jnp.log1p(jnp.exp(-jnp.abs(z)))
    lw = -jnp.exp(-softplus - 0.5)
    asig = jax.nn.sigmoid(a0 + al)
    kk = k * k_k
    kk = kk / jnp.maximum(jnp.sqrt(jnp.sum(kk * kk, axis=-1, keepdims=True)), L2_EPS)
    kmod = k * (1.0 + (asig - 1.0) * k_a)
    return lw, kmod, -kk, kk * asig


def _wkv_post(y, r, kmod, v, ln_w, ln_b, r_k):
    mu = jnp.mean(y, axis=-1, keepdims=True)
    var = jnp.mean(jnp.square(y - mu), axis=-1, keepdims=True)
    yn = (y - mu) * lax.rsqrt(var + GN_EPS)
    yn = yn * ln_w + ln_b
    return (yn + jnp.sum(r * kmod * r_k, axis=-1, keepdims=True) * v,)


def _attn_group(nonzero_block, q, kc, kp, vc, vp, cos_c, sin_c, cos_p, sin_p, q_gain, k_gain, sinks):
    ri = lax.broadcasted_iota(jnp.int32, (HEAD_DIM, HEAD_DIM), 0)
    ci = lax.broadcasted_iota(jnp.int32, (HEAD_DIM, HEAD_DIM), 1)
    half = HEAD_DIM // 2
    rot = jnp.where(ri == ci + half, -1.0, 0.0) + jnp.where(ri + half == ci, 1.0, 0.0)
    rows = Q_PER_KV * ATT_BLOCK

    def rope(x, cos, sin):
        return x * cos + _hdot(x, rot, _NN) * sin

    kcr = rope(_rms(kc, k_gain), cos_c, sin_c)
    kpr = rope(_rms(kp, k_gain), cos_p, sin_p)
    qn = _rms(q, q_gain)
    qr = qn * cos_c + _hdot(qn.reshape(rows, HEAD_DIM), rot, _NN).reshape(q.shape) * sin_c
    q2 = qr.reshape(rows, HEAD_DIM)
    qi = lax.broadcasted_iota(jnp.int32, (1, ATT_BLOCK, ATT_BLOCK), 1)
    ki = lax.broadcasted_iota(jnp.int32, (1, ATT_BLOCK, ATT_BLOCK), 2)
    mask_c = ki <= qi
    mask_p = jnp.logical_and(ki > qi, nonzero_block)
    lane0 = (lax.broadcasted_iota(jnp.int32, (1, 1, 128), 2) == 0).astype(F32)
    shape3 = (Q_PER_KV, ATT_BLOCK, ATT_BLOCK)
    sc = jnp.where(mask_c, (_bdot_nt(q2, kcr) * (HEAD_DIM ** -0.5)).reshape(shape3), MASK_VALUE)
    sp = jnp.where(mask_p, (_bdot_nt(q2, kpr) * (HEAD_DIM ** -0.5)).reshape(shape3), MASK_VALUE)
    sk = jnp.sum(sinks * lane0, axis=2, keepdims=True)
    mx = jnp.maximum(jnp.maximum(jnp.max(sc, axis=2, keepdims=True), jnp.max(sp, axis=2, keepdims=True)), sk)
    mx = lax.stop_gradient(mx)
    ec, ep = jnp.exp(sc - mx), jnp.exp(sp - mx)
    den = jnp.sum(ec, axis=2, keepdims=True) + jnp.sum(ep, axis=2, keepdims=True) + jnp.exp(sk - mx)
    out = _bdot((ec / den).reshape(rows, ATT_BLOCK), vc) + _bdot((ep / den).reshape(rows, ATT_BLOCK), vp)
    return out.reshape(q.shape)


def _attn_rows(nonzero_block, qs, kcs, kps, vcs, vps, tabs, q_gain, k_gain, sinks):
    return [_attn_group(nonzero_block, qs[g], kcs[g], kps[g], vcs[g], vps[g], *tabs, q_gain, k_gain, sinks[g])
            for g in range(N_KV_HEADS)]


def _attn_operands(q_tile, kvc_tile, kvp_tile, sink_ref):
    q3 = _split_heads(q_tile, N_HEADS)
    kvc, kvp = _split_heads(kvc_tile, 2 * N_KV_HEADS), _split_heads(kvp_tile, 2 * N_KV_HEADS)
    groups = range(N_KV_HEADS)
    qs = [q3[Q_PER_KV * g:Q_PER_KV * (g + 1)] for g in groups]
    sinks = [sink_ref[Q_PER_KV * g:Q_PER_KV * (g + 1)] for g in groups]
    return (qs, [kvc[g] for g in groups], [kvp[g] for g in groups], [kvc[N_KV_HEADS + g] for g in groups],
            [kvp[N_KV_HEADS + g] for g in groups], sinks)


def _attn_specs(t):
    nb = t // ATT_BLOCK
    prev = lambda n: jnp.maximum(n - 1, 0)
    kv_width = 2 * N_KV_HEADS * HEAD_DIM
    q_spec = pl.BlockSpec((ATT_BLOCK, D_MODEL), lambda n: (n, 0))
    kv_c = pl.BlockSpec((ATT_BLOCK, kv_width), lambda n: (n, 0))
    kv_p = pl.BlockSpec((ATT_BLOCK, kv_width), lambda n: (prev(n), 0))
    tab_c = pl.BlockSpec((ATT_BLOCK, HEAD_DIM), lambda n: (n, 0))
    tab_p = pl.BlockSpec((ATT_BLOCK, HEAD_DIM), lambda n: (prev(n), 0))
    gain = pl.BlockSpec((1, HEAD_DIM), lambda n: (0, 0))
    sink = pl.BlockSpec((N_HEADS, 1, 128), lambda n: (0, 0, 0))
    return nb, q_spec, kv_c, kv_p, tab_c, tab_p, gain, sink


def _attn_fwd(q, kv, cos, sin, q_gain, k_gain, sinks):
    t = q.shape[0]
    nb, q_spec, kv_c, kv_p, tab_c, tab_p, gain, sink = _attn_specs(t)

    def body(q_ref, kvc, kvp, cc, sc, cp, sp, qg, kg, sk, o_ref):
        qs, kcs, kps, vcs, vps, sinks_ = _attn_operands(q_ref[...], kvc[...], kvp[...], sk)
        outs = _attn_rows(pl.program_id(0) > 0, qs, kcs, kps, vcs, vps, (cc[...], sc[...], cp[...], sp[...]),
                          qg[...], kg[...], sinks_)
        o_ref[...] = _merge_heads(jnp.concatenate(outs, axis=0)).astype(o_ref.dtype)

    return pl.pallas_call(
        body, grid=(nb,), in_specs=[q_spec, kv_c, kv_p, tab_c, tab_c, tab_p, tab_p, gain, gain, sink],
        out_specs=q_spec, out_shape=_sds(q.shape, BF16), compiler_params=_params(("arbitrary",)), name="attn_fwd",
    )(q, kv, kv, cos, sin, cos, sin, q_gain, k_gain, sinks)


def _attn_bwd(q, kv, cos, sin, q_gain, k_gain, sinks, do):
    t = q.shape[0]
    nb, q_spec, kv_c, kv_p, tab_c, tab_p, gain, sink = _attn_specs(t)

    def body(q_ref, kvc, kvp, cc, sc, cp, sp, qg, kg, sk, do_ref, dq_ref, dkvc_ref, dkvp_ref, dqg_ref, dkg_ref, dsk_ref):
        nonzero = pl.program_id(0) > 0
        tabs = (cc[...], sc[...], cp[...], sp[...])
        qs, kcs, kps, vcs, vps, sinks_ = _attn_operands(q_ref[...], kvc[...], kvp[...], sk)

        def f(qs_, kcs_, kps_, vcs_, vps_, qgv, kgv, sks):
            return _attn_rows(nonzero, qs_, kcs_, kps_, vcs_, vps_, tabs, qgv, kgv, sks)

        _, pull = jax.vjp(f, qs, kcs, kps, vcs, vps, qg[...], kg[...], sinks_)
        do3 = _split_heads(do_ref[...], N_HEADS)
        dqs, dkcs, dkps, dvcs, dvps, dqg, dkg, dsks = pull([do3[Q_PER_KV * g:Q_PER_KV * (g + 1)] for g in range(N_KV_HEADS)])
        dq_ref[...] = _merge_heads(jnp.concatenate(dqs, axis=0)).astype(dq_ref.dtype)
        dkvc_ref[...] = jnp.concatenate(dkcs + dvcs, axis=1)
        dkvp_ref[...] = jnp.concatenate(dkps + dvps, axis=1)

        @pl.when(pl.program_id(0) == 0)
        def _():
            dqg_ref[...] = jnp.zeros_like(dqg_ref)
            dkg_ref[...] = jnp.zeros_like(dkg_ref)
            dsk_ref[...] = jnp.zeros_like(dsk_ref)

        dqg_ref[...] += dqg
        dkg_ref[...] += dkg
        for g in range(N_KV_HEADS):
            dsk_ref[Q_PER_KV * g:Q_PER_KV * (g + 1)] += dsks[g]

    return pl.pallas_call(
        body, grid=(nb,), in_specs=[q_spec, kv_c, kv_p, tab_c, tab_c, tab_p, tab_p, gain, gain, sink, q_spec],
        out_specs=[q_spec, kv_c, kv_c, gain, gain, sink],
        out_shape=[_sds(q.shape, BF16), _sds(kv.shape), _sds(kv.shape), _sds((1, HEAD_DIM)), _sds((1, HEAD_DIM)), _sds(sinks.shape)],
        compiler_params=_params(("arbitrary",)), name="attn_bwd",
    )(q, kv, kv, cos, sin, cos, sin, q_gain, k_gain, sinks, do)


def _time_shift_lerps(x, xs, gain, *mix):
    xn, xsn = _rms(x, gain), _rms(xs, gain)
    xx = xsn - xn
    return tuple(xn + xx * m for m in mix)


def _residual_norm(h, delta, gain):
    hn = h + delta
    return hn, _rms(hn, gain)


def _residual_norm2(h, delta, gain_a, gain_b):
    hn = h + delta
    return hn, _rms(hn, gain_a), _rms(hn, gain_b)


def _relu2(u):
    return jnp.square(jnp.maximum(u, 0.0))


def _sigmoid(z):
    return jax.nn.sigmoid(z)


def _shift_down(x):
    return jnp.pad(x[:-1], ((1, 0), (0, 0)))


def _shift_up(x):
    return jnp.pad(x[1:], ((0, 1), (0, 0)))


def _rope_tables(t):
    half = HEAD_DIM // 2
    inv_freq = jnp.power(ROPE_THETA, -jnp.arange(half, dtype=F32) / half)
    ang = jnp.arange(t, dtype=jnp.int32).astype(F32)[:, None] * inv_freq[None, :]
    cos, sin = jnp.cos(ang), jnp.sin(ang)
    return jnp.concatenate([cos, cos], axis=1), jnp.concatenate([sin, sin], axis=1)


def _mlp_fwd(hn, w_up, w_down, layer, up_dev_major):
    t = hn.shape[0]
    if up_dev_major:
        cw = w_up.shape[2]
        u = _mm(hn, w_up, name=f"mlp{layer}_up", dims=(t, D_FF, D_MODEL), tn=cw, tk=D_MODEL,
                b_spec=pl.BlockSpec((None, D_MODEL, cw), lambda i, j, q: (j, q, 0)))
    else:
        u = _mm(hn, w_up, name=f"mlp{layer}_up")
    out = _mm(u, w_down, a_pro=_relu2, name=f"mlp{layer}_down")
    return u, out


def _mm_pair_reduce(a, b, *, device_axis, tile, a_pro=None, name):
    t = a.shape[0]
    tr, tc = tile
    other = (b.shape[1] // tc) if device_axis == "m" else (a.shape[1] // tr)
    n_tiles = N_CHIPS * other
    core = lax.axis_index("c").astype(jnp.int32).reshape(1)
    shape = (N_CHIPS, tr, other * tc) if device_axis == "m" else (N_CHIPS, other * tr, tc)

    def body(core_ref, a_ref, b_ref, out_ref, recv_hbm, send_buf, recv_tile, send_sems, recv_sems, local_sem):
        phase, q, o = pl.program_id(0), pl.program_id(1), pl.program_id(2)
        sibling = (lax.axis_index("x"), lax.axis_index("y"), 1 - core_ref[0])
        tile_no = q * other + o
        slot = lax.rem(tile_no, 2)

        def send(number, buffer):
            return pltpu.make_async_remote_copy(
                src_ref=send_buf.at[buffer], dst_ref=recv_hbm.at[number // other, lax.rem(number, other)],
                send_sem=send_sems.at[number], recv_sem=recv_sems.at[number], device_id=sibling, device_id_type=_MESH_ID)

        def landed():
            return pltpu.make_async_copy(recv_hbm.at[q, o], recv_tile, local_sem)

        @pl.when(phase == 1)
        def _():
            @pl.when(tile_no == 0)
            def _():
                send(n_tiles - 2, lax.rem(n_tiles - 2, 2)).wait_send()
                send(n_tiles - 1, lax.rem(n_tiles - 1, 2)).wait_send()

            send(tile_no, slot).wait_recv()
            landed().start()

        av = a_ref[...]
        if a_pro is not None:
            av = a_pro(av.astype(F32))
        acc = lax.dot_general(av.astype(BF16), b_ref[...].astype(BF16), _TN, preferred_element_type=F32)

        @pl.when(phase == 0)
        def _():
            @pl.when(tile_no >= 2)
            def _():
                send(tile_no - 2, slot).wait_send()

            send_buf[slot] = acc.astype(send_buf.dtype)
            send(tile_no, slot).start()

        @pl.when(phase == 1)
        def _():
            landed().wait()
            out_ref[...] = (acc + recv_tile[...].astype(F32)).astype(out_ref.dtype)

    def device(phase, q, core_ref):
        return 2 * q + jnp.where(phase == 0, 1 - core_ref[0], core_ref[0])

    parked = lambda p, i: jnp.where(p == 0, 0, i)
    if device_axis == "m":
        a_spec = pl.BlockSpec((t, tr), lambda p, q, o, c: (0, device(p, q, c)))
        b_spec = pl.BlockSpec((t, tc), lambda p, q, o, c: (0, o))
        o_spec = pl.BlockSpec((None, tr, tc), lambda p, q, o, c: (parked(p, q), 0, parked(p, o)))
    else:
        a_spec = pl.BlockSpec((t, tr), lambda p, q, o, c: (0, o))
        b_spec = pl.BlockSpec((t, tc), lambda p, q, o, c: (0, device(p, q, c)))
        o_spec = pl.BlockSpec((None, tr, tc), lambda p, q, o, c: (parked(p, q), parked(p, o), 0))
    grid_spec = pltpu.PrefetchScalarGridSpec(
        num_scalar_prefetch=1, grid=(2, N_CHIPS, other), in_specs=[a_spec, b_spec], out_specs=[o_spec, _ANY],
        scratch_shapes=[pltpu.VMEM((2, tr, tc), BF16), pltpu.VMEM((tr, tc), BF16),
                        pltpu.SemaphoreType.DMA((n_tiles,)), pltpu.SemaphoreType.DMA((n_tiles,)), pltpu.SemaphoreType.DMA(())])
    sums, _ = pl.pallas_call(
        body, grid_spec=grid_spec, out_shape=[_sds(shape, BF16), _sds((N_CHIPS, other, tr, tc), BF16)],
        compiler_params=_params(("arbitrary", "arbitrary", "arbitrary")), name=name,
    )(core, a, b)
    return sums


MLP_PAIR_TILE = (1024, 1024)
SQUARE_PAIR_TILE = (D_MODEL // N_DEV, D_MODEL)


def _mlp_bwd(hn, u, dh, w_up, w_down, layer, up_dev_major, pair_reduce=False):
    t = hn.shape[0]
    du = _mm(dh, w_down, tb=True, epi=lambda r, uu: r * (2.0 * jnp.maximum(uu, 0.0)), epi_args=(u,), out_dtype=BF16,
             name=f"mlp{layer}_du")
    if pair_reduce:
        cw = w_up.shape[2]
        d_down = _mm_pair_reduce(u, dh, device_axis="m", tile=MLP_PAIR_TILE, a_pro=_relu2, name=f"mlp{layer}_ddown")
        d_up = _mm_pair_reduce(hn, du, device_axis="n", tile=MLP_PAIR_TILE, name=f"mlp{layer}_dup")
        dhn = _mm(du, w_up, tb=True, name=f"mlp{layer}_dhn", dims=(t, D_MODEL, D_FF), tk=cw,
                  b_spec=pl.BlockSpec((None, MM_TILE_N, cw), lambda i, j, q: (q, j, 0)))
        return dhn, d_up, d_down
    d_down = _mm(u, dh, ta=True, a_pro=_relu2, name=f"mlp{layer}_ddown")
    if up_dev_major:
        cw = w_up.shape[2]
        d_up = _mm(hn, du, ta=True, name=f"mlp{layer}_dup", dims=(D_MODEL, D_FF, t), tn=cw,
                   o_spec=pl.BlockSpec((None, MM_TILE_M, cw), lambda i, j, q: (j, i, 0)), o_shape=(N_DEV, D_MODEL, cw))
        dhn = _mm(du, w_up, tb=True, name=f"mlp{layer}_dhn", dims=(t, D_MODEL, D_FF), tk=cw,
                  b_spec=pl.BlockSpec((None, MM_TILE_N, cw), lambda i, j, q: (q, j, 0)))
    else:
        d_up = _mm(hn, du, ta=True, name=f"mlp{layer}_dup")
        dhn = _mm(du, w_up, tb=True, name=f"mlp{layer}_dhn")
    return dhn, d_up, d_down


_LATE_WEIGHTS = ("mlp_w_up0", "mlp_w_up1", "mlp_w_down0", "mlp_w_down1", "b_w_q", "b_w_o")
_LATE_GATHER = _LATE_WEIGHTS + ("a_w_out", "a_g1", "w_kv")
_EARLY_GRADS = _LATE_WEIGHTS + ("a_w_out",)
_RKV_GRADS = ("a_w_r", "a_w_k", "a_w_v")


def _local_step(x, target, w, late_shards=None, up_dev_major=True):
    t = x.shape[0]
    g = {}
    w = dict(w)
    row = lambda: _sds((t, D_MODEL))
    rowb = lambda: _sds((t, D_MODEL), BF16)
    vec = lambda: _sds((1, D_MODEL))

    xs = _shift_down(x)
    mix = [w["a_mix"][i:i + 1] for i in range(6)]
    xr, xk, xv, xw, xa, xg = _rowwise(_time_shift_lerps, [x, xs], [w["a_norm"]] + mix, [rowb()] * 6, tm=256, name="tmix_lerp")
    r = _mm(xr, w["a_w_r"], name="tmix_r")
    k = _mm(xk, w["a_w_k"], name="tmix_k")
    v = _mm(xv, w["a_w_v"], name="tmix_v")
    lw1 = _mm(xw, w["a_w1"], name="tmix_w1")
    wl = _mm(lw1, w["a_w2"], a_pro=jnp.tanh, name="tmix_w2")
    la1 = _mm(xa, w["a_a1"], name="tmix_a1")
    al = _mm(la1, w["a_a2"], name="tmix_a2")

    hv = lambda name: w[name].reshape(N_HEADS, 1, HEAD_DIM)
    wkv_params = [hv(name) for name in ("a_w0", "a_a0", "a_k_k", "a_k_a", "a_ln_x_w", "a_ln_x_b", "a_r_k")]
    y2, states, gathered = _wkv_fwd(r, k, v, wl, al, wkv_params, late_shards or ())
    for name, arr in zip(_LATE_GATHER, gathered):
        w[name] = arr if name.startswith("mlp_w_up") else arr.reshape(N_DEV * arr.shape[1], arr.shape[2])
    lg1 = _mm(xg, w["a_g1"], name="tmix_g1")
    gate = _mm(lg1, w["a_g2"], a_pro=_sigmoid, name="tmix_g2")
    (yg,) = _rowwise(lambda a, b: (a * b,), [y2, gate], [], [rowb()], name="tmix_gate")
    att = _mm(yg, w["a_w_out"], name="tmix_out")

    h1, hn0 = _rowwise(_residual_norm, [x, att], [w["mlp_norm0"]], [row(), rowb()], name="res_norm0")
    u0, m0 = _mlp_fwd(hn0, w["mlp_w_up0"], w["mlp_w_down0"], 0, up_dev_major)

    h2, kvn, qn = _rowwise(_residual_norm2, [h1, m0], [w["kv_norm"], w["b_norm"]], [row(), rowb(), rowb()], name="res_norm_kvq")
    kv = _mm(kvn, w["w_kv"], name="kv_proj")
    q = _mm(qn, w["b_w_q"], name="q_proj")
    cos, sin = _rope_tables(t)
    sinks = jnp.broadcast_to(w["b_sinks"].reshape(N_HEADS, 1, 1), (N_HEADS, 1, 128))
    o = _attn_fwd(q, kv, cos, sin, w["b_q_norm"], w["k_norm"], sinks)
    att2 = _mm(o, w["b_w_o"], name="attn_out")

    h3, hn1 = _rowwise(_residual_norm, [h2, att2], [w["mlp_norm1"]], [row(), rowb()], name="res_norm1")
    u1, m1 = _mlp_fwd(hn1, w["mlp_w_up1"], w["mlp_w_down1"], 1, up_dev_major)

    def loss_fn(h, m, tg):
        diff = (h + m) - tg
        part = 0.5 * jnp.sum(jnp.mean(jnp.square(diff), axis=-1, keepdims=True), axis=0, keepdims=True)
        dh = diff * (1.0 / D_MODEL)
        return dh, dh, jnp.broadcast_to(part, (1, 128))

    dh4, dh4b, loss = _rowwise(loss_fn, [h3, m1, target], [], [row(), rowb()], [_sds((1, 128))], name="loss")

    def res_norm_bwd(h, dnext, dhn, gain):
        dh, dgain = _vjp_of(lambda hh, gg: (_rms(hh, gg),), 2, (0, 1))(h, gain, dhn)
        return dnext + dh, dnext + dh, dgain

    distributed = late_shards is not None

    def square_dw(a_, b_, name):
        if distributed:
            return _mm_pair_reduce(a_, b_, device_axis="m", tile=SQUARE_PAIR_TILE, name=name)
        return _mm(a_, b_, ta=True, name=name)

    dhn1, g["mlp_w_up1"], g["mlp_w_down1"] = _mlp_bwd(hn1, u1, dh4b, w["mlp_w_up1"], w["mlp_w_down1"], 1, up_dev_major, distributed)
    dh3, dh3b, g["mlp_norm1"] = _rowwise(res_norm_bwd, [h3, dh4, dhn1], [w["mlp_norm1"]], [row(), rowb()], [vec()], name="res_norm1_bwd")

    g["b_w_o"] = square_dw(o, dh3b, "attn_out_dw")
    do = _mm(dh3b, w["b_w_o"], tb=True, name="attn_out_dx")
    dq, dkv_own, dkv_prev, g["b_q_norm"], g["k_norm"], dsinks = _attn_bwd(
        q, kv, cos, sin, w["b_q_norm"], w["k_norm"], sinks, do)
    g["b_sinks"] = dsinks[:, 0, 0].reshape(1, N_HEADS)
    g["b_w_q"] = square_dw(qn, dq, "q_proj_dw")
    dqn = _mm(dq, w["b_w_q"], tb=True, name="q_proj_dx")
    dkv_prev = jnp.pad(dkv_prev[ATT_BLOCK:], ((0, ATT_BLOCK), (0, 0)))
    (dkv,) = _rowwise(lambda a, b: (a + b,), [dkv_own, dkv_prev], [], [_sds(kv.shape, BF16)], name="kv_grad_sum")
    g["w_kv"] = _mm(kvn, dkv, ta=True, name="kv_proj_dw")
    dkvn = _mm(dkv, w["w_kv"], tb=True, name="kv_proj_dx")

    def res_norm2_bwd(h, dnext, dna, dnb, gain_a, gain_b):
        dha, dga = _vjp_of(lambda hh, gg: (_rms(hh, gg),), 2, (0, 1))(h, gain_a, dna)
        dhb, dgb = _vjp_of(lambda hh, gg: (_rms(hh, gg),), 2, (0, 1))(h, gain_b, dnb)
        return dnext + dha + dhb, dnext + dha + dhb, dga, dgb

    dh2, dh2b, g["kv_norm"], g["b_norm"] = _rowwise(res_norm2_bwd, [h2, dh3, dkvn, dqn], [w["kv_norm"], w["b_norm"]],
                                                    [row(), rowb()], [vec(), vec()], name="res_norm_kvq_bwd")

    dhn0, g["mlp_w_up0"], g["mlp_w_down0"] = _mlp_bwd(hn0, u0, dh2b, w["mlp_w_up0"], w["mlp_w_down0"], 0, up_dev_major, distributed)
    dh1, dh1b, g["mlp_norm0"] = _rowwise(res_norm_bwd, [h1, dh2, dhn0], [w["mlp_norm0"]], [row(), rowb()], [vec()], name="res_norm0_bwd")

    g["a_w_out"] = square_dw(yg, dh1b, "tmix_out_dw")
    dyg = _mm(dh1b, w["a_w_out"], tb=True, name="tmix_out_dx")
    dy2, dgate = _rowwise(lambda d, a, b: (d * b, d * a), [dyg, y2, gate], [], [row(), rowb()], name="tmix_gate_bwd")
    g["a_g2"] = _mm(lg1, dgate, ta=True, a_pro=_sigmoid, name="tmix_g2_dw")

    def dsigmoid(rr, z):
        s = jax.nn.sigmoid(z)
        return rr * s * (1.0 - s)

    dlg1 = _mm(dgate, w["a_g2"], tb=True, epi=dsigmoid, epi_args=(lg1,), out_dtype=BF16, name="tmix_g2_dx")
    g["a_g1"] = _mm(xg, dlg1, ta=True, name="tmix_g1_dw")
    dxg = _mm(dlg1, w["a_g1"], tb=True, name="tmix_g1_dx")

    early_sums = [g[name] for name in _EARLY_GRADS] if distributed else ()
    (dr, dk, dv, dwl, dal), param_grads, early_reduced = _wkv_bwd(r, k, v, wl, al, wkv_params, states, dy2, early_sums)
    for name, pg in zip(("a_w0", "a_a0", "a_k_k", "a_k_a", "a_ln_x_w", "a_ln_x_b", "a_r_k"), param_grads):
        g[name] = pg.reshape(1, D_MODEL)

    g["a_w_r"] = square_dw(xr, dr, "tmix_r_dw")
    g["a_w_k"] = square_dw(xk, dk, "tmix_k_dw")
    g["a_w_v"] = square_dw(xv, dv, "tmix_v_dw")
    dxr = _mm(dr, w["a_w_r"], tb=True, name="tmix_r_dx")
    dxk = _mm(dk, w["a_w_k"], tb=True, name="tmix_k_dx")
    dxv = _mm(dv, w["a_w_v"], tb=True, name="tmix_v_dx")
    g["a_w2"] = _mm(lw1, dwl, ta=True, a_pro=jnp.tanh, name="tmix_w2_dw")

    def dtanh(rr, z):
        th = jnp.tanh(z)
        return rr * (1.0 - th * th)

    dlw1 = _mm(dwl, w["a_w2"], tb=True, epi=dtanh, epi_args=(lw1,), out_dtype=BF16, name="tmix_w2_dx")
    g["a_w1"] = _mm(xw, dlw1, ta=True, name="tmix_w1_dw")
    dxw = _mm(dlw1, w["a_w1"], tb=True, name="tmix_w1_dx")
    g["a_a2"] = _mm(la1, dal, ta=True, name="tmix_a2_dw")
    dla1 = _mm(dal, w["a_a2"], tb=True, out_dtype=BF16, name="tmix_a2_dx")
    g["a_a1"] = _mm(xa, dla1, ta=True, name="tmix_a1_dw")
    dxa = _mm(dla1, w["a_a1"], tb=True, name="tmix_a1_dx")

    lerp_bwd = _vjp_of(_time_shift_lerps, 9, tuple(range(9)))

    def lerp_bwd_rows(x_, xs_, d0, d1, d2, d3, d4, d5, gain, *mx):
        return lerp_bwd(x_, xs_, gain, *mx, d0, d1, d2, d3, d4, d5)

    rkv_sums = [g[name] for name in _RKV_GRADS] if distributed else []
    outs = _rowwise(lerp_bwd_rows, [x, xs, dxr, dxk, dxv, dxw, dxa, dxg], [w["a_norm"]] + mix, [row(), row()], [vec()] * 7,
                    tm=128, name="tmix_lerp_bwd", chip_sums=rkv_sums)
    dx_a, dxs, g["a_norm"] = outs[0], outs[1], outs[2]
    g["a_mix"] = jnp.concatenate(outs[3:9], axis=0)
    (grad_x,) = _rowwise(lambda a, b, c: (a + b + c,), [dh1, dx_a, _shift_up(dxs)], [], [row()], name="grad_x_sum")
    reduced = dict(zip(_EARLY_GRADS + _RKV_GRADS, list(early_reduced) + list(outs[9:]))) if distributed else None
    return loss, grad_x, g, reduced


_ANY = pl.BlockSpec(memory_space=pl.ANY)
_MESH_ID = pl.DeviceIdType.MESH


def _linear(pos):
    return 4 * pos[0] + 2 * pos[1] + pos[2]


def _all_gather(shards, name):
    n = len(shards)

    def body(*refs):
        start, relay, finish = _gather_plan(refs[:n], refs[n:2 * n], *refs[2 * n:])
        start()
        relay()
        finish()

    return pl.pallas_call(
        body, out_shape=[_sds((N_DEV,) + s.shape, s.dtype) for s in shards], in_specs=[_ANY] * n, out_specs=[_ANY] * n,
        scratch_shapes=_gather_semaphores(n), name=name,
    )(*shards)


GATHER_COPIES = 8


def _gather_semaphores(n):
    return [pltpu.SemaphoreType.DMA((n * GATHER_COPIES,)), pltpu.SemaphoreType.DMA((n * GATHER_COPIES,)),
            pltpu.SemaphoreType.DMA((n,))]


def _gather_plan(ins, outs, send_sems, recv_sems, local_sems):
    n = len(ins)
    x, y, c = lax.axis_index("x"), lax.axis_index("y"), lax.axis_index("c")
    me, sibling = (x, y, c), (x, y, 1 - c)
    x_nbr, y_nbr, diag = (1 - x, y, c), (x, 1 - y, c), (1 - x, 1 - y, c)
    other = lambda pos: (pos[0], pos[1], 1 - c)

    def halves(a):
        rows = ins[a].shape[0]
        if rows % (2 * BF16_SUBLANES):
            return (0, rows), None
        return (0, rows // 2), (rows // 2, rows // 2)

    def copy(a, k, block, to, src=None, rows=None):
        dst = outs[a].at[_linear(block)]
        if rows is not None:
            dst = dst.at[pl.ds(rows[0], rows[1])]
        return pltpu.make_async_remote_copy(
            src_ref=dst if src is None else src, dst_ref=dst, send_sem=send_sems.at[a * GATHER_COPIES + k],
            recv_sem=recv_sems.at[a * GATHER_COPIES + k], device_id=to, device_id_type=_MESH_ID)

    def own_copies():
        mine = [pltpu.make_async_copy(ins[a], outs[a].at[_linear(me)], local_sems.at[a]) for a in range(n)]
        sent = []
        for a in range(n):
            sent += [copy(a, 0, me, sibling, src=ins[a]), copy(a, 1, me, x_nbr, src=ins[a]), copy(a, 2, me, y_nbr, src=ins[a])]
        return mine, sent

    def relayed_copies():
        sent = []
        for a in range(n):
            first, second = halves(a)
            sent += [copy(a, 3, x_nbr, y_nbr, rows=first), copy(a, 5, x_nbr, sibling), copy(a, 6, y_nbr, sibling)]
            if second is not None:
                sent.append(copy(a, 4, y_nbr, x_nbr, rows=second))
        return sent

    def start():
        mine, sent = own_copies()
        for cp in mine + sent:
            cp.start()

    def relay():
        for a in range(n):
            first, second = halves(a)
            copy(a, 1, x_nbr, me).wait_recv()
            copy(a, 3, x_nbr, y_nbr, rows=first).start()
            copy(a, 5, x_nbr, sibling).start()
        for a in range(n):
            first, second = halves(a)
            copy(a, 2, y_nbr, me).wait_recv()
            if second is not None:
                copy(a, 4, y_nbr, x_nbr, rows=second).start()
            copy(a, 6, y_nbr, sibling).start()

    def finish():
        mine, sent = own_copies()
        sent += relayed_copies()
        for a in range(n):
            first, second = halves(a)
            copy(a, 3, diag, me, rows=first).wait_recv()
            if second is not None:
                copy(a, 4, diag, me, rows=second).wait_recv()
            last = copy(a, 7, diag, sibling)
            last.start()
            sent.append(last)
        for a in range(n):
            copy(a, 0, other(me), me).wait_recv()
            copy(a, 5, other(x_nbr), me).wait_recv()
            copy(a, 6, other(y_nbr), me).wait_recv()
            copy(a, 7, other(diag), me).wait_recv()
        for cp in sent:
            cp.wait_send()
        for cp in mine:
            cp.wait()

    return start, relay, finish


N_CHIPS = 4


def _exchange_with_sibling(parts, name):
    n = len(parts)

    def body(*refs):
        ins, outs = refs[:n], refs[n:2 * n]
        send_sems, recv_sems = refs[2 * n:]
        x, y, c = lax.axis_index("x"), lax.axis_index("y"), lax.axis_index("c")
        copies = [pltpu.make_async_remote_copy(
            src_ref=ins[a].at[2 * q + (1 - c)], dst_ref=outs[a].at[q], send_sem=send_sems.at[a * N_CHIPS + q],
            recv_sem=recv_sems.at[a * N_CHIPS + q], device_id=(x, y, 1 - c), device_id_type=_MESH_ID)
            for a in range(n) for q in range(N_CHIPS)]
        for cp in copies:
            cp.start()
        for cp in copies:
            cp.wait()

    return pl.pallas_call(
        body, out_shape=[_sds((N_CHIPS,) + p.shape[1:], p.dtype) for p in parts], in_specs=[_ANY] * n, out_specs=[_ANY] * n,
        scratch_shapes=[pltpu.SemaphoreType.DMA((n * N_CHIPS,)), pltpu.SemaphoreType.DMA((n * N_CHIPS,))], name=name,
    )(*parts)


def _pair_sum(part, recv, core, out_dtype, name):
    _, r, cdim = recv.shape
    tr = max(8, min(r, STREAM_BLOCK_ELEMENTS // cdim))
    assert r % tr == 0, (name, r, tr)

    def body(core_ref, p_ref, r_ref, o_ref):
        o_ref[...] = (p_ref[...] + r_ref[...]).astype(o_ref.dtype)

    grid_spec = pltpu.PrefetchScalarGridSpec(
        num_scalar_prefetch=1, grid=(N_CHIPS, r // tr),
        in_specs=[pl.BlockSpec((None, None, tr, cdim), lambda q, i, core_ref: (q, core_ref[0], i, 0)),
                  pl.BlockSpec((None, tr, cdim), lambda q, i, core_ref: (q, i, 0))],
        out_specs=pl.BlockSpec((None, tr, cdim), lambda q, i, core_ref: (q, i, 0)))
    return pl.pallas_call(
        body, grid_spec=grid_spec, out_shape=_sds((N_CHIPS, r, cdim), out_dtype),
        compiler_params=_params(("parallel", "parallel")), name=name,
    )(core, part.reshape(N_CHIPS, 2, r, cdim), recv)


def _exchange_between_chips(parts, name):
    n = len(parts)

    def body(*refs):
        start, finish = _chip_exchange_plan(refs[:n], refs[n:2 * n], *refs[2 * n:])
        start()
        finish()

    return pl.pallas_call(
        body, out_shape=[_sds(p.shape, p.dtype) for p in parts], in_specs=[_ANY] * n, out_specs=[_ANY] * n,
        scratch_shapes=_chip_exchange_semaphores(n), name=name,
    )(*parts)


def _chip_exchange_semaphores(n):
    n_other = N_CHIPS - 1
    return [pltpu.SemaphoreType.DMA((n * n_other,)), pltpu.SemaphoreType.DMA((n * n_other,)), pltpu.SemaphoreType.DMA((n,))]


def _chip_exchange_plan(ins, outs, send_sems, recv_sems, local_sems):
    n = len(ins)
    n_other = N_CHIPS - 1
    x, y, c = lax.axis_index("x"), lax.axis_index("y"), lax.axis_index("c")
    my_chip = 2 * x + y

    def all_copies():
        mine = [pltpu.make_async_copy(ins[a].at[my_chip], outs[a].at[my_chip], local_sems.at[a]) for a in range(n)]
        remote = []
        for j, (fx, fy) in enumerate([(1, 0), (0, 1), (1, 1)]):
            px, py = (1 - x if fx else x), (1 - y if fy else y)
            for a in range(n):
                remote.append(pltpu.make_async_remote_copy(
                    src_ref=ins[a].at[2 * px + py], dst_ref=outs[a].at[my_chip], send_sem=send_sems.at[a * n_other + j],
                    recv_sem=recv_sems.at[a * n_other + j], device_id=(px, py, c), device_id_type=_MESH_ID))
        return mine, remote

    def start():
        mine, remote = all_copies()
        for cp in mine + remote:
            cp.start()

    def finish():
        mine, remote = all_copies()
        for cp in remote + mine:
            cp.wait()

    return start, finish


def _chip_sums(parts, names, tag):
    from_sibling = _exchange_with_sibling(parts, name="scatter_grads_sibling_" + tag)
    core = lax.axis_index("c").astype(jnp.int32).reshape(1)
    return [_pair_sum(p, r, core, F32 if nm.startswith("pack") else BF16, name="pair_sum_" + nm)
            for p, r, nm in zip(parts, from_sibling, names)]


def _adamw(w, m, v, slots, name, layer=0, n_layers=1, into=None):
    r, c = w.shape[-2:]
    ns = slots.shape[0]
    tr = max(8, min(r, STREAM_BLOCK_ELEMENTS // c))
    assert r % tr == 0, (name, r, tr)

    def body(w_ref, m_ref, v_ref, g_ref, *rest):
        g_out, d_out, m_out, v_out = rest[-4:]
        g = g_ref[0].astype(F32)
        for s in range(1, ns):
            g = g + g_ref[s].astype(F32)
        m_new = ADAM_B1 * m_ref[...] + (1.0 - ADAM_B1) * g
        v_new = ADAM_B2 * v_ref[...] + (1.0 - ADAM_B2) * jnp.square(g)
        m_hat = m_new / (1.0 - ADAM_B1 ** ADAM_STEP)
        v_hat = v_new / (1.0 - ADAM_B2 ** ADAM_STEP)
        d_out[...] = -ADAM_LR * (m_hat / (jnp.sqrt(v_hat) + ADAM_EPS) + ADAM_WD * w_ref[...])
        g_out[...], m_out[...], v_out[...] = g, m_new, v_new

    if n_layers == 1:
        spec, out_shape, earlier, aliases = pl.BlockSpec((tr, c), lambda i: (i, 0)), _sds((r, c)), [], {}
    else:
        spec, out_shape = pl.BlockSpec((None, tr, c), lambda i: (layer, i, 0)), _sds((n_layers, r, c))
        earlier = list(into) if into is not None else []
        aliases = {4 + i: i for i in range(len(earlier))}
    in_specs = [spec, spec, spec, pl.BlockSpec((ns, tr, c), lambda i: (0, i, 0))]
    return pl.pallas_call(
        body, grid=(r // tr,), in_specs=in_specs + [_ANY] * len(earlier), out_specs=[spec] * 4, out_shape=[out_shape] * 4,
        input_output_aliases=aliases, compiler_params=_params(("parallel",)), name=name,
    )(w, m, v, slots, *earlier)


_COL_VECTORS = ("a_norm", "a_mix", "a_w0", "a_a0", "a_k_k", "a_k_a", "a_ln_x_w", "a_ln_x_b")
_COL_VEC_ROWS = 16
_COL_ROWS = _COL_VEC_ROWS + 2 * LORA_PAD + 256
_ROW_COLS = 2 * LORA_PAD + 256 + 512
_REPL_ROWS = 8


def _pad_to(a, size, axis):
    widths = [(0, 0)] * a.ndim
    widths[axis] = (0, size - a.shape[axis])
    return jnp.pad(a, widths)


def _pack_cols(p):
    width = p["a_norm"].shape[-1]
    vecs = jnp.concatenate([p[n].reshape(-1, width) for n in _COL_VECTORS], axis=0)
    return jnp.concatenate([_pad_to(vecs, _COL_VEC_ROWS, 0), _pad_to(p["a_w2"].reshape(-1, width), LORA_PAD, 0),
                            _pad_to(p["a_a2"].reshape(-1, width), LORA_PAD, 0), p["a_g2"].reshape(-1, width)], axis=0)


def _unpack_cols(a, lead):
    width = a.shape[-1]
    out, row = {}, 0
    for n in _COL_VECTORS:
        k = 6 if n == "a_mix" else 1
        out[n] = a[row:row + k].reshape(lead + ((6, width) if n == "a_mix" else (width,)))
        row += k
    base = _COL_VEC_ROWS
    out["a_w2"] = a[base:base + 96].reshape(lead + (96, width))
    out["a_a2"] = a[base + LORA_PAD:base + LORA_PAD + 96].reshape(lead + (96, width))
    out["a_g2"] = a[base + 2 * LORA_PAD:].reshape(lead + (256, width))
    return out


def _pack_rows(p):
    rows = p["w_kv"].shape[0]
    return jnp.concatenate([_pad_to(p["a_w1"].reshape(rows, -1), LORA_PAD, 1), _pad_to(p["a_a1"].reshape(rows, -1), LORA_PAD, 1),
                            p["a_g1"].reshape(rows, -1), p["w_kv"]], axis=1)


def _unpack_rows(a, lead):
    rows = a.shape[0]
    return {"a_w1": a[:, :96].reshape(lead + (rows, 96)), "a_a1": a[:, LORA_PAD:LORA_PAD + 96].reshape(lead + (rows, 96)),
            "a_g1": a[:, 2 * LORA_PAD:2 * LORA_PAD + 256].reshape(lead + (rows, 256)), "w_kv": a[:, 2 * LORA_PAD + 256:]}


def _pack_repl(p):
    row = lambda a: _pad_to(a.reshape(1, -1), D_MODEL, 1)
    return jnp.concatenate([p["mlp_norm"].reshape(2, D_MODEL), row(p["kv_norm"]), row(p["b_norm"]), row(p["a_r_k"]),
                            row(p["k_norm"]), row(p["b_q_norm"]), row(p["b_sinks"])], axis=0)


def _unpack_repl(a):
    return {"mlp_norm": a[0:2], "kv_norm": a[2], "b_norm": a[3:4], "a_r_k": a[4].reshape(1, N_HEADS, HEAD_DIM),
            "k_norm": a[5, :HEAD_DIM], "b_q_norm": a[6:7, :HEAD_DIM], "b_sinks": a[7:8, :N_HEADS]}


_WEIGHTS = ("a_norm", "a_mix", "a_w_rkv", "a_w0", "a_w1", "a_w2", "a_a0", "a_a1", "a_a2", "a_g1", "a_g2", "a_k_k", "a_k_a",
            "a_r_k", "a_ln_x_w", "a_ln_x_b", "a_w_out", "mlp_norm", "mlp_w_up", "mlp_w_down", "kv_norm", "w_kv", "k_norm",
            "b_norm", "b_w_q", "b_q_norm", "b_sinks", "b_w_o")


def _big_shards(p):
    return [p["a_w_rkv"][0, 0], p["a_w_rkv"][0, 1], p["a_w_rkv"][0, 2], p["a_w_out"][0], p["mlp_w_up"][0], p["mlp_w_up"][1],
            p["mlp_w_down"][0], p["mlp_w_down"][1], p["b_w_q"][0], p["b_w_o"][0]]


_BIG_NAMES = ("a_w_r", "a_w_k", "a_w_v", "a_w_out", "mlp_w_up0", "mlp_w_up1", "mlp_w_down0", "mlp_w_down1", "b_w_q", "b_w_o")


def kernel(x, a_norm, a_mix, a_w_rkv, a_w0, a_w1, a_w2, a_a0, a_a1, a_a2, a_g1, a_g2, a_k_k, a_k_a, a_r_k, a_ln_x_w,
           a_ln_x_b, a_w_out, mlp_norm, mlp_w_up, mlp_w_down, kv_norm, w_kv, k_norm, b_norm, b_w_q, b_q_norm, b_sinks,
           b_w_o, loss_target, m_a_norm, m_a_mix, m_a_w_rkv, m_a_w0, m_a_w1, m_a_w2, m_a_a0, m_a_a1, m_a_a2, m_a_g1,
           m_a_g2, m_a_k_k, m_a_k_a, m_a_r_k, m_a_ln_x_w, m_a_ln_x_b, m_a_w_out, m_mlp_norm, m_mlp_w_up, m_mlp_w_down,
           m_kv_norm, m_w_kv, m_k_norm, m_b_norm, m_b_w_q, m_b_q_norm, m_b_sinks, m_b_w_o, v_a_norm, v_a_mix, v_a_w_rkv,
           v_a_w0, v_a_w1, v_a_w2, v_a_a0, v_a_a1, v_a_a2, v_a_g1, v_a_g2, v_a_k_k, v_a_k_a, v_a_r_k, v_a_ln_x_w,
           v_a_ln_x_b, v_a_w_out, v_mlp_norm, v_mlp_w_up, v_mlp_w_down, v_kv_norm, v_w_kv, v_k_norm, v_b_norm, v_b_w_q,
           v_b_q_norm, v_b_sinks, v_b_w_o):
    given = locals()
    wts = {n: given[n] for n in _WEIGHTS}
    mom = {n: given["m_" + n] for n in _WEIGHTS}
    var = {n: given["v_" + n] for n in _WEIGHTS}

    cols_w, rows_w, repl_w = _pack_cols(wts), _pack_rows(wts), _pack_repl(wts)
    big_w = _big_shards(wts)
    big_bf16 = dict(zip(_BIG_NAMES, [b.astype(BF16) for b in big_w]))
    first_names = [k for k in _BIG_NAMES if k not in _LATE_GATHER]
    lora_in = rows_w[:, :2 * LORA_PAD]
    gathered = _all_gather([cols_w, lora_in] + [big_bf16[k] for k in first_names], name="gather_weights")
    full_cols = gathered[0].transpose(1, 0, 2).reshape(_COL_ROWS, D_MODEL)
    full_lora_in = gathered[1].reshape(D_MODEL, 2 * LORA_PAD)
    w = {}
    w.update({k: v.reshape(v.shape[1:]) for k, v in _unpack_cols(full_cols, (1,)).items()})
    for k in ("a_norm", "a_w0", "a_a0", "a_k_k", "a_k_a", "a_ln_x_w", "a_ln_x_b"):
        w[k] = w[k].reshape(1, D_MODEL)
    for k in ("a_w2", "a_a2"):
        w[k] = _pad_to(w[k], LORA_PAD, 0)
    w["a_w1"], w["a_a1"] = full_lora_in[:, :LORA_PAD], full_lora_in[:, LORA_PAD:]
    for k, arr in zip(first_names, gathered[2:]):
        w[k] = arr.reshape(N_DEV * arr.shape[1], arr.shape[2])
    late_shards = [big_bf16[k] for k in _LATE_WEIGHTS] + [big_bf16["a_w_out"], a_g1[0], w_kv]
    w["mlp_norm0"], w["mlp_norm1"] = mlp_norm[0:1], mlp_norm[1:2]
    w["kv_norm"], w["k_norm"] = kv_norm.reshape(1, D_MODEL), k_norm.reshape(1, HEAD_DIM)
    w["b_norm"], w["b_q_norm"], w["b_sinks"], w["a_r_k"] = b_norm, b_q_norm, b_sinks, a_r_k.reshape(1, D_MODEL)

    loss_local, grad_x, g, big_reduced = _local_step(x[0], loss_target[0], w, late_shards)
    loss = lax.psum(loss_local[0, 0], MESH_AXES)

    g_lead = {k: g[k][None] for k in ("a_norm", "a_mix", "a_w0", "a_a0", "a_k_k", "a_k_a", "a_ln_x_w", "a_ln_x_b", "a_g2")}
    g_lead["a_w2"], g_lead["a_a2"] = g["a_w2"][None, :96], g["a_a2"][None, :96]
    g_cols = _pack_cols(g_lead).reshape(_COL_ROWS, N_DEV, D_MODEL // N_DEV).transpose(1, 0, 2)
    g_rows = _pack_rows({"a_w1": g["a_w1"][:, :96], "a_a1": g["a_a1"][:, :96], "a_g1": g["a_g1"], "w_kv": g["w_kv"]})
    g_rows = g_rows.reshape(N_DEV, D_MODEL // N_DEV, _ROW_COLS)
    pack_sums = _chip_sums([g_cols, g_rows], ("pack_cols", "pack_rows"), "late")
    reduced = list(_exchange_between_chips(pack_sums, name="scatter_grads_chips_late")) + [big_reduced[k] for k in _BIG_NAMES]
    g_repl = _pack_repl({"mlp_norm": jnp.concatenate([g["mlp_norm0"], g["mlp_norm1"]], axis=0), "kv_norm": g["kv_norm"],
                         "b_norm": g["b_norm"], "a_r_k": g["a_r_k"], "k_norm": g["k_norm"], "b_q_norm": g["b_q_norm"],
                         "b_sinks": g["b_sinks"]})
    (repl_slots,) = _all_gather([g_repl], name="gather_replicated_grads")

    res = {}
    cols4 = _adamw(cols_w, _pack_cols(mom), _pack_cols(var), reduced[0], name="adamw_cols")
    rows4 = _adamw(rows_w, _pack_rows(mom), _pack_rows(var), reduced[1], name="adamw_rows")
    repl4 = _adamw(repl_w, _pack_repl(mom), _pack_repl(var), repl_slots, name="adamw_replicated")
    for unpacked in ([_unpack_cols(a, (1,)) for a in cols4], [_unpack_rows(a, (1,)) for a in rows4], [_unpack_repl(a) for a in repl4]):
        for k in unpacked[0]:
            res[k] = tuple(u[k] for u in unpacked)
    stacked = {"a_w_r": ("a_w_rkv", 0), "a_w_k": ("a_w_rkv", 1), "a_w_v": ("a_w_rkv", 2), "mlp_w_up0": ("mlp_w_up", 0),
               "mlp_w_up1": ("mlp_w_up", 1), "mlp_w_down0": ("mlp_w_down", 0), "mlp_w_down1": ("mlp_w_down", 1)}
    big4, earlier = {}, None
    for k, bw, bm, bv, slots in zip(_BIG_NAMES, big_w, _big_shards(mom), _big_shards(var), reduced[2:]):
        if k in stacked:
            src, layer = stacked[k]
            as_layers = lambda a: a.reshape((-1,) + a.shape[-2:])
            earlier = _adamw(as_layers(wts[src]), as_layers(mom[src]), as_layers(var[src]), slots, name="adamw_" + k,
                             layer=layer, n_layers=as_layers(wts[src]).shape[0], into=earlier if layer else None)
        else:
            earlier = _adamw(bw, bm, bv, slots, name="adamw_" + k)
        big4[k] = earlier
    res["a_w_rkv"] = tuple(a[None] for a in big4["a_w_v"])
    res["a_w_out"] = tuple(a[None] for a in big4["a_w_out"])
    res["mlp_w_up"], res["mlp_w_down"] = tuple(big4["mlp_w_up1"]), tuple(big4["mlp_w_down1"])
    res["b_w_q"] = tuple(a[None] for a in big4["b_w_q"])
    res["b_w_o"] = tuple(a[None] for a in big4["b_w_o"])
    res["w_kv"] = tuple(a.reshape(w_kv.shape) for a in res["w_kv"])

    outs = [loss, grad_x[None]]
    for i in range(4):
        outs += [res[n][i].reshape(given[n].shape) for n in _WEIGHTS]
    return tuple(outs)
```

```python
import functools
import math

import jax
import jax.numpy as jnp
from jax import lax
from jax.experimental import pallas as pl
from jax.experimental.pallas import tpu as pltpu

F32 = jnp.float32
BF16 = jnp.bfloat16

D_MODEL = 2048
N_HEADS = 32
HEAD_DIM = 64
N_KV_HEADS = 4
Q_PER_KV = 8
ATT_BLOCK = 128
WKV_CHUNK = 64
LORA_PAD = 128
D_FF = 8192
N_DEV = 8
RMS_EPS = 1e-6
GN_EPS = 64e-5
L2_EPS = 1e-12
ROPE_THETA = 10000.0
ADAM_LR, ADAM_B1, ADAM_B2, ADAM_EPS, ADAM_WD, ADAM_STEP = 0.001, 0.9, 0.999, 1e-08, 0.01, 10
MASK_VALUE = -1e30
VMEM_LIMIT_BYTES = 56 * 1024 * 1024
MM_TILE_M, MM_TILE_N, MM_TILE_K = 1024, 1024, 2048
STREAM_BLOCK_ELEMENTS = 1 << 18
BF16_SUBLANES = 16
MESH_AXES = ("x", "y", "c")

_NN = (((1,), (0,)), ((), ()))
_NT = (((1,), (1,)), ((), ()))
_TN = (((0,), (0,)), ((), ()))
_BNN = (((2,), (1,)), ((0,), (0,)))
_BNT = (((2,), (2,)), ((0,), (0,)))
_BTN = (((1,), (1,)), ((0,), (0,)))


def _params(sem):
    return pltpu.CompilerParams(dimension_semantics=sem, vmem_limit_bytes=VMEM_LIMIT_BYTES)


def _split2(a):
    hi = a.astype(BF16)
    return hi, (a - hi.astype(F32)).astype(BF16)


def _dot3(a, b, dims):
    ah, al = _split2(a)
    bh, bl = _split2(b)
    if dims in (_NN, _NT, _TN):
        d = lambda p, q: lax.dot_general(p, q, dims, preferred_element_type=F32)
        return d(ah, bh) + (d(al, bh) + d(ah, bl))
    (ca,), (cb,) = dims[0]
    return lax.dot_general(jnp.concatenate([ah, al, ah], axis=ca), jnp.concatenate([bh, bh, bl], axis=cb), dims,
                           preferred_element_type=F32)


@functools.partial(jax.custom_vjp, nondiff_argnums=(2,))
def _hdot(a, b, dims=_NN):
    return _dot3(a, b, dims)


def _hdot_fwd(a, b, dims):
    return _dot3(a, b, dims), (a, b)


def _hdot_bwd(dims, res, g):
    a, b = res
    nn, nt, tn = (_NN, _NT, _TN) if dims in (_NN, _NT, _TN) else (_BNN, _BNT, _BTN)
    if dims == nn:
        return _dot3(g, b, nt), _dot3(a, g, tn)
    if dims == nt:
        return _dot3(g, b, nn), _dot3(g, a, tn)
    assert dims == tn
    return _dot3(b, g, nt), _dot3(a, g, nn)


_hdot.defvjp(_hdot_fwd, _hdot_bwd)


def _tri_parts(x):
    hi = x.astype(BF16)
    r1 = x - hi.astype(F32)
    mid = r1.astype(BF16)
    return hi, mid, (r1 - mid.astype(F32)).astype(BF16)


@jax.custom_vjp
def _mask_dot(mask, x):
    mb = mask.astype(BF16)
    return lax.dot_general(jnp.concatenate([mb, mb, mb], axis=2), jnp.concatenate(_tri_parts(x), axis=1), _BNN,
                           preferred_element_type=F32)


def _mask_dot_fwd(mask, x):
    return _mask_dot(mask, x), mask


def _mask_dot_bwd(mask, g):
    mb = mask.astype(BF16)
    return jnp.zeros_like(mask), lax.dot_general(jnp.concatenate([mb, mb, mb], axis=1), jnp.concatenate(_tri_parts(g), axis=1),
                                                 _BTN, preferred_element_type=F32)


_mask_dot.defvjp(_mask_dot_fwd, _mask_dot_bwd)


def _b16dot(a, b, dims):
    return lax.dot_general(a.astype(BF16), b.astype(BF16), dims, preferred_element_type=F32)


@jax.custom_vjp
def _bdot(a, b):
    return _b16dot(a, b, _NN)


def _bdot_fwd(a, b):
    return _b16dot(a, b, _NN), (a, b)


def _bdot_bwd(res, g):
    a, b = res
    return _b16dot(g, b, _NT), _b16dot(a, g, _TN)


_bdot.defvjp(_bdot_fwd, _bdot_bwd)


@jax.custom_vjp
def _bdot_nt(a, b):
    return _b16dot(a, b, _NT)


def _bdot_nt_fwd(a, b):
    return _b16dot(a, b, _NT), (a, b)


def _bdot_nt_bwd(res, g):
    a, b = res
    return _b16dot(g, b, _NN), _b16dot(g, a, _TN)


_bdot_nt.defvjp(_bdot_nt_fwd, _bdot_nt_bwd)


def _rms(x, gain):
    return x * lax.rsqrt(jnp.mean(x * x, axis=-1, keepdims=True) + RMS_EPS) * gain


def _vjp_of(f, n_in, diff):
    def g(*args):
        ins, cts = args[:n_in], args[n_in:]

        def fd(*d):
            full = list(ins)
            for pos, i in enumerate(diff):
                full[i] = d[pos]
            return f(*full)

        _, pull = jax.vjp(fd, *[ins[i] for i in diff])
        return pull(tuple(cts))
    return g


def _mm(a, b, *, name, ta=False, tb=False, a_pro=None, epi=None, epi_args=(), out_dtype=F32,
        tm=MM_TILE_M, tn=MM_TILE_N, tk=MM_TILE_K, dims=None, b_spec=None, o_spec=None, o_shape=None):
    if dims is None:
        m, k = (a.shape[1], a.shape[0]) if ta else a.shape
        n = b.shape[0] if tb else b.shape[1]
    else:
        m, n, k = dims
    tm, tn, tk = min(tm, m), min(tn, n), min(tk, k)
    assert m % tm == 0 and n % tn == 0 and k % tk == 0, (name, m, n, k, tm, tn, tk)
    nk = k // tk
    ne = len(epi_args)
    cdims = (((0 if ta else 1,), (1 if tb else 0,)), ((), ()))

    def body(a_ref, b_ref, *rest):
        e_refs, o_ref, acc = rest[:ne], rest[ne], rest[ne + 1]
        kk = pl.program_id(2)

        @pl.when(kk == 0)
        def _():
            acc[...] = jnp.zeros_like(acc)

        av = a_ref[...]
        if a_pro is not None:
            av = a_pro(av.astype(F32))
        acc[...] += lax.dot_general(av.astype(BF16), b_ref[...].astype(BF16), cdims, preferred_element_type=F32)

        @pl.when(kk == nk - 1)
        def _():
            r = acc[...]
            if epi is not None:
                r = epi(r, *[e[...] for e in e_refs])
            o_ref[...] = r.astype(o_ref.dtype)

    a_spec = pl.BlockSpec((tk, tm), lambda i, j, q: (q, i)) if ta else pl.BlockSpec((tm, tk), lambda i, j, q: (i, q))
    if b_spec is None:
        b_spec = pl.BlockSpec((tn, tk), lambda i, j, q: (j, q)) if tb else pl.BlockSpec((tk, tn), lambda i, j, q: (q, j))
    if o_spec is None:
        o_spec = pl.BlockSpec((tm, tn), lambda i, j, q: (i, j))
        o_shape = (m, n)
    e_specs = [pl.BlockSpec((tm, tn), lambda i, j, q: (i, j)) for _ in epi_args]
    return pl.pallas_call(
        body, grid=(m // tm, n // tn, nk), in_specs=[a_spec, b_spec] + e_specs, out_specs=o_spec,
        out_shape=jax.ShapeDtypeStruct(o_shape, out_dtype), scratch_shapes=[pltpu.VMEM((tm, tn), F32)],
        compiler_params=_params(("parallel", "parallel", "arbitrary")), name=name,
    )(a, b, *epi_args)


def _rowwise(fn, rows, params, out_rows, out_params=(), *, tm=256, name, chip_sums=()):
    t = rows[0].shape[0]
    tm = min(tm, t)
    assert t % tm == 0
    nr, npar, nor, nop, ns = len(rows), len(params), len(out_rows), len(out_params), len(chip_sums)
    steps = t // tm

    def body(*refs):
        r, p = refs[:nr], refs[nr:nr + npar]
        o, op = refs[nr + npar + ns:nr + npar + ns + nor], refs[nr + npar + ns + nor:nr + npar + ns + nor + nop]
        if ns:
            start, finish = _chip_exchange_plan(refs[nr + npar:nr + npar + ns], refs[nr + npar + ns + nor + nop:nr + npar + 2 * ns + nor + nop],
                                                *refs[nr + npar + 2 * ns + nor + nop:])
            pl.when(pl.program_id(0) == 0)(start)
        outs = fn(*[x[...] for x in r], *[x[...] for x in p])
        for ref, val in zip(o, outs[:nor]):
            ref[...] = val.astype(ref.dtype)
        if nop:
            @pl.when(pl.program_id(0) == 0)
            def _():
                for ref in op:
                    ref[...] = jnp.zeros_like(ref)

            for ref, val in zip(op, outs[nor:]):
                ref[...] += val.astype(F32)
        if ns:
            pl.when(pl.program_id(0) == steps - 1)(finish)

    in_specs = [pl.BlockSpec((tm, x.shape[1]), lambda i: (i, 0)) for x in rows]
    in_specs += [pl.BlockSpec(p.shape, lambda i: (0, 0)) for p in params]
    out_specs = [pl.BlockSpec((tm, s.shape[1]), lambda i: (i, 0)) for s in out_rows]
    out_specs += [pl.BlockSpec(s.shape, lambda i: (0, 0)) for s in out_params]
    return pl.pallas_call(
        body, grid=(steps,), in_specs=in_specs + [_ANY] * ns, out_specs=out_specs + [_ANY] * ns,
        out_shape=list(out_rows) + list(out_params) + [_sds(c.shape, c.dtype) for c in chip_sums],
        scratch_shapes=_chip_exchange_semaphores(ns) if ns else [],
        compiler_params=_params(("arbitrary",)), name=name,
    )(*rows, *params, *chip_sums)


def _sds(shape, dtype=F32):
    return jax.ShapeDtypeStruct(tuple(shape), dtype)


def _doubling_powers(l):
    powers = [l]
    for _ in range(int(math.log2(l.shape[-1])) - 1):
        powers.append(_dot3(powers[-1], powers[-1], _BNN))
    return powers


@jax.custom_vjp
def _unit_lower_solve(l, z):
    u = z
    for p in _doubling_powers(l):
        u = u + _dot3(p, u, _BNN)
    return u


def _unit_lower_solve_fwd(l, z):
    powers = _doubling_powers(l)
    u = z
    for p in powers:
        u = u + _dot3(p, u, _BNN)
    return u, (powers, u)


def _unit_lower_solve_bwd(res, du):
    powers, u = res
    g = du
    for p in powers:
        g = g + _dot3(p, g, _BTN)
    return _dot3(g, u, _BNT), g


_unit_lower_solve.defvjp(_unit_lower_solve_fwd, _unit_lower_solve_bwd)


def _wkv_chunk(s0, r, lw, k, v, a, b):
    nb, c, _ = r.shape
    ti = lax.broadcasted_iota(jnp.int32, (nb, c, c), 1)
    si = lax.broadcasted_iota(jnp.int32, (nb, c, c), 2)
    incl, strict = si <= ti, si < ti
    cum = _mask_dot(incl.astype(F32), lw)
    tot = jnp.sum(lw, axis=1, keepdims=True)
    rcum = tot - cum
    w_inv = jnp.exp(-cum)
    at, rt, bt, kt = a * jnp.exp(cum - lw), r * jnp.exp(cum), b * w_inv, k * w_inv
    l_ab = jnp.where(strict, _hdot(at, bt, _BNT), 0.0)
    l_ak = jnp.where(strict, _hdot(at, kt, _BNT), 0.0)
    t_rb = jnp.where(incl, _hdot(rt, bt, _BNT), 0.0)
    t_rk = jnp.where(incl, _hdot(rt, kt, _BNT), 0.0)
    u = _unit_lower_solve(l_ab, _hdot(at, s0, _BNT) + _hdot(l_ak, v, _BNN))
    y = _hdot(rt, s0, _BNT) + _hdot(t_rb, u, _BNN) + _hdot(t_rk, v, _BNN)
    e = jnp.exp(rcum)
    s1 = s0 * jnp.exp(tot) + _hdot(u, b * e, _BTN) + _hdot(v, k * e, _BTN)
    return y, s1


WKV_HEADS_PER_STEP = 16
GATHER_RELAY_AT = 0.625


def _first_and_last_step(grid):
    i, j = pl.program_id(0), pl.program_id(1)
    return jnp.logical_and(i == 0, j == 0), jnp.logical_and(i == grid[0] - 1, j == grid[1] - 1)


N_WKV_PARAMS = 7


def _tmix_chunk(s0, r, k, v, wl, al, w0, a0, k_k, k_a, ln_w, ln_b, r_k):
    lw, kmod, a, b = _wkv_prep(k, wl, al, w0, a0, k_k, k_a)
    y, s1 = _wkv_chunk(s0, r, lw, kmod, v, a, b)
    (y2,) = _wkv_post(y, r, kmod, v, ln_w, ln_b, r_k)
    return y2, s1


def _split_heads(x, nh):
    return x.reshape(x.shape[0], nh, HEAD_DIM).transpose(1, 0, 2)


def _merge_heads(xh):
    return jnp.concatenate([xh[h] for h in range(xh.shape[0])], axis=1)


def _wkv_fwd(r, k, v, wl, al, params, shards=()):
    t = r.shape[0]
    nh, n = N_HEADS, HEAD_DIM
    nc = t // WKV_CHUNK
    hb = WKV_HEADS_PER_STEP
    grid = (nh // hb, nc)
    ns = len(shards)
    n_in = 5 + N_WKV_PARAMS

    def body(*refs):
        y_ref, s_ref = refs[n_in + ns:n_in + ns + 2]
        state = refs[n_in + 2 * ns + 2]
        if ns:
            start, relay, finish = _gather_plan(refs[n_in:n_in + ns], refs[n_in + ns + 2:n_in + 2 * ns + 2], *refs[n_in + 2 * ns + 3:])
            first, last = _first_and_last_step(grid)
            pl.when(first)(start)
            relay_step = int(GATHER_RELAY_AT * grid[0] * nc)
            pl.when(jnp.logical_and(pl.program_id(0) == relay_step // nc, pl.program_id(1) == relay_step % nc))(relay)

        @pl.when(pl.program_id(1) == 0)
        def _():
            state[...] = jnp.zeros_like(state)

        s0 = state[...]
        s_ref[:, 0] = s0
        rows = [_split_heads(ref[...], hb) for ref in refs[:5]]
        y, s1 = _tmix_chunk(s0, *rows, *[ref[...] for ref in refs[5:n_in]])
        y_ref[...] = _merge_heads(y)
        state[...] = s1
        if ns:
            pl.when(last)(finish)

    blk = pl.BlockSpec((WKV_CHUNK, hb * n), lambda h, c: (c, h))
    pblk = pl.BlockSpec((hb, 1, n), lambda h, c: (h, 0, 0))
    sblk = pl.BlockSpec((hb, 1, n, n), lambda h, c: (h, c, 0, 0))
    outs = pl.pallas_call(
        body, grid=grid, in_specs=[blk] * 5 + [pblk] * N_WKV_PARAMS + [_ANY] * ns, out_specs=[blk, sblk] + [_ANY] * ns,
        out_shape=[_sds((t, nh * n)), _sds((nh, nc, n, n))] + [_sds((N_DEV,) + s.shape, s.dtype) for s in shards],
        scratch_shapes=[pltpu.VMEM((hb, n, n), F32)] + (_gather_semaphores(ns) if ns else []),
        compiler_params=_params(("arbitrary", "arbitrary")), name="wkv_fwd",
    )(r, k, v, wl, al, *params, *shards)
    return outs[0], outs[1], list(outs[2:])


def _wkv_bwd(r, k, v, wl, al, params, states, dy, chip_sums=()):
    t = r.shape[0]
    nh, n = N_HEADS, HEAD_DIM
    nc = t // WKV_CHUNK
    hb = WKV_HEADS_PER_STEP
    grid = (nh // hb, nc)
    ns = len(chip_sums)
    n_in = 5 + N_WKV_PARAMS
    n_out = 5 + N_WKV_PARAMS

    def body(*refs):
        s_ref, dy_ref = refs[n_in:n_in + 2]
        out_refs = refs[n_in + 2 + ns:n_in + 2 + ns + n_out]
        dstate = refs[n_in + 2 + 2 * ns + n_out]
        if ns:
            start, finish = _chip_exchange_plan(refs[n_in + 2:n_in + 2 + ns], refs[n_in + 2 + ns + n_out:n_in + 2 + 2 * ns + n_out],
                                                *refs[n_in + 3 + 2 * ns + n_out:])
            first, last = _first_and_last_step(grid)
            pl.when(first)(start)

        @pl.when(pl.program_id(1) == 0)
        def _():
            dstate[...] = jnp.zeros_like(dstate)
            for ref in out_refs[5:]:
                ref[...] = jnp.zeros_like(ref)

        rows = [_split_heads(ref[...], hb) for ref in refs[:5]]
        _, pull = jax.vjp(_tmix_chunk, s_ref[:, 0], *rows, *[ref[...] for ref in refs[5:n_in]])
        grads = pull((_split_heads(dy_ref[...], hb), dstate[...]))
        dstate[...] = grads[0]
        for ref, val in zip(out_refs[:5], grads[1:6]):
            ref[...] = _merge_heads(val).astype(ref.dtype)
        for ref, val in zip(out_refs[5:], grads[6:]):
            ref[...] += val
        if ns:
            pl.when(last)(finish)

    blk = pl.BlockSpec((WKV_CHUNK, hb * n), lambda h, c: (nc - 1 - c, h))
    pblk = pl.BlockSpec((hb, 1, n), lambda h, c: (h, 0, 0))
    sblk = pl.BlockSpec((hb, 1, n, n), lambda h, c: (h, nc - 1 - c, 0, 0))
    outs = pl.pallas_call(
        body, grid=grid, in_specs=[blk] * 5 + [pblk] * N_WKV_PARAMS + [sblk, blk] + [_ANY] * ns,
        out_specs=[blk] * 5 + [pblk] * N_WKV_PARAMS + [_ANY] * ns,
        out_shape=[_sds((t, nh * n), BF16)] * 5 + [_sds((nh, 1, n))] * N_WKV_PARAMS + [_sds(p.shape, p.dtype) for p in chip_sums],
        scratch_shapes=[pltpu.VMEM((hb, n, n), F32)] + (_chip_exchange_semaphores(ns) if ns else []),
        compiler_params=_params(("arbitrary", "arbitrary")), name="wkv_bwd",
    )(r, k, v, wl, al, *params, states, dy, *chip_sums)
    return outs[:5], outs[5:n_out], list(outs[n_out:])


def _wkv_prep(k, wl, al, w0, a0, k_k, k_a):
    z = -(w0 + wl)
    softplus = jnp.maximum(z, 0.0) + jnp.log1p(jnp.exp(-jnp.abs(z)))
    lw = -jnp.exp(-softplus - 0.5)
    asig = jax.nn.sigmoid(a0 + al)
    kk = k * k_k
    kk = kk / jnp.maximum(jnp.sqrt(jnp.sum(kk * kk, axis=-1, keepdims=True)), L2_EPS)
    kmod = k * (1.0 + (asig - 1.0) * k_a)
    return lw, kmod, -kk, kk * asig


def _wkv_post(y, r, kmod, v, ln_w, ln_b, r_k):
    mu = jnp.mean(y, axis=-1, keepdims=True)
    var = jnp.mean(jnp.square(y - mu), axis=-1, keepdims=True)
    yn = (y - mu) * lax.rsqrt(var + GN_EPS)
    yn = yn * ln_w + ln_b
    return (yn + jnp.sum(r * kmod * r_k, axis=-1, keepdims=True) * v,)


def _attn_group(nonzero_block, q, kc, kp, vc, vp, cos_c, sin_c, cos_p, sin_p, q_gain, k_gain, sinks):
    ri = lax.broadcasted_iota(jnp.int32, (HEAD_DIM, HEAD_DIM), 0)
    ci = lax.broadcasted_iota(jnp.int32, (HEAD_DIM, HEAD_DIM), 1)
    half = HEAD_DIM // 2
    rot = jnp.where(ri == ci + half, -1.0, 0.0) + jnp.where(ri + half == ci, 1.0, 0.0)
    rows = Q_PER_KV * ATT_BLOCK

    def rope(x, cos, sin):
        return x * cos + _hdot(x, rot, _NN) * sin

    kcr = rope(_rms(kc, k_gain), cos_c, sin_c)
    kpr = rope(_rms(kp, k_gain), cos_p, sin_p)
    qn = _rms(q, q_gain)
    qr = qn * cos_c + _hdot(qn.reshape(rows, HEAD_DIM), rot, _NN).reshape(q.shape) * sin_c
    q2 = qr.reshape(rows, HEAD_DIM)
    qi = lax.broadcasted_iota(jnp.int32, (1, ATT_BLOCK, ATT_BLOCK), 1)
    ki = lax.broadcasted_iota(jnp.int32, (1, ATT_BLOCK, ATT_BLOCK), 2)
    mask_c = ki <= qi
    mask_p = jnp.logical_and(ki > qi, nonzero_block)
    lane0 = (lax.broadcasted_iota(jnp.int32, (1, 1, 128), 2) == 0).astype(F32)
    shape3 = (Q_PER_KV, ATT_BLOCK, ATT_BLOCK)
    sc = jnp.where(mask_c, (_bdot_nt(q2, kcr) * (HEAD_DIM ** -0.5)).reshape(shape3), MASK_VALUE)
    sp = jnp.where(mask_p, (_bdot_nt(q2, kpr) * (HEAD_DIM ** -0.5)).reshape(shape3), MASK_VALUE)
    sk = jnp.sum(sinks * lane0, axis=2, keepdims=True)
    mx = jnp.maximum(jnp.maximum(jnp.max(sc, axis=2, keepdims=True), jnp.max(sp, axis=2, keepdims=True)), sk)
    mx = lax.stop_gradient(mx)
    ec, ep = jnp.exp(sc - mx), jnp.exp(sp - mx)
    den = jnp.sum(ec, axis=2, keepdims=True) + jnp.sum(ep, axis=2, keepdims=True) + jnp.exp(sk - mx)
    out = _bdot((ec / den).reshape(rows, ATT_BLOCK), vc) + _bdot((ep / den).reshape(rows, ATT_BLOCK), vp)
    return out.reshape(q.shape)


def _attn_rows(nonzero_block, qs, kcs, kps, vcs, vps, tabs, q_gain, k_gain, sinks):
    return [_attn_group(nonzero_block, qs[g], kcs[g], kps[g], vcs[g], vps[g], *tabs, q_gain, k_gain, sinks[g])
            for g in range(N_KV_HEADS)]


def _attn_operands(q_tile, kvc_tile, kvp_tile, sink_ref):
    q3 = _split_heads(q_tile, N_HEADS)
    kvc, kvp = _split_heads(kvc_tile, 2 * N_KV_HEADS), _split_heads(kvp_tile, 2 * N_KV_HEADS)
    groups = range(N_KV_HEADS)
    qs = [q3[Q_PER_KV * g:Q_PER_KV * (g + 1)] for g in groups]
    sinks = [sink_ref[Q_PER_KV * g:Q_PER_KV * (g + 1)] for g in groups]
    return (qs, [kvc[g] for g in groups], [kvp[g] for g in groups], [kvc[N_KV_HEADS + g] for g in groups],
            [kvp[N_KV_HEADS + g] for g in groups], sinks)


def _attn_specs(t):
    nb = t // ATT_BLOCK
    prev = lambda n: jnp.maximum(n - 1, 0)
    kv_width = 2 * N_KV_HEADS * HEAD_DIM
    q_spec = pl.BlockSpec((ATT_BLOCK, D_MODEL), lambda n: (n, 0))
    kv_c = pl.BlockSpec((ATT_BLOCK, kv_width), lambda n: (n, 0))
    kv_p = pl.BlockSpec((ATT_BLOCK, kv_width), lambda n: (prev(n), 0))
    tab_c = pl.BlockSpec((ATT_BLOCK, HEAD_DIM), lambda n: (n, 0))
    tab_p = pl.BlockSpec((ATT_BLOCK, HEAD_DIM), lambda n: (prev(n), 0))
    gain = pl.BlockSpec((1, HEAD_DIM), lambda n: (0, 0))
    sink = pl.BlockSpec((N_HEADS, 1, 128), lambda n: (0, 0, 0))
    return nb, q_spec, kv_c, kv_p, tab_c, tab_p, gain, sink


def _attn_fwd(q, kv, cos, sin, q_gain, k_gain, sinks):
    t = q.shape[0]
    nb, q_spec, kv_c, kv_p, tab_c, tab_p, gain, sink = _attn_specs(t)

    def body(q_ref, kvc, kvp, cc, sc, cp, sp, qg, kg, sk, o_ref):
        qs, kcs, kps, vcs, vps, sinks_ = _attn_operands(q_ref[...], kvc[...], kvp[...], sk)
        outs = _attn_rows(pl.program_id(0) > 0, qs, kcs, kps, vcs, vps, (cc[...], sc[...], cp[...], sp[...]),
                          qg[...], kg[...], sinks_)
        o_ref[...] = _merge_heads(jnp.concatenate(outs, axis=0)).astype(o_ref.dtype)

    return pl.pallas_call(
        body, grid=(nb,), in_specs=[q_spec, kv_c, kv_p, tab_c, tab_c, tab_p, tab_p, gain, gain, sink],
        out_specs=q_spec, out_shape=_sds(q.shape, BF16), compiler_params=_params(("arbitrary",)), name="attn_fwd",
    )(q, kv, kv, cos, sin, cos, sin, q_gain, k_gain, sinks)


def _attn_bwd(q, kv, cos, sin, q_gain, k_gain, sinks, do):
    t = q.shape[0]
    nb, q_spec, kv_c, kv_p, tab_c, tab_p, gain, sink = _attn_specs(t)

    def body(q_ref, kvc, kvp, cc, sc, cp, sp, qg, kg, sk, do_ref, dq_ref, dkvc_ref, dkvp_ref, dqg_ref, dkg_ref, dsk_ref):
        nonzero = pl.program_id(0) > 0
        tabs = (cc[...], sc[...], cp[...], sp[...])
        qs, kcs, kps, vcs, vps, sinks_ = _attn_operands(q_ref[...], kvc[...], kvp[...], sk)

        def f(qs_, kcs_, kps_, vcs_, vps_, qgv, kgv, sks):
            return _attn_rows(nonzero, qs_, kcs_, kps_, vcs_, vps_, tabs, qgv, kgv, sks)

        _, pull = jax.vjp(f, qs, kcs, kps, vcs, vps, qg[...], kg[...], sinks_)
        do3 = _split_heads(do_ref[...], N_HEADS)
        dqs, dkcs, dkps, dvcs, dvps, dqg, dkg, dsks = pull([do3[Q_PER_KV * g:Q_PER_KV * (g + 1)] for g in range(N_KV_HEADS)])
        dq_ref[...] = _merge_heads(jnp.concatenate(dqs, axis=0)).astype(dq_ref.dtype)
        dkvc_ref[...] = jnp.concatenate(dkcs + dvcs, axis=1)
        dkvp_ref[...] = jnp.concatenate(dkps + dvps, axis=1)

        @pl.when(pl.program_id(0) == 0)
        def _():
            dqg_ref[...] = jnp.zeros_like(dqg_ref)
            dkg_ref[...] = jnp.zeros_like(dkg_ref)
            dsk_ref[...] = jnp.zeros_like(dsk_ref)

        dqg_ref[...] += dqg
        dkg_ref[...] += dkg
        for g in range(N_KV_HEADS):
            dsk_ref[Q_PER_KV * g:Q_PER_KV * (g + 1)] += dsks[g]

    return pl.pallas_call(
        body, grid=(nb,), in_specs=[q_spec, kv_c, kv_p, tab_c, tab_c, tab_p, tab_p, gain, gain, sink, q_spec],
        out_specs=[q_spec, kv_c, kv_c, gain, gain, sink],
        out_shape=[_sds(q.shape, BF16), _sds(kv.shape), _sds(kv.shape), _sds((1, HEAD_DIM)), _sds((1, HEAD_DIM)), _sds(sinks.shape)],
        compiler_params=_params(("arbitrary",)), name="attn_bwd",
    )(q, kv, kv, cos, sin, cos, sin, q_gain, k_gain, sinks, do)


def _time_shift_lerps(x, xs, gain, *mix):
    xn, xsn = _rms(x, gain), _rms(xs, gain)
    xx = xsn - xn
    return tuple(xn + xx * m for m in mix)


def _residual_norm(h, delta, gain):
    hn = h + delta
    return hn, _rms(hn, gain)


def _residual_norm2(h, delta, gain_a, gain_b):
    hn = h + delta
    return hn, _rms(hn, gain_a), _rms(hn, gain_b)


def _relu2(u):
    return jnp.square(jnp.maximum(u, 0.0))


def _sigmoid(z):
    return jax.nn.sigmoid(z)


def _shift_down(x):
    return jnp.pad(x[:-1], ((1, 0), (0, 0)))


def _shift_up(x):
    return jnp.pad(x[1:], ((0, 1), (0, 0)))


def _rope_tables(t):
    half = HEAD_DIM // 2
    inv_freq = jnp.power(ROPE_THETA, -jnp.arange(half, dtype=F32) / half)
    ang = jnp.arange(t, dtype=jnp.int32).astype(F32)[:, None] * inv_freq[None, :]
    cos, sin = jnp.cos(ang), jnp.sin(ang)
    return jnp.concatenate([cos, cos], axis=1), jnp.concatenate([sin, sin], axis=1)


def _mlp_fwd(hn, w_up, w_down, layer, up_dev_major):
    t = hn.shape[0]
    if up_dev_major:
        cw = w_up.shape[2]
        u = _mm(hn, w_up, name=f"mlp{layer}_up", dims=(t, D_FF, D_MODEL), tn=cw, tk=D_MODEL,
                b_spec=pl.BlockSpec((None, D_MODEL, cw), lambda i, j, q: (j, q, 0)))
    else:
        u = _mm(hn, w_up, name=f"mlp{layer}_up")
    out = _mm(u, w_down, a_pro=_relu2, name=f"mlp{layer}_down")
    return u, out


def _mm_pair_reduce(a, b, *, device_axis, tile, a_pro=None, name):
    t = a.shape[0]
    tr, tc = tile
    other = (b.shape[1] // tc) if device_axis == "m" else (a.shape[1] // tr)
    n_tiles = N_CHIPS * other
    core = lax.axis_index("c").astype(jnp.int32).reshape(1)
    shape = (N_CHIPS, tr, other * tc) if device_axis == "m" else (N_CHIPS, other * tr, tc)

    def body(core_ref, a_ref, b_ref, out_ref, recv_hbm, send_buf, recv_tile, send_sems, recv_sems, local_sem):
        phase, q, o = pl.program_id(0), pl.program_id(1), pl.program_id(2)
        sibling = (lax.axis_index("x"), lax.axis_index("y"), 1 - core_ref[0])
        tile_no = q * other + o
        slot = lax.rem(tile_no, 2)

        def send(number, buffer):
            return pltpu.make_async_remote_copy(
                src_ref=send_buf.at[buffer], dst_ref=recv_hbm.at[number // other, lax.rem(number, other)],
                send_sem=send_sems.at[number], recv_sem=recv_sems.at[number], device_id=sibling, device_id_type=_MESH_ID)

        def landed():
            return pltpu.make_async_copy(recv_hbm.at[q, o], recv_tile, local_sem)

        @pl.when(phase == 1)
        def _():
            @pl.when(tile_no == 0)
            def _():
                send(n_tiles - 2, lax.rem(n_tiles - 2, 2)).wait_send()
                send(n_tiles - 1, lax.rem(n_tiles - 1, 2)).wait_send()

            send(tile_no, slot).wait_recv()
            landed().start()

        av = a_ref[...]
        if a_pro is not None:
            av = a_pro(av.astype(F32))
        acc = lax.dot_general(av.astype(BF16), b_ref[...].astype(BF16), _TN, preferred_element_type=F32)

        @pl.when(phase == 0)
        def _():
            @pl.when(tile_no >= 2)
            def _():
                send(tile_no - 2, slot).wait_send()

            send_buf[slot] = acc.astype(send_buf.dtype)
            send(tile_no, slot).start()

        @pl.when(phase == 1)
        def _():
            landed().wait()
            out_ref[...] = (acc + recv_tile[...].astype(F32)).astype(out_ref.dtype)

    def device(phase, q, core_ref):
        return 2 * q + jnp.where(phase == 0, 1 - core_ref[0], core_ref[0])

    parked = lambda p, i: jnp.where(p == 0, 0, i)
    if device_axis == "m":
        a_spec = pl.BlockSpec((t, tr), lambda p, q, o, c: (0, device(p, q, c)))
        b_spec = pl.BlockSpec((t, tc), lambda p, q, o, c: (0, o))
        o_spec = pl.BlockSpec((None, tr, tc), lambda p, q, o, c: (parked(p, q), 0, parked(p, o)))
    else:
        a_spec = pl.BlockSpec((t, tr), lambda p, q, o, c: (0, o))
        b_spec = pl.BlockSpec((t, tc), lambda p, q, o, c: (0, device(p, q, c)))
        o_spec = pl.BlockSpec((None, tr, tc), lambda p, q, o, c: (parked(p, q), parked(p, o), 0))
    grid_spec = pltpu.PrefetchScalarGridSpec(
        num_scalar_prefetch=1, grid=(2, N_CHIPS, other), in_specs=[a_spec, b_spec], out_specs=[o_spec, _ANY],
        scratch_shapes=[pltpu.VMEM((2, tr, tc), BF16), pltpu.VMEM((tr, tc), BF16),
                        pltpu.SemaphoreType.DMA((n_tiles,)), pltpu.SemaphoreType.DMA((n_tiles,)), pltpu.SemaphoreType.DMA(())])
    sums, _ = pl.pallas_call(
        body, grid_spec=grid_spec, out_shape=[_sds(shape, BF16), _sds((N_CHIPS, other, tr, tc), BF16)],
        compiler_params=_params(("arbitrary", "arbitrary", "arbitrary")), name=name,
    )(core, a, b)
    return sums


MLP_PAIR_TILE = (1024, 1024)
SQUARE_PAIR_TILE = (D_MODEL // N_DEV, D_MODEL)


def _mlp_bwd(hn, u, dh, w_up, w_down, layer, up_dev_major, pair_reduce=False):
    t = hn.shape[0]
    du = _mm(dh, w_down, tb=True, epi=lambda r, uu: r * (2.0 * jnp.maximum(uu, 0.0)), epi_args=(u,), out_dtype=BF16,
             name=f"mlp{layer}_du")
    if pair_reduce:
        cw = w_up.shape[2]
        d_down = _mm_pair_reduce(u, dh, device_axis="m", tile=MLP_PAIR_TILE, a_pro=_relu2, name=f"mlp{layer}_ddown")
        d_up = _mm_pair_reduce(hn, du, device_axis="n", tile=MLP_PAIR_TILE, name=f"mlp{layer}_dup")
        dhn = _mm(du, w_up, tb=True, name=f"mlp{layer}_dhn", dims=(t, D_MODEL, D_FF), tk=cw,
                  b_spec=pl.BlockSpec((None, MM_TILE_N, cw), lambda i, j, q: (q, j, 0)))
        return dhn, d_up, d_down
    d_down = _mm(u, dh, ta=True, a_pro=_relu2, name=f"mlp{layer}_ddown")
    if up_dev_major:
        cw = w_up.shape[2]
        d_up = _mm(hn, du, ta=True, name=f"mlp{layer}_dup", dims=(D_MODEL, D_FF, t), tn=cw,
                   o_spec=pl.BlockSpec((None, MM_TILE_M, cw), lambda i, j, q: (j, i, 0)), o_shape=(N_DEV, D_MODEL, cw))
        dhn = _mm(du, w_up, tb=True, name=f"mlp{layer}_dhn", dims=(t, D_MODEL, D_FF), tk=cw,
                  b_spec=pl.BlockSpec((None, MM_TILE_N, cw), lambda i, j, q: (q, j, 0)))
    else:
        d_up = _mm(hn, du, ta=True, name=f"mlp{layer}_dup")
        dhn = _mm(du, w_up, tb=True, name=f"mlp{layer}_dhn")
    return dhn, d_up, d_down


_LATE_WEIGHTS = ("mlp_w_up0", "mlp_w_up1", "mlp_w_down0", "mlp_w_down1", "b_w_q", "b_w_o")
_LATE_GATHER = _LATE_WEIGHTS + ("a_w_out", "a_g1", "w_kv")
_EARLY_GRADS = _LATE_WEIGHTS + ("a_w_out",)
_RKV_GRADS = ("a_w_r", "a_w_k", "a_w_v")


def _local_step(x, target, w, late_shards=None, up_dev_major=True):
    t = x.shape[0]
    g = {}
    w = dict(w)
    row = lambda: _sds((t, D_MODEL))
    rowb = lambda: _sds((t, D_MODEL), BF16)
    vec = lambda: _sds((1, D_MODEL))

    xs = _shift_down(x)
    mix = [w["a_mix"][i:i + 1] for i in range(6)]
    xr, xk, xv, xw, xa, xg = _rowwise(_time_shift_lerps, [x, xs], [w["a_norm"]] + mix, [rowb()] * 6, tm=256, name="tmix_lerp")
    r = _mm(xr, w["a_w_r"], name="tmix_r")
    k = _mm(xk, w["a_w_k"], name="tmix_k")
    v = _mm(xv, w["a_w_v"], name="tmix_v")
    lw1 = _mm(xw, w["a_w1"], name="tmix_w1")
    wl = _mm(lw1, w["a_w2"], a_pro=jnp.tanh, name="tmix_w2")
    la1 = _mm(xa, w["a_a1"], name="tmix_a1")
    al = _mm(la1, w["a_a2"], name="tmix_a2")

    hv = lambda name: w[name].reshape(N_HEADS, 1, HEAD_DIM)
    wkv_params = [hv(name) for name in ("a_w0", "a_a0", "a_k_k", "a_k_a", "a_ln_x_w", "a_ln_x_b", "a_r_k")]
    y2, states, gathered = _wkv_fwd(r, k, v, wl, al, wkv_params, late_shards or ())
    for name, arr in zip(_LATE_GATHER, gathered):
        w[name] = arr if name.startswith("mlp_w_up") else arr.reshape(N_DEV * arr.shape[1], arr.shape[2])
    lg1 = _mm(xg, w["a_g1"], name="tmix_g1")
    gate = _mm(lg1, w["a_g2"], a_pro=_sigmoid, name="tmix_g2")
    (yg,) = _rowwise(lambda a, b: (a * b,), [y2, gate], [], [rowb()], name="tmix_gate")
    att = _mm(yg, w["a_w_out"], name="tmix_out")

    h1, hn0 = _rowwise(_residual_norm, [x, att], [w["mlp_norm0"]], [row(), rowb()], name="res_norm0")
    u0, m0 = _mlp_fwd(hn0, w["mlp_w_up0"], w["mlp_w_down0"], 0, up_dev_major)

    h2, kvn, qn = _rowwise(_residual_norm2, [h1, m0], [w["kv_norm"], w["b_norm"]], [row(), rowb(), rowb()], name="res_norm_kvq")
    kv = _mm(kvn, w["w_kv"], name="kv_proj")
    q = _mm(qn, w["b_w_q"], name="q_proj")
    cos, sin = _rope_tables(t)
    sinks = jnp.broadcast_to(w["b_sinks"].reshape(N_HEADS, 1, 1), (N_HEADS, 1, 128))
    o = _attn_fwd(q, kv, cos, sin, w["b_q_norm"], w["k_norm"], sinks)
    att2 = _mm(o, w["b_w_o"], name="attn_out")

    h3, hn1 = _rowwise(_residual_norm, [h2, att2], [w["mlp_norm1"]], [row(), rowb()], name="res_norm1")
    u1, m1 = _mlp_fwd(hn1, w["mlp_w_up1"], w["mlp_w_down1"], 1, up_dev_major)

    def loss_fn(h, m, tg):
        diff = (h + m) - tg
        part = 0.5 * jnp.sum(jnp.mean(jnp.square(diff), axis=-1, keepdims=True), axis=0, keepdims=True)
        dh = diff * (1.0 / D_MODEL)
        return dh, dh, jnp.broadcast_to(part, (1, 128))

    dh4, dh4b, loss = _rowwise(loss_fn, [h3, m1, target], [], [row(), rowb()], [_sds((1, 128))], name="loss")

    def res_norm_bwd(h, dnext, dhn, gain):
        dh, dgain = _vjp_of(lambda hh, gg: (_rms(hh, gg),), 2, (0, 1))(h, gain, dhn)
        return dnext + dh, dnext + dh, dgain

    distributed = late_shards is not None

    def square_dw(a_, b_, name):
        if distributed:
            return _mm_pair_reduce(a_, b_, device_axis="m", tile=SQUARE_PAIR_TILE, name=name)
        return _mm(a_, b_, ta=True, name=name)

    dhn1, g["mlp_w_up1"], g["mlp_w_down1"] = _mlp_bwd(hn1, u1, dh4b, w["mlp_w_up1"], w["mlp_w_down1"], 1, up_dev_major, distributed)
    dh3, dh3b, g["mlp_norm1"] = _rowwise(res_norm_bwd, [h3, dh4, dhn1], [w["mlp_norm1"]], [row(), rowb()], [vec()], name="res_norm1_bwd")

    g["b_w_o"] = square_dw(o, dh3b, "attn_out_dw")
    do = _mm(dh3b, w["b_w_o"], tb=True, name="attn_out_dx")
    dq, dkv_own, dkv_prev, g["b_q_norm"], g["k_norm"], dsinks = _attn_bwd(
        q, kv, cos, sin, w["b_q_norm"], w["k_norm"], sinks, do)
    g["b_sinks"] = dsinks[:, 0, 0].reshape(1, N_HEADS)
    g["b_w_q"] = square_dw(qn, dq, "q_proj_dw")
    dqn = _mm(dq, w["b_w_q"], tb=True, name="q_proj_dx")
    dkv_prev = jnp.pad(dkv_prev[ATT_BLOCK:], ((0, ATT_BLOCK), (0, 0)))
    (dkv,) = _rowwise(lambda a, b: (a + b,), [dkv_own, dkv_prev], [], [_sds(kv.shape, BF16)], name="kv_grad_sum")
    g["w_kv"] = _mm(kvn, dkv, ta=True, name="kv_proj_dw")
    dkvn = _mm(dkv, w["w_kv"], tb=True, name="kv_proj_dx")

    def res_norm2_bwd(h, dnext, dna, dnb, gain_a, gain_b):
        dha, dga = _vjp_of(lambda hh, gg: (_rms(hh, gg),), 2, (0, 1))(h, gain_a, dna)
        dhb, dgb = _vjp_of(lambda hh, gg: (_rms(hh, gg),), 2, (0, 1))(h, gain_b, dnb)
        return dnext + dha + dhb, dnext + dha + dhb, dga, dgb

    dh2, dh2b, g["kv_norm"], g["b_norm"] = _rowwise(res_norm2_bwd, [h2, dh3, dkvn, dqn], [w["kv_norm"], w["b_norm"]],
                                                    [row(), rowb()], [vec(), vec()], name="res_norm_kvq_bwd")

    dhn0, g["mlp_w_up0"], g["mlp_w_down0"] = _mlp_bwd(hn0, u0, dh2b, w["mlp_w_up0"], w["mlp_w_down0"], 0, up_dev_major, distributed)
    dh1, dh1b, g["mlp_norm0"] = _rowwise(res_norm_bwd, [h1, dh2, dhn0], [w["mlp_norm0"]], [row(), rowb()], [vec()], name="res_norm0_bwd")

    g["a_w_out"] = square_dw(yg, dh1b, "tmix_out_dw")
    dyg = _mm(dh1b, w["a_w_out"], tb=True, name="tmix_out_dx")
    dy2, dgate = _rowwise(lambda d, a, b: (d * b, d * a), [dyg, y2, gate], [], [row(), rowb()], name="tmix_gate_bwd")
    g["a_g2"] = _mm(lg1, dgate, ta=True, a_pro=_sigmoid, name="tmix_g2_dw")

    def dsigmoid(rr, z):
        s = jax.nn.sigmoid(z)
        return rr * s * (1.0 - s)

    dlg1 = _mm(dgate, w["a_g2"], tb=True, epi=dsigmoid, epi_args=(lg1,), out_dtype=BF16, name="tmix_g2_dx")
    g["a_g1"] = _mm(xg, dlg1, ta=True, name="tmix_g1_dw")
    dxg = _mm(dlg1, w["a_g1"], tb=True, name="tmix_g1_dx")

    early_sums = [g[name] for name in _EARLY_GRADS] if distributed else ()
    (dr, dk, dv, dwl, dal), param_grads, early_reduced = _wkv_bwd(r, k, v, wl, al, wkv_params, states, dy2, early_sums)
    for name, pg in zip(("a_w0", "a_a0", "a_k_k", "a_k_a", "a_ln_x_w", "a_ln_x_b", "a_r_k"), param_grads):
        g[name] = pg.reshape(1, D_MODEL)

    g["a_w_r"] = square_dw(xr, dr, "tmix_r_dw")
    g["a_w_k"] = square_dw(xk, dk, "tmix_k_dw")
    g["a_w_v"] = square_dw(xv, dv, "tmix_v_dw")
    dxr = _mm(dr, w["a_w_r"], tb=True, name="tmix_r_dx")
    dxk = _mm(dk, w["a_w_k"], tb=True, name="tmix_k_dx")
    dxv = _mm(dv, w["a_w_v"], tb=True, name="tmix_v_dx")
    g["a_w2"] = _mm(lw1, dwl, ta=True, a_pro=jnp.tanh, name="tmix_w2_dw")

    def dtanh(rr, z):
        th = jnp.tanh(z)
        return rr * (1.0 - th * th)

    dlw1 = _mm(dwl, w["a_w2"], tb=True, epi=dtanh, epi_args=(lw1,), out_dtype=BF16, name="tmix_w2_dx")
    g["a_w1"] = _mm(xw, dlw1, ta=True, name="tmix_w1_dw")
    dxw = _mm(dlw1, w["a_w1"], tb=True, name="tmix_w1_dx")
    g["a_a2"] = _mm(la1, dal, ta=True, name="tmix_a2_dw")
    dla1 = _mm(dal, w["a_a2"], tb=True, out_dtype=BF16, name="tmix_a2_dx")
    g["a_a1"] = _mm(xa, dla1, ta=True, name="tmix_a1_dw")
    dxa = _mm(dla1, w["a_a1"], tb=True, name="tmix_a1_dx")

    lerp_bwd = _vjp_of(_time_shift_lerps, 9, tuple(range(9)))

    def lerp_bwd_rows(x_, xs_, d0, d1, d2, d3, d4, d5, gain, *mx):
        return lerp_bwd(x_, xs_, gain, *mx, d0, d1, d2, d3, d4, d5)

    rkv_sums = [g[name] for name in _RKV_GRADS] if distributed else []
    outs = _rowwise(lerp_bwd_rows, [x, xs, dxr, dxk, dxv, dxw, dxa, dxg], [w["a_norm"]] + mix, [row(), row()], [vec()] * 7,
                    tm=128, name="tmix_lerp_bwd", chip_sums=rkv_sums)
    dx_a, dxs, g["a_norm"] = outs[0], outs[1], outs[2]
    g["a_mix"] = jnp.concatenate(outs[3:9], axis=0)
    (grad_x,) = _rowwise(lambda a, b, c: (a + b + c,), [dh1, dx_a, _shift_up(dxs)], [], [row()], name="grad_x_sum")
    reduced = dict(zip(_EARLY_GRADS + _RKV_GRADS, list(early_reduced) + list(outs[9:]))) if distributed else None
    return loss, grad_x, g, reduced


_ANY = pl.BlockSpec(memory_space=pl.ANY)
_MESH_ID = pl.DeviceIdType.MESH


def _linear(pos):
    return 4 * pos[0] + 2 * pos[1] + pos[2]


def _all_gather(shards, name):
    n = len(shards)

    def body(*refs):
        start, relay, finish = _gather_plan(refs[:n], refs[n:2 * n], *refs[2 * n:])
        start()
        relay()
        finish()

    return pl.pallas_call(
        body, out_shape=[_sds((N_DEV,) + s.shape, s.dtype) for s in shards], in_specs=[_ANY] * n, out_specs=[_ANY] * n,
        scratch_shapes=_gather_semaphores(n), name=name,
    )(*shards)


GATHER_COPIES = 8


def _gather_semaphores(n):
    return [pltpu.SemaphoreType.DMA((n * GATHER_COPIES,)), pltpu.SemaphoreType.DMA((n * GATHER_COPIES,)),
            pltpu.SemaphoreType.DMA((n,))]


def _gather_plan(ins, outs, send_sems, recv_sems, local_sems):
    n = len(ins)
    x, y, c = lax.axis_index("x"), lax.axis_index("y"), lax.axis_index("c")
    me, sibling = (x, y, c), (x, y, 1 - c)
    x_nbr, y_nbr, diag = (1 - x, y, c), (x, 1 - y, c), (1 - x, 1 - y, c)
    other = lambda pos: (pos[0], pos[1], 1 - c)

    def halves(a):
        rows = ins[a].shape[0]
        if rows % (2 * BF16_SUBLANES):
            return (0, rows), None
        return (0, rows // 2), (rows // 2, rows // 2)

    def copy(a, k, block, to, src=None, rows=None):
        dst = outs[a].at[_linear(block)]
        if rows is not None:
            dst = dst.at[pl.ds(rows[0], rows[1])]
        return pltpu.make_async_remote_copy(
            src_ref=dst if src is None else src, dst_ref=dst, send_sem=send_sems.at[a * GATHER_COPIES + k],
            recv_sem=recv_sems.at[a * GATHER_COPIES + k], device_id=to, device_id_type=_MESH_ID)

    def own_copies():
        mine = [pltpu.make_async_copy(ins[a], outs[a].at[_linear(me)], local_sems.at[a]) for a in range(n)]
        sent = []
        for a in range(n):
            sent += [copy(a, 0, me, sibling, src=ins[a]), copy(a, 1, me, x_nbr, src=ins[a]), copy(a, 2, me, y_nbr, src=ins[a])]
        return mine, sent

    def relayed_copies():
        sent = []
        for a in range(n):
            first, second = halves(a)
            sent += [copy(a, 3, x_nbr, y_nbr, rows=first), copy(a, 5, x_nbr, sibling), copy(a, 6, y_nbr, sibling)]
            if second is not None:
                sent.append(copy(a, 4, y_nbr, x_nbr, rows=second))
        return sent

    def start():
        mine, sent = own_copies()
        for cp in mine + sent:
            cp.start()

    def relay():
        for a in range(n):
            first, second = halves(a)
            copy(a, 1, x_nbr, me).wait_recv()
            copy(a, 3, x_nbr, y_nbr, rows=first).start()
            copy(a, 5, x_nbr, sibling).start()
        for a in range(n):
            first, second = halves(a)
            copy(a, 2, y_nbr, me).wait_recv()
            if second is not None:
                copy(a, 4, y_nbr, x_nbr, rows=second).start()
            copy(a, 6, y_nbr, sibling).start()

    def finish():
        mine, sent = own_copies()
        sent += relayed_copies()
        for a in range(n):
            first, second = halves(a)
            copy(a, 3, diag, me, rows=first).wait_recv()
            if second is not None:
                copy(a, 4, diag, me, rows=second).wait_recv()
            last = copy(a, 7, diag, sibling)
            last.start()
            sent.append(last)
        for a in range(n):
            copy(a, 0, other(me), me).wait_recv()
            copy(a, 5, other(x_nbr), me).wait_recv()
            copy(a, 6, other(y_nbr), me).wait_recv()
            copy(a, 7, other(diag), me).wait_recv()
        for cp in sent:
            cp.wait_send()
        for cp in mine:
            cp.wait()

    return start, relay, finish


N_CHIPS = 4


def _exchange_with_sibling(parts, name):
    n = len(parts)

    def body(*refs):
        ins, outs = refs[:n], refs[n:2 * n]
        send_sems, recv_sems = refs[2 * n:]
        x, y, c = lax.axis_index("x"), lax.axis_index("y"), lax.axis_index("c")
        copies = [pltpu.make_async_remote_copy(
            src_ref=ins[a].at[2 * q + (1 - c)], dst_ref=outs[a].at[q], send_sem=send_sems.at[a * N_CHIPS + q],
            recv_sem=recv_sems.at[a * N_CHIPS + q], device_id=(x, y, 1 - c), device_id_type=_MESH_ID)
            for a in range(n) for q in range(N_CHIPS)]
        for cp in copies:
            cp.start()
        for cp in copies:
            cp.wait()

    return pl.pallas_call(
        body, out_shape=[_sds((N_CHIPS,) + p.shape[1:], p.dtype) for p in parts], in_specs=[_ANY] * n, out_specs=[_ANY] * n,
        scratch_shapes=[pltpu.SemaphoreType.DMA((n * N_CHIPS,)), pltpu.SemaphoreType.DMA((n * N_CHIPS,))], name=name,
    )(*parts)


def _pair_sum(part, recv, core, out_dtype, name):
    _, r, cdim = recv.shape
    tr = max(8, min(r, STREAM_BLOCK_ELEMENTS // cdim))
    assert r % tr == 0, (name, r, tr)

    def body(core_ref, p_ref, r_ref, o_ref):
        o_ref[...] = (p_ref[...] + r_ref[...]).astype(o_ref.dtype)

    grid_spec = pltpu.PrefetchScalarGridSpec(
        num_scalar_prefetch=1, grid=(N_CHIPS, r // tr),
        in_specs=[pl.BlockSpec((None, None, tr, cdim), lambda q, i, core_ref: (q, core_ref[0], i, 0)),
                  pl.BlockSpec((None, tr, cdim), lambda q, i, core_ref: (q, i, 0))],
        out_specs=pl.BlockSpec((None, tr, cdim), lambda q, i, core_ref: (q, i, 0)))
    return pl.pallas_call(
        body, grid_spec=grid_spec, out_shape=_sds((N_CHIPS, r, cdim), out_dtype),
        compiler_params=_params(("parallel", "parallel")), name=name,
    )(core, part.reshape(N_CHIPS, 2, r, cdim), recv)


def _exchange_between_chips(parts, name):
    n = len(parts)

    def body(*refs):
        start, finish = _chip_exchange_plan(refs[:n], refs[n:2 * n], *refs[2 * n:])
        start()
        finish()

    return pl.pallas_call(
        body, out_shape=[_sds(p.shape, p.dtype) for p in parts], in_specs=[_ANY] * n, out_specs=[_ANY] * n,
        scratch_shapes=_chip_exchange_semaphores(n), name=name,
    )(*parts)


def _chip_exchange_semaphores(n):
    n_other = N_CHIPS - 1
    return [pltpu.SemaphoreType.DMA((n * n_other,)), pltpu.SemaphoreType.DMA((n * n_other,)), pltpu.SemaphoreType.DMA((n,))]


def _chip_exchange_plan(ins, outs, send_sems, recv_sems, local_sems):
    n = len(ins)
    n_other = N_CHIPS - 1
    x, y, c = lax.axis_index("x"), lax.axis_index("y"), lax.axis_index("c")
    my_chip = 2 * x + y

    def all_copies():
        mine = [pltpu.make_async_copy(ins[a].at[my_chip], outs[a].at[my_chip], local_sems.at[a]) for a in range(n)]
        remote = []
        for j, (fx, fy) in enumerate([(1, 0), (0, 1), (1, 1)]):
            px, py = (1 - x if fx else x), (1 - y if fy else y)
            for a in range(n):
                remote.append(pltpu.make_async_remote_copy(
                    src_ref=ins[a].at[2 * px + py], dst_ref=outs[a].at[my_chip], send_sem=send_sems.at[a * n_other + j],
                    recv_sem=recv_sems.at[a * n_other + j], device_id=(px, py, c), device_id_type=_MESH_ID))
        return mine, remote

    def start():
        mine, remote = all_copies()
        for cp in mine + remote:
            cp.start()

    def finish():
        mine, remote = all_copies()
        for cp in remote + mine:
            cp.wait()

    return start, finish


def _chip_sums(parts, names, tag):
    from_sibling = _exchange_with_sibling(parts, name="scatter_grads_sibling_" + tag)
    core = lax.axis_index("c").astype(jnp.int32).reshape(1)
    return [_pair_sum(p, r, core, F32 if nm.startswith("pack") else BF16, name="pair_sum_" + nm)
            for p, r, nm in zip(parts, from_sibling, names)]


def _adamw(w, m, v, slots, name, layer=0, n_layers=1, into=None):
    r, c = w.shape[-2:]
    ns = slots.shape[0]
    tr = max(8, min(r, STREAM_BLOCK_ELEMENTS // c))
    assert r % tr == 0, (name, r, tr)

    def body(w_ref, m_ref, v_ref, g_ref, *rest):
        g_out, d_out, m_out, v_out = rest[-4:]
        g = g_ref[0].astype(F32)
        for s in range(1, ns):
            g = g + g_ref[s].astype(F32)
        m_new = ADAM_B1 * m_ref[...] + (1.0 - ADAM_B1) * g
        v_new = ADAM_B2 * v_ref[...] + (1.0 - ADAM_B2) * jnp.square(g)
        m_hat = m_new / (1.0 - ADAM_B1 ** ADAM_STEP)
        v_hat = v_new / (1.0 - ADAM_B2 ** ADAM_STEP)
        d_out[...] = -ADAM_LR * (m_hat / (jnp.sqrt(v_hat) + ADAM_EPS) + ADAM_WD * w_ref[...])
        g_out[...], m_out[...], v_out[...] = g, m_new, v_new

    if n_layers == 1:
        spec, out_shape, earlier, aliases = pl.BlockSpec((tr, c), lambda i: (i, 0)), _sds((r, c)), [], {}
    else:
        spec, out_shape = pl.BlockSpec((None, tr, c), lambda i: (layer, i, 0)), _sds((n_layers, r, c))
        earlier = list(into) if into is not None else []
        aliases = {4 + i: i for i in range(len(earlier))}
    in_specs = [spec, spec, spec, pl.BlockSpec((ns, tr, c), lambda i: (0, i, 0))]
    return pl.pallas_call(
        body, grid=(r // tr,), in_specs=in_specs + [_ANY] * len(earlier), out_specs=[spec] * 4, out_shape=[out_shape] * 4,
        input_output_aliases=aliases, compiler_params=_params(("parallel",)), name=name,
    )(w, m, v, slots, *earlier)


_COL_VECTORS = ("a_norm", "a_mix", "a_w0", "a_a0", "a_k_k", "a_k_a", "a_ln_x_w", "a_ln_x_b")
_COL_VEC_ROWS = 16
_COL_ROWS = _COL_VEC_ROWS + 2 * LORA_PAD + 256
_ROW_COLS = 2 * LORA_PAD + 256 + 512
_REPL_ROWS = 8


def _pad_to(a, size, axis):
    widths = [(0, 0)] * a.ndim
    widths[axis] = (0, size - a.shape[axis])
    return jnp.pad(a, widths)


def _pack_cols(p):
    width = p["a_norm"].shape[-1]
    vecs = jnp.concatenate([p[n].reshape(-1, width) for n in _COL_VECTORS], axis=0)
    return jnp.concatenate([_pad_to(vecs, _COL_VEC_ROWS, 0), _pad_to(p["a_w2"].reshape(-1, width), LORA_PAD, 0),
                            _pad_to(p["a_a2"].reshape(-1, width), LORA_PAD, 0), p["a_g2"].reshape(-1, width)], axis=0)


def _unpack_cols(a, lead):
    width = a.shape[-1]
    out, row = {}, 0
    for n in _COL_VECTORS:
        k = 6 if n == "a_mix" else 1
        out[n] = a[row:row + k].reshape(lead + ((6, width) if n == "a_mix" else (width,)))
        row += k
    base = _COL_VEC_ROWS
    out["a_w2"] = a[base:base + 96].reshape(lead + (96, width))
    out["a_a2"] = a[base + LORA_PAD:base + LORA_PAD + 96].reshape(lead + (96, width))
    out["a_g2"] = a[base + 2 * LORA_PAD:].reshape(lead + (256, width))
    return out


def _pack_rows(p):
    rows = p["w_kv"].shape[0]
    return jnp.concatenate([_pad_to(p["a_w1"].reshape(rows, -1), LORA_PAD, 1), _pad_to(p["a_a1"].reshape(rows, -1), LORA_PAD, 1),
                            p["a_g1"].reshape(rows, -1), p["w_kv"]], axis=1)


def _unpack_rows(a, lead):
    rows = a.shape[0]
    return {"a_w1": a[:, :96].reshape(lead + (rows, 96)), "a_a1": a[:, LORA_PAD:LORA_PAD + 96].reshape(lead + (rows, 96)),
            "a_g1": a[:, 2 * LORA_PAD:2 * LORA_PAD + 256].reshape(lead + (rows, 256)), "w_kv": a[:, 2 * LORA_PAD + 256:]}


def _pack_repl(p):
    row = lambda a: _pad_to(a.reshape(1, -1), D_MODEL, 1)
    return jnp.concatenate([p["mlp_norm"].reshape(2, D_MODEL), row(p["kv_norm"]), row(p["b_norm"]), row(p["a_r_k"]),
                            row(p["k_norm"]), row(p["b_q_norm"]), row(p["b_sinks"])], axis=0)


def _unpack_repl(a):
    return {"mlp_norm": a[0:2], "kv_norm": a[2], "b_norm": a[3:4], "a_r_k": a[4].reshape(1, N_HEADS, HEAD_DIM),
            "k_norm": a[5, :HEAD_DIM], "b_q_norm": a[6:7, :HEAD_DIM], "b_sinks": a[7:8, :N_HEADS]}


_WEIGHTS = ("a_norm", "a_mix", "a_w_rkv", "a_w0", "a_w1", "a_w2", "a_a0", "a_a1", "a_a2", "a_g1", "a_g2", "a_k_k", "a_k_a",
            "a_r_k", "a_ln_x_w", "a_ln_x_b", "a_w_out", "mlp_norm", "mlp_w_up", "mlp_w_down", "kv_norm", "w_kv", "k_norm",
            "b_norm", "b_w_q", "b_q_norm", "b_sinks", "b_w_o")


def _big_shards(p):
    return [p["a_w_rkv"][0, 0], p["a_w_rkv"][0, 1], p["a_w_rkv"][0, 2], p["a_w_out"][0], p["mlp_w_up"][0], p["mlp_w_up"][1],
            p["mlp_w_down"][0], p["mlp_w_down"][1], p["b_w_q"][0], p["b_w_o"][0]]


_BIG_NAMES = ("a_w_r", "a_w_k", "a_w_v", "a_w_out", "mlp_w_up0", "mlp_w_up1", "mlp_w_down0", "mlp_w_down1", "b_w_q", "b_w_o")


def kernel(x, a_norm, a_mix, a_w_rkv, a_w0, a_w1, a_w2, a_a0, a_a1, a_a2, a_g1, a_g2, a_k_k, a_k_a, a_r_k, a_ln_x_w,
           a_ln_x_b, a_w_out, mlp_norm, mlp_w_up, mlp_w_down, kv_norm, w_kv, k_norm, b_norm, b_w_q, b_q_norm, b_sinks,
           b_w_o, loss_target, m_a_norm, m_a_mix, m_a_w_rkv, m_a_w0, m_a_w1, m_a_w2, m_a_a0, m_a_a1, m_a_a2, m_a_g1,
           m_a_g2, m_a_k_k, m_a_k_a, m_a_r_k, m_a_ln_x_w, m_a_ln_x_b, m_a_w_out, m_mlp_norm, m_mlp_w_up, m_mlp_w_down,
           m_kv_norm, m_w_kv, m_k_norm, m_b_norm, m_b_w_q, m_b_q_norm, m_b_sinks, m_b_w_o, v_a_norm, v_a_mix, v_a_w_rkv,
           v_a_w0, v_a_w1, v_a_w2, v_a_a0, v_a_a1, v_a_a2, v_a_g1, v_a_g2, v_a_k_k, v_a_k_a, v_a_r_k, v_a_ln_x_w,
           v_a_ln_x_b, v_a_w_out, v_mlp_norm, v_mlp_w_up, v_mlp_w_down, v_kv_norm, v_w_kv, v_k_norm, v_b_norm, v_b_w_q,
           v_b_q_norm, v_b_sinks, v_b_w_o):
    given = locals()
    wts = {n: given[n] for n in _WEIGHTS}
    mom = {n: given["m_" + n] for n in _WEIGHTS}
    var = {n: given["v_" + n] for n in _WEIGHTS}

    cols_w, rows_w, repl_w = _pack_cols(wts), _pack_rows(wts), _pack_repl(wts)
    big_w = _big_shards(wts)
    big_bf16 = dict(zip(_BIG_NAMES, [b.astype(BF16) for b in big_w]))
    first_names = [k for k in _BIG_NAMES if k not in _LATE_GATHER]
    lora_in = rows_w[:, :2 * LORA_PAD]
    gathered = _all_gather([cols_w, lora_in] + [big_bf16[k] for k in first_names], name="gather_weights")
    full_cols = gathered[0].transpose(1, 0, 2).reshape(_COL_ROWS, D_MODEL)
    full_lora_in = gathered[1].reshape(D_MODEL, 2 * LORA_PAD)
    w = {}
    w.update({k: v.reshape(v.shape[1:]) for k, v in _unpack_cols(full_cols, (1,)).items()})
    for k in ("a_norm", "a_w0", "a_a0", "a_k_k", "a_k_a", "a_ln_x_w", "a_ln_x_b"):
        w[k] = w[k].reshape(1, D_MODEL)
    for k in ("a_w2", "a_a2"):
        w[k] = _pad_to(w[k], LORA_PAD, 0)
    w["a_w1"], w["a_a1"] = full_lora_in[:, :LORA_PAD], full_lora_in[:, LORA_PAD:]
    for k, arr in zip(first_names, gathered[2:]):
        w[k] = arr.reshape(N_DEV * arr.shape[1], arr.shape[2])
    late_shards = [big_bf16[k] for k in _LATE_WEIGHTS] + [big_bf16["a_w_out"], a_g1[0], w_kv]
    w["mlp_norm0"], w["mlp_norm1"] = mlp_norm[0:1], mlp_norm[1:2]
    w["kv_norm"], w["k_norm"] = kv_norm.reshape(1, D_MODEL), k_norm.reshape(1, HEAD_DIM)
    w["b_norm"], w["b_q_norm"], w["b_sinks"], w["a_r_k"] = b_norm, b_q_norm, b_sinks, a_r_k.reshape(1, D_MODEL)

    loss_local, grad_x, g, big_reduced = _local_step(x[0], loss_target[0], w, late_shards)
    loss = lax.psum(loss_local[0, 0], MESH_AXES)

    g_lead = {k: g[k][None] for k in ("a_norm", "a_mix", "a_w0", "a_a0", "a_k_k", "a_k_a", "a_ln_x_w", "a_ln_x_b", "a_g2")}
    g_lead["a_w2"], g_lead["a_a2"] = g["a_w2"][None, :96], g["a_a2"][None, :96]
    g_cols = _pack_cols(g_lead).reshape(_COL_ROWS, N_DEV, D_MODEL // N_DEV).transpose(1, 0, 2)
    g_rows = _pack_rows({"a_w1": g["a_w1"][:, :96], "a_a1": g["a_a1"][:, :96], "a_g1": g["a_g1"], "w_kv": g["w_kv"]})
    g_rows = g_rows.reshape(N_DEV, D_MODEL // N_DEV, _ROW_COLS)
    pack_sums = _chip_sums([g_cols, g_rows], ("pack_cols", "pack_rows"), "late")
    reduced = list(_exchange_between_chips(pack_sums, name="scatter_grads_chips_late")) + [big_reduced[k] for k in _BIG_NAMES]
    g_repl = _pack_repl({"mlp_norm": jnp.concatenate([g["mlp_norm0"], g["mlp_norm1"]], axis=0), "kv_norm": g["kv_norm"],
                         "b_norm": g["b_norm"], "a_r_k": g["a_r_k"], "k_norm": g["k_norm"], "b_q_norm": g["b_q_norm"],
                         "b_sinks": g["b_sinks"]})
    (repl_slots,) = _all_gather([g_repl], name="gather_replicated_grads")

    res = {}
    cols4 = _adamw(cols_w, _pack_cols(mom), _pack_cols(var), reduced[0], name="adamw_cols")
    rows4 = _adamw(rows_w, _pack_rows(mom), _pack_rows(var), reduced[1], name="adamw_rows")
    repl4 = _adamw(repl_w, _pack_repl(mom), _pack_repl(var), repl_slots, name="adamw_replicated")
    for unpacked in ([_unpack_cols(a, (1,)) for a in cols4], [_unpack_rows(a, (1,)) for a in rows4], [_unpack_repl(a) for a in repl4]):
        for k in unpacked[0]:
            res[k] = tuple(u[k] for u in unpacked)
    stacked = {"a_w_r": ("a_w_rkv", 0), "a_w_k": ("a_w_rkv", 1), "a_w_v": ("a_w_rkv", 2), "mlp_w_up0": ("mlp_w_up", 0),
               "mlp_w_up1": ("mlp_w_up", 1), "mlp_w_down0": ("mlp_w_down", 0), "mlp_w_down1": ("mlp_w_down", 1)}
    big4, earlier = {}, None
    for k, bw, bm, bv, slots in zip(_BIG_NAMES, big_w, _big_shards(mom), _big_shards(var), reduced[2:]):
        if k in stacked:
            src, layer = stacked[k]
            as_layers = lambda a: a.reshape((-1,) + a.shape[-2:])
            earlier = _adamw(as_layers(wts[src]), as_layers(mom[src]), as_layers(var[src]), slots, name="adamw_" + k,
                             layer=layer, n_layers=as_layers(wts[src]).shape[0], into=earlier if layer else None)
        else:
            earlier = _adamw(bw, bm, bv, slots, name="adamw_" + k)
        big4[k] = earlier
    res["a_w_rkv"] = tuple(a[None] for a in big4["a_w_v"])
    res["a_w_out"] = tuple(a[None] for a in big4["a_w_out"])
    res["mlp_w_up"], res["mlp_w_down"] = tuple(big4["mlp_w_up1"]), tuple(big4["mlp_w_down1"])
    res["b_w_q"] = tuple(a[None] for a in big4["b_w_q"])
    res["b_w_o"] = tuple(a[None] for a in big4["b_w_o"])
    res["w_kv"] = tuple(a.reshape(w_kv.shape) for a in res["w_kv"])

    outs = [loss, grad_x[None]]
    for i in range(4):
        outs += [res[n][i].reshape(given[n].shape) for n in _WEIGHTS]
    return tuple(outs)
```

```python
import functools
import math

import jax
import jax.numpy as jnp
from jax import lax
from jax.experimental import pallas as pl
from jax.experimental.pallas import tpu as pltpu

F32 = jnp.float32
BF16 = jnp.bfloat16

D_MODEL = 2048
N_HEADS = 32
HEAD_DIM = 64
N_KV_HEADS = 4
Q_PER_KV = 8
ATT_BLOCK = 128
WKV_CHUNK = 64
LORA_PAD = 128
D_FF = 8192
N_DEV = 8
RMS_EPS = 1e-6
GN_EPS = 64e-5
L2_EPS = 1e-12
ROPE_THETA = 10000.0
ADAM_LR, ADAM_B1, ADAM_B2, ADAM_EPS, ADAM_WD, ADAM_STEP = 0.001, 0.9, 0.999, 1e-08, 0.01, 10
MASK_VALUE = -1e30
VMEM_LIMIT_BYTES = 56 * 1024 * 1024
MM_TILE_M, MM_TILE_N, MM_TILE_K = 1024, 1024, 2048
STREAM_BLOCK_ELEMENTS = 1 << 18
BF16_SUBLANES = 16
MESH_AXES = ("x", "y", "c")

_NN = (((1,), (0,)), ((), ()))
_NT = (((1,), (1,)), ((), ()))
_TN = (((0,), (0,)), ((), ()))
_BNN = (((2,), (1,)), ((0,), (0,)))
_BNT = (((2,), (2,)), ((0,), (0,)))
_BTN = (((1,), (1,)), ((0,), (0,)))


def _params(sem):
    return pltpu.CompilerParams(dimension_semantics=sem, vmem_limit_bytes=VMEM_LIMIT_BYTES)


def _split2(a):
    hi = a.astype(BF16)
    return hi, (a - hi.astype(F32)).astype(BF16)


def _dot3(a, b, dims):
    ah, al = _split2(a)
    bh, bl = _split2(b)
    if dims in (_NN, _NT, _TN):
        d = lambda p, q: lax.dot_general(p, q, dims, preferred_element_type=F32)
        return d(ah, bh) + (d(al, bh) + d(ah, bl))
    (ca,), (cb,) = dims[0]
    return lax.dot_general(jnp.concatenate([ah, al, ah], axis=ca), jnp.concatenate([bh, bh, bl], axis=cb), dims,
                           preferred_element_type=F32)


@functools.partial(jax.custom_vjp, nondiff_argnums=(2,))
def _hdot(a, b, dims=_NN):
    return _dot3(a, b, dims)


def _hdot_fwd(a, b, dims):
    return _dot3(a, b, dims), (a, b)


def _hdot_bwd(dims, res, g):
    a, b = res
    nn, nt, tn = (_NN, _NT, _TN) if dims in (_NN, _NT, _TN) else (_BNN, _BNT, _BTN)
    if dims == nn:
        return _dot3(g, b, nt), _dot3(a, g, tn)
    if dims == nt:
        return _dot3(g, b, nn), _dot3(g, a, tn)
    assert dims == tn
    return _dot3(b, g, nt), _dot3(a, g, nn)


_hdot.defvjp(_hdot_fwd, _hdot_bwd)


def _tri_parts(x):
    hi = x.astype(BF16)
    r1 = x - hi.astype(F32)
    mid = r1.astype(BF16)
    return hi, mid, (r1 - mid.astype(F32)).astype(BF16)


@jax.custom_vjp
def _mask_dot(mask, x):
    mb = mask.astype(BF16)
    return lax.dot_general(jnp.concatenate([mb, mb, mb], axis=2), jnp.concatenate(_tri_parts(x), axis=1), _BNN,
                           preferred_element_type=F32)


def _mask_dot_fwd(mask, x):
    return _mask_dot(mask, x), mask


def _mask_dot_bwd(mask, g):
    mb = mask.astype(BF16)
    return jnp.zeros_like(mask), lax.dot_general(jnp.concatenate([mb, mb, mb], axis=1), jnp.concatenate(_tri_parts(g), axis=1),
                                                 _BTN, preferred_element_type=F32)


_mask_dot.defvjp(_mask_dot_fwd, _mask_dot_bwd)


def _b16dot(a, b, dims):
    return lax.dot_general(a.astype(BF16), b.astype(BF16), dims, preferred_element_type=F32)


@jax.custom_vjp
def _bdot(a, b):
    return _b16dot(a, b, _NN)


def _bdot_fwd(a, b):
    return _b16dot(a, b, _NN), (a, b)


def _bdot_bwd(res, g):
    a, b = res
    return _b16dot(g, b, _NT), _b16dot(a, g, _TN)


_bdot.defvjp(_bdot_fwd, _bdot_bwd)


@jax.custom_vjp
def _bdot_nt(a, b):
    return _b16dot(a, b, _NT)


def _bdot_nt_fwd(a, b):
    return _b16dot(a, b, _NT), (a, b)


def _bdot_nt_bwd(res, g):
    a, b = res
    return _b16dot(g, b, _NN), _b16dot(g, a, _TN)


_bdot_nt.defvjp(_bdot_nt_fwd, _bdot_nt_bwd)


def _rms(x, gain):
    return x * lax.rsqrt(jnp.mean(x * x, axis=-1, keepdims=True) + RMS_EPS) * gain


def _vjp_of(f, n_in, diff):
    def g(*args):
        ins, cts = args[:n_in], args[n_in:]

        def fd(*d):
            full = list(ins)
            for pos, i in enumerate(diff):
                full[i] = d[pos]
            return f(*full)

        _, pull = jax.vjp(fd, *[ins[i] for i in diff])
        return pull(tuple(cts))
    return g


def _mm(a, b, *, name, ta=False, tb=False, a_pro=None, epi=None, epi_args=(), out_dtype=F32,
        tm=MM_TILE_M, tn=MM_TILE_N, tk=MM_TILE_K, dims=None, b_spec=None, o_spec=None, o_shape=None):
    if dims is None:
        m, k = (a.shape[1], a.shape[0]) if ta else a.shape
        n = b.shape[0] if tb else b.shape[1]
    else:
        m, n, k = dims
    tm, tn, tk = min(tm, m), min(tn, n), min(tk, k)
    assert m % tm == 0 and n % tn == 0 and k % tk == 0, (name, m, n, k, tm, tn, tk)
    nk = k // tk
    ne = len(epi_args)
    cdims = (((0 if ta else 1,), (1 if tb else 0,)), ((), ()))

    def body(a_ref, b_ref, *rest):
        e_refs, o_ref, acc = rest[:ne], rest[ne], rest[ne + 1]
        kk = pl.program_id(2)

        @pl.when(kk == 0)
        def _():
            acc[...] = jnp.zeros_like(acc)

        av = a_ref[...]
        if a_pro is not None:
            av = a_pro(av.astype(F32))
        acc[...] += lax.dot_general(av.astype(BF16), b_ref[...].astype(BF16), cdims, preferred_element_type=F32)

        @pl.when(kk == nk - 1)
        def _():
            r = acc[...]
            if epi is not None:
                r = epi(r, *[e[...] for e in e_refs])
            o_ref[...] = r.astype(o_ref.dtype)

    a_spec = pl.BlockSpec((tk, tm), lambda i, j, q: (q, i)) if ta else pl.BlockSpec((tm, tk), lambda i, j, q: (i, q))
    if b_spec is None:
        b_spec = pl.BlockSpec((tn, tk), lambda i, j, q: (j, q)) if tb else pl.BlockSpec((tk, tn), lambda i, j, q: (q, j))
    if o_spec is None:
        o_spec = pl.BlockSpec((tm, tn), lambda i, j, q: (i, j))
        o_shape = (m, n)
    e_specs = [pl.BlockSpec((tm, tn), lambda i, j, q: (i, j)) for _ in epi_args]
    return pl.pallas_call(
        body, grid=(m // tm, n // tn, nk), in_specs=[a_spec, b_spec] + e_specs, out_specs=o_spec,
        out_shape=jax.ShapeDtypeStruct(o_shape, out_dtype), scratch_shapes=[pltpu.VMEM((tm, tn), F32)],
        compiler_params=_params(("parallel", "parallel", "arbitrary")), name=name,
    )(a, b, *epi_args)


def _rowwise(fn, rows, params, out_rows, out_params=(), *, tm=256, name, chip_sums=()):
    t = rows[0].shape[0]
    tm = min(tm, t)
    assert t % tm == 0
    nr, npar, nor, nop, ns = len(rows), len(params), len(out_rows), len(out_params), len(chip_sums)
    steps = t // tm

    def body(*refs):
        r, p = refs[:nr], refs[nr:nr + npar]
        o, op = refs[nr + npar + ns:nr + npar + ns + nor], refs[nr + npar + ns + nor:nr + npar + ns + nor + nop]
        if ns:
            start, finish = _chip_exchange_plan(refs[nr + npar:nr + npar + ns], refs[nr + npar + ns + nor + nop:nr + npar + 2 * ns + nor + nop],
                                                *refs[nr + npar + 2 * ns + nor + nop:])
            pl.when(pl.program_id(0) == 0)(start)
        outs = fn(*[x[...] for x in r], *[x[...] for x in p])
        for ref, val in zip(o, outs[:nor]):
            ref[...] = val.astype(ref.dtype)
        if nop:
            @pl.when(pl.program_id(0) == 0)
            def _():
                for ref in op:
                    ref[...] = jnp.zeros_like(ref)

            for ref, val in zip(op, outs[nor:]):
                ref[...] += val.astype(F32)
        if ns:
            pl.when(pl.program_id(0) == steps - 1)(finish)

    in_specs = [pl.BlockSpec((tm, x.shape[1]), lambda i: (i, 0)) for x in rows]
    in_specs += [pl.BlockSpec(p.shape, lambda i: (0, 0)) for p in params]
    out_specs = [pl.BlockSpec((tm, s.shape[1]), lambda i: (i, 0)) for s in out_rows]
    out_specs += [pl.BlockSpec(s.shape, lambda i: (0, 0)) for s in out_params]
    return pl.pallas_call(
        body, grid=(steps,), in_specs=in_specs + [_ANY] * ns, out_specs=out_specs + [_ANY] * ns,
        out_shape=list(out_rows) + list(out_params) + [_sds(c.shape, c.dtype) for c in chip_sums],
        scratch_shapes=_chip_exchange_semaphores(ns) if ns else [],
        compiler_params=_params(("arbitrary",)), name=name,
    )(*rows, *params, *chip_sums)


def _sds(shape, dtype=F32):
    return jax.ShapeDtypeStruct(tuple(shape), dtype)


def _doubling_powers(l):
    powers = [l]
    for _ in range(int(math.log2(l.shape[-1])) - 1):
        powers.append(_dot3(powers[-1], powers[-1], _BNN))
    return powers


@jax.custom_vjp
def _unit_lower_solve(l, z):
    u = z
    for p in _doubling_powers(l):
        u = u + _dot3(p, u, _BNN)
    return u


def _unit_lower_solve_fwd(l, z):
    powers = _doubling_powers(l)
    u = z
    for p in powers:
        u = u + _dot3(p, u, _BNN)
    return u, (powers, u)


def _unit_lower_solve_bwd(res, du):
    powers, u = res
    g = du
    for p in powers:
        g = g + _dot3(p, g, _BTN)
    return _dot3(g, u, _BNT), g


_unit_lower_solve.defvjp(_unit_lower_solve_fwd, _unit_lower_solve_bwd)


def _wkv_chunk(s0, r, lw, k, v, a, b):
    nb, c, _ = r.shape
    ti = lax.broadcasted_iota(jnp.int32, (nb, c, c), 1)
    si = lax.broadcasted_iota(jnp.int32, (nb, c, c), 2)
    incl, strict = si <= ti, si < ti
    cum = _mask_dot(incl.astype(F32), lw)
    tot = jnp.sum(lw, axis=1, keepdims=True)
    rcum = tot - cum
    w_inv = jnp.exp(-cum)
    at, rt, bt, kt = a * jnp.exp(cum - lw), r * jnp.exp(cum), b * w_inv, k * w_inv
    l_ab = jnp.where(strict, _hdot(at, bt, _BNT), 0.0)
    l_ak = jnp.where(strict, _hdot(at, kt, _BNT), 0.0)
    t_rb = jnp.where(incl, _hdot(rt, bt, _BNT), 0.0)
    t_rk = jnp.where(incl, _hdot(rt, kt, _BNT), 0.0)
    u = _unit_lower_solve(l_ab, _hdot(at, s0, _BNT) + _hdot(l_ak, v, _BNN))
    y = _hdot(rt, s0, _BNT) + _hdot(t_rb, u, _BNN) + _hdot(t_rk, v, _BNN)
    e = jnp.exp(rcum)
    s1 = s0 * jnp.exp(tot) + _hdot(u, b * e, _BTN) + _hdot(v, k * e, _BTN)
    return y, s1


WKV_HEADS_PER_STEP = 16
GATHER_RELAY_AT = 0.625


def _first_and_last_step(grid):
    i, j = pl.program_id(0), pl.program_id(1)
    return jnp.logical_and(i == 0, j == 0), jnp.logical_and(i == grid[0] - 1, j == grid[1] - 1)


N_WKV_PARAMS = 7


def _tmix_chunk(s0, r, k, v, wl, al, w0, a0, k_k, k_a, ln_w, ln_b, r_k):
    lw, kmod, a, b = _wkv_prep(k, wl, al, w0, a0, k_k, k_a)
    y, s1 = _wkv_chunk(s0, r, lw, kmod, v, a, b)
    (y2,) = _wkv_post(y, r, kmod, v, ln_w, ln_b, r_k)
    return y2, s1


def _split_heads(x, nh):
    return x.reshape(x.shape[0], nh, HEAD_DIM).transpose(1, 0, 2)


def _merge_heads(xh):
    return jnp.concatenate([xh[h] for h in range(xh.shape[0])], axis=1)


def _wkv_fwd(r, k, v, wl, al, params, shards=()):
    t = r.shape[0]
    nh, n = N_HEADS, HEAD_DIM
    nc = t // WKV_CHUNK
    hb = WKV_HEADS_PER_STEP
    grid = (nh // hb, nc)
    ns = len(shards)
    n_in = 5 + N_WKV_PARAMS

    def body(*refs):
        y_ref, s_ref = refs[n_in + ns:n_in + ns + 2]
        state = refs[n_in + 2 * ns + 2]
        if ns:
            start, relay, finish = _gather_plan(refs[n_in:n_in + ns], refs[n_in + ns + 2:n_in + 2 * ns + 2], *refs[n_in + 2 * ns + 3:])
            first, last = _first_and_last_step(grid)
            pl.when(first)(start)
            relay_step = int(GATHER_RELAY_AT * grid[0] * nc)
            pl.when(jnp.logical_and(pl.program_id(0) == relay_step // nc, pl.program_id(1) == relay_step % nc))(relay)

        @pl.when(pl.program_id(1) == 0)
        def _():
            state[...] = jnp.zeros_like(state)

        s0 = state[...]
        s_ref[:, 0] = s0
        rows = [_split_heads(ref[...], hb) for ref in refs[:5]]
        y, s1 = _tmix_chunk(s0, *rows, *[ref[...] for ref in refs[5:n_in]])
        y_ref[...] = _merge_heads(y)
        state[...] = s1
        if ns:
            pl.when(last)(finish)

    blk = pl.BlockSpec((WKV_CHUNK, hb * n), lambda h, c: (c, h))
    pblk = pl.BlockSpec((hb, 1, n), lambda h, c: (h, 0, 0))
    sblk = pl.BlockSpec((hb, 1, n, n), lambda h, c: (h, c, 0, 0))
    outs = pl.pallas_call(
        body, grid=grid, in_specs=[blk] * 5 + [pblk] * N_WKV_PARAMS + [_ANY] * ns, out_specs=[blk, sblk] + [_ANY] * ns,
        out_shape=[_sds((t, nh * n)), _sds((nh, nc, n, n))] + [_sds((N_DEV,) + s.shape, s.dtype) for s in shards],
        scratch_shapes=[pltpu.VMEM((hb, n, n), F32)] + (_gather_semaphores(ns) if ns else []),
        compiler_params=_params(("arbitrary", "arbitrary")), name="wkv_fwd",
    )(r, k, v, wl, al, *params, *shards)
    return outs[0], outs[1], list(outs[2:])


def _wkv_bwd(r, k, v, wl, al, params, states, dy, chip_sums=()):
    t = r.shape[0]
    nh, n = N_HEADS, HEAD_DIM
    nc = t // WKV_CHUNK
    hb = WKV_HEADS_PER_STEP
    grid = (nh // hb, nc)
    ns = len(chip_sums)
    n_in = 5 + N_WKV_PARAMS
    n_out = 5 + N_WKV_PARAMS

    def body(*refs):
        s_ref, dy_ref = refs[n_in:n_in + 2]
        out_refs = refs[n_in + 2 + ns:n_in + 2 + ns + n_out]
        dstate = refs[n_in + 2 + 2 * ns + n_out]
        if ns:
            start, finish = _chip_exchange_plan(refs[n_in + 2:n_in + 2 + ns], refs[n_in + 2 + ns + n_out:n_in + 2 + 2 * ns + n_out],
                                                *refs[n_in + 3 + 2 * ns + n_out:])
            first, last = _first_and_last_step(grid)
            pl.when(first)(start)

        @pl.when(pl.program_id(1) == 0)
        def _():
            dstate[...] = jnp.zeros_like(dstate)
            for ref in out_refs[5:]:
                ref[...] = jnp.zeros_like(ref)

        rows = [_split_heads(ref[...], hb) for ref in refs[:5]]
        _, pull = jax.vjp(_tmix_chunk, s_ref[:, 0], *rows, *[ref[...] for ref in refs[5:n_in]])
        grads = pull((_split_heads(dy_ref[...], hb), dstate[...]))
        dstate[...] = grads[0]
        for ref, val in zip(out_refs[:5], grads[1:6]):
            ref[...] = _merge_heads(val).astype(ref.dtype)
        for ref, val in zip(out_refs[5:], grads[6:]):
            ref[...] += val
        if ns:
            pl.when(last)(finish)

    blk = pl.BlockSpec((WKV_CHUNK, hb * n), lambda h, c: (nc - 1 - c, h))
    pblk = pl.BlockSpec((hb, 1, n), lambda h, c: (h, 0, 0))
    sblk = pl.BlockSpec((hb, 1, n, n), lambda h, c: (h, nc - 1 - c, 0, 0))
    outs = pl.pallas_call(
        body, grid=grid, in_specs=[blk] * 5 + [pblk] * N_WKV_PARAMS + [sblk, blk] + [_ANY] * ns,
        out_specs=[blk] * 5 + [pblk] * N_WKV_PARAMS + [_ANY] * ns,
        out_shape=[_sds((t, nh * n), BF16)] * 5 + [_sds((nh, 1, n))] * N_WKV_PARAMS + [_sds(p.shape, p.dtype) for p in chip_sums],
        scratch_shapes=[pltpu.VMEM((hb, n, n), F32)] + (_chip_exchange_semaphores(ns) if ns else []),
        compiler_params=_params(("arbitrary", "arbitrary")), name="wkv_bwd",
    )(r, k, v, wl, al, *params, states, dy, *chip_sums)
    return outs[:5], outs[5:n_out], list(outs[n_out:])


def _wkv_prep(k, wl, al, w0, a0, k_k, k_a):
    z = -(w0 + wl)
    softplus = jnp.maximum(z, 0.0) + jnp.log1p(jnp.exp(-jnp.abs(z)))
    lw = -jnp.exp(-softplus - 0.5)
    asig = jax.nn.sigmoid(a0 + al)
    kk = k * k_k
    kk = kk / jnp.maximum(jnp.sqrt(jnp.sum(kk * kk, axis=-1, keepdims=True)), L2_EPS)
    kmod = k * (1.0 + (asig - 1.0) * k_a)
    return lw, kmod, -kk, kk * asig


def _wkv_post(y, r, kmod, v, ln_w, ln_b, r_k):
    mu = jnp.mean(y, axis=-1, keepdims=True)
    var = jnp.mean(jnp.square(y - mu), axis=-1, keepdims=True)
    yn = (y - mu) * lax.rsqrt(var + GN_EPS)
    yn = yn * ln_w + ln_b
    return (yn + jnp.sum(r * kmod * r_k, axis=-1, keepdims=True) * v,)


def _attn_group(nonzero_block, q, kc, kp, vc, vp, cos_c, sin_c, cos_p, sin_p, q_gain, k_gain, sinks):
    ri = lax.broadcasted_iota(jnp.int32, (HEAD_DIM, HEAD_DIM), 0)
    ci = lax.broadcasted_iota(jnp.int32, (HEAD_DIM, HEAD_DIM), 1)
    half = HEAD_DIM // 2
    rot = jnp.where(ri == ci + half, -1.0, 0.0) + jnp.where(ri + half == ci, 1.0, 0.0)
    rows = Q_PER_KV * ATT_BLOCK

    def rope(x, cos, sin):
        return x * cos + _hdot(x, rot, _NN) * sin

    kcr = rope(_rms(kc, k_gain), cos_c, sin_c)
    kpr = rope(_rms(kp, k_gain), cos_p, sin_p)
    qn = _rms(q, q_gain)
    qr = qn * cos_c + _hdot(qn.reshape(rows, HEAD_DIM), rot, _NN).reshape(q.shape) * sin_c
    q2 = qr.reshape(rows, HEAD_DIM)
    qi = lax.broadcasted_iota(jnp.int32, (1, ATT_BLOCK, ATT_BLOCK), 1)
    ki = lax.broadcasted_iota(jnp.int32, (1, ATT_BLOCK, ATT_BLOCK), 2)
    mask_c = ki <= qi
    mask_p = jnp.logical_and(ki > qi, nonzero_block)
    lane0 = (lax.broadcasted_iota(jnp.int32, (1, 1, 128), 2) == 0).astype(F32)
    shape3 = (Q_PER_KV, ATT_BLOCK, ATT_BLOCK)
    sc = jnp.where(mask_c, (_bdot_nt(q2, kcr) * (HEAD_DIM ** -0.5)).reshape(shape3), MASK_VALUE)
    sp = jnp.where(mask_p, (_bdot_nt(q2, kpr) * (HEAD_DIM ** -0.5)).reshape(shape3), MASK_VALUE)
    sk = jnp.sum(sinks * lane0, axis=2, keepdims=True)
    mx = jnp.maximum(jnp.maximum(jnp.max(sc, axis=2, keepdims=True), jnp.max(sp, axis=2, keepdims=True)), sk)
    mx = lax.stop_gradient(mx)
    ec, ep = jnp.exp(sc - mx), jnp.exp(sp - mx)
    den = jnp.sum(ec, axis=2, keepdims=True) + jnp.sum(ep, axis=2, keepdims=True) + jnp.exp(sk - mx)
    out = _bdot((ec / den).reshape(rows, ATT_BLOCK), vc) + _bdot((ep / den).reshape(rows, ATT_BLOCK), vp)
    return out.reshape(q.shape)


def _attn_rows(nonzero_block, qs, kcs, kps, vcs, vps, tabs, q_gain, k_gain, sinks):
    return [_attn_group(nonzero_block, qs[g], kcs[g], kps[g], vcs[g], vps[g], *tabs, q_gain, k_gain, sinks[g])
            for g in range(N_KV_HEADS)]


def _attn_operands(q_tile, kvc_tile, kvp_tile, sink_ref):
    q3 = _split_heads(q_tile, N_HEADS)
    kvc, kvp = _split_heads(kvc_tile, 2 * N_KV_HEADS), _split_heads(kvp_tile, 2 * N_KV_HEADS)
    groups = range(N_KV_HEADS)
    qs = [q3[Q_PER_KV * g:Q_PER_KV * (g + 1)] for g in groups]
    sinks = [sink_ref[Q_PER_KV * g:Q_PER_KV * (g + 1)] for g in groups]
    return (qs, [kvc[g] for g in groups], [kvp[g] for g in groups], [kvc[N_KV_HEADS + g] for g in groups],
            [kvp[N_KV_HEADS + g] for g in groups], sinks)


def _attn_specs(t):
    nb = t // ATT_BLOCK
    prev = lambda n: jnp.maximum(n - 1, 0)
    kv_width = 2 * N_KV_HEADS * HEAD_DIM
    q_spec = pl.BlockSpec((ATT_BLOCK, D_MODEL), lambda n: (n, 0))
    kv_c = pl.BlockSpec((ATT_BLOCK, kv_width), lambda n: (n, 0))
    kv_p = pl.BlockSpec((ATT_BLOCK, kv_width), lambda n: (prev(n), 0))
    tab_c = pl.BlockSpec((ATT_BLOCK, HEAD_DIM), lambda n: (n, 0))
    tab_p = pl.BlockSpec((ATT_BLOCK, HEAD_DIM), lambda n: (prev(n), 0))
    gain = pl.BlockSpec((1, HEAD_DIM), lambda n: (0, 0))
    sink = pl.BlockSpec((N_HEADS, 1, 128), lambda n: (0, 0, 0))
    return nb, q_spec, kv_c, kv_p, tab_c, tab_p, gain, sink


def _attn_fwd(q, kv, cos, sin, q_gain, k_gain, sinks):
    t = q.shape[0]
    nb, q_spec, kv_c, kv_p, tab_c, tab_p, gain, sink = _attn_specs(t)

    def body(q_ref, kvc, kvp, cc, sc, cp, sp, qg, kg, sk, o_ref):
        qs, kcs, kps, vcs, vps, sinks_ = _attn_operands(q_ref[...], kvc[...], kvp[...], sk)
        outs = _attn_rows(pl.program_id(0) > 0, qs, kcs, kps, vcs, vps, (cc[...], sc[...], cp[...], sp[...]),
                          qg[...], kg[...], sinks_)
        o_ref[...] = _merge_heads(jnp.concatenate(outs, axis=0)).astype(o_ref.dtype)

    return pl.pallas_call(
        body, grid=(nb,), in_specs=[q_spec, kv_c, kv_p, tab_c, tab_c, tab_p, tab_p, gain, gain, sink],
        out_specs=q_spec, out_shape=_sds(q.shape, BF16), compiler_params=_params(("arbitrary",)), name="attn_fwd",
    )(q, kv, kv, cos, sin, cos, sin, q_gain, k_gain, sinks)


def _attn_bwd(q, kv, cos, sin, q_gain, k_gain, sinks, do):
    t = q.shape[0]
    nb, q_spec, kv_c, kv_p, tab_c, tab_p, gain, sink = _attn_specs(t)

    def body(q_ref, kvc, kvp, cc, sc, cp, sp, qg, kg, sk, do_ref, dq_ref, dkvc_ref, dkvp_ref, dqg_ref, dkg_ref, dsk_ref):
        nonzero = pl.program_id(0) > 0
        tabs = (cc[...], sc[...], cp[...], sp[...])
        qs, kcs, kps, vcs, vps, sinks_ = _attn_operands(q_ref[...], kvc[...], kvp[...], sk)

        def f(qs_, kcs_, kps_, vcs_, vps_, qgv, kgv, sks):
            return _attn_rows(nonzero, qs_, kcs_, kps_, vcs_, vps_, tabs, qgv, kgv, sks)

        _, pull = jax.vjp(f, qs, kcs, kps, vcs, vps, qg[...], kg[...], sinks_)
        do3 = _split_heads(do_ref[...], N_HEADS)
        dqs, dkcs, dkps, dvcs, dvps, dqg, dkg, dsks = pull([do3[Q_PER_KV * g:Q_PER_KV * (g + 1)] for g in range(N_KV_HEADS)])
        dq_ref[...] = _merge_heads(jnp.concatenate(dqs, axis=0)).astype(dq_ref.dtype)
        dkvc_ref[...] = jnp.concatenate(dkcs + dvcs, axis=1)
        dkvp_ref[...] = jnp.concatenate(dkps + dvps, axis=1)

        @pl.when(pl.program_id(0) == 0)
        def _():
            dqg_ref[...] = jnp.zeros_like(dqg_ref)
            dkg_ref[...] = jnp.zeros_like(dkg_ref)
            dsk_ref[...] = jnp.zeros_like(dsk_ref)

        dqg_ref[...] += dqg
        dkg_ref[...] += dkg
        for g in range(N_KV_HEADS):
            dsk_ref[Q_PER_KV * g:Q_PER_KV * (g + 1)] += dsks[g]

    return pl.pallas_call(
        body, grid=(nb,), in_specs=[q_spec, kv_c, kv_p, tab_c, tab_c, tab_p, tab_p, gain, gain, sink, q_spec],
        out_specs=[q_spec, kv_c, kv_c, gain, gain, sink],
        out_shape=[_sds(q.shape, BF16), _sds(kv.shape), _sds(kv.shape), _sds((1, HEAD_DIM)), _sds((1, HEAD_DIM)), _sds(sinks.shape)],
        compiler_params=_params(("arbitrary",)), name="attn_bwd",
    )(q, kv, kv, cos, sin, cos, sin, q_gain, k_gain, sinks, do)


def _time_shift_lerps(x, xs, gain, *mix):
    xn, xsn = _rms(x, gain), _rms(xs, gain)
    xx = xsn - xn
    return tuple(xn + xx * m for m in mix)


def _residual_norm(h, delta, gain):
    hn = h + delta
    return hn, _rms(hn, gain)


def _residual_norm2(h, delta, gain_a, gain_b):
    hn = h + delta
    return hn, _rms(hn, gain_a), _rms(hn, gain_b)


def _relu2(u):
    return jnp.square(jnp.maximum(u, 0.0))


def _sigmoid(z):
    return jax.nn.sigmoid(z)


def _shift_down(x):
    return jnp.pad(x[:-1], ((1, 0), (0, 0)))


def _shift_up(x):
    return jnp.pad(x[1:], ((0, 1), (0, 0)))


def _rope_tables(t):
    half = HEAD_DIM // 2
    inv_freq = jnp.power(ROPE_THETA, -jnp.arange(half, dtype=F32) / half)
    ang = jnp.arange(t, dtype=jnp.int32).astype(F32)[:, None] * inv_freq[None, :]
    cos, sin = jnp.cos(ang), jnp.sin(ang)
    return jnp.concatenate([cos, cos], axis=1), jnp.concatenate([sin, sin], axis=1)


def _mlp_fwd(hn, w_up, w_down, layer, up_dev_major):
    t = hn.shape[0]
    if up_dev_major:
        cw = w_up.shape[2]
        u = _mm(hn, w_up, name=f"mlp{layer}_up", dims=(t, D_FF, D_MODEL), tn=cw, tk=D_MODEL,
                b_spec=pl.BlockSpec((None, D_MODEL, cw), lambda i, j, q: (j, q, 0)))
    else:
        u = _mm(hn, w_up, name=f"mlp{layer}_up")
    out = _mm(u, w_down, a_pro=_relu2, name=f"mlp{layer}_down")
    return u, out


def _mm_pair_reduce(a, b, *, device_axis, tile, a_pro=None, name):
    t = a.shape[0]
    tr, tc = tile
    other = (b.shape[1] // tc) if device_axis == "m" else (a.shape[1] // tr)
    n_tiles = N_CHIPS * other
    core = lax.axis_index("c").astype(jnp.int32).reshape(1)
    shape = (N_CHIPS, tr, other * tc) if device_axis == "m" else (N_CHIPS, other * tr, tc)

    def body(core_ref, a_ref, b_ref, out_ref, recv_hbm, send_buf, recv_tile, send_sems, recv_sems, local_sem):
        phase, q, o = pl.program_id(0), pl.program_id(1), pl.program_id(2)
        sibling = (lax.axis_index("x"), lax.axis_index("y"), 1 - core_ref[0])
        tile_no = q * other + o
        slot = lax.rem(tile_no, 2)

        def send(number, buffer):
            return pltpu.make_async_remote_copy(
                src_ref=send_buf.at[buffer], dst_ref=recv_hbm.at[number // other, lax.rem(number, other)],
                send_sem=send_sems.at[number], recv_sem=recv_sems.at[number], device_id=sibling, device_id_type=_MESH_ID)

        def landed():
            return pltpu.make_async_copy(recv_hbm.at[q, o], recv_tile, local_sem)

        @pl.when(phase == 1)
        def _():
            @pl.when(tile_no == 0)
            def _():
                send(n_tiles - 2, lax.rem(n_tiles - 2, 2)).wait_send()
                send(n_tiles - 1, lax.rem(n_tiles - 1, 2)).wait_send()

            send(tile_no, slot).wait_recv()
            landed().start()

        av = a_ref[...]
        if a_pro is not None:
            av = a_pro(av.astype(F32))
        acc = lax.dot_general(av.astype(BF16), b_ref[...].astype(BF16), _TN, preferred_element_type=F32)

        @pl.when(phase == 0)
        def _():
            @pl.when(tile_no >= 2)
            def _():
                send(tile_no - 2, slot).wait_send()

            send_buf[slot] = acc.astype(send_buf.dtype)
            send(tile_no, slot).start()

        @pl.when(phase == 1)
        def _():
            landed().wait()
            out_ref[...] = (acc + recv_tile[...].astype(F32)).astype(out_ref.dtype)

    def device(phase, q, core_ref):
        return 2 * q + jnp.where(phase == 0, 1 - core_ref[0], core_ref[0])

    parked = lambda p, i: jnp.where(p == 0, 0, i)
    if device_axis == "m":
        a_spec = pl.BlockSpec((t, tr), lambda p, q, o, c: (0, device(p, q, c)))
        b_spec = pl.BlockSpec((t, tc), lambda p, q, o, c: (0, o))
        o_spec = pl.BlockSpec((None, tr, tc), lambda p, q, o, c: (parked(p, q), 0, parked(p, o)))
    else:
        a_spec = pl.BlockSpec((t, tr), lambda p, q, o, c: (0, o))
        b_spec = pl.BlockSpec((t, tc), lambda p, q, o, c: (0, device(p, q, c)))
        o_spec = pl.BlockSpec((None, tr, tc), lambda p, q, o, c: (parked(p, q), parked(p, o), 0))
    grid_spec = pltpu.PrefetchScalarGridSpec(
        num_scalar_prefetch=1, grid=(2, N_CHIPS, other), in_specs=[a_spec, b_spec], out_specs=[o_spec, _ANY],
        scratch_shapes=[pltpu.VMEM((2, tr, tc), BF16), pltpu.VMEM((tr, tc), BF16),
                        pltpu.SemaphoreType.DMA((n_tiles,)), pltpu.SemaphoreType.DMA((n_tiles,)), pltpu.SemaphoreType.DMA(())])
    sums, _ = pl.pallas_call(
        body, grid_spec=grid_spec, out_shape=[_sds(shape, BF16), _sds((N_CHIPS, other, tr, tc), BF16)],
        compiler_params=_params(("arbitrary", "arbitrary", "arbitrary")), name=name,
    )(core, a, b)
    return sums


MLP_PAIR_TILE = (1024, 1024)
SQUARE_PAIR_TILE = (D_MODEL // N_DEV, D_MODEL)


def _mlp_bwd(hn, u, dh, w_up, w_down, layer, up_dev_major, pair_reduce=False):
    t = hn.shape[0]
    du = _mm(dh, w_down, tb=True, epi=lambda r, uu: r * (2.0 * jnp.maximum(uu, 0.0)), epi_args=(u,), out_dtype=BF16,
             name=f"mlp{layer}_du")
    if pair_reduce:
        cw = w_up.shape[2]
        d_down = _mm_pair_reduce(u, dh, device_axis="m", tile=MLP_PAIR_TILE, a_pro=_relu2, name=f"mlp{layer}_ddown")
        d_up = _mm_pair_reduce(hn, du, device_axis="n", tile=MLP_PAIR_TILE, name=f"mlp{layer}_dup")
        dhn = _mm(du, w_up, tb=True, name=f"mlp{layer}_dhn", dims=(t, D_MODEL, D_FF), tk=cw,
                  b_spec=pl.BlockSpec((None, MM_TILE_N, cw), lambda i, j, q: (q, j, 0)))
        return dhn, d_up, d_down
    d_down = _mm(u, dh, ta=True, a_pro=_relu2, name=f"mlp{layer}_ddown")
    if up_dev_major:
        cw = w_up.shape[2]
        d_up = _mm(hn, du, ta=True, name=f"mlp{layer}_dup", dims=(D_MODEL, D_FF, t), tn=cw,
                   o_spec=pl.BlockSpec((None, MM_TILE_M, cw), lambda i, j, q: (j, i, 0)), o_shape=(N_DEV, D_MODEL, cw))
        dhn = _mm(du, w_up, tb=True, name=f"mlp{layer}_dhn", dims=(t, D_MODEL, D_FF), tk=cw,
                  b_spec=pl.BlockSpec((None, MM_TILE_N, cw), lambda i, j, q: (q, j, 0)))
    else:
        d_up = _mm(hn, du, ta=True, name=f"mlp{layer}_dup")
        dhn = _mm(du, w_up, tb=True, name=f"mlp{layer}_dhn")
    return dhn, d_up, d_down


_LATE_WEIGHTS = ("mlp_w_up0", "mlp_w_up1", "mlp_w_down0", "mlp_w_down1", "b_w_q", "b_w_o")
_LATE_GATHER = _LATE_WEIGHTS + ("a_w_out", "a_g1", "w_kv")
_EARLY_GRADS = _LATE_WEIGHTS + ("a_w_out",)
_RKV_GRADS = ("a_w_r", "a_w_k", "a_w_v")


def _local_step(x, target, w, late_shards=None, up_dev_major=True):
    t = x.shape[0]
    g = {}
    w = dict(w)
    row = lambda: _sds((t, D_MODEL))
    rowb = lambda: _sds((t, D_MODEL), BF16)
    vec = lambda: _sds((1, D_MODEL))

    xs = _shift_down(x)
    mix = [w["a_mix"][i:i + 1] for i in range(6)]
    xr, xk, xv, xw, xa, xg = _rowwise(_time_shift_lerps, [x, xs], [w["a_norm"]] + mix, [rowb()] * 6, tm=256, name="tmix_lerp")
    r = _mm(xr, w["a_w_r"], name="tmix_r")
    k = _mm(xk, w["a_w_k"], name="tmix_k")
    v = _mm(xv, w["a_w_v"], name="tmix_v")
    lw1 = _mm(xw, w["a_w1"], name="tmix_w1")
    wl = _mm(lw1, w["a_w2"], a_pro=jnp.tanh, name="tmix_w2")
    la1 = _mm(xa, w["a_a1"], name="tmix_a1")
    al = _mm(la1, w["a_a2"], name="tmix_a2")

    hv = lambda name: w[name].reshape(N_HEADS, 1, HEAD_DIM)
    wkv_params = [hv(name) for name in ("a_w0", "a_a0", "a_k_k", "a_k_a", "a_ln_x_w", "a_ln_x_b", "a_r_k")]
    y2, states, gathered = _wkv_fwd(r, k, v, wl, al, wkv_params, late_shards or ())
    for name, arr in zip(_LATE_GATHER, gathered):
        w[name] = arr if name.startswith("mlp_w_up") else arr.reshape(N_DEV * arr.shape[1], arr.shape[2])
    lg1 = _mm(xg, w["a_g1"], name="tmix_g1")
    gate = _mm(lg1, w["a_g2"], a_pro=_sigmoid, name="tmix_g2")
    (yg,) = _rowwise(lambda a, b: (a * b,), [y2, gate], [], [rowb()], name="tmix_gate")
    att = _mm(yg, w["a_w_out"], name="tmix_out")

    h1, hn0 = _rowwise(_residual_norm, [x, att], [w["mlp_norm0"]], [row(), rowb()], name="res_norm0")
    u0, m0 = _mlp_fwd(hn0, w["mlp_w_up0"], w["mlp_w_down0"], 0, up_dev_major)

    h2, kvn, qn = _rowwise(_residual_norm2, [h1, m0], [w["kv_norm"], w["b_norm"]], [row(), rowb(), rowb()], name="res_norm_kvq")
    kv = _mm(kvn, w["w_kv"], name="kv_proj")
    q = _mm(qn, w["b_w_q"], name="q_proj")
    cos, sin = _rope_tables(t)
    sinks = jnp.broadcast_to(w["b_sinks"].reshape(N_HEADS, 1, 1), (N_HEADS, 1, 128))
    o = _attn_fwd(q, kv, cos, sin, w["b_q_norm"], w["k_norm"], sinks)
    att2 = _mm(o, w["b_w_o"], name="attn_out")

    h3, hn1 = _rowwise(_residual_norm, [h2, att2], [w["mlp_norm1"]], [row(), rowb()], name="res_norm1")
    u1, m1 = _mlp_fwd(hn1, w["mlp_w_up1"], w["mlp_w_down1"], 1, up_dev_major)

    def loss_fn(h, m, tg):
        diff = (h + m) - tg
        part = 0.5 * jnp.sum(jnp.mean(jnp.square(diff), axis=-1, keepdims=True), axis=0, keepdims=True)
        dh = diff * (1.0 / D_MODEL)
        return dh, dh, jnp.broadcast_to(part, (1, 128))

    dh4, dh4b, loss = _rowwise(loss_fn, [h3, m1, target], [], [row(), rowb()], [_sds((1, 128))], name="loss")

    def res_norm_bwd(h, dnext, dhn, gain):
        dh, dgain = _vjp_of(lambda hh, gg: (_rms(hh, gg),), 2, (0, 1))(h, gain, dhn)
        return dnext + dh, dnext + dh, dgain

    distributed = late_shards is not None

    def square_dw(a_, b_, name):
        if distributed:
            return _mm_pair_reduce(a_, b_, device_axis="m", tile=SQUARE_PAIR_TILE, name=name)
        return _mm(a_, b_, ta=True, name=name)

    dhn1, g["mlp_w_up1"], g["mlp_w_down1"] = _mlp_bwd(hn1, u1, dh4b, w["mlp_w_up1"], w["mlp_w_down1"], 1, up_dev_major, distributed)
    dh3, dh3b, g["mlp_norm1"] = _rowwise(res_norm_bwd, [h3, dh4, dhn1], [w["mlp_norm1"]], [row(), rowb()], [vec()], name="res_norm1_bwd")

    g["b_w_o"] = square_dw(o, dh3b, "attn_out_dw")
    do = _mm(dh3b, w["b_w_o"], tb=True, name="attn_out_dx")
    dq, dkv_own, dkv_prev, g["b_q_norm"], g["k_norm"], dsinks = _attn_bwd(
        q, kv, cos, sin, w["b_q_norm"], w["k_norm"], sinks, do)
    g["b_sinks"] = dsinks[:, 0, 0].reshape(1, N_HEADS)
    g["b_w_q"] = square_dw(qn, dq, "q_proj_dw")
    dqn = _mm(dq, w["b_w_q"], tb=True, name="q_proj_dx")
    dkv_prev = jnp.pad(dkv_prev[ATT_BLOCK:], ((0, ATT_BLOCK), (0, 0)))
    (dkv,) = _rowwise(lambda a, b: (a + b,), [dkv_own, dkv_prev], [], [_sds(kv.shape, BF16)], name="kv_grad_sum")
    g["w_kv"] = _mm(kvn, dkv, ta=True, name="kv_proj_dw")
    dkvn = _mm(dkv, w["w_kv"], tb=True, name="kv_proj_dx")

    def res_norm2_bwd(h, dnext, dna, dnb, gain_a, gain_b):
        dha, dga = _vjp_of(lambda hh, gg: (_rms(hh, gg),), 2, (0, 1))(h, gain_a, dna)
        dhb, dgb = _vjp_of(lambda hh, gg: (_rms(hh, gg),), 2, (0, 1))(h, gain_b, dnb)
        return dnext + dha + dhb, dnext + dha + dhb, dga, dgb

    dh2, dh2b, g["kv_norm"], g["b_norm"] = _rowwise(res_norm2_bwd, [h2, dh3, dkvn, dqn], [w["kv_norm"], w["b_norm"]],
                                                    [row(), rowb()], [vec(), vec()], name="res_norm_kvq_bwd")

    dhn0, g["mlp_w_up0"], g["mlp_w_down0"] = _mlp_bwd(hn0, u0, dh2b, w["mlp_w_up0"], w["mlp_w_down0"], 0, up_dev_major, distributed)
    dh1, dh1b, g["mlp_norm0"] = _rowwise(res_norm_bwd, [h1, dh2, dhn0], [w["mlp_norm0"]], [row(), rowb()], [vec()], name="res_norm0_bwd")

    g["a_w_out"] = square_dw(yg, dh1b, "tmix_out_dw")
    dyg = _mm(dh1b, w["a_w_out"], tb=True, name="tmix_out_dx")
    dy2, dgate = _rowwise(lambda d, a, b: (d * b, d * a), [dyg, y2, gate], [], [row(), rowb()], name="tmix_gate_bwd")
    g["a_g2"] = _mm(lg1, dgate, ta=True, a_pro=_sigmoid, name="tmix_g2_dw")

    def dsigmoid(rr, z):
        s = jax.nn.sigmoid(z)
        return rr * s * (1.0 - s)

    dlg1 = _mm(dgate, w["a_g2"], tb=True, epi=dsigmoid, epi_args=(lg1,), out_dtype=BF16, name="tmix_g2_dx")
    g["a_g1"] = _mm(xg, dlg1, ta=True, name="tmix_g1_dw")
    dxg = _mm(dlg1, w["a_g1"], tb=True, name="tmix_g1_dx")

    early_sums = [g[name] for name in _EARLY_GRADS] if distributed else ()
    (dr, dk, dv, dwl, dal), param_grads, early_reduced = _wkv_bwd(r, k, v, wl, al, wkv_params, states, dy2, early_sums)
    for name, pg in zip(("a_w0", "a_a0", "a_k_k", "a_k_a", "a_ln_x_w", "a_ln_x_b", "a_r_k"), param_grads):
        g[name] = pg.reshape(1, D_MODEL)

    g["a_w_r"] = square_dw(xr, dr, "tmix_r_dw")
    g["a_w_k"] = square_dw(xk, dk, "tmix_k_dw")
    g["a_w_v"] = square_dw(xv, dv, "tmix_v_dw")
    dxr = _mm(dr, w["a_w_r"], tb=True, name="tmix_r_dx")
    dxk = _mm(dk, w["a_w_k"], tb=True, name="tmix_k_dx")
    dxv = _mm(dv, w["a_w_v"], tb=True, name="tmix_v_dx")
    g["a_w2"] = _mm(lw1, dwl, ta=True, a_pro=jnp.tanh, name="tmix_w2_dw")

    def dtanh(rr, z):
        th = jnp.tanh(z)
        return rr * (1.0 - th * th)

    dlw1 = _mm(dwl, w["a_w2"], tb=True, epi=dtanh, epi_args=(lw1,), out_dtype=BF16, name="tmix_w2_dx")
    g["a_w1"] = _mm(xw, dlw1, ta=True, name="tmix_w1_dw")
    dxw = _mm(dlw1, w["a_w1"], tb=True, name="tmix_w1_dx")
    g["a_a2"] = _mm(la1, dal, ta=True, name="tmix_a2_dw")
    dla1 = _mm(dal, w["a_a2"], tb=True, out_dtype=BF16, name="tmix_a2_dx")
    g["a_a1"] = _mm(xa, dla1, ta=True, name="tmix_a1_dw")
    dxa = _mm(dla1, w["a_a1"], tb=True, name="tmix_a1_dx")

    lerp_bwd = _vjp_of(_time_shift_lerps, 9, tuple(range(9)))

    def lerp_bwd_rows(x_, xs_, d0, d1, d2, d3, d4, d5, gain, *mx):
        return lerp_bwd(x_, xs_, gain, *mx, d0, d1, d2, d3, d4, d5)

    rkv_sums = [g[name] for name in _RKV_GRADS] if distributed else []
    outs = _rowwise(lerp_bwd_rows, [x, xs, dxr, dxk, dxv, dxw, dxa, dxg], [w["a_norm"]] + mix, [row(), row()], [vec()] * 7,
                    tm=128, name="tmix_lerp_bwd", chip_sums=rkv_sums)
    dx_a, dxs, g["a_norm"] = outs[0], outs[1], outs[2]
    g["a_mix"] = jnp.concatenate(outs[3:9], axis=0)
    (grad_x,) = _rowwise(lambda a, b, c: (a + b + c,), [dh1, dx_a, _shift_up(dxs)], [], [row()], name="grad_x_sum")
    reduced = dict(zip(_EARLY_GRADS + _RKV_GRADS, list(early_reduced) + list(outs[9:]))) if distributed else None
    return loss, grad_x, g, reduced


_ANY = pl.BlockSpec(memory_space=pl.ANY)
_MESH_ID = pl.DeviceIdType.MESH


def _linear(pos):
    return 4 * pos[0] + 2 * pos[1] + pos[2]


def _all_gather(shards, name):
    n = len(shards)

    def body(*refs):
        start, relay, finish = _gather_plan(refs[:n], refs[n:2 * n], *refs[2 * n:])
        start()
        relay()
        finish()

    return pl.pallas_call(
        body, out_shape=[_sds((N_DEV,) + s.shape, s.dtype) for s in shards], in_specs=[_ANY] * n, out_specs=[_ANY] * n,
        scratch_shapes=_gather_semaphores(n), name=name,
    )(*shards)


GATHER_COPIES = 8


def _gather_semaphores(n):
    return [pltpu.SemaphoreType.DMA((n * GATHER_COPIES,)), pltpu.SemaphoreType.DMA((n * GATHER_COPIES,)),
            pltpu.SemaphoreType.DMA((n,))]


def _gather_plan(ins, outs, send_sems, recv_sems, local_sems):
    n = len(ins)
    x, y, c = lax.axis_index("x"), lax.axis_index("y"), lax.axis_index("c")
    me, sibling = (x, y, c), (x, y, 1 - c)
    x_nbr, y_nbr, diag = (1 - x, y, c), (x, 1 - y, c), (1 - x, 1 - y, c)
    other = lambda pos: (pos[0], pos[1], 1 - c)

    def halves(a):
        rows = ins[a].shape[0]
        if rows % (2 * BF16_SUBLANES):
            return (0, rows), None
        return (0, rows // 2), (rows // 2, rows // 2)

    def copy(a, k, block, to, src=None, rows=None):
        dst = outs[a].at[_linear(block)]
        if rows is not None:
            dst = dst.at[pl.ds(rows[0], rows[1])]
        return pltpu.make_async_remote_copy(
            src_ref=dst if src is None else src, dst_ref=dst, send_sem=send_sems.at[a * GATHER_COPIES + k],
            recv_sem=recv_sems.at[a * GATHER_COPIES + k], device_id=to, device_id_type=_MESH_ID)

    def own_copies():
        mine = [pltpu.make_async_copy(ins[a], outs[a].at[_linear(me)], local_sems.at[a]) for a in range(n)]
        sent = []
        for a in range(n):
            sent += [copy(a, 0, me, sibling, src=ins[a]), copy(a, 1, me, x_nbr, src=ins[a]), copy(a, 2, me, y_nbr, src=ins[a])]
        return mine, sent

    def relayed_copies():
        sent = []
        for a in range(n):
            first, second = halves(a)
            sent += [copy(a, 3, x_nbr, y_nbr, rows=first), copy(a, 5, x_nbr, sibling), copy(a, 6, y_nbr, sibling)]
            if second is not None:
                sent.append(copy(a, 4, y_nbr, x_nbr, rows=second))
        return sent

    def start():
        mine, sent = own_copies()
        for cp in mine + sent:
            cp.start()

    def relay():
        for a in range(n):
            first, second = halves(a)
            copy(a, 1, x_nbr, me).wait_recv()
            copy(a, 3, x_nbr, y_nbr, rows=first).start()
            copy(a, 5, x_nbr, sibling).start()
        for a in range(n):
            first, second = halves(a)
            copy(a, 2, y_nbr, me).wait_recv()
            if second is not None:
                copy(a, 4, y_nbr, x_nbr, rows=second).start()
            copy(a, 6, y_nbr, sibling).start()

    def finish():
        mine, sent = own_copies()
        sent += relayed_copies()
        for a in range(n):
            first, second = halves(a)
            copy(a, 3, diag, me, rows=first).wait_recv()
            if second is not None:
                copy(a, 4, diag, me, rows=second).wait_recv()
            last = copy(a, 7, diag, sibling)
            last.start()
            sent.append(last)
        for a in range(n):
            copy(a, 0, other(me), me).wait_recv()
            copy(a, 5, other(x_nbr), me).wait_recv()
            copy(a, 6, other(y_nbr), me).wait_recv()
            copy(a, 7, other(diag), me).wait_recv()
        for cp in sent:
            cp.wait_send()
        for cp in mine:
            cp.wait()

    return start, relay, finish


N_CHIPS = 4


def _exchange_with_sibling(parts, name):
    n = len(parts)

    def body(*refs):
        ins, outs = refs[:n], refs[n:2 * n]
        send_sems, recv_sems = refs[2 * n:]
        x, y, c = lax.axis_index("x"), lax.axis_index("y"), lax.axis_index("c")
        copies = [pltpu.make_async_remote_copy(
            src_ref=ins[a].at[2 * q + (1 - c)], dst_ref=outs[a].at[q], send_sem=send_sems.at[a * N_CHIPS + q],
            recv_sem=recv_sems.at[a * N_CHIPS + q], device_id=(x, y, 1 - c), device_id_type=_MESH_ID)
            for a in range(n) for q in range(N_CHIPS)]
        for cp in copies:
            cp.start()
        for cp in copies:
            cp.wait()

    return pl.pallas_call(
        body, out_shape=[_sds((N_CHIPS,) + p.shape[1:], p.dtype) for p in parts], in_specs=[_ANY] * n, out_specs=[_ANY] * n,
        scratch_shapes=[pltpu.SemaphoreType.DMA((n * N_CHIPS,)), pltpu.SemaphoreType.DMA((n * N_CHIPS,))], name=name,
    )(*parts)


def _pair_sum(part, recv, core, out_dtype, name):
    _, r, cdim = recv.shape
    tr = max(8, min(r, STREAM_BLOCK_ELEMENTS // cdim))
    assert r % tr == 0, (name, r, tr)

    def body(core_ref, p_ref, r_ref, o_ref):
        o_ref[...] = (p_ref[...] + r_ref[...]).astype(o_ref.dtype)

    grid_spec = pltpu.PrefetchScalarGridSpec(
        num_scalar_prefetch=1, grid=(N_CHIPS, r // tr),
        in_specs=[pl.BlockSpec((None, None, tr, cdim), lambda q, i, core_ref: (q, core_ref[0], i, 0)),
                  pl.BlockSpec((None, tr, cdim), lambda q, i, core_ref: (q, i, 0))],
        out_specs=pl.BlockSpec((None, tr, cdim), lambda q, i, core_ref: (q, i, 0)))
    return pl.pallas_call(
        body, grid_spec=grid_spec, out_shape=_sds((N_CHIPS, r, cdim), out_dtype),
        compiler_params=_params(("parallel", "parallel")), name=name,
    )(core, part.reshape(N_CHIPS, 2, r, cdim), recv)


def _exchange_between_chips(parts, name):
    n = len(parts)

    def body(*refs):
        start, finish = _chip_exchange_plan(refs[:n], refs[n:2 * n], *refs[2 * n:])
        start()
        finish()

    return pl.pallas_call(
        body, out_shape=[_sds(p.shape, p.dtype) for p in parts], in_specs=[_ANY] * n, out_specs=[_ANY] * n,
        scratch_shapes=_chip_exchange_semaphores(n), name=name,
    )(*parts)


def _chip_exchange_semaphores(n):
    n_other = N_CHIPS - 1
    return [pltpu.SemaphoreType.DMA((n * n_other,)), pltpu.SemaphoreType.DMA((n * n_other,)), pltpu.SemaphoreType.DMA((n,))]


def _chip_exchange_plan(ins, outs, send_sems, recv_sems, local_sems):
    n = len(ins)
    n_other = N_CHIPS - 1
    x, y, c = lax.axis_index("x"), lax.axis_index("y"), lax.axis_index("c")
    my_chip = 2 * x + y

    def all_copies():
        mine = [pltpu.make_async_copy(ins[a].at[my_chip], outs[a].at[my_chip], local_sems.at[a]) for a in range(n)]
        remote = []
        for j, (fx, fy) in enumerate([(1, 0), (0, 1), (1, 1)]):
            px, py = (1 - x if fx else x), (1 - y if fy else y)
            for a in range(n):
                remote.append(pltpu.make_async_remote_copy(
                    src_ref=ins[a].at[2 * px + py], dst_ref=outs[a].at[my_chip], send_sem=send_sems.at[a * n_other + j],
                    recv_sem=recv_sems.at[a * n_other + j], device_id=(px, py, c), device_id_type=_MESH_ID))
        return mine, remote

    def start():
        mine, remote = all_copies()
        for cp in mine + remote:
            cp.start()

    def finish():
        mine, remote = all_copies()
        for cp in remote + mine:
            cp.wait()

    return start, finish


def _chip_sums(parts, names, tag):
    from_sibling = _exchange_with_sibling(parts, name="scatter_grads_sibling_" + tag)
    core = lax.axis_index("c").astype(jnp.int32).reshape(1)
    return [_pair_sum(p, r, core, F32 if nm.startswith("pack") else BF16, name="pair_sum_" + nm)
            for p, r, nm in zip(parts, from_sibling, names)]


def _adamw(w, m, v, slots, name, layer=0, n_layers=1, into=None):
    r, c = w.shape[-2:]
    ns = slots.shape[0]
    tr = max(8, min(r, STREAM_BLOCK_ELEMENTS // c))
    assert r % tr == 0, (name, r, tr)

    def body(w_ref, m_ref, v_ref, g_ref, *rest):
        g_out, d_out, m_out, v_out = rest[-4:]
        g = g_ref[0].astype(F32)
        for s in range(1, ns):
            g = g + g_ref[s].astype(F32)
        m_new = ADAM_B1 * m_ref[...] + (1.0 - ADAM_B1) * g
        v_new = ADAM_B2 * v_ref[...] + (1.0 - ADAM_B2) * jnp.square(g)
        m_hat = m_new / (1.0 - ADAM_B1 ** ADAM_STEP)
        v_hat = v_new / (1.0 - ADAM_B2 ** ADAM_STEP)
        d_out[...] = -ADAM_LR * (m_hat / (jnp.sqrt(v_hat) + ADAM_EPS) + ADAM_WD * w_ref[...])
        g_out[...], m_out[...], v_out[...] = g, m_new, v_new

    if n_layers == 1:
        spec, out_shape, earlier, aliases = pl.BlockSpec((tr, c), lambda i: (i, 0)), _sds((r, c)), [], {}
    else:
        spec, out_shape = pl.BlockSpec((None, tr, c), lambda i: (layer, i, 0)), _sds((n_layers, r, c))
        earlier = list(into) if into is not None else []
        aliases = {4 + i: i for i in range(len(earlier))}
    in_specs = [spec, spec, spec, pl.BlockSpec((ns, tr, c), lambda i: (0, i, 0))]
    return pl.pallas_call(
        body, grid=(r // tr,), in_specs=in_specs + [_ANY] * len(earlier), out_specs=[spec] * 4, out_shape=[out_shape] * 4,
        input_output_aliases=aliases, compiler_params=_params(("parallel",)), name=name,
    )(w, m, v, slots, *earlier)


_COL_VECTORS = ("a_norm", "a_mix", "a_w0", "a_a0", "a_k_k", "a_k_a", "a_ln_x_w", "a_ln_x_b")
_COL_VEC_ROWS = 16
_COL_ROWS = _COL_VEC_ROWS + 2 * LORA_PAD + 256
_ROW_COLS = 2 * LORA_PAD + 256 + 512
_REPL_ROWS = 8


def _pad_to(a, size, axis):
    widths = [(0, 0)] * a.ndim
    widths[axis] = (0, size - a.shape[axis])
    return jnp.pad(a, widths)


def _pack_cols(p):
    width = p["a_norm"].shape[-1]
    vecs = jnp.concatenate([p[n].reshape(-1, width) for n in _COL_VECTORS], axis=0)
    return jnp.concatenate([_pad_to(vecs, _COL_VEC_ROWS, 0), _pad_to(p["a_w2"].reshape(-1, width), LORA_PAD, 0),
                            _pad_to(p["a_a2"].reshape(-1, width), LORA_PAD, 0), p["a_g2"].reshape(-1, width)], axis=0)


def _unpack_cols(a, lead):
    width = a.shape[-1]
    out, row = {}, 0
    for n in _COL_VECTORS:
        k = 6 if n == "a_mix" else 1
        out[n] = a[row:row + k].reshape(lead + ((6, width) if n == "a_mix" else (width,)))
        row += k
    base = _COL_VEC_ROWS
    out["a_w2"] = a[base:base + 96].reshape(lead + (96, width))
    out["a_a2"] = a[base + LORA_PAD:base + LORA_PAD + 96].reshape(lead + (96, width))
    out["a_g2"] = a[base + 2 * LORA_PAD:].reshape(lead + (256, width))
    return out


def _pack_rows(p):
    rows = p["w_kv"].shape[0]
    return jnp.concatenate([_pad_to(p["a_w1"].reshape(rows, -1), LORA_PAD, 1), _pad_to(p["a_a1"].reshape(rows, -1), LORA_PAD, 1),
                            p["a_g1"].reshape(rows, -1), p["w_kv"]], axis=1)


def _unpack_rows(a, lead):
    rows = a.shape[0]
    return {"a_w1": a[:, :96].reshape(lead + (rows, 96)), "a_a1": a[:, LORA_PAD:LORA_PAD + 96].reshape(lead + (rows, 96)),
            "a_g1": a[:, 2 * LORA_PAD:2 * LORA_PAD + 256].reshape(lead + (rows, 256)), "w_kv": a[:, 2 * LORA_PAD + 256:]}


def _pack_repl(p):
    row = lambda a: _pad_to(a.reshape(1, -1), D_MODEL, 1)
    return jnp.concatenate([p["mlp_norm"].reshape(2, D_MODEL), row(p["kv_norm"]), row(p["b_norm"]), row(p["a_r_k"]),
                            row(p["k_norm"]), row(p["b_q_norm"]), row(p["b_sinks"])], axis=0)


def _unpack_repl(a):
    return {"mlp_norm": a[0:2], "kv_norm": a[2], "b_norm": a[3:4], "a_r_k": a[4].reshape(1, N_HEADS, HEAD_DIM),
            "k_norm": a[5, :HEAD_DIM], "b_q_norm": a[6:7, :HEAD_DIM], "b_sinks": a[7:8, :N_HEADS]}


_WEIGHTS = ("a_norm", "a_mix", "a_w_rkv", "a_w0", "a_w1", "a_w2", "a_a0", "a_a1", "a_a2", "a_g1", "a_g2", "a_k_k", "a_k_a",
            "a_r_k", "a_ln_x_w", "a_ln_x_b", "a_w_out", "mlp_norm", "mlp_w_up", "mlp_w_down", "kv_norm", "w_kv", "k_norm",
            "b_norm", "b_w_q", "b_q_norm", "b_sinks", "b_w_o")


def _big_shards(p):
    return [p["a_w_rkv"][0, 0], p["a_w_rkv"][0, 1], p["a_w_rkv"][0, 2], p["a_w_out"][0], p["mlp_w_up"][0], p["mlp_w_up"][1],
            p["mlp_w_down"][0], p["mlp_w_down"][1], p["b_w_q"][0], p["b_w_o"][0]]


_BIG_NAMES = ("a_w_r", "a_w_k", "a_w_v", "a_w_out", "mlp_w_up0", "mlp_w_up1", "mlp_w_down0", "mlp_w_down1", "b_w_q", "b_w_o")


def kernel(x, a_norm, a_mix, a_w_rkv, a_w0, a_w1, a_w2, a_a0, a_a1, a_a2, a_g1, a_g2, a_k_k, a_k_a, a_r_k, a_ln_x_w,
           a_ln_x_b, a_w_out, mlp_norm, mlp_w_up, mlp_w_down, kv_norm, w_kv, k_norm, b_norm, b_w_q, b_q_norm, b_sinks,
           b_w_o, loss_target, m_a_norm, m_a_mix, m_a_w_rkv, m_a_w0, m_a_w1, m_a_w2, m_a_a0, m_a_a1, m_a_a2, m_a_g1,
           m_a_g2, m_a_k_k, m_a_k_a, m_a_r_k, m_a_ln_x_w, m_a_ln_x_b, m_a_w_out, m_mlp_norm, m_mlp_w_up, m_mlp_w_down,
           m_kv_norm, m_w_kv, m_k_norm, m_b_norm, m_b_w_q, m_b_q_norm, m_b_sinks, m_b_w_o, v_a_norm, v_a_mix, v_a_w_rkv,
           v_a_w0, v_a_w1, v_a_w2, v_a_a0, v_a_a1, v_a_a2, v_a_g1, v_a_g2, v_a_k_k, v_a_k_a, v_a_r_k, v_a_ln_x_w,
           v_a_ln_x_b, v_a_w_out, v_mlp_norm, v_mlp_w_up, v_mlp_w_down, v_kv_norm, v_w_kv, v_k_norm, v_b_norm, v_b_w_q,
           v_b_q_norm, v_b_sinks, v_b_w_o):
    given = locals()
    wts = {n: given[n] for n in _WEIGHTS}
    mom = {n: given["m_" + n] for n in _WEIGHTS}
    var = {n: given["v_" + n] for n in _WEIGHTS}

    cols_w, rows_w, repl_w = _pack_cols(wts), _pack_rows(wts), _pack_repl(wts)
    big_w = _big_shards(wts)
    big_bf16 = dict(zip(_BIG_NAMES, [b.astype(BF16) for b in big_w]))
    first_names = [k for k in _BIG_NAMES if k not in _LATE_GATHER]
    lora_in = rows_w[:, :2 * LORA_PAD].astype(BF16)
    gathered = _all_gather([cols_w, lora_in] + [big_bf16[k] for k in first_names], name="gather_weights")
    full_cols = gathered[0].transpose(1, 0, 2).reshape(_COL_ROWS, D_MODEL)
    full_lora_in = gathered[1].reshape(D_MODEL, 2 * LORA_PAD)
    w = {}
    w.update({k: v.reshape(v.shape[1:]) for k, v in _unpack_cols(full_cols, (1,)).items()})
    for k in ("a_norm", "a_w0", "a_a0", "a_k_k", "a_k_a", "a_ln_x_w", "a_ln_x_b"):
        w[k] = w[k].reshape(1, D_MODEL)
    for k in ("a_w2", "a_a2"):
        w[k] = _pad_to(w[k], LORA_PAD, 0)
    w["a_w1"], w["a_a1"] = full_lora_in[:, :LORA_PAD], full_lora_in[:, LORA_PAD:]
    for k, arr in zip(first_names, gathered[2:]):
        w[k] = arr.reshape(N_DEV * arr.shape[1], arr.shape[2])
    late_shards = [big_bf16[k] for k in _LATE_WEIGHTS] + [big_bf16["a_w_out"], a_g1[0].astype(BF16), w_kv.astype(BF16)]
    w["mlp_norm0"], w["mlp_norm1"] = mlp_norm[0:1], mlp_norm[1:2]
    w["kv_norm"], w["k_norm"] = kv_norm.reshape(1, D_MODEL), k_norm.reshape(1, HEAD_DIM)
    w["b_norm"], w["b_q_norm"], w["b_sinks"], w["a_r_k"] = b_norm, b_q_norm, b_sinks, a_r_k.reshape(1, D_MODEL)

    loss_local, grad_x, g, big_reduced = _local_step(x[0], loss_target[0], w, late_shards)
    loss = lax.psum(loss_local[0, 0], MESH_AXES)

    g_lead = {k: g[k][None] for k in ("a_norm", "a_mix", "a_w0", "a_a0", "a_k_k", "a_k_a", "a_ln_x_w", "a_ln_x_b", "a_g2")}
    g_lead["a_w2"], g_lead["a_a2"] = g["a_w2"][None, :96], g["a_a2"][None, :96]
    g_cols = _pack_cols(g_lead).reshape(_COL_ROWS, N_DEV, D_MODEL // N_DEV).transpose(1, 0, 2)
    g_rows = _pack_rows({"a_w1": g["a_w1"][:, :96], "a_a1": g["a_a1"][:, :96], "a_g1": g["a_g1"], "w_kv": g["w_kv"]})
    g_rows = g_rows.reshape(N_DEV, D_MODEL // N_DEV, _ROW_COLS)
    pack_sums = _chip_sums([g_cols, g_rows], ("pack_cols", "pack_rows"), "late")
    reduced = list(_exchange_between_chips(pack_sums, name="scatter_grads_chips_late")) + [big_reduced[k] for k in _BIG_NAMES]
    g_repl = _pack_repl({"mlp_norm": jnp.concatenate([g["mlp_norm0"], g["mlp_norm1"]], axis=0), "kv_norm": g["kv_norm"],
                         "b_norm": g["b_norm"], "a_r_k": g["a_r_k"], "k_norm": g["k_norm"], "b_q_norm": g["b_q_norm"],
                         "b_sinks": g["b_sinks"]})
    (repl_slots,) = _all_gather([g_repl], name="gather_replicated_grads")

    res = {}
    cols4 = _adamw(cols_w, _pack_cols(mom), _pack_cols(var), reduced[0], name="adamw_cols")
    rows4 = _adamw(rows_w, _pack_rows(mom), _pack_rows(var), reduced[1], name="adamw_rows")
    repl4 = _adamw(repl_w, _pack_repl(mom), _pack_repl(var), repl_slots, name="adamw_replicated")
    for unpacked in ([_unpack_cols(a, (1,)) for a in cols4], [_unpack_rows(a, (1,)) for a in rows4], [_unpack_repl(a) for a in repl4]):
        for k in unpacked[0]:
            res[k] = tuple(u[k] for u in unpacked)
    stacked = {"a_w_r": ("a_w_rkv", 0), "a_w_k": ("a_w_rkv", 1), "a_w_v": ("a_w_rkv", 2), "mlp_w_up0": ("mlp_w_up", 0),
               "mlp_w_up1": ("mlp_w_up", 1), "mlp_w_down0": ("mlp_w_down", 0), "mlp_w_down1": ("mlp_w_down", 1)}
    big4, earlier = {}, None
    for k, bw, bm, bv, slots in zip(_BIG_NAMES, big_w, _big_shards(mom), _big_shards(var), reduced[2:]):
        if k in stacked:
            src, layer = stacked[k]
            as_layers = lambda a: a.reshape((-1,) + a.shape[-2:])
            earlier = _adamw(as_layers(wts[src]), as_layers(mom[src]), as_layers(var[src]), slots, name="adamw_" + k,
                             layer=layer, n_layers=as_layers(wts[src]).shape[0], into=earlier if layer else None)
        else:
            earlier = _adamw(bw, bm, bv, slots, name="adamw_" + k)
        big4[k] = earlier
    res["a_w_rkv"] = tuple(a[None] for a in big4["a_w_v"])
    res["a_w_out"] = tuple(a[None] for a in big4["a_w_out"])
    res["mlp_w_up"], res["mlp_w_down"] = tuple(big4["mlp_w_up1"]), tuple(big4["mlp_w_down1"])
    res["b_w_q"] = tuple(a[None] for a in big4["b_w_q"])
    res["b_w_o"] = tuple(a[None] for a in big4["b_w_o"])
    res["w_kv"] = tuple(a.reshape(w_kv.shape) for a in res["w_kv"])

    outs = [loss, grad_x[None]]
    for i in range(4):
        outs += [res[n][i].reshape(given[n].shape) for n in _WEIGHTS]
    return tuple(outs)
```
